```python
import jax, jax.numpy as jnp
from jax import lax
import numpy as np

D_MODEL = 1024
BATCH = 16
SEQ = 2048
DEPTH = 1

P_DIM = 256
LRU_WIDTH = 1024
LRU_HEADS = 8
LRU_HEAD_DIM = LRU_WIDTH // LRU_HEADS
CONV_WIDTH = 4
LRU_C = 8.0
POOL_WIDTH = D_MODEL // 2
POOL_WINDOWS = (2, 4, 8, 16)
POOL_GROUPS = len(POOL_WINDOWS)
POOL_GROUP_DIM = POOL_WIDTH // POOL_GROUPS
MAX_WIN = max(POOL_WINDOWS)
IN_COLS = 2 * LRU_WIDTH + 2 * POOL_WIDTH + 2 * D_MODEL
EPS = 1e-6

kernel_name = "hybrid_rglru_pool_gated_merge"


def rmsnorm(x, g):
    x32 = x.astype(jnp.float32)
    ms = jnp.mean(x32 * x32, axis=-1, keepdims=True)
    return (x32 * lax.rsqrt(ms + EPS)).astype(x.dtype) * g


def causal_depthwise_conv(x, w, b):
    s = x.shape[1]
    xp = jnp.pad(x, ((0, 0), (CONV_WIDTH - 1, 0), (0, 0)))
    y = b
    for k in range(CONV_WIDTH):
        y = y + xp[:, k:k + s, :] * w[k]
    return y


def block_diag_linear(x, w, b):
    bsz, s, _ = x.shape
    h, dh, _ = w.shape
    xh = x.reshape(bsz, s, h, dh)
    y = jnp.einsum('bshd,hde->bshe', xh, w) + b
    return y.reshape(bsz, s, h * dh)


def rg_lru(x, w_a, b_a, w_x, b_x, lam):
    x32 = x.astype(jnp.float32)
    r = jax.nn.sigmoid(block_diag_linear(x, w_a, b_a).astype(jnp.float32))
    i = jax.nn.sigmoid(block_diag_linear(x, w_x, b_x).astype(jnp.float32))
    log_a = -LRU_C * r * jax.nn.softplus(-lam.astype(jnp.float32))
    a = jnp.exp(log_a)
    mult = jnp.sqrt(-jnp.expm1(2.0 * log_a))
    u = mult * (i * x32)

    def combine(c1, c2):
        a1, b1 = c1
        a2, b2 = c2
        return a2 * a1, a2 * b1 + b2

    _, h = lax.associative_scan(combine, (a, u), axis=1)
    return h.astype(x.dtype)


def multiscale_pool(x, w_pool, scale):
    bsz, s, _ = x.shape
    x32 = x.astype(jnp.float32)
    c = jnp.cumsum(x32, axis=1)
    c_pad = jnp.pad(c, ((0, 0), (MAX_WIN, 0), (0, 0)))
    pos = jnp.arange(s)
    outs = []
    for g, k in enumerate(POOL_WINDOWS):
        cg = c_pad[..., g * POOL_GROUP_DIM:(g + 1) * POOL_GROUP_DIM]
        win_sum = cg[:, MAX_WIN:, :] - cg[:, MAX_WIN - k:MAX_WIN - k + s, :]
        count = jnp.minimum(pos + 1, k).astype(jnp.float32)[None, :, None]
        outs.append(win_sum / count)
    pooled = jnp.concatenate(outs, axis=-1)
    diff = (pooled - x32).astype(x.dtype).reshape(bsz, s, POOL_GROUPS, POOL_GROUP_DIM)
    y = jnp.einsum('bsgd,gde->bsge', diff, w_pool).reshape(bsz, s, POOL_WIDTH)
    return y * scale


def _fwd_setup_inputs(seed: int = 0) -> dict:
    key = jax.random.key(seed)
    ks = jax.random.split(key, 24)
    f32 = jnp.float32

    def nrm(k, shape, fan_in):
        return jax.random.normal(k, shape, f32) * (fan_in ** -0.5)

    x = jax.random.normal(ks[0], (BATCH, SEQ, D_MODEL), f32)
    p = jax.random.normal(ks[1], (DEPTH, BATCH, SEQ, P_DIM), f32)
    norm_g = 1.0 + 0.05 * jax.random.normal(ks[2], (DEPTH, D_MODEL), f32)
    w_in = nrm(ks[3], (DEPTH, D_MODEL, IN_COLS), D_MODEL)
    conv_w = nrm(ks[4], (DEPTH, CONV_WIDTH, LRU_WIDTH), CONV_WIDTH)
    conv_b = 0.02 * jax.random.normal(ks[5], (DEPTH, LRU_WIDTH), f32)
    lru_w_a = nrm(ks[6], (DEPTH, LRU_HEADS, LRU_HEAD_DIM, LRU_HEAD_DIM), LRU_HEAD_DIM)
    lru_b_a = 0.02 * jax.random.normal(ks[7], (DEPTH, LRU_HEADS, LRU_HEAD_DIM), f32)
    lru_w_x = nrm(ks[8], (DEPTH, LRU_HEADS, LRU_HEAD_DIM, LRU_HEAD_DIM), LRU_HEAD_DIM)
    lru_b_x = 0.02 * jax.random.normal(ks[9], (DEPTH, LRU_HEADS, LRU_HEAD_DIM), f32)
    u = jax.random.uniform(ks[10], (DEPTH, LRU_WIDTH), f32, 0.9, 0.999)
    sa = u ** (1.0 / LRU_C)
    lru_lambda = jnp.log(sa) - jnp.log1p(-sa)
    pool_w = nrm(ks[11], (DEPTH, POOL_GROUPS, POOL_GROUP_DIM, POOL_GROUP_DIM), POOL_GROUP_DIM)
    pool_scale = 1.0 + 0.1 * jax.random.normal(ks[12], (DEPTH, POOL_WIDTH), f32)
    w_proj_lru = nrm(ks[13], (DEPTH, LRU_WIDTH, D_MODEL), LRU_WIDTH)
    w_proj_pool = nrm(ks[14], (DEPTH, POOL_WIDTH, D_MODEL), POOL_WIDTH)
    w_out = nrm(ks[15], (DEPTH, D_MODEL, D_MODEL), D_MODEL)
    ple_norm_g = 1.0 + 0.05 * jax.random.normal(ks[16], (DEPTH, D_MODEL), f32)
    w_ple_gate = nrm(ks[17], (DEPTH, D_MODEL, D_MODEL), D_MODEL)
    w_ple_proj = nrm(ks[18], (DEPTH, P_DIM, D_MODEL), P_DIM)
    final_g = 1.0 + 0.05 * jax.random.normal(ks[19], (D_MODEL,), f32)
    return {
        "x": x, "p": p, "norm_g": norm_g, "w_in": w_in,
        "conv_w": conv_w, "conv_b": conv_b,
        "lru_w_a": lru_w_a, "lru_b_a": lru_b_a, "lru_w_x": lru_w_x, "lru_b_x": lru_b_x,
        "lru_lambda": lru_lambda, "pool_w": pool_w, "pool_scale": pool_scale,
        "w_proj_lru": w_proj_lru, "w_proj_pool": w_proj_pool, "w_out": w_out,
        "ple_norm_g": ple_norm_g, "w_ple_gate": w_ple_gate, "w_ple_proj": w_ple_proj,
        "final_g": final_g,
    }


def _fwd_reference(x, p, norm_g, w_in, conv_w, conv_b, lru_w_a, lru_b_a, lru_w_x, lru_b_x,
              lru_lambda, pool_w, pool_scale, w_proj_lru, w_proj_pool, w_out,
              ple_norm_g, w_ple_gate, w_ple_proj, final_g):
    split_points = np.cumsum([LRU_WIDTH, LRU_WIDTH, POOL_WIDTH, POOL_WIDTH, D_MODEL]).tolist()
    for i in range(DEPTH):
        h = rmsnorm(x, norm_g[i])
        z = h @ w_in[i]
        xa, ga, xb, gb, ma, mb = jnp.split(z, split_points, axis=-1)
        xa = causal_depthwise_conv(xa, conv_w[i], conv_b[i])
        ya = rg_lru(xa, lru_w_a[i], lru_b_a[i], lru_w_x[i], lru_b_x[i], lru_lambda[i]) * jax.nn.silu(ga)
        yb = multiscale_pool(xb, pool_w[i], pool_scale[i]) * jax.nn.silu(gb)
        merged = jax.nn.sigmoid(ma) * (ya @ w_proj_lru[i]) + jax.nn.sigmoid(mb) * (yb @ w_proj_pool[i])
        x = x + merged @ w_out[i]
        gate = jax.nn.sigmoid(rmsnorm(x, ple_norm_g[i]) @ w_ple_gate[i])
        x = x + gate * (p[i] @ w_ple_proj[i])
    return rmsnorm(x, final_g)


import jax as _jax
import jax.numpy as _jnp

TWIN_FORMAT = 'train_step'
FWD_PARAMS = ['x', 'p', 'norm_g', 'w_in', 'conv_w', 'conv_b', 'lru_w_a', 'lru_b_a', 'lru_w_x', 'lru_b_x', 'lru_lambda', 'pool_w', 'pool_scale', 'w_proj_lru', 'w_proj_pool', 'w_out', 'ple_norm_g', 'w_ple_gate', 'w_ple_proj', 'final_g']
TWIN_WEIGHTS = ['norm_g', 'w_in', 'conv_w', 'conv_b', 'lru_w_a', 'lru_b_a', 'lru_w_x', 'lru_b_x', 'lru_lambda', 'pool_w', 'pool_scale', 'w_proj_lru', 'w_proj_pool', 'w_out', 'ple_norm_g', 'w_ple_gate', 'w_ple_proj', 'final_g']
TWIN_DIFF_INPUT = 'x'
TWIN_INPUTS = ['x', 'p', 'norm_g', 'w_in', 'conv_w', 'conv_b', 'lru_w_a', 'lru_b_a', 'lru_w_x', 'lru_b_x', 'lru_lambda', 'pool_w', 'pool_scale', 'w_proj_lru', 'w_proj_pool', 'w_out', 'ple_norm_g', 'w_ple_gate', 'w_ple_proj', 'final_g', 'loss_target', 'm_norm_g', 'm_w_in', 'm_conv_w', 'm_conv_b', 'm_lru_w_a', 'm_lru_b_a', 'm_lru_w_x', 'm_lru_b_x', 'm_lru_lambda', 'm_pool_w', 'm_pool_scale', 'm_w_proj_lru', 'm_w_proj_pool', 'm_w_out', 'm_ple_norm_g', 'm_w_ple_gate', 'm_w_ple_proj', 'm_final_g', 'v_norm_g', 'v_w_in', 'v_conv_w', 'v_conv_b', 'v_lru_w_a', 'v_lru_b_a', 'v_lru_w_x', 'v_lru_b_x', 'v_lru_lambda', 'v_pool_w', 'v_pool_scale', 'v_w_proj_lru', 'v_w_proj_pool', 'v_w_out', 'v_ple_norm_g', 'v_w_ple_gate', 'v_w_ple_proj', 'v_final_g']
TWIN_OUTPUTS = ['loss', 'grad_x', 'grad_norm_g', 'grad_w_in', 'grad_conv_w', 'grad_conv_b', 'grad_lru_w_a', 'grad_lru_b_a', 'grad_lru_w_x', 'grad_lru_b_x', 'grad_lru_lambda', 'grad_pool_w', 'grad_pool_scale', 'grad_w_proj_lru', 'grad_w_proj_pool', 'grad_w_out', 'grad_ple_norm_g', 'grad_w_ple_gate', 'grad_w_ple_proj', 'grad_final_g', 'delta_norm_g', 'delta_w_in', 'delta_conv_w', 'delta_conv_b', 'delta_lru_w_a', 'delta_lru_b_a', 'delta_lru_w_x', 'delta_lru_b_x', 'delta_lru_lambda', 'delta_pool_w', 'delta_pool_scale', 'delta_w_proj_lru', 'delta_w_proj_pool', 'delta_w_out', 'delta_ple_norm_g', 'delta_w_ple_gate', 'delta_w_ple_proj', 'delta_final_g', 'new_m_norm_g', 'new_m_w_in', 'new_m_conv_w', 'new_m_conv_b', 'new_m_lru_w_a', 'new_m_lru_b_a', 'new_m_lru_w_x', 'new_m_lru_b_x', 'new_m_lru_lambda', 'new_m_pool_w', 'new_m_pool_scale', 'new_m_w_proj_lru', 'new_m_w_proj_pool', 'new_m_w_out', 'new_m_ple_norm_g', 'new_m_w_ple_gate', 'new_m_w_ple_proj', 'new_m_final_g', 'new_v_norm_g', 'new_v_w_in', 'new_v_conv_w', 'new_v_conv_b', 'new_v_lru_w_a', 'new_v_lru_b_a', 'new_v_lru_w_x', 'new_v_lru_b_x', 'new_v_lru_lambda', 'new_v_pool_w', 'new_v_pool_scale', 'new_v_w_proj_lru', 'new_v_w_proj_pool', 'new_v_w_out', 'new_v_ple_norm_g', 'new_v_w_ple_gate', 'new_v_w_ple_proj', 'new_v_final_g']
TWIN_LEAF_KINDS = {'loss': 'loss', 'grad_x': 'grad_x', 'grad_norm_g': 'grad_w', 'grad_w_in': 'grad_w', 'grad_conv_w': 'grad_w', 'grad_conv_b': 'grad_w', 'grad_lru_w_a': 'grad_w', 'grad_lru_b_a': 'grad_w', 'grad_lru_w_x': 'grad_w', 'grad_lru_b_x': 'grad_w', 'grad_lru_lambda': 'grad_w', 'grad_pool_w': 'grad_w', 'grad_pool_scale': 'grad_w', 'grad_w_proj_lru': 'grad_w', 'grad_w_proj_pool': 'grad_w', 'grad_w_out': 'grad_w', 'grad_ple_norm_g': 'grad_w', 'grad_w_ple_gate': 'grad_w', 'grad_w_ple_proj': 'grad_w', 'grad_final_g': 'grad_w', 'delta_norm_g': 'delta_w', 'delta_w_in': 'delta_w', 'delta_conv_w': 'delta_w', 'delta_conv_b': 'delta_w', 'delta_lru_w_a': 'delta_w', 'delta_lru_b_a': 'delta_w', 'delta_lru_w_x': 'delta_w', 'delta_lru_b_x': 'delta_w', 'delta_lru_lambda': 'delta_w', 'delta_pool_w': 'delta_w', 'delta_pool_scale': 'delta_w', 'delta_w_proj_lru': 'delta_w', 'delta_w_proj_pool': 'delta_w', 'delta_w_out': 'delta_w', 'delta_ple_norm_g': 'delta_w', 'delta_w_ple_gate': 'delta_w', 'delta_w_ple_proj': 'delta_w', 'delta_final_g': 'delta_w', 'new_m_norm_g': 'new_m', 'new_m_w_in': 'new_m', 'new_m_conv_w': 'new_m', 'new_m_conv_b': 'new_m', 'new_m_lru_w_a': 'new_m', 'new_m_lru_b_a': 'new_m', 'new_m_lru_w_x': 'new_m', 'new_m_lru_b_x': 'new_m', 'new_m_lru_lambda': 'new_m', 'new_m_pool_w': 'new_m', 'new_m_pool_scale': 'new_m', 'new_m_w_proj_lru': 'new_m', 'new_m_w_proj_pool': 'new_m', 'new_m_w_out': 'new_m', 'new_m_ple_norm_g': 'new_m', 'new_m_w_ple_gate': 'new_m', 'new_m_w_ple_proj': 'new_m', 'new_m_final_g': 'new_m', 'new_v_norm_g': 'new_v', 'new_v_w_in': 'new_v', 'new_v_conv_w': 'new_v', 'new_v_conv_b': 'new_v', 'new_v_lru_w_a': 'new_v', 'new_v_lru_b_a': 'new_v', 'new_v_lru_w_x': 'new_v', 'new_v_lru_b_x': 'new_v', 'new_v_lru_lambda': 'new_v', 'new_v_pool_w': 'new_v', 'new_v_pool_scale': 'new_v', 'new_v_w_proj_lru': 'new_v', 'new_v_w_proj_pool': 'new_v', 'new_v_w_out': 'new_v', 'new_v_ple_norm_g': 'new_v', 'new_v_w_ple_gate': 'new_v', 'new_v_w_ple_proj': 'new_v', 'new_v_final_g': 'new_v'}


def _forward(args):
    return _fwd_reference(*[args[k] for k in FWD_PARAMS])


def _output_shape():
    out = _jax.eval_shape(lambda: _forward(_fwd_setup_inputs(0)))
    return out.shape, out.dtype

N_MICROBATCH = 1
ADAM_LR = 0.001
ADAM_B1 = 0.9
ADAM_B2 = 0.999
ADAM_EPS = 1e-08
ADAM_WD = 0.01
ADAM_STEP = 10
PER_EXAMPLE_BATCH_AXIS = {'x': 0, 'p': 1, 'loss_target': 0}
SHARED_INPUTS = []
_WEIGHT_DTYPES = {'norm_g': _jnp.float32, 'w_in': _jnp.float32, 'conv_w': _jnp.float32, 'conv_b': _jnp.float32, 'lru_w_a': _jnp.float32, 'lru_b_a': _jnp.float32, 'lru_w_x': _jnp.float32, 'lru_b_x': _jnp.float32, 'lru_lambda': _jnp.float32, 'pool_w': _jnp.float32, 'pool_scale': _jnp.float32, 'w_proj_lru': _jnp.float32, 'w_proj_pool': _jnp.float32, 'w_out': _jnp.float32, 'ple_norm_g': _jnp.float32, 'w_ple_gate': _jnp.float32, 'w_ple_proj': _jnp.float32, 'final_g': _jnp.float32}
MOMENT_SCALE = {'norm_g': 8.464210e-02, 'w_in': 3.648019e-02, 'conv_w': 3.634009e-02, 'conv_b': 4.204682e-01, 'lru_w_a': 1.220354e-02, 'lru_b_a': 8.965704e-03, 'lru_w_x': 2.225255e-02, 'lru_b_x': 1.402539e-02, 'lru_lambda': 1.829057e-02, 'pool_w': 6.203771e-02, 'pool_scale': 6.441726e-02, 'w_proj_lru': 3.528087e-02, 'w_proj_pool': 4.408698e-02, 'w_out': 5.616171e-02, 'ple_norm_g': 3.293819e-02, 'w_ple_gate': 3.188961e-02, 'w_ple_proj': 8.295368e-02, 'final_g': 3.199278e+01}


def _to_microbatches(a, axis):
    t = _jnp.moveaxis(a, axis, 0)
    t = t.reshape((N_MICROBATCH, t.shape[0] // N_MICROBATCH) + t.shape[1:])
    return _jnp.moveaxis(t, 1, axis + 1)


def setup_inputs(seed: int = 0) -> dict:
    inp = _fwd_setup_inputs(seed)
    key = _jax.random.fold_in(_jax.random.key(seed), 7919)
    shape, _ = _output_shape()
    out = dict(inp)
    out["loss_target"] = _jax.random.normal(_jax.random.fold_in(key, 0), shape, _jnp.float32)
    for i, name in enumerate(TWIN_WEIGHTS):
        w = inp[name].astype(_jnp.float32)
        if MOMENT_SCALE is None:
            s = _jnp.sqrt(_jnp.mean(_jnp.square(w)) + 1e-30)
        else:
            s = MOMENT_SCALE[name]
        km, kv = _jax.random.split(_jax.random.fold_in(key, i + 1))
        out[name] = w
        out["m_" + name] = s * _jax.random.normal(km, w.shape, _jnp.float32)
        out["v_" + name] = (s * s) * _jax.random.uniform(kv, w.shape, _jnp.float32, 0.5, 1.5)
    if N_MICROBATCH > 1:
        for name, axis in PER_EXAMPLE_BATCH_AXIS.items():
            out[name] = _to_microbatches(out[name], axis)
    return {'x': out['x'], 'p': out['p'], 'norm_g': out['norm_g'], 'w_in': out['w_in'], 'conv_w': out['conv_w'], 'conv_b': out['conv_b'], 'lru_w_a': out['lru_w_a'], 'lru_b_a': out['lru_b_a'], 'lru_w_x': out['lru_w_x'], 'lru_b_x': out['lru_b_x'], 'lru_lambda': out['lru_lambda'], 'pool_w': out['pool_w'], 'pool_scale': out['pool_scale'], 'w_proj_lru': out['w_proj_lru'], 'w_proj_pool': out['w_proj_pool'], 'w_out': out['w_out'], 'ple_norm_g': out['ple_norm_g'], 'w_ple_gate': out['w_ple_gate'], 'w_ple_proj': out['w_ple_proj'], 'final_g': out['final_g'], 'loss_target': out['loss_target'], 'm_norm_g': out['m_norm_g'], 'm_w_in': out['m_w_in'], 'm_conv_w': out['m_conv_w'], 'm_conv_b': out['m_conv_b'], 'm_lru_w_a': out['m_lru_w_a'], 'm_lru_b_a': out['m_lru_b_a'], 'm_lru_w_x': out['m_lru_w_x'], 'm_lru_b_x': out['m_lru_b_x'], 'm_lru_lambda': out['m_lru_lambda'], 'm_pool_w': out['m_pool_w'], 'm_pool_scale': out['m_pool_scale'], 'm_w_proj_lru': out['m_w_proj_lru'], 'm_w_proj_pool': out['m_w_proj_pool'], 'm_w_out': out['m_w_out'], 'm_ple_norm_g': out['m_ple_norm_g'], 'm_w_ple_gate': out['m_w_ple_gate'], 'm_w_ple_proj': out['m_w_ple_proj'], 'm_final_g': out['m_final_g'], 'v_norm_g': out['v_norm_g'], 'v_w_in': out['v_w_in'], 'v_conv_w': out['v_conv_w'], 'v_conv_b': out['v_conv_b'], 'v_lru_w_a': out['v_lru_w_a'], 'v_lru_b_a': out['v_lru_b_a'], 'v_lru_w_x': out['v_lru_w_x'], 'v_lru_b_x': out['v_lru_b_x'], 'v_lru_lambda': out['v_lru_lambda'], 'v_pool_w': out['v_pool_w'], 'v_pool_scale': out['v_pool_scale'], 'v_w_proj_lru': out['v_w_proj_lru'], 'v_w_proj_pool': out['v_w_proj_pool'], 'v_w_out': out['v_w_out'], 'v_ple_norm_g': out['v_ple_norm_g'], 'v_w_ple_gate': out['v_w_ple_gate'], 'v_w_ple_proj': out['v_w_ple_proj'], 'v_final_g': out['v_final_g']}


def _loss(weights, diff, rest, loss_target):
    with _jax.named_scope("forward"):
        args = {**rest, TWIN_DIFF_INPUT: diff, **{k: w.astype(_WEIGHT_DTYPES[k]) for k, w in weights.items()}}
        y = _forward(args)
    with _jax.named_scope("loss_head"):
        err = _jnp.square(y.astype(_jnp.float32) - loss_target)
        return 0.5 * _jnp.sum(_jnp.mean(err, axis=-1)) if err.ndim else 0.5 * err


def _adamw(w, g, m, v):
    m = ADAM_B1 * m + (1.0 - ADAM_B1) * g
    v = ADAM_B2 * v + (1.0 - ADAM_B2) * _jnp.square(g)
    m_hat = m / (1.0 - ADAM_B1 ** ADAM_STEP)
    v_hat = v / (1.0 - ADAM_B2 ** ADAM_STEP)
    delta = -ADAM_LR * (m_hat / (_jnp.sqrt(v_hat) + ADAM_EPS) + ADAM_WD * w)
    return delta, m, v


def reference(x, p, norm_g, w_in, conv_w, conv_b, lru_w_a, lru_b_a, lru_w_x, lru_b_x, lru_lambda, pool_w, pool_scale, w_proj_lru, w_proj_pool, w_out, ple_norm_g, w_ple_gate, w_ple_proj, final_g, loss_target, m_norm_g, m_w_in, m_conv_w, m_conv_b, m_lru_w_a, m_lru_b_a, m_lru_w_x, m_lru_b_x, m_lru_lambda, m_pool_w, m_pool_scale, m_w_proj_lru, m_w_proj_pool, m_w_out, m_ple_norm_g, m_w_ple_gate, m_w_ple_proj, m_final_g, v_norm_g, v_w_in, v_conv_w, v_conv_b, v_lru_w_a, v_lru_b_a, v_lru_w_x, v_lru_b_x, v_lru_lambda, v_pool_w, v_pool_scale, v_w_proj_lru, v_w_proj_pool, v_w_out, v_ple_norm_g, v_w_ple_gate, v_w_ple_proj, v_final_g):
    given = dict(x=x, p=p, norm_g=norm_g, w_in=w_in, conv_w=conv_w, conv_b=conv_b, lru_w_a=lru_w_a, lru_b_a=lru_b_a, lru_w_x=lru_w_x, lru_b_x=lru_b_x, lru_lambda=lru_lambda, pool_w=pool_w, pool_scale=pool_scale, w_proj_lru=w_proj_lru, w_proj_pool=w_proj_pool, w_out=w_out, ple_norm_g=ple_norm_g, w_ple_gate=w_ple_gate, w_ple_proj=w_ple_proj, final_g=final_g, loss_target=loss_target, m_norm_g=m_norm_g, m_w_in=m_w_in, m_conv_w=m_conv_w, m_conv_b=m_conv_b, m_lru_w_a=m_lru_w_a, m_lru_b_a=m_lru_b_a, m_lru_w_x=m_lru_w_x, m_lru_b_x=m_lru_b_x, m_lru_lambda=m_lru_lambda, m_pool_w=m_pool_w, m_pool_scale=m_pool_scale, m_w_proj_lru=m_w_proj_lru, m_w_proj_pool=m_w_proj_pool, m_w_out=m_w_out, m_ple_norm_g=m_ple_norm_g, m_w_ple_gate=m_w_ple_gate, m_w_ple_proj=m_w_ple_proj, m_final_g=m_final_g, v_norm_g=v_norm_g, v_w_in=v_w_in, v_conv_w=v_conv_w, v_conv_b=v_conv_b, v_lru_w_a=v_lru_w_a, v_lru_b_a=v_lru_b_a, v_lru_w_x=v_lru_w_x, v_lru_b_x=v_lru_b_x, v_lru_lambda=v_lru_lambda, v_pool_w=v_pool_w, v_pool_scale=v_pool_scale, v_w_proj_lru=v_w_proj_lru, v_w_proj_pool=v_w_proj_pool, v_w_out=v_w_out, v_ple_norm_g=v_ple_norm_g, v_w_ple_gate=v_w_ple_gate, v_w_ple_proj=v_w_ple_proj, v_final_g=v_final_g)
    weights = {n: given[n] for n in TWIN_WEIGHTS}
    shared = {n: given[n] for n in SHARED_INPUTS}
    per_example = {n: given[n] for n in ['x', 'p']}
    grad_fn = _jax.value_and_grad(_loss, argnums=(0, 1))

    def one_microbatch(ex, loss_target):
        ex = dict(ex)
        diff = ex.pop(TWIN_DIFF_INPUT)
        return grad_fn(weights, diff, {**shared, **ex}, loss_target)

    if N_MICROBATCH == 1:
        loss, (grad_w, grad_x) = one_microbatch(per_example, given["loss_target"])
    else:
        def body(carry, xs):
            loss_sum, grad_sum = carry
            l_k, (gw_k, gx_k) = one_microbatch(xs[0], xs[1])
            with _jax.named_scope("update"):
                return (loss_sum + l_k, _jax.tree.map(_jnp.add, grad_sum, gw_k)), gx_k

        init = (_jnp.zeros((), _jnp.float32), _jax.tree.map(_jnp.zeros_like, weights))
        (loss, grad_w), grad_x = _jax.lax.scan(body, init, (per_example, given["loss_target"]))
    with _jax.named_scope("update"):
        delta_w, new_m, new_v = {}, {}, {}
        for n in TWIN_WEIGHTS:
            delta_w[n], new_m[n], new_v[n] = _adamw(weights[n], grad_w[n], given["m_" + n], given["v_" + n])
    return (loss, grad_x, *[grad_w[n] for n in TWIN_WEIGHTS], *[delta_w[n] for n in TWIN_WEIGHTS],
            *[new_m[n] for n in TWIN_WEIGHTS], *[new_v[n] for n in TWIN_WEIGHTS])
```

```python
import functools

import jax
import jax.numpy as jnp
from jax import lax
from jax.experimental import pallas as pl
from jax.experimental.pallas import tpu as pltpu

F32 = jnp.float32
BF16 = jnp.bfloat16
_MM = jnp.bfloat16
_WIRE = jnp.bfloat16

EPS = 1e-6
LRU_C = 8.0
POOL_WINDOWS = (2, 4, 8, 16)
N_HEADS = 8
HEAD = 128
HALO = 16
SUB = 8

ADAM_LR = 0.001
ADAM_B1 = 0.9
ADAM_B2 = 0.999
ADAM_EPS = 1e-08
ADAM_WD = 0.01
ADAM_STEP = 10

N_DEV = 8
MESH = pl.DeviceIdType.MESH
VMEM_LIMIT = 60 * 1024 * 1024
TILE_M = 256

_SEQ_ROWS = {"dwa": (0, 1024), "dwx": (1024, 1024), "dpw": (2048, 512), "dba": (2560, 8), "dbx": (2568, 8),
             "dcb": (2576, 8), "dlam": (2584, 8), "dcw": (2592, 32), "dps": (2624, 8)}
SEQ_PACK_ROWS = 2632
MID_PACK_ROWS = 16
INP_PACK_ROWS = 8
SMALL_ROWS = 3072
SMALL_SEC = SMALL_ROWS // N_DEV
_OFF_MID = SEQ_PACK_ROWS
_OFF_INP = SEQ_PACK_ROWS + MID_PACK_ROWS


def _dot(a, b):
    return jnp.dot(a, b, preferred_element_type=F32)


def _dot_nt(a, b):
    return lax.dot_general(a, b, (((1,), (1,)), ((), ())), preferred_element_type=F32)


def _dot_tn(a, b):
    return lax.dot_general(a, b, (((0,), (0,)), ((), ())), preferred_element_type=F32)


def _sigmoid(v):
    return jax.nn.sigmoid(v)


def _neg_expm1(v):
    series = -v * (1.0 + v * (0.5 + v * (1.0 / 6.0 + v * (1.0 / 24.0 + v * (1.0 / 120.0)))))
    return jnp.where(v > -0.1, series, 1.0 - jnp.exp(v))


def _softplus_neg(lam):
    e = jnp.exp(-jnp.abs(lam))
    w = 1.0 + e
    l1p = jnp.where(w == 1.0, e, jnp.log(w) * (e / (w - 1.0)))
    return jnp.maximum(-lam, 0.0) + l1p


def _rsqrt_mean_sq(v):
    return lax.rsqrt(jnp.mean(v * v, axis=-1, keepdims=True) + EPS)


def _colsum(v):
    return jnp.sum(v, axis=0, keepdims=True)


def _const_spec(shape):
    nd = len(shape)
    return pl.BlockSpec(shape, lambda *_: (0,) * nd, pipeline_mode=pl.Buffered(1))


def _lru_gates(xc, h, wa_ref, ba_ref, wx_ref, bx_ref, neg_c_sp):
    xc16 = xc.astype(_MM)
    r = _sigmoid(_dot(xc16, wa_ref[h]) + ba_ref[pl.ds(h, 1), :])
    i = _sigmoid(_dot(xc16, wx_ref[h]) + bx_ref[pl.ds(h, 1), :])
    return r, i, neg_c_sp * r


def _conv_head(xa_ext, cw_ref, cb_ref, cols, tm):
    xc = cb_ref[:, cols]
    for k in range(4):
        xc = xc + cw_ref[pl.ds(k, 1), cols] * xa_ext[pl.ds(HALO - 3 + k, tm), cols]
    return xc


def _pool_diff(xb_ext, g, k, pos, tm):
    cols = slice(g * HEAD, (g + 1) * HEAD)
    cur = xb_ext[pl.ds(HALO, tm), cols]
    ws = cur
    for jj in range(1, k):
        ws = ws + xb_ext[pl.ds(HALO - jj, tm), cols]
    cnt = jnp.minimum(pos + 1, k).astype(F32)
    return ws / cnt - cur, cnt


def _fwd_seq(x, g0, win, cw, cb, wa, ba, wx, bx, lam, pw, ps, tm):
    nb, s, d = x.shape
    nc = win.shape[1]
    lw = cw.shape[1]
    pwid = ps.shape[1]
    nt = s // tm

    def body(x_ref, g0_ref, win_ref, cw_ref, cb_ref, wa_ref, ba_ref, wx_ref, bx_ref, lam_ref, pw_ref, ps_ref,
             z_ref, hl_ref, ya_ref, yb_ref, xa_ext, xb_ext, a_buf, carry):
        j = pl.program_id(1)

        @pl.when(j == 0)
        def _():
            xa_ext[pl.ds(0, HALO), :] = jnp.zeros((HALO, lw), F32)
            xb_ext[pl.ds(0, HALO), :] = jnp.zeros((HALO, pwid), F32)
            carry[...] = jnp.zeros_like(carry)

        xv = x_ref[...]
        h16 = ((xv * _rsqrt_mean_sq(xv)) * g0_ref[...]).astype(_MM)
        z_ref[...] = _dot(h16, win_ref[...])

        xa_ext[pl.ds(HALO, tm), :] = z_ref[:, 0:lw]
        for h in range(N_HEADS):
            cols = slice(h * HEAD, (h + 1) * HEAD)
            xc = _conv_head(xa_ext, cw_ref, cb_ref, cols, tm)
            neg_c_sp = -LRU_C * _softplus_neg(lam_ref[:, cols])
            _, i, log_a = _lru_gates(xc, h, wa_ref, ba_ref, wx_ref, bx_ref, neg_c_sp)
            a_buf[:, cols] = jnp.exp(log_a)
            hl_ref[:, cols] = jnp.sqrt(_neg_expm1(2.0 * log_a)) * (i * xc)

        rows = lax.broadcasted_iota(jnp.int32, (SUB, lw), 0)

        def step(c, car):
            i0 = pl.multiple_of(c * SUB, SUB)
            av = a_buf[pl.ds(i0, SUB), :]
            bv = hl_ref[pl.ds(i0, SUB), :]
            for sh in (1, 2, 4):
                m = rows >= sh
                a_sh = jnp.where(m, pltpu.roll(av, sh, 0), 1.0)
                b_sh = jnp.where(m, pltpu.roll(bv, sh, 0), 0.0)
                bv = av * b_sh + bv
                av = av * a_sh
            hv = av * car + bv
            hl_ref[pl.ds(i0, SUB), :] = hv
            return jnp.broadcast_to(hv[SUB - 1:SUB, :], (SUB, lw))

        carry[...] = lax.fori_loop(0, tm // SUB, step, carry[...])

        ga = z_ref[:, lw:2 * lw]
        ya_ref[...] = (hl_ref[...] * (ga * _sigmoid(ga))).astype(ya_ref.dtype)

        xb_ext[pl.ds(HALO, tm), :] = z_ref[:, 2 * lw:2 * lw + pwid]
        pos = j * tm + lax.broadcasted_iota(jnp.int32, (tm, HEAD), 0)
        for g, k in enumerate(POOL_WINDOWS):
            cols = slice(g * HEAD, (g + 1) * HEAD)
            diff, _ = _pool_diff(xb_ext, g, k, pos, tm)
            yp = _dot(diff.astype(_MM), pw_ref[g])
            gb = z_ref[:, 2 * lw + pwid + g * HEAD:2 * lw + pwid + (g + 1) * HEAD]
            yb_ref[:, cols] = ((yp * ps_ref[:, cols]) * (gb * _sigmoid(gb))).astype(yb_ref.dtype)

        xa_ext[pl.ds(0, HALO), :] = xa_ext[pl.ds(tm, HALO), :]
        xb_ext[pl.ds(0, HALO), :] = xb_ext[pl.ds(tm, HALO), :]

    tile = lambda w: pl.BlockSpec((None, tm, w), lambda b, j: (b, j, 0))
    return pl.pallas_call(
        body, name="fwd_seq", grid=(nb, nt),
        in_specs=[tile(d), _const_spec(g0.shape), _const_spec(win.shape), _const_spec(cw.shape), _const_spec(cb.shape),
                  _const_spec(wa.shape), _const_spec(ba.shape), _const_spec(wx.shape), _const_spec(bx.shape),
                  _const_spec(lam.shape), _const_spec(pw.shape), _const_spec(ps.shape)],
        out_specs=(tile(nc), tile(lw), tile(lw), tile(pwid)),
        out_shape=(jax.ShapeDtypeStruct((nb, s, nc), F32), jax.ShapeDtypeStruct((nb, s, lw), F32),
                   jax.ShapeDtypeStruct((nb, s, lw), _MM), jax.ShapeDtypeStruct((nb, s, pwid), _MM)),
        scratch_shapes=[pltpu.VMEM((tm + HALO, lw), F32), pltpu.VMEM((tm + HALO, pwid), F32),
                        pltpu.VMEM((tm, lw), F32), pltpu.VMEM((SUB, lw), F32)],
        compiler_params=pltpu.CompilerParams(dimension_semantics=("arbitrary", "arbitrary"), vmem_limit_bytes=VMEM_LIMIT),
    )(x, g0, win, cw, cb, wa, ba, wx, bx, lam, pw, ps)


def _store_chunks(ref, row0, vec):
    for h in range(vec.shape[1] // HEAD):
        ref[pl.ds(row0 + h, 1), :] += vec[:, h * HEAD:(h + 1) * HEAD]


def _mid(ya, yb, z, x, p, tgt, wpl, wpp, wout, wpg, wpe, g1, gf, tm):
    t, d = x.shape
    lw = ya.shape[1]
    pwid = yb.shape[1]
    pdim = p.shape[1]
    n = t // tm
    sec_pp = wpp.shape[1] // N_DEV
    sec_pe = wpe.shape[1] // N_DEV

    def body(ya_ref, yb_ref, ma_ref, mb_ref, x_ref, p_ref, tgt_ref, wpl_ref, wpp_ref, wout_ref, wpg_ref, wpe_ref,
             g1_ref, gf_ref,
             dya_ref, dyb_ref, dm_ref, dx1_ref, gpl_ref, gpp_ref, gout_ref, gpg_ref, gpe_ref, pack_ref, loss_ref,
             acc_pl, acc_pp, acc_out, acc_pg, acc_pe):
        i = pl.program_id(0)

        @pl.when(i == 0)
        def _():
            for acc in (acc_pl, acc_pp, acc_out, acc_pg, acc_pe):
                acc[...] = jnp.zeros_like(acc)
            pack_ref[...] = jnp.zeros_like(pack_ref)
            loss_ref[...] = jnp.zeros_like(loss_ref)

        ya16 = ya_ref[...]
        yb16 = yb_ref[...]
        br_a = _dot(ya16, wpl_ref[...])
        br_b = _dot(yb16, wpp_ref[...])
        sa = _sigmoid(ma_ref[...])
        sb = _sigmoid(mb_ref[...])
        mg16 = (sa * br_a + sb * br_b).astype(_MM)
        x1 = x_ref[...] + _dot(mg16, wout_ref[...])
        r1 = _rsqrt_mean_sq(x1)
        n1 = x1 * r1
        h116 = (n1 * g1_ref[...]).astype(_MM)
        gate = _sigmoid(_dot(h116, wpg_ref[...]))
        p16 = p_ref[...].astype(_MM)
        pe = _dot(p16, wpe_ref[...])
        x2 = x1 + gate * pe
        r2 = _rsqrt_mean_sq(x2)
        n2 = x2 * r2
        err = n2 * gf_ref[...] - tgt_ref[...]
        sq = jnp.sum(_colsum(err * err), axis=1, keepdims=True)
        loss_ref[...] += jnp.broadcast_to(sq * (0.5 / d), loss_ref.shape)

        dy = err * (1.0 / d)
        _store_chunks(pack_ref, SUB, _colsum(dy * n2))
        dn2 = dy * gf_ref[...]
        dx2 = r2 * (dn2 - n2 * jnp.mean(dn2 * n2, axis=-1, keepdims=True))
        dpe16 = (dx2 * gate).astype(_MM)
        dpg16 = ((dx2 * pe) * (gate * (1.0 - gate))).astype(_MM)
        acc_pe[...] += _dot_tn(p16, dpe16)
        acc_pg[...] += _dot_tn(h116, dpg16)
        dh1 = _dot_nt(dpg16, wpg_ref[...])
        _store_chunks(pack_ref, 0, _colsum(dh1 * n1))
        dn1 = dh1 * g1_ref[...]
        dx1 = dx2 + r1 * (dn1 - n1 * jnp.mean(dn1 * n1, axis=-1, keepdims=True))
        dx1_ref[...] = dx1
        dx116 = dx1.astype(_MM)
        acc_out[...] += _dot_tn(mg16, dx116)
        dmg = _dot_nt(dx116, wout_ref[...])
        da16 = (dmg * sa).astype(_MM)
        db16 = (dmg * sb).astype(_MM)
        dm_ref[:, 0:d] = ((dmg * br_a) * (sa * (1.0 - sa))).astype(dm_ref.dtype)
        dm_ref[:, d:2 * d] = ((dmg * br_b) * (sb * (1.0 - sb))).astype(dm_ref.dtype)
        acc_pl[...] += _dot_tn(ya16, da16)
        acc_pp[...] += _dot_tn(yb16, db16)
        dya_ref[...] = _dot_nt(da16, wpl_ref[...])
        dyb_ref[...] = _dot_nt(db16, wpp_ref[...])

        @pl.when(i == n - 1)
        def _():
            pltpu.sync_copy(acc_pl, gpl_ref)
            pltpu.sync_copy(acc_out, gout_ref)
            pltpu.sync_copy(acc_pg, gpg_ref)
            for k in range(N_DEV):
                pltpu.sync_copy(acc_pp.at[:, pl.ds(k * sec_pp, sec_pp)], gpp_ref.at[k])
                pltpu.sync_copy(acc_pe.at[:, pl.ds(k * sec_pe, sec_pe)], gpe_ref.at[k])

    tile = lambda w: pl.BlockSpec((tm, w), lambda i: (i, 0))
    zcol = lambda c: pl.BlockSpec((tm, d), lambda i: (i, c))
    any_spec = pl.BlockSpec(memory_space=pl.ANY)
    full = lambda shape: pl.BlockSpec(shape, lambda i: (0,) * len(shape))
    return pl.pallas_call(
        body, name="mid", grid=(n,),
        in_specs=[tile(lw), tile(pwid), zcol(3), zcol(4), tile(d), tile(pdim), tile(d),
                  _const_spec(wpl.shape), _const_spec(wpp.shape), _const_spec(wout.shape), _const_spec(wpg.shape),
                  _const_spec(wpe.shape), _const_spec(g1.shape), _const_spec(gf.shape)],
        out_specs=(tile(lw), tile(pwid), tile(2 * d), tile(d), any_spec, any_spec, any_spec, any_spec, any_spec,
                   full((MID_PACK_ROWS, HEAD)), full((1, HEAD))),
        out_shape=(jax.ShapeDtypeStruct((t, lw), F32), jax.ShapeDtypeStruct((t, pwid), F32),
                   jax.ShapeDtypeStruct((t, 2 * d), _MM), jax.ShapeDtypeStruct((t, d), F32),
                   jax.ShapeDtypeStruct(wpl.shape, F32), jax.ShapeDtypeStruct((N_DEV, wpp.shape[0], sec_pp), F32),
                   jax.ShapeDtypeStruct(wout.shape, F32), jax.ShapeDtypeStruct(wpg.shape, F32),
                   jax.ShapeDtypeStruct((N_DEV, wpe.shape[0], sec_pe), F32),
                   jax.ShapeDtypeStruct((MID_PACK_ROWS, HEAD), F32), jax.ShapeDtypeStruct((1, HEAD), F32)),
        scratch_shapes=[pltpu.VMEM(wpl.shape, F32), pltpu.VMEM(wpp.shape, F32), pltpu.VMEM(wout.shape, F32),
                        pltpu.VMEM(wpg.shape, F32), pltpu.VMEM(wpe.shape, F32)],
        compiler_params=pltpu.CompilerParams(dimension_semantics=("arbitrary",), vmem_limit_bytes=VMEM_LIMIT),
    )(ya, yb, z, z, x, p, tgt, wpl, wpp, wout, wpg, wpe, g1, gf)


def _bwd_seq(z, hl, dya, dyb, cw, cb, wa, ba, wx, bx, lam, pw, ps, tm):
    nb, s, nc = z.shape
    lw = cw.shape[1]
    pwid = ps.shape[1]
    nt = s // tm
    zw = 2 * lw + 2 * pwid
    R = _SEQ_ROWS

    def body(zq_ref, zh_ref, hl_ref, hlh_ref, dya_ref, dyb_ref, cw_ref, cb_ref, wa_ref, ba_ref, wx_ref, bx_ref,
             lam_ref, pw_ref, ps_ref,
             dz_ref, pack_ref,
             xa_ext, xb_ext, a_ext, an_buf, hl_ext, dh_buf, xc_buf, r_buf, i_buf, dxc_ext, q_ext, carry):
        b = pl.program_id(0)
        j = pl.program_id(1)
        jr = nt - 1 - j
        has_prev = jr > 0

        @pl.when((b == 0) & (j == 0))
        def _():
            pack_ref[...] = jnp.zeros_like(pack_ref)

        @pl.when(j == 0)
        def _():
            a_ext[pl.ds(tm, SUB), :] = jnp.zeros((SUB, lw), F32)
            dxc_ext[pl.ds(tm, HALO), :] = jnp.zeros((HALO, lw), F32)
            q_ext[pl.ds(tm, HALO), :] = jnp.zeros((HALO, pwid), F32)
            carry[...] = jnp.zeros_like(carry)

        xa_ext[pl.ds(0, HALO), :] = jnp.where(has_prev, zh_ref[:, 0:lw], 0.0)
        xa_ext[pl.ds(HALO, tm), :] = zq_ref[:, 0:lw]
        hl_ext[pl.ds(0, SUB), :] = jnp.where(has_prev, hlh_ref[...], 0.0)
        hl_ext[pl.ds(SUB, tm), :] = hl_ref[...]
        for h in range(N_HEADS):
            cols = slice(h * HEAD, (h + 1) * HEAD)
            xc = _conv_head(xa_ext, cw_ref, cb_ref, cols, tm)
            neg_c_sp = -LRU_C * _softplus_neg(lam_ref[:, cols])
            r, i, log_a = _lru_gates(xc, h, wa_ref, ba_ref, wx_ref, bx_ref, neg_c_sp)
            xc_buf[:, cols] = xc
            r_buf[:, cols] = r
            i_buf[:, cols] = i
            a_ext[pl.ds(0, tm), cols] = jnp.exp(log_a)
            ga = zq_ref[:, lw + h * HEAD:lw + (h + 1) * HEAD]
            sg = _sigmoid(ga)
            dyav = dya_ref[:, cols]
            dh_buf[:, cols] = dyav * (ga * sg)
            dga = (dyav * hl_ext[pl.ds(SUB, tm), cols]) * (sg * (1.0 + ga * (1.0 - sg)))
            dz_ref[:, lw + h * HEAD:lw + (h + 1) * HEAD] = dga.astype(dz_ref.dtype)
        an_buf[...] = a_ext[pl.ds(1, tm), :]

        rows = lax.broadcasted_iota(jnp.int32, (SUB, lw), 0)
        nch = tm // SUB

        def step(c, car):
            i0 = pl.multiple_of((nch - 1 - c) * SUB, SUB)
            cv = an_buf[pl.ds(i0, SUB), :]
            bv = dh_buf[pl.ds(i0, SUB), :]
            for sh in (1, 2, 4):
                m = rows < SUB - sh
                c_sh = jnp.where(m, pltpu.roll(cv, SUB - sh, 0), 1.0)
                b_sh = jnp.where(m, pltpu.roll(bv, SUB - sh, 0), 0.0)
                bv = cv * b_sh + bv
                cv = cv * c_sh
            hv = cv * car + bv
            dh_buf[pl.ds(i0, SUB), :] = hv
            return jnp.broadcast_to(hv[0:1, :], (SUB, lw))

        carry[...] = lax.fori_loop(0, nch, step, carry[...])

        for h in range(N_HEADS):
            cols = slice(h * HEAD, (h + 1) * HEAD)
            lam_h = lam_ref[:, cols]
            neg_c_sp = -LRU_C * _softplus_neg(lam_h)
            xc = xc_buf[:, cols]
            r = r_buf[:, cols]
            i = i_buf[:, cols]
            a = a_ext[pl.ds(0, tm), cols]
            mult = jnp.sqrt(_neg_expm1(2.0 * (neg_c_sp * r)))
            dh = dh_buf[:, cols]
            dhx = dh * xc
            di = dhx * mult
            dlog_a = (dh * hl_ext[pl.ds(SUB - 1, tm), cols]) * a - (dhx * i) * ((a * a) / mult)
            pack_ref[pl.ds(R["dlam"][0] + h, 1), :] += _colsum(dlog_a * r) * (LRU_C * _sigmoid(-lam_h))
            dpa = (dlog_a * neg_c_sp) * (r * (1.0 - r))
            dpx = di * (i * (1.0 - i))
            dpa16 = dpa.astype(_MM)
            dpx16 = dpx.astype(_MM)
            xc16 = xc.astype(_MM)
            dxc = dh * (mult * i) + _dot_nt(dpa16, wa_ref[h]) + _dot_nt(dpx16, wx_ref[h])
            pack_ref[pl.ds(R["dwa"][0] + h * HEAD, HEAD), :] += _dot_tn(xc16, dpa16)
            pack_ref[pl.ds(R["dwx"][0] + h * HEAD, HEAD), :] += _dot_tn(xc16, dpx16)
            pack_ref[pl.ds(R["dba"][0] + h, 1), :] += _colsum(dpa)
            pack_ref[pl.ds(R["dbx"][0] + h, 1), :] += _colsum(dpx)
            pack_ref[pl.ds(R["dcb"][0] + h, 1), :] += _colsum(dxc)
            for k in range(4):
                pack_ref[pl.ds(R["dcw"][0] + SUB * k + h, 1), :] += _colsum(dxc * xa_ext[pl.ds(HALO - 3 + k, tm), cols])
            dxc_ext[pl.ds(0, tm), cols] = dxc

        for h in range(N_HEADS):
            cols = slice(h * HEAD, (h + 1) * HEAD)
            dxa = cw_ref[pl.ds(0, 1), cols] * dxc_ext[pl.ds(3, tm), cols]
            for k in range(1, 4):
                dxa = dxa + cw_ref[pl.ds(k, 1), cols] * dxc_ext[pl.ds(3 - k, tm), cols]
            dz_ref[:, cols] = dxa.astype(dz_ref.dtype)

        xb_ext[pl.ds(0, HALO), :] = jnp.where(has_prev, zh_ref[:, 2 * lw:2 * lw + pwid], 0.0)
        xb_ext[pl.ds(HALO, tm), :] = zq_ref[:, 2 * lw:2 * lw + pwid]
        pos = jr * tm + lax.broadcasted_iota(jnp.int32, (tm, HEAD), 0)
        for g, k in enumerate(POOL_WINDOWS):
            cols = slice(g * HEAD, (g + 1) * HEAD)
            diff, cnt = _pool_diff(xb_ext, g, k, pos, tm)
            diff16 = diff.astype(_MM)
            yp = _dot(diff16, pw_ref[g])
            sc = ps_ref[:, cols]
            gb = zq_ref[:, 2 * lw + pwid + g * HEAD:2 * lw + pwid + (g + 1) * HEAD]
            sgb = _sigmoid(gb)
            dybv = dyb_ref[:, cols]
            dy_pool = dybv * (gb * sgb)
            dgb = (dybv * (yp * sc)) * (sgb * (1.0 + gb * (1.0 - sgb)))
            pack_ref[pl.ds(R["dps"][0] + g, 1), :] += _colsum(dy_pool * yp)
            dyp16 = (dy_pool * sc).astype(_MM)
            pack_ref[pl.ds(R["dpw"][0] + g * HEAD, HEAD), :] += _dot_tn(diff16, dyp16)
            ddiff = _dot_nt(dyp16, pw_ref[g])
            q_ext[pl.ds(0, tm), cols] = ddiff / cnt
            dxb = q_ext[pl.ds(0, tm), cols] - ddiff
            for jj in range(1, k):
                dxb = dxb + q_ext[pl.ds(jj, tm), cols]
            dz_ref[:, 2 * lw + g * HEAD:2 * lw + (g + 1) * HEAD] = dxb.astype(dz_ref.dtype)
            dz_ref[:, 2 * lw + pwid + g * HEAD:2 * lw + pwid + (g + 1) * HEAD] = dgb.astype(dz_ref.dtype)

        a_ext[pl.ds(tm, SUB), :] = a_ext[pl.ds(0, SUB), :]
        dxc_ext[pl.ds(tm, HALO), :] = dxc_ext[pl.ds(0, HALO), :]
        q_ext[pl.ds(tm, HALO), :] = q_ext[pl.ds(0, HALO), :]

    rev = lambda w: pl.BlockSpec((None, tm, w), lambda b, j: (b, nt - 1 - j, 0))
    prev_rows = lambda rows, w: pl.BlockSpec(
        (None, rows, w), lambda b, j: (b, jnp.maximum((nt - 1 - j) * (tm // rows) - 1, 0), 0))
    return pl.pallas_call(
        body, name="bwd_seq", grid=(nb, nt),
        in_specs=[rev(zw), prev_rows(HALO, zw), rev(lw), prev_rows(SUB, lw), rev(lw), rev(pwid),
                  _const_spec(cw.shape), _const_spec(cb.shape), _const_spec(wa.shape), _const_spec(ba.shape),
                  _const_spec(wx.shape), _const_spec(bx.shape), _const_spec(lam.shape), _const_spec(pw.shape),
                  _const_spec(ps.shape)],
        out_specs=(rev(zw), pl.BlockSpec((SEQ_PACK_ROWS, HEAD), lambda b, j: (0, 0))),
        out_shape=(jax.ShapeDtypeStruct((nb, s, zw), _MM), jax.ShapeDtypeStruct((SEQ_PACK_ROWS, HEAD), F32)),
        scratch_shapes=[pltpu.VMEM((tm + HALO, lw), F32), pltpu.VMEM((tm + HALO, pwid), F32),
                        pltpu.VMEM((tm + SUB, lw), F32), pltpu.VMEM((tm, lw), F32), pltpu.VMEM((tm + SUB, lw), F32),
                        pltpu.VMEM((tm, lw), F32), pltpu.VMEM((tm, lw), F32), pltpu.VMEM((tm, lw), F32),
                        pltpu.VMEM((tm, lw), F32), pltpu.VMEM((tm + HALO, lw), F32),
                        pltpu.VMEM((tm + HALO, pwid), F32), pltpu.VMEM((SUB, lw), F32)],
        compiler_params=pltpu.CompilerParams(dimension_semantics=("arbitrary", "arbitrary"), vmem_limit_bytes=VMEM_LIMIT),
    )(z, z, hl, hl, dya, dyb, cw, cb, wa, ba, wx, bx, lam, pw, ps)


def _bwd_inp(dza, dm, x, dx1, win, g0, tm):
    t, d = x.shape
    nc = win.shape[1]
    za = dza.shape[1]
    n = t // tm
    sec = nc // N_DEV

    def body(dza_ref, dm_ref, x_ref, dx1_ref, win_ref, g0_ref, gx_ref, gwin_ref, pack_ref, acc):
        i = pl.program_id(0)

        @pl.when(i == 0)
        def _():
            acc[...] = jnp.zeros_like(acc)
            pack_ref[...] = jnp.zeros_like(pack_ref)

        xv = x_ref[...]
        r0 = _rsqrt_mean_sq(xv)
        xh = xv * r0
        h16 = (xh * g0_ref[...]).astype(_MM)
        dza16 = dza_ref[...]
        dm16 = dm_ref[...]
        dh = _dot_nt(dza16, win_ref[:, 0:za]) + _dot_nt(dm16, win_ref[:, za:nc])
        _store_chunks(pack_ref, 0, _colsum(dh * xh))
        dxh = dh * g0_ref[...]
        gx_ref[...] = dx1_ref[...] + r0 * (dxh - xh * jnp.mean(dxh * xh, axis=-1, keepdims=True))
        acc[:, 0:za] += _dot_tn(h16, dza16)
        acc[:, za:nc] += _dot_tn(h16, dm16)

        @pl.when(i == n - 1)
        def _():
            for k in range(N_DEV):
                pltpu.sync_copy(acc.at[:, pl.ds(k * sec, sec)], gwin_ref.at[k])

    tile = lambda w: pl.BlockSpec((tm, w), lambda i: (i, 0))
    return pl.pallas_call(
        body, name="bwd_inp", grid=(n,),
        in_specs=[tile(za), tile(nc - za), tile(d), tile(d), _const_spec(win.shape), _const_spec(g0.shape)],
        out_specs=(tile(d), pl.BlockSpec(memory_space=pl.ANY), pl.BlockSpec((INP_PACK_ROWS, HEAD), lambda i: (0, 0))),
        out_shape=(jax.ShapeDtypeStruct((t, d), F32), jax.ShapeDtypeStruct((N_DEV, d, sec), F32),
                   jax.ShapeDtypeStruct((INP_PACK_ROWS, HEAD), F32)),
        scratch_shapes=[pltpu.VMEM((d, nc), F32)],
        compiler_params=pltpu.CompilerParams(dimension_semantics=("arbitrary",), vmem_limit_bytes=VMEM_LIMIT),
    )(dza, dm, x, dx1, win, g0)


def _my_pos():
    return lax.axis_index("x"), lax.axis_index("y"), lax.axis_index("c")


def _other_chips(x, y):
    return [(1 - x, y), (x, 1 - y), (1 - x, 1 - y)]


def _all_gather(shards, full_shapes, slicers, name):
    nw = len(shards)

    def body(*refs):
        srcs, outs = refs[:nw], refs[nw:2 * nw]
        send_sems, recv_sems, local_sems = refs[2 * nw:]
        x, y, c = _my_pos()
        me, sibling = (x, y, c), (x, y, 1 - c)
        chips = _other_chips(x, y)

        def part(w, pos):
            return slicers[w](outs[w], 4 * pos[0] + 2 * pos[1] + pos[2])

        def copy(w, k, block, to, src=None):
            return pltpu.make_async_remote_copy(
                src_ref=part(w, block) if src is None else src, dst_ref=part(w, block),
                send_sem=send_sems.at[w, k], recv_sem=recv_sems.at[w, k], device_id=to, device_id_type=MESH)

        mine = [pltpu.make_async_copy(srcs[w], part(w, me), local_sems.at[w]) for w in range(nw)]
        for cp in mine:
            cp.start()
        first = []
        for w in range(nw):
            first.append(copy(w, 0, me, sibling, src=srcs[w]))
            first += [copy(w, 1 + j, me, (*chip, c), src=srcs[w]) for j, chip in enumerate(chips)]
        for cp in first:
            cp.start()
        passed = []
        for j, chip in enumerate(chips):
            for w in range(nw):
                copy(w, 1 + j, (*chip, c), me).wait_recv()
                fwd = copy(w, 4 + j, (*chip, c), sibling)
                fwd.start()
                passed.append(fwd)
        for w in range(nw):
            copy(w, 0, sibling, me).wait_recv()
            for j, chip in enumerate(chips):
                copy(w, 4 + j, (*chip, 1 - c), me).wait_recv()
        for cp in first + passed:
            cp.wait_send()
        for cp in mine:
            cp.wait()

    hbm = pl.BlockSpec(memory_space=pl.ANY)
    return pl.pallas_call(
        body, name=name, in_specs=[hbm] * nw, out_specs=tuple([hbm] * nw),
        out_shape=tuple(jax.ShapeDtypeStruct(shp, a.dtype) for shp, a in zip(full_shapes, shards)),
        scratch_shapes=[pltpu.SemaphoreType.DMA((nw, 7)), pltpu.SemaphoreType.DMA((nw, 7)), pltpu.SemaphoreType.DMA((nw,))],
    )(*shards)


def _rs_sibling(grads):
    nw = len(grads)

    def body(*refs):
        srcs, outs = refs[:nw], refs[nw:2 * nw]
        send_sems, recv_sems = refs[2 * nw:]
        x, y, c = _my_pos()
        copies = []
        for w in range(nw):
            for q in range(4):
                copies.append(pltpu.make_async_remote_copy(
                    src_ref=srcs[w].at[2 * q + (1 - c)], dst_ref=outs[w].at[q],
                    send_sem=send_sems.at[w, q], recv_sem=recv_sems.at[w, q],
                    device_id=(x, y, 1 - c), device_id_type=MESH))
        for cp in copies:
            cp.start()
        for cp in copies:
            cp.wait()

    hbm = pl.BlockSpec(memory_space=pl.ANY)
    return pl.pallas_call(
        body, name="rs_sibling", in_specs=[hbm] * nw, out_specs=tuple([hbm] * nw),
        out_shape=tuple(jax.ShapeDtypeStruct((4,) + g.shape[1:], g.dtype) for g in grads),
        scratch_shapes=[pltpu.SemaphoreType.DMA((nw, 4)), pltpu.SemaphoreType.DMA((nw, 4))],
    )(*grads)


def _rs_chips(chip_sums):
    nw = len(chip_sums)

    def body(*refs):
        srcs, outs = refs[:nw], refs[nw:2 * nw]
        send_sems, recv_sems = refs[2 * nw:]
        x, y, c = _my_pos()
        copies = []
        for w in range(nw):
            for k, (px, py) in enumerate(_other_chips(x, y)):
                copies.append(pltpu.make_async_remote_copy(
                    src_ref=srcs[w].at[2 * px + py], dst_ref=outs[w].at[k],
                    send_sem=send_sems.at[w, k], recv_sem=recv_sems.at[w, k],
                    device_id=(px, py, c), device_id_type=MESH))
        for cp in copies:
            cp.start()
        for cp in copies:
            cp.wait()

    hbm = pl.BlockSpec(memory_space=pl.ANY)
    return pl.pallas_call(
        body, name="rs_chips", in_specs=[hbm] * nw, out_specs=tuple([hbm] * nw),
        out_shape=tuple(jax.ShapeDtypeStruct((3,) + g.shape[1:], g.dtype) for g in chip_sums),
        scratch_shapes=[pltpu.SemaphoreType.DMA((nw, 3)), pltpu.SemaphoreType.DMA((nw, 3))],
    )(*chip_sums)


ROW_BLOCKS = 8


def _chip_sums(core_idx, grads, recvs, dtypes):
    nw = len(grads)

    def body(idx_ref, *refs):
        gs, rs, outs = refs[:nw], refs[nw:2 * nw], refs[2 * nw:]
        for g, r, o in zip(gs, rs, outs):
            o[...] = (g[...] + r[...]).astype(o.dtype)

    def blk(g):
        return (None, g.shape[1] // ROW_BLOCKS, g.shape[2])

    return pl.pallas_call(
        body, name="chip_sums",
        grid_spec=pltpu.PrefetchScalarGridSpec(
            num_scalar_prefetch=1, grid=(4, ROW_BLOCKS),
            in_specs=[pl.BlockSpec(blk(g), lambda q, i, s: (2 * q + s[0], i, 0)) for g in grads]
            + [pl.BlockSpec(blk(g), lambda q, i, s: (q, i, 0)) for g in grads],
            out_specs=tuple(pl.BlockSpec(blk(g), lambda q, i, s: (q, i, 0)) for g in grads)),
        out_shape=tuple(jax.ShapeDtypeStruct((4,) + g.shape[1:], dt) for g, dt in zip(grads, dtypes)),
    )(core_idx, *grads, *recvs)


def _adamw(w, g, m, v):
    m = ADAM_B1 * m + (1.0 - ADAM_B1) * g
    v = ADAM_B2 * v + (1.0 - ADAM_B2) * (g * g)
    m_hat = m / (1.0 - ADAM_B1 ** ADAM_STEP)
    v_hat = v / (1.0 - ADAM_B2 ** ADAM_STEP)
    delta = -ADAM_LR * (m_hat / (jnp.sqrt(v_hat) + ADAM_EPS) + ADAM_WD * w)
    return delta, m, v


def _adam_sections(sec_idx, grads, recv_sib, recv_chips, wmv):
    nw = len(wmv)

    def body(idx_ref, *refs):
        gs, rs, cs = refs[:nw + 1], refs[nw + 1:2 * nw + 2], refs[2 * nw + 2:3 * nw + 3]
        params = refs[3 * nw + 3:6 * nw + 3]
        outs = refs[6 * nw + 3:]
        for w in range(nw + 1):
            g = gs[w][...] + rs[w][...]
            for k in range(3):
                g = g + cs[w][k].astype(F32)
            if w == nw:
                outs[4 * nw][...] = g
                continue
            wv, mv, vv = (params[3 * w + t][...] for t in range(3))
            delta, m_new, v_new = _adamw(wv, g, mv, vv)
            outs[4 * w][...] = g
            outs[4 * w + 1][...] = delta
            outs[4 * w + 2][...] = m_new
            outs[4 * w + 3][...] = v_new

    def rb(g):
        return g.shape[1] // ROW_BLOCKS

    in_specs = [pl.BlockSpec((None, rb(g), g.shape[2]), lambda i, s: (s[0], i, 0)) for g in grads]
    in_specs += [pl.BlockSpec((None, rb(g), g.shape[2]), lambda i, s: (s[1], i, 0)) for g in grads]
    in_specs += [pl.BlockSpec((3, rb(g), g.shape[2]), lambda i, s: (0, i, 0)) for g in grads]
    sec = lambda g: pl.BlockSpec((rb(g), g.shape[2]), lambda i, s: (i, 0))
    for w in range(nw):
        in_specs += [sec(grads[w])] * 3
    out_specs, out_shape = [], []
    for g in grads[:nw]:
        out_specs += [sec(g)] * 4
        out_shape += [jax.ShapeDtypeStruct(g.shape[1:], F32)] * 4
    out_specs.append(sec(grads[nw]))
    out_shape.append(jax.ShapeDtypeStruct(grads[nw].shape[1:], F32))
    flat = [a for t in wmv for a in t]
    outs = pl.pallas_call(
        body, name="adam_sections",
        grid_spec=pltpu.PrefetchScalarGridSpec(num_scalar_prefetch=1, grid=(ROW_BLOCKS,), in_specs=in_specs,
                                               out_specs=tuple(out_specs)),
        out_shape=tuple(out_shape),
    )(sec_idx, *grads, *recv_sib, *recv_chips, *flat)
    return [outs[4 * w:4 * w + 4] for w in range(nw)], outs[4 * nw]


def _adam_small(sec_idx, gpack, vec_params, mat_params, conv_wmv):
    items = [(r0, t, "vec") for r0, t in vec_params] + [(r0, t, "mat") for r0, t in mat_params]
    items.append((_SEQ_ROWS["dcw"][0], conv_wmv, "conv"))

    def body(idx_ref, g_ref, *refs):
        ins, outs = refs[:3 * len(items)], refs[3 * len(items):]
        for n, (r0, _, kind) in enumerate(items):
            w_ref, m_ref, v_ref = ins[3 * n:3 * n + 3]
            o = outs[4 * n:4 * n + 4]
            if kind == "mat":
                g = g_ref[pl.ds(r0, w_ref.shape[0]), :]
                res = (g,) + _adamw(w_ref[...], g, m_ref[...], v_ref[...])
                for ref, val in zip(o, res):
                    ref[...] = val
            elif kind == "vec":
                for h in range(w_ref.shape[1] // HEAD):
                    cols = slice(h * HEAD, (h + 1) * HEAD)
                    g = g_ref[pl.ds(r0 + h, 1), :]
                    res = (g,) + _adamw(w_ref[:, cols], g, m_ref[:, cols], v_ref[:, cols])
                    for ref, val in zip(o, res):
                        ref[:, cols] = val
            else:
                rows = lax.broadcasted_iota(jnp.int32, (SUB, HEAD), 0)
                for k in range(4):
                    blk = g_ref[pl.ds(r0 + SUB * k, SUB), :]
                    g = jnp.sum(jnp.where(rows == idx_ref[0], blk, 0.0), axis=0, keepdims=True)
                    row = pl.ds(k, 1)
                    res = (g,) + _adamw(w_ref[row, :], g, m_ref[row, :], v_ref[row, :])
                    for ref, val in zip(o, res):
                        ref[row, :] = val

    flat = [a for _, t, _ in items for a in t]
    full = lambda a: pl.BlockSpec(a.shape, lambda i, s: (0,) * a.ndim)
    out_specs, out_shape = [], []
    for _, t, _ in items:
        out_specs += [full(t[0])] * 4
        out_shape += [jax.ShapeDtypeStruct(t[0].shape, F32)] * 4
    outs = pl.pallas_call(
        body, name="adam_small",
        grid_spec=pltpu.PrefetchScalarGridSpec(num_scalar_prefetch=1, grid=(1,),
                                               in_specs=[full(gpack)] + [full(a) for a in flat],
                                               out_specs=tuple(out_specs)),
        out_shape=tuple(out_shape),
    )(sec_idx, gpack, *flat)
    return [outs[4 * n:4 * n + 4] for n in range(len(items))]


def _col_slicer(width):
    return lambda ref, idx: ref.at[:, pl.ds(pl.multiple_of(idx * width, HEAD), width)]


def _row_slicer(rows):
    return lambda ref, idx: ref.at[pl.ds(pl.multiple_of(idx * rows, SUB), rows), :]


def kernel(x, p, norm_g, w_in, conv_w, conv_b, lru_w_a, lru_b_a, lru_w_x, lru_b_x, lru_lambda, pool_w, pool_scale, w_proj_lru, w_proj_pool, w_out, ple_norm_g, w_ple_gate, w_ple_proj, final_g, loss_target, m_norm_g, m_w_in, m_conv_w, m_conv_b, m_lru_w_a, m_lru_b_a, m_lru_w_x, m_lru_b_x, m_lru_lambda, m_pool_w, m_pool_scale, m_w_proj_lru, m_w_proj_pool, m_w_out, m_ple_norm_g, m_w_ple_gate, m_w_ple_proj, m_final_g, v_norm_g, v_w_in, v_conv_w, v_conv_b, v_lru_w_a, v_lru_b_a, v_lru_w_x, v_lru_b_x, v_lru_lambda, v_pool_w, v_pool_scale, v_w_proj_lru, v_w_proj_pool, v_w_out, v_ple_norm_g, v_w_ple_gate, v_w_ple_proj, v_final_g):
    nb, s, d = x.shape
    t = nb * s
    tm = min(TILE_M, s)
    mesh_axes = ("x", "y", "c")
    sec_idx = (4 * lax.axis_index("x") + 2 * lax.axis_index("y") + lax.axis_index("c")).astype(jnp.int32)
    chip_idx = (2 * lax.axis_index("x") + lax.axis_index("y")).astype(jnp.int32)
    core_idx = lax.axis_index("c").astype(jnp.int32)

    shards = [w_in[0].astype(_MM), w_proj_lru[0].astype(_MM), w_proj_pool[0].astype(_MM), w_out[0].astype(_MM),
              w_ple_gate[0].astype(_MM), w_ple_proj[0].astype(_MM), conv_w[0]]
    full_shapes = [(d, N_DEV * w_in.shape[2]), (N_DEV * w_proj_lru.shape[1], d), (w_proj_pool.shape[1], N_DEV * w_proj_pool.shape[2]),
                   (N_DEV * w_out.shape[1], d), (N_DEV * w_ple_gate.shape[1], d), (w_ple_proj.shape[1], N_DEV * w_ple_proj.shape[2]),
                   (conv_w.shape[1], N_DEV * conv_w.shape[2])]
    slicers = [_col_slicer(w_in.shape[2]), _row_slicer(w_proj_lru.shape[1]), _col_slicer(w_proj_pool.shape[2]),
               _row_slicer(w_out.shape[1]), _row_slicer(w_ple_gate.shape[1]), _col_slicer(w_ple_proj.shape[2]),
               _col_slicer(conv_w.shape[2])]
    win, wpl, wpp, wout, wpg, wpe, cw = _all_gather(shards, full_shapes, slicers, "gather_weights")

    lw = cw.shape[1]
    pwid = pool_scale.shape[1]
    wa = lru_w_a[0].astype(_MM)
    wx = lru_w_x[0].astype(_MM)
    pw = pool_w[0].astype(_MM)
    ba, bx = lru_b_a[0], lru_b_x[0]
    gf = final_g.reshape(1, d)

    z, hl, ya, yb = _fwd_seq(x, norm_g, win, cw, conv_b, wa, ba, wx, bx, lru_lambda, pw, pool_scale, tm)
    x2d = x.reshape(t, d)
    (dya, dyb, dm, dx1, g_pl, g_pp, g_out, g_pg, g_pe, pack_mid, loss_part) = _mid(
        ya.reshape(t, lw), yb.reshape(t, pwid), z.reshape(t, -1), x2d, p[0].reshape(t, -1), loss_target.reshape(t, d),
        wpl, wpp, wout, wpg, wpe, ple_norm_g, gf, tm)
    dza, pack_seq = _bwd_seq(z, hl, dya.reshape(nb, s, lw), dyb.reshape(nb, s, pwid), cw, conv_b, wa, ba, wx, bx,
                             lru_lambda, pw, pool_scale, tm)
    grad_x, g_in, pack_inp = _bwd_inp(dza.reshape(t, -1), dm, x2d, dx1, win, norm_g, tm)
    loss = lax.psum(loss_part[0, 0], mesh_axes)

    small = jnp.concatenate(
        [pack_seq, pack_mid, pack_inp, jnp.zeros((SMALL_ROWS - _OFF_INP - INP_PACK_ROWS, HEAD), F32)], axis=0)
    rows_of = lambda g: g.reshape(N_DEV, g.shape[0] // N_DEV, g.shape[1])
    grads = [g_in, rows_of(g_pl), g_pp, rows_of(g_out), rows_of(g_pg), g_pe, small.reshape(N_DEV, SMALL_SEC, HEAD)]
    recv_sib = _rs_sibling(grads)
    wire = [_WIRE] * (len(grads) - 1) + [F32]
    sums = _chip_sums(core_idx.reshape(1), grads, recv_sib, wire)
    recv_chips = _rs_chips(sums)

    wmv = [(w_in[0], m_w_in[0], v_w_in[0]), (w_proj_lru[0], m_w_proj_lru[0], v_w_proj_lru[0]),
           (w_proj_pool[0], m_w_proj_pool[0], v_w_proj_pool[0]), (w_out[0], m_w_out[0], v_w_out[0]),
           (w_ple_gate[0], m_w_ple_gate[0], v_w_ple_gate[0]), (w_ple_proj[0], m_w_ple_proj[0], v_w_ple_proj[0])]
    big, small_sec = _adam_sections(jnp.stack([sec_idx, chip_idx]), grads, recv_sib, recv_chips, wmv)
    (gsmall,) = _all_gather([small_sec], [(SMALL_ROWS, HEAD)], [_row_slicer(SMALL_SEC)], "gather_small")

    R = _SEQ_ROWS
    vec = lambda r0, *t: (r0, tuple(a.reshape(1, -1) for a in t))
    mat = lambda r0, *t: (r0, tuple(a.reshape(-1, HEAD) for a in t))
    vec_params = [vec(_OFF_INP, norm_g, m_norm_g, v_norm_g), vec(R["dcb"][0], conv_b, m_conv_b, v_conv_b),
                  vec(R["dlam"][0], lru_lambda, m_lru_lambda, v_lru_lambda),
                  vec(R["dps"][0], pool_scale, m_pool_scale, v_pool_scale),
                  vec(_OFF_MID, ple_norm_g, m_ple_norm_g, v_ple_norm_g),
                  vec(_OFF_MID + SUB, final_g, m_final_g, v_final_g)]
    mat_params = [mat(R["dwa"][0], lru_w_a, m_lru_w_a, v_lru_w_a), mat(R["dba"][0], lru_b_a, m_lru_b_a, v_lru_b_a),
                  mat(R["dwx"][0], lru_w_x, m_lru_w_x, v_lru_w_x), mat(R["dbx"][0], lru_b_x, m_lru_b_x, v_lru_b_x),
                  mat(R["dpw"][0], pool_w, m_pool_w, v_pool_w)]
    conv_wmv = (conv_w[0], m_conv_w[0], v_conv_w[0])
    small_out = _adam_small(sec_idx.reshape(1), gsmall, vec_params, mat_params, conv_wmv)

    res = {}
    names_small = ["norm_g", "conv_b", "lru_lambda", "pool_scale", "ple_norm_g", "final_g",
                   "lru_w_a", "lru_b_a", "lru_w_x", "lru_b_x", "pool_w", "conv_w"]
    shapes = {"norm_g": norm_g, "conv_b": conv_b, "lru_lambda": lru_lambda, "pool_scale": pool_scale,
              "ple_norm_g": ple_norm_g, "final_g": final_g, "lru_w_a": lru_w_a, "lru_b_a": lru_b_a, "lru_w_x": lru_w_x,
              "lru_b_x": lru_b_x, "pool_w": pool_w, "conv_w": conv_w}
    for name, quad in zip(names_small, small_out):
        res[name] = [a.reshape(shapes[name].shape) for a in quad]
    names_big = ["w_in", "w_proj_lru", "w_proj_pool", "w_out", "w_ple_gate", "w_ple_proj"]
    for name, quad, (w, _, _) in zip(names_big, big, wmv):
        res[name] = [a.reshape((1,) + w.shape) for a in quad]
    order = ["norm_g", "w_in", "conv_w", "conv_b", "lru_w_a", "lru_b_a", "lru_w_x", "lru_b_x", "lru_lambda", "pool_w",
             "pool_scale", "w_proj_lru", "w_proj_pool", "w_out", "ple_norm_g", "w_ple_gate", "w_ple_proj", "final_g"]
    out = [loss, grad_x.reshape(nb, s, d)]
    for kind in range(4):
        out += [res[name][kind] for name in order]
    return tuple(out)
```

```python
import functools

import jax
import jax.numpy as jnp
from jax import lax
from jax.experimental import pallas as pl
from jax.experimental.pallas import tpu as pltpu

F32 = jnp.float32
BF16 = jnp.bfloat16
_MM = jnp.bfloat16
_WIRE = jnp.bfloat16

EPS = 1e-6
LRU_C = 8.0
POOL_WINDOWS = (2, 4, 8, 16)
N_HEADS = 8
HEAD = 128
HALO = 16
SUB = 8

ADAM_LR = 0.001
ADAM_B1 = 0.9
ADAM_B2 = 0.999
ADAM_EPS = 1e-08
ADAM_WD = 0.01
ADAM_STEP = 10

N_DEV = 8
MESH = pl.DeviceIdType.MESH
VMEM_LIMIT = 60 * 1024 * 1024
TILE_M = 256

_SEQ_ROWS = {"dwa": (0, 1024), "dwx": (1024, 1024), "dpw": (2048, 512), "dba": (2560, 8), "dbx": (2568, 8),
             "dcb": (2576, 8), "dlam": (2584, 8), "dcw": (2592, 32), "dps": (2624, 8)}
SEQ_PACK_ROWS = 2632
MID_PACK_ROWS = 16
INP_PACK_ROWS = 8
SMALL_ROWS = 3072
SMALL_SEC = SMALL_ROWS // N_DEV
_OFF_MID = SEQ_PACK_ROWS
_OFF_INP = SEQ_PACK_ROWS + MID_PACK_ROWS


def _dot(a, b):
    return jnp.dot(a, b, preferred_element_type=F32)


def _dot_nt(a, b):
    return lax.dot_general(a, b, (((1,), (1,)), ((), ())), preferred_element_type=F32)


def _dot_tn(a, b):
    return lax.dot_general(a, b, (((0,), (0,)), ((), ())), preferred_element_type=F32)


def _sigmoid(v):
    return jax.nn.sigmoid(v)


def _neg_expm1(v):
    series = -v * (1.0 + v * (0.5 + v * (1.0 / 6.0 + v * (1.0 / 24.0 + v * (1.0 / 120.0)))))
    return jnp.where(v > -0.1, series, 1.0 - jnp.exp(v))


def _softplus_neg(lam):
    e = jnp.exp(-jnp.abs(lam))
    w = 1.0 + e
    l1p = jnp.where(w == 1.0, e, jnp.log(w) * (e / (w - 1.0)))
    return jnp.maximum(-lam, 0.0) + l1p


def _rsqrt_mean_sq(v):
    return lax.rsqrt(jnp.mean(v * v, axis=-1, keepdims=True) + EPS)


def _colsum(v):
    return jnp.sum(v, axis=0, keepdims=True)


def _const_spec(shape):
    nd = len(shape)
    return pl.BlockSpec(shape, lambda *_: (0,) * nd, pipeline_mode=pl.Buffered(1))


HBM_SPEC = pl.BlockSpec(memory_space=pl.ANY)


def _my_pos():
    return lax.axis_index("x"), lax.axis_index("y"), lax.axis_index("c")


def _other_chips(x, y):
    return [(1 - x, y), (x, 1 - y), (1 - x, 1 - y)]


def _gather_sems(nw):
    return [pltpu.SemaphoreType.DMA((nw, 7)), pltpu.SemaphoreType.DMA((nw, 7)), pltpu.SemaphoreType.DMA((nw,))]


def _gather_phases(srcs, outs, slicers, send_sems, recv_sems, local_sems):
    nw = len(srcs)
    x, y, c = _my_pos()
    me, sibling = (x, y, c), (x, y, 1 - c)
    chips = _other_chips(x, y)

    def part(w, pos):
        return slicers[w](outs[w], 4 * pos[0] + 2 * pos[1] + pos[2])

    def copy(w, k, block, to, src=None):
        return pltpu.make_async_remote_copy(
            src_ref=part(w, block) if src is None else src, dst_ref=part(w, block),
            send_sem=send_sems.at[w, k], recv_sem=recv_sems.at[w, k], device_id=to, device_id_type=MESH)

    def mine():
        return [pltpu.make_async_copy(srcs[w], part(w, me), local_sems.at[w]) for w in range(nw)]

    def first():
        out = []
        for w in range(nw):
            out.append(copy(w, 0, me, sibling, src=srcs[w]))
            out += [copy(w, 1 + j, me, (*chip, c), src=srcs[w]) for j, chip in enumerate(chips)]
        return out

    def passed():
        return [copy(w, 4 + j, (*chip, c), sibling) for j, chip in enumerate(chips) for w in range(nw)]

    def start():
        for cp in mine() + first():
            cp.start()

    def forward():
        for j, chip in enumerate(chips):
            for w in range(nw):
                copy(w, 1 + j, (*chip, c), me).wait_recv()
                copy(w, 4 + j, (*chip, c), sibling).start()

    def finish():
        for w in range(nw):
            copy(w, 0, sibling, me).wait_recv()
            for j, chip in enumerate(chips):
                copy(w, 4 + j, (*chip, 1 - c), me).wait_recv()
        for cp in first() + passed():
            cp.wait_send()
        for cp in mine():
            cp.wait()

    return start, forward, finish


def _sibling_copies(srcs, outs, send_sems, recv_sems):
    x, y, c = _my_pos()
    return [pltpu.make_async_remote_copy(
        src_ref=srcs[w].at[2 * q + (1 - c)], dst_ref=outs[w].at[q], send_sem=send_sems.at[w, q],
        recv_sem=recv_sems.at[w, q], device_id=(x, y, 1 - c), device_id_type=MESH)
        for w in range(len(srcs)) for q in range(4)]


def _chip_copies(srcs, outs, send_sems, recv_sems):
    x, y, c = _my_pos()
    return [pltpu.make_async_remote_copy(
        src_ref=srcs[w].at[2 * px + py], dst_ref=outs[w].at[k], send_sem=send_sems.at[w, k],
        recv_sem=recv_sems.at[w, k], device_id=(px, py, c), device_id_type=MESH)
        for w in range(len(srcs)) for k, (px, py) in enumerate(_other_chips(x, y))]


def _lru_gates(xc, h, wa_ref, ba_ref, wx_ref, bx_ref, neg_c_sp):
    xc16 = xc.astype(_MM)
    r = _sigmoid(_dot(xc16, wa_ref[h]) + ba_ref[pl.ds(h, 1), :])
    i = _sigmoid(_dot(xc16, wx_ref[h]) + bx_ref[pl.ds(h, 1), :])
    return r, i, neg_c_sp * r


def _conv_head(xa_ext, cw_ref, cb_ref, cols, tm):
    xc = cb_ref[:, cols]
    for k in range(4):
        xc = xc + cw_ref[pl.ds(k, 1), cols] * xa_ext[pl.ds(HALO - 3 + k, tm), cols]
    return xc


def _pool_diff(xb_ext, g, k, pos, tm):
    cols = slice(g * HEAD, (g + 1) * HEAD)
    cur = xb_ext[pl.ds(HALO, tm), cols]
    ws = cur
    for jj in range(1, k):
        ws = ws + xb_ext[pl.ds(HALO - jj, tm), cols]
    cnt = jnp.minimum(pos + 1, k).astype(F32)
    return ws / cnt - cur, cnt


def _fwd_seq(x, g0, win, cw, cb, wa, ba, wx, bx, lam, pw, ps, late_shards, late_shapes, late_slicers, tm):
    nb, s, d = x.shape
    nc = win.shape[1]
    lw = cw.shape[1]
    pwid = ps.shape[1]
    nt = s // tm
    nl = len(late_shards)
    n_steps = nb * nt

    def body(x_ref, g0_ref, win_ref, cw_ref, cb_ref, wa_ref, ba_ref, wx_ref, bx_ref, lam_ref, pw_ref, ps_ref, *rest):
        late_src, rest = rest[:nl], rest[nl:]
        z_ref, hl_ref, ya_ref, yb_ref, h_ref = rest[:5]
        late_out, rest = rest[5:5 + nl], rest[5 + nl:]
        xa_ext, xb_ext, a_buf, carry, send_sems, recv_sems, local_sems = rest
        j = pl.program_id(1)
        step_no = pl.program_id(0) * nt + j
        g_start, g_forward, g_finish = _gather_phases(late_src, late_out, late_slicers, send_sems, recv_sems, local_sems)
        pl.when(step_no == 0)(g_start)
        pl.when(step_no == n_steps // 2)(g_forward)

        @pl.when(j == 0)
        def _():
            xa_ext[pl.ds(0, HALO), :] = jnp.zeros((HALO, lw), F32)
            xb_ext[pl.ds(0, HALO), :] = jnp.zeros((HALO, pwid), F32)
            carry[...] = jnp.zeros_like(carry)

        xv = x_ref[...]
        h16 = ((xv * _rsqrt_mean_sq(xv)) * g0_ref[...]).astype(_MM)
        h_ref[...] = h16.T
        z_ref[...] = _dot(h16, win_ref[...])

        xa_ext[pl.ds(HALO, tm), :] = z_ref[:, 0:lw]
        for h in range(N_HEADS):
            cols = slice(h * HEAD, (h + 1) * HEAD)
            xc = _conv_head(xa_ext, cw_ref, cb_ref, cols, tm)
            neg_c_sp = -LRU_C * _softplus_neg(lam_ref[:, cols])
            _, i, log_a = _lru_gates(xc, h, wa_ref, ba_ref, wx_ref, bx_ref, neg_c_sp)
            a_buf[:, cols] = jnp.exp(log_a)
            hl_ref[:, cols] = jnp.sqrt(_neg_expm1(2.0 * log_a)) * (i * xc)

        rows = lax.broadcasted_iota(jnp.int32, (SUB, lw), 0)

        def step(c, car):
            i0 = pl.multiple_of(c * SUB, SUB)
            av = a_buf[pl.ds(i0, SUB), :]
            bv = hl_ref[pl.ds(i0, SUB), :]
            for sh in (1, 2, 4):
                m = rows >= sh
                a_sh = jnp.where(m, pltpu.roll(av, sh, 0), 1.0)
                b_sh = jnp.where(m, pltpu.roll(bv, sh, 0), 0.0)
                bv = av * b_sh + bv
                av = av * a_sh
            hv = av * car + bv
            hl_ref[pl.ds(i0, SUB), :] = hv
            return jnp.broadcast_to(hv[SUB - 1:SUB, :], (SUB, lw))

        carry[...] = lax.fori_loop(0, tm // SUB, step, carry[...])

        ga = z_ref[:, lw:2 * lw]
        ya_ref[...] = (hl_ref[...] * (ga * _sigmoid(ga))).astype(ya_ref.dtype)

        xb_ext[pl.ds(HALO, tm), :] = z_ref[:, 2 * lw:2 * lw + pwid]
        pos = j * tm + lax.broadcasted_iota(jnp.int32, (tm, HEAD), 0)
        for g, k in enumerate(POOL_WINDOWS):
            cols = slice(g * HEAD, (g + 1) * HEAD)
            diff, _ = _pool_diff(xb_ext, g, k, pos, tm)
            yp = _dot(diff.astype(_MM), pw_ref[g])
            gb = z_ref[:, 2 * lw + pwid + g * HEAD:2 * lw + pwid + (g + 1) * HEAD]
            yb_ref[:, cols] = ((yp * ps_ref[:, cols]) * (gb * _sigmoid(gb))).astype(yb_ref.dtype)

        xa_ext[pl.ds(0, HALO), :] = xa_ext[pl.ds(tm, HALO), :]
        xb_ext[pl.ds(0, HALO), :] = xb_ext[pl.ds(tm, HALO), :]
        pl.when(step_no == n_steps - 1)(g_finish)

    tile = lambda w: pl.BlockSpec((None, tm, w), lambda b, j: (b, j, 0))
    return pl.pallas_call(
        body, name="fwd_seq", grid=(nb, nt),
        in_specs=[tile(d), _const_spec(g0.shape), _const_spec(win.shape), _const_spec(cw.shape), _const_spec(cb.shape),
                  _const_spec(wa.shape), _const_spec(ba.shape), _const_spec(wx.shape), _const_spec(bx.shape),
                  _const_spec(lam.shape), _const_spec(pw.shape), _const_spec(ps.shape)] + [HBM_SPEC] * nl,
        out_specs=(tile(nc), tile(lw), tile(lw), tile(pwid), pl.BlockSpec((None, d, tm), lambda b, j: (b, 0, j)))
        + (HBM_SPEC,) * nl,
        out_shape=(jax.ShapeDtypeStruct((nb, s, nc), F32), jax.ShapeDtypeStruct((nb, s, lw), F32),
                   jax.ShapeDtypeStruct((nb, s, lw), _MM), jax.ShapeDtypeStruct((nb, s, pwid), _MM),
                   jax.ShapeDtypeStruct((nb, d, s), _MM))
        + tuple(jax.ShapeDtypeStruct(shp, a.dtype) for shp, a in zip(late_shapes, late_shards)),
        scratch_shapes=[pltpu.VMEM((tm + HALO, lw), F32), pltpu.VMEM((tm + HALO, pwid), F32),
                        pltpu.VMEM((tm, lw), F32), pltpu.VMEM((SUB, lw), F32)] + _gather_sems(nl),
        compiler_params=pltpu.CompilerParams(dimension_semantics=("arbitrary", "arbitrary"), vmem_limit_bytes=VMEM_LIMIT),
    )(x, g0, win, cw, cb, wa, ba, wx, bx, lam, pw, ps, *late_shards)


def _store_chunks(ref, row0, vec):
    for h in range(vec.shape[1] // HEAD):
        ref[pl.ds(row0 + h, 1), :] += vec[:, h * HEAD:(h + 1) * HEAD]


def _mid(ya, yb, z, x, p, tgt, wpl, wpp, wout, wpg, wpe, g1, gf, tm):
    t, d = x.shape
    lw = ya.shape[1]
    pwid = yb.shape[1]
    pdim = p.shape[1]
    n = t // tm
    sec_pp = wpp.shape[1] // N_DEV
    sec_pe = wpe.shape[1] // N_DEV

    def body(ya_ref, yb_ref, ma_ref, mb_ref, x_ref, p_ref, tgt_ref, wpl_ref, wpp_ref, wout_ref, wpg_ref, wpe_ref,
             g1_ref, gf_ref,
             dya_ref, dyb_ref, dm_ref, dx1_ref, gpl_ref, gpp_ref, gout_ref, gpg_ref, gpe_ref, pack_ref, loss_ref,
             acc_pl, acc_pp, acc_out, acc_pg, acc_pe):
        i = pl.program_id(0)

        @pl.when(i == 0)
        def _():
            for acc in (acc_pl, acc_pp, acc_out, acc_pg, acc_pe):
                acc[...] = jnp.zeros_like(acc)
            pack_ref[...] = jnp.zeros_like(pack_ref)
            loss_ref[...] = jnp.zeros_like(loss_ref)

        ya16 = ya_ref[...]
        yb16 = yb_ref[...]
        br_a = _dot(ya16, wpl_ref[...])
        br_b = _dot(yb16, wpp_ref[...])
        sa = _sigmoid(ma_ref[...])
        sb = _sigmoid(mb_ref[...])
        mg16 = (sa * br_a + sb * br_b).astype(_MM)
        x1 = x_ref[...] + _dot(mg16, wout_ref[...])
        r1 = _rsqrt_mean_sq(x1)
        n1 = x1 * r1
        h116 = (n1 * g1_ref[...]).astype(_MM)
        gate = _sigmoid(_dot(h116, wpg_ref[...]))
        p16 = p_ref[...].astype(_MM)
        pe = _dot(p16, wpe_ref[...])
        x2 = x1 + gate * pe
        r2 = _rsqrt_mean_sq(x2)
        n2 = x2 * r2
        err = n2 * gf_ref[...] - tgt_ref[...]
        sq = jnp.sum(_colsum(err * err), axis=1, keepdims=True)
        loss_ref[...] += jnp.broadcast_to(sq * (0.5 / d), loss_ref.shape)

        dy = err * (1.0 / d)
        _store_chunks(pack_ref, SUB, _colsum(dy * n2))
        dn2 = dy * gf_ref[...]
        dx2 = r2 * (dn2 - n2 * jnp.mean(dn2 * n2, axis=-1, keepdims=True))
        dpe16 = (dx2 * gate).astype(_MM)
        dpg16 = ((dx2 * pe) * (gate * (1.0 - gate))).astype(_MM)
        acc_pe[...] += _dot_tn(p16, dpe16)
        acc_pg[...] += _dot_tn(h116, dpg16)
        dh1 = _dot_nt(dpg16, wpg_ref[...])
        _store_chunks(pack_ref, 0, _colsum(dh1 * n1))
        dn1 = dh1 * g1_ref[...]
        dx1 = dx2 + r1 * (dn1 - n1 * jnp.mean(dn1 * n1, axis=-1, keepdims=True))
        dx1_ref[...] = dx1
        dx116 = dx1.astype(_MM)
        acc_out[...] += _dot_tn(mg16, dx116)
        dmg = _dot_nt(dx116, wout_ref[...])
        da16 = (dmg * sa).astype(_MM)
        db16 = (dmg * sb).astype(_MM)
        dm_ref[:, 0:d] = ((dmg * br_a) * (sa * (1.0 - sa))).astype(dm_ref.dtype)
        dm_ref[:, d:2 * d] = ((dmg * br_b) * (sb * (1.0 - sb))).astype(dm_ref.dtype)
        acc_pl[...] += _dot_tn(ya16, da16)
        acc_pp[...] += _dot_tn(yb16, db16)
        dya_ref[...] = _dot_nt(da16, wpl_ref[...])
        dyb_ref[...] = _dot_nt(db16, wpp_ref[...])

        @pl.when(i == n - 1)
        def _():
            pltpu.sync_copy(acc_pl, gpl_ref)
            pltpu.sync_copy(acc_out, gout_ref)
            pltpu.sync_copy(acc_pg, gpg_ref)
            for k in range(N_DEV):
                pltpu.sync_copy(acc_pp.at[:, pl.ds(k * sec_pp, sec_pp)], gpp_ref.at[k])
                pltpu.sync_copy(acc_pe.at[:, pl.ds(k * sec_pe, sec_pe)], gpe_ref.at[k])

    tile = lambda w: pl.BlockSpec((tm, w), lambda i: (i, 0))
    zcol = lambda c: pl.BlockSpec((tm, d), lambda i: (i, c))
    any_spec = pl.BlockSpec(memory_space=pl.ANY)
    full = lambda shape: pl.BlockSpec(shape, lambda i: (0,) * len(shape))
    return pl.pallas_call(
        body, name="mid", grid=(n,),
        in_specs=[tile(lw), tile(pwid), zcol(3), zcol(4), tile(d), tile(pdim), tile(d),
                  _const_spec(wpl.shape), _const_spec(wpp.shape), _const_spec(wout.shape), _const_spec(wpg.shape),
                  _const_spec(wpe.shape), _const_spec(g1.shape), _const_spec(gf.shape)],
        out_specs=(tile(lw), tile(pwid), tile(2 * d), tile(d), any_spec, any_spec, any_spec, any_spec, any_spec,
                   full((MID_PACK_ROWS, HEAD)), full((1, HEAD))),
        out_shape=(jax.ShapeDtypeStruct((t, lw), F32), jax.ShapeDtypeStruct((t, pwid), F32),
                   jax.ShapeDtypeStruct((t, 2 * d), _MM), jax.ShapeDtypeStruct((t, d), F32),
                   jax.ShapeDtypeStruct(wpl.shape, F32), jax.ShapeDtypeStruct((N_DEV, wpp.shape[0], sec_pp), F32),
                   jax.ShapeDtypeStruct(wout.shape, F32), jax.ShapeDtypeStruct(wpg.shape, F32),
                   jax.ShapeDtypeStruct((N_DEV, wpe.shape[0], sec_pe), F32),
                   jax.ShapeDtypeStruct((MID_PACK_ROWS, HEAD), F32), jax.ShapeDtypeStruct((1, HEAD), F32)),
        scratch_shapes=[pltpu.VMEM(wpl.shape, F32), pltpu.VMEM(wpp.shape, F32), pltpu.VMEM(wout.shape, F32),
                        pltpu.VMEM(wpg.shape, F32), pltpu.VMEM(wpe.shape, F32)],
        compiler_params=pltpu.CompilerParams(dimension_semantics=("arbitrary",), vmem_limit_bytes=VMEM_LIMIT),
    )(ya, yb, z, z, x, p, tgt, wpl, wpp, wout, wpg, wpe, g1, gf)


def _bwd_seq(z, hl, dya, dyb, dm, cw, cb, wa, ba, wx, bx, lam, pw, ps, sib_grads, tm):
    nb, s, nc = z.shape
    lw = cw.shape[1]
    pwid = ps.shape[1]
    nt = s // tm
    zw = 2 * lw + 2 * pwid
    R = _SEQ_ROWS
    ns = len(sib_grads)

    def body(zq_ref, zh_ref, hl_ref, hlh_ref, dya_ref, dyb_ref, dm_ref, cw_ref, cb_ref, wa_ref, ba_ref, wx_ref, bx_ref,
             lam_ref, pw_ref, ps_ref, *rest):
        sib_src, rest = rest[:ns], rest[ns:]
        dz_ref, pack_ref = rest[:2]
        sib_out, rest = rest[2:2 + ns], rest[2 + ns:]
        (xa_ext, xb_ext, a_ext, an_buf, hl_ext, dh_buf, xc_buf, r_buf, i_buf, dxc_ext, q_ext, carry,
         send_sems, recv_sems) = rest
        b = pl.program_id(0)
        j = pl.program_id(1)
        jr = nt - 1 - j
        has_prev = jr > 0

        @pl.when((b == 0) & (j == 0))
        def _():
            for cp in _sibling_copies(sib_src, sib_out, send_sems, recv_sems):
                cp.start()

        dz_ref[:, zw:nc] = dm_ref[...]

        @pl.when((b == 0) & (j == 0))
        def _():
            pack_ref[...] = jnp.zeros_like(pack_ref)

        @pl.when(j == 0)
        def _():
            a_ext[pl.ds(tm, SUB), :] = jnp.zeros((SUB, lw), F32)
            dxc_ext[pl.ds(tm, HALO), :] = jnp.zeros((HALO, lw), F32)
            q_ext[pl.ds(tm, HALO), :] = jnp.zeros((HALO, pwid), F32)
            carry[...] = jnp.zeros_like(carry)

        xa_ext[pl.ds(0, HALO), :] = jnp.where(has_prev, zh_ref[:, 0:lw], 0.0)
        xa_ext[pl.ds(HALO, tm), :] = zq_ref[:, 0:lw]
        hl_ext[pl.ds(0, SUB), :] = jnp.where(has_prev, hlh_ref[...], 0.0)
        hl_ext[pl.ds(SUB, tm), :] = hl_ref[...]
        for h in range(N_HEADS):
            cols = slice(h * HEAD, (h + 1) * HEAD)
            xc = _conv_head(xa_ext, cw_ref, cb_ref, cols, tm)
            neg_c_sp = -LRU_C * _softplus_neg(lam_ref[:, cols])
            r, i, log_a = _lru_gates(xc, h, wa_ref, ba_ref, wx_ref, bx_ref, neg_c_sp)
            xc_buf[:, cols] = xc
            r_buf[:, cols] = r
            i_buf[:, cols] = i
            a_ext[pl.ds(0, tm), cols] = jnp.exp(log_a)
            ga = zq_ref[:, lw + h * HEAD:lw + (h + 1) * HEAD]
            sg = _sigmoid(ga)
            dyav = dya_ref[:, cols]
            dh_buf[:, cols] = dyav * (ga * sg)
            dga = (dyav * hl_ext[pl.ds(SUB, tm), cols]) * (sg * (1.0 + ga * (1.0 - sg)))
            dz_ref[:, lw + h * HEAD:lw + (h + 1) * HEAD] = dga.astype(dz_ref.dtype)
        an_buf[...] = a_ext[pl.ds(1, tm), :]

        rows = lax.broadcasted_iota(jnp.int32, (SUB, lw), 0)
        nch = tm // SUB

        def step(c, car):
            i0 = pl.multiple_of((nch - 1 - c) * SUB, SUB)
            cv = an_buf[pl.ds(i0, SUB), :]
            bv = dh_buf[pl.ds(i0, SUB), :]
            for sh in (1, 2, 4):
                m = rows < SUB - sh
                c_sh = jnp.where(m, pltpu.roll(cv, SUB - sh, 0), 1.0)
                b_sh = jnp.where(m, pltpu.roll(bv, SUB - sh, 0), 0.0)
                bv = cv * b_sh + bv
                cv = cv * c_sh
            hv = cv * car + bv
            dh_buf[pl.ds(i0, SUB), :] = hv
            return jnp.broadcast_to(hv[0:1, :], (SUB, lw))

        carry[...] = lax.fori_loop(0, nch, step, carry[...])

        for h in range(N_HEADS):
            cols = slice(h * HEAD, (h + 1) * HEAD)
            lam_h = lam_ref[:, cols]
            neg_c_sp = -LRU_C * _softplus_neg(lam_h)
            xc = xc_buf[:, cols]
            r = r_buf[:, cols]
            i = i_buf[:, cols]
            a = a_ext[pl.ds(0, tm), cols]
            mult = jnp.sqrt(_neg_expm1(2.0 * (neg_c_sp * r)))
            dh = dh_buf[:, cols]
            dhx = dh * xc
            di = dhx * mult
            dlog_a = (dh * hl_ext[pl.ds(SUB - 1, tm), cols]) * a - (dhx * i) * ((a * a) / mult)
            pack_ref[pl.ds(R["dlam"][0] + h, 1), :] += _colsum(dlog_a * r) * (LRU_C * _sigmoid(-lam_h))
            dpa = (dlog_a * neg_c_sp) * (r * (1.0 - r))
            dpx = di * (i * (1.0 - i))
            dpa16 = dpa.astype(_MM)
            dpx16 = dpx.astype(_MM)
            xc16 = xc.astype(_MM)
            dxc = dh * (mult * i) + _dot_nt(dpa16, wa_ref[h]) + _dot_nt(dpx16, wx_ref[h])
            pack_ref[pl.ds(R["dwa"][0] + h * HEAD, HEAD), :] += _dot_tn(xc16, dpa16)
            pack_ref[pl.ds(R["dwx"][0] + h * HEAD, HEAD), :] += _dot_tn(xc16, dpx16)
            pack_ref[pl.ds(R["dba"][0] + h, 1), :] += _colsum(dpa)
            pack_ref[pl.ds(R["dbx"][0] + h, 1), :] += _colsum(dpx)
            pack_ref[pl.ds(R["dcb"][0] + h, 1), :] += _colsum(dxc)
            for k in range(4):
                pack_ref[pl.ds(R["dcw"][0] + SUB * k + h, 1), :] += _colsum(dxc * xa_ext[pl.ds(HALO - 3 + k, tm), cols])
            dxc_ext[pl.ds(0, tm), cols] = dxc

        for h in range(N_HEADS):
            cols = slice(h * HEAD, (h + 1) * HEAD)
            dxa = cw_ref[pl.ds(0, 1), cols] * dxc_ext[pl.ds(3, tm), cols]
            for k in range(1, 4):
                dxa = dxa + cw_ref[pl.ds(k, 1), cols] * dxc_ext[pl.ds(3 - k, tm), cols]
            dz_ref[:, cols] = dxa.astype(dz_ref.dtype)

        xb_ext[pl.ds(0, HALO), :] = jnp.where(has_prev, zh_ref[:, 2 * lw:2 * lw + pwid], 0.0)
        xb_ext[pl.ds(HALO, tm), :] = zq_ref[:, 2 * lw:2 * lw + pwid]
        pos = jr * tm + lax.broadcasted_iota(jnp.int32, (tm, HEAD), 0)
        for g, k in enumerate(POOL_WINDOWS):
            cols = slice(g * HEAD, (g + 1) * HEAD)
            diff, cnt = _pool_diff(xb_ext, g, k, pos, tm)
            diff16 = diff.astype(_MM)
            yp = _dot(diff16, pw_ref[g])
            sc = ps_ref[:, cols]
            gb = zq_ref[:, 2 * lw + pwid + g * HEAD:2 * lw + pwid + (g + 1) * HEAD]
            sgb = _sigmoid(gb)
            dybv = dyb_ref[:, cols]
            dy_pool = dybv * (gb * sgb)
            dgb = (dybv * (yp * sc)) * (sgb * (1.0 + gb * (1.0 - sgb)))
            pack_ref[pl.ds(R["dps"][0] + g, 1), :] += _colsum(dy_pool * yp)
            dyp16 = (dy_pool * sc).astype(_MM)
            pack_ref[pl.ds(R["dpw"][0] + g * HEAD, HEAD), :] += _dot_tn(diff16, dyp16)
            ddiff = _dot_nt(dyp16, pw_ref[g])
            q_ext[pl.ds(0, tm), cols] = ddiff / cnt
            dxb = q_ext[pl.ds(0, tm), cols] - ddiff
            for jj in range(1, k):
                dxb = dxb + q_ext[pl.ds(jj, tm), cols]
            dz_ref[:, 2 * lw + g * HEAD:2 * lw + (g + 1) * HEAD] = dxb.astype(dz_ref.dtype)
            dz_ref[:, 2 * lw + pwid + g * HEAD:2 * lw + pwid + (g + 1) * HEAD] = dgb.astype(dz_ref.dtype)

        a_ext[pl.ds(tm, SUB), :] = a_ext[pl.ds(0, SUB), :]
        dxc_ext[pl.ds(tm, HALO), :] = dxc_ext[pl.ds(0, HALO), :]
        q_ext[pl.ds(tm, HALO), :] = q_ext[pl.ds(0, HALO), :]

        @pl.when((b == nb - 1) & (j == nt - 1))
        def _():
            for cp in _sibling_copies(sib_src, sib_out, send_sems, recv_sems):
                cp.wait()

    rev = lambda w: pl.BlockSpec((None, tm, w), lambda b, j: (b, nt - 1 - j, 0))
    prev_rows = lambda rows, w: pl.BlockSpec(
        (None, rows, w), lambda b, j: (b, jnp.maximum((nt - 1 - j) * (tm // rows) - 1, 0), 0))
    return pl.pallas_call(
        body, name="bwd_seq", grid=(nb, nt),
        in_specs=[rev(zw), prev_rows(HALO, zw), rev(lw), prev_rows(SUB, lw), rev(lw), rev(pwid), rev(nc - zw),
                  _const_spec(cw.shape), _const_spec(cb.shape), _const_spec(wa.shape), _const_spec(ba.shape),
                  _const_spec(wx.shape), _const_spec(bx.shape), _const_spec(lam.shape), _const_spec(pw.shape),
                  _const_spec(ps.shape)] + [HBM_SPEC] * ns,
        out_specs=(rev(nc), pl.BlockSpec((SEQ_PACK_ROWS, HEAD), lambda b, j: (0, 0))) + (HBM_SPEC,) * ns,
        out_shape=(jax.ShapeDtypeStruct((nb, s, nc), _MM), jax.ShapeDtypeStruct((SEQ_PACK_ROWS, HEAD), F32))
        + tuple(jax.ShapeDtypeStruct((4,) + g.shape[1:], g.dtype) for g in sib_grads),
        scratch_shapes=[pltpu.VMEM((tm + HALO, lw), F32), pltpu.VMEM((tm + HALO, pwid), F32),
                        pltpu.VMEM((tm + SUB, lw), F32), pltpu.VMEM((tm, lw), F32), pltpu.VMEM((tm + SUB, lw), F32),
                        pltpu.VMEM((tm, lw), F32), pltpu.VMEM((tm, lw), F32), pltpu.VMEM((tm, lw), F32),
                        pltpu.VMEM((tm, lw), F32), pltpu.VMEM((tm + HALO, lw), F32),
                        pltpu.VMEM((tm + HALO, pwid), F32), pltpu.VMEM((SUB, lw), F32),
                        pltpu.SemaphoreType.DMA((ns, 4)), pltpu.SemaphoreType.DMA((ns, 4))],
        compiler_params=pltpu.CompilerParams(dimension_semantics=("arbitrary", "arbitrary"), vmem_limit_bytes=VMEM_LIMIT),
    )(z, z, hl, hl, dya, dyb, dm, cw, cb, wa, ba, wx, bx, lam, pw, ps, *sib_grads)


def _bwd_win(ht, dz, chip_srcs, small):
    nb, d, s = ht.shape
    nc = dz.shape[2]
    sec = nc // N_DEV
    nm = len(chip_srcs)

    def body(ht_ref, dz_ref, *rest):
        chip_src, small_src = rest[:nm], rest[nm]
        gwin_ref, recv_ref = rest[nm + 1:nm + 3]
        chip_out, small_out = rest[nm + 3:2 * nm + 3], rest[2 * nm + 3]
        acc, local_sems, sib_send, sib_recv, chip_send, chip_recv, small_send, small_recv = rest[2 * nm + 4:]
        k = pl.program_id(0)
        b = pl.program_id(1)
        slot = k % 2
        x, y, c = _my_pos()
        to_sibling = slot == 1 - c

        def local_copy(kk, sl):
            return pltpu.make_async_copy(acc.at[sl], gwin_ref.at[kk], local_sems.at[sl])

        def sib_copy(kk, sl):
            return pltpu.make_async_remote_copy(
                src_ref=acc.at[sl], dst_ref=recv_ref.at[kk // 2], send_sem=sib_send.at[kk // 2],
                recv_sem=sib_recv.at[kk // 2], device_id=(x, y, 1 - c), device_id_type=MESH)

        def others():
            return (_chip_copies(chip_src, chip_out, chip_send, chip_recv)
                    + _sibling_copies([small_src], [small_out], small_send, small_recv))

        @pl.when((k == 0) & (b == 0))
        def _():
            for cp in others():
                cp.start()

        @pl.when((k >= 2) & (b == 0))
        def _():
            local_copy(k - 2, slot).wait()
            pl.when(to_sibling)(lambda: sib_copy(k - 2, slot).wait_send())

        part = _dot(ht_ref[b], dz_ref[...])

        @pl.when(b == 0)
        def _():
            acc[slot] = part

        @pl.when(b != 0)
        def _():
            acc[slot] += part

        @pl.when(b == nb - 1)
        def _():
            local_copy(k, slot).start()
            pl.when(to_sibling)(lambda: sib_copy(k, slot).start())

        @pl.when((k == N_DEV - 1) & (b == nb - 1))
        def _():
            local_copy(N_DEV - 2, 0).wait()
            local_copy(N_DEV - 1, 1).wait()
            sib_copy(N_DEV - 2 + (1 - c), 1 - c).wait_send()
            for q in range(N_DEV // 2):
                sib_copy(2 * q, 0).wait_recv()
            for cp in others():
                cp.wait()

    return pl.pallas_call(
        body, name="bwd_win", grid=(N_DEV, nb),
        in_specs=[_const_spec(ht.shape), pl.BlockSpec((None, s, sec), lambda k, b: (b, 0, k))] + [HBM_SPEC] * (nm + 1),
        out_specs=(HBM_SPEC,) * (nm + 3),
        out_shape=(jax.ShapeDtypeStruct((N_DEV, d, sec), F32), jax.ShapeDtypeStruct((N_DEV // 2, d, sec), F32))
        + tuple(jax.ShapeDtypeStruct((3,) + g.shape[1:], g.dtype) for g in chip_srcs)
        + (jax.ShapeDtypeStruct((4,) + small.shape[1:], small.dtype),),
        scratch_shapes=[pltpu.VMEM((2, d, sec), F32), pltpu.SemaphoreType.DMA((2,)),
                        pltpu.SemaphoreType.DMA((N_DEV // 2,)), pltpu.SemaphoreType.DMA((N_DEV // 2,)),
                        pltpu.SemaphoreType.DMA((nm, 3)), pltpu.SemaphoreType.DMA((nm, 3)),
                        pltpu.SemaphoreType.DMA((1, 4)), pltpu.SemaphoreType.DMA((1, 4))],
        compiler_params=pltpu.CompilerParams(dimension_semantics=("arbitrary", "arbitrary"), vmem_limit_bytes=VMEM_LIMIT),
    )(ht, dz, *chip_srcs, small)


def _bwd_dx(dz, x, dx1, win, g0, chip_srcs, tm):
    t, d = x.shape
    nc = win.shape[1]
    n = t // tm
    nm = len(chip_srcs)

    def body(dz_ref, x_ref, dx1_ref, win_ref, g0_ref, *rest):
        chip_src = rest[:nm]
        gx_ref, pack_ref = rest[nm:nm + 2]
        chip_out = rest[nm + 2:2 * nm + 2]
        chip_send, chip_recv = rest[2 * nm + 2:]
        i = pl.program_id(0)

        @pl.when(i == 0)
        def _():
            pack_ref[...] = jnp.zeros_like(pack_ref)
            for cp in _chip_copies(chip_src, chip_out, chip_send, chip_recv):
                cp.start()

        xv = x_ref[...]
        r0 = _rsqrt_mean_sq(xv)
        xh = xv * r0
        dh = _dot_nt(dz_ref[...], win_ref[...])
        _store_chunks(pack_ref, 0, _colsum(dh * xh))
        dxh = dh * g0_ref[...]
        gx_ref[...] = dx1_ref[...] + r0 * (dxh - xh * jnp.mean(dxh * xh, axis=-1, keepdims=True))

        @pl.when(i == n - 1)
        def _():
            for cp in _chip_copies(chip_src, chip_out, chip_send, chip_recv):
                cp.wait()

    tile = lambda w: pl.BlockSpec((tm, w), lambda i: (i, 0))
    return pl.pallas_call(
        body, name="bwd_dx", grid=(n,),
        in_specs=[tile(nc), tile(d), tile(d), _const_spec(win.shape), _const_spec(g0.shape)] + [HBM_SPEC] * nm,
        out_specs=(tile(d), pl.BlockSpec((INP_PACK_ROWS, HEAD), lambda i: (0, 0))) + (HBM_SPEC,) * nm,
        out_shape=(jax.ShapeDtypeStruct((t, d), F32), jax.ShapeDtypeStruct((INP_PACK_ROWS, HEAD), F32))
        + tuple(jax.ShapeDtypeStruct((3,) + g.shape[1:], g.dtype) for g in chip_srcs),
        scratch_shapes=[pltpu.SemaphoreType.DMA((nm, 3)), pltpu.SemaphoreType.DMA((nm, 3))],
        compiler_params=pltpu.CompilerParams(dimension_semantics=("arbitrary",), vmem_limit_bytes=VMEM_LIMIT),
    )(dz, x, dx1, win, g0, *chip_srcs)


def _all_gather(shards, full_shapes, slicers, name):
    nw = len(shards)

    def body(*refs):
        start, forward, finish = _gather_phases(refs[:nw], refs[nw:2 * nw], slicers, *refs[2 * nw:])
        start()
        forward()
        finish()

    return pl.pallas_call(
        body, name=name, in_specs=[HBM_SPEC] * nw, out_specs=(HBM_SPEC,) * nw,
        out_shape=tuple(jax.ShapeDtypeStruct(shp, a.dtype) for shp, a in zip(full_shapes, shards)),
        scratch_shapes=_gather_sems(nw),
    )(*shards)


ROW_BLOCKS = 8


def _chip_sums(core_idx, grads, recvs, dtypes, name):
    nw = len(grads)

    def body(idx_ref, *refs):
        gs, rs, outs = refs[:nw], refs[nw:2 * nw], refs[2 * nw:]
        for g, r, o in zip(gs, rs, outs):
            o[...] = (g[...] + r[...]).astype(o.dtype)

    def blk(g):
        return (None, g.shape[1] // ROW_BLOCKS, g.shape[2])

    return pl.pallas_call(
        body, name=name,
        grid_spec=pltpu.PrefetchScalarGridSpec(
            num_scalar_prefetch=1, grid=(4, ROW_BLOCKS),
            in_specs=[pl.BlockSpec(blk(g), lambda q, i, s: (2 * q + s[0], i, 0)) for g in grads]
            + [pl.BlockSpec(blk(g), lambda q, i, s: (q, i, 0)) for g in grads],
            out_specs=tuple(pl.BlockSpec(blk(g), lambda q, i, s: (q, i, 0)) for g in grads)),
        out_shape=tuple(jax.ShapeDtypeStruct((4,) + g.shape[1:], dt) for g, dt in zip(grads, dtypes)),
    )(core_idx, *grads, *recvs)


def _adamw(w, g, m, v):
    m = ADAM_B1 * m + (1.0 - ADAM_B1) * g
    v = ADAM_B2 * v + (1.0 - ADAM_B2) * (g * g)
    m_hat = m / (1.0 - ADAM_B1 ** ADAM_STEP)
    v_hat = v / (1.0 - ADAM_B2 ** ADAM_STEP)
    delta = -ADAM_LR * (m_hat / (jnp.sqrt(v_hat) + ADAM_EPS) + ADAM_WD * w)
    return delta, m, v


def _adam_sections(sec_idx, grads, recv_sib, recv_chips, wmv):
    nw = len(wmv)

    def body(idx_ref, *refs):
        gs, rs, cs = refs[:nw + 1], refs[nw + 1:2 * nw + 2], refs[2 * nw + 2:3 * nw + 3]
        params = refs[3 * nw + 3:6 * nw + 3]
        outs = refs[6 * nw + 3:]
        for w in range(nw + 1):
            g = gs[w][...] + rs[w][...]
            for k in range(3):
                g = g + cs[w][k].astype(F32)
            if w == nw:
                outs[4 * nw][...] = g
                continue
            wv, mv, vv = (params[3 * w + t][...] for t in range(3))
            delta, m_new, v_new = _adamw(wv, g, mv, vv)
            outs[4 * w][...] = g
            outs[4 * w + 1][...] = delta
            outs[4 * w + 2][...] = m_new
            outs[4 * w + 3][...] = v_new

    def rb(g):
        return g.shape[1] // ROW_BLOCKS

    in_specs = [pl.BlockSpec((None, rb(g), g.shape[2]), lambda i, s: (s[0], i, 0)) for g in grads]
    in_specs += [pl.BlockSpec((None, rb(g), g.shape[2]), lambda i, s: (s[1], i, 0)) for g in grads]
    in_specs += [pl.BlockSpec((3, rb(g), g.shape[2]), lambda i, s: (0, i, 0)) for g in grads]
    sec = lambda g: pl.BlockSpec((rb(g), g.shape[2]), lambda i, s: (i, 0))
    for w in range(nw):
        in_specs += [sec(grads[w])] * 3
    out_specs, out_shape = [], []
    for g in grads[:nw]:
        out_specs += [sec(g)] * 4
        out_shape += [jax.ShapeDtypeStruct(g.shape[1:], F32)] * 4
    out_specs.append(sec(grads[nw]))
    out_shape.append(jax.ShapeDtypeStruct(grads[nw].shape[1:], F32))
    flat = [a for t in wmv for a in t]
    outs = pl.pallas_call(
        body, name="adam_sections",
        grid_spec=pltpu.PrefetchScalarGridSpec(num_scalar_prefetch=1, grid=(ROW_BLOCKS,), in_specs=in_specs,
                                               out_specs=tuple(out_specs)),
        out_shape=tuple(out_shape),
    )(sec_idx, *grads, *recv_sib, *recv_chips, *flat)
    return [outs[4 * w:4 * w + 4] for w in range(nw)], outs[4 * nw]


def _adam_small(sec_idx, gpack, parts, parts_wmv, vec_params, mat_params, conv_wmv):
    items = [(0, parts_wmv, "parts")]
    items += [(r0, t, "vec") for r0, t in vec_params] + [(r0, t, "mat") for r0, t in mat_params]
    items.append((_SEQ_ROWS["dcw"][0], conv_wmv, "conv"))

    def body(idx_ref, g_ref, parts_ref, *refs):
        ins, outs = refs[:3 * len(items)], refs[3 * len(items):]
        for n, (r0, _, kind) in enumerate(items):
            w_ref, m_ref, v_ref = ins[3 * n:3 * n + 3]
            o = outs[4 * n:4 * n + 4]
            if kind == "mat":
                g = g_ref[pl.ds(r0, w_ref.shape[0]), :]
                res = (g,) + _adamw(w_ref[...], g, m_ref[...], v_ref[...])
                for ref, val in zip(o, res):
                    ref[...] = val
            elif kind in ("vec", "parts"):
                for h in range(w_ref.shape[1] // HEAD):
                    cols = slice(h * HEAD, (h + 1) * HEAD)
                    if kind == "vec":
                        g = g_ref[pl.ds(r0 + h, 1), :]
                    else:
                        g = parts_ref[pl.ds(h, 1), :]
                        for dev in range(1, N_DEV):
                            g = g + parts_ref[pl.ds(SUB * dev + h, 1), :]
                    res = (g,) + _adamw(w_ref[:, cols], g, m_ref[:, cols], v_ref[:, cols])
                    for ref, val in zip(o, res):
                        ref[:, cols] = val
            else:
                rows = lax.broadcasted_iota(jnp.int32, (SUB, HEAD), 0)
                for k in range(4):
                    blk = g_ref[pl.ds(r0 + SUB * k, SUB), :]
                    g = jnp.sum(jnp.where(rows == idx_ref[0], blk, 0.0), axis=0, keepdims=True)
                    row = pl.ds(k, 1)
                    res = (g,) + _adamw(w_ref[row, :], g, m_ref[row, :], v_ref[row, :])
                    for ref, val in zip(o, res):
                        ref[row, :] = val

    flat = [a for _, t, _ in items for a in t]
    full = lambda a: pl.BlockSpec(a.shape, lambda i, s: (0,) * a.ndim)
    out_specs, out_shape = [], []
    for _, t, _ in items:
        out_specs += [full(t[0])] * 4
        out_shape += [jax.ShapeDtypeStruct(t[0].shape, F32)] * 4
    outs = pl.pallas_call(
        body, name="adam_small",
        grid_spec=pltpu.PrefetchScalarGridSpec(num_scalar_prefetch=1, grid=(1,),
                                               in_specs=[full(gpack), full(parts)] + [full(a) for a in flat],
                                               out_specs=tuple(out_specs)),
        out_shape=tuple(out_shape),
    )(sec_idx, gpack, parts, *flat)
    return [outs[4 * n:4 * n + 4] for n in range(len(items))]


def _col_slicer(width):
    return lambda ref, idx: ref.at[:, pl.ds(pl.multiple_of(idx * width, HEAD), width)]


def _row_slicer(rows):
    return lambda ref, idx: ref.at[pl.ds(pl.multiple_of(idx * rows, SUB), rows), :]


def kernel(x, p, norm_g, w_in, conv_w, conv_b, lru_w_a, lru_b_a, lru_w_x, lru_b_x, lru_lambda, pool_w, pool_scale, w_proj_lru, w_proj_pool, w_out, ple_norm_g, w_ple_gate, w_ple_proj, final_g, loss_target, m_norm_g, m_w_in, m_conv_w, m_conv_b, m_lru_w_a, m_lru_b_a, m_lru_w_x, m_lru_b_x, m_lru_lambda, m_pool_w, m_pool_scale, m_w_proj_lru, m_w_proj_pool, m_w_out, m_ple_norm_g, m_w_ple_gate, m_w_ple_proj, m_final_g, v_norm_g, v_w_in, v_conv_w, v_conv_b, v_lru_w_a, v_lru_b_a, v_lru_w_x, v_lru_b_x, v_lru_lambda, v_pool_w, v_pool_scale, v_w_proj_lru, v_w_proj_pool, v_w_out, v_ple_norm_g, v_w_ple_gate, v_w_ple_proj, v_final_g):
    nb, s, d = x.shape
    t = nb * s
    tm = min(TILE_M, s)
    mesh_axes = ("x", "y", "c")
    sec_idx = (4 * lax.axis_index("x") + 2 * lax.axis_index("y") + lax.axis_index("c")).astype(jnp.int32)
    chip_idx = (2 * lax.axis_index("x") + lax.axis_index("y")).astype(jnp.int32)
    core_idx = lax.axis_index("c").astype(jnp.int32)

    win, cw = _all_gather(
        [w_in[0].astype(_MM), conv_w[0]], [(d, N_DEV * w_in.shape[2]), (conv_w.shape[1], N_DEV * conv_w.shape[2])],
        [_col_slicer(w_in.shape[2]), _col_slicer(conv_w.shape[2])], "gather_w_in")
    late_shards = [w_proj_lru[0].astype(_MM), w_proj_pool[0].astype(_MM), w_out[0].astype(_MM),
                   w_ple_gate[0].astype(_MM), w_ple_proj[0].astype(_MM)]
    late_shapes = [(N_DEV * w_proj_lru.shape[1], d), (w_proj_pool.shape[1], N_DEV * w_proj_pool.shape[2]),
                   (N_DEV * w_out.shape[1], d), (N_DEV * w_ple_gate.shape[1], d),
                   (w_ple_proj.shape[1], N_DEV * w_ple_proj.shape[2])]
    late_slicers = [_row_slicer(w_proj_lru.shape[1]), _col_slicer(w_proj_pool.shape[2]), _row_slicer(w_out.shape[1]),
                    _row_slicer(w_ple_gate.shape[1]), _col_slicer(w_ple_proj.shape[2])]

    lw = cw.shape[1]
    pwid = pool_scale.shape[1]
    wa = lru_w_a[0].astype(_MM)
    wx = lru_w_x[0].astype(_MM)
    pw = pool_w[0].astype(_MM)
    ba, bx = lru_b_a[0], lru_b_x[0]
    gf = final_g.reshape(1, d)
    rows_of = lambda g: g.reshape(N_DEV, g.shape[0] // N_DEV, g.shape[1])
    core = core_idx.reshape(1)

    z, hl, ya, yb, ht, wpl, wpp, wout, wpg, wpe = _fwd_seq(
        x, norm_g, win, cw, conv_b, wa, ba, wx, bx, lru_lambda, pw, pool_scale, late_shards, late_shapes, late_slicers, tm)
    x2d = x.reshape(t, d)
    (dya, dyb, dm, dx1, g_pl, g_pp, g_out, g_pg, g_pe, pack_mid, loss_part) = _mid(
        ya.reshape(t, lw), yb.reshape(t, pwid), z.reshape(t, -1), x2d, p[0].reshape(t, -1), loss_target.reshape(t, d),
        wpl, wpp, wout, wpg, wpe, ple_norm_g, gf, tm)
    loss = lax.psum(loss_part[0, 0], mesh_axes)
    grads_mid = [rows_of(g_pl), g_pp, rows_of(g_out), rows_of(g_pg), g_pe]
    seq_out = _bwd_seq(z, hl, dya.reshape(nb, s, lw), dyb.reshape(nb, s, pwid), dm.reshape(nb, s, -1), cw, conv_b,
                       wa, ba, wx, bx, lru_lambda, pw, pool_scale, grads_mid, tm)
    dz, pack_seq, sib_mid = seq_out[0], seq_out[1], list(seq_out[2:])
    sums_mid = _chip_sums(core, grads_mid, sib_mid, [_WIRE] * len(grads_mid), "chip_sums_mid")
    small = jnp.concatenate(
        [pack_seq, pack_mid, jnp.zeros((SMALL_ROWS - _OFF_MID - MID_PACK_ROWS, HEAD), F32)], axis=0
    ).reshape(N_DEV, SMALL_SEC, HEAD)
    win_out = _bwd_win(ht, dz, sums_mid, small)
    g_in, sib_in, chips_mid, sib_small = win_out[0], win_out[1], list(win_out[2:-1]), win_out[-1]
    sums_in = _chip_sums(core, [g_in, small], [sib_in, sib_small], [_WIRE, F32], "chip_sums_in")
    dx_out = _bwd_dx(dz.reshape(t, -1), x2d, dx1, win, norm_g, sums_in, tm)
    grad_x, pack_inp, chips_in, chips_small = dx_out

    grads = [g_in] + grads_mid + [small]
    recv_sib = [sib_in] + sib_mid + [sib_small]
    recv_chips = [chips_in] + chips_mid + [chips_small]
    wmv = [(w_in[0], m_w_in[0], v_w_in[0]), (w_proj_lru[0], m_w_proj_lru[0], v_w_proj_lru[0]),
           (w_proj_pool[0], m_w_proj_pool[0], v_w_proj_pool[0]), (w_out[0], m_w_out[0], v_w_out[0]),
           (w_ple_gate[0], m_w_ple_gate[0], v_w_ple_gate[0]), (w_ple_proj[0], m_w_ple_proj[0], v_w_ple_proj[0])]
    big, small_sec = _adam_sections(jnp.stack([sec_idx, chip_idx]), grads, recv_sib, recv_chips, wmv)
    gsmall, g0_parts = _all_gather([small_sec, pack_inp], [(SMALL_ROWS, HEAD), (N_DEV * INP_PACK_ROWS, HEAD)],
                                   [_row_slicer(SMALL_SEC), _row_slicer(INP_PACK_ROWS)], "gather_small")

    R = _SEQ_ROWS
    vec = lambda r0, *t: (r0, tuple(a.reshape(1, -1) for a in t))
    mat = lambda r0, *t: (r0, tuple(a.reshape(-1, HEAD) for a in t))
    norm_wmv = vec(0, norm_g, m_norm_g, v_norm_g)[1]
    vec_params = [vec(R["dcb"][0], conv_b, m_conv_b, v_conv_b),
                  vec(R["dlam"][0], lru_lambda, m_lru_lambda, v_lru_lambda),
                  vec(R["dps"][0], pool_scale, m_pool_scale, v_pool_scale),
                  vec(_OFF_MID, ple_norm_g, m_ple_norm_g, v_ple_norm_g),
                  vec(_OFF_MID + SUB, final_g, m_final_g, v_final_g)]
    mat_params = [mat(R["dwa"][0], lru_w_a, m_lru_w_a, v_lru_w_a), mat(R["dba"][0], lru_b_a, m_lru_b_a, v_lru_b_a),
                  mat(R["dwx"][0], lru_w_x, m_lru_w_x, v_lru_w_x), mat(R["dbx"][0], lru_b_x, m_lru_b_x, v_lru_b_x),
                  mat(R["dpw"][0], pool_w, m_pool_w, v_pool_w)]
    conv_wmv = (conv_w[0], m_conv_w[0], v_conv_w[0])
    small_out = _adam_small(sec_idx.reshape(1), gsmall, g0_parts, norm_wmv, vec_params, mat_params, conv_wmv)

    res = {}
    names_small = ["norm_g", "conv_b", "lru_lambda", "pool_scale", "ple_norm_g", "final_g",
                   "lru_w_a", "lru_b_a", "lru_w_x", "lru_b_x", "pool_w", "conv_w"]
    shapes = {"norm_g": norm_g, "conv_b": conv_b, "lru_lambda": lru_lambda, "pool_scale": pool_scale,
              "ple_norm_g": ple_norm_g, "final_g": final_g, "lru_w_a": lru_w_a, "lru_b_a": lru_b_a, "lru_w_x": lru_w_x,
              "lru_b_x": lru_b_x, "pool_w": pool_w, "conv_w": conv_w}
    for name, quad in zip(names_small, small_out):
        res[name] = [a.reshape(shapes[name].shape) for a in quad]
    names_big = ["w_in", "w_proj_lru", "w_proj_pool", "w_out", "w_ple_gate", "w_ple_proj"]
    for name, quad, (w, _, _) in zip(names_big, big, wmv):
        res[name] = [a.reshape((1,) + w.shape) for a in quad]
    order = ["norm_g", "w_in", "conv_w", "conv_b", "lru_w_a", "lru_b_a", "lru_w_x", "lru_b_x", "lru_lambda", "pool_w",
             "pool_scale", "w_proj_lru", "w_proj_pool", "w_out", "ple_norm_g", "w_ple_gate", "w_ple_proj", "final_g"]
    out = [loss, grad_x.reshape(nb, s, d)]
    for kind in range(4):
        out += [res[name][kind] for name in order]
    return tuple(out)
```

```python
import jax
import jax.numpy as jnp
from jax import lax
from jax.experimental import pallas as pl
from jax.experimental.pallas import tpu as pltpu

F32 = jnp.float32
_MM = jnp.bfloat16
_WIRE = jnp.bfloat16

EPS = 1e-6
LRU_C = 8.0
POOL_WINDOWS = (2, 4, 8, 16)
N_HEADS = 8
HEAD = 128
HALO = 16
SUB = 8

ADAM_LR = 0.001
ADAM_B1 = 0.9
ADAM_B2 = 0.999
ADAM_EPS = 1e-08
ADAM_WD = 0.01
ADAM_STEP = 10

N_DEV = 8
MESH = pl.DeviceIdType.MESH
VMEM_LIMIT = 60 * 1024 * 1024
TILE_M = 256

_SEQ_ROWS = {"dwa": (0, 1024), "dwx": (1024, 1024), "dpw": (2048, 512), "dba": (2560, 8), "dbx": (2568, 8),
             "dcb": (2576, 8), "dlam": (2584, 8), "dcw": (2592, 32), "dps": (2624, 8)}
SEQ_PACK_ROWS = 2632
MID_PACK_ROWS = 24
MID_LOSS_ROW = 16
INP_PACK_ROWS = 8
SMALL_ROWS = 3072
SMALL_SEC = SMALL_ROWS // N_DEV
_OFF_MID = SEQ_PACK_ROWS


def _dot(a, b):
    return jnp.dot(a, b, preferred_element_type=F32)


def _dot_nt(a, b):
    return lax.dot_general(a, b, (((1,), (1,)), ((), ())), preferred_element_type=F32)


def _dot_tn(a, b):
    return lax.dot_general(a, b, (((0,), (0,)), ((), ())), preferred_element_type=F32)


def _sigmoid(v):
    return 0.5 * jnp.tanh(0.5 * v) + 0.5


def _lru_mult(log_a, a):
    return jnp.sqrt(-jnp.tanh(log_a) * (1.0 + a * a))


def _rows_back(v, n):
    return pltpu.roll(v, n, 0) if n else v


def _rows_ahead(v, n):
    return pltpu.roll(v, v.shape[0] - n, 0) if n else v


def _softplus_neg(lam):
    e = jnp.exp(-jnp.abs(lam))
    w = 1.0 + e
    l1p = jnp.where(w == 1.0, e, jnp.log(w) * (e / (w - 1.0)))
    return jnp.maximum(-lam, 0.0) + l1p


def _rsqrt_mean_sq(v):
    return lax.rsqrt(jnp.mean(v * v, axis=-1, keepdims=True) + EPS)


def _colsum(v):
    return jnp.sum(v, axis=0, keepdims=True)


def _const_spec(shape):
    nd = len(shape)
    return pl.BlockSpec(shape, lambda *_: (0,) * nd, pipeline_mode=pl.Buffered(1))


HBM_SPEC = pl.BlockSpec(memory_space=pl.ANY)


def _my_pos():
    return lax.axis_index("x"), lax.axis_index("y"), lax.axis_index("c")


def _other_chips(x, y):
    return [(1 - x, y), (x, 1 - y), (1 - x, 1 - y)]


def _gather_sems(nw):
    return [pltpu.SemaphoreType.DMA((nw, 7)), pltpu.SemaphoreType.DMA((nw, 7)), pltpu.SemaphoreType.DMA((nw,))]


def _gather_phases(srcs, outs, slicers, send_sems, recv_sems, local_sems):
    nw = len(srcs)
    x, y, c = _my_pos()
    me, sibling = (x, y, c), (x, y, 1 - c)
    chips = _other_chips(x, y)

    def part(w, pos):
        return slicers[w](outs[w], 4 * pos[0] + 2 * pos[1] + pos[2])

    def copy(w, k, block, to, src=None):
        return pltpu.make_async_remote_copy(
            src_ref=part(w, block) if src is None else src, dst_ref=part(w, block),
            send_sem=send_sems.at[w, k], recv_sem=recv_sems.at[w, k], device_id=to, device_id_type=MESH)

    def mine():
        return [pltpu.make_async_copy(srcs[w], part(w, me), local_sems.at[w]) for w in range(nw)]

    def first():
        out = []
        for w in range(nw):
            out.append(copy(w, 0, me, sibling, src=srcs[w]))
            out += [copy(w, 1 + j, me, (*chip, c), src=srcs[w]) for j, chip in enumerate(chips)]
        return out

    def passed():
        return [copy(w, 4 + j, (*chip, c), sibling) for j, chip in enumerate(chips) for w in range(nw)]

    def start():
        for cp in mine() + first():
            cp.start()

    def forward():
        for j, chip in enumerate(chips):
            for w in range(nw):
                copy(w, 1 + j, (*chip, c), me).wait_recv()
                copy(w, 4 + j, (*chip, c), sibling).start()

    def finish():
        for w in range(nw):
            copy(w, 0, sibling, me).wait_recv()
            for j, chip in enumerate(chips):
                copy(w, 4 + j, (*chip, 1 - c), me).wait_recv()
        for cp in first() + passed():
            cp.wait_send()
        for cp in mine():
            cp.wait()

    return start, forward, finish


def _sibling_copies(srcs, outs, send_sems, recv_sems):
    x, y, c = _my_pos()
    return [pltpu.make_async_remote_copy(
        src_ref=srcs[w].at[2 * q + (1 - c)], dst_ref=outs[w].at[q], send_sem=send_sems.at[w, q],
        recv_sem=recv_sems.at[w, q], device_id=(x, y, 1 - c), device_id_type=MESH)
        for w in range(len(srcs)) for q in range(4)]


def _chip_copies(srcs, outs, send_sems, recv_sems):
    x, y, c = _my_pos()
    return [pltpu.make_async_remote_copy(
        src_ref=srcs[w].at[2 * px + py], dst_ref=outs[w].at[k], send_sem=send_sems.at[w, k],
        recv_sem=recv_sems.at[w, k], device_id=(px, py, c), device_id_type=MESH)
        for w in range(len(srcs)) for k, (px, py) in enumerate(_other_chips(x, y))]


def _lru_gates(xc, h, wa_ref, ba_ref, wx_ref, bx_ref, neg_c_sp):
    xc16 = xc.astype(_MM)
    r = _sigmoid(_dot(xc16, wa_ref[h]) + ba_ref[pl.ds(h, 1), :])
    i = _sigmoid(_dot(xc16, wx_ref[h]) + bx_ref[pl.ds(h, 1), :])
    return r, i, neg_c_sp * r


def _conv_head(xa_ext, cw_ref, cb_ref, cols, tm):
    ext = xa_ext[:, cols]
    xc = cb_ref[:, cols] + cw_ref[pl.ds(3, 1), cols] * ext[HALO:HALO + tm]
    for k in range(3):
        xc = xc + cw_ref[pl.ds(k, 1), cols] * _rows_back(ext, 3 - k)[HALO:HALO + tm]
    return xc


def _window_sum(ext, k, shift):
    n = 1
    while n < k:
        ext = ext + shift(ext, n)
        n *= 2
    return ext


def _pool_diff(xb_ext, g, k, pos, tm):
    cols = slice(g * HEAD, (g + 1) * HEAD)
    ext = xb_ext[:, cols]
    ws = _window_sum(ext, k, _rows_back)[HALO:HALO + tm]
    cnt = jnp.minimum(pos + 1, k).astype(F32)
    return ws / cnt - ext[HALO:HALO + tm], cnt


def _fwd_seq(x, g0, win, cw, cb, wa, ba, wx, bx, lam, pw, ps, late_shards, late_shapes, late_slicers, tm):
    nb, s, d = x.shape
    nc = win.shape[1]
    lw = cw.shape[1]
    pwid = ps.shape[1]
    nt = s // tm
    nl = len(late_shards)
    n_steps = nb * nt

    def body(x_ref, g0_ref, win_ref, cw_ref, cb_ref, wa_ref, ba_ref, wx_ref, bx_ref, lam_ref, pw_ref, ps_ref, *rest):
        late_src, rest = rest[:nl], rest[nl:]
        z_ref, hl_ref, ya_ref, yb_ref, h_ref = rest[:5]
        late_out, rest = rest[5:5 + nl], rest[5 + nl:]
        xa_ext, xb_ext, a_buf, carry, send_sems, recv_sems, local_sems = rest
        j = pl.program_id(1)
        step_no = pl.program_id(0) * nt + j
        g_start, g_forward, g_finish = _gather_phases(late_src, late_out, late_slicers, send_sems, recv_sems, local_sems)
        pl.when(step_no == 0)(g_start)
        pl.when(step_no == n_steps // 2)(g_forward)

        @pl.when(j == 0)
        def _():
            xa_ext[pl.ds(0, HALO), :] = jnp.zeros((HALO, lw), F32)
            xb_ext[pl.ds(0, HALO), :] = jnp.zeros((HALO, pwid), F32)
            carry[...] = jnp.zeros_like(carry)

        xv = x_ref[...]
        h16 = ((xv * _rsqrt_mean_sq(xv)) * g0_ref[...]).astype(_MM)
        h_ref[...] = h16.T
        z_ref[...] = _dot(h16, win_ref[...])

        xa_ext[pl.ds(HALO, tm), :] = z_ref[:, 0:lw]
        for h in range(N_HEADS):
            cols = slice(h * HEAD, (h + 1) * HEAD)
            xc = _conv_head(xa_ext, cw_ref, cb_ref, cols, tm)
            neg_c_sp = -LRU_C * _softplus_neg(lam_ref[:, cols])
            _, i, log_a = _lru_gates(xc, h, wa_ref, ba_ref, wx_ref, bx_ref, neg_c_sp)
            a = jnp.exp(log_a)
            a_buf[:, cols] = a
            hl_ref[:, cols] = _lru_mult(log_a, a) * (i * xc)

        rows = lax.broadcasted_iota(jnp.int32, (SUB, lw), 0)

        def step(c, car):
            i0 = pl.multiple_of(c * SUB, SUB)
            av = a_buf[pl.ds(i0, SUB), :]
            bv = hl_ref[pl.ds(i0, SUB), :]
            for sh in (1, 2, 4):
                m = rows >= sh
                a_sh = jnp.where(m, pltpu.roll(av, sh, 0), 1.0)
                b_sh = jnp.where(m, pltpu.roll(bv, sh, 0), 0.0)
                bv = av * b_sh + bv
                av = av * a_sh
            hv = av * car + bv
            hl_ref[pl.ds(i0, SUB), :] = hv
            return jnp.broadcast_to(hv[SUB - 1:SUB, :], (SUB, lw))

        carry[...] = lax.fori_loop(0, tm // SUB, step, carry[...])

        ga = z_ref[:, lw:2 * lw]
        ya_ref[...] = (hl_ref[...] * (ga * _sigmoid(ga))).astype(ya_ref.dtype)

        xb_ext[pl.ds(HALO, tm), :] = z_ref[:, 2 * lw:2 * lw + pwid]
        pos = j * tm + lax.broadcasted_iota(jnp.int32, (tm, HEAD), 0)
        for g, k in enumerate(POOL_WINDOWS):
            cols = slice(g * HEAD, (g + 1) * HEAD)
            diff, _ = _pool_diff(xb_ext, g, k, pos, tm)
            yp = _dot(diff.astype(_MM), pw_ref[g])
            gb = z_ref[:, 2 * lw + pwid + g * HEAD:2 * lw + pwid + (g + 1) * HEAD]
            yb_ref[:, cols] = ((yp * ps_ref[:, cols]) * (gb * _sigmoid(gb))).astype(yb_ref.dtype)

        xa_ext[pl.ds(0, HALO), :] = xa_ext[pl.ds(tm, HALO), :]
        xb_ext[pl.ds(0, HALO), :] = xb_ext[pl.ds(tm, HALO), :]
        pl.when(step_no == n_steps - 1)(g_finish)

    tile = lambda w: pl.BlockSpec((None, tm, w), lambda b, j: (b, j, 0))
    return pl.pallas_call(
        body, name="fwd_seq", grid=(nb, nt),
        in_specs=[tile(d), _const_spec(g0.shape), _const_spec(win.shape), _const_spec(cw.shape), _const_spec(cb.shape),
                  _const_spec(wa.shape), _const_spec(ba.shape), _const_spec(wx.shape), _const_spec(bx.shape),
                  _const_spec(lam.shape), _const_spec(pw.shape), _const_spec(ps.shape)] + [HBM_SPEC] * nl,
        out_specs=(tile(nc), tile(lw), tile(lw), tile(pwid), pl.BlockSpec((None, d, tm), lambda b, j: (b, 0, j)))
        + (HBM_SPEC,) * nl,
        out_shape=(jax.ShapeDtypeStruct((nb, s, nc), F32), jax.ShapeDtypeStruct((nb, s, lw), F32),
                   jax.ShapeDtypeStruct((nb, s, lw), _MM), jax.ShapeDtypeStruct((nb, s, pwid), _MM),
                   jax.ShapeDtypeStruct((nb, d, s), _MM))
        + tuple(jax.ShapeDtypeStruct(shp, a.dtype) for shp, a in zip(late_shapes, late_shards)),
        scratch_shapes=[pltpu.VMEM((tm + HALO, lw), F32), pltpu.VMEM((tm + HALO, pwid), F32),
                        pltpu.VMEM((tm, lw), F32), pltpu.VMEM((SUB, lw), F32)] + _gather_sems(nl),
        compiler_params=pltpu.CompilerParams(dimension_semantics=("arbitrary", "arbitrary"), vmem_limit_bytes=VMEM_LIMIT),
    )(x, g0, win, cw, cb, wa, ba, wx, bx, lam, pw, ps, *late_shards)


def _store_chunks(ref, row0, vec):
    for h in range(vec.shape[1] // HEAD):
        ref[pl.ds(row0 + h, 1), :] += vec[:, h * HEAD:(h + 1) * HEAD]


def _mid(ya, yb, z, x, p, tgt, wpl, wpp, wout, wpg, wpe, g1, gf, tm):
    t, d = x.shape
    lw = ya.shape[1]
    pwid = yb.shape[1]
    pdim = p.shape[1]
    n = t // tm
    sec_pp = wpp.shape[1] // N_DEV
    sec_pe = wpe.shape[1] // N_DEV

    def body(ya_ref, yb_ref, ma_ref, mb_ref, x_ref, p_ref, tgt_ref, wpl_ref, wpp_ref, wout_ref, wpg_ref, wpe_ref,
             g1_ref, gf_ref,
             dya_ref, dyb_ref, dm_ref, dx1_ref, gpl_ref, gpp_ref, gout_ref, gpg_ref, gpe_ref, pack_ref,
             acc_pl, acc_pp, acc_out, acc_pg, acc_pe):
        i = pl.program_id(0)

        @pl.when(i == 0)
        def _():
            for acc in (acc_pl, acc_pp, acc_out, acc_pg, acc_pe):
                acc[...] = jnp.zeros_like(acc)
            pack_ref[...] = jnp.zeros_like(pack_ref)

        ya16 = ya_ref[...]
        yb16 = yb_ref[...]
        br_a = _dot(ya16, wpl_ref[...])
        br_b = _dot(yb16, wpp_ref[...])
        sa = _sigmoid(ma_ref[...])
        sb = _sigmoid(mb_ref[...])
        mg16 = (sa * br_a + sb * br_b).astype(_MM)
        x1 = x_ref[...] + _dot(mg16, wout_ref[...])
        r1 = _rsqrt_mean_sq(x1)
        n1 = x1 * r1
        h116 = (n1 * g1_ref[...]).astype(_MM)
        gate = _sigmoid(_dot(h116, wpg_ref[...]))
        p16 = p_ref[...].astype(_MM)
        pe = _dot(p16, wpe_ref[...])
        x2 = x1 + gate * pe
        r2 = _rsqrt_mean_sq(x2)
        n2 = x2 * r2
        err = n2 * gf_ref[...] - tgt_ref[...]
        sq = jnp.sum(_colsum(err * err), axis=1, keepdims=True)
        pack_ref[pl.ds(MID_LOSS_ROW, 1), :] += jnp.broadcast_to(sq * (0.5 / d), (1, HEAD))

        dy = err * (1.0 / d)
        _store_chunks(pack_ref, SUB, _colsum(dy * n2))
        dn2 = dy * gf_ref[...]
        dx2 = r2 * (dn2 - n2 * jnp.mean(dn2 * n2, axis=-1, keepdims=True))
        dpe16 = (dx2 * gate).astype(_MM)
        dpg16 = ((dx2 * pe) * (gate * (1.0 - gate))).astype(_MM)
        acc_pe[...] += _dot_tn(p16, dpe16)
        acc_pg[...] += _dot_tn(h116, dpg16)
        dh1 = _dot_nt(dpg16, wpg_ref[...])
        _store_chunks(pack_ref, 0, _colsum(dh1 * n1))
        dn1 = dh1 * g1_ref[...]
        dx1 = dx2 + r1 * (dn1 - n1 * jnp.mean(dn1 * n1, axis=-1, keepdims=True))
        dx1_ref[...] = dx1
        dx116 = dx1.astype(_MM)
        acc_out[...] += _dot_tn(mg16, dx116)
        dmg = _dot_nt(dx116, wout_ref[...])
        da16 = (dmg * sa).astype(_MM)
        db16 = (dmg * sb).astype(_MM)
        dm_ref[:, 0:d] = ((dmg * br_a) * (sa * (1.0 - sa))).astype(dm_ref.dtype)
        dm_ref[:, d:2 * d] = ((dmg * br_b) * (sb * (1.0 - sb))).astype(dm_ref.dtype)
        acc_pl[...] += _dot_tn(ya16, da16)
        acc_pp[...] += _dot_tn(yb16, db16)
        dya_ref[...] = _dot_nt(da16, wpl_ref[...])
        dyb_ref[...] = _dot_nt(db16, wpp_ref[...])

        @pl.when(i == n - 1)
        def _():
            pltpu.sync_copy(acc_pl, gpl_ref)
            pltpu.sync_copy(acc_out, gout_ref)
            pltpu.sync_copy(acc_pg, gpg_ref)
            for k in range(N_DEV):
                pltpu.sync_copy(acc_pp.at[:, pl.ds(k * sec_pp, sec_pp)], gpp_ref.at[k])
                pltpu.sync_copy(acc_pe.at[:, pl.ds(k * sec_pe, sec_pe)], gpe_ref.at[k])

    tile = lambda w: pl.BlockSpec((tm, w), lambda i: (i, 0))
    zcol = lambda c: pl.BlockSpec((tm, d), lambda i: (i, c))
    any_spec = pl.BlockSpec(memory_space=pl.ANY)
    full = lambda shape: pl.BlockSpec(shape, lambda i: (0,) * len(shape))
    return pl.pallas_call(
        body, name="mid", grid=(n,),
        in_specs=[tile(lw), tile(pwid), zcol(3), zcol(4), tile(d), tile(pdim), tile(d),
                  _const_spec(wpl.shape), _const_spec(wpp.shape), _const_spec(wout.shape), _const_spec(wpg.shape),
                  _const_spec(wpe.shape), _const_spec(g1.shape), _const_spec(gf.shape)],
        out_specs=(tile(lw), tile(pwid), tile(2 * d), tile(d), any_spec, any_spec, any_spec, any_spec, any_spec,
                   full((MID_PACK_ROWS, HEAD))),
        out_shape=(jax.ShapeDtypeStruct((t, lw), F32), jax.ShapeDtypeStruct((t, pwid), F32),
                   jax.ShapeDtypeStruct((t, 2 * d), _MM), jax.ShapeDtypeStruct((t, d), F32),
                   jax.ShapeDtypeStruct(wpl.shape, F32), jax.ShapeDtypeStruct((N_DEV, wpp.shape[0], sec_pp), F32),
                   jax.ShapeDtypeStruct(wout.shape, F32), jax.ShapeDtypeStruct(wpg.shape, F32),
                   jax.ShapeDtypeStruct((N_DEV, wpe.shape[0], sec_pe), F32),
                   jax.ShapeDtypeStruct((MID_PACK_ROWS, HEAD), F32)),
        scratch_shapes=[pltpu.VMEM(wpl.shape, F32), pltpu.VMEM(wpp.shape, F32), pltpu.VMEM(wout.shape, F32),
                        pltpu.VMEM(wpg.shape, F32), pltpu.VMEM(wpe.shape, F32)],
        compiler_params=pltpu.CompilerParams(dimension_semantics=("arbitrary",), vmem_limit_bytes=VMEM_LIMIT),
    )(ya, yb, z, z, x, p, tgt, wpl, wpp, wout, wpg, wpe, g1, gf)


def _bwd_seq(z, hl, dya, dyb, dm, cw, cb, wa, ba, wx, bx, lam, pw, ps, sib_grads, tm):
    nb, s, nc = z.shape
    lw = cw.shape[1]
    pwid = ps.shape[1]
    nt = s // tm
    zw = 2 * lw + 2 * pwid
    R = _SEQ_ROWS
    ns = len(sib_grads)

    def body(zq_ref, zh_ref, hl_ref, hlh_ref, dya_ref, dyb_ref, dm_ref, cw_ref, cb_ref, wa_ref, ba_ref, wx_ref, bx_ref,
             lam_ref, pw_ref, ps_ref, *rest):
        sib_src, rest = rest[:ns], rest[ns:]
        dz_ref, pack_ref = rest[:2]
        sib_out, rest = rest[2:2 + ns], rest[2 + ns:]
        (xa_ext, xb_ext, a_ext, an_buf, hl_ext, dh_buf, xc_buf, r_buf, i_buf, dxc_ext, q_ext, carry,
         send_sems, recv_sems) = rest
        b = pl.program_id(0)
        j = pl.program_id(1)
        jr = nt - 1 - j
        has_prev = jr > 0

        @pl.when((b == 0) & (j == 0))
        def _():
            for cp in _sibling_copies(sib_src, sib_out, send_sems, recv_sems):
                cp.start()

        dz_ref[:, zw:nc] = dm_ref[...]

        @pl.when((b == 0) & (j == 0))
        def _():
            pack_ref[...] = jnp.zeros_like(pack_ref)

        @pl.when(j == 0)
        def _():
            a_ext[pl.ds(tm, SUB), :] = jnp.zeros((SUB, lw), F32)
            dxc_ext[pl.ds(tm, HALO), :] = jnp.zeros((HALO, lw), F32)
            q_ext[pl.ds(tm, HALO), :] = jnp.zeros((HALO, pwid), F32)
            carry[...] = jnp.zeros_like(carry)

        xa_ext[pl.ds(0, HALO), :] = jnp.where(has_prev, zh_ref[:, 0:lw], 0.0)
        xa_ext[pl.ds(HALO, tm), :] = zq_ref[:, 0:lw]
        hl_ext[pl.ds(0, SUB), :] = jnp.where(has_prev, hlh_ref[...], 0.0)
        hl_ext[pl.ds(SUB, tm), :] = hl_ref[...]
        for h in range(N_HEADS):
            cols = slice(h * HEAD, (h + 1) * HEAD)
            xc = _conv_head(xa_ext, cw_ref, cb_ref, cols, tm)
            neg_c_sp = -LRU_C * _softplus_neg(lam_ref[:, cols])
            r, i, log_a = _lru_gates(xc, h, wa_ref, ba_ref, wx_ref, bx_ref, neg_c_sp)
            xc_buf[:, cols] = xc
            r_buf[:, cols] = r
            i_buf[:, cols] = i
            a_ext[pl.ds(0, tm), cols] = jnp.exp(log_a)
            ga = zq_ref[:, lw + h * HEAD:lw + (h + 1) * HEAD]
            sg = _sigmoid(ga)
            dyav = dya_ref[:, cols]
            dh_buf[:, cols] = dyav * (ga * sg)
            dga = (dyav * hl_ext[pl.ds(SUB, tm), cols]) * (sg * (1.0 + ga * (1.0 - sg)))
            dz_ref[:, lw + h * HEAD:lw + (h + 1) * HEAD] = dga.astype(dz_ref.dtype)
        an_buf[...] = _rows_ahead(a_ext[...], 1)[0:tm]

        rows = lax.broadcasted_iota(jnp.int32, (SUB, lw), 0)
        nch = tm // SUB

        def step(c, car):
            i0 = pl.multiple_of((nch - 1 - c) * SUB, SUB)
            cv = an_buf[pl.ds(i0, SUB), :]
            bv = dh_buf[pl.ds(i0, SUB), :]
            for sh in (1, 2, 4):
                m = rows < SUB - sh
                c_sh = jnp.where(m, pltpu.roll(cv, SUB - sh, 0), 1.0)
                b_sh = jnp.where(m, pltpu.roll(bv, SUB - sh, 0), 0.0)
                bv = cv * b_sh + bv
                cv = cv * c_sh
            hv = cv * car + bv
            dh_buf[pl.ds(i0, SUB), :] = hv
            return jnp.broadcast_to(hv[0:1, :], (SUB, lw))

        carry[...] = lax.fori_loop(0, nch, step, carry[...])

        for h in range(N_HEADS):
            cols = slice(h * HEAD, (h + 1) * HEAD)
            lam_h = lam_ref[:, cols]
            neg_c_sp = -LRU_C * _softplus_neg(lam_h)
            xc = xc_buf[:, cols]
            r = r_buf[:, cols]
            i = i_buf[:, cols]
            a = a_ext[pl.ds(0, tm), cols]
            mult = _lru_mult(neg_c_sp * r, a)
            dh = dh_buf[:, cols]
            dhx = dh * xc
            di = dhx * mult
            h_prev = _rows_back(hl_ext[:, cols], 1)[SUB:SUB + tm]
            dlog_a = (dh * h_prev) * a - (dhx * i) * ((a * a) / mult)
            pack_ref[pl.ds(R["dlam"][0] + h, 1), :] += _colsum(dlog_a * r) * (LRU_C * _sigmoid(-lam_h))
            dpa = (dlog_a * neg_c_sp) * (r * (1.0 - r))
            dpx = di * (i * (1.0 - i))
            dpa16 = dpa.astype(_MM)
            dpx16 = dpx.astype(_MM)
            xc16 = xc.astype(_MM)
            dxc = dh * (mult * i) + _dot_nt(dpa16, wa_ref[h]) + _dot_nt(dpx16, wx_ref[h])
            pack_ref[pl.ds(R["dwa"][0] + h * HEAD, HEAD), :] += _dot_tn(xc16, dpa16)
            pack_ref[pl.ds(R["dwx"][0] + h * HEAD, HEAD), :] += _dot_tn(xc16, dpx16)
            pack_ref[pl.ds(R["dba"][0] + h, 1), :] += _colsum(dpa)
            pack_ref[pl.ds(R["dbx"][0] + h, 1), :] += _colsum(dpx)
            pack_ref[pl.ds(R["dcb"][0] + h, 1), :] += _colsum(dxc)
            dxc_ext[pl.ds(0, tm), cols] = dxc

        for h in range(N_HEADS):
            cols = slice(h * HEAD, (h + 1) * HEAD)
            dxc_all = dxc_ext[:, cols]
            xa = xa_ext[pl.ds(HALO, tm), cols]
            dxa = None
            for k in range(4):
                dxc_k = _rows_ahead(dxc_all, 3 - k)[0:tm]
                pack_ref[pl.ds(R["dcw"][0] + SUB * k + h, 1), :] += _colsum(dxc_k * xa)
                term = cw_ref[pl.ds(k, 1), cols] * dxc_k
                dxa = term if dxa is None else dxa + term
            dz_ref[:, cols] = dxa.astype(dz_ref.dtype)

        xb_ext[pl.ds(0, HALO), :] = jnp.where(has_prev, zh_ref[:, 2 * lw:2 * lw + pwid], 0.0)
        xb_ext[pl.ds(HALO, tm), :] = zq_ref[:, 2 * lw:2 * lw + pwid]
        pos = jr * tm + lax.broadcasted_iota(jnp.int32, (tm, HEAD), 0)
        for g, k in enumerate(POOL_WINDOWS):
            cols = slice(g * HEAD, (g + 1) * HEAD)
            diff, cnt = _pool_diff(xb_ext, g, k, pos, tm)
            diff16 = diff.astype(_MM)
            yp = _dot(diff16, pw_ref[g])
            sc = ps_ref[:, cols]
            gb = zq_ref[:, 2 * lw + pwid + g * HEAD:2 * lw + pwid + (g + 1) * HEAD]
            sgb = _sigmoid(gb)
            dybv = dyb_ref[:, cols]
            dy_pool = dybv * (gb * sgb)
            dgb = (dybv * (yp * sc)) * (sgb * (1.0 + gb * (1.0 - sgb)))
            pack_ref[pl.ds(R["dps"][0] + g, 1), :] += _colsum(dy_pool * yp)
            dyp16 = (dy_pool * sc).astype(_MM)
            pack_ref[pl.ds(R["dpw"][0] + g * HEAD, HEAD), :] += _dot_tn(diff16, dyp16)
            ddiff = _dot_nt(dyp16, pw_ref[g])
            q_ext[pl.ds(0, tm), cols] = ddiff / cnt
            dxb = _window_sum(q_ext[:, cols], k, _rows_ahead)[0:tm] - ddiff
            dz_ref[:, 2 * lw + g * HEAD:2 * lw + (g + 1) * HEAD] = dxb.astype(dz_ref.dtype)
            dz_ref[:, 2 * lw + pwid + g * HEAD:2 * lw + pwid + (g + 1) * HEAD] = dgb.astype(dz_ref.dtype)

        a_ext[pl.ds(tm, SUB), :] = a_ext[pl.ds(0, SUB), :]
        dxc_ext[pl.ds(tm, HALO), :] = dxc_ext[pl.ds(0, HALO), :]
        q_ext[pl.ds(tm, HALO), :] = q_ext[pl.ds(0, HALO), :]

        @pl.when((b == nb - 1) & (j == nt - 1))
        def _():
            for cp in _sibling_copies(sib_src, sib_out, send_sems, recv_sems):
                cp.wait()

    rev = lambda w: pl.BlockSpec((None, tm, w), lambda b, j: (b, nt - 1 - j, 0))
    prev_rows = lambda rows, w: pl.BlockSpec(
        (None, rows, w), lambda b, j: (b, jnp.maximum((nt - 1 - j) * (tm // rows) - 1, 0), 0))
    return pl.pallas_call(
        body, name="bwd_seq", grid=(nb, nt),
        in_specs=[rev(zw), prev_rows(HALO, zw), rev(lw), prev_rows(SUB, lw), rev(lw), rev(pwid), rev(nc - zw),
                  _const_spec(cw.shape), _const_spec(cb.shape), _const_spec(wa.shape), _const_spec(ba.shape),
                  _const_spec(wx.shape), _const_spec(bx.shape), _const_spec(lam.shape), _const_spec(pw.shape),
                  _const_spec(ps.shape)] + [HBM_SPEC] * ns,
        out_specs=(rev(nc), pl.BlockSpec((SEQ_PACK_ROWS, HEAD), lambda b, j: (0, 0))) + (HBM_SPEC,) * ns,
        out_shape=(jax.ShapeDtypeStruct((nb, s, nc), _MM), jax.ShapeDtypeStruct((SEQ_PACK_ROWS, HEAD), F32))
        + tuple(jax.ShapeDtypeStruct((4,) + g.shape[1:], g.dtype) for g in sib_grads),
        scratch_shapes=[pltpu.VMEM((tm + HALO, lw), F32), pltpu.VMEM((tm + HALO, pwid), F32),
                        pltpu.VMEM((tm + SUB, lw), F32), pltpu.VMEM((tm, lw), F32), pltpu.VMEM((tm + SUB, lw), F32),
                        pltpu.VMEM((tm, lw), F32), pltpu.VMEM((tm, lw), F32), pltpu.VMEM((tm, lw), F32),
                        pltpu.VMEM((tm, lw), F32), pltpu.VMEM((tm + HALO, lw), F32),
                        pltpu.VMEM((tm + HALO, pwid), F32), pltpu.VMEM((SUB, lw), F32),
                        pltpu.SemaphoreType.DMA((ns, 4)), pltpu.SemaphoreType.DMA((ns, 4))],
        compiler_params=pltpu.CompilerParams(dimension_semantics=("arbitrary", "arbitrary"), vmem_limit_bytes=VMEM_LIMIT),
    )(z, z, hl, hl, dya, dyb, dm, cw, cb, wa, ba, wx, bx, lam, pw, ps, *sib_grads)


def _bwd_win(ht, dz, chip_srcs, small):
    nb, d, s = ht.shape
    nc = dz.shape[2]
    sec = nc // N_DEV
    n_chips = N_DEV // 2
    nm = len(chip_srcs)

    def body(ht_ref, dz_ref, *rest):
        chip_src, small_src = rest[:nm], rest[nm]
        gwin_ref, recv_ref = rest[nm + 1:nm + 3]
        chip_out, small_out = rest[nm + 3:2 * nm + 3], rest[2 * nm + 3]
        acc, local_sems, sib_send, sib_recv, chip_send, chip_recv, small_send, small_recv = rest[2 * nm + 4:]
        q = pl.program_id(0)
        b = pl.program_id(1)
        slot = q % 2
        x, y, c = _my_pos()

        def half(sl, core):
            return acc.at[sl, :, pl.ds(pl.multiple_of(core * sec, HEAD), sec)]

        def local_copy(qq, sl):
            return pltpu.make_async_copy(half(sl, c), gwin_ref.at[2 * qq + c], local_sems.at[sl])

        def sib_copy(qq, sl):
            return pltpu.make_async_remote_copy(
                src_ref=half(sl, 1 - c), dst_ref=recv_ref.at[qq], send_sem=sib_send.at[qq],
                recv_sem=sib_recv.at[qq], device_id=(x, y, 1 - c), device_id_type=MESH)

        def others():
            return (_chip_copies(chip_src, chip_out, chip_send, chip_recv)
                    + _sibling_copies([small_src], [small_out], small_send, small_recv))

        @pl.when((q == 0) & (b == 0))
        def _():
            for cp in others():
                cp.start()

        @pl.when((q >= 2) & (b == 0))
        def _():
            local_copy(q - 2, slot).wait()
            sib_copy(q - 2, slot).wait_send()

        part = _dot(ht_ref[b], dz_ref[...])

        @pl.when(b == 0)
        def _():
            acc[slot] = part

        @pl.when(b != 0)
        def _():
            acc[slot] += part

        @pl.when(b == nb - 1)
        def _():
            local_copy(q, slot).start()
            sib_copy(q, slot).start()

        @pl.when((q == n_chips - 1) & (b == nb - 1))
        def _():
            for qq in (n_chips - 2, n_chips - 1):
                local_copy(qq, qq % 2).wait()
                sib_copy(qq, qq % 2).wait_send()
            for qq in range(n_chips):
                sib_copy(qq, 0).wait_recv()
            for cp in others():
                cp.wait()

    return pl.pallas_call(
        body, name="bwd_win", grid=(n_chips, nb),
        in_specs=[_const_spec(ht.shape), pl.BlockSpec((None, s, 2 * sec), lambda q, b: (b, 0, q))]
        + [HBM_SPEC] * (nm + 1),
        out_specs=(HBM_SPEC,) * (nm + 3),
        out_shape=(jax.ShapeDtypeStruct((N_DEV, d, sec), F32), jax.ShapeDtypeStruct((n_chips, d, sec), F32))
        + tuple(jax.ShapeDtypeStruct((3,) + g.shape[1:], g.dtype) for g in chip_srcs)
        + (jax.ShapeDtypeStruct((4,) + small.shape[1:], small.dtype),),
        scratch_shapes=[pltpu.VMEM((2, d, 2 * sec), F32), pltpu.SemaphoreType.DMA((2,)),
                        pltpu.SemaphoreType.DMA((N_DEV // 2,)), pltpu.SemaphoreType.DMA((N_DEV // 2,)),
                        pltpu.SemaphoreType.DMA((nm, 3)), pltpu.SemaphoreType.DMA((nm, 3)),
                        pltpu.SemaphoreType.DMA((1, 4)), pltpu.SemaphoreType.DMA((1, 4))],
        compiler_params=pltpu.CompilerParams(dimension_semantics=("arbitrary", "arbitrary"), vmem_limit_bytes=VMEM_LIMIT),
    )(ht, dz, *chip_srcs, small)


def _bwd_dx(dz, x, dx1, win, g0, chip_srcs, tm):
    t, d = x.shape
    nc = win.shape[1]
    n = t // tm
    nm = len(chip_srcs)

    def body(dz_ref, x_ref, dx1_ref, win_ref, g0_ref, *rest):
        chip_src = rest[:nm]
        gx_ref, pack_ref = rest[nm:nm + 2]
        chip_out = rest[nm + 2:2 * nm + 2]
        chip_send, chip_recv = rest[2 * nm + 2:]
        i = pl.program_id(0)

        @pl.when(i == 0)
        def _():
            pack_ref[...] = jnp.zeros_like(pack_ref)
            for cp in _chip_copies(chip_src, chip_out, chip_send, chip_recv):
                cp.start()

        xv = x_ref[...]
        r0 = _rsqrt_mean_sq(xv)
        xh = xv * r0
        dh = _dot_nt(dz_ref[...], win_ref[...])
        _store_chunks(pack_ref, 0, _colsum(dh * xh))
        dxh = dh * g0_ref[...]
        gx_ref[...] = dx1_ref[...] + r0 * (dxh - xh * jnp.mean(dxh * xh, axis=-1, keepdims=True))

        @pl.when(i == n - 1)
        def _():
            for cp in _chip_copies(chip_src, chip_out, chip_send, chip_recv):
                cp.wait()

    tile = lambda w: pl.BlockSpec((tm, w), lambda i: (i, 0))
    return pl.pallas_call(
        body, name="bwd_dx", grid=(n,),
        in_specs=[tile(nc), tile(d), tile(d), _const_spec(win.shape), _const_spec(g0.shape)] + [HBM_SPEC] * nm,
        out_specs=(tile(d), pl.BlockSpec((INP_PACK_ROWS, HEAD), lambda i: (0, 0))) + (HBM_SPEC,) * nm,
        out_shape=(jax.ShapeDtypeStruct((t, d), F32), jax.ShapeDtypeStruct((INP_PACK_ROWS, HEAD), F32))
        + tuple(jax.ShapeDtypeStruct((3,) + g.shape[1:], g.dtype) for g in chip_srcs),
        scratch_shapes=[pltpu.SemaphoreType.DMA((nm, 3)), pltpu.SemaphoreType.DMA((nm, 3))],
        compiler_params=pltpu.CompilerParams(dimension_semantics=("arbitrary",), vmem_limit_bytes=VMEM_LIMIT),
    )(dz, x, dx1, win, g0, *chip_srcs)


def _all_gather(shards, full_shapes, slicers, name):
    nw = len(shards)

    def body(*refs):
        start, forward, finish = _gather_phases(refs[:nw], refs[nw:2 * nw], slicers, *refs[2 * nw:])
        start()
        forward()
        finish()

    return pl.pallas_call(
        body, name=name, in_specs=[HBM_SPEC] * nw, out_specs=(HBM_SPEC,) * nw,
        out_shape=tuple(jax.ShapeDtypeStruct(shp, a.dtype) for shp, a in zip(full_shapes, shards)),
        scratch_shapes=_gather_sems(nw),
    )(*shards)


ROW_BLOCKS = 8


def _chip_sums(core_idx, grads, recvs, dtypes, name):
    nw = len(grads)

    def body(idx_ref, *refs):
        gs, rs, outs = refs[:nw], refs[nw:2 * nw], refs[2 * nw:]
        for g, r, o in zip(gs, rs, outs):
            o[...] = (g[...] + r[...]).astype(o.dtype)

    def blk(g):
        return (None,) + g.shape[1:]

    return pl.pallas_call(
        body, name=name,
        grid_spec=pltpu.PrefetchScalarGridSpec(
            num_scalar_prefetch=1, grid=(4,),
            in_specs=[pl.BlockSpec(blk(g), lambda q, s: (2 * q + s[0], 0, 0)) for g in grads]
            + [pl.BlockSpec(blk(g), lambda q, s: (q, 0, 0)) for g in grads],
            out_specs=tuple(pl.BlockSpec(blk(g), lambda q, s: (q, 0, 0)) for g in grads)),
        out_shape=tuple(jax.ShapeDtypeStruct((4,) + g.shape[1:], dt) for g, dt in zip(grads, dtypes)),
    )(core_idx, *grads, *recvs)


def _adamw(w, g, m, v):
    m = ADAM_B1 * m + (1.0 - ADAM_B1) * g
    v = ADAM_B2 * v + (1.0 - ADAM_B2) * (g * g)
    m_hat = m / (1.0 - ADAM_B1 ** ADAM_STEP)
    v_hat = v / (1.0 - ADAM_B2 ** ADAM_STEP)
    delta = -ADAM_LR * (m_hat / (jnp.sqrt(v_hat) + ADAM_EPS) + ADAM_WD * w)
    return delta, m, v


def _adam_sections(sec_idx, grads, recv_sib, recv_chips, wmv):
    nw = len(wmv)

    def body(idx_ref, *refs):
        gs, rs, cs = refs[:nw + 1], refs[nw + 1:2 * nw + 2], refs[2 * nw + 2:3 * nw + 3]
        params = refs[3 * nw + 3:6 * nw + 3]
        outs = refs[6 * nw + 3:]
        for w in range(nw + 1):
            g = gs[w][...] + rs[w][...]
            for k in range(3):
                g = g + cs[w][k].astype(F32)
            if w == nw:
                outs[4 * nw][...] = g
                continue
            wv, mv, vv = (params[3 * w + t][...] for t in range(3))
            delta, m_new, v_new = _adamw(wv, g, mv, vv)
            outs[4 * w][...] = g
            outs[4 * w + 1][...] = delta
            outs[4 * w + 2][...] = m_new
            outs[4 * w + 3][...] = v_new

    def rb(g):
        return g.shape[1] // ROW_BLOCKS

    in_specs = [pl.BlockSpec((None, rb(g), g.shape[2]), lambda i, s: (s[0], i, 0)) for g in grads]
    in_specs += [pl.BlockSpec((None, rb(g), g.shape[2]), lambda i, s: (s[1], i, 0)) for g in grads]
    in_specs += [pl.BlockSpec((3, rb(g), g.shape[2]), lambda i, s: (0, i, 0)) for g in grads]
    sec = lambda g: pl.BlockSpec((rb(g), g.shape[2]), lambda i, s: (i, 0))
    for w in range(nw):
        in_specs += [sec(grads[w])] * 3
    out_specs, out_shape = [], []
    for g in grads[:nw]:
        out_specs += [sec(g)] * 4
        out_shape += [jax.ShapeDtypeStruct(g.shape[1:], F32)] * 4
    out_specs.append(sec(grads[nw]))
    out_shape.append(jax.ShapeDtypeStruct(grads[nw].shape[1:], F32))
    flat = [a for t in wmv for a in t]
    outs = pl.pallas_call(
        body, name="adam_sections",
        grid_spec=pltpu.PrefetchScalarGridSpec(num_scalar_prefetch=1, grid=(ROW_BLOCKS,), in_specs=in_specs,
                                               out_specs=tuple(out_specs)),
        out_shape=tuple(out_shape),
    )(sec_idx, *grads, *recv_sib, *recv_chips, *flat)
    return [outs[4 * w:4 * w + 4] for w in range(nw)], outs[4 * nw]


def _adam_small(sec_idx, gpack, parts, parts_wmv, vec_params, mat_params, conv_wmv):
    items = [(0, parts_wmv, "parts")]
    items += [(r0, t, "vec") for r0, t in vec_params] + [(r0, t, "mat") for r0, t in mat_params]
    items.append((_SEQ_ROWS["dcw"][0], conv_wmv, "conv"))

    def body(idx_ref, g_ref, parts_ref, *refs):
        ins, outs = refs[:3 * len(items)], refs[3 * len(items):]
        for n, (r0, _, kind) in enumerate(items):
            w_ref, m_ref, v_ref = ins[3 * n:3 * n + 3]
            o = outs[4 * n:4 * n + 4]
            if kind == "mat":
                g = g_ref[pl.ds(r0, w_ref.shape[0]), :]
                res = (g,) + _adamw(w_ref[...], g, m_ref[...], v_ref[...])
                for ref, val in zip(o, res):
                    ref[...] = val
            elif kind in ("vec", "parts"):
                for h in range(w_ref.shape[1] // HEAD):
                    cols = slice(h * HEAD, (h + 1) * HEAD)
                    if kind == "vec":
                        g = g_ref[pl.ds(r0 + h, 1), :]
                    else:
                        g = parts_ref[pl.ds(h, 1), :]
                        for dev in range(1, N_DEV):
                            g = g + parts_ref[pl.ds(SUB * dev + h, 1), :]
                    res = (g,) + _adamw(w_ref[:, cols], g, m_ref[:, cols], v_ref[:, cols])
                    for ref, val in zip(o, res):
                        ref[:, cols] = val
            else:
                rows = lax.broadcasted_iota(jnp.int32, (SUB, HEAD), 0)
                for k in range(4):
                    blk = g_ref[pl.ds(r0 + SUB * k, SUB), :]
                    g = jnp.sum(jnp.where(rows == idx_ref[0], blk, 0.0), axis=0, keepdims=True)
                    row = pl.ds(k, 1)
                    res = (g,) + _adamw(w_ref[row, :], g, m_ref[row, :], v_ref[row, :])
                    for ref, val in zip(o, res):
                        ref[row, :] = val
        outs[4 * len(items)][...] = g_ref[pl.ds(_OFF_MID + MID_LOSS_ROW, 1), :]

    flat = [a for _, t, _ in items for a in t]
    full = lambda a: pl.BlockSpec(a.shape, lambda i, s: (0,) * a.ndim)
    out_specs, out_shape = [], []
    for _, t, _ in items:
        out_specs += [full(t[0])] * 4
        out_shape += [jax.ShapeDtypeStruct(t[0].shape, F32)] * 4
    loss_row = jax.ShapeDtypeStruct((1, HEAD), F32)
    out_specs.append(full(loss_row))
    out_shape.append(loss_row)
    outs = pl.pallas_call(
        body, name="adam_small",
        grid_spec=pltpu.PrefetchScalarGridSpec(num_scalar_prefetch=1, grid=(1,),
                                               in_specs=[full(gpack), full(parts)] + [full(a) for a in flat],
                                               out_specs=tuple(out_specs)),
        out_shape=tuple(out_shape),
    )(sec_idx, gpack, parts, *flat)
    return [outs[4 * n:4 * n + 4] for n in range(len(items))], outs[4 * len(items)]


def _col_slicer(width):
    return lambda ref, idx: ref.at[:, pl.ds(pl.multiple_of(idx * width, HEAD), width)]


def _row_slicer(rows):
    return lambda ref, idx: ref.at[pl.ds(pl.multiple_of(idx * rows, SUB), rows), :]


def kernel(x, p, norm_g, w_in, conv_w, conv_b, lru_w_a, lru_b_a, lru_w_x, lru_b_x, lru_lambda, pool_w, pool_scale, w_proj_lru, w_proj_pool, w_out, ple_norm_g, w_ple_gate, w_ple_proj, final_g, loss_target, m_norm_g, m_w_in, m_conv_w, m_conv_b, m_lru_w_a, m_lru_b_a, m_lru_w_x, m_lru_b_x, m_lru_lambda, m_pool_w, m_pool_scale, m_w_proj_lru, m_w_proj_pool, m_w_out, m_ple_norm_g, m_w_ple_gate, m_w_ple_proj, m_final_g, v_norm_g, v_w_in, v_conv_w, v_conv_b, v_lru_w_a, v_lru_b_a, v_lru_w_x, v_lru_b_x, v_lru_lambda, v_pool_w, v_pool_scale, v_w_proj_lru, v_w_proj_pool, v_w_out, v_ple_norm_g, v_w_ple_gate, v_w_ple_proj, v_final_g):
    nb, s, d = x.shape
    t = nb * s
    tm = min(TILE_M, s)
    sec_idx = (4 * lax.axis_index("x") + 2 * lax.axis_index("y") + lax.axis_index("c")).astype(jnp.int32)
    chip_idx = (2 * lax.axis_index("x") + lax.axis_index("y")).astype(jnp.int32)
    core_idx = lax.axis_index("c").astype(jnp.int32)

    win, cw = _all_gather(
        [w_in[0].astype(_MM), conv_w[0]], [(d, N_DEV * w_in.shape[2]), (conv_w.shape[1], N_DEV * conv_w.shape[2])],
        [_col_slicer(w_in.shape[2]), _col_slicer(conv_w.shape[2])], "gather_w_in")
    late_shards = [w_proj_lru[0].astype(_MM), w_proj_pool[0].astype(_MM), w_out[0].astype(_MM),
                   w_ple_gate[0].astype(_MM), w_ple_proj[0].astype(_MM)]
    late_shapes = [(N_DEV * w_proj_lru.shape[1], d), (w_proj_pool.shape[1], N_DEV * w_proj_pool.shape[2]),
                   (N_DEV * w_out.shape[1], d), (N_DEV * w_ple_gate.shape[1], d),
                   (w_ple_proj.shape[1], N_DEV * w_ple_proj.shape[2])]
    late_slicers = [_row_slicer(w_proj_lru.shape[1]), _col_slicer(w_proj_pool.shape[2]), _row_slicer(w_out.shape[1]),
                    _row_slicer(w_ple_gate.shape[1]), _col_slicer(w_ple_proj.shape[2])]

    lw = cw.shape[1]
    pwid = pool_scale.shape[1]
    wa = lru_w_a[0].astype(_MM)
    wx = lru_w_x[0].astype(_MM)
    pw = pool_w[0].astype(_MM)
    ba, bx = lru_b_a[0], lru_b_x[0]
    gf = final_g.reshape(1, d)
    rows_of = lambda g: g.reshape(N_DEV, g.shape[0] // N_DEV, g.shape[1])
    core = core_idx.reshape(1)

    z, hl, ya, yb, ht, wpl, wpp, wout, wpg, wpe = _fwd_seq(
        x, norm_g, win, cw, conv_b, wa, ba, wx, bx, lru_lambda, pw, pool_scale, late_shards, late_shapes, late_slicers, tm)
    x2d = x.reshape(t, d)
    (dya, dyb, dm, dx1, g_pl, g_pp, g_out, g_pg, g_pe, pack_mid) = _mid(
        ya.reshape(t, lw), yb.reshape(t, pwid), z.reshape(t, -1), x2d, p[0].reshape(t, -1), loss_target.reshape(t, d),
        wpl, wpp, wout, wpg, wpe, ple_norm_g, gf, tm)
    grads_mid = [rows_of(g_pl), g_pp, rows_of(g_out), rows_of(g_pg), g_pe]
    seq_out = _bwd_seq(z, hl, dya.reshape(nb, s, lw), dyb.reshape(nb, s, pwid), dm.reshape(nb, s, -1), cw, conv_b,
                       wa, ba, wx, bx, lru_lambda, pw, pool_scale, grads_mid, tm)
    dz, pack_seq, sib_mid = seq_out[0], seq_out[1], list(seq_out[2:])
    sums_mid = _chip_sums(core, grads_mid, sib_mid, [_WIRE] * len(grads_mid), "chip_sums_mid")
    small = jnp.concatenate(
        [pack_seq, pack_mid, jnp.zeros((SMALL_ROWS - _OFF_MID - MID_PACK_ROWS, HEAD), F32)], axis=0
    ).reshape(N_DEV, SMALL_SEC, HEAD)
    win_out = _bwd_win(ht, dz, sums_mid, small)
    g_in, sib_in, chips_mid, sib_small = win_out[0], win_out[1], list(win_out[2:-1]), win_out[-1]
    sums_in = _chip_sums(core, [g_in, small], [sib_in, sib_small], [_WIRE, F32], "chip_sums_in")
    dx_out = _bwd_dx(dz.reshape(t, -1), x2d, dx1, win, norm_g, sums_in, tm)
    grad_x, pack_inp, chips_in, chips_small = dx_out

    grads = [g_in] + grads_mid + [small]
    recv_sib = [sib_in] + sib_mid + [sib_small]
    recv_chips = [chips_in] + chips_mid + [chips_small]
    wmv = [(w_in[0], m_w_in[0], v_w_in[0]), (w_proj_lru[0], m_w_proj_lru[0], v_w_proj_lru[0]),
           (w_proj_pool[0], m_w_proj_pool[0], v_w_proj_pool[0]), (w_out[0], m_w_out[0], v_w_out[0]),
           (w_ple_gate[0], m_w_ple_gate[0], v_w_ple_gate[0]), (w_ple_proj[0], m_w_ple_proj[0], v_w_ple_proj[0])]
    big, small_sec = _adam_sections(jnp.stack([sec_idx, chip_idx]), grads, recv_sib, recv_chips, wmv)
    gsmall, g0_parts = _all_gather([small_sec, pack_inp], [(SMALL_ROWS, HEAD), (N_DEV * INP_PACK_ROWS, HEAD)],
                                   [_row_slicer(SMALL_SEC), _row_slicer(INP_PACK_ROWS)], "gather_small")

    R = _SEQ_ROWS
    vec = lambda r0, *t: (r0, tuple(a.reshape(1, -1) for a in t))
    mat = lambda r0, *t: (r0, tuple(a.reshape(-1, HEAD) for a in t))
    norm_wmv = vec(0, norm_g, m_norm_g, v_norm_g)[1]
    vec_params = [vec(R["dcb"][0], conv_b, m_conv_b, v_conv_b),
                  vec(R["dlam"][0], lru_lambda, m_lru_lambda, v_lru_lambda),
                  vec(R["dps"][0], pool_scale, m_pool_scale, v_pool_scale),
                  vec(_OFF_MID, ple_norm_g, m_ple_norm_g, v_ple_norm_g),
                  vec(_OFF_MID + SUB, final_g, m_final_g, v_final_g)]
    mat_params = [mat(R["dwa"][0], lru_w_a, m_lru_w_a, v_lru_w_a), mat(R["dba"][0], lru_b_a, m_lru_b_a, v_lru_b_a),
                  mat(R["dwx"][0], lru_w_x, m_lru_w_x, v_lru_w_x), mat(R["dbx"][0], lru_b_x, m_lru_b_x, v_lru_b_x),
                  mat(R["dpw"][0], pool_w, m_pool_w, v_pool_w)]
    conv_wmv = (conv_w[0], m_conv_w[0], v_conv_w[0])
    small_out, loss_row = _adam_small(sec_idx.reshape(1), gsmall, g0_parts, norm_wmv, vec_params, mat_params, conv_wmv)
    loss = loss_row[0, 0]

    res = {}
    names_small = ["norm_g", "conv_b", "lru_lambda", "pool_scale", "ple_norm_g", "final_g",
                   "lru_w_a", "lru_b_a", "lru_w_x", "lru_b_x", "pool_w", "conv_w"]
    shapes = {"norm_g": norm_g, "conv_b": conv_b, "lru_lambda": lru_lambda, "pool_scale": pool_scale,
              "ple_norm_g": ple_norm_g, "final_g": final_g, "lru_w_a": lru_w_a, "lru_b_a": lru_b_a, "lru_w_x": lru_w_x,
              "lru_b_x": lru_b_x, "pool_w": pool_w, "conv_w": conv_w}
    for name, quad in zip(names_small, small_out):
        res[name] = [a.reshape(shapes[name].shape) for a in quad]
    names_big = ["w_in", "w_proj_lru", "w_proj_pool", "w_out", "w_ple_gate", "w_ple_proj"]
    for name, quad, (w, _, _) in zip(names_big, big, wmv):
        res[name] = [a.reshape((1,) + w.shape) for a in quad]
    order = ["norm_g", "w_in", "conv_w", "conv_b", "lru_w_a", "lru_b_a", "lru_w_x", "lru_b_x", "lru_lambda", "pool_w",
             "pool_scale", "w_proj_lru", "w_proj_pool", "w_out", "ple_norm_g", "w_ple_gate", "w_ple_proj", "final_g"]
    out = [loss, grad_x.reshape(nb, s, d)]
    for kind in range(4):
        out += [res[name][kind] for name in order]
    return tuple(out)
```

```python
import jax
import jax.numpy as jnp
from jax import lax
from jax.experimental import pallas as pl
from jax.experimental.pallas import tpu as pltpu

F32 = jnp.float32
_MM = jnp.bfloat16
_WIRE = jnp.bfloat16

EPS = 1e-6
LRU_C = 8.0
POOL_WINDOWS = (2, 4, 8, 16)
N_HEADS = 8
HEAD = 128
HALO = 16
SUB = 8

ADAM_LR = 0.001
ADAM_B1 = 0.9
ADAM_B2 = 0.999
ADAM_EPS = 1e-08
ADAM_WD = 0.01
ADAM_STEP = 10

N_DEV = 8
MESH = pl.DeviceIdType.MESH
VMEM_LIMIT = 60 * 1024 * 1024
TILE_M = 256
TILE_M_FWD = 512
TILE_M_DX = 512

_SEQ_ROWS = {"dwa": (0, 1024), "dwx": (1024, 1024), "dpw": (2048, 512), "dba": (2560, 8), "dbx": (2568, 8),
             "dcb": (2576, 8), "dlam": (2584, 8), "dcw": (2592, 32), "dps": (2624, 8)}
SEQ_PACK_ROWS = 2632
MID_PACK_ROWS = 24
MID_LOSS_ROW = 16
INP_PACK_ROWS = 8
SMALL_ROWS = 3072
SMALL_SEC = SMALL_ROWS // N_DEV
_OFF_MID = SEQ_PACK_ROWS


def _dot(a, b):
    return jnp.dot(a, b, preferred_element_type=F32)


def _dot_nt(a, b):
    return lax.dot_general(a, b, (((1,), (1,)), ((), ())), preferred_element_type=F32)


def _dot_tn(a, b):
    return lax.dot_general(a, b, (((0,), (0,)), ((), ())), preferred_element_type=F32)


def _sigmoid(v):
    return 0.5 * jnp.tanh(0.5 * v) + 0.5


def _lru_mult(log_a, a):
    return jnp.sqrt(-jnp.tanh(log_a) * (1.0 + a * a))


def _rows_back(v, n):
    return pltpu.roll(v, n, 0) if n else v


def _rows_ahead(v, n):
    return pltpu.roll(v, v.shape[0] - n, 0) if n else v


def _softplus_neg(lam):
    e = jnp.exp(-jnp.abs(lam))
    w = 1.0 + e
    l1p = jnp.where(w == 1.0, e, jnp.log(w) * (e / (w - 1.0)))
    return jnp.maximum(-lam, 0.0) + l1p


def _rsqrt_mean_sq(v):
    return lax.rsqrt(jnp.mean(v * v, axis=-1, keepdims=True) + EPS)


def _colsum(v):
    return jnp.sum(v, axis=0, keepdims=True)


def _const_spec(shape):
    nd = len(shape)
    return pl.BlockSpec(shape, lambda *_: (0,) * nd, pipeline_mode=pl.Buffered(1))


HBM_SPEC = pl.BlockSpec(memory_space=pl.ANY)


def _my_pos():
    return lax.axis_index("x"), lax.axis_index("y"), lax.axis_index("c")


def _other_chips(x, y):
    return [(1 - x, y), (x, 1 - y), (1 - x, 1 - y)]


def _gather_sems(nw):
    return [pltpu.SemaphoreType.DMA((nw, 7)), pltpu.SemaphoreType.DMA((nw, 7)), pltpu.SemaphoreType.DMA((nw,))]


def _gather_phases(srcs, outs, slicers, send_sems, recv_sems, local_sems):
    nw = len(srcs)
    x, y, c = _my_pos()
    me, sibling = (x, y, c), (x, y, 1 - c)
    chips = _other_chips(x, y)

    def part(w, pos):
        return slicers[w](outs[w], 4 * pos[0] + 2 * pos[1] + pos[2])

    def copy(w, k, block, to, src=None):
        return pltpu.make_async_remote_copy(
            src_ref=part(w, block) if src is None else src, dst_ref=part(w, block),
            send_sem=send_sems.at[w, k], recv_sem=recv_sems.at[w, k], device_id=to, device_id_type=MESH)

    def mine():
        return [pltpu.make_async_copy(srcs[w], part(w, me), local_sems.at[w]) for w in range(nw)]

    def first():
        out = []
        for w in range(nw):
            out.append(copy(w, 0, me, sibling, src=srcs[w]))
            out += [copy(w, 1 + j, me, (*chip, c), src=srcs[w]) for j, chip in enumerate(chips)]
        return out

    def passed():
        return [copy(w, 4 + j, (*chip, c), sibling) for j, chip in enumerate(chips) for w in range(nw)]

    def start():
        for cp in mine() + first():
            cp.start()

    def forward():
        for j, chip in enumerate(chips):
            for w in range(nw):
                copy(w, 1 + j, (*chip, c), me).wait_recv()
                copy(w, 4 + j, (*chip, c), sibling).start()

    def finish():
        for w in range(nw):
            copy(w, 0, sibling, me).wait_recv()
            for j, chip in enumerate(chips):
                copy(w, 4 + j, (*chip, 1 - c), me).wait_recv()
        for cp in first() + passed():
            cp.wait_send()
        for cp in mine():
            cp.wait()

    return start, forward, finish


def _sibling_copies(srcs, outs, send_sems, recv_sems):
    x, y, c = _my_pos()
    return [pltpu.make_async_remote_copy(
        src_ref=srcs[w].at[2 * q + (1 - c)], dst_ref=outs[w].at[q], send_sem=send_sems.at[w, q],
        recv_sem=recv_sems.at[w, q], device_id=(x, y, 1 - c), device_id_type=MESH)
        for w in range(len(srcs)) for q in range(4)]


def _chip_copies(srcs, outs, send_sems, recv_sems):
    x, y, c = _my_pos()
    return [pltpu.make_async_remote_copy(
        src_ref=srcs[w].at[2 * px + py], dst_ref=outs[w].at[k], send_sem=send_sems.at[w, k],
        recv_sem=recv_sems.at[w, k], device_id=(px, py, c), device_id_type=MESH)
        for w in range(len(srcs)) for k, (px, py) in enumerate(_other_chips(x, y))]


def _lru_gates(xc, h, wa_ref, ba_ref, wx_ref, bx_ref, neg_c_sp):
    xc16 = xc.astype(_MM)
    r = _sigmoid(_dot(xc16, wa_ref[h]) + ba_ref[pl.ds(h, 1), :])
    i = _sigmoid(_dot(xc16, wx_ref[h]) + bx_ref[pl.ds(h, 1), :])
    return r, i, neg_c_sp * r


def _conv_head(xa_ext, cw_ref, cb_ref, cols, tm):
    ext = xa_ext[:, cols]
    xc = cb_ref[:, cols] + cw_ref[pl.ds(3, 1), cols] * ext[HALO:HALO + tm]
    for k in range(3):
        xc = xc + cw_ref[pl.ds(k, 1), cols] * _rows_back(ext, 3 - k)[HALO:HALO + tm]
    return xc


def _window_sum(ext, k, shift):
    n = 1
    while n < k:
        ext = ext + shift(ext, n)
        n *= 2
    return ext


def _pool_diff(xb_ext, g, k, pos, tm):
    cols = slice(g * HEAD, (g + 1) * HEAD)
    ext = xb_ext[:, cols]
    ws = _window_sum(ext, k, _rows_back)[HALO:HALO + tm]
    cnt = jnp.minimum(pos + 1, k).astype(F32)
    return ws / cnt - ext[HALO:HALO + tm], cnt


def _fwd_seq(x, g0, win, cw, cb, wa, ba, wx, bx, lam, pw, ps, late_shards, late_shapes, late_slicers, tm):
    nb, s, d = x.shape
    nc = win.shape[1]
    lw = cw.shape[1]
    pwid = ps.shape[1]
    nt = s // tm
    nl = len(late_shards)
    n_steps = nb * nt

    def body(x_ref, g0_ref, win_ref, cw_ref, cb_ref, wa_ref, ba_ref, wx_ref, bx_ref, lam_ref, pw_ref, ps_ref, *rest):
        late_src, rest = rest[:nl], rest[nl:]
        z_ref, hl_ref, ya_ref, yb_ref, h_ref = rest[:5]
        late_out, rest = rest[5:5 + nl], rest[5 + nl:]
        xa_ext, xb_ext, a_buf, carry, send_sems, recv_sems, local_sems = rest
        j = pl.program_id(1)
        step_no = pl.program_id(0) * nt + j
        g_start, g_forward, g_finish = _gather_phases(late_src, late_out, late_slicers, send_sems, recv_sems, local_sems)
        pl.when(step_no == 0)(g_start)
        pl.when(step_no == n_steps // 2)(g_forward)

        @pl.when(j == 0)
        def _():
            xa_ext[pl.ds(0, HALO), :] = jnp.zeros((HALO, lw), F32)
            xb_ext[pl.ds(0, HALO), :] = jnp.zeros((HALO, pwid), F32)
            carry[...] = jnp.zeros_like(carry)

        xv = x_ref[...]
        h16 = ((xv * _rsqrt_mean_sq(xv)) * g0_ref[...]).astype(_MM)
        h_ref[...] = h16.T
        z_ref[...] = _dot(h16, win_ref[...])

        xa_ext[pl.ds(HALO, tm), :] = z_ref[:, 0:lw]
        for h in range(N_HEADS):
            cols = slice(h * HEAD, (h + 1) * HEAD)
            xc = _conv_head(xa_ext, cw_ref, cb_ref, cols, tm)
            neg_c_sp = -LRU_C * _softplus_neg(lam_ref[:, cols])
            _, i, log_a = _lru_gates(xc, h, wa_ref, ba_ref, wx_ref, bx_ref, neg_c_sp)
            a = jnp.exp(log_a)
            a_buf[:, cols] = a
            hl_ref[:, cols] = _lru_mult(log_a, a) * (i * xc)

        rows = lax.broadcasted_iota(jnp.int32, (SUB, lw), 0)

        def step(c, car):
            i0 = pl.multiple_of(c * SUB, SUB)
            av = a_buf[pl.ds(i0, SUB), :]
            bv = hl_ref[pl.ds(i0, SUB), :]
            for sh in (1, 2, 4):
                m = rows >= sh
                a_sh = jnp.where(m, pltpu.roll(av, sh, 0), 1.0)
                b_sh = jnp.where(m, pltpu.roll(bv, sh, 0), 0.0)
                bv = av * b_sh + bv
                av = av * a_sh
            hv = av * car + bv
            hl_ref[pl.ds(i0, SUB), :] = hv
            return jnp.broadcast_to(hv[SUB - 1:SUB, :], (SUB, lw))

        carry[...] = lax.fori_loop(0, tm // SUB, step, carry[...])

        ga = z_ref[:, lw:2 * lw]
        ya_ref[...] = (hl_ref[...] * (ga * _sigmoid(ga))).astype(ya_ref.dtype)

        xb_ext[pl.ds(HALO, tm), :] = z_ref[:, 2 * lw:2 * lw + pwid]
        pos = j * tm + lax.broadcasted_iota(jnp.int32, (tm, HEAD), 0)
        for g, k in enumerate(POOL_WINDOWS):
            cols = slice(g * HEAD, (g + 1) * HEAD)
            diff, _ = _pool_diff(xb_ext, g, k, pos, tm)
            yp = _dot(diff.astype(_MM), pw_ref[g])
            gb = z_ref[:, 2 * lw + pwid + g * HEAD:2 * lw + pwid + (g + 1) * HEAD]
            yb_ref[:, cols] = ((yp * ps_ref[:, cols]) * (gb * _sigmoid(gb))).astype(yb_ref.dtype)

        xa_ext[pl.ds(0, HALO), :] = xa_ext[pl.ds(tm, HALO), :]
        xb_ext[pl.ds(0, HALO), :] = xb_ext[pl.ds(tm, HALO), :]
        pl.when(step_no == n_steps - 1)(g_finish)

    tile = lambda w: pl.BlockSpec((None, tm, w), lambda b, j: (b, j, 0))
    return pl.pallas_call(
        body, name="fwd_seq", grid=(nb, nt),
        in_specs=[tile(d), _const_spec(g0.shape), _const_spec(win.shape), _const_spec(cw.shape), _const_spec(cb.shape),
                  _const_spec(wa.shape), _const_spec(ba.shape), _const_spec(wx.shape), _const_spec(bx.shape),
                  _const_spec(lam.shape), _const_spec(pw.shape), _const_spec(ps.shape)] + [HBM_SPEC] * nl,
        out_specs=(tile(nc), tile(lw), tile(lw), tile(pwid), pl.BlockSpec((None, d, tm), lambda b, j: (b, 0, j)))
        + (HBM_SPEC,) * nl,
        out_shape=(jax.ShapeDtypeStruct((nb, s, nc), F32), jax.ShapeDtypeStruct((nb, s, lw), F32),
                   jax.ShapeDtypeStruct((nb, s, lw), _MM), jax.ShapeDtypeStruct((nb, s, pwid), _MM),
                   jax.ShapeDtypeStruct((nb, d, s), _MM))
        + tuple(jax.ShapeDtypeStruct(shp, a.dtype) for shp, a in zip(late_shapes, late_shards)),
        scratch_shapes=[pltpu.VMEM((tm + HALO, lw), F32), pltpu.VMEM((tm + HALO, pwid), F32),
                        pltpu.VMEM((tm, lw), F32), pltpu.VMEM((SUB, lw), F32)] + _gather_sems(nl),
        compiler_params=pltpu.CompilerParams(dimension_semantics=("arbitrary", "arbitrary"), vmem_limit_bytes=VMEM_LIMIT),
    )(x, g0, win, cw, cb, wa, ba, wx, bx, lam, pw, ps, *late_shards)


def _store_chunks(ref, row0, vec):
    for h in range(vec.shape[1] // HEAD):
        ref[pl.ds(row0 + h, 1), :] += vec[:, h * HEAD:(h + 1) * HEAD]


def _mid(ya, yb, z, x, p, tgt, wpl, wpp, wout, wpg, wpe, g1, gf, tm):
    t, d = x.shape
    lw = ya.shape[1]
    pwid = yb.shape[1]
    pdim = p.shape[1]
    n = t // tm
    sec_pp = wpp.shape[1] // N_DEV
    sec_pe = wpe.shape[1] // N_DEV

    def body(ya_ref, yb_ref, ma_ref, mb_ref, x_ref, p_ref, tgt_ref, wpl_ref, wpp_ref, wout_ref, wpg_ref, wpe_ref,
             g1_ref, gf_ref,
             dya_ref, dyb_ref, dm_ref, dx1_ref, gpl_ref, gpp_ref, gout_ref, gpg_ref, gpe_ref, pack_ref,
             acc_pl, acc_pp, acc_out, acc_pg, acc_pe):
        i = pl.program_id(0)

        @pl.when(i == 0)
        def _():
            for acc in (acc_pl, acc_pp, acc_out, acc_pg, acc_pe):
                acc[...] = jnp.zeros_like(acc)
            pack_ref[...] = jnp.zeros_like(pack_ref)

        ya16 = ya_ref[...]
        yb16 = yb_ref[...]
        br_a = _dot(ya16, wpl_ref[...])
        br_b = _dot(yb16, wpp_ref[...])
        sa = _sigmoid(ma_ref[...])
        sb = _sigmoid(mb_ref[...])
        mg16 = (sa * br_a + sb * br_b).astype(_MM)
        x1 = x_ref[...] + _dot(mg16, wout_ref[...])
        r1 = _rsqrt_mean_sq(x1)
        n1 = x1 * r1
        h116 = (n1 * g1_ref[...]).astype(_MM)
        gate = _sigmoid(_dot(h116, wpg_ref[...]))
        p16 = p_ref[...].astype(_MM)
        pe = _dot(p16, wpe_ref[...])
        x2 = x1 + gate * pe
        r2 = _rsqrt_mean_sq(x2)
        n2 = x2 * r2
        err = n2 * gf_ref[...] - tgt_ref[...]
        sq = jnp.sum(_colsum(err * err), axis=1, keepdims=True)
        pack_ref[pl.ds(MID_LOSS_ROW, 1), :] += jnp.broadcast_to(sq * (0.5 / d), (1, HEAD))

        dy = err * (1.0 / d)
        _store_chunks(pack_ref, SUB, _colsum(dy * n2))
        dn2 = dy * gf_ref[...]
        dx2 = r2 * (dn2 - n2 * jnp.mean(dn2 * n2, axis=-1, keepdims=True))
        dpe16 = (dx2 * gate).astype(_MM)
        dpg16 = ((dx2 * pe) * (gate * (1.0 - gate))).astype(_MM)
        acc_pe[...] += _dot_tn(p16, dpe16)
        acc_pg[...] += _dot_tn(h116, dpg16)
        dh1 = _dot_nt(dpg16, wpg_ref[...])
        _store_chunks(pack_ref, 0, _colsum(dh1 * n1))
        dn1 = dh1 * g1_ref[...]
        dx1 = dx2 + r1 * (dn1 - n1 * jnp.mean(dn1 * n1, axis=-1, keepdims=True))
        dx1_ref[...] = dx1
        dx116 = dx1.astype(_MM)
        acc_out[...] += _dot_tn(mg16, dx116)
        dmg = _dot_nt(dx116, wout_ref[...])
        da16 = (dmg * sa).astype(_MM)
        db16 = (dmg * sb).astype(_MM)
        dm_ref[:, 0:d] = ((dmg * br_a) * (sa * (1.0 - sa))).astype(dm_ref.dtype)
        dm_ref[:, d:2 * d] = ((dmg * br_b) * (sb * (1.0 - sb))).astype(dm_ref.dtype)
        acc_pl[...] += _dot_tn(ya16, da16)
        acc_pp[...] += _dot_tn(yb16, db16)
        dya_ref[...] = _dot_nt(da16, wpl_ref[...])
        dyb_ref[...] = _dot_nt(db16, wpp_ref[...])

        @pl.when(i == n - 1)
        def _():
            pltpu.sync_copy(acc_pl, gpl_ref)
            pltpu.sync_copy(acc_out, gout_ref)
            pltpu.sync_copy(acc_pg, gpg_ref)
            for k in range(N_DEV):
                pltpu.sync_copy(acc_pp.at[:, pl.ds(k * sec_pp, sec_pp)], gpp_ref.at[k])
                pltpu.sync_copy(acc_pe.at[:, pl.ds(k * sec_pe, sec_pe)], gpe_ref.at[k])

    tile = lambda w: pl.BlockSpec((tm, w), lambda i: (i, 0))
    zcol = lambda c: pl.BlockSpec((tm, d), lambda i: (i, c))
    any_spec = pl.BlockSpec(memory_space=pl.ANY)
    full = lambda shape: pl.BlockSpec(shape, lambda i: (0,) * len(shape))
    return pl.pallas_call(
        body, name="mid", grid=(n,),
        in_specs=[tile(lw), tile(pwid), zcol(3), zcol(4), tile(d), tile(pdim), tile(d),
                  _const_spec(wpl.shape), _const_spec(wpp.shape), _const_spec(wout.shape), _const_spec(wpg.shape),
                  _const_spec(wpe.shape), _const_spec(g1.shape), _const_spec(gf.shape)],
        out_specs=(tile(lw), tile(pwid), tile(2 * d), tile(d), any_spec, any_spec, any_spec, any_spec, any_spec,
                   full((MID_PACK_ROWS, HEAD))),
        out_shape=(jax.ShapeDtypeStruct((t, lw), F32), jax.ShapeDtypeStruct((t, pwid), F32),
                   jax.ShapeDtypeStruct((t, 2 * d), _MM), jax.ShapeDtypeStruct((t, d), F32),
                   jax.ShapeDtypeStruct(wpl.shape, F32), jax.ShapeDtypeStruct((N_DEV, wpp.shape[0], sec_pp), F32),
                   jax.ShapeDtypeStruct(wout.shape, F32), jax.ShapeDtypeStruct(wpg.shape, F32),
                   jax.ShapeDtypeStruct((N_DEV, wpe.shape[0], sec_pe), F32),
                   jax.ShapeDtypeStruct((MID_PACK_ROWS, HEAD), F32)),
        scratch_shapes=[pltpu.VMEM(wpl.shape, F32), pltpu.VMEM(wpp.shape, F32), pltpu.VMEM(wout.shape, F32),
                        pltpu.VMEM(wpg.shape, F32), pltpu.VMEM(wpe.shape, F32)],
        compiler_params=pltpu.CompilerParams(dimension_semantics=("arbitrary",), vmem_limit_bytes=VMEM_LIMIT),
    )(ya, yb, z, z, x, p, tgt, wpl, wpp, wout, wpg, wpe, g1, gf)


def _bwd_seq(z, hl, dya, dyb, dm, cw, cb, wa, ba, wx, bx, lam, pw, ps, sib_grads, tm):
    nb, s, nc = z.shape
    lw = cw.shape[1]
    pwid = ps.shape[1]
    nt = s // tm
    zw = 2 * lw + 2 * pwid
    R = _SEQ_ROWS
    ns = len(sib_grads)

    def body(zq_ref, zh_ref, hl_ref, hlh_ref, dya_ref, dyb_ref, dm_ref, cw_ref, cb_ref, wa_ref, ba_ref, wx_ref, bx_ref,
             lam_ref, pw_ref, ps_ref, *rest):
        sib_src, rest = rest[:ns], rest[ns:]
        dz_ref, pack_ref = rest[:2]
        sib_out, rest = rest[2:2 + ns], rest[2 + ns:]
        (xa_ext, xb_ext, a_ext, an_buf, hl_ext, dh_buf, xc_buf, r_buf, i_buf, dxc_ext, q_ext, carry,
         send_sems, recv_sems) = rest
        b = pl.program_id(0)
        j = pl.program_id(1)
        jr = nt - 1 - j
        has_prev = jr > 0

        @pl.when((b == 0) & (j == 0))
        def _():
            for cp in _sibling_copies(sib_src, sib_out, send_sems, recv_sems):
                cp.start()

        dz_ref[:, zw:nc] = dm_ref[...]

        @pl.when((b == 0) & (j == 0))
        def _():
            pack_ref[...] = jnp.zeros_like(pack_ref)

        @pl.when(j == 0)
        def _():
            a_ext[pl.ds(tm, SUB), :] = jnp.zeros((SUB, lw), F32)
            dxc_ext[pl.ds(tm, HALO), :] = jnp.zeros((HALO, lw), F32)
            q_ext[pl.ds(tm, HALO), :] = jnp.zeros((HALO, pwid), F32)
            carry[...] = jnp.zeros_like(carry)

        xa_ext[pl.ds(0, HALO), :] = jnp.where(has_prev, zh_ref[:, 0:lw], 0.0)
        xa_ext[pl.ds(HALO, tm), :] = zq_ref[:, 0:lw]
        hl_ext[pl.ds(0, SUB), :] = jnp.where(has_prev, hlh_ref[...], 0.0)
        hl_ext[pl.ds(SUB, tm), :] = hl_ref[...]
        for h in range(N_HEADS):
            cols = slice(h * HEAD, (h + 1) * HEAD)
            xc = _conv_head(xa_ext, cw_ref, cb_ref, cols, tm)
            neg_c_sp = -LRU_C * _softplus_neg(lam_ref[:, cols])
            r, i, log_a = _lru_gates(xc, h, wa_ref, ba_ref, wx_ref, bx_ref, neg_c_sp)
            xc_buf[:, cols] = xc
            r_buf[:, cols] = r
            i_buf[:, cols] = i
            a_ext[pl.ds(0, tm), cols] = jnp.exp(log_a)
            ga = zq_ref[:, lw + h * HEAD:lw + (h + 1) * HEAD]
            sg = _sigmoid(ga)
            dyav = dya_ref[:, cols]
            dh_buf[:, cols] = dyav * (ga * sg)
            dga = (dyav * hl_ext[pl.ds(SUB, tm), cols]) * (sg * (1.0 + ga * (1.0 - sg)))
            dz_ref[:, lw + h * HEAD:lw + (h + 1) * HEAD] = dga.astype(dz_ref.dtype)
        an_buf[...] = _rows_ahead(a_ext[...], 1)[0:tm]

        rows = lax.broadcasted_iota(jnp.int32, (SUB, lw), 0)
        nch = tm // SUB

        def step(c, car):
            i0 = pl.multiple_of((nch - 1 - c) * SUB, SUB)
            cv = an_buf[pl.ds(i0, SUB), :]
            bv = dh_buf[pl.ds(i0, SUB), :]
            for sh in (1, 2, 4):
                m = rows < SUB - sh
                c_sh = jnp.where(m, pltpu.roll(cv, SUB - sh, 0), 1.0)
                b_sh = jnp.where(m, pltpu.roll(bv, SUB - sh, 0), 0.0)
                bv = cv * b_sh + bv
                cv = cv * c_sh
            hv = cv * car + bv
            dh_buf[pl.ds(i0, SUB), :] = hv
            return jnp.broadcast_to(hv[0:1, :], (SUB, lw))

        carry[...] = lax.fori_loop(0, nch, step, carry[...])

        for h in range(N_HEADS):
            cols = slice(h * HEAD, (h + 1) * HEAD)
            lam_h = lam_ref[:, cols]
            neg_c_sp = -LRU_C * _softplus_neg(lam_h)
            xc = xc_buf[:, cols]
            r = r_buf[:, cols]
            i = i_buf[:, cols]
            a = a_ext[pl.ds(0, tm), cols]
            mult = _lru_mult(neg_c_sp * r, a)
            dh = dh_buf[:, cols]
            dhx = dh * xc
            di = dhx * mult
            h_prev = _rows_back(hl_ext[:, cols], 1)[SUB:SUB + tm]
            dlog_a = (dh * h_prev) * a - (dhx * i) * ((a * a) / mult)
            pack_ref[pl.ds(R["dlam"][0] + h, 1), :] += _colsum(dlog_a * r) * (LRU_C * _sigmoid(-lam_h))
            dpa = (dlog_a * neg_c_sp) * (r * (1.0 - r))
            dpx = di * (i * (1.0 - i))
            dpa16 = dpa.astype(_MM)
            dpx16 = dpx.astype(_MM)
            xc16 = xc.astype(_MM)
            dxc = dh * (mult * i) + _dot_nt(dpa16, wa_ref[h]) + _dot_nt(dpx16, wx_ref[h])
            pack_ref[pl.ds(R["dwa"][0] + h * HEAD, HEAD), :] += _dot_tn(xc16, dpa16)
            pack_ref[pl.ds(R["dwx"][0] + h * HEAD, HEAD), :] += _dot_tn(xc16, dpx16)
            pack_ref[pl.ds(R["dba"][0] + h, 1), :] += _colsum(dpa)
            pack_ref[pl.ds(R["dbx"][0] + h, 1), :] += _colsum(dpx)
            pack_ref[pl.ds(R["dcb"][0] + h, 1), :] += _colsum(dxc)
            dxc_ext[pl.ds(0, tm), cols] = dxc

        for h in range(N_HEADS):
            cols = slice(h * HEAD, (h + 1) * HEAD)
            dxc_all = dxc_ext[:, cols]
            xa = xa_ext[pl.ds(HALO, tm), cols]
            dxa = None
            for k in range(4):
                dxc_k = _rows_ahead(dxc_all, 3 - k)[0:tm]
                pack_ref[pl.ds(R["dcw"][0] + SUB * k + h, 1), :] += _colsum(dxc_k * xa)
                term = cw_ref[pl.ds(k, 1), cols] * dxc_k
                dxa = term if dxa is None else dxa + term
            dz_ref[:, cols] = dxa.astype(dz_ref.dtype)

        xb_ext[pl.ds(0, HALO), :] = jnp.where(has_prev, zh_ref[:, 2 * lw:2 * lw + pwid], 0.0)
        xb_ext[pl.ds(HALO, tm), :] = zq_ref[:, 2 * lw:2 * lw + pwid]
        pos = jr * tm + lax.broadcasted_iota(jnp.int32, (tm, HEAD), 0)
        for g, k in enumerate(POOL_WINDOWS):
            cols = slice(g * HEAD, (g + 1) * HEAD)
            diff, cnt = _pool_diff(xb_ext, g, k, pos, tm)
            diff16 = diff.astype(_MM)
            yp = _dot(diff16, pw_ref[g])
            sc = ps_ref[:, cols]
            gb = zq_ref[:, 2 * lw + pwid + g * HEAD:2 * lw + pwid + (g + 1) * HEAD]
            sgb = _sigmoid(gb)
            dybv = dyb_ref[:, cols]
            dy_pool = dybv * (gb * sgb)
            dgb = (dybv * (yp * sc)) * (sgb * (1.0 + gb * (1.0 - sgb)))
            pack_ref[pl.ds(R["dps"][0] + g, 1), :] += _colsum(dy_pool * yp)
            dyp16 = (dy_pool * sc).astype(_MM)
            pack_ref[pl.ds(R["dpw"][0] + g * HEAD, HEAD), :] += _dot_tn(diff16, dyp16)
            ddiff = _dot_nt(dyp16, pw_ref[g])
            q_ext[pl.ds(0, tm), cols] = ddiff / cnt
            dxb = _window_sum(q_ext[:, cols], k, _rows_ahead)[0:tm] - ddiff
            dz_ref[:, 2 * lw + g * HEAD:2 * lw + (g + 1) * HEAD] = dxb.astype(dz_ref.dtype)
            dz_ref[:, 2 * lw + pwid + g * HEAD:2 * lw + pwid + (g + 1) * HEAD] = dgb.astype(dz_ref.dtype)

        a_ext[pl.ds(tm, SUB), :] = a_ext[pl.ds(0, SUB), :]
        dxc_ext[pl.ds(tm, HALO), :] = dxc_ext[pl.ds(0, HALO), :]
        q_ext[pl.ds(tm, HALO), :] = q_ext[pl.ds(0, HALO), :]

        @pl.when((b == nb - 1) & (j == nt - 1))
        def _():
            for cp in _sibling_copies(sib_src, sib_out, send_sems, recv_sems):
                cp.wait()

    rev = lambda w: pl.BlockSpec((None, tm, w), lambda b, j: (b, nt - 1 - j, 0))
    prev_rows = lambda rows, w: pl.BlockSpec(
        (None, rows, w), lambda b, j: (b, jnp.maximum((nt - 1 - j) * (tm // rows) - 1, 0), 0))
    return pl.pallas_call(
        body, name="bwd_seq", grid=(nb, nt),
        in_specs=[rev(zw), prev_rows(HALO, zw), rev(lw), prev_rows(SUB, lw), rev(lw), rev(pwid), rev(nc - zw),
                  _const_spec(cw.shape), _const_spec(cb.shape), _const_spec(wa.shape), _const_spec(ba.shape),
                  _const_spec(wx.shape), _const_spec(bx.shape), _const_spec(lam.shape), _const_spec(pw.shape),
                  _const_spec(ps.shape)] + [HBM_SPEC] * ns,
        out_specs=(rev(nc), pl.BlockSpec((SEQ_PACK_ROWS, HEAD), lambda b, j: (0, 0))) + (HBM_SPEC,) * ns,
        out_shape=(jax.ShapeDtypeStruct((nb, s, nc), _MM), jax.ShapeDtypeStruct((SEQ_PACK_ROWS, HEAD), F32))
        + tuple(jax.ShapeDtypeStruct((4,) + g.shape[1:], g.dtype) for g in sib_grads),
        scratch_shapes=[pltpu.VMEM((tm + HALO, lw), F32), pltpu.VMEM((tm + HALO, pwid), F32),
                        pltpu.VMEM((tm + SUB, lw), F32), pltpu.VMEM((tm, lw), F32), pltpu.VMEM((tm + SUB, lw), F32),
                        pltpu.VMEM((tm, lw), F32), pltpu.VMEM((tm, lw), F32), pltpu.VMEM((tm, lw), F32),
                        pltpu.VMEM((tm, lw), F32), pltpu.VMEM((tm + HALO, lw), F32),
                        pltpu.VMEM((tm + HALO, pwid), F32), pltpu.VMEM((SUB, lw), F32),
                        pltpu.SemaphoreType.DMA((ns, 4)), pltpu.SemaphoreType.DMA((ns, 4))],
        compiler_params=pltpu.CompilerParams(dimension_semantics=("arbitrary", "arbitrary"), vmem_limit_bytes=VMEM_LIMIT),
    )(z, z, hl, hl, dya, dyb, dm, cw, cb, wa, ba, wx, bx, lam, pw, ps, *sib_grads)


def _bwd_win(ht, dz, chip_srcs, small):
    nb, d, s = ht.shape
    nc = dz.shape[2]
    sec = nc // N_DEV
    n_chips = N_DEV // 2
    nm = len(chip_srcs)

    def body(ht_ref, dz_ref, *rest):
        chip_src, small_src = rest[:nm], rest[nm]
        gwin_ref, recv_ref = rest[nm + 1:nm + 3]
        chip_out, small_out = rest[nm + 3:2 * nm + 3], rest[2 * nm + 3]
        acc, local_sems, sib_send, sib_recv, chip_send, chip_recv, small_send, small_recv = rest[2 * nm + 4:]
        q = pl.program_id(0)
        b = pl.program_id(1)
        slot = q % 2
        x, y, c = _my_pos()

        def half(sl, core):
            return acc.at[sl, :, pl.ds(pl.multiple_of(core * sec, HEAD), sec)]

        def local_copy(qq, sl):
            return pltpu.make_async_copy(half(sl, c), gwin_ref.at[2 * qq + c], local_sems.at[sl])

        def sib_copy(qq, sl):
            return pltpu.make_async_remote_copy(
                src_ref=half(sl, 1 - c), dst_ref=recv_ref.at[qq], send_sem=sib_send.at[qq],
                recv_sem=sib_recv.at[qq], device_id=(x, y, 1 - c), device_id_type=MESH)

        def others():
            return (_chip_copies(chip_src, chip_out, chip_send, chip_recv)
                    + _sibling_copies([small_src], [small_out], small_send, small_recv))

        @pl.when((q == 0) & (b == 0))
        def _():
            for cp in others():
                cp.start()

        @pl.when((q >= 2) & (b == 0))
        def _():
            local_copy(q - 2, slot).wait()
            sib_copy(q - 2, slot).wait_send()

        part = _dot(ht_ref[b], dz_ref[...])

        @pl.when(b == 0)
        def _():
            acc[slot] = part

        @pl.when(b != 0)
        def _():
            acc[slot] += part

        @pl.when(b == nb - 1)
        def _():
            local_copy(q, slot).start()
            sib_copy(q, slot).start()

        @pl.when((q == n_chips - 1) & (b == nb - 1))
        def _():
            for qq in (n_chips - 2, n_chips - 1):
                local_copy(qq, qq % 2).wait()
                sib_copy(qq, qq % 2).wait_send()
            for qq in range(n_chips):
                sib_copy(qq, 0).wait_recv()
            for cp in others():
                cp.wait()

    return pl.pallas_call(
        body, name="bwd_win", grid=(n_chips, nb),
        in_specs=[_const_spec(ht.shape), pl.BlockSpec((None, s, 2 * sec), lambda q, b: (b, 0, q))]
        + [HBM_SPEC] * (nm + 1),
        out_specs=(HBM_SPEC,) * (nm + 3),
        out_shape=(jax.ShapeDtypeStruct((N_DEV, d, sec), F32), jax.ShapeDtypeStruct((n_chips, d, sec), F32))
        + tuple(jax.ShapeDtypeStruct((3,) + g.shape[1:], g.dtype) for g in chip_srcs)
        + (jax.ShapeDtypeStruct((4,) + small.shape[1:], small.dtype),),
        scratch_shapes=[pltpu.VMEM((2, d, 2 * sec), F32), pltpu.SemaphoreType.DMA((2,)),
                        pltpu.SemaphoreType.DMA((N_DEV // 2,)), pltpu.SemaphoreType.DMA((N_DEV // 2,)),
                        pltpu.SemaphoreType.DMA((nm, 3)), pltpu.SemaphoreType.DMA((nm, 3)),
                        pltpu.SemaphoreType.DMA((1, 4)), pltpu.SemaphoreType.DMA((1, 4))],
        compiler_params=pltpu.CompilerParams(dimension_semantics=("arbitrary", "arbitrary"), vmem_limit_bytes=VMEM_LIMIT),
    )(ht, dz, *chip_srcs, small)


def _bwd_dx(dz, x, dx1, win, g0, chip_srcs, tm):
    t, d = x.shape
    nc = win.shape[0]
    n = t // tm
    nm = len(chip_srcs)

    def body(dz_ref, x_ref, dx1_ref, win_ref, g0_ref, *rest):
        chip_src = rest[:nm]
        gx_ref, pack_ref = rest[nm:nm + 2]
        chip_out = rest[nm + 2:2 * nm + 2]
        chip_send, chip_recv = rest[2 * nm + 2:]
        i = pl.program_id(0)

        @pl.when(i == 0)
        def _():
            pack_ref[...] = jnp.zeros_like(pack_ref)
            for cp in _chip_copies(chip_src, chip_out, chip_send, chip_recv):
                cp.start()

        xv = x_ref[...]
        r0 = _rsqrt_mean_sq(xv)
        xh = xv * r0
        dh = _dot(dz_ref[...], win_ref[...])
        _store_chunks(pack_ref, 0, _colsum(dh * xh))
        dxh = dh * g0_ref[...]
        gx_ref[...] = dx1_ref[...] + r0 * (dxh - xh * jnp.mean(dxh * xh, axis=-1, keepdims=True))

        @pl.when(i == n - 1)
        def _():
            for cp in _chip_copies(chip_src, chip_out, chip_send, chip_recv):
                cp.wait()

    tile = lambda w: pl.BlockSpec((tm, w), lambda i: (i, 0))
    return pl.pallas_call(
        body, name="bwd_dx", grid=(n,),
        in_specs=[tile(nc), tile(d), tile(d), _const_spec(win.shape), _const_spec(g0.shape)] + [HBM_SPEC] * nm,
        out_specs=(tile(d), pl.BlockSpec((INP_PACK_ROWS, HEAD), lambda i: (0, 0))) + (HBM_SPEC,) * nm,
        out_shape=(jax.ShapeDtypeStruct((t, d), F32), jax.ShapeDtypeStruct((INP_PACK_ROWS, HEAD), F32))
        + tuple(jax.ShapeDtypeStruct((3,) + g.shape[1:], g.dtype) for g in chip_srcs),
        scratch_shapes=[pltpu.SemaphoreType.DMA((nm, 3)), pltpu.SemaphoreType.DMA((nm, 3))],
        compiler_params=pltpu.CompilerParams(dimension_semantics=("arbitrary",), vmem_limit_bytes=VMEM_LIMIT),
    )(dz, x, dx1, win, g0, *chip_srcs)


def _all_gather(shards, full_shapes, slicers, name):
    nw = len(shards)

    def body(*refs):
        start, forward, finish = _gather_phases(refs[:nw], refs[nw:2 * nw], slicers, *refs[2 * nw:])
        start()
        forward()
        finish()

    return pl.pallas_call(
        body, name=name, in_specs=[HBM_SPEC] * nw, out_specs=(HBM_SPEC,) * nw,
        out_shape=tuple(jax.ShapeDtypeStruct(shp, a.dtype) for shp, a in zip(full_shapes, shards)),
        scratch_shapes=_gather_sems(nw),
    )(*shards)


ROW_BLOCKS = 8


def _chip_sums(core_idx, grads, recvs, dtypes, name):
    nw = len(grads)

    def body(idx_ref, *refs):
        gs, rs, outs = refs[:nw], refs[nw:2 * nw], refs[2 * nw:]
        for g, r, o in zip(gs, rs, outs):
            o[...] = (g[...] + r[...]).astype(o.dtype)

    def blk(g):
        return (None,) + g.shape[1:]

    return pl.pallas_call(
        body, name=name,
        grid_spec=pltpu.PrefetchScalarGridSpec(
            num_scalar_prefetch=1, grid=(4,),
            in_specs=[pl.BlockSpec(blk(g), lambda q, s: (2 * q + s[0], 0, 0)) for g in grads]
            + [pl.BlockSpec(blk(g), lambda q, s: (q, 0, 0)) for g in grads],
            out_specs=tuple(pl.BlockSpec(blk(g), lambda q, s: (q, 0, 0)) for g in grads)),
        out_shape=tuple(jax.ShapeDtypeStruct((4,) + g.shape[1:], dt) for g, dt in zip(grads, dtypes)),
    )(core_idx, *grads, *recvs)


def _adamw(w, g, m, v):
    m = ADAM_B1 * m + (1.0 - ADAM_B1) * g
    v = ADAM_B2 * v + (1.0 - ADAM_B2) * (g * g)
    m_hat = m / (1.0 - ADAM_B1 ** ADAM_STEP)
    v_hat = v / (1.0 - ADAM_B2 ** ADAM_STEP)
    delta = -ADAM_LR * (m_hat / (jnp.sqrt(v_hat) + ADAM_EPS) + ADAM_WD * w)
    return delta, m, v


def _adam_sections(sec_idx, grads, recv_sib, recv_chips, wmv):
    nw = len(wmv)

    def body(idx_ref, *refs):
        gs, rs, cs = refs[:nw + 1], refs[nw + 1:2 * nw + 2], refs[2 * nw + 2:3 * nw + 3]
        params = refs[3 * nw + 3:6 * nw + 3]
        outs = refs[6 * nw + 3:]
        for w in range(nw + 1):
            g = gs[w][...] + rs[w][...]
            for k in range(3):
                g = g + cs[w][k].astype(F32)
            if w == nw:
                outs[4 * nw][...] = g
                continue
            wv, mv, vv = (params[3 * w + t][...] for t in range(3))
            delta, m_new, v_new = _adamw(wv, g, mv, vv)
            outs[4 * w][...] = g
            outs[4 * w + 1][...] = delta
            outs[4 * w + 2][...] = m_new
            outs[4 * w + 3][...] = v_new

    def rb(g):
        return g.shape[1] // ROW_BLOCKS

    in_specs = [pl.BlockSpec((None, rb(g), g.shape[2]), lambda i, s: (s[0], i, 0)) for g in grads]
    in_specs += [pl.BlockSpec((None, rb(g), g.shape[2]), lambda i, s: (s[1], i, 0)) for g in grads]
    in_specs += [pl.BlockSpec((3, rb(g), g.shape[2]), lambda i, s: (0, i, 0)) for g in grads]
    sec = lambda g: pl.BlockSpec((rb(g), g.shape[2]), lambda i, s: (i, 0))
    for w in range(nw):
        in_specs += [sec(grads[w])] * 3
    out_specs, out_shape = [], []
    for g in grads[:nw]:
        out_specs += [sec(g)] * 4
        out_shape += [jax.ShapeDtypeStruct(g.shape[1:], F32)] * 4
    out_specs.append(sec(grads[nw]))
    out_shape.append(jax.ShapeDtypeStruct(grads[nw].shape[1:], F32))
    flat = [a for t in wmv for a in t]
    outs = pl.pallas_call(
        body, name="adam_sections",
        grid_spec=pltpu.PrefetchScalarGridSpec(num_scalar_prefetch=1, grid=(ROW_BLOCKS,), in_specs=in_specs,
                                               out_specs=tuple(out_specs)),
        out_shape=tuple(out_shape),
    )(sec_idx, *grads, *recv_sib, *recv_chips, *flat)
    return [outs[4 * w:4 * w + 4] for w in range(nw)], outs[4 * nw]


def _adam_small(sec_idx, gpack, parts, parts_wmv, vec_params, mat_params, conv_wmv):
    items = [(0, parts_wmv, "parts")]
    items += [(r0, t, "vec") for r0, t in vec_params] + [(r0, t, "mat") for r0, t in mat_params]
    items.append((_SEQ_ROWS["dcw"][0], conv_wmv, "conv"))

    def body(idx_ref, g_ref, parts_ref, *refs):
        ins, outs = refs[:3 * len(items)], refs[3 * len(items):]
        for n, (r0, _, kind) in enumerate(items):
            w_ref, m_ref, v_ref = ins[3 * n:3 * n + 3]
            o = outs[4 * n:4 * n + 4]
            if kind == "mat":
                g = g_ref[pl.ds(r0, w_ref.shape[0]), :]
                res = (g,) + _adamw(w_ref[...], g, m_ref[...], v_ref[...])
                for ref, val in zip(o, res):
                    ref[...] = val
            elif kind in ("vec", "parts"):
                for h in range(w_ref.shape[1] // HEAD):
                    cols = slice(h * HEAD, (h + 1) * HEAD)
                    if kind == "vec":
                        g = g_ref[pl.ds(r0 + h, 1), :]
                    else:
                        g = parts_ref[pl.ds(h, 1), :]
                        for dev in range(1, N_DEV):
                            g = g + parts_ref[pl.ds(SUB * dev + h, 1), :]
                    res = (g,) + _adamw(w_ref[:, cols], g, m_ref[:, cols], v_ref[:, cols])
                    for ref, val in zip(o, res):
                        ref[:, cols] = val
            else:
                rows = lax.broadcasted_iota(jnp.int32, (SUB, HEAD), 0)
                for k in range(4):
                    blk = g_ref[pl.ds(r0 + SUB * k, SUB), :]
                    g = jnp.sum(jnp.where(rows == idx_ref[0], blk, 0.0), axis=0, keepdims=True)
                    row = pl.ds(k, 1)
                    res = (g,) + _adamw(w_ref[row, :], g, m_ref[row, :], v_ref[row, :])
                    for ref, val in zip(o, res):
                        ref[row, :] = val
        outs[4 * len(items)][...] = g_ref[pl.ds(_OFF_MID + MID_LOSS_ROW, 1), :]

    flat = [a for _, t, _ in items for a in t]
    full = lambda a: pl.BlockSpec(a.shape, lambda i, s: (0,) * a.ndim)
    out_specs, out_shape = [], []
    for _, t, _ in items:
        out_specs += [full(t[0])] * 4
        out_shape += [jax.ShapeDtypeStruct(t[0].shape, F32)] * 4
    loss_row = jax.ShapeDtypeStruct((1, HEAD), F32)
    out_specs.append(full(loss_row))
    out_shape.append(loss_row)
    outs = pl.pallas_call(
        body, name="adam_small",
        grid_spec=pltpu.PrefetchScalarGridSpec(num_scalar_prefetch=1, grid=(1,),
                                               in_specs=[full(gpack), full(parts)] + [full(a) for a in flat],
                                               out_specs=tuple(out_specs)),
        out_shape=tuple(out_shape),
    )(sec_idx, gpack, parts, *flat)
    return [outs[4 * n:4 * n + 4] for n in range(len(items))], outs[4 * len(items)]


def _col_slicer(width):
    return lambda ref, idx: ref.at[:, pl.ds(pl.multiple_of(idx * width, HEAD), width)]


def _row_slicer(rows):
    return lambda ref, idx: ref.at[pl.ds(pl.multiple_of(idx * rows, SUB), rows), :]


def kernel(x, p, norm_g, w_in, conv_w, conv_b, lru_w_a, lru_b_a, lru_w_x, lru_b_x, lru_lambda, pool_w, pool_scale, w_proj_lru, w_proj_pool, w_out, ple_norm_g, w_ple_gate, w_ple_proj, final_g, loss_target, m_norm_g, m_w_in, m_conv_w, m_conv_b, m_lru_w_a, m_lru_b_a, m_lru_w_x, m_lru_b_x, m_lru_lambda, m_pool_w, m_pool_scale, m_w_proj_lru, m_w_proj_pool, m_w_out, m_ple_norm_g, m_w_ple_gate, m_w_ple_proj, m_final_g, v_norm_g, v_w_in, v_conv_w, v_conv_b, v_lru_w_a, v_lru_b_a, v_lru_w_x, v_lru_b_x, v_lru_lambda, v_pool_w, v_pool_scale, v_w_proj_lru, v_w_proj_pool, v_w_out, v_ple_norm_g, v_w_ple_gate, v_w_ple_proj, v_final_g):
    nb, s, d = x.shape
    t = nb * s
    tm = min(TILE_M, s)
    sec_idx = (4 * lax.axis_index("x") + 2 * lax.axis_index("y") + lax.axis_index("c")).astype(jnp.int32)
    chip_idx = (2 * lax.axis_index("x") + lax.axis_index("y")).astype(jnp.int32)
    core_idx = lax.axis_index("c").astype(jnp.int32)

    win, cw = _all_gather(
        [w_in[0].astype(_MM), conv_w[0]], [(d, N_DEV * w_in.shape[2]), (conv_w.shape[1], N_DEV * conv_w.shape[2])],
        [_col_slicer(w_in.shape[2]), _col_slicer(conv_w.shape[2])], "gather_w_in")
    late_shards = [w_proj_lru[0].astype(_MM), w_proj_pool[0].astype(_MM), w_out[0].astype(_MM),
                   w_ple_gate[0].astype(_MM), w_ple_proj[0].astype(_MM)]
    late_shapes = [(N_DEV * w_proj_lru.shape[1], d), (w_proj_pool.shape[1], N_DEV * w_proj_pool.shape[2]),
                   (N_DEV * w_out.shape[1], d), (N_DEV * w_ple_gate.shape[1], d),
                   (w_ple_proj.shape[1], N_DEV * w_ple_proj.shape[2])]
    late_slicers = [_row_slicer(w_proj_lru.shape[1]), _col_slicer(w_proj_pool.shape[2]), _row_slicer(w_out.shape[1]),
                    _row_slicer(w_ple_gate.shape[1]), _col_slicer(w_ple_proj.shape[2])]

    lw = cw.shape[1]
    pwid = pool_scale.shape[1]
    wa = lru_w_a[0].astype(_MM)
    wx = lru_w_x[0].astype(_MM)
    pw = pool_w[0].astype(_MM)
    ba, bx = lru_b_a[0], lru_b_x[0]
    gf = final_g.reshape(1, d)
    rows_of = lambda g: g.reshape(N_DEV, g.shape[0] // N_DEV, g.shape[1])
    core = core_idx.reshape(1)

    z, hl, ya, yb, ht, wpl, wpp, wout, wpg, wpe = _fwd_seq(
        x, norm_g, win, cw, conv_b, wa, ba, wx, bx, lru_lambda, pw, pool_scale, late_shards, late_shapes, late_slicers,
        min(TILE_M_FWD, s))
    x2d = x.reshape(t, d)
    (dya, dyb, dm, dx1, g_pl, g_pp, g_out, g_pg, g_pe, pack_mid) = _mid(
        ya.reshape(t, lw), yb.reshape(t, pwid), z.reshape(t, -1), x2d, p[0].reshape(t, -1), loss_target.reshape(t, d),
        wpl, wpp, wout, wpg, wpe, ple_norm_g, gf, tm)
    grads_mid = [rows_of(g_pl), g_pp, rows_of(g_out), rows_of(g_pg), g_pe]
    seq_out = _bwd_seq(z, hl, dya.reshape(nb, s, lw), dyb.reshape(nb, s, pwid), dm.reshape(nb, s, -1), cw, conv_b,
                       wa, ba, wx, bx, lru_lambda, pw, pool_scale, grads_mid, tm)
    dz, pack_seq, sib_mid = seq_out[0], seq_out[1], list(seq_out[2:])
    sums_mid = _chip_sums(core, grads_mid, sib_mid, [_WIRE] * len(grads_mid), "chip_sums_mid")
    small = jnp.concatenate(
        [pack_seq, pack_mid, jnp.zeros((SMALL_ROWS - _OFF_MID - MID_PACK_ROWS, HEAD), F32)], axis=0
    ).reshape(N_DEV, SMALL_SEC, HEAD)
    win_out = _bwd_win(ht, dz, sums_mid, small)
    g_in, sib_in, chips_mid, sib_small = win_out[0], win_out[1], list(win_out[2:-1]), win_out[-1]
    sums_in = _chip_sums(core, [g_in, small], [sib_in, sib_small], [_WIRE, F32], "chip_sums_in")
    dx_out = _bwd_dx(dz.reshape(t, -1), x2d, dx1, win.T, norm_g, sums_in, min(TILE_M_DX, t))
    grad_x, pack_inp, chips_in, chips_small = dx_out

    grads = [g_in] + grads_mid + [small]
    recv_sib = [sib_in] + sib_mid + [sib_small]
    recv_chips = [chips_in] + chips_mid + [chips_small]
    wmv = [(w_in[0], m_w_in[0], v_w_in[0]), (w_proj_lru[0], m_w_proj_lru[0], v_w_proj_lru[0]),
           (w_proj_pool[0], m_w_proj_pool[0], v_w_proj_pool[0]), (w_out[0], m_w_out[0], v_w_out[0]),
           (w_ple_gate[0], m_w_ple_gate[0], v_w_ple_gate[0]), (w_ple_proj[0], m_w_ple_proj[0], v_w_ple_proj[0])]
    big, small_sec = _adam_sections(jnp.stack([sec_idx, chip_idx]), grads, recv_sib, recv_chips, wmv)
    gsmall, g0_parts = _all_gather([small_sec, pack_inp], [(SMALL_ROWS, HEAD), (N_DEV * INP_PACK_ROWS, HEAD)],
                                   [_row_slicer(SMALL_SEC), _row_slicer(INP_PACK_ROWS)], "gather_small")

    R = _SEQ_ROWS
    vec = lambda r0, *t: (r0, tuple(a.reshape(1, -1) for a in t))
    mat = lambda r0, *t: (r0, tuple(a.reshape(-1, HEAD) for a in t))
    norm_wmv = vec(0, norm_g, m_norm_g, v_norm_g)[1]
    vec_params = [vec(R["dcb"][0], conv_b, m_conv_b, v_conv_b),
                  vec(R["dlam"][0], lru_lambda, m_lru_lambda, v_lru_lambda),
                  vec(R["dps"][0], pool_scale, m_pool_scale, v_pool_scale),
                  vec(_OFF_MID, ple_norm_g, m_ple_norm_g, v_ple_norm_g),
                  vec(_OFF_MID + SUB, final_g, m_final_g, v_final_g)]
    mat_params = [mat(R["dwa"][0], lru_w_a, m_lru_w_a, v_lru_w_a), mat(R["dba"][0], lru_b_a, m_lru_b_a, v_lru_b_a),
                  mat(R["dwx"][0], lru_w_x, m_lru_w_x, v_lru_w_x), mat(R["dbx"][0], lru_b_x, m_lru_b_x, v_lru_b_x),
                  mat(R["dpw"][0], pool_w, m_pool_w, v_pool_w)]
    conv_wmv = (conv_w[0], m_conv_w[0], v_conv_w[0])
    small_out, loss_row = _adam_small(sec_idx.reshape(1), gsmall, g0_parts, norm_wmv, vec_params, mat_params, conv_wmv)
    loss = loss_row[0, 0]

    res = {}
    names_small = ["norm_g", "conv_b", "lru_lambda", "pool_scale", "ple_norm_g", "final_g",
                   "lru_w_a", "lru_b_a", "lru_w_x", "lru_b_x", "pool_w", "conv_w"]
    shapes = {"norm_g": norm_g, "conv_b": conv_b, "lru_lambda": lru_lambda, "pool_scale": pool_scale,
              "ple_norm_g": ple_norm_g, "final_g": final_g, "lru_w_a": lru_w_a, "lru_b_a": lru_b_a, "lru_w_x": lru_w_x,
              "lru_b_x": lru_b_x, "pool_w": pool_w, "conv_w": conv_w}
    for name, quad in zip(names_small, small_out):
        res[name] = [a.reshape(shapes[name].shape) for a in quad]
    names_big = ["w_in", "w_proj_lru", "w_proj_pool", "w_out", "w_ple_gate", "w_ple_proj"]
    for name, quad, (w, _, _) in zip(names_big, big, wmv):
        res[name] = [a.reshape((1,) + w.shape) for a in quad]
    order = ["norm_g", "w_in", "conv_w", "conv_b", "lru_w_a", "lru_b_a", "lru_w_x", "lru_b_x", "lru_lambda", "pool_w",
             "pool_scale", "w_proj_lru", "w_proj_pool", "w_out", "ple_norm_g", "w_ple_gate", "w_ple_proj", "final_g"]
    out = [loss, grad_x.reshape(nb, s, d)]
    for kind in range(4):
        out += [res[name][kind] for name in order]
    return tuple(out)
```

```python
import jax
import jax.numpy as jnp
from jax import lax
from jax.experimental import pallas as pl
from jax.experimental.pallas import tpu as pltpu

F32 = jnp.float32
_MM = jnp.bfloat16
_WIRE = jnp.bfloat16

EPS = 1e-6
LRU_C = 8.0
POOL_WINDOWS = (2, 4, 8, 16)
N_HEADS = 8
HEAD = 128
HALO = 16
SUB = 8

ADAM_LR = 0.001
ADAM_B1 = 0.9
ADAM_B2 = 0.999
ADAM_EPS = 1e-08
ADAM_WD = 0.01
ADAM_STEP = 10

N_DEV = 8
MESH = pl.DeviceIdType.MESH
VMEM_LIMIT = 60 * 1024 * 1024
TILE_M = 256

_SEQ_ROWS = {"dwa": (0, 1024), "dwx": (1024, 1024), "dpw": (2048, 512), "dba": (2560, 8), "dbx": (2568, 8),
             "dcb": (2576, 8), "dlam": (2584, 8), "dcw": (2592, 32), "dps": (2624, 8)}
SEQ_PACK_ROWS = 2632
MID_PACK_ROWS = 24
MID_LOSS_ROW = 16
INP_PACK_ROWS = 8
SMALL_ROWS = 3072
SMALL_SEC = SMALL_ROWS // N_DEV
_OFF_MID = SEQ_PACK_ROWS


def _dot(a, b):
    return jnp.dot(a, b, preferred_element_type=F32)


def _dot_nt(a, b):
    return lax.dot_general(a, b, (((1,), (1,)), ((), ())), preferred_element_type=F32)


def _dot_tn(a, b):
    return lax.dot_general(a, b, (((0,), (0,)), ((), ())), preferred_element_type=F32)


def _sigmoid(v):
    return 0.5 * jnp.tanh(0.5 * v) + 0.5


def _lru_mult(log_a, a):
    return jnp.sqrt(-jnp.tanh(log_a) * (1.0 + a * a))


def _rows_back(v, n):
    return pltpu.roll(v, n, 0) if n else v


def _rows_ahead(v, n):
    return pltpu.roll(v, v.shape[0] - n, 0) if n else v


def _softplus_neg(lam):
    e = jnp.exp(-jnp.abs(lam))
    w = 1.0 + e
    l1p = jnp.where(w == 1.0, e, jnp.log(w) * (e / (w - 1.0)))
    return jnp.maximum(-lam, 0.0) + l1p


def _rsqrt_mean_sq(v):
    return lax.rsqrt(jnp.mean(v * v, axis=-1, keepdims=True) + EPS)


def _colsum(v):
    return jnp.sum(v, axis=0, keepdims=True)


def _const_spec(shape):
    nd = len(shape)
    return pl.BlockSpec(shape, lambda *_: (0,) * nd, pipeline_mode=pl.Buffered(1))


HBM_SPEC = pl.BlockSpec(memory_space=pl.ANY)


def _my_pos():
    return lax.axis_index("x"), lax.axis_index("y"), lax.axis_index("c")


def _other_chips(x, y):
    return [(1 - x, y), (x, 1 - y), (1 - x, 1 - y)]


def _gather_sems(nw):
    return [pltpu.SemaphoreType.DMA((nw, 7)), pltpu.SemaphoreType.DMA((nw, 7)), pltpu.SemaphoreType.DMA((nw,))]


def _gather_phases(srcs, outs, slicers, send_sems, recv_sems, local_sems):
    nw = len(srcs)
    x, y, c = _my_pos()
    me, sibling = (x, y, c), (x, y, 1 - c)
    chips = _other_chips(x, y)

    def part(w, pos):
        return slicers[w](outs[w], 4 * pos[0] + 2 * pos[1] + pos[2])

    def copy(w, k, block, to, src=None):
        return pltpu.make_async_remote_copy(
            src_ref=part(w, block) if src is None else src, dst_ref=part(w, block),
            send_sem=send_sems.at[w, k], recv_sem=recv_sems.at[w, k], device_id=to, device_id_type=MESH)

    def mine():
        return [pltpu.make_async_copy(srcs[w], part(w, me), local_sems.at[w]) for w in range(nw)]

    def first():
        out = []
        for w in range(nw):
            out.append(copy(w, 0, me, sibling, src=srcs[w]))
            out += [copy(w, 1 + j, me, (*chip, c), src=srcs[w]) for j, chip in enumerate(chips)]
        return out

    def passed():
        return [copy(w, 4 + j, (*chip, c), sibling) for j, chip in enumerate(chips) for w in range(nw)]

    def start():
        for cp in mine() + first():
            cp.start()

    def forward():
        for j, chip in enumerate(chips):
            for w in range(nw):
                copy(w, 1 + j, (*chip, c), me).wait_recv()
                copy(w, 4 + j, (*chip, c), sibling).start()

    def finish():
        for w in range(nw):
            copy(w, 0, sibling, me).wait_recv()
            for j, chip in enumerate(chips):
                copy(w, 4 + j, (*chip, 1 - c), me).wait_recv()
        for cp in first() + passed():
            cp.wait_send()
        for cp in mine():
            cp.wait()

    return start, forward, finish


def _sibling_copies(srcs, outs, send_sems, recv_sems):
    x, y, c = _my_pos()
    return [pltpu.make_async_remote_copy(
        src_ref=srcs[w].at[2 * q + (1 - c)], dst_ref=outs[w].at[q], send_sem=send_sems.at[w, q],
        recv_sem=recv_sems.at[w, q], device_id=(x, y, 1 - c), device_id_type=MESH)
        for w in range(len(srcs)) for q in range(4)]


def _chip_copies(srcs, outs, send_sems, recv_sems):
    x, y, c = _my_pos()
    return [pltpu.make_async_remote_copy(
        src_ref=srcs[w].at[2 * px + py], dst_ref=outs[w].at[k], send_sem=send_sems.at[w, k],
        recv_sem=recv_sems.at[w, k], device_id=(px, py, c), device_id_type=MESH)
        for w in range(len(srcs)) for k, (px, py) in enumerate(_other_chips(x, y))]


def _lru_gates(xc, h, wa_ref, ba_ref, wx_ref, bx_ref, neg_c_sp):
    xc16 = xc.astype(_MM)
    r = _sigmoid(_dot(xc16, wa_ref[h]) + ba_ref[pl.ds(h, 1), :])
    i = _sigmoid(_dot(xc16, wx_ref[h]) + bx_ref[pl.ds(h, 1), :])
    return r, i, neg_c_sp * r


def _conv_head(xa_ext, cw_ref, cb_ref, cols, tm):
    ext = xa_ext[:, cols]
    xc = cb_ref[:, cols] + cw_ref[pl.ds(3, 1), cols] * ext[HALO:HALO + tm]
    for k in range(3):
        xc = xc + cw_ref[pl.ds(k, 1), cols] * _rows_back(ext, 3 - k)[HALO:HALO + tm]
    return xc


def _window_sum(ext, k, shift):
    n = 1
    while n < k:
        ext = ext + shift(ext, n)
        n *= 2
    return ext


def _pool_diff(xb_ext, g, k, pos, tm):
    cols = slice(g * HEAD, (g + 1) * HEAD)
    ext = xb_ext[:, cols]
    ws = _window_sum(ext, k, _rows_back)[HALO:HALO + tm]
    cnt = jnp.minimum(pos + 1, k).astype(F32)
    return ws / cnt - ext[HALO:HALO + tm], cnt


def _pre(x, g0, shards, shapes, slicers, tm):
    nb, s, d = x.shape
    nt = s // tm
    nl = len(shards)

    def body(x_ref, g0_ref, *rest):
        src, rest = rest[:nl], rest[nl:]
        h_ref, ht_ref = rest[:2]
        out, (send_sems, recv_sems, local_sems) = rest[2:2 + nl], rest[2 + nl:]
        step_no = pl.program_id(0) * nt + pl.program_id(1)
        g_start, g_forward, g_finish = _gather_phases(src, out, slicers, send_sems, recv_sems, local_sems)
        pl.when(step_no == 0)(g_start)
        xv = x_ref[...]
        h16 = ((xv * _rsqrt_mean_sq(xv)) * g0_ref[...]).astype(_MM)
        h_ref[...] = h16
        ht_ref[...] = h16.T

        @pl.when(step_no == nb * nt - 1)
        def _():
            g_forward()
            g_finish()

    tile = pl.BlockSpec((None, tm, d), lambda b, j: (b, j, 0))
    return pl.pallas_call(
        body, name="pre", grid=(nb, nt),
        in_specs=[tile, _const_spec(g0.shape)] + [HBM_SPEC] * nl,
        out_specs=(tile, pl.BlockSpec((None, d, tm), lambda b, j: (b, 0, j))) + (HBM_SPEC,) * nl,
        out_shape=(jax.ShapeDtypeStruct((nb, s, d), _MM), jax.ShapeDtypeStruct((nb, d, s), _MM))
        + tuple(jax.ShapeDtypeStruct(shp, a.dtype) for shp, a in zip(shapes, shards)),
        scratch_shapes=_gather_sems(nl),
        compiler_params=pltpu.CompilerParams(dimension_semantics=("arbitrary", "arbitrary")),
    )(x, g0, *shards)


def _fwd_seq(h, win, cw, cb, wa, ba, wx, bx, lam, pw, ps, late_shards, late_shapes, late_slicers, tm):
    nb, s, d = h.shape
    nc = win.shape[1]
    lw = cw.shape[1]
    pwid = ps.shape[1]
    nt = s // tm
    nl = len(late_shards)
    n_steps = nb * nt

    def body(h_ref, win_ref, cw_ref, cb_ref, wa_ref, ba_ref, wx_ref, bx_ref, lam_ref, pw_ref, ps_ref, *rest):
        late_src, rest = rest[:nl], rest[nl:]
        z_ref, hl_ref, ya_ref, yb_ref = rest[:4]
        late_out, rest = rest[4:4 + nl], rest[4 + nl:]
        xa_ext, xb_ext, a_buf, carry, send_sems, recv_sems, local_sems = rest
        j = pl.program_id(1)
        step_no = pl.program_id(0) * nt + j
        g_start, g_forward, g_finish = _gather_phases(late_src, late_out, late_slicers, send_sems, recv_sems, local_sems)
        pl.when(step_no == 0)(g_start)
        pl.when(step_no == n_steps // 2)(g_forward)

        @pl.when(j == 0)
        def _():
            xa_ext[pl.ds(0, HALO), :] = jnp.zeros((HALO, lw), F32)
            xb_ext[pl.ds(0, HALO), :] = jnp.zeros((HALO, pwid), F32)
            carry[...] = jnp.zeros_like(carry)

        h16 = h_ref[...]
        xa = _dot(h16, win_ref[:, 0:lw])
        z_ref[:, 0:lw] = xa
        xa_ext[pl.ds(HALO, tm), :] = xa
        for c0 in range(lw, nc, lw):
            z_ref[:, c0:c0 + lw] = _dot(h16, win_ref[:, c0:c0 + lw])

        for h in range(N_HEADS):
            cols = slice(h * HEAD, (h + 1) * HEAD)
            xc = _conv_head(xa_ext, cw_ref, cb_ref, cols, tm)
            neg_c_sp = -LRU_C * _softplus_neg(lam_ref[:, cols])
            _, i, log_a = _lru_gates(xc, h, wa_ref, ba_ref, wx_ref, bx_ref, neg_c_sp)
            a = jnp.exp(log_a)
            a_buf[:, cols] = a
            hl_ref[:, cols] = _lru_mult(log_a, a) * (i * xc)

        rows = lax.broadcasted_iota(jnp.int32, (SUB, lw), 0)

        def step(c, car):
            i0 = pl.multiple_of(c * SUB, SUB)
            av = a_buf[pl.ds(i0, SUB), :]
            bv = hl_ref[pl.ds(i0, SUB), :]
            for sh in (1, 2, 4):
                m = rows >= sh
                a_sh = jnp.where(m, pltpu.roll(av, sh, 0), 1.0)
                b_sh = jnp.where(m, pltpu.roll(bv, sh, 0), 0.0)
                bv = av * b_sh + bv
                av = av * a_sh
            hv = av * car + bv
            hl_ref[pl.ds(i0, SUB), :] = hv
            return jnp.broadcast_to(hv[SUB - 1:SUB, :], (SUB, lw))

        carry[...] = lax.fori_loop(0, tm // SUB, step, carry[...])

        ga = z_ref[:, lw:2 * lw]
        ya_ref[...] = (hl_ref[...] * (ga * _sigmoid(ga))).astype(ya_ref.dtype)

        xb_ext[pl.ds(HALO, tm), :] = z_ref[:, 2 * lw:2 * lw + pwid]
        pos = j * tm + lax.broadcasted_iota(jnp.int32, (tm, HEAD), 0)
        for g, k in enumerate(POOL_WINDOWS):
            cols = slice(g * HEAD, (g + 1) * HEAD)
            diff, _ = _pool_diff(xb_ext, g, k, pos, tm)
            yp = _dot(diff.astype(_MM), pw_ref[g])
            gb = z_ref[:, 2 * lw + pwid + g * HEAD:2 * lw + pwid + (g + 1) * HEAD]
            yb_ref[:, cols] = ((yp * ps_ref[:, cols]) * (gb * _sigmoid(gb))).astype(yb_ref.dtype)

        xa_ext[pl.ds(0, HALO), :] = xa_ext[pl.ds(tm, HALO), :]
        xb_ext[pl.ds(0, HALO), :] = xb_ext[pl.ds(tm, HALO), :]
        pl.when(step_no == n_steps - 1)(g_finish)

    tile = lambda w: pl.BlockSpec((None, tm, w), lambda b, j: (b, j, 0))
    return pl.pallas_call(
        body, name="fwd_seq", grid=(nb, nt),
        in_specs=[tile(d), _const_spec(win.shape), _const_spec(cw.shape), _const_spec(cb.shape),
                  _const_spec(wa.shape), _const_spec(ba.shape), _const_spec(wx.shape), _const_spec(bx.shape),
                  _const_spec(lam.shape), _const_spec(pw.shape), _const_spec(ps.shape)] + [HBM_SPEC] * nl,
        out_specs=(tile(nc), tile(lw), tile(lw), tile(pwid)) + (HBM_SPEC,) * nl,
        out_shape=(jax.ShapeDtypeStruct((nb, s, nc), F32), jax.ShapeDtypeStruct((nb, s, lw), F32),
                   jax.ShapeDtypeStruct((nb, s, lw), _MM), jax.ShapeDtypeStruct((nb, s, pwid), _MM))
        + tuple(jax.ShapeDtypeStruct(shp, a.dtype) for shp, a in zip(late_shapes, late_shards)),
        scratch_shapes=[pltpu.VMEM((tm + HALO, lw), F32), pltpu.VMEM((tm + HALO, pwid), F32),
                        pltpu.VMEM((tm, lw), F32), pltpu.VMEM((SUB, lw), F32)] + _gather_sems(nl),
        compiler_params=pltpu.CompilerParams(dimension_semantics=("arbitrary", "arbitrary"), vmem_limit_bytes=VMEM_LIMIT),
    )(h, win, cw, cb, wa, ba, wx, bx, lam, pw, ps, *late_shards)


def _store_chunks(ref, row0, vec):
    for h in range(vec.shape[1] // HEAD):
        ref[pl.ds(row0 + h, 1), :] += vec[:, h * HEAD:(h + 1) * HEAD]


def _mid(ya, yb, z, x, p, tgt, wpl, wpp, wout, wpg, wpe, g1, gf, tm):
    t, d = x.shape
    lw = ya.shape[1]
    pwid = yb.shape[1]
    pdim = p.shape[1]
    n = t // tm
    sec_pp = wpp.shape[1] // N_DEV
    sec_pe = wpe.shape[1] // N_DEV

    def body(ya_ref, yb_ref, ma_ref, mb_ref, x_ref, p_ref, tgt_ref, wpl_ref, wpp_ref, wout_ref, wpg_ref, wpe_ref,
             g1_ref, gf_ref,
             dya_ref, dyb_ref, dm_ref, dx1_ref, gpl_ref, gpp_ref, gout_ref, gpg_ref, gpe_ref, pack_ref,
             acc_pl, acc_pp, acc_out, acc_pg, acc_pe):
        i = pl.program_id(0)

        @pl.when(i == 0)
        def _():
            for acc in (acc_pl, acc_pp, acc_out, acc_pg, acc_pe):
                acc[...] = jnp.zeros_like(acc)
            pack_ref[...] = jnp.zeros_like(pack_ref)

        ya16 = ya_ref[...]
        yb16 = yb_ref[...]
        br_a = _dot(ya16, wpl_ref[...])
        br_b = _dot(yb16, wpp_ref[...])
        sa = _sigmoid(ma_ref[...])
        sb = _sigmoid(mb_ref[...])
        mg16 = (sa * br_a + sb * br_b).astype(_MM)
        x1 = x_ref[...] + _dot(mg16, wout_ref[...])
        r1 = _rsqrt_mean_sq(x1)
        n1 = x1 * r1
        h116 = (n1 * g1_ref[...]).astype(_MM)
        gate = _sigmoid(_dot(h116, wpg_ref[...]))
        p16 = p_ref[...].astype(_MM)
        pe = _dot(p16, wpe_ref[...])
        x2 = x1 + gate * pe
        r2 = _rsqrt_mean_sq(x2)
        n2 = x2 * r2
        err = n2 * gf_ref[...] - tgt_ref[...]
        sq = jnp.sum(_colsum(err * err), axis=1, keepdims=True)
        pack_ref[pl.ds(MID_LOSS_ROW, 1), :] += jnp.broadcast_to(sq * (0.5 / d), (1, HEAD))

        dy = err * (1.0 / d)
        _store_chunks(pack_ref, SUB, _colsum(dy * n2))
        dn2 = dy * gf_ref[...]
        dx2 = r2 * (dn2 - n2 * jnp.mean(dn2 * n2, axis=-1, keepdims=True))
        dpe16 = (dx2 * gate).astype(_MM)
        dpg16 = ((dx2 * pe) * (gate * (1.0 - gate))).astype(_MM)
        acc_pe[...] += _dot_tn(p16, dpe16)
        acc_pg[...] += _dot_tn(h116, dpg16)
        dh1 = _dot_nt(dpg16, wpg_ref[...])
        _store_chunks(pack_ref, 0, _colsum(dh1 * n1))
        dn1 = dh1 * g1_ref[...]
        dx1 = dx2 + r1 * (dn1 - n1 * jnp.mean(dn1 * n1, axis=-1, keepdims=True))
        dx1_ref[...] = dx1
        dx116 = dx1.astype(_MM)
        acc_out[...] += _dot_tn(mg16, dx116)
        dmg = _dot_nt(dx116, wout_ref[...])
        da16 = (dmg * sa).astype(_MM)
        db16 = (dmg * sb).astype(_MM)
        dm_ref[:, 0:d] = ((dmg * br_a) * (sa * (1.0 - sa))).astype(dm_ref.dtype)
        dm_ref[:, d:2 * d] = ((dmg * br_b) * (sb * (1.0 - sb))).astype(dm_ref.dtype)
        acc_pl[...] += _dot_tn(ya16, da16)
        acc_pp[...] += _dot_tn(yb16, db16)
        dya_ref[...] = _dot_nt(da16, wpl_ref[...])
        dyb_ref[...] = _dot_nt(db16, wpp_ref[...])

        @pl.when(i == n - 1)
        def _():
            pltpu.sync_copy(acc_pl, gpl_ref)
            pltpu.sync_copy(acc_out, gout_ref)
            pltpu.sync_copy(acc_pg, gpg_ref)
            for k in range(N_DEV):
                pltpu.sync_copy(acc_pp.at[:, pl.ds(k * sec_pp, sec_pp)], gpp_ref.at[k])
                pltpu.sync_copy(acc_pe.at[:, pl.ds(k * sec_pe, sec_pe)], gpe_ref.at[k])

    tile = lambda w: pl.BlockSpec((tm, w), lambda i: (i, 0))
    zcol = lambda c: pl.BlockSpec((tm, d), lambda i: (i, c))
    any_spec = pl.BlockSpec(memory_space=pl.ANY)
    full = lambda shape: pl.BlockSpec(shape, lambda i: (0,) * len(shape))
    return pl.pallas_call(
        body, name="mid", grid=(n,),
        in_specs=[tile(lw), tile(pwid), zcol(3), zcol(4), tile(d), tile(pdim), tile(d),
                  _const_spec(wpl.shape), _const_spec(wpp.shape), _const_spec(wout.shape), _const_spec(wpg.shape),
                  _const_spec(wpe.shape), _const_spec(g1.shape), _const_spec(gf.shape)],
        out_specs=(tile(lw), tile(pwid), tile(2 * d), tile(d), any_spec, any_spec, any_spec, any_spec, any_spec,
                   full((MID_PACK_ROWS, HEAD))),
        out_shape=(jax.ShapeDtypeStruct((t, lw), F32), jax.ShapeDtypeStruct((t, pwid), F32),
                   jax.ShapeDtypeStruct((t, 2 * d), _MM), jax.ShapeDtypeStruct((t, d), F32),
                   jax.ShapeDtypeStruct(wpl.shape, F32), jax.ShapeDtypeStruct((N_DEV, wpp.shape[0], sec_pp), F32),
                   jax.ShapeDtypeStruct(wout.shape, F32), jax.ShapeDtypeStruct(wpg.shape, F32),
                   jax.ShapeDtypeStruct((N_DEV, wpe.shape[0], sec_pe), F32),
                   jax.ShapeDtypeStruct((MID_PACK_ROWS, HEAD), F32)),
        scratch_shapes=[pltpu.VMEM(wpl.shape, F32), pltpu.VMEM(wpp.shape, F32), pltpu.VMEM(wout.shape, F32),
                        pltpu.VMEM(wpg.shape, F32), pltpu.VMEM(wpe.shape, F32)],
        compiler_params=pltpu.CompilerParams(dimension_semantics=("arbitrary",), vmem_limit_bytes=VMEM_LIMIT),
    )(ya, yb, z, z, x, p, tgt, wpl, wpp, wout, wpg, wpe, g1, gf)


def _bwd_seq(z, hl, dya, dyb, dm, cw, cb, wa, ba, wx, bx, lam, pw, ps, sib_grads, tm):
    nb, s, nc = z.shape
    lw = cw.shape[1]
    pwid = ps.shape[1]
    nt = s // tm
    zw = 2 * lw + 2 * pwid
    R = _SEQ_ROWS
    ns = len(sib_grads)

    def body(zq_ref, zh_ref, hl_ref, hlh_ref, dya_ref, dyb_ref, dm_ref, cw_ref, cb_ref, wa_ref, ba_ref, wx_ref, bx_ref,
             lam_ref, pw_ref, ps_ref, *rest):
        sib_src, rest = rest[:ns], rest[ns:]
        dz_ref, pack_ref = rest[:2]
        sib_out, rest = rest[2:2 + ns], rest[2 + ns:]
        (xa_ext, xb_ext, a_ext, an_buf, hl_ext, dh_buf, xc_buf, r_buf, i_buf, dxc_ext, q_ext, carry,
         send_sems, recv_sems) = rest
        b = pl.program_id(0)
        j = pl.program_id(1)
        jr = nt - 1 - j
        has_prev = jr > 0

        @pl.when((b == 0) & (j == 0))
        def _():
            for cp in _sibling_copies(sib_src, sib_out, send_sems, recv_sems):
                cp.start()

        dz_ref[:, zw:nc] = dm_ref[...]

        @pl.when((b == 0) & (j == 0))
        def _():
            pack_ref[...] = jnp.zeros_like(pack_ref)

        @pl.when(j == 0)
        def _():
            a_ext[pl.ds(tm, SUB), :] = jnp.zeros((SUB, lw), F32)
            dxc_ext[pl.ds(tm, HALO), :] = jnp.zeros((HALO, lw), F32)
            q_ext[pl.ds(tm, HALO), :] = jnp.zeros((HALO, pwid), F32)
            carry[...] = jnp.zeros_like(carry)

        xa_ext[pl.ds(0, HALO), :] = jnp.where(has_prev, zh_ref[:, 0:lw], 0.0)
        xa_ext[pl.ds(HALO, tm), :] = zq_ref[:, 0:lw]
        hl_ext[pl.ds(0, SUB), :] = jnp.where(has_prev, hlh_ref[...], 0.0)
        hl_ext[pl.ds(SUB, tm), :] = hl_ref[...]
        for h in range(N_HEADS):
            cols = slice(h * HEAD, (h + 1) * HEAD)
            xc = _conv_head(xa_ext, cw_ref, cb_ref, cols, tm)
            neg_c_sp = -LRU_C * _softplus_neg(lam_ref[:, cols])
            r, i, log_a = _lru_gates(xc, h, wa_ref, ba_ref, wx_ref, bx_ref, neg_c_sp)
            xc_buf[:, cols] = xc
            r_buf[:, cols] = r
            i_buf[:, cols] = i
            a_ext[pl.ds(0, tm), cols] = jnp.exp(log_a)
            ga = zq_ref[:, lw + h * HEAD:lw + (h + 1) * HEAD]
            sg = _sigmoid(ga)
            dyav = dya_ref[:, cols]
            dh_buf[:, cols] = dyav * (ga * sg)
            dga = (dyav * hl_ext[pl.ds(SUB, tm), cols]) * (sg * (1.0 + ga * (1.0 - sg)))
            dz_ref[:, lw + h * HEAD:lw + (h + 1) * HEAD] = dga.astype(dz_ref.dtype)
        an_buf[...] = _rows_ahead(a_ext[...], 1)[0:tm]

        rows = lax.broadcasted_iota(jnp.int32, (SUB, lw), 0)
        nch = tm // SUB

        def step(c, car):
            i0 = pl.multiple_of((nch - 1 - c) * SUB, SUB)
            cv = an_buf[pl.ds(i0, SUB), :]
            bv = dh_buf[pl.ds(i0, SUB), :]
            for sh in (1, 2, 4):
                m = rows < SUB - sh
                c_sh = jnp.where(m, pltpu.roll(cv, SUB - sh, 0), 1.0)
                b_sh = jnp.where(m, pltpu.roll(bv, SUB - sh, 0), 0.0)
                bv = cv * b_sh + bv
                cv = cv * c_sh
            hv = cv * car + bv
            dh_buf[pl.ds(i0, SUB), :] = hv
            return jnp.broadcast_to(hv[0:1, :], (SUB, lw))

        carry[...] = lax.fori_loop(0, nch, step, carry[...])

        for h in range(N_HEADS):
            cols = slice(h * HEAD, (h + 1) * HEAD)
            lam_h = lam_ref[:, cols]
            neg_c_sp = -LRU_C * _softplus_neg(lam_h)
            xc = xc_buf[:, cols]
            r = r_buf[:, cols]
            i = i_buf[:, cols]
            a = a_ext[pl.ds(0, tm), cols]
            mult = _lru_mult(neg_c_sp * r, a)
            dh = dh_buf[:, cols]
            dhx = dh * xc
            di = dhx * mult
            h_prev = _rows_back(hl_ext[:, cols], 1)[SUB:SUB + tm]
            dlog_a = (dh * h_prev) * a - (dhx * i) * ((a * a) / mult)
            pack_ref[pl.ds(R["dlam"][0] + h, 1), :] += _colsum(dlog_a * r) * (LRU_C * _sigmoid(-lam_h))
            dpa = (dlog_a * neg_c_sp) * (r * (1.0 - r))
            dpx = di * (i * (1.0 - i))
            dpa16 = dpa.astype(_MM)
            dpx16 = dpx.astype(_MM)
            xc16 = xc.astype(_MM)
            dxc = dh * (mult * i) + _dot_nt(dpa16, wa_ref[h]) + _dot_nt(dpx16, wx_ref[h])
            pack_ref[pl.ds(R["dwa"][0] + h * HEAD, HEAD), :] += _dot_tn(xc16, dpa16)
            pack_ref[pl.ds(R["dwx"][0] + h * HEAD, HEAD), :] += _dot_tn(xc16, dpx16)
            pack_ref[pl.ds(R["dba"][0] + h, 1), :] += _colsum(dpa)
            pack_ref[pl.ds(R["dbx"][0] + h, 1), :] += _colsum(dpx)
            pack_ref[pl.ds(R["dcb"][0] + h, 1), :] += _colsum(dxc)
            dxc_ext[pl.ds(0, tm), cols] = dxc

        for h in range(N_HEADS):
            cols = slice(h * HEAD, (h + 1) * HEAD)
            dxc_all = dxc_ext[:, cols]
            xa = xa_ext[pl.ds(HALO, tm), cols]
            dxa = None
            for k in range(4):
                dxc_k = _rows_ahead(dxc_all, 3 - k)[0:tm]
                pack_ref[pl.ds(R["dcw"][0] + SUB * k + h, 1), :] += _colsum(dxc_k * xa)
                term = cw_ref[pl.ds(k, 1), cols] * dxc_k
                dxa = term if dxa is None else dxa + term
            dz_ref[:, cols] = dxa.astype(dz_ref.dtype)

        xb_ext[pl.ds(0, HALO), :] = jnp.where(has_prev, zh_ref[:, 2 * lw:2 * lw + pwid], 0.0)
        xb_ext[pl.ds(HALO, tm), :] = zq_ref[:, 2 * lw:2 * lw + pwid]
        pos = jr * tm + lax.broadcasted_iota(jnp.int32, (tm, HEAD), 0)
        for g, k in enumerate(POOL_WINDOWS):
            cols = slice(g * HEAD, (g + 1) * HEAD)
            diff, cnt = _pool_diff(xb_ext, g, k, pos, tm)
            diff16 = diff.astype(_MM)
            yp = _dot(diff16, pw_ref[g])
            sc = ps_ref[:, cols]
            gb = zq_ref[:, 2 * lw + pwid + g * HEAD:2 * lw + pwid + (g + 1) * HEAD]
            sgb = _sigmoid(gb)
            dybv = dyb_ref[:, cols]
            dy_pool = dybv * (gb * sgb)
            dgb = (dybv * (yp * sc)) * (sgb * (1.0 + gb * (1.0 - sgb)))
            pack_ref[pl.ds(R["dps"][0] + g, 1), :] += _colsum(dy_pool * yp)
            dyp16 = (dy_pool * sc).astype(_MM)
            pack_ref[pl.ds(R["dpw"][0] + g * HEAD, HEAD), :] += _dot_tn(diff16, dyp16)
            ddiff = _dot_nt(dyp16, pw_ref[g])
            q_ext[pl.ds(0, tm), cols] = ddiff / cnt
            dxb = _window_sum(q_ext[:, cols], k, _rows_ahead)[0:tm] - ddiff
            dz_ref[:, 2 * lw + g * HEAD:2 * lw + (g + 1) * HEAD] = dxb.astype(dz_ref.dtype)
            dz_ref[:, 2 * lw + pwid + g * HEAD:2 * lw + pwid + (g + 1) * HEAD] = dgb.astype(dz_ref.dtype)

        a_ext[pl.ds(tm, SUB), :] = a_ext[pl.ds(0, SUB), :]
        dxc_ext[pl.ds(tm, HALO), :] = dxc_ext[pl.ds(0, HALO), :]
        q_ext[pl.ds(tm, HALO), :] = q_ext[pl.ds(0, HALO), :]

        @pl.when((b == nb - 1) & (j == nt - 1))
        def _():
            for cp in _sibling_copies(sib_src, sib_out, send_sems, recv_sems):
                cp.wait()

    rev = lambda w: pl.BlockSpec((None, tm, w), lambda b, j: (b, nt - 1 - j, 0))
    prev_rows = lambda rows, w: pl.BlockSpec(
        (None, rows, w), lambda b, j: (b, jnp.maximum((nt - 1 - j) * (tm // rows) - 1, 0), 0))
    return pl.pallas_call(
        body, name="bwd_seq", grid=(nb, nt),
        in_specs=[rev(zw), prev_rows(HALO, zw), rev(lw), prev_rows(SUB, lw), rev(lw), rev(pwid), rev(nc - zw),
                  _const_spec(cw.shape), _const_spec(cb.shape), _const_spec(wa.shape), _const_spec(ba.shape),
                  _const_spec(wx.shape), _const_spec(bx.shape), _const_spec(lam.shape), _const_spec(pw.shape),
                  _const_spec(ps.shape)] + [HBM_SPEC] * ns,
        out_specs=(rev(nc), pl.BlockSpec((SEQ_PACK_ROWS, HEAD), lambda b, j: (0, 0))) + (HBM_SPEC,) * ns,
        out_shape=(jax.ShapeDtypeStruct((nb, s, nc), _MM), jax.ShapeDtypeStruct((SEQ_PACK_ROWS, HEAD), F32))
        + tuple(jax.ShapeDtypeStruct((4,) + g.shape[1:], g.dtype) for g in sib_grads),
        scratch_shapes=[pltpu.VMEM((tm + HALO, lw), F32), pltpu.VMEM((tm + HALO, pwid), F32),
                        pltpu.VMEM((tm + SUB, lw), F32), pltpu.VMEM((tm, lw), F32), pltpu.VMEM((tm + SUB, lw), F32),
                        pltpu.VMEM((tm, lw), F32), pltpu.VMEM((tm, lw), F32), pltpu.VMEM((tm, lw), F32),
                        pltpu.VMEM((tm, lw), F32), pltpu.VMEM((tm + HALO, lw), F32),
                        pltpu.VMEM((tm + HALO, pwid), F32), pltpu.VMEM((SUB, lw), F32),
                        pltpu.SemaphoreType.DMA((ns, 4)), pltpu.SemaphoreType.DMA((ns, 4))],
        compiler_params=pltpu.CompilerParams(dimension_semantics=("arbitrary", "arbitrary"), vmem_limit_bytes=VMEM_LIMIT),
    )(z, z, hl, hl, dya, dyb, dm, cw, cb, wa, ba, wx, bx, lam, pw, ps, *sib_grads)


def _bwd_win(ht, dz, chip_srcs, small):
    nb, d, s = ht.shape
    nc = dz.shape[2]
    sec = nc // N_DEV
    n_chips = N_DEV // 2
    nm = len(chip_srcs)

    def body(ht_ref, dz_ref, *rest):
        chip_src, small_src = rest[:nm], rest[nm]
        gwin_ref, recv_ref = rest[nm + 1:nm + 3]
        chip_out, small_out = rest[nm + 3:2 * nm + 3], rest[2 * nm + 3]
        acc, local_sems, sib_send, sib_recv, chip_send, chip_recv, small_send, small_recv = rest[2 * nm + 4:]
        q = pl.program_id(0)
        b = pl.program_id(1)
        slot = q % 2
        x, y, c = _my_pos()

        def half(sl, core):
            return acc.at[sl, :, pl.ds(pl.multiple_of(core * sec, HEAD), sec)]

        def local_copy(qq, sl):
            return pltpu.make_async_copy(half(sl, c), gwin_ref.at[2 * qq + c], local_sems.at[sl])

        def sib_copy(qq, sl):
            return pltpu.make_async_remote_copy(
                src_ref=half(sl, 1 - c), dst_ref=recv_ref.at[qq], send_sem=sib_send.at[qq],
                recv_sem=sib_recv.at[qq], device_id=(x, y, 1 - c), device_id_type=MESH)

        def others():
            return (_chip_copies(chip_src, chip_out, chip_send, chip_recv)
                    + _sibling_copies([small_src], [small_out], small_send, small_recv))

        @pl.when((q == 0) & (b == 0))
        def _():
            for cp in others():
                cp.start()

        @pl.when((q >= 2) & (b == 0))
        def _():
            local_copy(q - 2, slot).wait()
            sib_copy(q - 2, slot).wait_send()

        part = _dot(ht_ref[b], dz_ref[...])

        @pl.when(b == 0)
        def _():
            acc[slot] = part

        @pl.when(b != 0)
        def _():
            acc[slot] += part

        @pl.when(b == nb - 1)
        def _():
            local_copy(q, slot).start()
            sib_copy(q, slot).start()

        @pl.when((q == n_chips - 1) & (b == nb - 1))
        def _():
            for qq in (n_chips - 2, n_chips - 1):
                local_copy(qq, qq % 2).wait()
                sib_copy(qq, qq % 2).wait_send()
            for qq in range(n_chips):
                sib_copy(qq, 0).wait_recv()
            for cp in others():
                cp.wait()

    return pl.pallas_call(
        body, name="bwd_win", grid=(n_chips, nb),
        in_specs=[_const_spec(ht.shape), pl.BlockSpec((None, s, 2 * sec), lambda q, b: (b, 0, q))]
        + [HBM_SPEC] * (nm + 1),
        out_specs=(HBM_SPEC,) * (nm + 3),
        out_shape=(jax.ShapeDtypeStruct((N_DEV, d, sec), F32), jax.ShapeDtypeStruct((n_chips, d, sec), F32))
        + tuple(jax.ShapeDtypeStruct((3,) + g.shape[1:], g.dtype) for g in chip_srcs)
        + (jax.ShapeDtypeStruct((4,) + small.shape[1:], small.dtype),),
        scratch_shapes=[pltpu.VMEM((2, d, 2 * sec), F32), pltpu.SemaphoreType.DMA((2,)),
                        pltpu.SemaphoreType.DMA((N_DEV // 2,)), pltpu.SemaphoreType.DMA((N_DEV // 2,)),
                        pltpu.SemaphoreType.DMA((nm, 3)), pltpu.SemaphoreType.DMA((nm, 3)),
                        pltpu.SemaphoreType.DMA((1, 4)), pltpu.SemaphoreType.DMA((1, 4))],
        compiler_params=pltpu.CompilerParams(dimension_semantics=("arbitrary", "arbitrary"), vmem_limit_bytes=VMEM_LIMIT),
    )(ht, dz, *chip_srcs, small)


def _bwd_dx(dz, x, dx1, win, g0, chip_srcs, tm):
    t, d = x.shape
    nc = win.shape[1]
    n = t // tm
    nm = len(chip_srcs)

    def body(dz_ref, x_ref, dx1_ref, win_ref, g0_ref, *rest):
        chip_src = rest[:nm]
        gx_ref, pack_ref = rest[nm:nm + 2]
        chip_out = rest[nm + 2:2 * nm + 2]
        chip_send, chip_recv = rest[2 * nm + 2:]
        i = pl.program_id(0)

        @pl.when(i == 0)
        def _():
            pack_ref[...] = jnp.zeros_like(pack_ref)
            for cp in _chip_copies(chip_src, chip_out, chip_send, chip_recv):
                cp.start()

        xv = x_ref[...]
        r0 = _rsqrt_mean_sq(xv)
        xh = xv * r0
        dh = _dot_nt(dz_ref[...], win_ref[...])
        _store_chunks(pack_ref, 0, _colsum(dh * xh))
        dxh = dh * g0_ref[...]
        gx_ref[...] = dx1_ref[...] + r0 * (dxh - xh * jnp.mean(dxh * xh, axis=-1, keepdims=True))

        @pl.when(i == n - 1)
        def _():
            for cp in _chip_copies(chip_src, chip_out, chip_send, chip_recv):
                cp.wait()

    tile = lambda w: pl.BlockSpec((tm, w), lambda i: (i, 0))
    return pl.pallas_call(
        body, name="bwd_dx", grid=(n,),
        in_specs=[tile(nc), tile(d), tile(d), _const_spec(win.shape), _const_spec(g0.shape)] + [HBM_SPEC] * nm,
        out_specs=(tile(d), pl.BlockSpec((INP_PACK_ROWS, HEAD), lambda i: (0, 0))) + (HBM_SPEC,) * nm,
        out_shape=(jax.ShapeDtypeStruct((t, d), F32), jax.ShapeDtypeStruct((INP_PACK_ROWS, HEAD), F32))
        + tuple(jax.ShapeDtypeStruct((3,) + g.shape[1:], g.dtype) for g in chip_srcs),
        scratch_shapes=[pltpu.SemaphoreType.DMA((nm, 3)), pltpu.SemaphoreType.DMA((nm, 3))],
        compiler_params=pltpu.CompilerParams(dimension_semantics=("arbitrary",), vmem_limit_bytes=VMEM_LIMIT),
    )(dz, x, dx1, win, g0, *chip_srcs)


def _all_gather(shards, full_shapes, slicers, name):
    nw = len(shards)

    def body(*refs):
        start, forward, finish = _gather_phases(refs[:nw], refs[nw:2 * nw], slicers, *refs[2 * nw:])
        start()
        forward()
        finish()

    return pl.pallas_call(
        body, name=name, in_specs=[HBM_SPEC] * nw, out_specs=(HBM_SPEC,) * nw,
        out_shape=tuple(jax.ShapeDtypeStruct(shp, a.dtype) for shp, a in zip(full_shapes, shards)),
        scratch_shapes=_gather_sems(nw),
    )(*shards)


ROW_BLOCKS = 8


def _chip_sums(core_idx, grads, recvs, dtypes, name):
    nw = len(grads)

    def body(idx_ref, *refs):
        gs, rs, outs = refs[:nw], refs[nw:2 * nw], refs[2 * nw:]
        for g, r, o in zip(gs, rs, outs):
            o[...] = (g[...] + r[...]).astype(o.dtype)

    def blk(g):
        return (None,) + g.shape[1:]

    return pl.pallas_call(
        body, name=name,
        grid_spec=pltpu.PrefetchScalarGridSpec(
            num_scalar_prefetch=1, grid=(4,),
            in_specs=[pl.BlockSpec(blk(g), lambda q, s: (2 * q + s[0], 0, 0)) for g in grads]
            + [pl.BlockSpec(blk(g), lambda q, s: (q, 0, 0)) for g in grads],
            out_specs=tuple(pl.BlockSpec(blk(g), lambda q, s: (q, 0, 0)) for g in grads)),
        out_shape=tuple(jax.ShapeDtypeStruct((4,) + g.shape[1:], dt) for g, dt in zip(grads, dtypes)),
    )(core_idx, *grads, *recvs)


def _adamw(w, g, m, v):
    m = ADAM_B1 * m + (1.0 - ADAM_B1) * g
    v = ADAM_B2 * v + (1.0 - ADAM_B2) * (g * g)
    m_hat = m / (1.0 - ADAM_B1 ** ADAM_STEP)
    v_hat = v / (1.0 - ADAM_B2 ** ADAM_STEP)
    delta = -ADAM_LR * (m_hat / (jnp.sqrt(v_hat) + ADAM_EPS) + ADAM_WD * w)
    return delta, m, v


def _adam_sections(sec_idx, grads, recv_sib, recv_chips, wmv):
    nw = len(wmv)

    def body(idx_ref, *refs):
        gs, rs, cs = refs[:nw + 1], refs[nw + 1:2 * nw + 2], refs[2 * nw + 2:3 * nw + 3]
        params = refs[3 * nw + 3:6 * nw + 3]
        outs = refs[6 * nw + 3:]
        for w in range(nw + 1):
            g = gs[w][...] + rs[w][...]
            for k in range(3):
                g = g + cs[w][k].astype(F32)
            if w == nw:
                outs[4 * nw][...] = g
                continue
            wv, mv, vv = (params[3 * w + t][...] for t in range(3))
            delta, m_new, v_new = _adamw(wv, g, mv, vv)
            outs[4 * w][...] = g
            outs[4 * w + 1][...] = delta
            outs[4 * w + 2][...] = m_new
            outs[4 * w + 3][...] = v_new

    def rb(g):
        return g.shape[1] // ROW_BLOCKS

    in_specs = [pl.BlockSpec((None, rb(g), g.shape[2]), lambda i, s: (s[0], i, 0)) for g in grads]
    in_specs += [pl.BlockSpec((None, rb(g), g.shape[2]), lambda i, s: (s[1], i, 0)) for g in grads]
    in_specs += [pl.BlockSpec((3, rb(g), g.shape[2]), lambda i, s: (0, i, 0)) for g in grads]
    sec = lambda g: pl.BlockSpec((rb(g), g.shape[2]), lambda i, s: (i, 0))
    for w in range(nw):
        in_specs += [sec(grads[w])] * 3
    out_specs, out_shape = [], []
    for g in grads[:nw]:
        out_specs += [sec(g)] * 4
        out_shape += [jax.ShapeDtypeStruct(g.shape[1:], F32)] * 4
    out_specs.append(sec(grads[nw]))
    out_shape.append(jax.ShapeDtypeStruct(grads[nw].shape[1:], F32))
    flat = [a for t in wmv for a in t]
    outs = pl.pallas_call(
        body, name="adam_sections",
        grid_spec=pltpu.PrefetchScalarGridSpec(num_scalar_prefetch=1, grid=(ROW_BLOCKS,), in_specs=in_specs,
                                               out_specs=tuple(out_specs)),
        out_shape=tuple(out_shape),
    )(sec_idx, *grads, *recv_sib, *recv_chips, *flat)
    return [outs[4 * w:4 * w + 4] for w in range(nw)], outs[4 * nw]


def _adam_small(sec_idx, gpack, parts, parts_wmv, vec_params, mat_params, conv_wmv):
    items = [(0, parts_wmv, "parts")]
    items += [(r0, t, "vec") for r0, t in vec_params] + [(r0, t, "mat") for r0, t in mat_params]
    items.append((_SEQ_ROWS["dcw"][0], conv_wmv, "conv"))

    def body(idx_ref, g_ref, parts_ref, *refs):
        ins, outs = refs[:3 * len(items)], refs[3 * len(items):]
        for n, (r0, _, kind) in enumerate(items):
            w_ref, m_ref, v_ref = ins[3 * n:3 * n + 3]
            o = outs[4 * n:4 * n + 4]
            if kind == "mat":
                g = g_ref[pl.ds(r0, w_ref.shape[0]), :]
                res = (g,) + _adamw(w_ref[...], g, m_ref[...], v_ref[...])
                for ref, val in zip(o, res):
                    ref[...] = val
            elif kind in ("vec", "parts"):
                for h in range(w_ref.shape[1] // HEAD):
                    cols = slice(h * HEAD, (h + 1) * HEAD)
                    if kind == "vec":
                        g = g_ref[pl.ds(r0 + h, 1), :]
                    else:
                        g = parts_ref[pl.ds(h, 1), :]
                        for dev in range(1, N_DEV):
                            g = g + parts_ref[pl.ds(SUB * dev + h, 1), :]
                    res = (g,) + _adamw(w_ref[:, cols], g, m_ref[:, cols], v_ref[:, cols])
                    for ref, val in zip(o, res):
                        ref[:, cols] = val
            else:
                rows = lax.broadcasted_iota(jnp.int32, (SUB, HEAD), 0)
                for k in range(4):
                    blk = g_ref[pl.ds(r0 + SUB * k, SUB), :]
                    g = jnp.sum(jnp.where(rows == idx_ref[0], blk, 0.0), axis=0, keepdims=True)
                    row = pl.ds(k, 1)
                    res = (g,) + _adamw(w_ref[row, :], g, m_ref[row, :], v_ref[row, :])
                    for ref, val in zip(o, res):
                        ref[row, :] = val
        outs[4 * len(items)][...] = g_ref[pl.ds(_OFF_MID + MID_LOSS_ROW, 1), :]

    flat = [a for _, t, _ in items for a in t]
    full = lambda a: pl.BlockSpec(a.shape, lambda i, s: (0,) * a.ndim)
    out_specs, out_shape = [], []
    for _, t, _ in items:
        out_specs += [full(t[0])] * 4
        out_shape += [jax.ShapeDtypeStruct(t[0].shape, F32)] * 4
    loss_row = jax.ShapeDtypeStruct((1, HEAD), F32)
    out_specs.append(full(loss_row))
    out_shape.append(loss_row)
    outs = pl.pallas_call(
        body, name="adam_small",
        grid_spec=pltpu.PrefetchScalarGridSpec(num_scalar_prefetch=1, grid=(1,),
                                               in_specs=[full(gpack), full(parts)] + [full(a) for a in flat],
                                               out_specs=tuple(out_specs)),
        out_shape=tuple(out_shape),
    )(sec_idx, gpack, parts, *flat)
    return [outs[4 * n:4 * n + 4] for n in range(len(items))], outs[4 * len(items)]


def _col_slicer(width):
    return lambda ref, idx: ref.at[:, pl.ds(pl.multiple_of(idx * width, HEAD), width)]


def _row_slicer(rows):
    return lambda ref, idx: ref.at[pl.ds(pl.multiple_of(idx * rows, SUB), rows), :]


def kernel(x, p, norm_g, w_in, conv_w, conv_b, lru_w_a, lru_b_a, lru_w_x, lru_b_x, lru_lambda, pool_w, pool_scale, w_proj_lru, w_proj_pool, w_out, ple_norm_g, w_ple_gate, w_ple_proj, final_g, loss_target, m_norm_g, m_w_in, m_conv_w, m_conv_b, m_lru_w_a, m_lru_b_a, m_lru_w_x, m_lru_b_x, m_lru_lambda, m_pool_w, m_pool_scale, m_w_proj_lru, m_w_proj_pool, m_w_out, m_ple_norm_g, m_w_ple_gate, m_w_ple_proj, m_final_g, v_norm_g, v_w_in, v_conv_w, v_conv_b, v_lru_w_a, v_lru_b_a, v_lru_w_x, v_lru_b_x, v_lru_lambda, v_pool_w, v_pool_scale, v_w_proj_lru, v_w_proj_pool, v_w_out, v_ple_norm_g, v_w_ple_gate, v_w_ple_proj, v_final_g):
    nb, s, d = x.shape
    t = nb * s
    tm = min(TILE_M, s)
    sec_idx = (4 * lax.axis_index("x") + 2 * lax.axis_index("y") + lax.axis_index("c")).astype(jnp.int32)
    chip_idx = (2 * lax.axis_index("x") + lax.axis_index("y")).astype(jnp.int32)
    core_idx = lax.axis_index("c").astype(jnp.int32)

    h, ht, win, cw = _pre(
        x, norm_g, [w_in[0].astype(_MM), conv_w[0]],
        [(d, N_DEV * w_in.shape[2]), (conv_w.shape[1], N_DEV * conv_w.shape[2])],
        [_col_slicer(w_in.shape[2]), _col_slicer(conv_w.shape[2])], tm)
    late_shards = [w_proj_lru[0].astype(_MM), w_proj_pool[0].astype(_MM), w_out[0].astype(_MM),
                   w_ple_gate[0].astype(_MM), w_ple_proj[0].astype(_MM)]
    late_shapes = [(N_DEV * w_proj_lru.shape[1], d), (w_proj_pool.shape[1], N_DEV * w_proj_pool.shape[2]),
                   (N_DEV * w_out.shape[1], d), (N_DEV * w_ple_gate.shape[1], d),
                   (w_ple_proj.shape[1], N_DEV * w_ple_proj.shape[2])]
    late_slicers = [_row_slicer(w_proj_lru.shape[1]), _col_slicer(w_proj_pool.shape[2]), _row_slicer(w_out.shape[1]),
                    _row_slicer(w_ple_gate.shape[1]), _col_slicer(w_ple_proj.shape[2])]

    lw = cw.shape[1]
    pwid = pool_scale.shape[1]
    wa = lru_w_a[0].astype(_MM)
    wx = lru_w_x[0].astype(_MM)
    pw = pool_w[0].astype(_MM)
    ba, bx = lru_b_a[0], lru_b_x[0]
    gf = final_g.reshape(1, d)
    rows_of = lambda g: g.reshape(N_DEV, g.shape[0] // N_DEV, g.shape[1])
    core = core_idx.reshape(1)

    z, hl, ya, yb, wpl, wpp, wout, wpg, wpe = _fwd_seq(
        h, win, cw, conv_b, wa, ba, wx, bx, lru_lambda, pw, pool_scale, late_shards, late_shapes, late_slicers, tm)
    x2d = x.reshape(t, d)
    (dya, dyb, dm, dx1, g_pl, g_pp, g_out, g_pg, g_pe, pack_mid) = _mid(
        ya.reshape(t, lw), yb.reshape(t, pwid), z.reshape(t, -1), x2d, p[0].reshape(t, -1), loss_target.reshape(t, d),
        wpl, wpp, wout, wpg, wpe, ple_norm_g, gf, tm)
    grads_mid = [rows_of(g_pl), g_pp, rows_of(g_out), rows_of(g_pg), g_pe]
    seq_out = _bwd_seq(z, hl, dya.reshape(nb, s, lw), dyb.reshape(nb, s, pwid), dm.reshape(nb, s, -1), cw, conv_b,
                       wa, ba, wx, bx, lru_lambda, pw, pool_scale, grads_mid, tm)
    dz, pack_seq, sib_mid = seq_out[0], seq_out[1], list(seq_out[2:])
    sums_mid = _chip_sums(core, grads_mid, sib_mid, [_WIRE] * len(grads_mid), "chip_sums_mid")
    small = jnp.concatenate(
        [pack_seq, pack_mid, jnp.zeros((SMALL_ROWS - _OFF_MID - MID_PACK_ROWS, HEAD), F32)], axis=0
    ).reshape(N_DEV, SMALL_SEC, HEAD)
    win_out = _bwd_win(ht, dz, sums_mid, small)
    g_in, sib_in, chips_mid, sib_small = win_out[0], win_out[1], list(win_out[2:-1]), win_out[-1]
    sums_in = _chip_sums(core, [g_in, small], [sib_in, sib_small], [_WIRE, F32], "chip_sums_in")
    dx_out = _bwd_dx(dz.reshape(t, -1), x2d, dx1, win, norm_g, sums_in, tm)
    grad_x, pack_inp, chips_in, chips_small = dx_out

    grads = [g_in] + grads_mid + [small]
    recv_sib = [sib_in] + sib_mid + [sib_small]
    recv_chips = [chips_in] + chips_mid + [chips_small]
    wmv = [(w_in[0], m_w_in[0], v_w_in[0]), (w_proj_lru[0], m_w_proj_lru[0], v_w_proj_lru[0]),
           (w_proj_pool[0], m_w_proj_pool[0], v_w_proj_pool[0]), (w_out[0], m_w_out[0], v_w_out[0]),
           (w_ple_gate[0], m_w_ple_gate[0], v_w_ple_gate[0]), (w_ple_proj[0], m_w_ple_proj[0], v_w_ple_proj[0])]
    big, small_sec = _adam_sections(jnp.stack([sec_idx, chip_idx]), grads, recv_sib, recv_chips, wmv)
    gsmall, g0_parts = _all_gather([small_sec, pack_inp], [(SMALL_ROWS, HEAD), (N_DEV * INP_PACK_ROWS, HEAD)],
                                   [_row_slicer(SMALL_SEC), _row_slicer(INP_PACK_ROWS)], "gather_small")

    R = _SEQ_ROWS
    vec = lambda r0, *t: (r0, tuple(a.reshape(1, -1) for a in t))
    mat = lambda r0, *t: (r0, tuple(a.reshape(-1, HEAD) for a in t))
    norm_wmv = vec(0, norm_g, m_norm_g, v_norm_g)[1]
    vec_params = [vec(R["dcb"][0], conv_b, m_conv_b, v_conv_b),
                  vec(R["dlam"][0], lru_lambda, m_lru_lambda, v_lru_lambda),
                  vec(R["dps"][0], pool_scale, m_pool_scale, v_pool_scale),
                  vec(_OFF_MID, ple_norm_g, m_ple_norm_g, v_ple_norm_g),
                  vec(_OFF_MID + SUB, final_g, m_final_g, v_final_g)]
    mat_params = [mat(R["dwa"][0], lru_w_a, m_lru_w_a, v_lru_w_a), mat(R["dba"][0], lru_b_a, m_lru_b_a, v_lru_b_a),
                  mat(R["dwx"][0], lru_w_x, m_lru_w_x, v_lru_w_x), mat(R["dbx"][0], lru_b_x, m_lru_b_x, v_lru_b_x),
                  mat(R["dpw"][0], pool_w, m_pool_w, v_pool_w)]
    conv_wmv = (conv_w[0], m_conv_w[0], v_conv_w[0])
    small_out, loss_row = _adam_small(sec_idx.reshape(1), gsmall, g0_parts, norm_wmv, vec_params, mat_params, conv_wmv)
    loss = loss_row[0, 0]

    res = {}
    names_small = ["norm_g", "conv_b", "lru_lambda", "pool_scale", "ple_norm_g", "final_g",
                   "lru_w_a", "lru_b_a", "lru_w_x", "lru_b_x", "pool_w", "conv_w"]
    shapes = {"norm_g": norm_g, "conv_b": conv_b, "lru_lambda": lru_lambda, "pool_scale": pool_scale,
              "ple_norm_g": ple_norm_g, "final_g": final_g, "lru_w_a": lru_w_a, "lru_b_a": lru_b_a, "lru_w_x": lru_w_x,
              "lru_b_x": lru_b_x, "pool_w": pool_w, "conv_w": conv_w}
    for name, quad in zip(names_small, small_out):
        res[name] = [a.reshape(shapes[name].shape) for a in quad]
    names_big = ["w_in", "w_proj_lru", "w_proj_pool", "w_out", "w_ple_gate", "w_ple_proj"]
    for name, quad, (w, _, _) in zip(names_big, big, wmv):
        res[name] = [a.reshape((1,) + w.shape) for a in quad]
    order = ["norm_g", "w_in", "conv_w", "conv_b", "lru_w_a", "lru_b_a", "lru_w_x", "lru_b_x", "lru_lambda", "pool_w",
             "pool_scale", "w_proj_lru", "w_proj_pool", "w_out", "ple_norm_g", "w_ple_gate", "w_ple_proj", "final_g"]
    out = [loss, grad_x.reshape(nb, s, d)]
    for kind in range(4):
        out += [res[name][kind] for name in order]
    return tuple(out)
```

```python
import jax
import jax.numpy as jnp
from jax import lax
from jax.experimental import pallas as pl
from jax.experimental.pallas import tpu as pltpu

F32 = jnp.float32
_MM = jnp.bfloat16
_WIRE = jnp.bfloat16

EPS = 1e-6
LRU_C = 8.0
MULT_SQ_FLOOR = 1e-30
POOL_WINDOWS = (2, 4, 8, 16)
N_HEADS = 8
HEAD = 128
HALO = 16
SUB = 8

ADAM_LR = 0.001
ADAM_B1 = 0.9
ADAM_B2 = 0.999
ADAM_EPS = 1e-08
ADAM_WD = 0.01
ADAM_STEP = 10

N_DEV = 8
MESH = pl.DeviceIdType.MESH
VMEM_LIMIT = 60 * 1024 * 1024
TILE_M = 256

_SEQ_ROWS = {"dwa": (0, 1024), "dwx": (1024, 1024), "dpw": (2048, 512), "dba": (2560, 8), "dbx": (2568, 8),
             "dcb": (2576, 8), "dlam": (2584, 8), "dcw": (2592, 32), "dps": (2624, 8)}
SEQ_PACK_ROWS = 2632
MID_PACK_ROWS = 24
MID_LOSS_ROW = 16
INP_PACK_ROWS = 8
SMALL_ROWS = 3072
SMALL_SEC = SMALL_ROWS // N_DEV
_OFF_MID = SEQ_PACK_ROWS


def _dot(a, b):
    return jnp.dot(a, b, preferred_element_type=F32)


def _dot_nt(a, b):
    return lax.dot_general(a, b, (((1,), (1,)), ((), ())), preferred_element_type=F32)


def _dot_tn(a, b):
    return lax.dot_general(a, b, (((0,), (0,)), ((), ())), preferred_element_type=F32)


def _sigmoid(v):
    return 0.5 * jnp.tanh(0.5 * v) + 0.5


def _lru_mult(log_a, a):
    m2 = jnp.maximum(-jnp.tanh(log_a) * (1.0 + a * a), MULT_SQ_FLOOR)
    inv = lax.rsqrt(m2)
    return m2 * inv, inv


def _rows_back(v, n):
    return pltpu.roll(v, n, 0) if n else v


def _rows_ahead(v, n):
    return pltpu.roll(v, v.shape[0] - n, 0) if n else v


def _softplus_neg(lam):
    e = jnp.exp(-jnp.abs(lam))
    w = 1.0 + e
    l1p = jnp.where(w == 1.0, e, jnp.log(w) * (e / (w - 1.0)))
    return jnp.maximum(-lam, 0.0) + l1p


def _rsqrt_mean_sq(v):
    return lax.rsqrt(jnp.mean(v * v, axis=-1, keepdims=True) + EPS)


def _colsum(v):
    return jnp.sum(v, axis=0, keepdims=True)


def _const_spec(shape):
    nd = len(shape)
    return pl.BlockSpec(shape, lambda *_: (0,) * nd, pipeline_mode=pl.Buffered(1))


HBM_SPEC = pl.BlockSpec(memory_space=pl.ANY)


def _my_pos():
    return lax.axis_index("x"), lax.axis_index("y"), lax.axis_index("c")


def _other_chips(x, y):
    return [(1 - x, y), (x, 1 - y), (1 - x, 1 - y)]


def _gather_sems(nw):
    return [pltpu.SemaphoreType.DMA((nw, 7)), pltpu.SemaphoreType.DMA((nw, 7)), pltpu.SemaphoreType.DMA((nw,))]


def _gather_phases(srcs, outs, slicers, send_sems, recv_sems, local_sems):
    nw = len(srcs)
    x, y, c = _my_pos()
    me, sibling = (x, y, c), (x, y, 1 - c)
    chips = _other_chips(x, y)

    def part(w, pos):
        return slicers[w](outs[w], 4 * pos[0] + 2 * pos[1] + pos[2])

    def copy(w, k, block, to, src=None):
        return pltpu.make_async_remote_copy(
            src_ref=part(w, block) if src is None else src, dst_ref=part(w, block),
            send_sem=send_sems.at[w, k], recv_sem=recv_sems.at[w, k], device_id=to, device_id_type=MESH)

    def mine():
        return [pltpu.make_async_copy(srcs[w], part(w, me), local_sems.at[w]) for w in range(nw)]

    def first():
        out = []
        for w in range(nw):
            out.append(copy(w, 0, me, sibling, src=srcs[w]))
            out += [copy(w, 1 + j, me, (*chip, c), src=srcs[w]) for j, chip in enumerate(chips)]
        return out

    def passed():
        return [copy(w, 4 + j, (*chip, c), sibling) for j, chip in enumerate(chips) for w in range(nw)]

    def start():
        for cp in mine() + first():
            cp.start()

    def forward():
        for j, chip in enumerate(chips):
            for w in range(nw):
                copy(w, 1 + j, (*chip, c), me).wait_recv()
                copy(w, 4 + j, (*chip, c), sibling).start()

    def finish():
        for w in range(nw):
            copy(w, 0, sibling, me).wait_recv()
            for j, chip in enumerate(chips):
                copy(w, 4 + j, (*chip, 1 - c), me).wait_recv()
        for cp in first() + passed():
            cp.wait_send()
        for cp in mine():
            cp.wait()

    return start, forward, finish


def _sibling_copies(srcs, outs, send_sems, recv_sems):
    x, y, c = _my_pos()
    return [pltpu.make_async_remote_copy(
        src_ref=srcs[w].at[2 * q + (1 - c)], dst_ref=outs[w].at[q], send_sem=send_sems.at[w, q],
        recv_sem=recv_sems.at[w, q], device_id=(x, y, 1 - c), device_id_type=MESH)
        for w in range(len(srcs)) for q in range(4)]


def _chip_copies(srcs, outs, send_sems, recv_sems):
    x, y, c = _my_pos()
    return [pltpu.make_async_remote_copy(
        src_ref=srcs[w].at[2 * px + py], dst_ref=outs[w].at[k], send_sem=send_sems.at[w, k],
        recv_sem=recv_sems.at[w, k], device_id=(px, py, c), device_id_type=MESH)
        for w in range(len(srcs)) for k, (px, py) in enumerate(_other_chips(x, y))]


def _lru_gates(xc, h, wa_ref, ba_ref, wx_ref, bx_ref, neg_c_sp):
    xc16 = xc.astype(_MM)
    r = _sigmoid(_dot(xc16, wa_ref[h]) + ba_ref[pl.ds(h, 1), :])
    i = _sigmoid(_dot(xc16, wx_ref[h]) + bx_ref[pl.ds(h, 1), :])
    return r, i, neg_c_sp * r


def _conv_head(xa_ext, cw_ref, cb_ref, cols, tm):
    ext = xa_ext[:, cols]
    xc = cb_ref[:, cols] + cw_ref[pl.ds(3, 1), cols] * ext[HALO:HALO + tm]
    for k in range(3):
        xc = xc + cw_ref[pl.ds(k, 1), cols] * _rows_back(ext, 3 - k)[HALO:HALO + tm]
    return xc


def _window_sum(ext, k, shift):
    n = 1
    while n < k:
        ext = ext + shift(ext, n)
        n *= 2
    return ext


def _pool_diff(xb_ext, g, k, pos, tm):
    cols = slice(g * HEAD, (g + 1) * HEAD)
    ext = xb_ext[:, cols]
    ws = _window_sum(ext, k, _rows_back)[HALO:HALO + tm]
    inv_cnt = 1.0 / jnp.minimum(pos + 1, k).astype(F32)
    return ws * inv_cnt - ext[HALO:HALO + tm], inv_cnt


def _pre(x, g0, shards, shapes, slicers, tm):
    nb, s, d = x.shape
    nt = s // tm
    nl = len(shards)

    def body(x_ref, g0_ref, *rest):
        src, rest = rest[:nl], rest[nl:]
        h_ref, ht_ref = rest[:2]
        out, (send_sems, recv_sems, local_sems) = rest[2:2 + nl], rest[2 + nl:]
        step_no = pl.program_id(0) * nt + pl.program_id(1)
        g_start, g_forward, g_finish = _gather_phases(src, out, slicers, send_sems, recv_sems, local_sems)
        pl.when(step_no == 0)(g_start)
        xv = x_ref[...]
        h16 = ((xv * _rsqrt_mean_sq(xv)) * g0_ref[...]).astype(_MM)
        h_ref[...] = h16
        ht_ref[...] = h16.T

        @pl.when(step_no == nb * nt - 1)
        def _():
            g_forward()
            g_finish()

    tile = pl.BlockSpec((None, tm, d), lambda b, j: (b, j, 0))
    return pl.pallas_call(
        body, name="pre", grid=(nb, nt),
        in_specs=[tile, _const_spec(g0.shape)] + [HBM_SPEC] * nl,
        out_specs=(tile, pl.BlockSpec((None, d, tm), lambda b, j: (b, 0, j))) + (HBM_SPEC,) * nl,
        out_shape=(jax.ShapeDtypeStruct((nb, s, d), _MM), jax.ShapeDtypeStruct((nb, d, s), _MM))
        + tuple(jax.ShapeDtypeStruct(shp, a.dtype) for shp, a in zip(shapes, shards)),
        scratch_shapes=_gather_sems(nl),
        compiler_params=pltpu.CompilerParams(dimension_semantics=("arbitrary", "arbitrary")),
    )(x, g0, *shards)


def _fwd_seq(h, win, cw, cb, wa, ba, wx, bx, lam, pw, ps, late_shards, late_shapes, late_slicers, tm):
    nb, s, d = h.shape
    lw = cw.shape[1]
    pwid = ps.shape[1]
    nc = 2 * lw + 2 * pwid
    nt = s // tm
    nl = len(late_shards)
    n_steps = nb * nt

    def body(h_ref, win_ref, cw_ref, cb_ref, wa_ref, ba_ref, wx_ref, bx_ref, lam_ref, pw_ref, ps_ref, *rest):
        late_src, rest = rest[:nl], rest[nl:]
        z_ref, hl_ref, ya_ref, yb_ref = rest[:4]
        late_out, rest = rest[4:4 + nl], rest[4 + nl:]
        xa_ext, xb_ext, a_buf, carry, send_sems, recv_sems, local_sems = rest
        j = pl.program_id(1)
        step_no = pl.program_id(0) * nt + j
        g_start, g_forward, g_finish = _gather_phases(late_src, late_out, late_slicers, send_sems, recv_sems, local_sems)
        pl.when(step_no == 0)(g_start)
        pl.when(step_no == n_steps // 2)(g_forward)

        @pl.when(j == 0)
        def _():
            xa_ext[pl.ds(0, HALO), :] = jnp.zeros((HALO, lw), F32)
            xb_ext[pl.ds(0, HALO), :] = jnp.zeros((HALO, pwid), F32)
            carry[...] = jnp.zeros_like(carry)

        h16 = h_ref[...]
        xa = _dot(h16, win_ref[:, 0:lw])
        z_ref[:, 0:lw] = xa
        xa_ext[pl.ds(HALO, tm), :] = xa
        for c0 in range(lw, nc, lw):
            z_ref[:, c0:c0 + lw] = _dot(h16, win_ref[:, c0:c0 + lw])

        for h in range(N_HEADS):
            cols = slice(h * HEAD, (h + 1) * HEAD)
            xc = _conv_head(xa_ext, cw_ref, cb_ref, cols, tm)
            neg_c_sp = -LRU_C * _softplus_neg(lam_ref[:, cols])
            _, i, log_a = _lru_gates(xc, h, wa_ref, ba_ref, wx_ref, bx_ref, neg_c_sp)
            a = jnp.exp(log_a)
            a_buf[:, cols] = a
            hl_ref[:, cols] = _lru_mult(log_a, a)[0] * (i * xc)

        rows = lax.broadcasted_iota(jnp.int32, (SUB, lw), 0)

        def step(c, car):
            i0 = pl.multiple_of(c * SUB, SUB)
            av = a_buf[pl.ds(i0, SUB), :]
            bv = hl_ref[pl.ds(i0, SUB), :]
            for sh in (1, 2, 4):
                m = rows >= sh
                a_sh = jnp.where(m, pltpu.roll(av, sh, 0), 1.0)
                b_sh = jnp.where(m, pltpu.roll(bv, sh, 0), 0.0)
                bv = av * b_sh + bv
                av = av * a_sh
            hv = av * car + bv
            hl_ref[pl.ds(i0, SUB), :] = hv
            return jnp.broadcast_to(hv[SUB - 1:SUB, :], (SUB, lw))

        carry[...] = lax.fori_loop(0, tm // SUB, step, carry[...])

        ga = z_ref[:, lw:2 * lw]
        ya_ref[...] = (hl_ref[...] * (ga * _sigmoid(ga))).astype(ya_ref.dtype)

        xb_ext[pl.ds(HALO, tm), :] = z_ref[:, 2 * lw:2 * lw + pwid]
        pos = j * tm + lax.broadcasted_iota(jnp.int32, (tm, HEAD), 0)
        for g, k in enumerate(POOL_WINDOWS):
            cols = slice(g * HEAD, (g + 1) * HEAD)
            diff, _ = _pool_diff(xb_ext, g, k, pos, tm)
            yp = _dot(diff.astype(_MM), pw_ref[g])
            gb = z_ref[:, 2 * lw + pwid + g * HEAD:2 * lw + pwid + (g + 1) * HEAD]
            yb_ref[:, cols] = ((yp * ps_ref[:, cols]) * (gb * _sigmoid(gb))).astype(yb_ref.dtype)

        xa_ext[pl.ds(0, HALO), :] = xa_ext[pl.ds(tm, HALO), :]
        xb_ext[pl.ds(0, HALO), :] = xb_ext[pl.ds(tm, HALO), :]
        pl.when(step_no == n_steps - 1)(g_finish)

    tile = lambda w: pl.BlockSpec((None, tm, w), lambda b, j: (b, j, 0))
    return pl.pallas_call(
        body, name="fwd_seq", grid=(nb, nt),
        in_specs=[tile(d), _const_spec((d, nc)), _const_spec(cw.shape), _const_spec(cb.shape),
                  _const_spec(wa.shape), _const_spec(ba.shape), _const_spec(wx.shape), _const_spec(bx.shape),
                  _const_spec(lam.shape), _const_spec(pw.shape), _const_spec(ps.shape)] + [HBM_SPEC] * nl,
        out_specs=(tile(nc), tile(lw), tile(lw), tile(pwid)) + (HBM_SPEC,) * nl,
        out_shape=(jax.ShapeDtypeStruct((nb, s, nc), F32), jax.ShapeDtypeStruct((nb, s, lw), F32),
                   jax.ShapeDtypeStruct((nb, s, lw), _MM), jax.ShapeDtypeStruct((nb, s, pwid), _MM))
        + tuple(jax.ShapeDtypeStruct(shp, a.dtype) for shp, a in zip(late_shapes, late_shards)),
        scratch_shapes=[pltpu.VMEM((tm + HALO, lw), F32), pltpu.VMEM((tm + HALO, pwid), F32),
                        pltpu.VMEM((tm, lw), F32), pltpu.VMEM((SUB, lw), F32)] + _gather_sems(nl),
        compiler_params=pltpu.CompilerParams(dimension_semantics=("arbitrary", "arbitrary"), vmem_limit_bytes=VMEM_LIMIT),
    )(h, win, cw, cb, wa, ba, wx, bx, lam, pw, ps, *late_shards)


def _store_chunks(ref, row0, vec):
    for h in range(vec.shape[1] // HEAD):
        ref[pl.ds(row0 + h, 1), :] += vec[:, h * HEAD:(h + 1) * HEAD]


def _mid(ya, yb, h, wmerge, x, p, tgt, wpl, wpp, wout, wpg, wpe, g1, gf, tm):
    t, d = x.shape
    lw = ya.shape[1]
    pwid = yb.shape[1]
    pdim = p.shape[1]
    n = t // tm
    sec_pp = wpp.shape[1] // N_DEV
    sec_pe = wpe.shape[1] // N_DEV

    def body(ya_ref, yb_ref, h_ref, wmerge_ref, x_ref, p_ref, tgt_ref, wpl_ref, wpp_ref, wout_ref, wpg_ref, wpe_ref,
             g1_ref, gf_ref,
             dya_ref, dyb_ref, dm_ref, dx1_ref, gpl_ref, gpp_ref, gout_ref, gpg_ref, gpe_ref, pack_ref,
             acc_pl, acc_pp, acc_out, acc_pg, acc_pe):
        i = pl.program_id(0)

        @pl.when(i == 0)
        def _():
            for acc in (acc_pl, acc_pp, acc_out, acc_pg, acc_pe):
                acc[...] = jnp.zeros_like(acc)
            pack_ref[...] = jnp.zeros_like(pack_ref)

        ya16 = ya_ref[...]
        yb16 = yb_ref[...]
        br_a = _dot(ya16, wpl_ref[...])
        br_b = _dot(yb16, wpp_ref[...])
        logits = _dot(h_ref[...], wmerge_ref[...])
        sa = _sigmoid(logits[:, 0:d])
        sb = _sigmoid(logits[:, d:2 * d])
        mg16 = (sa * br_a + sb * br_b).astype(_MM)
        x1 = x_ref[...] + _dot(mg16, wout_ref[...])
        r1 = _rsqrt_mean_sq(x1)
        n1 = x1 * r1
        h116 = (n1 * g1_ref[...]).astype(_MM)
        gate = _sigmoid(_dot(h116, wpg_ref[...]))
        p16 = p_ref[...].astype(_MM)
        pe = _dot(p16, wpe_ref[...])
        x2 = x1 + gate * pe
        r2 = _rsqrt_mean_sq(x2)
        n2 = x2 * r2
        err = n2 * gf_ref[...] - tgt_ref[...]
        sq = jnp.sum(_colsum(err * err), axis=1, keepdims=True)
        pack_ref[pl.ds(MID_LOSS_ROW, 1), :] += jnp.broadcast_to(sq * (0.5 / d), (1, HEAD))

        dy = err * (1.0 / d)
        _store_chunks(pack_ref, SUB, _colsum(dy * n2))
        dn2 = dy * gf_ref[...]
        dx2 = r2 * (dn2 - n2 * jnp.mean(dn2 * n2, axis=-1, keepdims=True))
        dpe16 = (dx2 * gate).astype(_MM)
        dpg16 = ((dx2 * pe) * (gate * (1.0 - gate))).astype(_MM)
        acc_pe[...] += _dot_tn(p16, dpe16)
        acc_pg[...] += _dot_tn(h116, dpg16)
        dh1 = _dot_nt(dpg16, wpg_ref[...])
        _store_chunks(pack_ref, 0, _colsum(dh1 * n1))
        dn1 = dh1 * g1_ref[...]
        dx1 = dx2 + r1 * (dn1 - n1 * jnp.mean(dn1 * n1, axis=-1, keepdims=True))
        dx1_ref[...] = dx1
        dx116 = dx1.astype(_MM)
        acc_out[...] += _dot_tn(mg16, dx116)
        dmg = _dot_nt(dx116, wout_ref[...])
        da16 = (dmg * sa).astype(_MM)
        db16 = (dmg * sb).astype(_MM)
        dm_ref[:, 0:d] = ((dmg * br_a) * (sa * (1.0 - sa))).astype(dm_ref.dtype)
        dm_ref[:, d:2 * d] = ((dmg * br_b) * (sb * (1.0 - sb))).astype(dm_ref.dtype)
        acc_pl[...] += _dot_tn(ya16, da16)
        acc_pp[...] += _dot_tn(yb16, db16)
        dya_ref[...] = _dot_nt(da16, wpl_ref[...])
        dyb_ref[...] = _dot_nt(db16, wpp_ref[...])

        @pl.when(i == n - 1)
        def _():
            pltpu.sync_copy(acc_pl, gpl_ref)
            pltpu.sync_copy(acc_out, gout_ref)
            pltpu.sync_copy(acc_pg, gpg_ref)
            for k in range(N_DEV):
                pltpu.sync_copy(acc_pp.at[:, pl.ds(k * sec_pp, sec_pp)], gpp_ref.at[k])
                pltpu.sync_copy(acc_pe.at[:, pl.ds(k * sec_pe, sec_pe)], gpe_ref.at[k])

    tile = lambda w: pl.BlockSpec((tm, w), lambda i: (i, 0))
    any_spec = pl.BlockSpec(memory_space=pl.ANY)
    full = lambda shape: pl.BlockSpec(shape, lambda i: (0,) * len(shape))
    return pl.pallas_call(
        body, name="mid", grid=(n,),
        in_specs=[tile(lw), tile(pwid), tile(d), _const_spec(wmerge.shape), tile(d), tile(pdim), tile(d),
                  _const_spec(wpl.shape), _const_spec(wpp.shape), _const_spec(wout.shape), _const_spec(wpg.shape),
                  _const_spec(wpe.shape), _const_spec(g1.shape), _const_spec(gf.shape)],
        out_specs=(tile(lw), tile(pwid), tile(2 * d), tile(d), any_spec, any_spec, any_spec, any_spec, any_spec,
                   full((MID_PACK_ROWS, HEAD))),
        out_shape=(jax.ShapeDtypeStruct((t, lw), F32), jax.ShapeDtypeStruct((t, pwid), F32),
                   jax.ShapeDtypeStruct((t, 2 * d), _MM), jax.ShapeDtypeStruct((t, d), F32),
                   jax.ShapeDtypeStruct(wpl.shape, F32), jax.ShapeDtypeStruct((N_DEV, wpp.shape[0], sec_pp), F32),
                   jax.ShapeDtypeStruct(wout.shape, F32), jax.ShapeDtypeStruct(wpg.shape, F32),
                   jax.ShapeDtypeStruct((N_DEV, wpe.shape[0], sec_pe), F32),
                   jax.ShapeDtypeStruct((MID_PACK_ROWS, HEAD), F32)),
        scratch_shapes=[pltpu.VMEM(wpl.shape, F32), pltpu.VMEM(wpp.shape, F32), pltpu.VMEM(wout.shape, F32),
                        pltpu.VMEM(wpg.shape, F32), pltpu.VMEM(wpe.shape, F32)],
        compiler_params=pltpu.CompilerParams(dimension_semantics=("arbitrary",), vmem_limit_bytes=VMEM_LIMIT),
    )(ya, yb, h, wmerge, x, p, tgt, wpl, wpp, wout, wpg, wpe, g1, gf)


def _bwd_seq(z, hl, dya, dyb, dm, cw, cb, wa, ba, wx, bx, lam, pw, ps, sib_grads, tm):
    nb, s, zw = z.shape
    lw = cw.shape[1]
    pwid = ps.shape[1]
    nt = s // tm
    nc = zw + dm.shape[2]
    R = _SEQ_ROWS
    ns = len(sib_grads)

    def body(zq_ref, zh_ref, hl_ref, hlh_ref, dya_ref, dyb_ref, dm_ref, cw_ref, cb_ref, wa_ref, ba_ref, wx_ref, bx_ref,
             lam_ref, pw_ref, ps_ref, *rest):
        sib_src, rest = rest[:ns], rest[ns:]
        dz_ref, pack_ref = rest[:2]
        sib_out, rest = rest[2:2 + ns], rest[2 + ns:]
        (xa_ext, xb_ext, a_ext, an_buf, hl_ext, dh_buf, xc_buf, r_buf, i_buf, dxc_ext, q_ext, carry,
         send_sems, recv_sems) = rest
        b = pl.program_id(0)
        j = pl.program_id(1)
        jr = nt - 1 - j
        has_prev = jr > 0

        @pl.when((b == 0) & (j == 0))
        def _():
            for cp in _sibling_copies(sib_src, sib_out, send_sems, recv_sems):
                cp.start()

        dz_ref[:, zw:nc] = dm_ref[...]

        @pl.when((b == 0) & (j == 0))
        def _():
            pack_ref[...] = jnp.zeros_like(pack_ref)

        @pl.when(j == 0)
        def _():
            a_ext[pl.ds(tm, SUB), :] = jnp.zeros((SUB, lw), F32)
            dxc_ext[pl.ds(tm, HALO), :] = jnp.zeros((HALO, lw), F32)
            q_ext[pl.ds(tm, HALO), :] = jnp.zeros((HALO, pwid), F32)
            carry[...] = jnp.zeros_like(carry)

        xa_ext[pl.ds(0, HALO), :] = jnp.where(has_prev, zh_ref[:, 0:lw], 0.0)
        xa_ext[pl.ds(HALO, tm), :] = zq_ref[:, 0:lw]
        hl_ext[pl.ds(0, SUB), :] = jnp.where(has_prev, hlh_ref[...], 0.0)
        hl_ext[pl.ds(SUB, tm), :] = hl_ref[...]
        for h in range(N_HEADS):
            cols = slice(h * HEAD, (h + 1) * HEAD)
            xc = _conv_head(xa_ext, cw_ref, cb_ref, cols, tm)
            neg_c_sp = -LRU_C * _softplus_neg(lam_ref[:, cols])
            r, i, log_a = _lru_gates(xc, h, wa_ref, ba_ref, wx_ref, bx_ref, neg_c_sp)
            xc_buf[:, cols] = xc
            r_buf[:, cols] = r
            i_buf[:, cols] = i
            a_ext[pl.ds(0, tm), cols] = jnp.exp(log_a)
            ga = zq_ref[:, lw + h * HEAD:lw + (h + 1) * HEAD]
            sg = _sigmoid(ga)
            dyav = dya_ref[:, cols]
            dh_buf[:, cols] = dyav * (ga * sg)
            dga = (dyav * hl_ext[pl.ds(SUB, tm), cols]) * (sg * (1.0 + ga * (1.0 - sg)))
            dz_ref[:, lw + h * HEAD:lw + (h + 1) * HEAD] = dga.astype(dz_ref.dtype)
        an_buf[...] = _rows_ahead(a_ext[...], 1)[0:tm]

        rows = lax.broadcasted_iota(jnp.int32, (SUB, lw), 0)
        nch = tm // SUB

        def step(c, car):
            i0 = pl.multiple_of((nch - 1 - c) * SUB, SUB)
            cv = an_buf[pl.ds(i0, SUB), :]
            bv = dh_buf[pl.ds(i0, SUB), :]
            for sh in (1, 2, 4):
                m = rows < SUB - sh
                c_sh = jnp.where(m, pltpu.roll(cv, SUB - sh, 0), 1.0)
                b_sh = jnp.where(m, pltpu.roll(bv, SUB - sh, 0), 0.0)
                bv = cv * b_sh + bv
                cv = cv * c_sh
            hv = cv * car + bv
            dh_buf[pl.ds(i0, SUB), :] = hv
            return jnp.broadcast_to(hv[0:1, :], (SUB, lw))

        carry[...] = lax.fori_loop(0, nch, step, carry[...])

        for h in range(N_HEADS):
            cols = slice(h * HEAD, (h + 1) * HEAD)
            lam_h = lam_ref[:, cols]
            neg_c_sp = -LRU_C * _softplus_neg(lam_h)
            xc = xc_buf[:, cols]
            r = r_buf[:, cols]
            i = i_buf[:, cols]
            a = a_ext[pl.ds(0, tm), cols]
            mult, inv_mult = _lru_mult(neg_c_sp * r, a)
            dh = dh_buf[:, cols]
            dhx = dh * xc
            di = dhx * mult
            h_prev = _rows_back(hl_ext[:, cols], 1)[SUB:SUB + tm]
            dlog_a = (dh * h_prev) * a - (dhx * i) * ((a * a) * inv_mult)
            pack_ref[pl.ds(R["dlam"][0] + h, 1), :] += _colsum(dlog_a * r) * (LRU_C * _sigmoid(-lam_h))
            dpa = (dlog_a * neg_c_sp) * (r * (1.0 - r))
            dpx = di * (i * (1.0 - i))
            dpa16 = dpa.astype(_MM)
            dpx16 = dpx.astype(_MM)
            xc16 = xc.astype(_MM)
            dxc = dh * (mult * i) + _dot_nt(dpa16, wa_ref[h]) + _dot_nt(dpx16, wx_ref[h])
            pack_ref[pl.ds(R["dwa"][0] + h * HEAD, HEAD), :] += _dot_tn(xc16, dpa16)
            pack_ref[pl.ds(R["dwx"][0] + h * HEAD, HEAD), :] += _dot_tn(xc16, dpx16)
            pack_ref[pl.ds(R["dba"][0] + h, 1), :] += _colsum(dpa)
            pack_ref[pl.ds(R["dbx"][0] + h, 1), :] += _colsum(dpx)
            pack_ref[pl.ds(R["dcb"][0] + h, 1), :] += _colsum(dxc)
            dxc_ext[pl.ds(0, tm), cols] = dxc

        for h in range(N_HEADS):
            cols = slice(h * HEAD, (h + 1) * HEAD)
            dxc_all = dxc_ext[:, cols]
            xa = xa_ext[pl.ds(HALO, tm), cols]
            dxa = None
            for k in range(4):
                dxc_k = _rows_ahead(dxc_all, 3 - k)[0:tm]
                pack_ref[pl.ds(R["dcw"][0] + SUB * k + h, 1), :] += _colsum(dxc_k * xa)
                term = cw_ref[pl.ds(k, 1), cols] * dxc_k
                dxa = term if dxa is None else dxa + term
            dz_ref[:, cols] = dxa.astype(dz_ref.dtype)

        xb_ext[pl.ds(0, HALO), :] = jnp.where(has_prev, zh_ref[:, 2 * lw:2 * lw + pwid], 0.0)
        xb_ext[pl.ds(HALO, tm), :] = zq_ref[:, 2 * lw:2 * lw + pwid]
        pos = jr * tm + lax.broadcasted_iota(jnp.int32, (tm, HEAD), 0)
        for g, k in enumerate(POOL_WINDOWS):
            cols = slice(g * HEAD, (g + 1) * HEAD)
            diff, inv_cnt = _pool_diff(xb_ext, g, k, pos, tm)
            diff16 = diff.astype(_MM)
            yp = _dot(diff16, pw_ref[g])
            sc = ps_ref[:, cols]
            gb = zq_ref[:, 2 * lw + pwid + g * HEAD:2 * lw + pwid + (g + 1) * HEAD]
            sgb = _sigmoid(gb)
            dybv = dyb_ref[:, cols]
            dy_pool = dybv * (gb * sgb)
            dgb = (dybv * (yp * sc)) * (sgb * (1.0 + gb * (1.0 - sgb)))
            pack_ref[pl.ds(R["dps"][0] + g, 1), :] += _colsum(dy_pool * yp)
            dyp16 = (dy_pool * sc).astype(_MM)
            pack_ref[pl.ds(R["dpw"][0] + g * HEAD, HEAD), :] += _dot_tn(diff16, dyp16)
            ddiff = _dot_nt(dyp16, pw_ref[g])
            q_ext[pl.ds(0, tm), cols] = ddiff * inv_cnt
            dxb = _window_sum(q_ext[:, cols], k, _rows_ahead)[0:tm] - ddiff
            dz_ref[:, 2 * lw + g * HEAD:2 * lw + (g + 1) * HEAD] = dxb.astype(dz_ref.dtype)
            dz_ref[:, 2 * lw + pwid + g * HEAD:2 * lw + pwid + (g + 1) * HEAD] = dgb.astype(dz_ref.dtype)

        a_ext[pl.ds(tm, SUB), :] = a_ext[pl.ds(0, SUB), :]
        dxc_ext[pl.ds(tm, HALO), :] = dxc_ext[pl.ds(0, HALO), :]
        q_ext[pl.ds(tm, HALO), :] = q_ext[pl.ds(0, HALO), :]

        @pl.when((b == nb - 1) & (j == nt - 1))
        def _():
            for cp in _sibling_copies(sib_src, sib_out, send_sems, recv_sems):
                cp.wait()

    rev = lambda w: pl.BlockSpec((None, tm, w), lambda b, j: (b, nt - 1 - j, 0))
    prev_rows = lambda rows, w: pl.BlockSpec(
        (None, rows, w), lambda b, j: (b, jnp.maximum((nt - 1 - j) * (tm // rows) - 1, 0), 0))
    return pl.pallas_call(
        body, name="bwd_seq", grid=(nb, nt),
        in_specs=[rev(zw), prev_rows(HALO, zw), rev(lw), prev_rows(SUB, lw), rev(lw), rev(pwid), rev(nc - zw),
                  _const_spec(cw.shape), _const_spec(cb.shape), _const_spec(wa.shape), _const_spec(ba.shape),
                  _const_spec(wx.shape), _const_spec(bx.shape), _const_spec(lam.shape), _const_spec(pw.shape),
                  _const_spec(ps.shape)] + [HBM_SPEC] * ns,
        out_specs=(rev(nc), pl.BlockSpec((SEQ_PACK_ROWS, HEAD), lambda b, j: (0, 0))) + (HBM_SPEC,) * ns,
        out_shape=(jax.ShapeDtypeStruct((nb, s, nc), _MM), jax.ShapeDtypeStruct((SEQ_PACK_ROWS, HEAD), F32))
        + tuple(jax.ShapeDtypeStruct((4,) + g.shape[1:], g.dtype) for g in sib_grads),
        scratch_shapes=[pltpu.VMEM((tm + HALO, lw), F32), pltpu.VMEM((tm + HALO, pwid), F32),
                        pltpu.VMEM((tm + SUB, lw), F32), pltpu.VMEM((tm, lw), F32), pltpu.VMEM((tm + SUB, lw), F32),
                        pltpu.VMEM((tm, lw), F32), pltpu.VMEM((tm, lw), F32), pltpu.VMEM((tm, lw), F32),
                        pltpu.VMEM((tm, lw), F32), pltpu.VMEM((tm + HALO, lw), F32),
                        pltpu.VMEM((tm + HALO, pwid), F32), pltpu.VMEM((SUB, lw), F32),
                        pltpu.SemaphoreType.DMA((ns, 4)), pltpu.SemaphoreType.DMA((ns, 4))],
        compiler_params=pltpu.CompilerParams(dimension_semantics=("arbitrary", "arbitrary"), vmem_limit_bytes=VMEM_LIMIT),
    )(z, z, hl, hl, dya, dyb, dm, cw, cb, wa, ba, wx, bx, lam, pw, ps, *sib_grads)


def _bwd_win(ht, dz, chip_srcs, small):
    nb, d, s = ht.shape
    nc = dz.shape[2]
    sec = nc // N_DEV
    n_chips = N_DEV // 2
    nm = len(chip_srcs)

    def body(ht_ref, dz_ref, *rest):
        chip_src, small_src = rest[:nm], rest[nm]
        gwin_ref, recv_ref = rest[nm + 1:nm + 3]
        chip_out, small_out = rest[nm + 3:2 * nm + 3], rest[2 * nm + 3]
        acc, local_sems, sib_send, sib_recv, chip_send, chip_recv, small_send, small_recv = rest[2 * nm + 4:]
        q = pl.program_id(0)
        b = pl.program_id(1)
        slot = q % 2
        x, y, c = _my_pos()

        def half(sl, core):
            return acc.at[sl, :, pl.ds(pl.multiple_of(core * sec, HEAD), sec)]

        def local_copy(qq, sl):
            return pltpu.make_async_copy(half(sl, c), gwin_ref.at[2 * qq + c], local_sems.at[sl])

        def sib_copy(qq, sl):
            return pltpu.make_async_remote_copy(
                src_ref=half(sl, 1 - c), dst_ref=recv_ref.at[qq], send_sem=sib_send.at[qq],
                recv_sem=sib_recv.at[qq], device_id=(x, y, 1 - c), device_id_type=MESH)

        def others():
            return (_chip_copies(chip_src, chip_out, chip_send, chip_recv)
                    + _sibling_copies([small_src], [small_out], small_send, small_recv))

        @pl.when((q == 0) & (b == 0))
        def _():
            for cp in others():
                cp.start()

        @pl.when((q >= 2) & (b == 0))
        def _():
            local_copy(q - 2, slot).wait()
            sib_copy(q - 2, slot).wait_send()

        part = _dot(ht_ref[b], dz_ref[...])

        @pl.when(b == 0)
        def _():
            acc[slot] = part

        @pl.when(b != 0)
        def _():
            acc[slot] += part

        @pl.when(b == nb - 1)
        def _():
            local_copy(q, slot).start()
            sib_copy(q, slot).start()

        @pl.when((q == n_chips - 1) & (b == nb - 1))
        def _():
            for qq in (n_chips - 2, n_chips - 1):
                local_copy(qq, qq % 2).wait()
                sib_copy(qq, qq % 2).wait_send()
            for qq in range(n_chips):
                sib_copy(qq, 0).wait_recv()
            for cp in others():
                cp.wait()

    return pl.pallas_call(
        body, name="bwd_win", grid=(n_chips, nb),
        in_specs=[_const_spec(ht.shape), pl.BlockSpec((None, s, 2 * sec), lambda q, b: (b, 0, q))]
        + [HBM_SPEC] * (nm + 1),
        out_specs=(HBM_SPEC,) * (nm + 3),
        out_shape=(jax.ShapeDtypeStruct((N_DEV, d, sec), F32), jax.ShapeDtypeStruct((n_chips, d, sec), F32))
        + tuple(jax.ShapeDtypeStruct((3,) + g.shape[1:], g.dtype) for g in chip_srcs)
        + (jax.ShapeDtypeStruct((4,) + small.shape[1:], small.dtype),),
        scratch_shapes=[pltpu.VMEM((2, d, 2 * sec), F32), pltpu.SemaphoreType.DMA((2,)),
                        pltpu.SemaphoreType.DMA((N_DEV // 2,)), pltpu.SemaphoreType.DMA((N_DEV // 2,)),
                        pltpu.SemaphoreType.DMA((nm, 3)), pltpu.SemaphoreType.DMA((nm, 3)),
                        pltpu.SemaphoreType.DMA((1, 4)), pltpu.SemaphoreType.DMA((1, 4))],
        compiler_params=pltpu.CompilerParams(dimension_semantics=("arbitrary", "arbitrary"), vmem_limit_bytes=VMEM_LIMIT),
    )(ht, dz, *chip_srcs, small)


def _bwd_dx(dz, x, dx1, win, g0, chip_srcs, tm):
    t, d = x.shape
    nc = win.shape[1]
    n = t // tm
    nm = len(chip_srcs)

    def body(dz_ref, x_ref, dx1_ref, win_ref, g0_ref, *rest):
        chip_src = rest[:nm]
        gx_ref, pack_ref = rest[nm:nm + 2]
        chip_out = rest[nm + 2:2 * nm + 2]
        chip_send, chip_recv = rest[2 * nm + 2:]
        i = pl.program_id(0)

        @pl.when(i == 0)
        def _():
            pack_ref[...] = jnp.zeros_like(pack_ref)
            for cp in _chip_copies(chip_src, chip_out, chip_send, chip_recv):
                cp.start()

        xv = x_ref[...]
        r0 = _rsqrt_mean_sq(xv)
        xh = xv * r0
        dh = _dot_nt(dz_ref[...], win_ref[...])
        _store_chunks(pack_ref, 0, _colsum(dh * xh))
        dxh = dh * g0_ref[...]
        gx_ref[...] = dx1_ref[...] + r0 * (dxh - xh * jnp.mean(dxh * xh, axis=-1, keepdims=True))

        @pl.when(i == n - 1)
        def _():
            for cp in _chip_copies(chip_src, chip_out, chip_send, chip_recv):
                cp.wait()

    tile = lambda w: pl.BlockSpec((tm, w), lambda i: (i, 0))
    return pl.pallas_call(
        body, name="bwd_dx", grid=(n,),
        in_specs=[tile(nc), tile(d), tile(d), _const_spec(win.shape), _const_spec(g0.shape)] + [HBM_SPEC] * nm,
        out_specs=(tile(d), pl.BlockSpec((INP_PACK_ROWS, HEAD), lambda i: (0, 0))) + (HBM_SPEC,) * nm,
        out_shape=(jax.ShapeDtypeStruct((t, d), F32), jax.ShapeDtypeStruct((INP_PACK_ROWS, HEAD), F32))
        + tuple(jax.ShapeDtypeStruct((3,) + g.shape[1:], g.dtype) for g in chip_srcs),
        scratch_shapes=[pltpu.SemaphoreType.DMA((nm, 3)), pltpu.SemaphoreType.DMA((nm, 3))],
        compiler_params=pltpu.CompilerParams(dimension_semantics=("arbitrary",), vmem_limit_bytes=VMEM_LIMIT),
    )(dz, x, dx1, win, g0, *chip_srcs)


def _all_gather(shards, full_shapes, slicers, name):
    nw = len(shards)

    def body(*refs):
        start, forward, finish = _gather_phases(refs[:nw], refs[nw:2 * nw], slicers, *refs[2 * nw:])
        start()
        forward()
        finish()

    return pl.pallas_call(
        body, name=name, in_specs=[HBM_SPEC] * nw, out_specs=(HBM_SPEC,) * nw,
        out_shape=tuple(jax.ShapeDtypeStruct(shp, a.dtype) for shp, a in zip(full_shapes, shards)),
        scratch_shapes=_gather_sems(nw),
    )(*shards)


ROW_BLOCKS = 8


def _chip_sums(core_idx, grads, recvs, dtypes, name):
    nw = len(grads)

    def body(idx_ref, *refs):
        gs, rs, outs = refs[:nw], refs[nw:2 * nw], refs[2 * nw:]
        for g, r, o in zip(gs, rs, outs):
            o[...] = (g[...] + r[...]).astype(o.dtype)

    def blk(g):
        return (None,) + g.shape[1:]

    return pl.pallas_call(
        body, name=name,
        grid_spec=pltpu.PrefetchScalarGridSpec(
            num_scalar_prefetch=1, grid=(4,),
            in_specs=[pl.BlockSpec(blk(g), lambda q, s: (2 * q + s[0], 0, 0)) for g in grads]
            + [pl.BlockSpec(blk(g), lambda q, s: (q, 0, 0)) for g in grads],
            out_specs=tuple(pl.BlockSpec(blk(g), lambda q, s: (q, 0, 0)) for g in grads)),
        out_shape=tuple(jax.ShapeDtypeStruct((4,) + g.shape[1:], dt) for g, dt in zip(grads, dtypes)),
    )(core_idx, *grads, *recvs)


def _adamw(w, g, m, v):
    m = ADAM_B1 * m + (1.0 - ADAM_B1) * g
    v = ADAM_B2 * v + (1.0 - ADAM_B2) * (g * g)
    m_hat = m / (1.0 - ADAM_B1 ** ADAM_STEP)
    v_hat = v / (1.0 - ADAM_B2 ** ADAM_STEP)
    delta = -ADAM_LR * (m_hat / (jnp.sqrt(v_hat) + ADAM_EPS) + ADAM_WD * w)
    return delta, m, v


def _adam_sections(sec_idx, grads, recv_sib, recv_chips, wmv):
    nw = len(wmv)

    def body(idx_ref, *refs):
        gs, rs, cs = refs[:nw + 1], refs[nw + 1:2 * nw + 2], refs[2 * nw + 2:3 * nw + 3]
        params = refs[3 * nw + 3:6 * nw + 3]
        outs = refs[6 * nw + 3:]
        for w in range(nw + 1):
            g = gs[w][...] + rs[w][...]
            for k in range(3):
                g = g + cs[w][k].astype(F32)
            if w == nw:
                outs[4 * nw][...] = g
                continue
            wv, mv, vv = (params[3 * w + t][...] for t in range(3))
            delta, m_new, v_new = _adamw(wv, g, mv, vv)
            outs[4 * w][...] = g
            outs[4 * w + 1][...] = delta
            outs[4 * w + 2][...] = m_new
            outs[4 * w + 3][...] = v_new

    def rb(g):
        return g.shape[1] // ROW_BLOCKS

    in_specs = [pl.BlockSpec((None, rb(g), g.shape[2]), lambda i, s: (s[0], i, 0)) for g in grads]
    in_specs += [pl.BlockSpec((None, rb(g), g.shape[2]), lambda i, s: (s[1], i, 0)) for g in grads]
    in_specs += [pl.BlockSpec((3, rb(g), g.shape[2]), lambda i, s: (0, i, 0)) for g in grads]
    sec = lambda g: pl.BlockSpec((rb(g), g.shape[2]), lambda i, s: (i, 0))
    for w in range(nw):
        in_specs += [sec(grads[w])] * 3
    out_specs, out_shape = [], []
    for g in grads[:nw]:
        out_specs += [sec(g)] * 4
        out_shape += [jax.ShapeDtypeStruct(g.shape[1:], F32)] * 4
    out_specs.append(sec(grads[nw]))
    out_shape.append(jax.ShapeDtypeStruct(grads[nw].shape[1:], F32))
    flat = [a for t in wmv for a in t]
    outs = pl.pallas_call(
        body, name="adam_sections",
        grid_spec=pltpu.PrefetchScalarGridSpec(num_scalar_prefetch=1, grid=(ROW_BLOCKS,), in_specs=in_specs,
                                               out_specs=tuple(out_specs)),
        out_shape=tuple(out_shape),
    )(sec_idx, *grads, *recv_sib, *recv_chips, *flat)
    return [outs[4 * w:4 * w + 4] for w in range(nw)], outs[4 * nw]


def _adam_small(sec_idx, gpack, parts, parts_wmv, vec_params, mat_params, conv_wmv):
    items = [(0, parts_wmv, "parts")]
    items += [(r0, t, "vec") for r0, t in vec_params] + [(r0, t, "mat") for r0, t in mat_params]
    items.append((_SEQ_ROWS["dcw"][0], conv_wmv, "conv"))

    def body(idx_ref, g_ref, parts_ref, *refs):
        ins, outs = refs[:3 * len(items)], refs[3 * len(items):]
        for n, (r0, _, kind) in enumerate(items):
            w_ref, m_ref, v_ref = ins[3 * n:3 * n + 3]
            o = outs[4 * n:4 * n + 4]
            if kind == "mat":
                g = g_ref[pl.ds(r0, w_ref.shape[0]), :]
                res = (g,) + _adamw(w_ref[...], g, m_ref[...], v_ref[...])
                for ref, val in zip(o, res):
                    ref[...] = val
            elif kind in ("vec", "parts"):
                for h in range(w_ref.shape[1] // HEAD):
                    cols = slice(h * HEAD, (h + 1) * HEAD)
                    if kind == "vec":
                        g = g_ref[pl.ds(r0 + h, 1), :]
                    else:
                        g = parts_ref[pl.ds(h, 1), :]
                        for dev in range(1, N_DEV):
                            g = g + parts_ref[pl.ds(SUB * dev + h, 1), :]
                    res = (g,) + _adamw(w_ref[:, cols], g, m_ref[:, cols], v_ref[:, cols])
                    for ref, val in zip(o, res):
                        ref[:, cols] = val
            else:
                rows = lax.broadcasted_iota(jnp.int32, (SUB, HEAD), 0)
                for k in range(4):
                    blk = g_ref[pl.ds(r0 + SUB * k, SUB), :]
                    g = jnp.sum(jnp.where(rows == idx_ref[0], blk, 0.0), axis=0, keepdims=True)
                    row = pl.ds(k, 1)
                    res = (g,) + _adamw(w_ref[row, :], g, m_ref[row, :], v_ref[row, :])
                    for ref, val in zip(o, res):
                        ref[row, :] = val
        outs[4 * len(items)][...] = g_ref[pl.ds(_OFF_MID + MID_LOSS_ROW, 1), :]

    flat = [a for _, t, _ in items for a in t]
    full = lambda a: pl.BlockSpec(a.shape, lambda i, s: (0,) * a.ndim)
    out_specs, out_shape = [], []
    for _, t, _ in items:
        out_specs += [full(t[0])] * 4
        out_shape += [jax.ShapeDtypeStruct(t[0].shape, F32)] * 4
    loss_row = jax.ShapeDtypeStruct((1, HEAD), F32)
    out_specs.append(full(loss_row))
    out_shape.append(loss_row)
    outs = pl.pallas_call(
        body, name="adam_small",
        grid_spec=pltpu.PrefetchScalarGridSpec(num_scalar_prefetch=1, grid=(1,),
                                               in_specs=[full(gpack), full(parts)] + [full(a) for a in flat],
                                               out_specs=tuple(out_specs)),
        out_shape=tuple(out_shape),
    )(sec_idx, gpack, parts, *flat)
    return [outs[4 * n:4 * n + 4] for n in range(len(items))], outs[4 * len(items)]


def _col_slicer(width):
    return lambda ref, idx: ref.at[:, pl.ds(pl.multiple_of(idx * width, HEAD), width)]


def _row_slicer(rows):
    return lambda ref, idx: ref.at[pl.ds(pl.multiple_of(idx * rows, SUB), rows), :]


def kernel(x, p, norm_g, w_in, conv_w, conv_b, lru_w_a, lru_b_a, lru_w_x, lru_b_x, lru_lambda, pool_w, pool_scale, w_proj_lru, w_proj_pool, w_out, ple_norm_g, w_ple_gate, w_ple_proj, final_g, loss_target, m_norm_g, m_w_in, m_conv_w, m_conv_b, m_lru_w_a, m_lru_b_a, m_lru_w_x, m_lru_b_x, m_lru_lambda, m_pool_w, m_pool_scale, m_w_proj_lru, m_w_proj_pool, m_w_out, m_ple_norm_g, m_w_ple_gate, m_w_ple_proj, m_final_g, v_norm_g, v_w_in, v_conv_w, v_conv_b, v_lru_w_a, v_lru_b_a, v_lru_w_x, v_lru_b_x, v_lru_lambda, v_pool_w, v_pool_scale, v_w_proj_lru, v_w_proj_pool, v_w_out, v_ple_norm_g, v_w_ple_gate, v_w_ple_proj, v_final_g):
    nb, s, d = x.shape
    t = nb * s
    tm = min(TILE_M, s)
    sec_idx = (4 * lax.axis_index("x") + 2 * lax.axis_index("y") + lax.axis_index("c")).astype(jnp.int32)
    chip_idx = (2 * lax.axis_index("x") + lax.axis_index("y")).astype(jnp.int32)
    core_idx = lax.axis_index("c").astype(jnp.int32)

    h, ht, win, cw = _pre(
        x, norm_g, [w_in[0].astype(_MM), conv_w[0]],
        [(d, N_DEV * w_in.shape[2]), (conv_w.shape[1], N_DEV * conv_w.shape[2])],
        [_col_slicer(w_in.shape[2]), _col_slicer(conv_w.shape[2])], tm)
    late_shards = [w_proj_lru[0].astype(_MM), w_proj_pool[0].astype(_MM), w_out[0].astype(_MM),
                   w_ple_gate[0].astype(_MM), w_ple_proj[0].astype(_MM)]
    late_shapes = [(N_DEV * w_proj_lru.shape[1], d), (w_proj_pool.shape[1], N_DEV * w_proj_pool.shape[2]),
                   (N_DEV * w_out.shape[1], d), (N_DEV * w_ple_gate.shape[1], d),
                   (w_ple_proj.shape[1], N_DEV * w_ple_proj.shape[2])]
    late_slicers = [_row_slicer(w_proj_lru.shape[1]), _col_slicer(w_proj_pool.shape[2]), _row_slicer(w_out.shape[1]),
                    _row_slicer(w_ple_gate.shape[1]), _col_slicer(w_ple_proj.shape[2])]

    lw = cw.shape[1]
    pwid = pool_scale.shape[1]
    wa = lru_w_a[0].astype(_MM)
    wx = lru_w_x[0].astype(_MM)
    pw = pool_w[0].astype(_MM)
    ba, bx = lru_b_a[0], lru_b_x[0]
    gf = final_g.reshape(1, d)
    rows_of = lambda g: g.reshape(N_DEV, g.shape[0] // N_DEV, g.shape[1])
    core = core_idx.reshape(1)

    z, hl, ya, yb, wpl, wpp, wout, wpg, wpe = _fwd_seq(
        h, win, cw, conv_b, wa, ba, wx, bx, lru_lambda, pw, pool_scale, late_shards, late_shapes, late_slicers, tm)
    x2d = x.reshape(t, d)
    (dya, dyb, dm, dx1, g_pl, g_pp, g_out, g_pg, g_pe, pack_mid) = _mid(
        ya.reshape(t, lw), yb.reshape(t, pwid), h.reshape(t, d), win[:, 2 * lw + 2 * pwid:], x2d, p[0].reshape(t, -1),
        loss_target.reshape(t, d), wpl, wpp, wout, wpg, wpe, ple_norm_g, gf, tm)
    grads_mid = [rows_of(g_pl), g_pp, rows_of(g_out), rows_of(g_pg), g_pe]
    seq_out = _bwd_seq(z, hl, dya.reshape(nb, s, lw), dyb.reshape(nb, s, pwid), dm.reshape(nb, s, -1), cw, conv_b,
                       wa, ba, wx, bx, lru_lambda, pw, pool_scale, grads_mid, tm)
    dz, pack_seq, sib_mid = seq_out[0], seq_out[1], list(seq_out[2:])
    sums_mid = _chip_sums(core, grads_mid, sib_mid, [_WIRE] * len(grads_mid), "chip_sums_mid")
    small = jnp.concatenate(
        [pack_seq, pack_mid, jnp.zeros((SMALL_ROWS - _OFF_MID - MID_PACK_ROWS, HEAD), F32)], axis=0
    ).reshape(N_DEV, SMALL_SEC, HEAD)
    win_out = _bwd_win(ht, dz, sums_mid, small)
    g_in, sib_in, chips_mid, sib_small = win_out[0], win_out[1], list(win_out[2:-1]), win_out[-1]
    sums_in = _chip_sums(core, [g_in, small], [sib_in, sib_small], [_WIRE, F32], "chip_sums_in")
    dx_out = _bwd_dx(dz.reshape(t, -1), x2d, dx1, win, norm_g, sums_in, tm)
    grad_x, pack_inp, chips_in, chips_small = dx_out

    grads = [g_in] + grads_mid + [small]
    recv_sib = [sib_in] + sib_mid + [sib_small]
    recv_chips = [chips_in] + chips_mid + [chips_small]
    wmv = [(w_in[0], m_w_in[0], v_w_in[0]), (w_proj_lru[0], m_w_proj_lru[0], v_w_proj_lru[0]),
           (w_proj_pool[0], m_w_proj_pool[0], v_w_proj_pool[0]), (w_out[0], m_w_out[0], v_w_out[0]),
           (w_ple_gate[0], m_w_ple_gate[0], v_w_ple_gate[0]), (w_ple_proj[0], m_w_ple_proj[0], v_w_ple_proj[0])]
    big, small_sec = _adam_sections(jnp.stack([sec_idx, chip_idx]), grads, recv_sib, recv_chips, wmv)
    gsmall, g0_parts = _all_gather([small_sec, pack_inp], [(SMALL_ROWS, HEAD), (N_DEV * INP_PACK_ROWS, HEAD)],
                                   [_row_slicer(SMALL_SEC), _row_slicer(INP_PACK_ROWS)], "gather_small")

    R = _SEQ_ROWS
    vec = lambda r0, *t: (r0, tuple(a.reshape(1, -1) for a in t))
    mat = lambda r0, *t: (r0, tuple(a.reshape(-1, HEAD) for a in t))
    norm_wmv = vec(0, norm_g, m_norm_g, v_norm_g)[1]
    vec_params = [vec(R["dcb"][0], conv_b, m_conv_b, v_conv_b),
                  vec(R["dlam"][0], lru_lambda, m_lru_lambda, v_lru_lambda),
                  vec(R["dps"][0], pool_scale, m_pool_scale, v_pool_scale),
                  vec(_OFF_MID, ple_norm_g, m_ple_norm_g, v_ple_norm_g),
                  vec(_OFF_MID + SUB, final_g, m_final_g, v_final_g)]
    mat_params = [mat(R["dwa"][0], lru_w_a, m_lru_w_a, v_lru_w_a), mat(R["dba"][0], lru_b_a, m_lru_b_a, v_lru_b_a),
                  mat(R["dwx"][0], lru_w_x, m_lru_w_x, v_lru_w_x), mat(R["dbx"][0], lru_b_x, m_lru_b_x, v_lru_b_x),
                  mat(R["dpw"][0], pool_w, m_pool_w, v_pool_w)]
    conv_wmv = (conv_w[0], m_conv_w[0], v_conv_w[0])
    small_out, loss_row = _adam_small(sec_idx.reshape(1), gsmall, g0_parts, norm_wmv, vec_params, mat_params, conv_wmv)
    loss = loss_row[0, 0]

    res = {}
    names_small = ["norm_g", "conv_b", "lru_lambda", "pool_scale", "ple_norm_g", "final_g",
                   "lru_w_a", "lru_b_a", "lru_w_x", "lru_b_x", "pool_w", "conv_w"]
    shapes = {"norm_g": norm_g, "conv_b": conv_b, "lru_lambda": lru_lambda, "pool_scale": pool_scale,
              "ple_norm_g": ple_norm_g, "final_g": final_g, "lru_w_a": lru_w_a, "lru_b_a": lru_b_a, "lru_w_x": lru_w_x,
              "lru_b_x": lru_b_x, "pool_w": pool_w, "conv_w": conv_w}
    for name, quad in zip(names_small, small_out):
        res[name] = [a.reshape(shapes[name].shape) for a in quad]
    names_big = ["w_in", "w_proj_lru", "w_proj_pool", "w_out", "w_ple_gate", "w_ple_proj"]
    for name, quad, (w, _, _) in zip(names_big, big, wmv):
        res[name] = [a.reshape((1,) + w.shape) for a in quad]
    order = ["norm_g", "w_in", "conv_w", "conv_b", "lru_w_a", "lru_b_a", "lru_w_x", "lru_b_x", "lru_lambda", "pool_w",
             "pool_scale", "w_proj_lru", "w_proj_pool", "w_out", "ple_norm_g", "w_ple_gate", "w_ple_proj", "final_g"]
    out = [loss, grad_x.reshape(nb, s, d)]
    for kind in range(4):
        out += [res[name][kind] for name in order]
    return tuple(out)
```

```python
import jax
import jax.numpy as jnp
from jax import lax
from jax.experimental import pallas as pl
from jax.experimental.pallas import tpu as pltpu

F32 = jnp.float32
_MM = jnp.bfloat16
_WIRE = jnp.bfloat16

EPS = 1e-6
LRU_C = 8.0
MULT_SQ_FLOOR = 1e-30
POOL_WINDOWS = (2, 4, 8, 16)
N_HEADS = 8
HEAD = 128
HALO = 16
SUB = 8

ADAM_LR = 0.001
ADAM_B1 = 0.9
ADAM_B2 = 0.999
ADAM_EPS = 1e-08
ADAM_WD = 0.01
ADAM_STEP = 10

N_DEV = 8
MESH = pl.DeviceIdType.MESH
VMEM_LIMIT = 60 * 1024 * 1024
TILE_M = 256
N_SEQ_SAVED = 10

_SEQ_ROWS = {"dwa": (0, 1024), "dwx": (1024, 1024), "dpw": (2048, 512), "dba": (2560, 8), "dbx": (2568, 8),
             "dcb": (2576, 8), "dlam": (2584, 8), "dcw": (2592, 32), "dps": (2624, 8)}
SEQ_PACK_ROWS = 2632
MID_PACK_ROWS = 24
MID_LOSS_ROW = 16
INP_PACK_ROWS = 8
SMALL_ROWS = 3072
SMALL_SEC = SMALL_ROWS // N_DEV
_OFF_MID = SEQ_PACK_ROWS


def _dot(a, b):
    return jnp.dot(a, b, preferred_element_type=F32)


def _dot_nt(a, b):
    return lax.dot_general(a, b, (((1,), (1,)), ((), ())), preferred_element_type=F32)


def _dot_tn(a, b):
    return lax.dot_general(a, b, (((0,), (0,)), ((), ())), preferred_element_type=F32)


def _sigmoid(v):
    return 0.5 * jnp.tanh(0.5 * v) + 0.5


def _lru_mult(log_a, a):
    m2 = jnp.maximum(-jnp.tanh(log_a) * (1.0 + a * a), MULT_SQ_FLOOR)
    inv = lax.rsqrt(m2)
    return m2 * inv, inv


def _rows_back(v, n):
    return pltpu.roll(v, n, 0) if n else v


def _rows_ahead(v, n):
    return pltpu.roll(v, v.shape[0] - n, 0) if n else v


def _softplus_neg(lam):
    e = jnp.exp(-jnp.abs(lam))
    w = 1.0 + e
    l1p = jnp.where(w == 1.0, e, jnp.log(w) * (e / (w - 1.0)))
    return jnp.maximum(-lam, 0.0) + l1p


def _rsqrt_mean_sq(v):
    return lax.rsqrt(jnp.mean(v * v, axis=-1, keepdims=True) + EPS)


def _colsum(v):
    return jnp.sum(v, axis=0, keepdims=True)


def _const_spec(shape):
    nd = len(shape)
    return pl.BlockSpec(shape, lambda *_: (0,) * nd, pipeline_mode=pl.Buffered(1))


HBM_SPEC = pl.BlockSpec(memory_space=pl.ANY)


def _my_pos():
    return lax.axis_index("x"), lax.axis_index("y"), lax.axis_index("c")


def _other_chips(x, y):
    return [(1 - x, y), (x, 1 - y), (1 - x, 1 - y)]


def _gather_sems(nw):
    return [pltpu.SemaphoreType.DMA((nw, 7)), pltpu.SemaphoreType.DMA((nw, 7)), pltpu.SemaphoreType.DMA((nw,))]


def _gather_phases(srcs, outs, slicers, send_sems, recv_sems, local_sems):
    nw = len(srcs)
    x, y, c = _my_pos()
    me, sibling = (x, y, c), (x, y, 1 - c)
    chips = _other_chips(x, y)

    def part(w, pos):
        return slicers[w](outs[w], 4 * pos[0] + 2 * pos[1] + pos[2])

    def copy(w, k, block, to, src=None):
        return pltpu.make_async_remote_copy(
            src_ref=part(w, block) if src is None else src, dst_ref=part(w, block),
            send_sem=send_sems.at[w, k], recv_sem=recv_sems.at[w, k], device_id=to, device_id_type=MESH)

    def mine():
        return [pltpu.make_async_copy(srcs[w], part(w, me), local_sems.at[w]) for w in range(nw)]

    def first():
        out = []
        for w in range(nw):
            out.append(copy(w, 0, me, sibling, src=srcs[w]))
            out += [copy(w, 1 + j, me, (*chip, c), src=srcs[w]) for j, chip in enumerate(chips)]
        return out

    def passed():
        return [copy(w, 4 + j, (*chip, c), sibling) for j, chip in enumerate(chips) for w in range(nw)]

    def start():
        for cp in mine() + first():
            cp.start()

    def forward():
        for j, chip in enumerate(chips):
            for w in range(nw):
                copy(w, 1 + j, (*chip, c), me).wait_recv()
                copy(w, 4 + j, (*chip, c), sibling).start()

    def finish():
        for w in range(nw):
            copy(w, 0, sibling, me).wait_recv()
            for j, chip in enumerate(chips):
                copy(w, 4 + j, (*chip, 1 - c), me).wait_recv()
        for cp in first() + passed():
            cp.wait_send()
        for cp in mine():
            cp.wait()

    return start, forward, finish


def _sibling_copies(srcs, outs, send_sems, recv_sems):
    x, y, c = _my_pos()
    return [pltpu.make_async_remote_copy(
        src_ref=srcs[w].at[2 * q + (1 - c)], dst_ref=outs[w].at[q], send_sem=send_sems.at[w, q],
        recv_sem=recv_sems.at[w, q], device_id=(x, y, 1 - c), device_id_type=MESH)
        for w in range(len(srcs)) for q in range(4)]


def _chip_copies(srcs, outs, send_sems, recv_sems):
    x, y, c = _my_pos()
    return [pltpu.make_async_remote_copy(
        src_ref=srcs[w].at[2 * px + py], dst_ref=outs[w].at[k], send_sem=send_sems.at[w, k],
        recv_sem=recv_sems.at[w, k], device_id=(px, py, c), device_id_type=MESH)
        for w in range(len(srcs)) for k, (px, py) in enumerate(_other_chips(x, y))]


def _lru_gates(xc, h, wa_ref, ba_ref, wx_ref, bx_ref, neg_c_sp):
    xc16 = xc.astype(_MM)
    r = _sigmoid(_dot(xc16, wa_ref[h]) + ba_ref[pl.ds(h, 1), :])
    i = _sigmoid(_dot(xc16, wx_ref[h]) + bx_ref[pl.ds(h, 1), :])
    return r, i, neg_c_sp * r


def _conv_head(xa_ext, cw_ref, cb_ref, cols, tm):
    ext = xa_ext[:, cols]
    xc = cb_ref[:, cols] + cw_ref[pl.ds(3, 1), cols] * ext[HALO:HALO + tm]
    for k in range(3):
        xc = xc + cw_ref[pl.ds(k, 1), cols] * _rows_back(ext, 3 - k)[HALO:HALO + tm]
    return xc


def _window_sum(ext, k, shift):
    n = 1
    while n < k:
        ext = ext + shift(ext, n)
        n *= 2
    return ext


def _pool_diff(xb_ext, g, k, pos, tm):
    cols = slice(g * HEAD, (g + 1) * HEAD)
    ext = xb_ext[:, cols]
    ws = _window_sum(ext, k, _rows_back)[HALO:HALO + tm]
    inv_cnt = 1.0 / jnp.minimum(pos + 1, k).astype(F32)
    return ws * inv_cnt - ext[HALO:HALO + tm], inv_cnt


def _pre(x, g0, shards, shapes, slicers, tm):
    nb, s, d = x.shape
    nt = s // tm
    nl = len(shards)

    def body(x_ref, g0_ref, *rest):
        src, rest = rest[:nl], rest[nl:]
        h_ref, ht_ref = rest[:2]
        out, (send_sems, recv_sems, local_sems) = rest[2:2 + nl], rest[2 + nl:]
        step_no = pl.program_id(0) * nt + pl.program_id(1)
        g_start, g_forward, g_finish = _gather_phases(src, out, slicers, send_sems, recv_sems, local_sems)
        pl.when(step_no == 0)(g_start)
        xv = x_ref[...]
        h16 = ((xv * _rsqrt_mean_sq(xv)) * g0_ref[...]).astype(_MM)
        h_ref[...] = h16
        ht_ref[...] = h16.T

        @pl.when(step_no == nb * nt - 1)
        def _():
            g_forward()
            g_finish()

    tile = pl.BlockSpec((None, tm, d), lambda b, j: (b, j, 0))
    return pl.pallas_call(
        body, name="pre", grid=(nb, nt),
        in_specs=[tile, _const_spec(g0.shape)] + [HBM_SPEC] * nl,
        out_specs=(tile, pl.BlockSpec((None, d, tm), lambda b, j: (b, 0, j))) + (HBM_SPEC,) * nl,
        out_shape=(jax.ShapeDtypeStruct((nb, s, d), _MM), jax.ShapeDtypeStruct((nb, d, s), _MM))
        + tuple(jax.ShapeDtypeStruct(shp, a.dtype) for shp, a in zip(shapes, shards)),
        scratch_shapes=_gather_sems(nl),
        compiler_params=pltpu.CompilerParams(dimension_semantics=("arbitrary", "arbitrary")),
    )(x, g0, *shards)


def _fwd_seq(h, win, cw, cb, wa, ba, wx, bx, lam, pw, ps, late_shards, late_shapes, late_slicers, tm):
    nb, s, d = h.shape
    lw = cw.shape[1]
    pwid = ps.shape[1]
    nc = 2 * lw + 2 * pwid
    nt = s // tm
    nl = len(late_shards)
    n_steps = nb * nt

    def body(h_ref, win_ref, cw_ref, cb_ref, wa_ref, ba_ref, wx_ref, bx_ref, lam_ref, pw_ref, ps_ref, *rest):
        late_src, rest = rest[:nl], rest[nl:]
        z_ref, hl_ref, ya_ref, yb_ref, xc_ref, r_ref, i_ref, a_buf, diff_ref, yp_ref = rest[:N_SEQ_SAVED]
        late_out, rest = rest[N_SEQ_SAVED:N_SEQ_SAVED + nl], rest[N_SEQ_SAVED + nl:]
        xa_ext, xb_ext, carry, send_sems, recv_sems, local_sems = rest
        j = pl.program_id(1)
        step_no = pl.program_id(0) * nt + j
        g_start, g_forward, g_finish = _gather_phases(late_src, late_out, late_slicers, send_sems, recv_sems, local_sems)
        pl.when(step_no == 0)(g_start)
        pl.when(step_no == n_steps // 2)(g_forward)

        @pl.when(j == 0)
        def _():
            xa_ext[pl.ds(0, HALO), :] = jnp.zeros((HALO, lw), F32)
            xb_ext[pl.ds(0, HALO), :] = jnp.zeros((HALO, pwid), F32)
            carry[...] = jnp.zeros_like(carry)

        h16 = h_ref[...]
        xa = _dot(h16, win_ref[:, 0:lw])
        z_ref[:, 0:lw] = xa
        xa_ext[pl.ds(HALO, tm), :] = xa
        for c0 in range(lw, nc, lw):
            z_ref[:, c0:c0 + lw] = _dot(h16, win_ref[:, c0:c0 + lw])

        for h in range(N_HEADS):
            cols = slice(h * HEAD, (h + 1) * HEAD)
            xc = _conv_head(xa_ext, cw_ref, cb_ref, cols, tm)
            neg_c_sp = -LRU_C * _softplus_neg(lam_ref[:, cols])
            r, i, log_a = _lru_gates(xc, h, wa_ref, ba_ref, wx_ref, bx_ref, neg_c_sp)
            a = jnp.exp(log_a)
            xc_ref[:, cols] = xc
            r_ref[:, cols] = r
            i_ref[:, cols] = i
            a_buf[:, cols] = a
            hl_ref[:, cols] = _lru_mult(log_a, a)[0] * (i * xc)

        rows = lax.broadcasted_iota(jnp.int32, (SUB, lw), 0)

        def step(c, car):
            i0 = pl.multiple_of(c * SUB, SUB)
            av = a_buf[pl.ds(i0, SUB), :]
            bv = hl_ref[pl.ds(i0, SUB), :]
            for sh in (1, 2, 4):
                m = rows >= sh
                a_sh = jnp.where(m, pltpu.roll(av, sh, 0), 1.0)
                b_sh = jnp.where(m, pltpu.roll(bv, sh, 0), 0.0)
                bv = av * b_sh + bv
                av = av * a_sh
            hv = av * car + bv
            hl_ref[pl.ds(i0, SUB), :] = hv
            return jnp.broadcast_to(hv[SUB - 1:SUB, :], (SUB, lw))

        carry[...] = lax.fori_loop(0, tm // SUB, step, carry[...])

        ga = z_ref[:, lw:2 * lw]
        ya_ref[...] = (hl_ref[...] * (ga * _sigmoid(ga))).astype(ya_ref.dtype)

        xb_ext[pl.ds(HALO, tm), :] = z_ref[:, 2 * lw:2 * lw + pwid]
        pos = j * tm + lax.broadcasted_iota(jnp.int32, (tm, HEAD), 0)
        for g, k in enumerate(POOL_WINDOWS):
            cols = slice(g * HEAD, (g + 1) * HEAD)
            diff16 = _pool_diff(xb_ext, g, k, pos, tm)[0].astype(_MM)
            yp = _dot(diff16, pw_ref[g])
            diff_ref[:, cols] = diff16
            yp_ref[:, cols] = yp
            gb = z_ref[:, 2 * lw + pwid + g * HEAD:2 * lw + pwid + (g + 1) * HEAD]
            yb_ref[:, cols] = ((yp * ps_ref[:, cols]) * (gb * _sigmoid(gb))).astype(yb_ref.dtype)

        xa_ext[pl.ds(0, HALO), :] = xa_ext[pl.ds(tm, HALO), :]
        xb_ext[pl.ds(0, HALO), :] = xb_ext[pl.ds(tm, HALO), :]
        pl.when(step_no == n_steps - 1)(g_finish)

    tile = lambda w: pl.BlockSpec((None, tm, w), lambda b, j: (b, j, 0))
    return pl.pallas_call(
        body, name="fwd_seq", grid=(nb, nt),
        in_specs=[tile(d), _const_spec((d, nc)), _const_spec(cw.shape), _const_spec(cb.shape),
                  _const_spec(wa.shape), _const_spec(ba.shape), _const_spec(wx.shape), _const_spec(bx.shape),
                  _const_spec(lam.shape), _const_spec(pw.shape), _const_spec(ps.shape)] + [HBM_SPEC] * nl,
        out_specs=(tile(nc), tile(lw), tile(lw), tile(pwid), tile(lw), tile(lw), tile(lw), tile(lw), tile(pwid),
                   tile(pwid)) + (HBM_SPEC,) * nl,
        out_shape=(jax.ShapeDtypeStruct((nb, s, nc), F32), jax.ShapeDtypeStruct((nb, s, lw), F32),
                   jax.ShapeDtypeStruct((nb, s, lw), _MM), jax.ShapeDtypeStruct((nb, s, pwid), _MM))
        + (jax.ShapeDtypeStruct((nb, s, lw), F32),) * 4
        + (jax.ShapeDtypeStruct((nb, s, pwid), _MM), jax.ShapeDtypeStruct((nb, s, pwid), F32))
        + tuple(jax.ShapeDtypeStruct(shp, a.dtype) for shp, a in zip(late_shapes, late_shards)),
        scratch_shapes=[pltpu.VMEM((tm + HALO, lw), F32), pltpu.VMEM((tm + HALO, pwid), F32),
                        pltpu.VMEM((SUB, lw), F32)] + _gather_sems(nl),
        compiler_params=pltpu.CompilerParams(dimension_semantics=("arbitrary", "arbitrary"), vmem_limit_bytes=VMEM_LIMIT),
    )(h, win, cw, cb, wa, ba, wx, bx, lam, pw, ps, *late_shards)


def _store_chunks(ref, row0, vec):
    for h in range(vec.shape[1] // HEAD):
        ref[pl.ds(row0 + h, 1), :] += vec[:, h * HEAD:(h + 1) * HEAD]


def _mid(ya, yb, h, wmerge, x, p, tgt, wpl, wpp, wout, wpg, wpe, g1, gf, tm):
    t, d = x.shape
    lw = ya.shape[1]
    pwid = yb.shape[1]
    pdim = p.shape[1]
    n = t // tm
    sec_pp = wpp.shape[1] // N_DEV
    sec_pe = wpe.shape[1] // N_DEV

    def body(ya_ref, yb_ref, h_ref, wmerge_ref, x_ref, p_ref, tgt_ref, wpl_ref, wpp_ref, wout_ref, wpg_ref, wpe_ref,
             g1_ref, gf_ref,
             dya_ref, dyb_ref, dm_ref, dx1_ref, gpl_ref, gpp_ref, gout_ref, gpg_ref, gpe_ref, pack_ref,
             acc_pl, acc_pp, acc_out, acc_pg, acc_pe):
        i = pl.program_id(0)

        @pl.when(i == 0)
        def _():
            for acc in (acc_pl, acc_pp, acc_out, acc_pg, acc_pe):
                acc[...] = jnp.zeros_like(acc)
            pack_ref[...] = jnp.zeros_like(pack_ref)

        ya16 = ya_ref[...]
        yb16 = yb_ref[...]
        br_a = _dot(ya16, wpl_ref[...])
        br_b = _dot(yb16, wpp_ref[...])
        logits = _dot(h_ref[...], wmerge_ref[...])
        sa = _sigmoid(logits[:, 0:d])
        sb = _sigmoid(logits[:, d:2 * d])
        mg16 = (sa * br_a + sb * br_b).astype(_MM)
        x1 = x_ref[...] + _dot(mg16, wout_ref[...])
        r1 = _rsqrt_mean_sq(x1)
        n1 = x1 * r1
        h116 = (n1 * g1_ref[...]).astype(_MM)
        gate = _sigmoid(_dot(h116, wpg_ref[...]))
        p16 = p_ref[...].astype(_MM)
        pe = _dot(p16, wpe_ref[...])
        x2 = x1 + gate * pe
        r2 = _rsqrt_mean_sq(x2)
        n2 = x2 * r2
        err = n2 * gf_ref[...] - tgt_ref[...]
        sq = jnp.sum(_colsum(err * err), axis=1, keepdims=True)
        pack_ref[pl.ds(MID_LOSS_ROW, 1), :] += jnp.broadcast_to(sq * (0.5 / d), (1, HEAD))

        dy = err * (1.0 / d)
        _store_chunks(pack_ref, SUB, _colsum(dy * n2))
        dn2 = dy * gf_ref[...]
        dx2 = r2 * (dn2 - n2 * jnp.mean(dn2 * n2, axis=-1, keepdims=True))
        dpe16 = (dx2 * gate).astype(_MM)
        dpg16 = ((dx2 * pe) * (gate * (1.0 - gate))).astype(_MM)
        acc_pe[...] += _dot_tn(p16, dpe16)
        acc_pg[...] += _dot_tn(h116, dpg16)
        dh1 = _dot_nt(dpg16, wpg_ref[...])
        _store_chunks(pack_ref, 0, _colsum(dh1 * n1))
        dn1 = dh1 * g1_ref[...]
        dx1 = dx2 + r1 * (dn1 - n1 * jnp.mean(dn1 * n1, axis=-1, keepdims=True))
        dx1_ref[...] = dx1
        dx116 = dx1.astype(_MM)
        acc_out[...] += _dot_tn(mg16, dx116)
        dmg = _dot_nt(dx116, wout_ref[...])
        da16 = (dmg * sa).astype(_MM)
        db16 = (dmg * sb).astype(_MM)
        dm_ref[:, 0:d] = ((dmg * br_a) * (sa * (1.0 - sa))).astype(dm_ref.dtype)
        dm_ref[:, d:2 * d] = ((dmg * br_b) * (sb * (1.0 - sb))).astype(dm_ref.dtype)
        acc_pl[...] += _dot_tn(ya16, da16)
        acc_pp[...] += _dot_tn(yb16, db16)
        dya_ref[...] = _dot_nt(da16, wpl_ref[...])
        dyb_ref[...] = _dot_nt(db16, wpp_ref[...])

        @pl.when(i == n - 1)
        def _():
            pltpu.sync_copy(acc_pl, gpl_ref)
            pltpu.sync_copy(acc_out, gout_ref)
            pltpu.sync_copy(acc_pg, gpg_ref)
            for k in range(N_DEV):
                pltpu.sync_copy(acc_pp.at[:, pl.ds(k * sec_pp, sec_pp)], gpp_ref.at[k])
                pltpu.sync_copy(acc_pe.at[:, pl.ds(k * sec_pe, sec_pe)], gpe_ref.at[k])

    tile = lambda w: pl.BlockSpec((tm, w), lambda i: (i, 0))
    any_spec = pl.BlockSpec(memory_space=pl.ANY)
    full = lambda shape: pl.BlockSpec(shape, lambda i: (0,) * len(shape))
    return pl.pallas_call(
        body, name="mid", grid=(n,),
        in_specs=[tile(lw), tile(pwid), tile(d), _const_spec(wmerge.shape), tile(d), tile(pdim), tile(d),
                  _const_spec(wpl.shape), _const_spec(wpp.shape), _const_spec(wout.shape), _const_spec(wpg.shape),
                  _const_spec(wpe.shape), _const_spec(g1.shape), _const_spec(gf.shape)],
        out_specs=(tile(lw), tile(pwid), tile(2 * d), tile(d), any_spec, any_spec, any_spec, any_spec, any_spec,
                   full((MID_PACK_ROWS, HEAD))),
        out_shape=(jax.ShapeDtypeStruct((t, lw), F32), jax.ShapeDtypeStruct((t, pwid), F32),
                   jax.ShapeDtypeStruct((t, 2 * d), _MM), jax.ShapeDtypeStruct((t, d), F32),
                   jax.ShapeDtypeStruct(wpl.shape, F32), jax.ShapeDtypeStruct((N_DEV, wpp.shape[0], sec_pp), F32),
                   jax.ShapeDtypeStruct(wout.shape, F32), jax.ShapeDtypeStruct(wpg.shape, F32),
                   jax.ShapeDtypeStruct((N_DEV, wpe.shape[0], sec_pe), F32),
                   jax.ShapeDtypeStruct((MID_PACK_ROWS, HEAD), F32)),
        scratch_shapes=[pltpu.VMEM(wpl.shape, F32), pltpu.VMEM(wpp.shape, F32), pltpu.VMEM(wout.shape, F32),
                        pltpu.VMEM(wpg.shape, F32), pltpu.VMEM(wpe.shape, F32)],
        compiler_params=pltpu.CompilerParams(dimension_semantics=("arbitrary",), vmem_limit_bytes=VMEM_LIMIT),
    )(ya, yb, h, wmerge, x, p, tgt, wpl, wpp, wout, wpg, wpe, g1, gf)


def _bwd_seq(z, hl, saved, dya, dyb, dm, cw, wa, wx, lam, pw, ps, sib_grads, tm):
    nb, s, zw = z.shape
    lw = cw.shape[1]
    pwid = ps.shape[1]
    nt = s // tm
    nc = zw + dm.shape[2]
    R = _SEQ_ROWS
    ns = len(sib_grads)

    def body(zq_ref, hl_ref, hlh_ref, xc_buf, r_buf, i_buf, a_ref, diff_ref, yp_ref, dya_ref, dyb_ref, dm_ref,
             cw_ref, wa_ref, wx_ref, lam_ref, pw_ref, ps_ref, *rest):
        sib_src, rest = rest[:ns], rest[ns:]
        dz_ref, pack_ref = rest[:2]
        sib_out, rest = rest[2:2 + ns], rest[2 + ns:]
        a_ext, an_buf, hl_ext, dh_buf, dxc_ext, q_ext, carry, send_sems, recv_sems = rest
        b = pl.program_id(0)
        j = pl.program_id(1)
        jr = nt - 1 - j
        has_prev = jr > 0

        @pl.when((b == 0) & (j == 0))
        def _():
            for cp in _sibling_copies(sib_src, sib_out, send_sems, recv_sems):
                cp.start()

        dz_ref[:, zw:nc] = dm_ref[...]

        @pl.when((b == 0) & (j == 0))
        def _():
            pack_ref[...] = jnp.zeros_like(pack_ref)

        @pl.when(j == 0)
        def _():
            a_ext[pl.ds(tm, SUB), :] = jnp.zeros((SUB, lw), F32)
            dxc_ext[pl.ds(tm, HALO), :] = jnp.zeros((HALO, lw), F32)
            q_ext[pl.ds(tm, HALO), :] = jnp.zeros((HALO, pwid), F32)
            carry[...] = jnp.zeros_like(carry)

        hl_ext[pl.ds(0, SUB), :] = jnp.where(has_prev, hlh_ref[...], 0.0)
        hl_ext[pl.ds(SUB, tm), :] = hl_ref[...]
        a_ext[pl.ds(0, tm), :] = a_ref[...]
        for h in range(N_HEADS):
            cols = slice(h * HEAD, (h + 1) * HEAD)
            ga = zq_ref[:, lw + h * HEAD:lw + (h + 1) * HEAD]
            sg = _sigmoid(ga)
            dyav = dya_ref[:, cols]
            dh_buf[:, cols] = dyav * (ga * sg)
            dga = (dyav * hl_ext[pl.ds(SUB, tm), cols]) * (sg * (1.0 + ga * (1.0 - sg)))
            dz_ref[:, lw + h * HEAD:lw + (h + 1) * HEAD] = dga.astype(dz_ref.dtype)
        an_buf[...] = _rows_ahead(a_ext[...], 1)[0:tm]

        rows = lax.broadcasted_iota(jnp.int32, (SUB, lw), 0)
        nch = tm // SUB

        def step(c, car):
            i0 = pl.multiple_of((nch - 1 - c) * SUB, SUB)
            cv = an_buf[pl.ds(i0, SUB), :]
            bv = dh_buf[pl.ds(i0, SUB), :]
            for sh in (1, 2, 4):
                m = rows < SUB - sh
                c_sh = jnp.where(m, pltpu.roll(cv, SUB - sh, 0), 1.0)
                b_sh = jnp.where(m, pltpu.roll(bv, SUB - sh, 0), 0.0)
                bv = cv * b_sh + bv
                cv = cv * c_sh
            hv = cv * car + bv
            dh_buf[pl.ds(i0, SUB), :] = hv
            return jnp.broadcast_to(hv[0:1, :], (SUB, lw))

        carry[...] = lax.fori_loop(0, nch, step, carry[...])

        for h in range(N_HEADS):
            cols = slice(h * HEAD, (h + 1) * HEAD)
            lam_h = lam_ref[:, cols]
            neg_c_sp = -LRU_C * _softplus_neg(lam_h)
            xc = xc_buf[:, cols]
            r = r_buf[:, cols]
            i = i_buf[:, cols]
            a = a_ref[:, cols]
            mult, inv_mult = _lru_mult(neg_c_sp * r, a)
            dh = dh_buf[:, cols]
            dhx = dh * xc
            di = dhx * mult
            h_prev = _rows_back(hl_ext[:, cols], 1)[SUB:SUB + tm]
            dlog_a = (dh * h_prev) * a - (dhx * i) * ((a * a) * inv_mult)
            pack_ref[pl.ds(R["dlam"][0] + h, 1), :] += _colsum(dlog_a * r) * (LRU_C * _sigmoid(-lam_h))
            dpa = (dlog_a * neg_c_sp) * (r * (1.0 - r))
            dpx = di * (i * (1.0 - i))
            dpa16 = dpa.astype(_MM)
            dpx16 = dpx.astype(_MM)
            xc16 = xc.astype(_MM)
            dxc = dh * (mult * i) + _dot_nt(dpa16, wa_ref[h]) + _dot_nt(dpx16, wx_ref[h])
            pack_ref[pl.ds(R["dwa"][0] + h * HEAD, HEAD), :] += _dot_tn(xc16, dpa16)
            pack_ref[pl.ds(R["dwx"][0] + h * HEAD, HEAD), :] += _dot_tn(xc16, dpx16)
            pack_ref[pl.ds(R["dba"][0] + h, 1), :] += _colsum(dpa)
            pack_ref[pl.ds(R["dbx"][0] + h, 1), :] += _colsum(dpx)
            pack_ref[pl.ds(R["dcb"][0] + h, 1), :] += _colsum(dxc)
            dxc_ext[pl.ds(0, tm), cols] = dxc

        for h in range(N_HEADS):
            cols = slice(h * HEAD, (h + 1) * HEAD)
            dxc_all = dxc_ext[:, cols]
            xa = zq_ref[:, cols]
            dxa = None
            for k in range(4):
                dxc_k = _rows_ahead(dxc_all, 3 - k)[0:tm]
                pack_ref[pl.ds(R["dcw"][0] + SUB * k + h, 1), :] += _colsum(dxc_k * xa)
                term = cw_ref[pl.ds(k, 1), cols] * dxc_k
                dxa = term if dxa is None else dxa + term
            dz_ref[:, cols] = dxa.astype(dz_ref.dtype)

        pos = jr * tm + lax.broadcasted_iota(jnp.int32, (tm, HEAD), 0)
        for g, k in enumerate(POOL_WINDOWS):
            cols = slice(g * HEAD, (g + 1) * HEAD)
            inv_cnt = 1.0 / jnp.minimum(pos + 1, k).astype(F32)
            diff16 = diff_ref[:, cols]
            yp = yp_ref[:, cols]
            sc = ps_ref[:, cols]
            gb = zq_ref[:, 2 * lw + pwid + g * HEAD:2 * lw + pwid + (g + 1) * HEAD]
            sgb = _sigmoid(gb)
            dybv = dyb_ref[:, cols]
            dy_pool = dybv * (gb * sgb)
            dgb = (dybv * (yp * sc)) * (sgb * (1.0 + gb * (1.0 - sgb)))
            pack_ref[pl.ds(R["dps"][0] + g, 1), :] += _colsum(dy_pool * yp)
            dyp16 = (dy_pool * sc).astype(_MM)
            pack_ref[pl.ds(R["dpw"][0] + g * HEAD, HEAD), :] += _dot_tn(diff16, dyp16)
            ddiff = _dot_nt(dyp16, pw_ref[g])
            q_ext[pl.ds(0, tm), cols] = ddiff * inv_cnt
            dxb = _window_sum(q_ext[:, cols], k, _rows_ahead)[0:tm] - ddiff
            dz_ref[:, 2 * lw + g * HEAD:2 * lw + (g + 1) * HEAD] = dxb.astype(dz_ref.dtype)
            dz_ref[:, 2 * lw + pwid + g * HEAD:2 * lw + pwid + (g + 1) * HEAD] = dgb.astype(dz_ref.dtype)

        a_ext[pl.ds(tm, SUB), :] = a_ext[pl.ds(0, SUB), :]
        dxc_ext[pl.ds(tm, HALO), :] = dxc_ext[pl.ds(0, HALO), :]
        q_ext[pl.ds(tm, HALO), :] = q_ext[pl.ds(0, HALO), :]

        @pl.when((b == nb - 1) & (j == nt - 1))
        def _():
            for cp in _sibling_copies(sib_src, sib_out, send_sems, recv_sems):
                cp.wait()

    rev = lambda w: pl.BlockSpec((None, tm, w), lambda b, j: (b, nt - 1 - j, 0))
    prev_rows = lambda rows, w: pl.BlockSpec(
        (None, rows, w), lambda b, j: (b, jnp.maximum((nt - 1 - j) * (tm // rows) - 1, 0), 0))
    return pl.pallas_call(
        body, name="bwd_seq", grid=(nb, nt),
        in_specs=[rev(zw), rev(lw), prev_rows(SUB, lw), rev(lw), rev(lw), rev(lw), rev(lw), rev(pwid), rev(pwid),
                  rev(lw), rev(pwid), rev(nc - zw),
                  _const_spec(cw.shape), _const_spec(wa.shape), _const_spec(wx.shape), _const_spec(lam.shape),
                  _const_spec(pw.shape), _const_spec(ps.shape)] + [HBM_SPEC] * ns,
        out_specs=(rev(nc), pl.BlockSpec((SEQ_PACK_ROWS, HEAD), lambda b, j: (0, 0))) + (HBM_SPEC,) * ns,
        out_shape=(jax.ShapeDtypeStruct((nb, s, nc), _MM), jax.ShapeDtypeStruct((SEQ_PACK_ROWS, HEAD), F32))
        + tuple(jax.ShapeDtypeStruct((4,) + g.shape[1:], g.dtype) for g in sib_grads),
        scratch_shapes=[pltpu.VMEM((tm + SUB, lw), F32), pltpu.VMEM((tm, lw), F32), pltpu.VMEM((tm + SUB, lw), F32),
                        pltpu.VMEM((tm, lw), F32), pltpu.VMEM((tm + HALO, lw), F32),
                        pltpu.VMEM((tm + HALO, pwid), F32), pltpu.VMEM((SUB, lw), F32),
                        pltpu.SemaphoreType.DMA((ns, 4)), pltpu.SemaphoreType.DMA((ns, 4))],
        compiler_params=pltpu.CompilerParams(dimension_semantics=("arbitrary", "arbitrary"), vmem_limit_bytes=VMEM_LIMIT),
    )(z, hl, hl, *saved, dya, dyb, dm, cw, wa, wx, lam, pw, ps, *sib_grads)


def _bwd_win(ht, dz, chip_srcs, small):
    nb, d, s = ht.shape
    nc = dz.shape[2]
    sec = nc // N_DEV
    n_chips = N_DEV // 2
    nm = len(chip_srcs)

    def body(ht_ref, dz_ref, *rest):
        chip_src, small_src = rest[:nm], rest[nm]
        gwin_ref, recv_ref = rest[nm + 1:nm + 3]
        chip_out, small_out = rest[nm + 3:2 * nm + 3], rest[2 * nm + 3]
        acc, local_sems, sib_send, sib_recv, chip_send, chip_recv, small_send, small_recv = rest[2 * nm + 4:]
        q = pl.program_id(0)
        b = pl.program_id(1)
        slot = q % 2
        x, y, c = _my_pos()

        def half(sl, core):
            return acc.at[sl, :, pl.ds(pl.multiple_of(core * sec, HEAD), sec)]

        def local_copy(qq, sl):
            return pltpu.make_async_copy(half(sl, c), gwin_ref.at[2 * qq + c], local_sems.at[sl])

        def sib_copy(qq, sl):
            return pltpu.make_async_remote_copy(
                src_ref=half(sl, 1 - c), dst_ref=recv_ref.at[qq], send_sem=sib_send.at[qq],
                recv_sem=sib_recv.at[qq], device_id=(x, y, 1 - c), device_id_type=MESH)

        def others():
            return (_chip_copies(chip_src, chip_out, chip_send, chip_recv)
                    + _sibling_copies([small_src], [small_out], small_send, small_recv))

        @pl.when((q == 0) & (b == 0))
        def _():
            for cp in others():
                cp.start()

        @pl.when((q >= 2) & (b == 0))
        def _():
            local_copy(q - 2, slot).wait()
            sib_copy(q - 2, slot).wait_send()

        part = _dot(ht_ref[b], dz_ref[...])

        @pl.when(b == 0)
        def _():
            acc[slot] = part

        @pl.when(b != 0)
        def _():
            acc[slot] += part

        @pl.when(b == nb - 1)
        def _():
            local_copy(q, slot).start()
            sib_copy(q, slot).start()

        @pl.when((q == n_chips - 1) & (b == nb - 1))
        def _():
            for qq in (n_chips - 2, n_chips - 1):
                local_copy(qq, qq % 2).wait()
                sib_copy(qq, qq % 2).wait_send()
            for qq in range(n_chips):
                sib_copy(qq, 0).wait_recv()
            for cp in others():
                cp.wait()

    return pl.pallas_call(
        body, name="bwd_win", grid=(n_chips, nb),
        in_specs=[_const_spec(ht.shape), pl.BlockSpec((None, s, 2 * sec), lambda q, b: (b, 0, q))]
        + [HBM_SPEC] * (nm + 1),
        out_specs=(HBM_SPEC,) * (nm + 3),
        out_shape=(jax.ShapeDtypeStruct((N_DEV, d, sec), F32), jax.ShapeDtypeStruct((n_chips, d, sec), F32))
        + tuple(jax.ShapeDtypeStruct((3,) + g.shape[1:], g.dtype) for g in chip_srcs)
        + (jax.ShapeDtypeStruct((4,) + small.shape[1:], small.dtype),),
        scratch_shapes=[pltpu.VMEM((2, d, 2 * sec), F32), pltpu.SemaphoreType.DMA((2,)),
                        pltpu.SemaphoreType.DMA((N_DEV // 2,)), pltpu.SemaphoreType.DMA((N_DEV // 2,)),
                        pltpu.SemaphoreType.DMA((nm, 3)), pltpu.SemaphoreType.DMA((nm, 3)),
                        pltpu.SemaphoreType.DMA((1, 4)), pltpu.SemaphoreType.DMA((1, 4))],
        compiler_params=pltpu.CompilerParams(dimension_semantics=("arbitrary", "arbitrary"), vmem_limit_bytes=VMEM_LIMIT),
    )(ht, dz, *chip_srcs, small)


def _bwd_dx(dz, x, dx1, win, g0, chip_srcs, tm):
    t, d = x.shape
    nc = win.shape[1]
    n = t // tm
    nm = len(chip_srcs)

    def body(dz_ref, x_ref, dx1_ref, win_ref, g0_ref, *rest):
        chip_src = rest[:nm]
        gx_ref, pack_ref = rest[nm:nm + 2]
        chip_out = rest[nm + 2:2 * nm + 2]
        chip_send, chip_recv = rest[2 * nm + 2:]
        i = pl.program_id(0)

        @pl.when(i == 0)
        def _():
            pack_ref[...] = jnp.zeros_like(pack_ref)
            for cp in _chip_copies(chip_src, chip_out, chip_send, chip_recv):
                cp.start()

        xv = x_ref[...]
        r0 = _rsqrt_mean_sq(xv)
        xh = xv * r0
        dh = _dot_nt(dz_ref[...], win_ref[...])
        _store_chunks(pack_ref, 0, _colsum(dh * xh))
        dxh = dh * g0_ref[...]
        gx_ref[...] = dx1_ref[...] + r0 * (dxh - xh * jnp.mean(dxh * xh, axis=-1, keepdims=True))

        @pl.when(i == n - 1)
        def _():
            for cp in _chip_copies(chip_src, chip_out, chip_send, chip_recv):
                cp.wait()

    tile = lambda w: pl.BlockSpec((tm, w), lambda i: (i, 0))
    return pl.pallas_call(
        body, name="bwd_dx", grid=(n,),
        in_specs=[tile(nc), tile(d), tile(d), _const_spec(win.shape), _const_spec(g0.shape)] + [HBM_SPEC] * nm,
        out_specs=(tile(d), pl.BlockSpec((INP_PACK_ROWS, HEAD), lambda i: (0, 0))) + (HBM_SPEC,) * nm,
        out_shape=(jax.ShapeDtypeStruct((t, d), F32), jax.ShapeDtypeStruct((INP_PACK_ROWS, HEAD), F32))
        + tuple(jax.ShapeDtypeStruct((3,) + g.shape[1:], g.dtype) for g in chip_srcs),
        scratch_shapes=[pltpu.SemaphoreType.DMA((nm, 3)), pltpu.SemaphoreType.DMA((nm, 3))],
        compiler_params=pltpu.CompilerParams(dimension_semantics=("arbitrary",), vmem_limit_bytes=VMEM_LIMIT),
    )(dz, x, dx1, win, g0, *chip_srcs)


def _all_gather(shards, full_shapes, slicers, name):
    nw = len(shards)

    def body(*refs):
        start, forward, finish = _gather_phases(refs[:nw], refs[nw:2 * nw], slicers, *refs[2 * nw:])
        start()
        forward()
        finish()

    return pl.pallas_call(
        body, name=name, in_specs=[HBM_SPEC] * nw, out_specs=(HBM_SPEC,) * nw,
        out_shape=tuple(jax.ShapeDtypeStruct(shp, a.dtype) for shp, a in zip(full_shapes, shards)),
        scratch_shapes=_gather_sems(nw),
    )(*shards)


ROW_BLOCKS = 8


def _chip_sums(core_idx, grads, recvs, dtypes, name):
    nw = len(grads)

    def body(idx_ref, *refs):
        gs, rs, outs = refs[:nw], refs[nw:2 * nw], refs[2 * nw:]
        for g, r, o in zip(gs, rs, outs):
            o[...] = (g[...] + r[...]).astype(o.dtype)

    def blk(g):
        return (None,) + g.shape[1:]

    return pl.pallas_call(
        body, name=name,
        grid_spec=pltpu.PrefetchScalarGridSpec(
            num_scalar_prefetch=1, grid=(4,),
            in_specs=[pl.BlockSpec(blk(g), lambda q, s: (2 * q + s[0], 0, 0)) for g in grads]
            + [pl.BlockSpec(blk(g), lambda q, s: (q, 0, 0)) for g in grads],
            out_specs=tuple(pl.BlockSpec(blk(g), lambda q, s: (q, 0, 0)) for g in grads)),
        out_shape=tuple(jax.ShapeDtypeStruct((4,) + g.shape[1:], dt) for g, dt in zip(grads, dtypes)),
    )(core_idx, *grads, *recvs)


def _adamw(w, g, m, v):
    m = ADAM_B1 * m + (1.0 - ADAM_B1) * g
    v = ADAM_B2 * v + (1.0 - ADAM_B2) * (g * g)
    m_hat = m / (1.0 - ADAM_B1 ** ADAM_STEP)
    v_hat = v / (1.0 - ADAM_B2 ** ADAM_STEP)
    delta = -ADAM_LR * (m_hat / (jnp.sqrt(v_hat) + ADAM_EPS) + ADAM_WD * w)
    return delta, m, v


def _adam_sections(sec_idx, grads, recv_sib, recv_chips, wmv):
    nw = len(wmv)

    def body(idx_ref, *refs):
        gs, rs, cs = refs[:nw + 1], refs[nw + 1:2 * nw + 2], refs[2 * nw + 2:3 * nw + 3]
        params = refs[3 * nw + 3:6 * nw + 3]
        outs = refs[6 * nw + 3:]
        for w in range(nw + 1):
            g = gs[w][...] + rs[w][...]
            for k in range(3):
                g = g + cs[w][k].astype(F32)
            if w == nw:
                outs[4 * nw][...] = g
                continue
            wv, mv, vv = (params[3 * w + t][...] for t in range(3))
            delta, m_new, v_new = _adamw(wv, g, mv, vv)
            outs[4 * w][...] = g
            outs[4 * w + 1][...] = delta
            outs[4 * w + 2][...] = m_new
            outs[4 * w + 3][...] = v_new

    def rb(g):
        return g.shape[1] // ROW_BLOCKS

    in_specs = [pl.BlockSpec((None, rb(g), g.shape[2]), lambda i, s: (s[0], i, 0)) for g in grads]
    in_specs += [pl.BlockSpec((None, rb(g), g.shape[2]), lambda i, s: (s[1], i, 0)) for g in grads]
    in_specs += [pl.BlockSpec((3, rb(g), g.shape[2]), lambda i, s: (0, i, 0)) for g in grads]
    sec = lambda g: pl.BlockSpec((rb(g), g.shape[2]), lambda i, s: (i, 0))
    for w in range(nw):
        in_specs += [sec(grads[w])] * 3
    out_specs, out_shape = [], []
    for g in grads[:nw]:
        out_specs += [sec(g)] * 4
        out_shape += [jax.ShapeDtypeStruct(g.shape[1:], F32)] * 4
    out_specs.append(sec(grads[nw]))
    out_shape.append(jax.ShapeDtypeStruct(grads[nw].shape[1:], F32))
    flat = [a for t in wmv for a in t]
    outs = pl.pallas_call(
        body, name="adam_sections",
        grid_spec=pltpu.PrefetchScalarGridSpec(num_scalar_prefetch=1, grid=(ROW_BLOCKS,), in_specs=in_specs,
                                               out_specs=tuple(out_specs)),
        out_shape=tuple(out_shape),
    )(sec_idx, *grads, *recv_sib, *recv_chips, *flat)
    return [outs[4 * w:4 * w + 4] for w in range(nw)], outs[4 * nw]


def _adam_small(sec_idx, gpack, parts, parts_wmv, vec_params, mat_params, conv_wmv):
    items = [(0, parts_wmv, "parts")]
    items += [(r0, t, "vec") for r0, t in vec_params] + [(r0, t, "mat") for r0, t in mat_params]
    items.append((_SEQ_ROWS["dcw"][0], conv_wmv, "conv"))

    def body(idx_ref, g_ref, parts_ref, *refs):
        ins, outs = refs[:3 * len(items)], refs[3 * len(items):]
        for n, (r0, _, kind) in enumerate(items):
            w_ref, m_ref, v_ref = ins[3 * n:3 * n + 3]
            o = outs[4 * n:4 * n + 4]
            if kind == "mat":
                g = g_ref[pl.ds(r0, w_ref.shape[0]), :]
                res = (g,) + _adamw(w_ref[...], g, m_ref[...], v_ref[...])
                for ref, val in zip(o, res):
                    ref[...] = val
            elif kind in ("vec", "parts"):
                for h in range(w_ref.shape[1] // HEAD):
                    cols = slice(h * HEAD, (h + 1) * HEAD)
                    if kind == "vec":
                        g = g_ref[pl.ds(r0 + h, 1), :]
                    else:
                        g = parts_ref[pl.ds(h, 1), :]
                        for dev in range(1, N_DEV):
                            g = g + parts_ref[pl.ds(SUB * dev + h, 1), :]
                    res = (g,) + _adamw(w_ref[:, cols], g, m_ref[:, cols], v_ref[:, cols])
                    for ref, val in zip(o, res):
                        ref[:, cols] = val
            else:
                rows = lax.broadcasted_iota(jnp.int32, (SUB, HEAD), 0)
                for k in range(4):
                    blk = g_ref[pl.ds(r0 + SUB * k, SUB), :]
                    g = jnp.sum(jnp.where(rows == idx_ref[0], blk, 0.0), axis=0, keepdims=True)
                    row = pl.ds(k, 1)
                    res = (g,) + _adamw(w_ref[row, :], g, m_ref[row, :], v_ref[row, :])
                    for ref, val in zip(o, res):
                        ref[row, :] = val
        outs[4 * len(items)][...] = g_ref[pl.ds(_OFF_MID + MID_LOSS_ROW, 1), :]

    flat = [a for _, t, _ in items for a in t]
    full = lambda a: pl.BlockSpec(a.shape, lambda i, s: (0,) * a.ndim)
    out_specs, out_shape = [], []
    for _, t, _ in items:
        out_specs += [full(t[0])] * 4
        out_shape += [jax.ShapeDtypeStruct(t[0].shape, F32)] * 4
    loss_row = jax.ShapeDtypeStruct((1, HEAD), F32)
    out_specs.append(full(loss_row))
    out_shape.append(loss_row)
    outs = pl.pallas_call(
        body, name="adam_small",
        grid_spec=pltpu.PrefetchScalarGridSpec(num_scalar_prefetch=1, grid=(1,),
                                               in_specs=[full(gpack), full(parts)] + [full(a) for a in flat],
                                               out_specs=tuple(out_specs)),
        out_shape=tuple(out_shape),
    )(sec_idx, gpack, parts, *flat)
    return [outs[4 * n:4 * n + 4] for n in range(len(items))], outs[4 * len(items)]


def _col_slicer(width):
    return lambda ref, idx: ref.at[:, pl.ds(pl.multiple_of(idx * width, HEAD), width)]


def _row_slicer(rows):
    return lambda ref, idx: ref.at[pl.ds(pl.multiple_of(idx * rows, SUB), rows), :]


def kernel(x, p, norm_g, w_in, conv_w, conv_b, lru_w_a, lru_b_a, lru_w_x, lru_b_x, lru_lambda, pool_w, pool_scale, w_proj_lru, w_proj_pool, w_out, ple_norm_g, w_ple_gate, w_ple_proj, final_g, loss_target, m_norm_g, m_w_in, m_conv_w, m_conv_b, m_lru_w_a, m_lru_b_a, m_lru_w_x, m_lru_b_x, m_lru_lambda, m_pool_w, m_pool_scale, m_w_proj_lru, m_w_proj_pool, m_w_out, m_ple_norm_g, m_w_ple_gate, m_w_ple_proj, m_final_g, v_norm_g, v_w_in, v_conv_w, v_conv_b, v_lru_w_a, v_lru_b_a, v_lru_w_x, v_lru_b_x, v_lru_lambda, v_pool_w, v_pool_scale, v_w_proj_lru, v_w_proj_pool, v_w_out, v_ple_norm_g, v_w_ple_gate, v_w_ple_proj, v_final_g):
    nb, s, d = x.shape
    t = nb * s
    tm = min(TILE_M, s)
    sec_idx = (4 * lax.axis_index("x") + 2 * lax.axis_index("y") + lax.axis_index("c")).astype(jnp.int32)
    chip_idx = (2 * lax.axis_index("x") + lax.axis_index("y")).astype(jnp.int32)
    core_idx = lax.axis_index("c").astype(jnp.int32)

    h, ht, win, cw = _pre(
        x, norm_g, [w_in[0].astype(_MM), conv_w[0]],
        [(d, N_DEV * w_in.shape[2]), (conv_w.shape[1], N_DEV * conv_w.shape[2])],
        [_col_slicer(w_in.shape[2]), _col_slicer(conv_w.shape[2])], tm)
    late_shards = [w_proj_lru[0].astype(_MM), w_proj_pool[0].astype(_MM), w_out[0].astype(_MM),
                   w_ple_gate[0].astype(_MM), w_ple_proj[0].astype(_MM)]
    late_shapes = [(N_DEV * w_proj_lru.shape[1], d), (w_proj_pool.shape[1], N_DEV * w_proj_pool.shape[2]),
                   (N_DEV * w_out.shape[1], d), (N_DEV * w_ple_gate.shape[1], d),
                   (w_ple_proj.shape[1], N_DEV * w_ple_proj.shape[2])]
    late_slicers = [_row_slicer(w_proj_lru.shape[1]), _col_slicer(w_proj_pool.shape[2]), _row_slicer(w_out.shape[1]),
                    _row_slicer(w_ple_gate.shape[1]), _col_slicer(w_ple_proj.shape[2])]

    lw = cw.shape[1]
    pwid = pool_scale.shape[1]
    wa = lru_w_a[0].astype(_MM)
    wx = lru_w_x[0].astype(_MM)
    pw = pool_w[0].astype(_MM)
    ba, bx = lru_b_a[0], lru_b_x[0]
    gf = final_g.reshape(1, d)
    rows_of = lambda g: g.reshape(N_DEV, g.shape[0] // N_DEV, g.shape[1])
    core = core_idx.reshape(1)

    fwd_out = _fwd_seq(
        h, win, cw, conv_b, wa, ba, wx, bx, lru_lambda, pw, pool_scale, late_shards, late_shapes, late_slicers, tm)
    z, hl, ya, yb = fwd_out[:4]
    saved = fwd_out[4:N_SEQ_SAVED]
    wpl, wpp, wout, wpg, wpe = fwd_out[N_SEQ_SAVED:]
    x2d = x.reshape(t, d)
    (dya, dyb, dm, dx1, g_pl, g_pp, g_out, g_pg, g_pe, pack_mid) = _mid(
        ya.reshape(t, lw), yb.reshape(t, pwid), h.reshape(t, d), win[:, 2 * lw + 2 * pwid:], x2d, p[0].reshape(t, -1),
        loss_target.reshape(t, d), wpl, wpp, wout, wpg, wpe, ple_norm_g, gf, tm)
    grads_mid = [rows_of(g_pl), g_pp, rows_of(g_out), rows_of(g_pg), g_pe]
    seq_out = _bwd_seq(z, hl, saved, dya.reshape(nb, s, lw), dyb.reshape(nb, s, pwid), dm.reshape(nb, s, -1), cw,
                       wa, wx, lru_lambda, pw, pool_scale, grads_mid, tm)
    dz, pack_seq, sib_mid = seq_out[0], seq_out[1], list(seq_out[2:])
    sums_mid = _chip_sums(core, grads_mid, sib_mid, [_WIRE] * len(grads_mid), "chip_sums_mid")
    small = jnp.concatenate(
        [pack_seq, pack_mid, jnp.zeros((SMALL_ROWS - _OFF_MID - MID_PACK_ROWS, HEAD), F32)], axis=0
    ).reshape(N_DEV, SMALL_SEC, HEAD)
    win_out = _bwd_win(ht, dz, sums_mid, small)
    g_in, sib_in, chips_mid, sib_small = win_out[0], win_out[1], list(win_out[2:-1]), win_out[-1]
    sums_in = _chip_sums(core, [g_in, small], [sib_in, sib_small], [_WIRE, F32], "chip_sums_in")
    dx_out = _bwd_dx(dz.reshape(t, -1), x2d, dx1, win, norm_g, sums_in, tm)
    grad_x, pack_inp, chips_in, chips_small = dx_out

    grads = [g_in] + grads_mid + [small]
    recv_sib = [sib_in] + sib_mid + [sib_small]
    recv_chips = [chips_in] + chips_mid + [chips_small]
    wmv = [(w_in[0], m_w_in[0], v_w_in[0]), (w_proj_lru[0], m_w_proj_lru[0], v_w_proj_lru[0]),
           (w_proj_pool[0], m_w_proj_pool[0], v_w_proj_pool[0]), (w_out[0], m_w_out[0], v_w_out[0]),
           (w_ple_gate[0], m_w_ple_gate[0], v_w_ple_gate[0]), (w_ple_proj[0], m_w_ple_proj[0], v_w_ple_proj[0])]
    big, small_sec = _adam_sections(jnp.stack([sec_idx, chip_idx]), grads, recv_sib, recv_chips, wmv)
    gsmall, g0_parts = _all_gather([small_sec, pack_inp], [(SMALL_ROWS, HEAD), (N_DEV * INP_PACK_ROWS, HEAD)],
                                   [_row_slicer(SMALL_SEC), _row_slicer(INP_PACK_ROWS)], "gather_small")

    R = _SEQ_ROWS
    vec = lambda r0, *t: (r0, tuple(a.reshape(1, -1) for a in t))
    mat = lambda r0, *t: (r0, tuple(a.reshape(-1, HEAD) for a in t))
    norm_wmv = vec(0, norm_g, m_norm_g, v_norm_g)[1]
    vec_params = [vec(R["dcb"][0], conv_b, m_conv_b, v_conv_b),
                  vec(R["dlam"][0], lru_lambda, m_lru_lambda, v_lru_lambda),
                  vec(R["dps"][0], pool_scale, m_pool_scale, v_pool_scale),
                  vec(_OFF_MID, ple_norm_g, m_ple_norm_g, v_ple_norm_g),
                  vec(_OFF_MID + SUB, final_g, m_final_g, v_final_g)]
    mat_params = [mat(R["dwa"][0], lru_w_a, m_lru_w_a, v_lru_w_a), mat(R["dba"][0], lru_b_a, m_lru_b_a, v_lru_b_a),
                  mat(R["dwx"][0], lru_w_x, m_lru_w_x, v_lru_w_x), mat(R["dbx"][0], lru_b_x, m_lru_b_x, v_lru_b_x),
                  mat(R["dpw"][0], pool_w, m_pool_w, v_pool_w)]
    conv_wmv = (conv_w[0], m_conv_w[0], v_conv_w[0])
    small_out, loss_row = _adam_small(sec_idx.reshape(1), gsmall, g0_parts, norm_wmv, vec_params, mat_params, conv_wmv)
    loss = loss_row[0, 0]

    res = {}
    names_small = ["norm_g", "conv_b", "lru_lambda", "pool_scale", "ple_norm_g", "final_g",
                   "lru_w_a", "lru_b_a", "lru_w_x", "lru_b_x", "pool_w", "conv_w"]
    shapes = {"norm_g": norm_g, "conv_b": conv_b, "lru_lambda": lru_lambda, "pool_scale": pool_scale,
              "ple_norm_g": ple_norm_g, "final_g": final_g, "lru_w_a": lru_w_a, "lru_b_a": lru_b_a, "lru_w_x": lru_w_x,
              "lru_b_x": lru_b_x, "pool_w": pool_w, "conv_w": conv_w}
    for name, quad in zip(names_small, small_out):
        res[name] = [a.reshape(shapes[name].shape) for a in quad]
    names_big = ["w_in", "w_proj_lru", "w_proj_pool", "w_out", "w_ple_gate", "w_ple_proj"]
    for name, quad, (w, _, _) in zip(names_big, big, wmv):
        res[name] = [a.reshape((1,) + w.shape) for a in quad]
    order = ["norm_g", "w_in", "conv_w", "conv_b", "lru_w_a", "lru_b_a", "lru_w_x", "lru_b_x", "lru_lambda", "pool_w",
             "pool_scale", "w_proj_lru", "w_proj_pool", "w_out", "ple_norm_g", "w_ple_gate", "w_ple_proj", "final_g"]
    out = [loss, grad_x.reshape(nb, s, d)]
    for kind in range(4):
        out += [res[name][kind] for name in order]
    return tuple(out)
```

```python
import jax
import jax.numpy as jnp
from jax import lax
from jax.experimental import pallas as pl
from jax.experimental.pallas import tpu as pltpu

F32 = jnp.float32
_MM = jnp.bfloat16
_WIRE = jnp.bfloat16

EPS = 1e-6
LRU_C = 8.0
MULT_SQ_FLOOR = 1e-30
POOL_WINDOWS = (2, 4, 8, 16)
N_HEADS = 8
HEAD = 128
HALO = 16
SUB = 8

ADAM_LR = 0.001
ADAM_B1 = 0.9
ADAM_B2 = 0.999
ADAM_EPS = 1e-08
ADAM_WD = 0.01
ADAM_STEP = 10

N_DEV = 8
MESH = pl.DeviceIdType.MESH
VMEM_LIMIT = 60 * 1024 * 1024
TILE_M = 256
N_SEQ_SAVED = 10

_SEQ_ROWS = {"dwa": (0, 1024), "dwx": (1024, 1024), "dpw": (2048, 512), "dba": (2560, 8), "dbx": (2568, 8),
             "dcb": (2576, 8), "dlam": (2584, 8), "dcw": (2592, 32), "dps": (2624, 8)}
SEQ_PACK_ROWS = 2632
MID_PACK_ROWS = 24
MID_LOSS_ROW = 16
INP_PACK_ROWS = 8
SMALL_ROWS = 3072
SMALL_SEC = SMALL_ROWS // N_DEV
_OFF_MID = SEQ_PACK_ROWS


def _dot(a, b):
    return jnp.dot(a, b, preferred_element_type=F32)


def _dot_nt(a, b):
    return lax.dot_general(a, b, (((1,), (1,)), ((), ())), preferred_element_type=F32)


def _dot_tn(a, b):
    return lax.dot_general(a, b, (((0,), (0,)), ((), ())), preferred_element_type=F32)


def _sigmoid(v):
    return 0.5 * jnp.tanh(0.5 * v) + 0.5


def _lru_mult(log_a, a):
    m2 = jnp.maximum(-jnp.tanh(log_a) * (1.0 + a * a), MULT_SQ_FLOOR)
    inv = lax.rsqrt(m2)
    return m2 * inv, inv


def _rows_back(v, n):
    return pltpu.roll(v, n, 0) if n else v


def _rows_ahead(v, n):
    return pltpu.roll(v, v.shape[0] - n, 0) if n else v


def _softplus_neg(lam):
    e = jnp.exp(-jnp.abs(lam))
    w = 1.0 + e
    l1p = jnp.where(w == 1.0, e, jnp.log(w) * (e / (w - 1.0)))
    return jnp.maximum(-lam, 0.0) + l1p


def _rsqrt_mean_sq(v):
    return lax.rsqrt(jnp.mean(v * v, axis=-1, keepdims=True) + EPS)


def _colsum(v):
    return jnp.sum(v, axis=0, keepdims=True)


def _const_spec(shape):
    nd = len(shape)
    return pl.BlockSpec(shape, lambda *_: (0,) * nd, pipeline_mode=pl.Buffered(1))


HBM_SPEC = pl.BlockSpec(memory_space=pl.ANY)


def _my_pos():
    return lax.axis_index("x"), lax.axis_index("y"), lax.axis_index("c")


def _other_chips(x, y):
    return [(1 - x, y), (x, 1 - y), (1 - x, 1 - y)]


def _gather_sems(nw):
    return [pltpu.SemaphoreType.DMA((nw, 7)), pltpu.SemaphoreType.DMA((nw, 7)), pltpu.SemaphoreType.DMA((nw,))]


def _gather_phases(srcs, outs, slicers, send_sems, recv_sems, local_sems):
    nw = len(srcs)
    x, y, c = _my_pos()
    me, sibling = (x, y, c), (x, y, 1 - c)
    chips = _other_chips(x, y)

    def part(w, pos):
        return slicers[w](outs[w], 4 * pos[0] + 2 * pos[1] + pos[2])

    def copy(w, k, block, to, src=None):
        return pltpu.make_async_remote_copy(
            src_ref=part(w, block) if src is None else src, dst_ref=part(w, block),
            send_sem=send_sems.at[w, k], recv_sem=recv_sems.at[w, k], device_id=to, device_id_type=MESH)

    def mine():
        return [pltpu.make_async_copy(srcs[w], part(w, me), local_sems.at[w]) for w in range(nw)]

    def first():
        out = []
        for w in range(nw):
            out.append(copy(w, 0, me, sibling, src=srcs[w]))
            out += [copy(w, 1 + j, me, (*chip, c), src=srcs[w]) for j, chip in enumerate(chips)]
        return out

    def passed():
        return [copy(w, 4 + j, (*chip, c), sibling) for j, chip in enumerate(chips) for w in range(nw)]

    def start():
        for cp in mine() + first():
            cp.start()

    def forward():
        for j, chip in enumerate(chips):
            for w in range(nw):
                copy(w, 1 + j, (*chip, c), me).wait_recv()
                copy(w, 4 + j, (*chip, c), sibling).start()

    def finish():
        for w in range(nw):
            copy(w, 0, sibling, me).wait_recv()
            for j, chip in enumerate(chips):
                copy(w, 4 + j, (*chip, 1 - c), me).wait_recv()
        for cp in first() + passed():
            cp.wait_send()
        for cp in mine():
            cp.wait()

    return start, forward, finish


def _sibling_copies(srcs, outs, send_sems, recv_sems):
    x, y, c = _my_pos()
    return [pltpu.make_async_remote_copy(
        src_ref=srcs[w].at[2 * q + (1 - c)], dst_ref=outs[w].at[q], send_sem=send_sems.at[w, q],
        recv_sem=recv_sems.at[w, q], device_id=(x, y, 1 - c), device_id_type=MESH)
        for w in range(len(srcs)) for q in range(4)]


def _chip_copies(srcs, outs, send_sems, recv_sems):
    x, y, c = _my_pos()
    return [pltpu.make_async_remote_copy(
        src_ref=srcs[w].at[2 * px + py], dst_ref=outs[w].at[k], send_sem=send_sems.at[w, k],
        recv_sem=recv_sems.at[w, k], device_id=(px, py, c), device_id_type=MESH)
        for w in range(len(srcs)) for k, (px, py) in enumerate(_other_chips(x, y))]


def _lru_gates(xc, h, wa_ref, ba_ref, wx_ref, bx_ref, neg_c_sp):
    xc16 = xc.astype(_MM)
    r = _sigmoid(_dot(xc16, wa_ref[h]) + ba_ref[pl.ds(h, 1), :])
    i = _sigmoid(_dot(xc16, wx_ref[h]) + bx_ref[pl.ds(h, 1), :])
    return r, i, neg_c_sp * r


def _conv_head(xa_ext, cw_ref, cb_ref, cols, tm):
    ext = xa_ext[:, cols]
    xc = cb_ref[:, cols] + cw_ref[pl.ds(3, 1), cols] * ext[HALO:HALO + tm]
    for k in range(3):
        xc = xc + cw_ref[pl.ds(k, 1), cols] * _rows_back(ext, 3 - k)[HALO:HALO + tm]
    return xc


def _window_sum(ext, k, shift):
    n = 1
    while n < k:
        ext = ext + shift(ext, n)
        n *= 2
    return ext


def _pool_diff(xb_ext, g, k, pos, tm):
    cols = slice(g * HEAD, (g + 1) * HEAD)
    ext = xb_ext[:, cols]
    ws = _window_sum(ext, k, _rows_back)[HALO:HALO + tm]
    inv_cnt = 1.0 / jnp.minimum(pos + 1, k).astype(F32)
    return ws * inv_cnt - ext[HALO:HALO + tm], inv_cnt


def _staged_sources(shards, as_operand):
    in_specs = [_const_spec(a.shape) if c else HBM_SPEC for a, c in zip(shards, as_operand)]
    stages = [pltpu.VMEM(a.shape, _MM) for a, c in zip(shards, as_operand) if c]
    dtypes = [_MM if c else a.dtype for a, c in zip(shards, as_operand)]
    return in_specs, stages, dtypes


def _gather_sources(srcs, stages, as_operand):
    stages = iter(stages)
    return [next(stages) if c else src for src, c in zip(srcs, as_operand)]


def _fill_stages(srcs, staged, as_operand):
    for src, dst, c in zip(srcs, staged, as_operand):
        if c:
            dst[...] = src[...].astype(dst.dtype)


def _pre(x, g0, shards, as_operand, shapes, slicers, tm):
    nb, s, d = x.shape
    nt = s // tm
    nl = len(shards)
    shard_specs, stage_shapes, wire = _staged_sources(shards, as_operand)

    def body(x_ref, g0_ref, *rest):
        src, rest = rest[:nl], rest[nl:]
        h_ref, ht_ref = rest[:2]
        out, rest = rest[2:2 + nl], rest[2 + nl:]
        (send_sems, recv_sems, local_sems), stages = rest[:3], rest[3:]
        step_no = pl.program_id(0) * nt + pl.program_id(1)
        staged = _gather_sources(src, stages, as_operand)
        g_start, g_forward, g_finish = _gather_phases(staged, out, slicers, send_sems, recv_sems, local_sems)

        @pl.when(step_no == 0)
        def _():
            _fill_stages(src, staged, as_operand)
            g_start()

        xv = x_ref[...]
        h16 = ((xv * _rsqrt_mean_sq(xv)) * g0_ref[...]).astype(_MM)
        h_ref[...] = h16
        ht_ref[...] = h16.T

        @pl.when(step_no == nb * nt - 1)
        def _():
            g_forward()
            g_finish()

    tile = pl.BlockSpec((None, tm, d), lambda b, j: (b, j, 0))
    return pl.pallas_call(
        body, name="pre", grid=(nb, nt),
        in_specs=[tile, _const_spec(g0.shape)] + shard_specs,
        out_specs=(tile, pl.BlockSpec((None, d, tm), lambda b, j: (b, 0, j))) + (HBM_SPEC,) * nl,
        out_shape=(jax.ShapeDtypeStruct((nb, s, d), _MM), jax.ShapeDtypeStruct((nb, d, s), _MM))
        + tuple(jax.ShapeDtypeStruct(shp, dt) for shp, dt in zip(shapes, wire)),
        scratch_shapes=_gather_sems(nl) + stage_shapes,
        compiler_params=pltpu.CompilerParams(dimension_semantics=("arbitrary", "arbitrary")),
    )(x, g0, *shards)


def _fwd_seq(h, win, cw, cb, wa, ba, wx, bx, lam, pw, ps, late_shards, late_shapes, late_slicers, tm):
    nb, s, d = h.shape
    lw = cw.shape[1]
    pwid = ps.shape[1]
    nc = 2 * lw + 2 * pwid
    nt = s // tm
    nl = len(late_shards)
    n_steps = nb * nt
    as_operand = [True] * nl
    shard_specs, stage_shapes, wire = _staged_sources(late_shards, as_operand)

    def body(h_ref, win_ref, cw_ref, cb_ref, wa_ref, ba_ref, wx_ref, bx_ref, lam_ref, pw_ref, ps_ref, *rest):
        late_src, rest = rest[:nl], rest[nl:]
        z_ref, hl_ref, ya_ref, yb_ref, xc_ref, r_ref, i_ref, a_buf, diff_ref, yp_ref = rest[:N_SEQ_SAVED]
        late_out, rest = rest[N_SEQ_SAVED:N_SEQ_SAVED + nl], rest[N_SEQ_SAVED + nl:]
        (xa_ext, xb_ext, carry, send_sems, recv_sems, local_sems), stages = rest[:6], rest[6:]
        j = pl.program_id(1)
        step_no = pl.program_id(0) * nt + j
        staged = _gather_sources(late_src, stages, as_operand)
        g_start, g_forward, g_finish = _gather_phases(staged, late_out, late_slicers, send_sems, recv_sems, local_sems)

        @pl.when(step_no == 0)
        def _():
            _fill_stages(late_src, staged, as_operand)
            g_start()

        pl.when(step_no == n_steps // 2)(g_forward)

        @pl.when(j == 0)
        def _():
            xa_ext[pl.ds(0, HALO), :] = jnp.zeros((HALO, lw), F32)
            xb_ext[pl.ds(0, HALO), :] = jnp.zeros((HALO, pwid), F32)
            carry[...] = jnp.zeros_like(carry)

        h16 = h_ref[...]
        xa = _dot(h16, win_ref[:, 0:lw])
        z_ref[:, 0:lw] = xa
        xa_ext[pl.ds(HALO, tm), :] = xa
        for c0 in range(lw, nc, lw):
            z_ref[:, c0:c0 + lw] = _dot(h16, win_ref[:, c0:c0 + lw])

        for h in range(N_HEADS):
            cols = slice(h * HEAD, (h + 1) * HEAD)
            xc = _conv_head(xa_ext, cw_ref, cb_ref, cols, tm)
            neg_c_sp = -LRU_C * _softplus_neg(lam_ref[:, cols])
            r, i, log_a = _lru_gates(xc, h, wa_ref, ba_ref, wx_ref, bx_ref, neg_c_sp)
            a = jnp.exp(log_a)
            xc_ref[:, cols] = xc
            r_ref[:, cols] = r
            i_ref[:, cols] = i
            a_buf[:, cols] = a
            hl_ref[:, cols] = _lru_mult(log_a, a)[0] * (i * xc)

        rows = lax.broadcasted_iota(jnp.int32, (SUB, lw), 0)

        def step(c, car):
            i0 = pl.multiple_of(c * SUB, SUB)
            av = a_buf[pl.ds(i0, SUB), :]
            bv = hl_ref[pl.ds(i0, SUB), :]
            for sh in (1, 2, 4):
                m = rows >= sh
                a_sh = jnp.where(m, pltpu.roll(av, sh, 0), 1.0)
                b_sh = jnp.where(m, pltpu.roll(bv, sh, 0), 0.0)
                bv = av * b_sh + bv
                av = av * a_sh
            hv = av * car + bv
            hl_ref[pl.ds(i0, SUB), :] = hv
            return jnp.broadcast_to(hv[SUB - 1:SUB, :], (SUB, lw))

        carry[...] = lax.fori_loop(0, tm // SUB, step, carry[...])

        ga = z_ref[:, lw:2 * lw]
        ya_ref[...] = (hl_ref[...] * (ga * _sigmoid(ga))).astype(ya_ref.dtype)

        xb_ext[pl.ds(HALO, tm), :] = z_ref[:, 2 * lw:2 * lw + pwid]
        pos = j * tm + lax.broadcasted_iota(jnp.int32, (tm, HEAD), 0)
        for g, k in enumerate(POOL_WINDOWS):
            cols = slice(g * HEAD, (g + 1) * HEAD)
            diff16 = _pool_diff(xb_ext, g, k, pos, tm)[0].astype(_MM)
            yp = _dot(diff16, pw_ref[g])
            diff_ref[:, cols] = diff16
            yp_ref[:, cols] = yp
            gb = z_ref[:, 2 * lw + pwid + g * HEAD:2 * lw + pwid + (g + 1) * HEAD]
            yb_ref[:, cols] = ((yp * ps_ref[:, cols]) * (gb * _sigmoid(gb))).astype(yb_ref.dtype)

        xa_ext[pl.ds(0, HALO), :] = xa_ext[pl.ds(tm, HALO), :]
        xb_ext[pl.ds(0, HALO), :] = xb_ext[pl.ds(tm, HALO), :]
        pl.when(step_no == n_steps - 1)(g_finish)

    tile = lambda w: pl.BlockSpec((None, tm, w), lambda b, j: (b, j, 0))
    return pl.pallas_call(
        body, name="fwd_seq", grid=(nb, nt),
        in_specs=[tile(d), _const_spec((d, nc)), _const_spec(cw.shape), _const_spec(cb.shape),
                  _const_spec(wa.shape), _const_spec(ba.shape), _const_spec(wx.shape), _const_spec(bx.shape),
                  _const_spec(lam.shape), _const_spec(pw.shape), _const_spec(ps.shape)] + shard_specs,
        out_specs=(tile(nc), tile(lw), tile(lw), tile(pwid), tile(lw), tile(lw), tile(lw), tile(lw), tile(pwid),
                   tile(pwid)) + (HBM_SPEC,) * nl,
        out_shape=(jax.ShapeDtypeStruct((nb, s, nc), F32), jax.ShapeDtypeStruct((nb, s, lw), F32),
                   jax.ShapeDtypeStruct((nb, s, lw), _MM), jax.ShapeDtypeStruct((nb, s, pwid), _MM))
        + (jax.ShapeDtypeStruct((nb, s, lw), F32),) * 4
        + (jax.ShapeDtypeStruct((nb, s, pwid), _MM), jax.ShapeDtypeStruct((nb, s, pwid), F32))
        + tuple(jax.ShapeDtypeStruct(shp, dt) for shp, dt in zip(late_shapes, wire)),
        scratch_shapes=[pltpu.VMEM((tm + HALO, lw), F32), pltpu.VMEM((tm + HALO, pwid), F32),
                        pltpu.VMEM((SUB, lw), F32)] + _gather_sems(nl) + stage_shapes,
        compiler_params=pltpu.CompilerParams(dimension_semantics=("arbitrary", "arbitrary"), vmem_limit_bytes=VMEM_LIMIT),
    )(h, win, cw, cb, wa, ba, wx, bx, lam, pw, ps, *late_shards)


def _store_chunks(ref, row0, vec):
    for h in range(vec.shape[1] // HEAD):
        ref[pl.ds(row0 + h, 1), :] += vec[:, h * HEAD:(h + 1) * HEAD]


def _mid(ya, yb, h, win, x, p, tgt, wpl, wpp, wout, wpg, wpe, g1, gf, tm):
    t, d = x.shape
    lw = ya.shape[1]
    pwid = yb.shape[1]
    pdim = p.shape[1]
    n = t // tm
    sec_pp = wpp.shape[1] // N_DEV
    sec_pe = wpe.shape[1] // N_DEV
    n_win_blocks = win.shape[1] // d
    win_cols = lambda c: pl.BlockSpec((d, d), lambda i: (0, c), pipeline_mode=pl.Buffered(1))

    def body(ya_ref, yb_ref, h_ref, wma_ref, wmb_ref, x_ref, p_ref, tgt_ref, wpl_ref, wpp_ref, wout_ref, wpg_ref, wpe_ref,
             g1_ref, gf_ref,
             dya_ref, dyb_ref, dm_ref, dx1_ref, gpl_ref, gpp_ref, gout_ref, gpg_ref, gpe_ref, pack_ref,
             acc_pl, acc_pp, acc_out, acc_pg, acc_pe):
        i = pl.program_id(0)

        @pl.when(i == 0)
        def _():
            for acc in (acc_pl, acc_pp, acc_out, acc_pg, acc_pe):
                acc[...] = jnp.zeros_like(acc)
            pack_ref[...] = jnp.zeros_like(pack_ref)

        ya16 = ya_ref[...]
        yb16 = yb_ref[...]
        br_a = _dot(ya16, wpl_ref[...])
        br_b = _dot(yb16, wpp_ref[...])
        h16 = h_ref[...]
        sa = _sigmoid(_dot(h16, wma_ref[...]))
        sb = _sigmoid(_dot(h16, wmb_ref[...]))
        mg16 = (sa * br_a + sb * br_b).astype(_MM)
        x1 = x_ref[...] + _dot(mg16, wout_ref[...])
        r1 = _rsqrt_mean_sq(x1)
        n1 = x1 * r1
        h116 = (n1 * g1_ref[...]).astype(_MM)
        gate = _sigmoid(_dot(h116, wpg_ref[...]))
        p16 = p_ref[...].astype(_MM)
        pe = _dot(p16, wpe_ref[...])
        x2 = x1 + gate * pe
        r2 = _rsqrt_mean_sq(x2)
        n2 = x2 * r2
        err = n2 * gf_ref[...] - tgt_ref[...]
        sq = jnp.sum(_colsum(err * err), axis=1, keepdims=True)
        pack_ref[pl.ds(MID_LOSS_ROW, 1), :] += jnp.broadcast_to(sq * (0.5 / d), (1, HEAD))

        dy = err * (1.0 / d)
        _store_chunks(pack_ref, SUB, _colsum(dy * n2))
        dn2 = dy * gf_ref[...]
        dx2 = r2 * (dn2 - n2 * jnp.mean(dn2 * n2, axis=-1, keepdims=True))
        dpe16 = (dx2 * gate).astype(_MM)
        dpg16 = ((dx2 * pe) * (gate * (1.0 - gate))).astype(_MM)
        acc_pe[...] += _dot_tn(p16, dpe16)
        acc_pg[...] += _dot_tn(h116, dpg16)
        dh1 = _dot_nt(dpg16, wpg_ref[...])
        _store_chunks(pack_ref, 0, _colsum(dh1 * n1))
        dn1 = dh1 * g1_ref[...]
        dx1 = dx2 + r1 * (dn1 - n1 * jnp.mean(dn1 * n1, axis=-1, keepdims=True))
        dx1_ref[...] = dx1
        dx116 = dx1.astype(_MM)
        acc_out[...] += _dot_tn(mg16, dx116)
        dmg = _dot_nt(dx116, wout_ref[...])
        da16 = (dmg * sa).astype(_MM)
        db16 = (dmg * sb).astype(_MM)
        dm_ref[:, 0:d] = ((dmg * br_a) * (sa * (1.0 - sa))).astype(dm_ref.dtype)
        dm_ref[:, d:2 * d] = ((dmg * br_b) * (sb * (1.0 - sb))).astype(dm_ref.dtype)
        acc_pl[...] += _dot_tn(ya16, da16)
        acc_pp[...] += _dot_tn(yb16, db16)
        dya_ref[...] = _dot_nt(da16, wpl_ref[...])
        dyb_ref[...] = _dot_nt(db16, wpp_ref[...])

        @pl.when(i == n - 1)
        def _():
            pltpu.sync_copy(acc_pl, gpl_ref)
            pltpu.sync_copy(acc_out, gout_ref)
            pltpu.sync_copy(acc_pg, gpg_ref)
            for k in range(N_DEV):
                pltpu.sync_copy(acc_pp.at[:, pl.ds(k * sec_pp, sec_pp)], gpp_ref.at[k])
                pltpu.sync_copy(acc_pe.at[:, pl.ds(k * sec_pe, sec_pe)], gpe_ref.at[k])

    tile = lambda w: pl.BlockSpec((tm, w), lambda i: (i, 0))
    any_spec = pl.BlockSpec(memory_space=pl.ANY)
    full = lambda shape: pl.BlockSpec(shape, lambda i: (0,) * len(shape))
    return pl.pallas_call(
        body, name="mid", grid=(n,),
        in_specs=[tile(lw), tile(pwid), tile(d), win_cols(n_win_blocks - 2), win_cols(n_win_blocks - 1),
                  tile(d), tile(pdim), tile(d),
                  _const_spec(wpl.shape), _const_spec(wpp.shape), _const_spec(wout.shape), _const_spec(wpg.shape),
                  _const_spec(wpe.shape), _const_spec(g1.shape), _const_spec(gf.shape)],
        out_specs=(tile(lw), tile(pwid), tile(2 * d), tile(d), any_spec, any_spec, any_spec, any_spec, any_spec,
                   full((MID_PACK_ROWS, HEAD))),
        out_shape=(jax.ShapeDtypeStruct((t, lw), F32), jax.ShapeDtypeStruct((t, pwid), F32),
                   jax.ShapeDtypeStruct((t, 2 * d), _MM), jax.ShapeDtypeStruct((t, d), F32),
                   jax.ShapeDtypeStruct(wpl.shape, F32), jax.ShapeDtypeStruct((N_DEV, wpp.shape[0], sec_pp), F32),
                   jax.ShapeDtypeStruct(wout.shape, F32), jax.ShapeDtypeStruct(wpg.shape, F32),
                   jax.ShapeDtypeStruct((N_DEV, wpe.shape[0], sec_pe), F32),
                   jax.ShapeDtypeStruct((MID_PACK_ROWS, HEAD), F32)),
        scratch_shapes=[pltpu.VMEM(wpl.shape, F32), pltpu.VMEM(wpp.shape, F32), pltpu.VMEM(wout.shape, F32),
                        pltpu.VMEM(wpg.shape, F32), pltpu.VMEM(wpe.shape, F32)],
        compiler_params=pltpu.CompilerParams(dimension_semantics=("arbitrary",), vmem_limit_bytes=VMEM_LIMIT),
    )(ya, yb, h, win, win, x, p, tgt, wpl, wpp, wout, wpg, wpe, g1, gf)


def _bwd_seq(z, hl, saved, dya, dyb, dm, cw, wa, wx, lam, pw, ps, pack_mid, sib_grads, tm):
    nb, s, zw = z.shape
    lw = cw.shape[1]
    pwid = ps.shape[1]
    nt = s // tm
    nc = zw + dm.shape[2]
    R = _SEQ_ROWS
    ns = len(sib_grads)

    def body(zq_ref, hl_ref, hlh_ref, xc_buf, r_buf, i_buf, a_ref, diff_ref, yp_ref, dya_ref, dyb_ref, dm_ref,
             cw_ref, wa_ref, wx_ref, lam_ref, pw_ref, ps_ref, pack_mid_ref, *rest):
        sib_src, rest = rest[:ns], rest[ns:]
        dz_ref, pack_ref = rest[:2]
        sib_out, rest = rest[2:2 + ns], rest[2 + ns:]
        a_ext, an_buf, hl_ext, dh_buf, dxc_ext, q_ext, carry, send_sems, recv_sems = rest
        b = pl.program_id(0)
        j = pl.program_id(1)
        jr = nt - 1 - j
        has_prev = jr > 0

        @pl.when((b == 0) & (j == 0))
        def _():
            for cp in _sibling_copies(sib_src, sib_out, send_sems, recv_sems):
                cp.start()

        dz_ref[:, zw:nc] = dm_ref[...]

        @pl.when((b == 0) & (j == 0))
        def _():
            pack_ref[...] = jnp.zeros_like(pack_ref)

        @pl.when(j == 0)
        def _():
            a_ext[pl.ds(tm, SUB), :] = jnp.zeros((SUB, lw), F32)
            dxc_ext[pl.ds(tm, HALO), :] = jnp.zeros((HALO, lw), F32)
            q_ext[pl.ds(tm, HALO), :] = jnp.zeros((HALO, pwid), F32)
            carry[...] = jnp.zeros_like(carry)

        hl_ext[pl.ds(0, SUB), :] = jnp.where(has_prev, hlh_ref[...], 0.0)
        hl_ext[pl.ds(SUB, tm), :] = hl_ref[...]
        a_ext[pl.ds(0, tm), :] = a_ref[...]
        for h in range(N_HEADS):
            cols = slice(h * HEAD, (h + 1) * HEAD)
            ga = zq_ref[:, lw + h * HEAD:lw + (h + 1) * HEAD]
            sg = _sigmoid(ga)
            dyav = dya_ref[:, cols]
            dh_buf[:, cols] = dyav * (ga * sg)
            dga = (dyav * hl_ext[pl.ds(SUB, tm), cols]) * (sg * (1.0 + ga * (1.0 - sg)))
            dz_ref[:, lw + h * HEAD:lw + (h + 1) * HEAD] = dga.astype(dz_ref.dtype)
        an_buf[...] = _rows_ahead(a_ext[...], 1)[0:tm]

        rows = lax.broadcasted_iota(jnp.int32, (SUB, lw), 0)
        nch = tm // SUB

        def step(c, car):
            i0 = pl.multiple_of((nch - 1 - c) * SUB, SUB)
            cv = an_buf[pl.ds(i0, SUB), :]
            bv = dh_buf[pl.ds(i0, SUB), :]
            for sh in (1, 2, 4):
                m = rows < SUB - sh
                c_sh = jnp.where(m, pltpu.roll(cv, SUB - sh, 0), 1.0)
                b_sh = jnp.where(m, pltpu.roll(bv, SUB - sh, 0), 0.0)
                bv = cv * b_sh + bv
                cv = cv * c_sh
            hv = cv * car + bv
            dh_buf[pl.ds(i0, SUB), :] = hv
            return jnp.broadcast_to(hv[0:1, :], (SUB, lw))

        carry[...] = lax.fori_loop(0, nch, step, carry[...])

        for h in range(N_HEADS):
            cols = slice(h * HEAD, (h + 1) * HEAD)
            lam_h = lam_ref[:, cols]
            neg_c_sp = -LRU_C * _softplus_neg(lam_h)
            xc = xc_buf[:, cols]
            r = r_buf[:, cols]
            i = i_buf[:, cols]
            a = a_ref[:, cols]
            mult, inv_mult = _lru_mult(neg_c_sp * r, a)
            dh = dh_buf[:, cols]
            dhx = dh * xc
            di = dhx * mult
            h_prev = _rows_back(hl_ext[:, cols], 1)[SUB:SUB + tm]
            dlog_a = (dh * h_prev) * a - (dhx * i) * ((a * a) * inv_mult)
            pack_ref[pl.ds(R["dlam"][0] + h, 1), :] += _colsum(dlog_a * r) * (LRU_C * _sigmoid(-lam_h))
            dpa = (dlog_a * neg_c_sp) * (r * (1.0 - r))
            dpx = di * (i * (1.0 - i))
            dpa16 = dpa.astype(_MM)
            dpx16 = dpx.astype(_MM)
            xc16 = xc.astype(_MM)
            dxc = dh * (mult * i) + _dot_nt(dpa16, wa_ref[h]) + _dot_nt(dpx16, wx_ref[h])
            pack_ref[pl.ds(R["dwa"][0] + h * HEAD, HEAD), :] += _dot_tn(xc16, dpa16)
            pack_ref[pl.ds(R["dwx"][0] + h * HEAD, HEAD), :] += _dot_tn(xc16, dpx16)
            pack_ref[pl.ds(R["dba"][0] + h, 1), :] += _colsum(dpa)
            pack_ref[pl.ds(R["dbx"][0] + h, 1), :] += _colsum(dpx)
            pack_ref[pl.ds(R["dcb"][0] + h, 1), :] += _colsum(dxc)
            dxc_ext[pl.ds(0, tm), cols] = dxc

        for h in range(N_HEADS):
            cols = slice(h * HEAD, (h + 1) * HEAD)
            dxc_all = dxc_ext[:, cols]
            xa = zq_ref[:, cols]
            dxa = None
            for k in range(4):
                dxc_k = _rows_ahead(dxc_all, 3 - k)[0:tm]
                pack_ref[pl.ds(R["dcw"][0] + SUB * k + h, 1), :] += _colsum(dxc_k * xa)
                term = cw_ref[pl.ds(k, 1), cols] * dxc_k
                dxa = term if dxa is None else dxa + term
            dz_ref[:, cols] = dxa.astype(dz_ref.dtype)

        pos = jr * tm + lax.broadcasted_iota(jnp.int32, (tm, HEAD), 0)
        for g, k in enumerate(POOL_WINDOWS):
            cols = slice(g * HEAD, (g + 1) * HEAD)
            inv_cnt = 1.0 / jnp.minimum(pos + 1, k).astype(F32)
            diff16 = diff_ref[:, cols]
            yp = yp_ref[:, cols]
            sc = ps_ref[:, cols]
            gb = zq_ref[:, 2 * lw + pwid + g * HEAD:2 * lw + pwid + (g + 1) * HEAD]
            sgb = _sigmoid(gb)
            dybv = dyb_ref[:, cols]
            dy_pool = dybv * (gb * sgb)
            dgb = (dybv * (yp * sc)) * (sgb * (1.0 + gb * (1.0 - sgb)))
            pack_ref[pl.ds(R["dps"][0] + g, 1), :] += _colsum(dy_pool * yp)
            dyp16 = (dy_pool * sc).astype(_MM)
            pack_ref[pl.ds(R["dpw"][0] + g * HEAD, HEAD), :] += _dot_tn(diff16, dyp16)
            ddiff = _dot_nt(dyp16, pw_ref[g])
            q_ext[pl.ds(0, tm), cols] = ddiff * inv_cnt
            dxb = _window_sum(q_ext[:, cols], k, _rows_ahead)[0:tm] - ddiff
            dz_ref[:, 2 * lw + g * HEAD:2 * lw + (g + 1) * HEAD] = dxb.astype(dz_ref.dtype)
            dz_ref[:, 2 * lw + pwid + g * HEAD:2 * lw + pwid + (g + 1) * HEAD] = dgb.astype(dz_ref.dtype)

        a_ext[pl.ds(tm, SUB), :] = a_ext[pl.ds(0, SUB), :]
        dxc_ext[pl.ds(tm, HALO), :] = dxc_ext[pl.ds(0, HALO), :]
        q_ext[pl.ds(tm, HALO), :] = q_ext[pl.ds(0, HALO), :]

        @pl.when((b == nb - 1) & (j == nt - 1))
        def _():
            pack_ref[pl.ds(_OFF_MID, MID_PACK_ROWS), :] = pack_mid_ref[...]
            for cp in _sibling_copies(sib_src, sib_out, send_sems, recv_sems):
                cp.wait()

    rev = lambda w: pl.BlockSpec((None, tm, w), lambda b, j: (b, nt - 1 - j, 0))
    prev_rows = lambda rows, w: pl.BlockSpec(
        (None, rows, w), lambda b, j: (b, jnp.maximum((nt - 1 - j) * (tm // rows) - 1, 0), 0))
    return pl.pallas_call(
        body, name="bwd_seq", grid=(nb, nt),
        in_specs=[rev(zw), rev(lw), prev_rows(SUB, lw), rev(lw), rev(lw), rev(lw), rev(lw), rev(pwid), rev(pwid),
                  rev(lw), rev(pwid), rev(nc - zw),
                  _const_spec(cw.shape), _const_spec(wa.shape), _const_spec(wx.shape), _const_spec(lam.shape),
                  _const_spec(pw.shape), _const_spec(ps.shape), _const_spec(pack_mid.shape)] + [HBM_SPEC] * ns,
        out_specs=(rev(nc), pl.BlockSpec((SMALL_ROWS, HEAD), lambda b, j: (0, 0))) + (HBM_SPEC,) * ns,
        out_shape=(jax.ShapeDtypeStruct((nb, s, nc), _MM), jax.ShapeDtypeStruct((SMALL_ROWS, HEAD), F32))
        + tuple(jax.ShapeDtypeStruct((4,) + g.shape[1:], g.dtype) for g in sib_grads),
        scratch_shapes=[pltpu.VMEM((tm + SUB, lw), F32), pltpu.VMEM((tm, lw), F32), pltpu.VMEM((tm + SUB, lw), F32),
                        pltpu.VMEM((tm, lw), F32), pltpu.VMEM((tm + HALO, lw), F32),
                        pltpu.VMEM((tm + HALO, pwid), F32), pltpu.VMEM((SUB, lw), F32),
                        pltpu.SemaphoreType.DMA((ns, 4)), pltpu.SemaphoreType.DMA((ns, 4))],
        compiler_params=pltpu.CompilerParams(dimension_semantics=("arbitrary", "arbitrary"), vmem_limit_bytes=VMEM_LIMIT),
    )(z, hl, hl, *saved, dya, dyb, dm, cw, wa, wx, lam, pw, ps, pack_mid, *sib_grads)


def _bwd_win(ht, dz, chip_srcs, small):
    nb, d, s = ht.shape
    nc = dz.shape[2]
    sec = nc // N_DEV
    n_chips = N_DEV // 2
    nm = len(chip_srcs)

    def body(ht_ref, dz_ref, *rest):
        chip_src, small_src = rest[:nm], rest[nm]
        gwin_ref, recv_ref = rest[nm + 1:nm + 3]
        chip_out, small_out = rest[nm + 3:2 * nm + 3], rest[2 * nm + 3]
        acc, local_sems, sib_send, sib_recv, chip_send, chip_recv, small_send, small_recv = rest[2 * nm + 4:]
        q = pl.program_id(0)
        b = pl.program_id(1)
        slot = q % 2
        x, y, c = _my_pos()

        def half(sl, core):
            return acc.at[sl, :, pl.ds(pl.multiple_of(core * sec, HEAD), sec)]

        def local_copy(qq, sl):
            return pltpu.make_async_copy(half(sl, c), gwin_ref.at[2 * qq + c], local_sems.at[sl])

        def sib_copy(qq, sl):
            return pltpu.make_async_remote_copy(
                src_ref=half(sl, 1 - c), dst_ref=recv_ref.at[qq], send_sem=sib_send.at[qq],
                recv_sem=sib_recv.at[qq], device_id=(x, y, 1 - c), device_id_type=MESH)

        def others():
            return (_chip_copies(chip_src, chip_out, chip_send, chip_recv)
                    + _sibling_copies([small_src], [small_out], small_send, small_recv))

        @pl.when((q == 0) & (b == 0))
        def _():
            for cp in others():
                cp.start()

        @pl.when((q >= 2) & (b == 0))
        def _():
            local_copy(q - 2, slot).wait()
            sib_copy(q - 2, slot).wait_send()

        part = _dot(ht_ref[b], dz_ref[...])

        @pl.when(b == 0)
        def _():
            acc[slot] = part

        @pl.when(b != 0)
        def _():
            acc[slot] += part

        @pl.when(b == nb - 1)
        def _():
            local_copy(q, slot).start()
            sib_copy(q, slot).start()

        @pl.when((q == n_chips - 1) & (b == nb - 1))
        def _():
            for qq in (n_chips - 2, n_chips - 1):
                local_copy(qq, qq % 2).wait()
                sib_copy(qq, qq % 2).wait_send()
            for qq in range(n_chips):
                sib_copy(qq, 0).wait_recv()
            for cp in others():
                cp.wait()

    return pl.pallas_call(
        body, name="bwd_win", grid=(n_chips, nb),
        in_specs=[_const_spec(ht.shape), pl.BlockSpec((None, s, 2 * sec), lambda q, b: (b, 0, q))]
        + [HBM_SPEC] * (nm + 1),
        out_specs=(HBM_SPEC,) * (nm + 3),
        out_shape=(jax.ShapeDtypeStruct((N_DEV, d, sec), F32), jax.ShapeDtypeStruct((n_chips, d, sec), F32))
        + tuple(jax.ShapeDtypeStruct((3,) + g.shape[1:], g.dtype) for g in chip_srcs)
        + (jax.ShapeDtypeStruct((4,) + small.shape[1:], small.dtype),),
        scratch_shapes=[pltpu.VMEM((2, d, 2 * sec), F32), pltpu.SemaphoreType.DMA((2,)),
                        pltpu.SemaphoreType.DMA((N_DEV // 2,)), pltpu.SemaphoreType.DMA((N_DEV // 2,)),
                        pltpu.SemaphoreType.DMA((nm, 3)), pltpu.SemaphoreType.DMA((nm, 3)),
                        pltpu.SemaphoreType.DMA((1, 4)), pltpu.SemaphoreType.DMA((1, 4))],
        compiler_params=pltpu.CompilerParams(dimension_semantics=("arbitrary", "arbitrary"), vmem_limit_bytes=VMEM_LIMIT),
    )(ht, dz, *chip_srcs, small)


def _bwd_dx(dz, x, dx1, win, g0, chip_srcs, tm):
    t, d = x.shape
    nc = win.shape[1]
    n = t // tm
    nm = len(chip_srcs)

    def body(dz_ref, x_ref, dx1_ref, win_ref, g0_ref, *rest):
        chip_src = rest[:nm]
        gx_ref, pack_ref = rest[nm:nm + 2]
        chip_out = rest[nm + 2:2 * nm + 2]
        chip_send, chip_recv = rest[2 * nm + 2:]
        i = pl.program_id(0)

        @pl.when(i == 0)
        def _():
            pack_ref[...] = jnp.zeros_like(pack_ref)
            for cp in _chip_copies(chip_src, chip_out, chip_send, chip_recv):
                cp.start()

        xv = x_ref[...]
        r0 = _rsqrt_mean_sq(xv)
        xh = xv * r0
        dh = _dot_nt(dz_ref[...], win_ref[...])
        _store_chunks(pack_ref, 0, _colsum(dh * xh))
        dxh = dh * g0_ref[...]
        gx_ref[...] = dx1_ref[...] + r0 * (dxh - xh * jnp.mean(dxh * xh, axis=-1, keepdims=True))

        @pl.when(i == n - 1)
        def _():
            for cp in _chip_copies(chip_src, chip_out, chip_send, chip_recv):
                cp.wait()

    tile = lambda w: pl.BlockSpec((tm, w), lambda i: (i, 0))
    return pl.pallas_call(
        body, name="bwd_dx", grid=(n,),
        in_specs=[tile(nc), tile(d), tile(d), _const_spec(win.shape), _const_spec(g0.shape)] + [HBM_SPEC] * nm,
        out_specs=(tile(d), pl.BlockSpec((INP_PACK_ROWS, HEAD), lambda i: (0, 0))) + (HBM_SPEC,) * nm,
        out_shape=(jax.ShapeDtypeStruct((t, d), F32), jax.ShapeDtypeStruct((INP_PACK_ROWS, HEAD), F32))
        + tuple(jax.ShapeDtypeStruct((3,) + g.shape[1:], g.dtype) for g in chip_srcs),
        scratch_shapes=[pltpu.SemaphoreType.DMA((nm, 3)), pltpu.SemaphoreType.DMA((nm, 3))],
        compiler_params=pltpu.CompilerParams(dimension_semantics=("arbitrary",), vmem_limit_bytes=VMEM_LIMIT),
    )(dz, x, dx1, win, g0, *chip_srcs)


def _all_gather(shards, full_shapes, slicers, name):
    nw = len(shards)

    def body(*refs):
        start, forward, finish = _gather_phases(refs[:nw], refs[nw:2 * nw], slicers, *refs[2 * nw:])
        start()
        forward()
        finish()

    return pl.pallas_call(
        body, name=name, in_specs=[HBM_SPEC] * nw, out_specs=(HBM_SPEC,) * nw,
        out_shape=tuple(jax.ShapeDtypeStruct(shp, a.dtype) for shp, a in zip(full_shapes, shards)),
        scratch_shapes=_gather_sems(nw),
    )(*shards)


ROW_BLOCKS = 8


def _chip_sums(core_idx, grads, recvs, dtypes, name):
    nw = len(grads)

    def body(idx_ref, *refs):
        gs, rs, outs = refs[:nw], refs[nw:2 * nw], refs[2 * nw:]
        for g, r, o in zip(gs, rs, outs):
            o[...] = (g[...] + r[...]).astype(o.dtype)

    def blk(g):
        return (None,) + g.shape[1:]

    return pl.pallas_call(
        body, name=name,
        grid_spec=pltpu.PrefetchScalarGridSpec(
            num_scalar_prefetch=1, grid=(4,),
            in_specs=[pl.BlockSpec(blk(g), lambda q, s: (2 * q + s[0], 0, 0)) for g in grads]
            + [pl.BlockSpec(blk(g), lambda q, s: (q, 0, 0)) for g in grads],
            out_specs=tuple(pl.BlockSpec(blk(g), lambda q, s: (q, 0, 0)) for g in grads)),
        out_shape=tuple(jax.ShapeDtypeStruct((4,) + g.shape[1:], dt) for g, dt in zip(grads, dtypes)),
    )(core_idx, *grads, *recvs)


def _adamw(w, g, m, v):
    m = ADAM_B1 * m + (1.0 - ADAM_B1) * g
    v = ADAM_B2 * v + (1.0 - ADAM_B2) * (g * g)
    m_hat = m / (1.0 - ADAM_B1 ** ADAM_STEP)
    v_hat = v / (1.0 - ADAM_B2 ** ADAM_STEP)
    delta = -ADAM_LR * (m_hat / (jnp.sqrt(v_hat) + ADAM_EPS) + ADAM_WD * w)
    return delta, m, v


def _adam_sections(sec_idx, grads, recv_sib, recv_chips, wmv):
    nw = len(wmv)

    def body(idx_ref, *refs):
        gs, rs, cs = refs[:nw + 1], refs[nw + 1:2 * nw + 2], refs[2 * nw + 2:3 * nw + 3]
        params = refs[3 * nw + 3:6 * nw + 3]
        outs = refs[6 * nw + 3:]
        for w in range(nw + 1):
            g = gs[w][...] + rs[w][...]
            for k in range(3):
                g = g + cs[w][k].astype(F32)
            if w == nw:
                outs[4 * nw][...] = g
                continue
            wv, mv, vv = (params[3 * w + t][...] for t in range(3))
            delta, m_new, v_new = _adamw(wv, g, mv, vv)
            outs[4 * w][...] = g
            outs[4 * w + 1][...] = delta
            outs[4 * w + 2][...] = m_new
            outs[4 * w + 3][...] = v_new

    def rb(g):
        return g.shape[1] // ROW_BLOCKS

    in_specs = [pl.BlockSpec((None, rb(g), g.shape[2]), lambda i, s: (s[0], i, 0)) for g in grads]
    in_specs += [pl.BlockSpec((None, rb(g), g.shape[2]), lambda i, s: (s[1], i, 0)) for g in grads]
    in_specs += [pl.BlockSpec((3, rb(g), g.shape[2]), lambda i, s: (0, i, 0)) for g in grads]
    sec = lambda g: pl.BlockSpec((rb(g), g.shape[2]), lambda i, s: (i, 0))
    for w in range(nw):
        in_specs += [sec(grads[w])] * 3
    out_specs, out_shape = [], []
    for g in grads[:nw]:
        out_specs += [sec(g)] * 4
        out_shape += [jax.ShapeDtypeStruct(g.shape[1:], F32)] * 4
    out_specs.append(sec(grads[nw]))
    out_shape.append(jax.ShapeDtypeStruct(grads[nw].shape[1:], F32))
    flat = [a for t in wmv for a in t]
    outs = pl.pallas_call(
        body, name="adam_sections",
        grid_spec=pltpu.PrefetchScalarGridSpec(num_scalar_prefetch=1, grid=(ROW_BLOCKS,), in_specs=in_specs,
                                               out_specs=tuple(out_specs)),
        out_shape=tuple(out_shape),
    )(sec_idx, *grads, *recv_sib, *recv_chips, *flat)
    return [outs[4 * w:4 * w + 4] for w in range(nw)], outs[4 * nw]


def _adam_small(sec_idx, gpack, parts, parts_wmv, vec_params, mat_params, conv_wmv):
    items = [(0, parts_wmv, "parts")]
    items += [(r0, t, "vec") for r0, t in vec_params] + [(r0, t, "mat") for r0, t in mat_params]
    items.append((_SEQ_ROWS["dcw"][0], conv_wmv, "conv"))

    def body(idx_ref, g_ref, parts_ref, *refs):
        ins, outs = refs[:3 * len(items)], refs[3 * len(items):]
        for n, (r0, _, kind) in enumerate(items):
            w_ref, m_ref, v_ref = ins[3 * n:3 * n + 3]
            o = outs[4 * n:4 * n + 4]
            if kind == "mat":
                g = g_ref[pl.ds(r0, w_ref.shape[0]), :]
                res = (g,) + _adamw(w_ref[...], g, m_ref[...], v_ref[...])
                for ref, val in zip(o, res):
                    ref[...] = val
            elif kind in ("vec", "parts"):
                for h in range(w_ref.shape[1] // HEAD):
                    cols = slice(h * HEAD, (h + 1) * HEAD)
                    if kind == "vec":
                        g = g_ref[pl.ds(r0 + h, 1), :]
                    else:
                        g = parts_ref[pl.ds(h, 1), :]
                        for dev in range(1, N_DEV):
                            g = g + parts_ref[pl.ds(SUB * dev + h, 1), :]
                    res = (g,) + _adamw(w_ref[:, cols], g, m_ref[:, cols], v_ref[:, cols])
                    for ref, val in zip(o, res):
                        ref[:, cols] = val
            else:
                rows = lax.broadcasted_iota(jnp.int32, (SUB, HEAD), 0)
                for k in range(4):
                    blk = g_ref[pl.ds(r0 + SUB * k, SUB), :]
                    g = jnp.sum(jnp.where(rows == idx_ref[0], blk, 0.0), axis=0, keepdims=True)
                    row = pl.ds(k, 1)
                    res = (g,) + _adamw(w_ref[row, :], g, m_ref[row, :], v_ref[row, :])
                    for ref, val in zip(o, res):
                        ref[row, :] = val
        outs[4 * len(items)][...] = g_ref[pl.ds(_OFF_MID + MID_LOSS_ROW, 1), :]

    flat = [a for _, t, _ in items for a in t]
    full = lambda a: pl.BlockSpec(a.shape, lambda i, s: (0,) * a.ndim)
    out_specs, out_shape = [], []
    for _, t, _ in items:
        out_specs += [full(t[0])] * 4
        out_shape += [jax.ShapeDtypeStruct(t[0].shape, F32)] * 4
    loss_row = jax.ShapeDtypeStruct((1, HEAD), F32)
    out_specs.append(full(loss_row))
    out_shape.append(loss_row)
    outs = pl.pallas_call(
        body, name="adam_small",
        grid_spec=pltpu.PrefetchScalarGridSpec(num_scalar_prefetch=1, grid=(1,),
                                               in_specs=[full(gpack), full(parts)] + [full(a) for a in flat],
                                               out_specs=tuple(out_specs)),
        out_shape=tuple(out_shape),
    )(sec_idx, gpack, parts, *flat)
    return [outs[4 * n:4 * n + 4] for n in range(len(items))], outs[4 * len(items)]


def _col_slicer(width):
    return lambda ref, idx: ref.at[:, pl.ds(pl.multiple_of(idx * width, HEAD), width)]


def _row_slicer(rows):
    return lambda ref, idx: ref.at[pl.ds(pl.multiple_of(idx * rows, SUB), rows), :]


def kernel(x, p, norm_g, w_in, conv_w, conv_b, lru_w_a, lru_b_a, lru_w_x, lru_b_x, lru_lambda, pool_w, pool_scale, w_proj_lru, w_proj_pool, w_out, ple_norm_g, w_ple_gate, w_ple_proj, final_g, loss_target, m_norm_g, m_w_in, m_conv_w, m_conv_b, m_lru_w_a, m_lru_b_a, m_lru_w_x, m_lru_b_x, m_lru_lambda, m_pool_w, m_pool_scale, m_w_proj_lru, m_w_proj_pool, m_w_out, m_ple_norm_g, m_w_ple_gate, m_w_ple_proj, m_final_g, v_norm_g, v_w_in, v_conv_w, v_conv_b, v_lru_w_a, v_lru_b_a, v_lru_w_x, v_lru_b_x, v_lru_lambda, v_pool_w, v_pool_scale, v_w_proj_lru, v_w_proj_pool, v_w_out, v_ple_norm_g, v_w_ple_gate, v_w_ple_proj, v_final_g):
    nb, s, d = x.shape
    t = nb * s
    tm = min(TILE_M, s)
    sec_idx = (4 * lax.axis_index("x") + 2 * lax.axis_index("y") + lax.axis_index("c")).astype(jnp.int32)
    chip_idx = (2 * lax.axis_index("x") + lax.axis_index("y")).astype(jnp.int32)
    core_idx = lax.axis_index("c").astype(jnp.int32)

    h, ht, win, cw = _pre(
        x, norm_g, [w_in[0], conv_w[0]], [True, False],
        [(d, N_DEV * w_in.shape[2]), (conv_w.shape[1], N_DEV * conv_w.shape[2])],
        [_col_slicer(w_in.shape[2]), _col_slicer(conv_w.shape[2])], tm)
    late_shards = [w_proj_lru[0], w_proj_pool[0], w_out[0], w_ple_gate[0], w_ple_proj[0]]
    late_shapes = [(N_DEV * w_proj_lru.shape[1], d), (w_proj_pool.shape[1], N_DEV * w_proj_pool.shape[2]),
                   (N_DEV * w_out.shape[1], d), (N_DEV * w_ple_gate.shape[1], d),
                   (w_ple_proj.shape[1], N_DEV * w_ple_proj.shape[2])]
    late_slicers = [_row_slicer(w_proj_lru.shape[1]), _col_slicer(w_proj_pool.shape[2]), _row_slicer(w_out.shape[1]),
                    _row_slicer(w_ple_gate.shape[1]), _col_slicer(w_ple_proj.shape[2])]

    lw = cw.shape[1]
    pwid = pool_scale.shape[1]
    wa = lru_w_a[0].astype(_MM)
    wx = lru_w_x[0].astype(_MM)
    pw = pool_w[0].astype(_MM)
    ba, bx = lru_b_a[0], lru_b_x[0]
    gf = final_g.reshape(1, d)
    rows_of = lambda g: g.reshape(N_DEV, g.shape[0] // N_DEV, g.shape[1])
    core = core_idx.reshape(1)

    fwd_out = _fwd_seq(
        h, win, cw, conv_b, wa, ba, wx, bx, lru_lambda, pw, pool_scale, late_shards, late_shapes, late_slicers, tm)
    z, hl, ya, yb = fwd_out[:4]
    saved = fwd_out[4:N_SEQ_SAVED]
    wpl, wpp, wout, wpg, wpe = fwd_out[N_SEQ_SAVED:]
    x2d = x.reshape(t, d)
    (dya, dyb, dm, dx1, g_pl, g_pp, g_out, g_pg, g_pe, pack_mid) = _mid(
        ya.reshape(t, lw), yb.reshape(t, pwid), h.reshape(t, d), win, x2d, p[0].reshape(t, -1),
        loss_target.reshape(t, d), wpl, wpp, wout, wpg, wpe, ple_norm_g, gf, tm)
    grads_mid = [rows_of(g_pl), g_pp, rows_of(g_out), rows_of(g_pg), g_pe]
    seq_out = _bwd_seq(z, hl, saved, dya.reshape(nb, s, lw), dyb.reshape(nb, s, pwid), dm.reshape(nb, s, -1), cw,
                       wa, wx, lru_lambda, pw, pool_scale, pack_mid, grads_mid, tm)
    dz, sib_mid = seq_out[0], list(seq_out[2:])
    small = seq_out[1].reshape(N_DEV, SMALL_SEC, HEAD)
    sums_mid = _chip_sums(core, grads_mid, sib_mid, [_WIRE] * len(grads_mid), "chip_sums_mid")
    win_out = _bwd_win(ht, dz, sums_mid, small)
    g_in, sib_in, chips_mid, sib_small = win_out[0], win_out[1], list(win_out[2:-1]), win_out[-1]
    sums_in = _chip_sums(core, [g_in, small], [sib_in, sib_small], [_WIRE, F32], "chip_sums_in")
    dx_out = _bwd_dx(dz.reshape(t, -1), x2d, dx1, win, norm_g, sums_in, tm)
    grad_x, pack_inp, chips_in, chips_small = dx_out

    grads = [g_in] + grads_mid + [small]
    recv_sib = [sib_in] + sib_mid + [sib_small]
    recv_chips = [chips_in] + chips_mid + [chips_small]
    wmv = [(w_in[0], m_w_in[0], v_w_in[0]), (w_proj_lru[0], m_w_proj_lru[0], v_w_proj_lru[0]),
           (w_proj_pool[0], m_w_proj_pool[0], v_w_proj_pool[0]), (w_out[0], m_w_out[0], v_w_out[0]),
           (w_ple_gate[0], m_w_ple_gate[0], v_w_ple_gate[0]), (w_ple_proj[0], m_w_ple_proj[0], v_w_ple_proj[0])]
    big, small_sec = _adam_sections(jnp.stack([sec_idx, chip_idx]), grads, recv_sib, recv_chips, wmv)
    gsmall, g0_parts = _all_gather([small_sec, pack_inp], [(SMALL_ROWS, HEAD), (N_DEV * INP_PACK_ROWS, HEAD)],
                                   [_row_slicer(SMALL_SEC), _row_slicer(INP_PACK_ROWS)], "gather_small")

    R = _SEQ_ROWS
    vec = lambda r0, *t: (r0, tuple(a.reshape(1, -1) for a in t))
    mat = lambda r0, *t: (r0, tuple(a.reshape(-1, HEAD) for a in t))
    norm_wmv = vec(0, norm_g, m_norm_g, v_norm_g)[1]
    vec_params = [vec(R["dcb"][0], conv_b, m_conv_b, v_conv_b),
                  vec(R["dlam"][0], lru_lambda, m_lru_lambda, v_lru_lambda),
                  vec(R["dps"][0], pool_scale, m_pool_scale, v_pool_scale),
                  vec(_OFF_MID, ple_norm_g, m_ple_norm_g, v_ple_norm_g),
                  vec(_OFF_MID + SUB, final_g, m_final_g, v_final_g)]
    mat_params = [mat(R["dwa"][0], lru_w_a, m_lru_w_a, v_lru_w_a), mat(R["dba"][0], lru_b_a, m_lru_b_a, v_lru_b_a),
                  mat(R["dwx"][0], lru_w_x, m_lru_w_x, v_lru_w_x), mat(R["dbx"][0], lru_b_x, m_lru_b_x, v_lru_b_x),
                  mat(R["dpw"][0], pool_w, m_pool_w, v_pool_w)]
    conv_wmv = (conv_w[0], m_conv_w[0], v_conv_w[0])
    small_out, loss_row = _adam_small(sec_idx.reshape(1), gsmall, g0_parts, norm_wmv, vec_params, mat_params, conv_wmv)
    loss = loss_row[0, 0]

    res = {}
    names_small = ["norm_g", "conv_b", "lru_lambda", "pool_scale", "ple_norm_g", "final_g",
                   "lru_w_a", "lru_b_a", "lru_w_x", "lru_b_x", "pool_w", "conv_w"]
    shapes = {"norm_g": norm_g, "conv_b": conv_b, "lru_lambda": lru_lambda, "pool_scale": pool_scale,
              "ple_norm_g": ple_norm_g, "final_g": final_g, "lru_w_a": lru_w_a, "lru_b_a": lru_b_a, "lru_w_x": lru_w_x,
              "lru_b_x": lru_b_x, "pool_w": pool_w, "conv_w": conv_w}
    for name, quad in zip(names_small, small_out):
        res[name] = [a.reshape(shapes[name].shape) for a in quad]
    names_big = ["w_in", "w_proj_lru", "w_proj_pool", "w_out", "w_ple_gate", "w_ple_proj"]
    for name, quad, (w, _, _) in zip(names_big, big, wmv):
        res[name] = [a.reshape((1,) + w.shape) for a in quad]
    order = ["norm_g", "w_in", "conv_w", "conv_b", "lru_w_a", "lru_b_a", "lru_w_x", "lru_b_x", "lru_lambda", "pool_w",
             "pool_scale", "w_proj_lru", "w_proj_pool", "w_out", "ple_norm_g", "w_ple_gate", "w_ple_proj", "final_g"]
    out = [loss, grad_x.reshape(nb, s, d)]
    for kind in range(4):
        out += [res[name][kind] for name in order]
    return tuple(out)
```

```python
import jax
import jax.numpy as jnp
from jax import lax
from jax.experimental import pallas as pl
from jax.experimental.pallas import tpu as pltpu

F32 = jnp.float32
_MM = jnp.bfloat16
_WIRE = jnp.bfloat16

EPS = 1e-6
LRU_C = 8.0
MULT_SQ_FLOOR = 1e-30
POOL_WINDOWS = (2, 4, 8, 16)
N_HEADS = 8
HEAD = 128
HALO = 16
SUB = 8

ADAM_LR = 0.001
ADAM_B1 = 0.9
ADAM_B2 = 0.999
ADAM_EPS = 1e-08
ADAM_WD = 0.01
ADAM_STEP = 10

N_DEV = 8
MESH = pl.DeviceIdType.MESH
VMEM_LIMIT = 60 * 1024 * 1024
TILE_M = 256
N_SEQ_SAVED = 10

_SEQ_ROWS = {"dwa": (0, 1024), "dwx": (1024, 1024), "dpw": (2048, 512), "dba": (2560, 8), "dbx": (2568, 8),
             "dcb": (2576, 8), "dlam": (2584, 8), "dcw": (2592, 32), "dps": (2624, 8)}
SEQ_PACK_ROWS = 2632
MID_PACK_ROWS = 24
MID_LOSS_ROW = 16
INP_PACK_ROWS = 8
SMALL_ROWS = 3072
SMALL_SEC = SMALL_ROWS // N_DEV
_OFF_MID = SEQ_PACK_ROWS


def _dot(a, b):
    return jnp.dot(a, b, preferred_element_type=F32)


def _dot_nt(a, b):
    return lax.dot_general(a, b, (((1,), (1,)), ((), ())), preferred_element_type=F32)


def _dot_tn(a, b):
    return lax.dot_general(a, b, (((0,), (0,)), ((), ())), preferred_element_type=F32)


def _sigmoid(v):
    return 0.5 * jnp.tanh(0.5 * v) + 0.5


def _lru_mult(log_a, a):
    m2 = jnp.maximum(-jnp.tanh(log_a) * (1.0 + a * a), MULT_SQ_FLOOR)
    inv = lax.rsqrt(m2)
    return m2 * inv, inv


def _rows_back(v, n):
    return pltpu.roll(v, n, 0) if n else v


def _rows_ahead(v, n):
    return pltpu.roll(v, v.shape[0] - n, 0) if n else v


def _softplus_neg(lam):
    e = jnp.exp(-jnp.abs(lam))
    w = 1.0 + e
    l1p = jnp.where(w == 1.0, e, jnp.log(w) * (e / (w - 1.0)))
    return jnp.maximum(-lam, 0.0) + l1p


def _rsqrt_mean_sq(v):
    return lax.rsqrt(jnp.mean(v * v, axis=-1, keepdims=True) + EPS)


def _colsum(v):
    return jnp.sum(v, axis=0, keepdims=True)


def _const_spec(shape):
    nd = len(shape)
    return pl.BlockSpec(shape, lambda *_: (0,) * nd, pipeline_mode=pl.Buffered(1))


HBM_SPEC = pl.BlockSpec(memory_space=pl.ANY)


def _my_pos():
    return lax.axis_index("x"), lax.axis_index("y"), lax.axis_index("c")


def _other_chips(x, y):
    return [(1 - x, y), (x, 1 - y), (1 - x, 1 - y)]


def _gather_sems(nw):
    return [pltpu.SemaphoreType.DMA((nw, 7)), pltpu.SemaphoreType.DMA((nw, 7)), pltpu.SemaphoreType.DMA((nw,))]


def _gather_phases(srcs, outs, slicers, send_sems, recv_sems, local_sems):
    nw = len(srcs)
    x, y, c = _my_pos()
    me, sibling = (x, y, c), (x, y, 1 - c)
    chips = _other_chips(x, y)

    def part(w, pos):
        return slicers[w](outs[w], 4 * pos[0] + 2 * pos[1] + pos[2])

    def copy(w, k, block, to, src=None):
        return pltpu.make_async_remote_copy(
            src_ref=part(w, block) if src is None else src, dst_ref=part(w, block),
            send_sem=send_sems.at[w, k], recv_sem=recv_sems.at[w, k], device_id=to, device_id_type=MESH)

    def mine():
        return [pltpu.make_async_copy(srcs[w], part(w, me), local_sems.at[w]) for w in range(nw)]

    def first():
        out = []
        for w in range(nw):
            out.append(copy(w, 0, me, sibling, src=srcs[w]))
            out += [copy(w, 1 + j, me, (*chip, c), src=srcs[w]) for j, chip in enumerate(chips)]
        return out

    def passed():
        return [copy(w, 4 + j, (*chip, c), sibling) for j, chip in enumerate(chips) for w in range(nw)]

    def start():
        for cp in mine() + first():
            cp.start()

    def forward():
        for j, chip in enumerate(chips):
            for w in range(nw):
                copy(w, 1 + j, (*chip, c), me).wait_recv()
                copy(w, 4 + j, (*chip, c), sibling).start()

    def finish():
        for w in range(nw):
            copy(w, 0, sibling, me).wait_recv()
            for j, chip in enumerate(chips):
                copy(w, 4 + j, (*chip, 1 - c), me).wait_recv()
        for cp in first() + passed():
            cp.wait_send()
        for cp in mine():
            cp.wait()

    return start, forward, finish


def _sibling_copies(srcs, outs, send_sems, recv_sems):
    x, y, c = _my_pos()
    return [pltpu.make_async_remote_copy(
        src_ref=srcs[w].at[2 * q + (1 - c)], dst_ref=outs[w].at[q], send_sem=send_sems.at[w, q],
        recv_sem=recv_sems.at[w, q], device_id=(x, y, 1 - c), device_id_type=MESH)
        for w in range(len(srcs)) for q in range(4)]


def _chip_copies(srcs, outs, send_sems, recv_sems, which=None):
    x, y, c = _my_pos()
    return [pltpu.make_async_remote_copy(
        src_ref=srcs[w].at[2 * px + py], dst_ref=outs[w].at[k], send_sem=send_sems.at[w, k],
        recv_sem=recv_sems.at[w, k], device_id=(px, py, c), device_id_type=MESH)
        for w in (range(len(srcs)) if which is None else which) for k, (px, py) in enumerate(_other_chips(x, y))]


def _lru_gates(xc, h, wa_ref, ba_ref, wx_ref, bx_ref, neg_c_sp):
    xc16 = xc.astype(_MM)
    r = _sigmoid(_dot(xc16, wa_ref[h]) + ba_ref[pl.ds(h, 1), :])
    i = _sigmoid(_dot(xc16, wx_ref[h]) + bx_ref[pl.ds(h, 1), :])
    return r, i, neg_c_sp * r


def _conv_head(xa_ext, cw_ref, cb_ref, cols, tm):
    ext = xa_ext[:, cols]
    xc = cb_ref[:, cols] + cw_ref[pl.ds(3, 1), cols] * ext[HALO:HALO + tm]
    for k in range(3):
        xc = xc + cw_ref[pl.ds(k, 1), cols] * _rows_back(ext, 3 - k)[HALO:HALO + tm]
    return xc


def _window_sum(ext, k, shift):
    n = 1
    while n < k:
        ext = ext + shift(ext, n)
        n *= 2
    return ext


def _pool_diff(xb_ext, g, k, pos, tm):
    cols = slice(g * HEAD, (g + 1) * HEAD)
    ext = xb_ext[:, cols]
    ws = _window_sum(ext, k, _rows_back)[HALO:HALO + tm]
    inv_cnt = 1.0 / jnp.minimum(pos + 1, k).astype(F32)
    return ws * inv_cnt - ext[HALO:HALO + tm], inv_cnt


def _staged_sources(shards, as_operand):
    in_specs = [_const_spec(a.shape) if c else HBM_SPEC for a, c in zip(shards, as_operand)]
    stages = [pltpu.VMEM(a.shape, _MM) for a, c in zip(shards, as_operand) if c]
    dtypes = [_MM if c else a.dtype for a, c in zip(shards, as_operand)]
    return in_specs, stages, dtypes


def _gather_sources(srcs, stages, as_operand):
    stages = iter(stages)
    return [next(stages) if c else src for src, c in zip(srcs, as_operand)]


def _fill_stages(srcs, staged, as_operand):
    for src, dst, c in zip(srcs, staged, as_operand):
        if c:
            dst[...] = src[...].astype(dst.dtype)


def _pre(x, g0, shards, as_operand, shapes, slicers, tm):
    nb, s, d = x.shape
    nt = s // tm
    nl = len(shards)
    shard_specs, stage_shapes, wire = _staged_sources(shards, as_operand)

    def body(x_ref, g0_ref, *rest):
        src, rest = rest[:nl], rest[nl:]
        h_ref, ht_ref = rest[:2]
        out, rest = rest[2:2 + nl], rest[2 + nl:]
        (send_sems, recv_sems, local_sems), stages = rest[:3], rest[3:]
        step_no = pl.program_id(0) * nt + pl.program_id(1)
        staged = _gather_sources(src, stages, as_operand)
        g_start, g_forward, g_finish = _gather_phases(staged, out, slicers, send_sems, recv_sems, local_sems)

        @pl.when(step_no == 0)
        def _():
            _fill_stages(src, staged, as_operand)
            g_start()

        xv = x_ref[...]
        h16 = ((xv * _rsqrt_mean_sq(xv)) * g0_ref[...]).astype(_MM)
        h_ref[...] = h16
        ht_ref[...] = h16.T

        @pl.when(step_no == nb * nt - 1)
        def _():
            g_forward()
            g_finish()

    tile = pl.BlockSpec((None, tm, d), lambda b, j: (b, j, 0))
    return pl.pallas_call(
        body, name="pre", grid=(nb, nt),
        in_specs=[tile, _const_spec(g0.shape)] + shard_specs,
        out_specs=(tile, pl.BlockSpec((None, d, tm), lambda b, j: (b, 0, j))) + (HBM_SPEC,) * nl,
        out_shape=(jax.ShapeDtypeStruct((nb, s, d), _MM), jax.ShapeDtypeStruct((nb, d, s), _MM))
        + tuple(jax.ShapeDtypeStruct(shp, dt) for shp, dt in zip(shapes, wire)),
        scratch_shapes=_gather_sems(nl) + stage_shapes,
        compiler_params=pltpu.CompilerParams(dimension_semantics=("arbitrary", "arbitrary")),
    )(x, g0, *shards)


def _fwd_seq(h, win, cw, cb, wa, ba, wx, bx, lam, pw, ps, late_shards, late_shapes, late_slicers, tm):
    nb, s, d = h.shape
    lw = cw.shape[1]
    pwid = ps.shape[1]
    nc = 2 * lw + 2 * pwid
    nt = s // tm
    nl = len(late_shards)
    n_steps = nb * nt
    as_operand = [True] * nl
    shard_specs, stage_shapes, wire = _staged_sources(late_shards, as_operand)

    def body(h_ref, win_ref, cw_ref, cb_ref, wa_ref, ba_ref, wx_ref, bx_ref, lam_ref, pw_ref, ps_ref, *rest):
        late_src, rest = rest[:nl], rest[nl:]
        z_ref, hl_ref, ya_ref, yb_ref, xc_ref, r_ref, i_ref, a_buf, diff_ref, yp_ref = rest[:N_SEQ_SAVED]
        late_out, rest = rest[N_SEQ_SAVED:N_SEQ_SAVED + nl], rest[N_SEQ_SAVED + nl:]
        (xa_ext, xb_ext, carry, send_sems, recv_sems, local_sems), stages = rest[:6], rest[6:]
        j = pl.program_id(1)
        step_no = pl.program_id(0) * nt + j
        staged = _gather_sources(late_src, stages, as_operand)
        g_start, g_forward, g_finish = _gather_phases(staged, late_out, late_slicers, send_sems, recv_sems, local_sems)

        @pl.when(step_no == 0)
        def _():
            _fill_stages(late_src, staged, as_operand)
            g_start()

        pl.when(step_no == n_steps // 2)(g_forward)

        @pl.when(j == 0)
        def _():
            xa_ext[pl.ds(0, HALO), :] = jnp.zeros((HALO, lw), F32)
            xb_ext[pl.ds(0, HALO), :] = jnp.zeros((HALO, pwid), F32)
            carry[...] = jnp.zeros_like(carry)

        h16 = h_ref[...]
        xa = _dot(h16, win_ref[:, 0:lw])
        z_ref[:, 0:lw] = xa
        xa_ext[pl.ds(HALO, tm), :] = xa
        for c0 in range(lw, nc, lw):
            z_ref[:, c0:c0 + lw] = _dot(h16, win_ref[:, c0:c0 + lw])

        for h in range(N_HEADS):
            cols = slice(h * HEAD, (h + 1) * HEAD)
            xc = _conv_head(xa_ext, cw_ref, cb_ref, cols, tm)
            neg_c_sp = -LRU_C * _softplus_neg(lam_ref[:, cols])
            r, i, log_a = _lru_gates(xc, h, wa_ref, ba_ref, wx_ref, bx_ref, neg_c_sp)
            a = jnp.exp(log_a)
            xc_ref[:, cols] = xc
            r_ref[:, cols] = r
            i_ref[:, cols] = i
            a_buf[:, cols] = a
            hl_ref[:, cols] = _lru_mult(log_a, a)[0] * (i * xc)

        rows = lax.broadcasted_iota(jnp.int32, (SUB, lw), 0)

        def step(c, car):
            i0 = pl.multiple_of(c * SUB, SUB)
            av = a_buf[pl.ds(i0, SUB), :]
            bv = hl_ref[pl.ds(i0, SUB), :]
            for sh in (1, 2, 4):
                m = rows >= sh
                a_sh = jnp.where(m, pltpu.roll(av, sh, 0), 1.0)
                b_sh = jnp.where(m, pltpu.roll(bv, sh, 0), 0.0)
                bv = av * b_sh + bv
                av = av * a_sh
            hv = av * car + bv
            hl_ref[pl.ds(i0, SUB), :] = hv
            return jnp.broadcast_to(hv[SUB - 1:SUB, :], (SUB, lw))

        carry[...] = lax.fori_loop(0, tm // SUB, step, carry[...])

        ga = z_ref[:, lw:2 * lw]
        ya_ref[...] = (hl_ref[...] * (ga * _sigmoid(ga))).astype(ya_ref.dtype)

        xb_ext[pl.ds(HALO, tm), :] = z_ref[:, 2 * lw:2 * lw + pwid]
        pos = j * tm + lax.broadcasted_iota(jnp.int32, (tm, HEAD), 0)
        for g, k in enumerate(POOL_WINDOWS):
            cols = slice(g * HEAD, (g + 1) * HEAD)
            diff16 = _pool_diff(xb_ext, g, k, pos, tm)[0].astype(_MM)
            yp = _dot(diff16, pw_ref[g])
            diff_ref[:, cols] = diff16
            yp_ref[:, cols] = yp
            gb = z_ref[:, 2 * lw + pwid + g * HEAD:2 * lw + pwid + (g + 1) * HEAD]
            yb_ref[:, cols] = ((yp * ps_ref[:, cols]) * (gb * _sigmoid(gb))).astype(yb_ref.dtype)

        xa_ext[pl.ds(0, HALO), :] = xa_ext[pl.ds(tm, HALO), :]
        xb_ext[pl.ds(0, HALO), :] = xb_ext[pl.ds(tm, HALO), :]
        pl.when(step_no == n_steps - 1)(g_finish)

    tile = lambda w: pl.BlockSpec((None, tm, w), lambda b, j: (b, j, 0))
    return pl.pallas_call(
        body, name="fwd_seq", grid=(nb, nt),
        in_specs=[tile(d), _const_spec((d, nc)), _const_spec(cw.shape), _const_spec(cb.shape),
                  _const_spec(wa.shape), _const_spec(ba.shape), _const_spec(wx.shape), _const_spec(bx.shape),
                  _const_spec(lam.shape), _const_spec(pw.shape), _const_spec(ps.shape)] + shard_specs,
        out_specs=(tile(nc), tile(lw), tile(lw), tile(pwid), tile(lw), tile(lw), tile(lw), tile(lw), tile(pwid),
                   tile(pwid)) + (HBM_SPEC,) * nl,
        out_shape=(jax.ShapeDtypeStruct((nb, s, nc), F32), jax.ShapeDtypeStruct((nb, s, lw), F32),
                   jax.ShapeDtypeStruct((nb, s, lw), _MM), jax.ShapeDtypeStruct((nb, s, pwid), _MM))
        + (jax.ShapeDtypeStruct((nb, s, lw), F32),) * 4
        + (jax.ShapeDtypeStruct((nb, s, pwid), _MM), jax.ShapeDtypeStruct((nb, s, pwid), F32))
        + tuple(jax.ShapeDtypeStruct(shp, dt) for shp, dt in zip(late_shapes, wire)),
        scratch_shapes=[pltpu.VMEM((tm + HALO, lw), F32), pltpu.VMEM((tm + HALO, pwid), F32),
                        pltpu.VMEM((SUB, lw), F32)] + _gather_sems(nl) + stage_shapes,
        compiler_params=pltpu.CompilerParams(dimension_semantics=("arbitrary", "arbitrary"), vmem_limit_bytes=VMEM_LIMIT),
    )(h, win, cw, cb, wa, ba, wx, bx, lam, pw, ps, *late_shards)


def _store_chunks(ref, row0, vec):
    for h in range(vec.shape[1] // HEAD):
        ref[pl.ds(row0 + h, 1), :] += vec[:, h * HEAD:(h + 1) * HEAD]


def _mid(ya, yb, h, win, x, p, tgt, wpl, wpp, wout, wpg, wpe, g1, gf, tm):
    t, d = x.shape
    lw = ya.shape[1]
    pwid = yb.shape[1]
    pdim = p.shape[1]
    n = t // tm
    sec_pp = wpp.shape[1] // N_DEV
    sec_pe = wpe.shape[1] // N_DEV
    n_win_blocks = win.shape[1] // d
    win_cols = lambda c: pl.BlockSpec((d, d), lambda i: (0, c), pipeline_mode=pl.Buffered(1))

    def body(ya_ref, yb_ref, h_ref, wma_ref, wmb_ref, x_ref, p_ref, tgt_ref, wpl_ref, wpp_ref, wout_ref, wpg_ref, wpe_ref,
             g1_ref, gf_ref,
             dya_ref, dyb_ref, dm_ref, dx1_ref, gpl_ref, gpp_ref, gout_ref, gpg_ref, gpe_ref, pack_ref,
             acc_pl, acc_pp, acc_out, acc_pg, acc_pe):
        i = pl.program_id(0)

        @pl.when(i == 0)
        def _():
            for acc in (acc_pl, acc_pp, acc_out, acc_pg, acc_pe):
                acc[...] = jnp.zeros_like(acc)
            pack_ref[...] = jnp.zeros_like(pack_ref)

        ya16 = ya_ref[...]
        yb16 = yb_ref[...]
        br_a = _dot(ya16, wpl_ref[...])
        br_b = _dot(yb16, wpp_ref[...])
        h16 = h_ref[...]
        sa = _sigmoid(_dot(h16, wma_ref[...]))
        sb = _sigmoid(_dot(h16, wmb_ref[...]))
        mg16 = (sa * br_a + sb * br_b).astype(_MM)
        x1 = x_ref[...] + _dot(mg16, wout_ref[...])
        r1 = _rsqrt_mean_sq(x1)
        n1 = x1 * r1
        h116 = (n1 * g1_ref[...]).astype(_MM)
        gate = _sigmoid(_dot(h116, wpg_ref[...]))
        p16 = p_ref[...].astype(_MM)
        pe = _dot(p16, wpe_ref[...])
        x2 = x1 + gate * pe
        r2 = _rsqrt_mean_sq(x2)
        n2 = x2 * r2
        err = n2 * gf_ref[...] - tgt_ref[...]
        sq = jnp.sum(_colsum(err * err), axis=1, keepdims=True)
        pack_ref[pl.ds(MID_LOSS_ROW, 1), :] += jnp.broadcast_to(sq * (0.5 / d), (1, HEAD))

        dy = err * (1.0 / d)
        _store_chunks(pack_ref, SUB, _colsum(dy * n2))
        dn2 = dy * gf_ref[...]
        dx2 = r2 * (dn2 - n2 * jnp.mean(dn2 * n2, axis=-1, keepdims=True))
        dpe16 = (dx2 * gate).astype(_MM)
        dpg16 = ((dx2 * pe) * (gate * (1.0 - gate))).astype(_MM)
        acc_pe[...] += _dot_tn(p16, dpe16)
        acc_pg[...] += _dot_tn(h116, dpg16)
        dh1 = _dot_nt(dpg16, wpg_ref[...])
        _store_chunks(pack_ref, 0, _colsum(dh1 * n1))
        dn1 = dh1 * g1_ref[...]
        dx1 = dx2 + r1 * (dn1 - n1 * jnp.mean(dn1 * n1, axis=-1, keepdims=True))
        dx1_ref[...] = dx1
        dx116 = dx1.astype(_MM)
        acc_out[...] += _dot_tn(mg16, dx116)
        dmg = _dot_nt(dx116, wout_ref[...])
        da16 = (dmg * sa).astype(_MM)
        db16 = (dmg * sb).astype(_MM)
        dm_ref[:, 0:d] = ((dmg * br_a) * (sa * (1.0 - sa))).astype(dm_ref.dtype)
        dm_ref[:, d:2 * d] = ((dmg * br_b) * (sb * (1.0 - sb))).astype(dm_ref.dtype)
        acc_pl[...] += _dot_tn(ya16, da16)
        acc_pp[...] += _dot_tn(yb16, db16)
        dya_ref[...] = _dot_nt(da16, wpl_ref[...])
        dyb_ref[...] = _dot_nt(db16, wpp_ref[...])

        @pl.when(i == n - 1)
        def _():
            pltpu.sync_copy(acc_pl, gpl_ref)
            pltpu.sync_copy(acc_out, gout_ref)
            pltpu.sync_copy(acc_pg, gpg_ref)
            for k in range(N_DEV):
                pltpu.sync_copy(acc_pp.at[:, pl.ds(k * sec_pp, sec_pp)], gpp_ref.at[k])
                pltpu.sync_copy(acc_pe.at[:, pl.ds(k * sec_pe, sec_pe)], gpe_ref.at[k])

    tile = lambda w: pl.BlockSpec((tm, w), lambda i: (i, 0))
    any_spec = pl.BlockSpec(memory_space=pl.ANY)
    full = lambda shape: pl.BlockSpec(shape, lambda i: (0,) * len(shape))
    return pl.pallas_call(
        body, name="mid", grid=(n,),
        in_specs=[tile(lw), tile(pwid), tile(d), win_cols(n_win_blocks - 2), win_cols(n_win_blocks - 1),
                  tile(d), tile(pdim), tile(d),
                  _const_spec(wpl.shape), _const_spec(wpp.shape), _const_spec(wout.shape), _const_spec(wpg.shape),
                  _const_spec(wpe.shape), _const_spec(g1.shape), _const_spec(gf.shape)],
        out_specs=(tile(lw), tile(pwid), tile(2 * d), tile(d), any_spec, any_spec, any_spec, any_spec, any_spec,
                   full((MID_PACK_ROWS, HEAD))),
        out_shape=(jax.ShapeDtypeStruct((t, lw), F32), jax.ShapeDtypeStruct((t, pwid), F32),
                   jax.ShapeDtypeStruct((t, 2 * d), _MM), jax.ShapeDtypeStruct((t, d), F32),
                   jax.ShapeDtypeStruct(wpl.shape, F32), jax.ShapeDtypeStruct((N_DEV, wpp.shape[0], sec_pp), F32),
                   jax.ShapeDtypeStruct(wout.shape, F32), jax.ShapeDtypeStruct(wpg.shape, F32),
                   jax.ShapeDtypeStruct((N_DEV, wpe.shape[0], sec_pe), F32),
                   jax.ShapeDtypeStruct((MID_PACK_ROWS, HEAD), F32)),
        scratch_shapes=[pltpu.VMEM(wpl.shape, F32), pltpu.VMEM(wpp.shape, F32), pltpu.VMEM(wout.shape, F32),
                        pltpu.VMEM(wpg.shape, F32), pltpu.VMEM(wpe.shape, F32)],
        compiler_params=pltpu.CompilerParams(dimension_semantics=("arbitrary",), vmem_limit_bytes=VMEM_LIMIT),
    )(ya, yb, h, win, win, x, p, tgt, wpl, wpp, wout, wpg, wpe, g1, gf)


def _bwd_seq(z, hl, saved, dya, dyb, dm, cw, wa, wx, lam, pw, ps, pack_mid, sib_grads, tm):
    nb, s, zw = z.shape
    lw = cw.shape[1]
    pwid = ps.shape[1]
    nt = s // tm
    nc = zw + dm.shape[2]
    R = _SEQ_ROWS
    ns = len(sib_grads)

    def body(zq_ref, hl_ref, hlh_ref, xc_buf, r_buf, i_buf, a_ref, diff_ref, yp_ref, dya_ref, dyb_ref, dm_ref,
             cw_ref, wa_ref, wx_ref, lam_ref, pw_ref, ps_ref, pack_mid_ref, *rest):
        sib_src, rest = rest[:ns], rest[ns:]
        dz_ref, pack_ref = rest[:2]
        sib_out, rest = rest[2:2 + ns], rest[2 + ns:]
        a_ext, an_buf, hl_ext, dh_buf, dxc_ext, q_ext, carry, send_sems, recv_sems = rest
        b = pl.program_id(0)
        j = pl.program_id(1)
        jr = nt - 1 - j
        has_prev = jr > 0

        @pl.when((b == 0) & (j == 0))
        def _():
            for cp in _sibling_copies(sib_src, sib_out, send_sems, recv_sems):
                cp.start()

        dz_ref[:, zw:nc] = dm_ref[...]

        @pl.when((b == 0) & (j == 0))
        def _():
            pack_ref[...] = jnp.zeros_like(pack_ref)

        @pl.when(j == 0)
        def _():
            a_ext[pl.ds(tm, SUB), :] = jnp.zeros((SUB, lw), F32)
            dxc_ext[pl.ds(tm, HALO), :] = jnp.zeros((HALO, lw), F32)
            q_ext[pl.ds(tm, HALO), :] = jnp.zeros((HALO, pwid), F32)
            carry[...] = jnp.zeros_like(carry)

        hl_ext[pl.ds(0, SUB), :] = jnp.where(has_prev, hlh_ref[...], 0.0)
        hl_ext[pl.ds(SUB, tm), :] = hl_ref[...]
        a_ext[pl.ds(0, tm), :] = a_ref[...]
        for h in range(N_HEADS):
            cols = slice(h * HEAD, (h + 1) * HEAD)
            ga = zq_ref[:, lw + h * HEAD:lw + (h + 1) * HEAD]
            sg = _sigmoid(ga)
            dyav = dya_ref[:, cols]
            dh_buf[:, cols] = dyav * (ga * sg)
            dga = (dyav * hl_ext[pl.ds(SUB, tm), cols]) * (sg * (1.0 + ga * (1.0 - sg)))
            dz_ref[:, lw + h * HEAD:lw + (h + 1) * HEAD] = dga.astype(dz_ref.dtype)
        an_buf[...] = _rows_ahead(a_ext[...], 1)[0:tm]

        rows = lax.broadcasted_iota(jnp.int32, (SUB, lw), 0)
        nch = tm // SUB

        def step(c, car):
            i0 = pl.multiple_of((nch - 1 - c) * SUB, SUB)
            cv = an_buf[pl.ds(i0, SUB), :]
            bv = dh_buf[pl.ds(i0, SUB), :]
            for sh in (1, 2, 4):
                m = rows < SUB - sh
                c_sh = jnp.where(m, pltpu.roll(cv, SUB - sh, 0), 1.0)
                b_sh = jnp.where(m, pltpu.roll(bv, SUB - sh, 0), 0.0)
                bv = cv * b_sh + bv
                cv = cv * c_sh
            hv = cv * car + bv
            dh_buf[pl.ds(i0, SUB), :] = hv
            return jnp.broadcast_to(hv[0:1, :], (SUB, lw))

        carry[...] = lax.fori_loop(0, nch, step, carry[...])

        for h in range(N_HEADS):
            cols = slice(h * HEAD, (h + 1) * HEAD)
            lam_h = lam_ref[:, cols]
            neg_c_sp = -LRU_C * _softplus_neg(lam_h)
            xc = xc_buf[:, cols]
            r = r_buf[:, cols]
            i = i_buf[:, cols]
            a = a_ref[:, cols]
            mult, inv_mult = _lru_mult(neg_c_sp * r, a)
            dh = dh_buf[:, cols]
            dhx = dh * xc
            di = dhx * mult
            h_prev = _rows_back(hl_ext[:, cols], 1)[SUB:SUB + tm]
            dlog_a = (dh * h_prev) * a - (dhx * i) * ((a * a) * inv_mult)
            pack_ref[pl.ds(R["dlam"][0] + h, 1), :] += _colsum(dlog_a * r) * (LRU_C * _sigmoid(-lam_h))
            dpa = (dlog_a * neg_c_sp) * (r * (1.0 - r))
            dpx = di * (i * (1.0 - i))
            dpa16 = dpa.astype(_MM)
            dpx16 = dpx.astype(_MM)
            xc16 = xc.astype(_MM)
            dxc = dh * (mult * i) + _dot_nt(dpa16, wa_ref[h]) + _dot_nt(dpx16, wx_ref[h])
            pack_ref[pl.ds(R["dwa"][0] + h * HEAD, HEAD), :] += _dot_tn(xc16, dpa16)
            pack_ref[pl.ds(R["dwx"][0] + h * HEAD, HEAD), :] += _dot_tn(xc16, dpx16)
            pack_ref[pl.ds(R["dba"][0] + h, 1), :] += _colsum(dpa)
            pack_ref[pl.ds(R["dbx"][0] + h, 1), :] += _colsum(dpx)
            pack_ref[pl.ds(R["dcb"][0] + h, 1), :] += _colsum(dxc)
            dxc_ext[pl.ds(0, tm), cols] = dxc

        for h in range(N_HEADS):
            cols = slice(h * HEAD, (h + 1) * HEAD)
            dxc_all = dxc_ext[:, cols]
            xa = zq_ref[:, cols]
            dxa = None
            for k in range(4):
                dxc_k = _rows_ahead(dxc_all, 3 - k)[0:tm]
                pack_ref[pl.ds(R["dcw"][0] + SUB * k + h, 1), :] += _colsum(dxc_k * xa)
                term = cw_ref[pl.ds(k, 1), cols] * dxc_k
                dxa = term if dxa is None else dxa + term
            dz_ref[:, cols] = dxa.astype(dz_ref.dtype)

        pos = jr * tm + lax.broadcasted_iota(jnp.int32, (tm, HEAD), 0)
        for g, k in enumerate(POOL_WINDOWS):
            cols = slice(g * HEAD, (g + 1) * HEAD)
            inv_cnt = 1.0 / jnp.minimum(pos + 1, k).astype(F32)
            diff16 = diff_ref[:, cols]
            yp = yp_ref[:, cols]
            sc = ps_ref[:, cols]
            gb = zq_ref[:, 2 * lw + pwid + g * HEAD:2 * lw + pwid + (g + 1) * HEAD]
            sgb = _sigmoid(gb)
            dybv = dyb_ref[:, cols]
            dy_pool = dybv * (gb * sgb)
            dgb = (dybv * (yp * sc)) * (sgb * (1.0 + gb * (1.0 - sgb)))
            pack_ref[pl.ds(R["dps"][0] + g, 1), :] += _colsum(dy_pool * yp)
            dyp16 = (dy_pool * sc).astype(_MM)
            pack_ref[pl.ds(R["dpw"][0] + g * HEAD, HEAD), :] += _dot_tn(diff16, dyp16)
            ddiff = _dot_nt(dyp16, pw_ref[g])
            q_ext[pl.ds(0, tm), cols] = ddiff * inv_cnt
            dxb = _window_sum(q_ext[:, cols], k, _rows_ahead)[0:tm] - ddiff
            dz_ref[:, 2 * lw + g * HEAD:2 * lw + (g + 1) * HEAD] = dxb.astype(dz_ref.dtype)
            dz_ref[:, 2 * lw + pwid + g * HEAD:2 * lw + pwid + (g + 1) * HEAD] = dgb.astype(dz_ref.dtype)

        a_ext[pl.ds(tm, SUB), :] = a_ext[pl.ds(0, SUB), :]
        dxc_ext[pl.ds(tm, HALO), :] = dxc_ext[pl.ds(0, HALO), :]
        q_ext[pl.ds(tm, HALO), :] = q_ext[pl.ds(0, HALO), :]

        @pl.when((b == nb - 1) & (j == nt - 1))
        def _():
            pack_ref[pl.ds(_OFF_MID, MID_PACK_ROWS), :] = pack_mid_ref[...]
            for cp in _sibling_copies(sib_src, sib_out, send_sems, recv_sems):
                cp.wait()

    rev = lambda w: pl.BlockSpec((None, tm, w), lambda b, j: (b, nt - 1 - j, 0))
    prev_rows = lambda rows, w: pl.BlockSpec(
        (None, rows, w), lambda b, j: (b, jnp.maximum((nt - 1 - j) * (tm // rows) - 1, 0), 0))
    return pl.pallas_call(
        body, name="bwd_seq", grid=(nb, nt),
        in_specs=[rev(zw), rev(lw), prev_rows(SUB, lw), rev(lw), rev(lw), rev(lw), rev(lw), rev(pwid), rev(pwid),
                  rev(lw), rev(pwid), rev(nc - zw),
                  _const_spec(cw.shape), _const_spec(wa.shape), _const_spec(wx.shape), _const_spec(lam.shape),
                  _const_spec(pw.shape), _const_spec(ps.shape), _const_spec(pack_mid.shape)] + [HBM_SPEC] * ns,
        out_specs=(rev(nc), pl.BlockSpec((SMALL_ROWS, HEAD), lambda b, j: (0, 0))) + (HBM_SPEC,) * ns,
        out_shape=(jax.ShapeDtypeStruct((nb, s, nc), _MM), jax.ShapeDtypeStruct((SMALL_ROWS, HEAD), F32))
        + tuple(jax.ShapeDtypeStruct((4,) + g.shape[1:], g.dtype) for g in sib_grads),
        scratch_shapes=[pltpu.VMEM((tm + SUB, lw), F32), pltpu.VMEM((tm, lw), F32), pltpu.VMEM((tm + SUB, lw), F32),
                        pltpu.VMEM((tm, lw), F32), pltpu.VMEM((tm + HALO, lw), F32),
                        pltpu.VMEM((tm + HALO, pwid), F32), pltpu.VMEM((SUB, lw), F32),
                        pltpu.SemaphoreType.DMA((ns, 4)), pltpu.SemaphoreType.DMA((ns, 4))],
        compiler_params=pltpu.CompilerParams(dimension_semantics=("arbitrary", "arbitrary"), vmem_limit_bytes=VMEM_LIMIT),
    )(z, hl, hl, *saved, dya, dyb, dm, cw, wa, wx, lam, pw, ps, pack_mid, *sib_grads)


def _bwd_win(ht, dz, chip_srcs, small):
    nb, d, s = ht.shape
    nc = dz.shape[2]
    sec = nc // N_DEV
    n_chips = N_DEV // 2
    nm = len(chip_srcs)

    def body(ht_ref, dz_ref, *rest):
        chip_src, small_src = rest[:nm], rest[nm]
        gwin_ref, recv_ref = rest[nm + 1:nm + 3]
        chip_out, small_out = rest[nm + 3:2 * nm + 3], rest[2 * nm + 3]
        acc, local_sems, sib_send, sib_recv, chip_send, chip_recv, small_send, small_recv = rest[2 * nm + 4:]
        q = pl.program_id(0)
        b = pl.program_id(1)
        slot = q % 2
        x, y, c = _my_pos()

        def half(sl, core):
            return acc.at[sl, :, pl.ds(pl.multiple_of(core * sec, HEAD), sec)]

        def local_copy(qq, sl):
            return pltpu.make_async_copy(half(sl, c), gwin_ref.at[2 * qq + c], local_sems.at[sl])

        def sib_copy(qq, sl):
            return pltpu.make_async_remote_copy(
                src_ref=half(sl, 1 - c), dst_ref=recv_ref.at[qq], send_sem=sib_send.at[qq],
                recv_sem=sib_recv.at[qq], device_id=(x, y, 1 - c), device_id_type=MESH)

        def others():
            return (_chip_copies(chip_src, chip_out, chip_send, chip_recv)
                    + _sibling_copies([small_src], [small_out], small_send, small_recv))

        @pl.when((q == 0) & (b == 0))
        def _():
            for cp in others():
                cp.start()

        @pl.when((q >= 2) & (b == 0))
        def _():
            local_copy(q - 2, slot).wait()
            sib_copy(q - 2, slot).wait_send()

        part = _dot(ht_ref[b], dz_ref[...])

        @pl.when(b == 0)
        def _():
            acc[slot] = part

        @pl.when(b != 0)
        def _():
            acc[slot] += part

        @pl.when(b == nb - 1)
        def _():
            local_copy(q, slot).start()
            sib_copy(q, slot).start()

        @pl.when((q == n_chips - 1) & (b == nb - 1))
        def _():
            for qq in (n_chips - 2, n_chips - 1):
                local_copy(qq, qq % 2).wait()
                sib_copy(qq, qq % 2).wait_send()
            for qq in range(n_chips):
                sib_copy(qq, 0).wait_recv()
            for cp in others():
                cp.wait()

    return pl.pallas_call(
        body, name="bwd_win", grid=(n_chips, nb),
        in_specs=[_const_spec(ht.shape), pl.BlockSpec((None, s, 2 * sec), lambda q, b: (b, 0, q))]
        + [HBM_SPEC] * (nm + 1),
        out_specs=(HBM_SPEC,) * (nm + 3),
        out_shape=(jax.ShapeDtypeStruct((N_DEV, d, sec), F32), jax.ShapeDtypeStruct((n_chips, d, sec), F32))
        + tuple(jax.ShapeDtypeStruct((3,) + g.shape[1:], g.dtype) for g in chip_srcs)
        + (jax.ShapeDtypeStruct((4,) + small.shape[1:], small.dtype),),
        scratch_shapes=[pltpu.VMEM((2, d, 2 * sec), F32), pltpu.SemaphoreType.DMA((2,)),
                        pltpu.SemaphoreType.DMA((N_DEV // 2,)), pltpu.SemaphoreType.DMA((N_DEV // 2,)),
                        pltpu.SemaphoreType.DMA((nm, 3)), pltpu.SemaphoreType.DMA((nm, 3)),
                        pltpu.SemaphoreType.DMA((1, 4)), pltpu.SemaphoreType.DMA((1, 4))],
        compiler_params=pltpu.CompilerParams(dimension_semantics=("arbitrary", "arbitrary"), vmem_limit_bytes=VMEM_LIMIT),
    )(ht, dz, *chip_srcs, small)


def _bwd_dx(dz, x, dx1, win, g0, small, sib_small, chip_srcs, tm):
    t, d = x.shape
    nc = win.shape[1]
    n = t // tm
    nm = len(chip_srcs)
    sec_rows = small.shape[1]
    n_pieces = 5

    def body(dz_ref, x_ref, dx1_ref, win_ref, g0_ref, small_ref, sib_small_ref, *rest):
        chip_src = rest[:nm]
        gx_ref = rest[nm]
        chip_out = rest[nm + 1:2 * nm + 1]
        gsmall_ref, parts_ref = rest[2 * nm + 1:2 * nm + 3]
        (pack, pieces, total, chip_send, chip_recv, piece_sems, g_send, g_recv, g_local,
         p_send, p_recv, p_local) = rest[2 * nm + 3:]
        i = pl.program_id(0)
        x_, y_, c_ = _my_pos()
        sec_idx = 4 * x_ + 2 * y_ + c_
        chip_copies = lambda which=None: _chip_copies(chip_src, chip_out, chip_send, chip_recv, which)
        g_start, g_forward, g_finish = _gather_phases([total], [gsmall_ref], [_row_slicer(sec_rows)], g_send, g_recv, g_local)

        @pl.when(i == 0)
        def _():
            pack[...] = jnp.zeros_like(pack)
            for cp in chip_copies():
                cp.start()

        @pl.when(i == min(2, n - 1))
        def _():
            for cp in chip_copies([0]):
                cp.wait_recv()
            srcs = [small_ref.at[sec_idx], sib_small_ref.at[2 * x_ + y_]] + [chip_out[0].at[k] for k in range(3)]
            loads = [pltpu.make_async_copy(src, pieces.at[k], piece_sems.at[k]) for k, src in enumerate(srcs)]
            for cp in loads:
                cp.start()
            for cp in loads:
                cp.wait()
            acc = pieces[0] + pieces[1]
            for k in range(2, n_pieces):
                acc = acc + pieces[k]
            total[...] = acc
            g_start()

        pl.when(i == max(n // 2, min(2, n - 1)))(g_forward)

        xv = x_ref[...]
        r0 = _rsqrt_mean_sq(xv)
        xh = xv * r0
        dh = _dot_nt(dz_ref[...], win_ref[...])
        _store_chunks(pack, 0, _colsum(dh * xh))
        dxh = dh * g0_ref[...]
        gx_ref[...] = dx1_ref[...] + r0 * (dxh - xh * jnp.mean(dxh * xh, axis=-1, keepdims=True))

        @pl.when(i == n - 1)
        def _():
            g_finish()
            for cp in chip_copies([0]):
                cp.wait_send()
            for cp in chip_copies(range(1, nm)):
                cp.wait()
            mine = parts_ref.at[pl.ds(pl.multiple_of(sec_idx * INP_PACK_ROWS, SUB), INP_PACK_ROWS), :]
            local = pltpu.make_async_copy(pack, mine, p_local)
            local.start()
            sends = []
            for k in range(1, N_DEV):
                peer = tuple(1 - v if (k >> bit) & 1 else v for v, bit in ((x_, 2), (y_, 1), (c_, 0)))
                sends.append(pltpu.make_async_remote_copy(
                    src_ref=pack, dst_ref=mine, send_sem=p_send.at[k - 1], recv_sem=p_recv.at[k - 1],
                    device_id=peer, device_id_type=MESH))
            for cp in sends:
                cp.start()
            for cp in sends:
                cp.wait()
            local.wait()

    tile = lambda w: pl.BlockSpec((tm, w), lambda i: (i, 0))
    return pl.pallas_call(
        body, name="bwd_dx", grid=(n,),
        in_specs=[tile(nc), tile(d), tile(d), _const_spec(win.shape), _const_spec(g0.shape)] + [HBM_SPEC] * (nm + 2),
        out_specs=(tile(d),) + (HBM_SPEC,) * (nm + 2),
        out_shape=(jax.ShapeDtypeStruct((t, d), F32),)
        + tuple(jax.ShapeDtypeStruct((3,) + g.shape[1:], g.dtype) for g in chip_srcs)
        + (jax.ShapeDtypeStruct((N_DEV * sec_rows, HEAD), F32), jax.ShapeDtypeStruct((N_DEV * INP_PACK_ROWS, HEAD), F32)),
        scratch_shapes=[pltpu.VMEM((INP_PACK_ROWS, HEAD), F32), pltpu.VMEM((n_pieces, sec_rows, HEAD), F32),
                        pltpu.VMEM((sec_rows, HEAD), F32),
                        pltpu.SemaphoreType.DMA((nm, 3)), pltpu.SemaphoreType.DMA((nm, 3)),
                        pltpu.SemaphoreType.DMA((n_pieces,))] + _gather_sems(1)
        + [pltpu.SemaphoreType.DMA((N_DEV - 1,)), pltpu.SemaphoreType.DMA((N_DEV - 1,)), pltpu.SemaphoreType.DMA],
        compiler_params=pltpu.CompilerParams(dimension_semantics=("arbitrary",), vmem_limit_bytes=VMEM_LIMIT),
    )(dz, x, dx1, win, g0, small, sib_small, *chip_srcs)


ROW_BLOCKS = 8


def _chip_sums(core_idx, grads, recvs, dtypes, name):
    nw = len(grads)

    def body(idx_ref, *refs):
        gs, rs, outs = refs[:nw], refs[nw:2 * nw], refs[2 * nw:]
        for g, r, o in zip(gs, rs, outs):
            o[...] = (g[...] + r[...]).astype(o.dtype)

    def blk(g):
        return (None,) + g.shape[1:]

    return pl.pallas_call(
        body, name=name,
        grid_spec=pltpu.PrefetchScalarGridSpec(
            num_scalar_prefetch=1, grid=(4,),
            in_specs=[pl.BlockSpec(blk(g), lambda q, s: (2 * q + s[0], 0, 0)) for g in grads]
            + [pl.BlockSpec(blk(g), lambda q, s: (q, 0, 0)) for g in grads],
            out_specs=tuple(pl.BlockSpec(blk(g), lambda q, s: (q, 0, 0)) for g in grads)),
        out_shape=tuple(jax.ShapeDtypeStruct((4,) + g.shape[1:], dt) for g, dt in zip(grads, dtypes)),
    )(core_idx, *grads, *recvs)


def _adamw(w, g, m, v):
    m = ADAM_B1 * m + (1.0 - ADAM_B1) * g
    v = ADAM_B2 * v + (1.0 - ADAM_B2) * (g * g)
    m_hat = m / (1.0 - ADAM_B1 ** ADAM_STEP)
    v_hat = v / (1.0 - ADAM_B2 ** ADAM_STEP)
    delta = -ADAM_LR * (m_hat / (jnp.sqrt(v_hat) + ADAM_EPS) + ADAM_WD * w)
    return delta, m, v


def _adam_sections(sec_idx, grads, recv_sib, recv_chips, wmv):
    nw = len(wmv)

    def body(idx_ref, *refs):
        gs, rs, cs = refs[:nw], refs[nw:2 * nw], refs[2 * nw:3 * nw]
        params = refs[3 * nw:6 * nw]
        outs = refs[6 * nw:]
        for w in range(nw):
            g = gs[w][...] + rs[w][...]
            for k in range(3):
                g = g + cs[w][k].astype(F32)
            wv, mv, vv = (params[3 * w + t][...] for t in range(3))
            delta, m_new, v_new = _adamw(wv, g, mv, vv)
            outs[4 * w][...] = g
            outs[4 * w + 1][...] = delta
            outs[4 * w + 2][...] = m_new
            outs[4 * w + 3][...] = v_new

    def rb(g):
        return g.shape[1] // ROW_BLOCKS

    in_specs = [pl.BlockSpec((None, rb(g), g.shape[2]), lambda i, s: (s[0], i, 0)) for g in grads]
    in_specs += [pl.BlockSpec((None, rb(g), g.shape[2]), lambda i, s: (s[1], i, 0)) for g in grads]
    in_specs += [pl.BlockSpec((3, rb(g), g.shape[2]), lambda i, s: (0, i, 0)) for g in grads]
    sec = lambda g: pl.BlockSpec((rb(g), g.shape[2]), lambda i, s: (i, 0))
    for w in range(nw):
        in_specs += [sec(grads[w])] * 3
    out_specs, out_shape = [], []
    for g in grads:
        out_specs += [sec(g)] * 4
        out_shape += [jax.ShapeDtypeStruct(g.shape[1:], F32)] * 4
    flat = [a for t in wmv for a in t]
    outs = pl.pallas_call(
        body, name="adam_sections",
        grid_spec=pltpu.PrefetchScalarGridSpec(num_scalar_prefetch=1, grid=(ROW_BLOCKS,), in_specs=in_specs,
                                               out_specs=tuple(out_specs)),
        out_shape=tuple(out_shape),
    )(sec_idx, *grads, *recv_sib, *recv_chips, *flat)
    return [outs[4 * w:4 * w + 4] for w in range(nw)]


def _adam_small(sec_idx, gpack, parts, parts_wmv, vec_params, mat_params, conv_wmv):
    items = [(0, parts_wmv, "parts")]
    items += [(r0, t, "vec") for r0, t in vec_params] + [(r0, t, "mat") for r0, t in mat_params]
    items.append((_SEQ_ROWS["dcw"][0], conv_wmv, "conv"))

    def body(idx_ref, g_ref, parts_ref, *refs):
        ins, outs = refs[:3 * len(items)], refs[3 * len(items):]
        for n, (r0, _, kind) in enumerate(items):
            w_ref, m_ref, v_ref = ins[3 * n:3 * n + 3]
            o = outs[4 * n:4 * n + 4]
            if kind == "mat":
                g = g_ref[pl.ds(r0, w_ref.shape[0]), :]
                res = (g,) + _adamw(w_ref[...], g, m_ref[...], v_ref[...])
                for ref, val in zip(o, res):
                    ref[...] = val
            elif kind in ("vec", "parts"):
                for h in range(w_ref.shape[1] // HEAD):
                    cols = slice(h * HEAD, (h + 1) * HEAD)
                    if kind == "vec":
                        g = g_ref[pl.ds(r0 + h, 1), :]
                    else:
                        g = parts_ref[pl.ds(h, 1), :]
                        for dev in range(1, N_DEV):
                            g = g + parts_ref[pl.ds(SUB * dev + h, 1), :]
                    res = (g,) + _adamw(w_ref[:, cols], g, m_ref[:, cols], v_ref[:, cols])
                    for ref, val in zip(o, res):
                        ref[:, cols] = val
            else:
                rows = lax.broadcasted_iota(jnp.int32, (SUB, HEAD), 0)
                for k in range(4):
                    blk = g_ref[pl.ds(r0 + SUB * k, SUB), :]
                    g = jnp.sum(jnp.where(rows == idx_ref[0], blk, 0.0), axis=0, keepdims=True)
                    row = pl.ds(k, 1)
                    res = (g,) + _adamw(w_ref[row, :], g, m_ref[row, :], v_ref[row, :])
                    for ref, val in zip(o, res):
                        ref[row, :] = val
        outs[4 * len(items)][...] = g_ref[pl.ds(_OFF_MID + MID_LOSS_ROW, 1), :]

    flat = [a for _, t, _ in items for a in t]
    full = lambda a: pl.BlockSpec(a.shape, lambda i, s: (0,) * a.ndim)
    out_specs, out_shape = [], []
    for _, t, _ in items:
        out_specs += [full(t[0])] * 4
        out_shape += [jax.ShapeDtypeStruct(t[0].shape, F32)] * 4
    loss_row = jax.ShapeDtypeStruct((1, HEAD), F32)
    out_specs.append(full(loss_row))
    out_shape.append(loss_row)
    outs = pl.pallas_call(
        body, name="adam_small",
        grid_spec=pltpu.PrefetchScalarGridSpec(num_scalar_prefetch=1, grid=(1,),
                                               in_specs=[full(gpack), full(parts)] + [full(a) for a in flat],
                                               out_specs=tuple(out_specs)),
        out_shape=tuple(out_shape),
    )(sec_idx, gpack, parts, *flat)
    return [outs[4 * n:4 * n + 4] for n in range(len(items))], outs[4 * len(items)]


def _col_slicer(width):
    return lambda ref, idx: ref.at[:, pl.ds(pl.multiple_of(idx * width, HEAD), width)]


def _row_slicer(rows):
    return lambda ref, idx: ref.at[pl.ds(pl.multiple_of(idx * rows, SUB), rows), :]


def kernel(x, p, norm_g, w_in, conv_w, conv_b, lru_w_a, lru_b_a, lru_w_x, lru_b_x, lru_lambda, pool_w, pool_scale, w_proj_lru, w_proj_pool, w_out, ple_norm_g, w_ple_gate, w_ple_proj, final_g, loss_target, m_norm_g, m_w_in, m_conv_w, m_conv_b, m_lru_w_a, m_lru_b_a, m_lru_w_x, m_lru_b_x, m_lru_lambda, m_pool_w, m_pool_scale, m_w_proj_lru, m_w_proj_pool, m_w_out, m_ple_norm_g, m_w_ple_gate, m_w_ple_proj, m_final_g, v_norm_g, v_w_in, v_conv_w, v_conv_b, v_lru_w_a, v_lru_b_a, v_lru_w_x, v_lru_b_x, v_lru_lambda, v_pool_w, v_pool_scale, v_w_proj_lru, v_w_proj_pool, v_w_out, v_ple_norm_g, v_w_ple_gate, v_w_ple_proj, v_final_g):
    nb, s, d = x.shape
    t = nb * s
    tm = min(TILE_M, s)
    sec_idx = (4 * lax.axis_index("x") + 2 * lax.axis_index("y") + lax.axis_index("c")).astype(jnp.int32)
    chip_idx = (2 * lax.axis_index("x") + lax.axis_index("y")).astype(jnp.int32)
    core_idx = lax.axis_index("c").astype(jnp.int32)

    h, ht, win, cw = _pre(
        x, norm_g, [w_in[0], conv_w[0]], [True, False],
        [(d, N_DEV * w_in.shape[2]), (conv_w.shape[1], N_DEV * conv_w.shape[2])],
        [_col_slicer(w_in.shape[2]), _col_slicer(conv_w.shape[2])], tm)
    late_shards = [w_proj_lru[0], w_proj_pool[0], w_out[0], w_ple_gate[0], w_ple_proj[0]]
    late_shapes = [(N_DEV * w_proj_lru.shape[1], d), (w_proj_pool.shape[1], N_DEV * w_proj_pool.shape[2]),
                   (N_DEV * w_out.shape[1], d), (N_DEV * w_ple_gate.shape[1], d),
                   (w_ple_proj.shape[1], N_DEV * w_ple_proj.shape[2])]
    late_slicers = [_row_slicer(w_proj_lru.shape[1]), _col_slicer(w_proj_pool.shape[2]), _row_slicer(w_out.shape[1]),
                    _row_slicer(w_ple_gate.shape[1]), _col_slicer(w_ple_proj.shape[2])]

    lw = cw.shape[1]
    pwid = pool_scale.shape[1]
    wa = lru_w_a[0].astype(_MM)
    wx = lru_w_x[0].astype(_MM)
    pw = pool_w[0].astype(_MM)
    ba, bx = lru_b_a[0], lru_b_x[0]
    gf = final_g.reshape(1, d)
    rows_of = lambda g: g.reshape(N_DEV, g.shape[0] // N_DEV, g.shape[1])
    core = core_idx.reshape(1)

    fwd_out = _fwd_seq(
        h, win, cw, conv_b, wa, ba, wx, bx, lru_lambda, pw, pool_scale, late_shards, late_shapes, late_slicers, tm)
    z, hl, ya, yb = fwd_out[:4]
    saved = fwd_out[4:N_SEQ_SAVED]
    wpl, wpp, wout, wpg, wpe = fwd_out[N_SEQ_SAVED:]
    x2d = x.reshape(t, d)
    (dya, dyb, dm, dx1, g_pl, g_pp, g_out, g_pg, g_pe, pack_mid) = _mid(
        ya.reshape(t, lw), yb.reshape(t, pwid), h.reshape(t, d), win, x2d, p[0].reshape(t, -1),
        loss_target.reshape(t, d), wpl, wpp, wout, wpg, wpe, ple_norm_g, gf, tm)
    grads_mid = [rows_of(g_pl), g_pp, rows_of(g_out), rows_of(g_pg), g_pe]
    seq_out = _bwd_seq(z, hl, saved, dya.reshape(nb, s, lw), dyb.reshape(nb, s, pwid), dm.reshape(nb, s, -1), cw,
                       wa, wx, lru_lambda, pw, pool_scale, pack_mid, grads_mid, tm)
    dz, sib_mid = seq_out[0], list(seq_out[2:])
    small = seq_out[1].reshape(N_DEV, SMALL_SEC, HEAD)
    sums_mid = _chip_sums(core, grads_mid, sib_mid, [_WIRE] * len(grads_mid), "chip_sums_mid")
    win_out = _bwd_win(ht, dz, sums_mid, small)
    g_in, sib_in, chips_mid, sib_small = win_out[0], win_out[1], list(win_out[2:-1]), win_out[-1]
    sums_small, sums_in = _chip_sums(core, [small, g_in], [sib_small, sib_in], [F32, _WIRE], "chip_sums_in")
    grad_x, _, chips_in, gsmall, g0_parts = _bwd_dx(
        dz.reshape(t, -1), x2d, dx1, win, norm_g, small, sib_small, [sums_small, sums_in], tm)

    grads = [g_in] + grads_mid
    recv_sib = [sib_in] + sib_mid
    recv_chips = [chips_in] + chips_mid
    wmv = [(w_in[0], m_w_in[0], v_w_in[0]), (w_proj_lru[0], m_w_proj_lru[0], v_w_proj_lru[0]),
           (w_proj_pool[0], m_w_proj_pool[0], v_w_proj_pool[0]), (w_out[0], m_w_out[0], v_w_out[0]),
           (w_ple_gate[0], m_w_ple_gate[0], v_w_ple_gate[0]), (w_ple_proj[0], m_w_ple_proj[0], v_w_ple_proj[0])]
    big = _adam_sections(jnp.stack([sec_idx, chip_idx]), grads, recv_sib, recv_chips, wmv)

    R = _SEQ_ROWS
    vec = lambda r0, *t: (r0, tuple(a.reshape(1, -1) for a in t))
    mat = lambda r0, *t: (r0, tuple(a.reshape(-1, HEAD) for a in t))
    norm_wmv = vec(0, norm_g, m_norm_g, v_norm_g)[1]
    vec_params = [vec(R["dcb"][0], conv_b, m_conv_b, v_conv_b),
                  vec(R["dlam"][0], lru_lambda, m_lru_lambda, v_lru_lambda),
                  vec(R["dps"][0], pool_scale, m_pool_scale, v_pool_scale),
                  vec(_OFF_MID, ple_norm_g, m_ple_norm_g, v_ple_norm_g),
                  vec(_OFF_MID + SUB, final_g, m_final_g, v_final_g)]
    mat_params = [mat(R["dwa"][0], lru_w_a, m_lru_w_a, v_lru_w_a), mat(R["dba"][0], lru_b_a, m_lru_b_a, v_lru_b_a),
                  mat(R["dwx"][0], lru_w_x, m_lru_w_x, v_lru_w_x), mat(R["dbx"][0], lru_b_x, m_lru_b_x, v_lru_b_x),
                  mat(R["dpw"][0], pool_w, m_pool_w, v_pool_w)]
    conv_wmv = (conv_w[0], m_conv_w[0], v_conv_w[0])
    small_out, loss_row = _adam_small(sec_idx.reshape(1), gsmall, g0_parts, norm_wmv, vec_params, mat_params, conv_wmv)
    loss = loss_row[0, 0]

    res = {}
    names_small = ["norm_g", "conv_b", "lru_lambda", "pool_scale", "ple_norm_g", "final_g",
                   "lru_w_a", "lru_b_a", "lru_w_x", "lru_b_x", "pool_w", "conv_w"]
    shapes = {"norm_g": norm_g, "conv_b": conv_b, "lru_lambda": lru_lambda, "pool_scale": pool_scale,
              "ple_norm_g": ple_norm_g, "final_g": final_g, "lru_w_a": lru_w_a, "lru_b_a": lru_b_a, "lru_w_x": lru_w_x,
              "lru_b_x": lru_b_x, "pool_w": pool_w, "conv_w": conv_w}
    for name, quad in zip(names_small, small_out):
        res[name] = [a.reshape(shapes[name].shape) for a in quad]
    names_big = ["w_in", "w_proj_lru", "w_proj_pool", "w_out", "w_ple_gate", "w_ple_proj"]
    for name, quad, (w, _, _) in zip(names_big, big, wmv):
        res[name] = [a.reshape((1,) + w.shape) for a in quad]
    order = ["norm_g", "w_in", "conv_w", "conv_b", "lru_w_a", "lru_b_a", "lru_w_x", "lru_b_x", "lru_lambda", "pool_w",
             "pool_scale", "w_proj_lru", "w_proj_pool", "w_out", "ple_norm_g", "w_ple_gate", "w_ple_proj", "final_g"]
    out = [loss, grad_x.reshape(nb, s, d)]
    for kind in range(4):
        out += [res[name][kind] for name in order]
    return tuple(out)
```

```python
import jax
import jax.numpy as jnp
from jax import lax
from jax.experimental import pallas as pl
from jax.experimental.pallas import tpu as pltpu

F32 = jnp.float32
_MM = jnp.bfloat16
_WIRE = jnp.bfloat16

EPS = 1e-6
LRU_C = 8.0
MULT_SQ_FLOOR = 1e-30
POOL_WINDOWS = (2, 4, 8, 16)
N_HEADS = 8
HEAD = 128
HALO = 16
SUB = 8

ADAM_LR = 0.001
ADAM_B1 = 0.9
ADAM_B2 = 0.999
ADAM_EPS = 1e-08
ADAM_WD = 0.01
ADAM_STEP = 10

N_DEV = 8
MESH = pl.DeviceIdType.MESH
VMEM_LIMIT = 60 * 1024 * 1024
TILE_M = 256
N_SEQ_SAVED = 10

_SEQ_ROWS = {"dwa": (0, 1024), "dwx": (1024, 1024), "dpw": (2048, 512), "dba": (2560, 8), "dbx": (2568, 8),
             "dcb": (2576, 8), "dlam": (2584, 8), "dcw": (2592, 32), "dps": (2624, 8)}
SEQ_PACK_ROWS = 2632
MID_PACK_ROWS = 24
MID_LOSS_ROW = 16
INP_PACK_ROWS = 8
SMALL_ROWS = 3072
SMALL_SEC = SMALL_ROWS // N_DEV
_OFF_MID = SEQ_PACK_ROWS


def _dot(a, b):
    return jnp.dot(a, b, preferred_element_type=F32)


def _dot_nt(a, b):
    return lax.dot_general(a, b, (((1,), (1,)), ((), ())), preferred_element_type=F32)


def _dot_tn(a, b):
    return lax.dot_general(a, b, (((0,), (0,)), ((), ())), preferred_element_type=F32)


def _sigmoid(v):
    return 0.5 * jnp.tanh(0.5 * v) + 0.5


def _lru_mult(log_a, a):
    m2 = jnp.maximum(-jnp.tanh(log_a) * (1.0 + a * a), MULT_SQ_FLOOR)
    inv = lax.rsqrt(m2)
    return m2 * inv, inv


def _rows_back(v, n):
    return pltpu.roll(v, n, 0) if n else v


def _rows_ahead(v, n):
    return pltpu.roll(v, v.shape[0] - n, 0) if n else v


def _softplus_neg(lam):
    e = jnp.exp(-jnp.abs(lam))
    w = 1.0 + e
    l1p = jnp.where(w == 1.0, e, jnp.log(w) * (e / (w - 1.0)))
    return jnp.maximum(-lam, 0.0) + l1p


def _rsqrt_mean_sq(v):
    return lax.rsqrt(jnp.mean(v * v, axis=-1, keepdims=True) + EPS)


def _colsum(v):
    return jnp.sum(v, axis=0, keepdims=True)


def _const_spec(shape):
    nd = len(shape)
    return pl.BlockSpec(shape, lambda *_: (0,) * nd, pipeline_mode=pl.Buffered(1))


HBM_SPEC = pl.BlockSpec(memory_space=pl.ANY)


def _my_pos():
    return lax.axis_index("x"), lax.axis_index("y"), lax.axis_index("c")


def _other_chips(x, y):
    return [(1 - x, y), (x, 1 - y), (1 - x, 1 - y)]


def _gather_sems(nw):
    return [pltpu.SemaphoreType.DMA((nw, 7)), pltpu.SemaphoreType.DMA((nw, 7)), pltpu.SemaphoreType.DMA((nw,))]


def _gather_phases(srcs, outs, slicers, send_sems, recv_sems, local_sems):
    nw = len(srcs)
    x, y, c = _my_pos()
    me, sibling = (x, y, c), (x, y, 1 - c)
    chips = _other_chips(x, y)

    def part(w, pos):
        return slicers[w](outs[w], 4 * pos[0] + 2 * pos[1] + pos[2])

    def copy(w, k, block, to, src=None):
        return pltpu.make_async_remote_copy(
            src_ref=part(w, block) if src is None else src, dst_ref=part(w, block),
            send_sem=send_sems.at[w, k], recv_sem=recv_sems.at[w, k], device_id=to, device_id_type=MESH)

    def mine():
        return [pltpu.make_async_copy(srcs[w], part(w, me), local_sems.at[w]) for w in range(nw)]

    def first():
        out = []
        for w in range(nw):
            out.append(copy(w, 0, me, sibling, src=srcs[w]))
            out += [copy(w, 1 + j, me, (*chip, c), src=srcs[w]) for j, chip in enumerate(chips)]
        return out

    def passed():
        return [copy(w, 4 + j, (*chip, c), sibling) for j, chip in enumerate(chips) for w in range(nw)]

    def start():
        for cp in mine() + first():
            cp.start()

    def forward():
        for j, chip in enumerate(chips):
            for w in range(nw):
                copy(w, 1 + j, (*chip, c), me).wait_recv()
                copy(w, 4 + j, (*chip, c), sibling).start()

    def finish():
        for w in range(nw):
            copy(w, 0, sibling, me).wait_recv()
            for j, chip in enumerate(chips):
                copy(w, 4 + j, (*chip, 1 - c), me).wait_recv()
        for cp in first() + passed():
            cp.wait_send()
        for cp in mine():
            cp.wait()

    return start, forward, finish


def _sibling_copies(srcs, outs, send_sems, recv_sems):
    x, y, c = _my_pos()
    return [pltpu.make_async_remote_copy(
        src_ref=srcs[w].at[2 * q + (1 - c)], dst_ref=outs[w].at[q], send_sem=send_sems.at[w, q],
        recv_sem=recv_sems.at[w, q], device_id=(x, y, 1 - c), device_id_type=MESH)
        for w in range(len(srcs)) for q in range(4)]


def _chip_copies(srcs, outs, send_sems, recv_sems, which=None):
    x, y, c = _my_pos()
    return [pltpu.make_async_remote_copy(
        src_ref=srcs[w].at[2 * px + py], dst_ref=outs[w].at[k], send_sem=send_sems.at[w, k],
        recv_sem=recv_sems.at[w, k], device_id=(px, py, c), device_id_type=MESH)
        for w in (range(len(srcs)) if which is None else which) for k, (px, py) in enumerate(_other_chips(x, y))]


def _lru_gates(xc, h, wa_ref, ba_ref, wx_ref, bx_ref, neg_c_sp):
    xc16 = xc.astype(_MM)
    r = _sigmoid(_dot(xc16, wa_ref[h]) + ba_ref[pl.ds(h, 1), :])
    i = _sigmoid(_dot(xc16, wx_ref[h]) + bx_ref[pl.ds(h, 1), :])
    return r, i, neg_c_sp * r


def _conv_head(xa_ext, cw_ref, cb_ref, cols, tm):
    ext = xa_ext[:, cols]
    xc = cb_ref[:, cols] + cw_ref[pl.ds(3, 1), cols] * ext[HALO:HALO + tm]
    for k in range(3):
        xc = xc + cw_ref[pl.ds(k, 1), cols] * _rows_back(ext, 3 - k)[HALO:HALO + tm]
    return xc


def _window_sum(ext, k, shift):
    n = 1
    while n < k:
        ext = ext + shift(ext, n)
        n *= 2
    return ext


def _pool_diff(xb_ext, g, k, pos, tm):
    cols = slice(g * HEAD, (g + 1) * HEAD)
    ext = xb_ext[:, cols]
    ws = _window_sum(ext, k, _rows_back)[HALO:HALO + tm]
    inv_cnt = 1.0 / jnp.minimum(pos + 1, k).astype(F32)
    return ws * inv_cnt - ext[HALO:HALO + tm], inv_cnt


def _staged_sources(shards, as_operand):
    in_specs = [_const_spec(a.shape) if c else HBM_SPEC for a, c in zip(shards, as_operand)]
    stages = [pltpu.VMEM(a.shape, _MM) for a, c in zip(shards, as_operand) if c]
    dtypes = [_MM if c else a.dtype for a, c in zip(shards, as_operand)]
    return in_specs, stages, dtypes


def _gather_sources(srcs, stages, as_operand):
    stages = iter(stages)
    return [next(stages) if c else src for src, c in zip(srcs, as_operand)]


def _fill_stages(srcs, staged, as_operand):
    for src, dst, c in zip(srcs, staged, as_operand):
        if c:
            dst[...] = src[...].astype(dst.dtype)


def _pre(x, g0, shards, as_operand, shapes, slicers, tm):
    nb, s, d = x.shape
    nt = s // tm
    nl = len(shards)
    shard_specs, stage_shapes, wire = _staged_sources(shards, as_operand)

    def body(x_ref, g0_ref, *rest):
        src, rest = rest[:nl], rest[nl:]
        h_ref, ht_ref = rest[:2]
        out, rest = rest[2:2 + nl], rest[2 + nl:]
        (send_sems, recv_sems, local_sems), stages = rest[:3], rest[3:]
        step_no = pl.program_id(0) * nt + pl.program_id(1)
        staged = _gather_sources(src, stages, as_operand)
        g_start, g_forward, g_finish = _gather_phases(staged, out, slicers, send_sems, recv_sems, local_sems)

        @pl.when(step_no == 0)
        def _():
            _fill_stages(src, staged, as_operand)
            g_start()

        xv = x_ref[...]
        h16 = ((xv * _rsqrt_mean_sq(xv)) * g0_ref[...]).astype(_MM)
        h_ref[...] = h16
        ht_ref[...] = h16.T

        @pl.when(step_no == nb * nt - 1)
        def _():
            g_forward()
            g_finish()

    tile = pl.BlockSpec((None, tm, d), lambda b, j: (b, j, 0))
    return pl.pallas_call(
        body, name="pre", grid=(nb, nt),
        in_specs=[tile, _const_spec(g0.shape)] + shard_specs,
        out_specs=(tile, pl.BlockSpec((None, d, tm), lambda b, j: (b, 0, j))) + (HBM_SPEC,) * nl,
        out_shape=(jax.ShapeDtypeStruct((nb, s, d), _MM), jax.ShapeDtypeStruct((nb, d, s), _MM))
        + tuple(jax.ShapeDtypeStruct(shp, dt) for shp, dt in zip(shapes, wire)),
        scratch_shapes=_gather_sems(nl) + stage_shapes,
        compiler_params=pltpu.CompilerParams(dimension_semantics=("arbitrary", "arbitrary")),
    )(x, g0, *shards)


def _fwd_seq(h, win, cw, cb, wa, ba, wx, bx, lam, pw, ps, late_shards, late_shapes, late_slicers, tm):
    nb, s, d = h.shape
    lw = cw.shape[1]
    pwid = ps.shape[1]
    nc = 2 * lw + 2 * pwid
    nt = s // tm
    nl = len(late_shards)
    n_steps = nb * nt
    as_operand = [True] * nl
    shard_specs, stage_shapes, wire = _staged_sources(late_shards, as_operand)

    def body(h_ref, win_ref, cw_ref, cb_ref, wa_ref, ba_ref, wx_ref, bx_ref, lam_ref, pw_ref, ps_ref, *rest):
        late_src, rest = rest[:nl], rest[nl:]
        z_ref, hl_ref, ya_ref, yb_ref, xc_ref, r_ref, i_ref, a_buf, diff_ref, yp_ref = rest[:N_SEQ_SAVED]
        late_out, rest = rest[N_SEQ_SAVED:N_SEQ_SAVED + nl], rest[N_SEQ_SAVED + nl:]
        (xa_ext, xb_ext, carry, send_sems, recv_sems, local_sems), stages = rest[:6], rest[6:]
        j = pl.program_id(1)
        step_no = pl.program_id(0) * nt + j
        staged = _gather_sources(late_src, stages, as_operand)
        g_start, g_forward, g_finish = _gather_phases(staged, late_out, late_slicers, send_sems, recv_sems, local_sems)

        @pl.when(step_no == 0)
        def _():
            _fill_stages(late_src, staged, as_operand)
            g_start()

        pl.when(step_no == n_steps // 2)(g_forward)

        @pl.when(j == 0)
        def _():
            xa_ext[pl.ds(0, HALO), :] = jnp.zeros((HALO, lw), F32)
            xb_ext[pl.ds(0, HALO), :] = jnp.zeros((HALO, pwid), F32)
            carry[...] = jnp.zeros_like(carry)

        h16 = h_ref[...]
        xa = _dot(h16, win_ref[:, 0:lw])
        z_ref[:, 0:lw] = xa
        xa_ext[pl.ds(HALO, tm), :] = xa
        for c0 in range(lw, nc, lw):
            z_ref[:, c0:c0 + lw] = _dot(h16, win_ref[:, c0:c0 + lw])

        for h in range(N_HEADS):
            cols = slice(h * HEAD, (h + 1) * HEAD)
            xc = _conv_head(xa_ext, cw_ref, cb_ref, cols, tm)
            neg_c_sp = -LRU_C * _softplus_neg(lam_ref[:, cols])
            r, i, log_a = _lru_gates(xc, h, wa_ref, ba_ref, wx_ref, bx_ref, neg_c_sp)
            a = jnp.exp(log_a)
            xc_ref[:, cols] = xc
            r_ref[:, cols] = r
            i_ref[:, cols] = i
            a_buf[:, cols] = a
            hl_ref[:, cols] = _lru_mult(log_a, a)[0] * (i * xc)

        rows = lax.broadcasted_iota(jnp.int32, (SUB, lw), 0)

        def step(c, car):
            i0 = pl.multiple_of(c * SUB, SUB)
            av = a_buf[pl.ds(i0, SUB), :]
            bv = hl_ref[pl.ds(i0, SUB), :]
            for sh in (1, 2, 4):
                m = rows >= sh
                a_sh = jnp.where(m, pltpu.roll(av, sh, 0), 1.0)
                b_sh = jnp.where(m, pltpu.roll(bv, sh, 0), 0.0)
                bv = av * b_sh + bv
                av = av * a_sh
            hv = av * car + bv
            hl_ref[pl.ds(i0, SUB), :] = hv
            return jnp.broadcast_to(hv[SUB - 1:SUB, :], (SUB, lw))

        carry[...] = lax.fori_loop(0, tm // SUB, step, carry[...])

        ga = z_ref[:, lw:2 * lw]
        ya_ref[...] = (hl_ref[...] * (ga * _sigmoid(ga))).astype(ya_ref.dtype)

        xb_ext[pl.ds(HALO, tm), :] = z_ref[:, 2 * lw:2 * lw + pwid]
        pos = j * tm + lax.broadcasted_iota(jnp.int32, (tm, HEAD), 0)
        for g, k in enumerate(POOL_WINDOWS):
            cols = slice(g * HEAD, (g + 1) * HEAD)
            diff16 = _pool_diff(xb_ext, g, k, pos, tm)[0].astype(_MM)
            yp = _dot(diff16, pw_ref[g])
            diff_ref[:, cols] = diff16
            yp_ref[:, cols] = yp
            gb = z_ref[:, 2 * lw + pwid + g * HEAD:2 * lw + pwid + (g + 1) * HEAD]
            yb_ref[:, cols] = ((yp * ps_ref[:, cols]) * (gb * _sigmoid(gb))).astype(yb_ref.dtype)

        xa_ext[pl.ds(0, HALO), :] = xa_ext[pl.ds(tm, HALO), :]
        xb_ext[pl.ds(0, HALO), :] = xb_ext[pl.ds(tm, HALO), :]
        pl.when(step_no == n_steps - 1)(g_finish)

    tile = lambda w: pl.BlockSpec((None, tm, w), lambda b, j: (b, j, 0))
    return pl.pallas_call(
        body, name="fwd_seq", grid=(nb, nt),
        in_specs=[tile(d), _const_spec((d, nc)), _const_spec(cw.shape), _const_spec(cb.shape),
                  _const_spec(wa.shape), _const_spec(ba.shape), _const_spec(wx.shape), _const_spec(bx.shape),
                  _const_spec(lam.shape), _const_spec(pw.shape), _const_spec(ps.shape)] + shard_specs,
        out_specs=(tile(nc), tile(lw), tile(lw), tile(pwid), tile(lw), tile(lw), tile(lw), tile(lw), tile(pwid),
                   tile(pwid)) + (HBM_SPEC,) * nl,
        out_shape=(jax.ShapeDtypeStruct((nb, s, nc), F32), jax.ShapeDtypeStruct((nb, s, lw), F32),
                   jax.ShapeDtypeStruct((nb, s, lw), _MM), jax.ShapeDtypeStruct((nb, s, pwid), _MM))
        + (jax.ShapeDtypeStruct((nb, s, lw), F32),) * 4
        + (jax.ShapeDtypeStruct((nb, s, pwid), _MM), jax.ShapeDtypeStruct((nb, s, pwid), F32))
        + tuple(jax.ShapeDtypeStruct(shp, dt) for shp, dt in zip(late_shapes, wire)),
        scratch_shapes=[pltpu.VMEM((tm + HALO, lw), F32), pltpu.VMEM((tm + HALO, pwid), F32),
                        pltpu.VMEM((SUB, lw), F32)] + _gather_sems(nl) + stage_shapes,
        compiler_params=pltpu.CompilerParams(dimension_semantics=("arbitrary", "arbitrary"), vmem_limit_bytes=VMEM_LIMIT),
    )(h, win, cw, cb, wa, ba, wx, bx, lam, pw, ps, *late_shards)


def _store_chunks(ref, row0, vec):
    for h in range(vec.shape[1] // HEAD):
        ref[pl.ds(row0 + h, 1), :] += vec[:, h * HEAD:(h + 1) * HEAD]


def _mid(ya, yb, h, win, x, p, tgt, wpl, wpp, wout, wpg, wpe, g1, gf, tm):
    t, d = x.shape
    lw = ya.shape[1]
    pwid = yb.shape[1]
    pdim = p.shape[1]
    n = t // tm
    sec_pp = wpp.shape[1] // N_DEV
    sec_pe = wpe.shape[1] // N_DEV
    n_win_blocks = win.shape[1] // d
    win_cols = lambda c: pl.BlockSpec((d, d), lambda i: (0, c), pipeline_mode=pl.Buffered(1))

    def body(ya_ref, yb_ref, h_ref, wma_ref, wmb_ref, x_ref, p_ref, tgt_ref, wpl_ref, wpp_ref, wout_ref, wpg_ref, wpe_ref,
             g1_ref, gf_ref,
             dya_ref, dyb_ref, dm_ref, dx1_ref, gpl_ref, gpp_ref, gout_ref, gpg_ref, gpe_ref, pack_ref,
             acc_pl, acc_pp, acc_out, acc_pg, acc_pe):
        i = pl.program_id(0)

        @pl.when(i == 0)
        def _():
            for acc in (acc_pl, acc_pp, acc_out, acc_pg, acc_pe):
                acc[...] = jnp.zeros_like(acc)
            pack_ref[...] = jnp.zeros_like(pack_ref)

        ya16 = ya_ref[...]
        yb16 = yb_ref[...]
        br_a = _dot(ya16, wpl_ref[...])
        br_b = _dot(yb16, wpp_ref[...])
        h16 = h_ref[...]
        sa = _sigmoid(_dot(h16, wma_ref[...]))
        sb = _sigmoid(_dot(h16, wmb_ref[...]))
        mg16 = (sa * br_a + sb * br_b).astype(_MM)
        x1 = x_ref[...] + _dot(mg16, wout_ref[...])
        r1 = _rsqrt_mean_sq(x1)
        n1 = x1 * r1
        h116 = (n1 * g1_ref[...]).astype(_MM)
        gate = _sigmoid(_dot(h116, wpg_ref[...]))
        p16 = p_ref[...].astype(_MM)
        pe = _dot(p16, wpe_ref[...])
        x2 = x1 + gate * pe
        r2 = _rsqrt_mean_sq(x2)
        n2 = x2 * r2
        err = n2 * gf_ref[...] - tgt_ref[...]
        sq = jnp.sum(_colsum(err * err), axis=1, keepdims=True)
        pack_ref[pl.ds(MID_LOSS_ROW, 1), :] += jnp.broadcast_to(sq * (0.5 / d), (1, HEAD))

        dy = err * (1.0 / d)
        _store_chunks(pack_ref, SUB, _colsum(dy * n2))
        dn2 = dy * gf_ref[...]
        dx2 = r2 * (dn2 - n2 * jnp.mean(dn2 * n2, axis=-1, keepdims=True))
        dpe16 = (dx2 * gate).astype(_MM)
        dpg16 = ((dx2 * pe) * (gate * (1.0 - gate))).astype(_MM)
        acc_pe[...] += _dot_tn(p16, dpe16)
        acc_pg[...] += _dot_tn(h116, dpg16)
        dh1 = _dot_nt(dpg16, wpg_ref[...])
        _store_chunks(pack_ref, 0, _colsum(dh1 * n1))
        dn1 = dh1 * g1_ref[...]
        dx1 = dx2 + r1 * (dn1 - n1 * jnp.mean(dn1 * n1, axis=-1, keepdims=True))
        dx1_ref[...] = dx1
        dx116 = dx1.astype(_MM)
        acc_out[...] += _dot_tn(mg16, dx116)
        dmg = _dot_nt(dx116, wout_ref[...])
        da16 = (dmg * sa).astype(_MM)
        db16 = (dmg * sb).astype(_MM)
        dm_ref[:, 0:d] = ((dmg * br_a) * (sa * (1.0 - sa))).astype(dm_ref.dtype)
        dm_ref[:, d:2 * d] = ((dmg * br_b) * (sb * (1.0 - sb))).astype(dm_ref.dtype)
        acc_pl[...] += _dot_tn(ya16, da16)
        acc_pp[...] += _dot_tn(yb16, db16)
        dya_ref[...] = _dot_nt(da16, wpl_ref[...])
        dyb_ref[...] = _dot_nt(db16, wpp_ref[...])

        @pl.when(i == n - 1)
        def _():
            pltpu.sync_copy(acc_pl, gpl_ref)
            pltpu.sync_copy(acc_out, gout_ref)
            pltpu.sync_copy(acc_pg, gpg_ref)
            for k in range(N_DEV):
                pltpu.sync_copy(acc_pp.at[:, pl.ds(k * sec_pp, sec_pp)], gpp_ref.at[k])
                pltpu.sync_copy(acc_pe.at[:, pl.ds(k * sec_pe, sec_pe)], gpe_ref.at[k])

    tile = lambda w: pl.BlockSpec((tm, w), lambda i: (i, 0))
    any_spec = pl.BlockSpec(memory_space=pl.ANY)
    full = lambda shape: pl.BlockSpec(shape, lambda i: (0,) * len(shape))
    return pl.pallas_call(
        body, name="mid", grid=(n,),
        in_specs=[tile(lw), tile(pwid), tile(d), win_cols(n_win_blocks - 2), win_cols(n_win_blocks - 1),
                  tile(d), tile(pdim), tile(d),
                  _const_spec(wpl.shape), _const_spec(wpp.shape), _const_spec(wout.shape), _const_spec(wpg.shape),
                  _const_spec(wpe.shape), _const_spec(g1.shape), _const_spec(gf.shape)],
        out_specs=(tile(lw), tile(pwid), tile(2 * d), tile(d), any_spec, any_spec, any_spec, any_spec, any_spec,
                   full((MID_PACK_ROWS, HEAD))),
        out_shape=(jax.ShapeDtypeStruct((t, lw), F32), jax.ShapeDtypeStruct((t, pwid), F32),
                   jax.ShapeDtypeStruct((t, 2 * d), _MM), jax.ShapeDtypeStruct((t, d), F32),
                   jax.ShapeDtypeStruct(wpl.shape, F32), jax.ShapeDtypeStruct((N_DEV, wpp.shape[0], sec_pp), F32),
                   jax.ShapeDtypeStruct(wout.shape, F32), jax.ShapeDtypeStruct(wpg.shape, F32),
                   jax.ShapeDtypeStruct((N_DEV, wpe.shape[0], sec_pe), F32),
                   jax.ShapeDtypeStruct((MID_PACK_ROWS, HEAD), F32)),
        scratch_shapes=[pltpu.VMEM(wpl.shape, F32), pltpu.VMEM(wpp.shape, F32), pltpu.VMEM(wout.shape, F32),
                        pltpu.VMEM(wpg.shape, F32), pltpu.VMEM(wpe.shape, F32)],
        compiler_params=pltpu.CompilerParams(dimension_semantics=("arbitrary",), vmem_limit_bytes=VMEM_LIMIT),
    )(ya, yb, h, win, win, x, p, tgt, wpl, wpp, wout, wpg, wpe, g1, gf)


def _bwd_seq(z, hl, saved, dya, dyb, dm, cw, wa, wx, lam, pw, ps, pack_mid, sib_grads, tm):
    nb, s, zw = z.shape
    lw = cw.shape[1]
    pwid = ps.shape[1]
    nt = s // tm
    nc = zw + dm.shape[2]
    R = _SEQ_ROWS
    ns = len(sib_grads)

    def body(zq_ref, hl_ref, hlh_ref, xc_buf, r_buf, i_buf, a_ref, diff_ref, yp_ref, dya_ref, dyb_ref, dm_ref,
             cw_ref, wa_ref, wx_ref, lam_ref, pw_ref, ps_ref, pack_mid_ref, *rest):
        sib_src, rest = rest[:ns], rest[ns:]
        dz_ref, pack_ref = rest[:2]
        sib_out, rest = rest[2:2 + ns], rest[2 + ns:]
        a_ext, an_buf, hl_ext, dh_buf, dxc_ext, q_ext, carry, send_sems, recv_sems = rest
        b = pl.program_id(0)
        j = pl.program_id(1)
        jr = nt - 1 - j
        has_prev = jr > 0

        @pl.when((b == 0) & (j == 0))
        def _():
            for cp in _sibling_copies(sib_src, sib_out, send_sems, recv_sems):
                cp.start()

        dz_ref[:, zw:nc] = dm_ref[...]

        @pl.when((b == 0) & (j == 0))
        def _():
            pack_ref[...] = jnp.zeros_like(pack_ref)

        @pl.when(j == 0)
        def _():
            a_ext[pl.ds(tm, SUB), :] = jnp.zeros((SUB, lw), F32)
            dxc_ext[pl.ds(tm, HALO), :] = jnp.zeros((HALO, lw), F32)
            q_ext[pl.ds(tm, HALO), :] = jnp.zeros((HALO, pwid), F32)
            carry[...] = jnp.zeros_like(carry)

        hl_ext[pl.ds(0, SUB), :] = jnp.where(has_prev, hlh_ref[...], 0.0)
        hl_ext[pl.ds(SUB, tm), :] = hl_ref[...]
        a_ext[pl.ds(0, tm), :] = a_ref[...]
        for h in range(N_HEADS):
            cols = slice(h * HEAD, (h + 1) * HEAD)
            ga = zq_ref[:, lw + h * HEAD:lw + (h + 1) * HEAD]
            sg = _sigmoid(ga)
            dyav = dya_ref[:, cols]
            dh_buf[:, cols] = dyav * (ga * sg)
            dga = (dyav * hl_ext[pl.ds(SUB, tm), cols]) * (sg * (1.0 + ga * (1.0 - sg)))
            dz_ref[:, lw + h * HEAD:lw + (h + 1) * HEAD] = dga.astype(dz_ref.dtype)
        an_buf[...] = _rows_ahead(a_ext[...], 1)[0:tm]

        rows = lax.broadcasted_iota(jnp.int32, (SUB, lw), 0)
        nch = tm // SUB

        def step(c, car):
            i0 = pl.multiple_of((nch - 1 - c) * SUB, SUB)
            cv = an_buf[pl.ds(i0, SUB), :]
            bv = dh_buf[pl.ds(i0, SUB), :]
            for sh in (1, 2, 4):
                m = rows < SUB - sh
                c_sh = jnp.where(m, pltpu.roll(cv, SUB - sh, 0), 1.0)
                b_sh = jnp.where(m, pltpu.roll(bv, SUB - sh, 0), 0.0)
                bv = cv * b_sh + bv
                cv = cv * c_sh
            hv = cv * car + bv
            dh_buf[pl.ds(i0, SUB), :] = hv
            return jnp.broadcast_to(hv[0:1, :], (SUB, lw))

        carry[...] = lax.fori_loop(0, nch, step, carry[...])

        for h in range(N_HEADS):
            cols = slice(h * HEAD, (h + 1) * HEAD)
            lam_h = lam_ref[:, cols]
            neg_c_sp = -LRU_C * _softplus_neg(lam_h)
            xc = xc_buf[:, cols]
            r = r_buf[:, cols]
            i = i_buf[:, cols]
            a = a_ref[:, cols]
            mult, inv_mult = _lru_mult(neg_c_sp * r, a)
            dh = dh_buf[:, cols]
            dhx = dh * xc
            di = dhx * mult
            h_prev = _rows_back(hl_ext[:, cols], 1)[SUB:SUB + tm]
            dlog_a = (dh * h_prev) * a - (dhx * i) * ((a * a) * inv_mult)
            pack_ref[pl.ds(R["dlam"][0] + h, 1), :] += _colsum(dlog_a * r) * (LRU_C * _sigmoid(-lam_h))
            dpa = (dlog_a * neg_c_sp) * (r * (1.0 - r))
            dpx = di * (i * (1.0 - i))
            dpa16 = dpa.astype(_MM)
            dpx16 = dpx.astype(_MM)
            xc16 = xc.astype(_MM)
            dxc = dh * (mult * i) + _dot_nt(dpa16, wa_ref[h]) + _dot_nt(dpx16, wx_ref[h])
            pack_ref[pl.ds(R["dwa"][0] + h * HEAD, HEAD), :] += _dot_tn(xc16, dpa16)
            pack_ref[pl.ds(R["dwx"][0] + h * HEAD, HEAD), :] += _dot_tn(xc16, dpx16)
            pack_ref[pl.ds(R["dba"][0] + h, 1), :] += _colsum(dpa)
            pack_ref[pl.ds(R["dbx"][0] + h, 1), :] += _colsum(dpx)
            pack_ref[pl.ds(R["dcb"][0] + h, 1), :] += _colsum(dxc)
            dxc_ext[pl.ds(0, tm), cols] = dxc

        for h in range(N_HEADS):
            cols = slice(h * HEAD, (h + 1) * HEAD)
            dxc_all = dxc_ext[:, cols]
            xa = zq_ref[:, cols]
            dxa = None
            for k in range(4):
                dxc_k = _rows_ahead(dxc_all, 3 - k)[0:tm]
                pack_ref[pl.ds(R["dcw"][0] + SUB * k + h, 1), :] += _colsum(dxc_k * xa)
                term = cw_ref[pl.ds(k, 1), cols] * dxc_k
                dxa = term if dxa is None else dxa + term
            dz_ref[:, cols] = dxa.astype(dz_ref.dtype)

        pos = jr * tm + lax.broadcasted_iota(jnp.int32, (tm, HEAD), 0)
        for g, k in enumerate(POOL_WINDOWS):
            cols = slice(g * HEAD, (g + 1) * HEAD)
            inv_cnt = 1.0 / jnp.minimum(pos + 1, k).astype(F32)
            diff16 = diff_ref[:, cols]
            yp = yp_ref[:, cols]
            sc = ps_ref[:, cols]
            gb = zq_ref[:, 2 * lw + pwid + g * HEAD:2 * lw + pwid + (g + 1) * HEAD]
            sgb = _sigmoid(gb)
            dybv = dyb_ref[:, cols]
            dy_pool = dybv * (gb * sgb)
            dgb = (dybv * (yp * sc)) * (sgb * (1.0 + gb * (1.0 - sgb)))
            pack_ref[pl.ds(R["dps"][0] + g, 1), :] += _colsum(dy_pool * yp)
            dyp16 = (dy_pool * sc).astype(_MM)
            pack_ref[pl.ds(R["dpw"][0] + g * HEAD, HEAD), :] += _dot_tn(diff16, dyp16)
            ddiff = _dot_nt(dyp16, pw_ref[g])
            q_ext[pl.ds(0, tm), cols] = ddiff * inv_cnt
            dxb = _window_sum(q_ext[:, cols], k, _rows_ahead)[0:tm] - ddiff
            dz_ref[:, 2 * lw + g * HEAD:2 * lw + (g + 1) * HEAD] = dxb.astype(dz_ref.dtype)
            dz_ref[:, 2 * lw + pwid + g * HEAD:2 * lw + pwid + (g + 1) * HEAD] = dgb.astype(dz_ref.dtype)

        a_ext[pl.ds(tm, SUB), :] = a_ext[pl.ds(0, SUB), :]
        dxc_ext[pl.ds(tm, HALO), :] = dxc_ext[pl.ds(0, HALO), :]
        q_ext[pl.ds(tm, HALO), :] = q_ext[pl.ds(0, HALO), :]

        @pl.when((b == nb - 1) & (j == nt - 1))
        def _():
            pack_ref[pl.ds(_OFF_MID, MID_PACK_ROWS), :] = pack_mid_ref[...]
            for cp in _sibling_copies(sib_src, sib_out, send_sems, recv_sems):
                cp.wait()

    rev = lambda w: pl.BlockSpec((None, tm, w), lambda b, j: (b, nt - 1 - j, 0))
    prev_rows = lambda rows, w: pl.BlockSpec(
        (None, rows, w), lambda b, j: (b, jnp.maximum((nt - 1 - j) * (tm // rows) - 1, 0), 0))
    return pl.pallas_call(
        body, name="bwd_seq", grid=(nb, nt),
        in_specs=[rev(zw), rev(lw), prev_rows(SUB, lw), rev(lw), rev(lw), rev(lw), rev(lw), rev(pwid), rev(pwid),
                  rev(lw), rev(pwid), rev(nc - zw),
                  _const_spec(cw.shape), _const_spec(wa.shape), _const_spec(wx.shape), _const_spec(lam.shape),
                  _const_spec(pw.shape), _const_spec(ps.shape), _const_spec(pack_mid.shape)] + [HBM_SPEC] * ns,
        out_specs=(rev(nc), pl.BlockSpec((SMALL_ROWS, HEAD), lambda b, j: (0, 0))) + (HBM_SPEC,) * ns,
        out_shape=(jax.ShapeDtypeStruct((nb, s, nc), _MM), jax.ShapeDtypeStruct((SMALL_ROWS, HEAD), F32))
        + tuple(jax.ShapeDtypeStruct((4,) + g.shape[1:], g.dtype) for g in sib_grads),
        scratch_shapes=[pltpu.VMEM((tm + SUB, lw), F32), pltpu.VMEM((tm, lw), F32), pltpu.VMEM((tm + SUB, lw), F32),
                        pltpu.VMEM((tm, lw), F32), pltpu.VMEM((tm + HALO, lw), F32),
                        pltpu.VMEM((tm + HALO, pwid), F32), pltpu.VMEM((SUB, lw), F32),
                        pltpu.SemaphoreType.DMA((ns, 4)), pltpu.SemaphoreType.DMA((ns, 4))],
        compiler_params=pltpu.CompilerParams(dimension_semantics=("arbitrary", "arbitrary"), vmem_limit_bytes=VMEM_LIMIT),
    )(z, hl, hl, *saved, dya, dyb, dm, cw, wa, wx, lam, pw, ps, pack_mid, *sib_grads)


def _bwd_win(ht, dz, chip_srcs, small):
    nb, d, s = ht.shape
    nc = dz.shape[2]
    sec = nc // N_DEV
    n_chips = N_DEV // 2
    nm = len(chip_srcs)

    def body(ht_ref, dz_ref, *rest):
        chip_src, small_src = rest[:nm], rest[nm]
        gwin_ref, recv_ref = rest[nm + 1:nm + 3]
        chip_out, small_out = rest[nm + 3:2 * nm + 3], rest[2 * nm + 3]
        acc, local_sems, sib_send, sib_recv, chip_send, chip_recv, small_send, small_recv = rest[2 * nm + 4:]
        q = pl.program_id(0)
        b = pl.program_id(1)
        slot = q % 2
        x, y, c = _my_pos()

        def half(sl, core):
            return acc.at[sl, :, pl.ds(pl.multiple_of(core * sec, HEAD), sec)]

        def local_copy(qq, sl):
            return pltpu.make_async_copy(half(sl, c), gwin_ref.at[2 * qq + c], local_sems.at[sl])

        def sib_copy(qq, sl):
            return pltpu.make_async_remote_copy(
                src_ref=half(sl, 1 - c), dst_ref=recv_ref.at[qq], send_sem=sib_send.at[qq],
                recv_sem=sib_recv.at[qq], device_id=(x, y, 1 - c), device_id_type=MESH)

        def others():
            return (_chip_copies(chip_src, chip_out, chip_send, chip_recv)
                    + _sibling_copies([small_src], [small_out], small_send, small_recv))

        @pl.when((q == 0) & (b == 0))
        def _():
            for cp in others():
                cp.start()

        @pl.when((q >= 2) & (b == 0))
        def _():
            local_copy(q - 2, slot).wait()
            sib_copy(q - 2, slot).wait_send()

        part = _dot(ht_ref[b], dz_ref[...])

        @pl.when(b == 0)
        def _():
            acc[slot] = part

        @pl.when(b != 0)
        def _():
            acc[slot] += part

        @pl.when(b == nb - 1)
        def _():
            local_copy(q, slot).start()
            sib_copy(q, slot).start()

        @pl.when((q == n_chips - 1) & (b == nb - 1))
        def _():
            for qq in (n_chips - 2, n_chips - 1):
                local_copy(qq, qq % 2).wait()
                sib_copy(qq, qq % 2).wait_send()
            for qq in range(n_chips):
                sib_copy(qq, 0).wait_recv()
            for cp in others():
                cp.wait()

    return pl.pallas_call(
        body, name="bwd_win", grid=(n_chips, nb),
        in_specs=[_const_spec(ht.shape), pl.BlockSpec((None, s, 2 * sec), lambda q, b: (b, 0, q))]
        + [HBM_SPEC] * (nm + 1),
        out_specs=(HBM_SPEC,) * (nm + 3),
        out_shape=(jax.ShapeDtypeStruct((N_DEV, d, sec), F32), jax.ShapeDtypeStruct((n_chips, d, sec), F32))
        + tuple(jax.ShapeDtypeStruct((3,) + g.shape[1:], g.dtype) for g in chip_srcs)
        + (jax.ShapeDtypeStruct((4,) + small.shape[1:], small.dtype),),
        scratch_shapes=[pltpu.VMEM((2, d, 2 * sec), F32), pltpu.SemaphoreType.DMA((2,)),
                        pltpu.SemaphoreType.DMA((N_DEV // 2,)), pltpu.SemaphoreType.DMA((N_DEV // 2,)),
                        pltpu.SemaphoreType.DMA((nm, 3)), pltpu.SemaphoreType.DMA((nm, 3)),
                        pltpu.SemaphoreType.DMA((1, 4)), pltpu.SemaphoreType.DMA((1, 4))],
        compiler_params=pltpu.CompilerParams(dimension_semantics=("arbitrary", "arbitrary"), vmem_limit_bytes=VMEM_LIMIT),
    )(ht, dz, *chip_srcs, small)


def _bwd_dx(dz, x, dx1, win, g0, small, sib_small, chip_srcs, tm):
    t, d = x.shape
    nc = win.shape[1]
    n = t // tm
    nm = len(chip_srcs)
    sec_rows = small.shape[1]
    n_pieces = 5

    def body(dz_ref, x_ref, dx1_ref, win_ref, g0_ref, small_ref, sib_small_ref, *rest):
        chip_src = rest[:nm]
        gx_ref = rest[nm]
        chip_out = rest[nm + 1:2 * nm + 1]
        gsmall_ref, parts_ref = rest[2 * nm + 1:2 * nm + 3]
        (pack, pieces, total, chip_send, chip_recv, piece_sems, g_send, g_recv, g_local,
         p_send, p_recv, p_local) = rest[2 * nm + 3:]
        i = pl.program_id(0)
        x_, y_, c_ = _my_pos()
        sec_idx = 4 * x_ + 2 * y_ + c_
        chip_copies = lambda which=None: _chip_copies(chip_src, chip_out, chip_send, chip_recv, which)
        g_start, g_forward, g_finish = _gather_phases([total], [gsmall_ref], [_row_slicer(sec_rows)], g_send, g_recv, g_local)

        @pl.when(i == 0)
        def _():
            pack[...] = jnp.zeros_like(pack)
            for cp in chip_copies([0]):
                cp.start()

        @pl.when(i == min(1, n - 1))
        def _():
            for cp in chip_copies([0]):
                cp.wait_recv()
            srcs = [small_ref.at[sec_idx], sib_small_ref.at[2 * x_ + y_]] + [chip_out[0].at[k] for k in range(3)]
            loads = [pltpu.make_async_copy(src, pieces.at[k], piece_sems.at[k]) for k, src in enumerate(srcs)]
            for cp in loads:
                cp.start()
            for cp in loads:
                cp.wait()
            acc = pieces[0] + pieces[1]
            for k in range(2, n_pieces):
                acc = acc + pieces[k]
            total[...] = acc
            g_start()
            for cp in chip_copies(range(1, nm)):
                cp.start()

        pl.when(i == max(n // 4, min(1, n - 1)))(g_forward)

        xv = x_ref[...]
        r0 = _rsqrt_mean_sq(xv)
        xh = xv * r0
        dh = _dot_nt(dz_ref[...], win_ref[...])
        _store_chunks(pack, 0, _colsum(dh * xh))
        dxh = dh * g0_ref[...]
        gx_ref[...] = dx1_ref[...] + r0 * (dxh - xh * jnp.mean(dxh * xh, axis=-1, keepdims=True))

        @pl.when(i == n - 1)
        def _():
            g_finish()
            for cp in chip_copies([0]):
                cp.wait_send()
            for cp in chip_copies(range(1, nm)):
                cp.wait()
            mine = parts_ref.at[pl.ds(pl.multiple_of(sec_idx * INP_PACK_ROWS, SUB), INP_PACK_ROWS), :]
            local = pltpu.make_async_copy(pack, mine, p_local)
            local.start()
            sends = []
            for k in range(1, N_DEV):
                peer = tuple(1 - v if (k >> bit) & 1 else v for v, bit in ((x_, 2), (y_, 1), (c_, 0)))
                sends.append(pltpu.make_async_remote_copy(
                    src_ref=pack, dst_ref=mine, send_sem=p_send.at[k - 1], recv_sem=p_recv.at[k - 1],
                    device_id=peer, device_id_type=MESH))
            for cp in sends:
                cp.start()
            for cp in sends:
                cp.wait()
            local.wait()

    tile = lambda w: pl.BlockSpec((tm, w), lambda i: (i, 0))
    return pl.pallas_call(
        body, name="bwd_dx", grid=(n,),
        in_specs=[tile(nc), tile(d), tile(d), _const_spec(win.shape), _const_spec(g0.shape)] + [HBM_SPEC] * (nm + 2),
        out_specs=(tile(d),) + (HBM_SPEC,) * (nm + 2),
        out_shape=(jax.ShapeDtypeStruct((t, d), F32),)
        + tuple(jax.ShapeDtypeStruct((3,) + g.shape[1:], g.dtype) for g in chip_srcs)
        + (jax.ShapeDtypeStruct((N_DEV * sec_rows, HEAD), F32), jax.ShapeDtypeStruct((N_DEV * INP_PACK_ROWS, HEAD), F32)),
        scratch_shapes=[pltpu.VMEM((INP_PACK_ROWS, HEAD), F32), pltpu.VMEM((n_pieces, sec_rows, HEAD), F32),
                        pltpu.VMEM((sec_rows, HEAD), F32),
                        pltpu.SemaphoreType.DMA((nm, 3)), pltpu.SemaphoreType.DMA((nm, 3)),
                        pltpu.SemaphoreType.DMA((n_pieces,))] + _gather_sems(1)
        + [pltpu.SemaphoreType.DMA((N_DEV - 1,)), pltpu.SemaphoreType.DMA((N_DEV - 1,)), pltpu.SemaphoreType.DMA],
        compiler_params=pltpu.CompilerParams(dimension_semantics=("arbitrary",), vmem_limit_bytes=VMEM_LIMIT),
    )(dz, x, dx1, win, g0, small, sib_small, *chip_srcs)


ROW_BLOCKS = 8


def _chip_sums(core_idx, grads, recvs, dtypes, name):
    nw = len(grads)

    def body(idx_ref, *refs):
        gs, rs, outs = refs[:nw], refs[nw:2 * nw], refs[2 * nw:]
        for g, r, o in zip(gs, rs, outs):
            o[...] = (g[...] + r[...]).astype(o.dtype)

    def blk(g):
        return (None,) + g.shape[1:]

    return pl.pallas_call(
        body, name=name,
        grid_spec=pltpu.PrefetchScalarGridSpec(
            num_scalar_prefetch=1, grid=(4,),
            in_specs=[pl.BlockSpec(blk(g), lambda q, s: (2 * q + s[0], 0, 0)) for g in grads]
            + [pl.BlockSpec(blk(g), lambda q, s: (q, 0, 0)) for g in grads],
            out_specs=tuple(pl.BlockSpec(blk(g), lambda q, s: (q, 0, 0)) for g in grads)),
        out_shape=tuple(jax.ShapeDtypeStruct((4,) + g.shape[1:], dt) for g, dt in zip(grads, dtypes)),
    )(core_idx, *grads, *recvs)


def _adamw(w, g, m, v):
    m = ADAM_B1 * m + (1.0 - ADAM_B1) * g
    v = ADAM_B2 * v + (1.0 - ADAM_B2) * (g * g)
    m_hat = m / (1.0 - ADAM_B1 ** ADAM_STEP)
    v_hat = v / (1.0 - ADAM_B2 ** ADAM_STEP)
    delta = -ADAM_LR * (m_hat / (jnp.sqrt(v_hat) + ADAM_EPS) + ADAM_WD * w)
    return delta, m, v


def _adam_sections(sec_idx, grads, recv_sib, recv_chips, wmv):
    nw = len(wmv)

    def body(idx_ref, *refs):
        gs, rs, cs = refs[:nw], refs[nw:2 * nw], refs[2 * nw:3 * nw]
        params = refs[3 * nw:6 * nw]
        outs = refs[6 * nw:]
        for w in range(nw):
            g = gs[w][...] + rs[w][...]
            for k in range(3):
                g = g + cs[w][k].astype(F32)
            wv, mv, vv = (params[3 * w + t][...] for t in range(3))
            delta, m_new, v_new = _adamw(wv, g, mv, vv)
            outs[4 * w][...] = g
            outs[4 * w + 1][...] = delta
            outs[4 * w + 2][...] = m_new
            outs[4 * w + 3][...] = v_new

    def rb(g):
        return g.shape[1] // ROW_BLOCKS

    in_specs = [pl.BlockSpec((None, rb(g), g.shape[2]), lambda i, s: (s[0], i, 0)) for g in grads]
    in_specs += [pl.BlockSpec((None, rb(g), g.shape[2]), lambda i, s: (s[1], i, 0)) for g in grads]
    in_specs += [pl.BlockSpec((3, rb(g), g.shape[2]), lambda i, s: (0, i, 0)) for g in grads]
    sec = lambda g: pl.BlockSpec((rb(g), g.shape[2]), lambda i, s: (i, 0))
    for w in range(nw):
        in_specs += [sec(grads[w])] * 3
    out_specs, out_shape = [], []
    for g in grads:
        out_specs += [sec(g)] * 4
        out_shape += [jax.ShapeDtypeStruct(g.shape[1:], F32)] * 4
    flat = [a for t in wmv for a in t]
    outs = pl.pallas_call(
        body, name="adam_sections",
        grid_spec=pltpu.PrefetchScalarGridSpec(num_scalar_prefetch=1, grid=(ROW_BLOCKS,), in_specs=in_specs,
                                               out_specs=tuple(out_specs)),
        out_shape=tuple(out_shape),
    )(sec_idx, *grads, *recv_sib, *recv_chips, *flat)
    return [outs[4 * w:4 * w + 4] for w in range(nw)]


def _adam_small(sec_idx, gpack, parts, parts_wmv, vec_params, mat_params, conv_wmv):
    items = [(0, parts_wmv, "parts")]
    items += [(r0, t, "vec") for r0, t in vec_params] + [(r0, t, "mat") for r0, t in mat_params]
    items.append((_SEQ_ROWS["dcw"][0], conv_wmv, "conv"))

    def body(idx_ref, g_ref, parts_ref, *refs):
        ins, outs = refs[:3 * len(items)], refs[3 * len(items):]
        for n, (r0, _, kind) in enumerate(items):
            w_ref, m_ref, v_ref = ins[3 * n:3 * n + 3]
            o = outs[4 * n:4 * n + 4]
            if kind == "mat":
                g = g_ref[pl.ds(r0, w_ref.shape[0]), :]
                res = (g,) + _adamw(w_ref[...], g, m_ref[...], v_ref[...])
                for ref, val in zip(o, res):
                    ref[...] = val
            elif kind in ("vec", "parts"):
                for h in range(w_ref.shape[1] // HEAD):
                    cols = slice(h * HEAD, (h + 1) * HEAD)
                    if kind == "vec":
                        g = g_ref[pl.ds(r0 + h, 1), :]
                    else:
                        g = parts_ref[pl.ds(h, 1), :]
                        for dev in range(1, N_DEV):
                            g = g + parts_ref[pl.ds(SUB * dev + h, 1), :]
                    res = (g,) + _adamw(w_ref[:, cols], g, m_ref[:, cols], v_ref[:, cols])
                    for ref, val in zip(o, res):
                        ref[:, cols] = val
            else:
                rows = lax.broadcasted_iota(jnp.int32, (SUB, HEAD), 0)
                for k in range(4):
                    blk = g_ref[pl.ds(r0 + SUB * k, SUB), :]
                    g = jnp.sum(jnp.where(rows == idx_ref[0], blk, 0.0), axis=0, keepdims=True)
                    row = pl.ds(k, 1)
                    res = (g,) + _adamw(w_ref[row, :], g, m_ref[row, :], v_ref[row, :])
                    for ref, val in zip(o, res):
                        ref[row, :] = val
        outs[4 * len(items)][...] = g_ref[pl.ds(_OFF_MID + MID_LOSS_ROW, 1), :]

    flat = [a for _, t, _ in items for a in t]
    full = lambda a: pl.BlockSpec(a.shape, lambda i, s: (0,) * a.ndim)
    out_specs, out_shape = [], []
    for _, t, _ in items:
        out_specs += [full(t[0])] * 4
        out_shape += [jax.ShapeDtypeStruct(t[0].shape, F32)] * 4
    loss_row = jax.ShapeDtypeStruct((1, HEAD), F32)
    out_specs.append(full(loss_row))
    out_shape.append(loss_row)
    outs = pl.pallas_call(
        body, name="adam_small",
        grid_spec=pltpu.PrefetchScalarGridSpec(num_scalar_prefetch=1, grid=(1,),
                                               in_specs=[full(gpack), full(parts)] + [full(a) for a in flat],
                                               out_specs=tuple(out_specs)),
        out_shape=tuple(out_shape),
    )(sec_idx, gpack, parts, *flat)
    return [outs[4 * n:4 * n + 4] for n in range(len(items))], outs[4 * len(items)]


def _col_slicer(width):
    return lambda ref, idx: ref.at[:, pl.ds(pl.multiple_of(idx * width, HEAD), width)]


def _row_slicer(rows):
    return lambda ref, idx: ref.at[pl.ds(pl.multiple_of(idx * rows, SUB), rows), :]


def kernel(x, p, norm_g, w_in, conv_w, conv_b, lru_w_a, lru_b_a, lru_w_x, lru_b_x, lru_lambda, pool_w, pool_scale, w_proj_lru, w_proj_pool, w_out, ple_norm_g, w_ple_gate, w_ple_proj, final_g, loss_target, m_norm_g, m_w_in, m_conv_w, m_conv_b, m_lru_w_a, m_lru_b_a, m_lru_w_x, m_lru_b_x, m_lru_lambda, m_pool_w, m_pool_scale, m_w_proj_lru, m_w_proj_pool, m_w_out, m_ple_norm_g, m_w_ple_gate, m_w_ple_proj, m_final_g, v_norm_g, v_w_in, v_conv_w, v_conv_b, v_lru_w_a, v_lru_b_a, v_lru_w_x, v_lru_b_x, v_lru_lambda, v_pool_w, v_pool_scale, v_w_proj_lru, v_w_proj_pool, v_w_out, v_ple_norm_g, v_w_ple_gate, v_w_ple_proj, v_final_g):
    nb, s, d = x.shape
    t = nb * s
    tm = min(TILE_M, s)
    sec_idx = (4 * lax.axis_index("x") + 2 * lax.axis_index("y") + lax.axis_index("c")).astype(jnp.int32)
    chip_idx = (2 * lax.axis_index("x") + lax.axis_index("y")).astype(jnp.int32)
    core_idx = lax.axis_index("c").astype(jnp.int32)

    h, ht, win, cw = _pre(
        x, norm_g, [w_in[0], conv_w[0]], [True, False],
        [(d, N_DEV * w_in.shape[2]), (conv_w.shape[1], N_DEV * conv_w.shape[2])],
        [_col_slicer(w_in.shape[2]), _col_slicer(conv_w.shape[2])], tm)
    late_shards = [w_proj_lru[0], w_proj_pool[0], w_out[0], w_ple_gate[0], w_ple_proj[0]]
    late_shapes = [(N_DEV * w_proj_lru.shape[1], d), (w_proj_pool.shape[1], N_DEV * w_proj_pool.shape[2]),
                   (N_DEV * w_out.shape[1], d), (N_DEV * w_ple_gate.shape[1], d),
                   (w_ple_proj.shape[1], N_DEV * w_ple_proj.shape[2])]
    late_slicers = [_row_slicer(w_proj_lru.shape[1]), _col_slicer(w_proj_pool.shape[2]), _row_slicer(w_out.shape[1]),
                    _row_slicer(w_ple_gate.shape[1]), _col_slicer(w_ple_proj.shape[2])]

    lw = cw.shape[1]
    pwid = pool_scale.shape[1]
    wa = lru_w_a[0].astype(_MM)
    wx = lru_w_x[0].astype(_MM)
    pw = pool_w[0].astype(_MM)
    ba, bx = lru_b_a[0], lru_b_x[0]
    gf = final_g.reshape(1, d)
    rows_of = lambda g: g.reshape(N_DEV, g.shape[0] // N_DEV, g.shape[1])
    core = core_idx.reshape(1)

    fwd_out = _fwd_seq(
        h, win, cw, conv_b, wa, ba, wx, bx, lru_lambda, pw, pool_scale, late_shards, late_shapes, late_slicers, tm)
    z, hl, ya, yb = fwd_out[:4]
    saved = fwd_out[4:N_SEQ_SAVED]
    wpl, wpp, wout, wpg, wpe = fwd_out[N_SEQ_SAVED:]
    x2d = x.reshape(t, d)
    (dya, dyb, dm, dx1, g_pl, g_pp, g_out, g_pg, g_pe, pack_mid) = _mid(
        ya.reshape(t, lw), yb.reshape(t, pwid), h.reshape(t, d), win, x2d, p[0].reshape(t, -1),
        loss_target.reshape(t, d), wpl, wpp, wout, wpg, wpe, ple_norm_g, gf, tm)
    grads_mid = [rows_of(g_pl), g_pp, rows_of(g_out), rows_of(g_pg), g_pe]
    seq_out = _bwd_seq(z, hl, saved, dya.reshape(nb, s, lw), dyb.reshape(nb, s, pwid), dm.reshape(nb, s, -1), cw,
                       wa, wx, lru_lambda, pw, pool_scale, pack_mid, grads_mid, tm)
    dz, sib_mid = seq_out[0], list(seq_out[2:])
    small = seq_out[1].reshape(N_DEV, SMALL_SEC, HEAD)
    sums_mid = _chip_sums(core, grads_mid, sib_mid, [_WIRE] * len(grads_mid), "chip_sums_mid")
    win_out = _bwd_win(ht, dz, sums_mid, small)
    g_in, sib_in, chips_mid, sib_small = win_out[0], win_out[1], list(win_out[2:-1]), win_out[-1]
    sums_small, sums_in = _chip_sums(core, [small, g_in], [sib_small, sib_in], [F32, _WIRE], "chip_sums_in")
    grad_x, _, chips_in, gsmall, g0_parts = _bwd_dx(
        dz.reshape(t, -1), x2d, dx1, win, norm_g, small, sib_small, [sums_small, sums_in], tm)

    grads = [g_in] + grads_mid
    recv_sib = [sib_in] + sib_mid
    recv_chips = [chips_in] + chips_mid
    wmv = [(w_in[0], m_w_in[0], v_w_in[0]), (w_proj_lru[0], m_w_proj_lru[0], v_w_proj_lru[0]),
           (w_proj_pool[0], m_w_proj_pool[0], v_w_proj_pool[0]), (w_out[0], m_w_out[0], v_w_out[0]),
           (w_ple_gate[0], m_w_ple_gate[0], v_w_ple_gate[0]), (w_ple_proj[0], m_w_ple_proj[0], v_w_ple_proj[0])]
    big = _adam_sections(jnp.stack([sec_idx, chip_idx]), grads, recv_sib, recv_chips, wmv)

    R = _SEQ_ROWS
    vec = lambda r0, *t: (r0, tuple(a.reshape(1, -1) for a in t))
    mat = lambda r0, *t: (r0, tuple(a.reshape(-1, HEAD) for a in t))
    norm_wmv = vec(0, norm_g, m_norm_g, v_norm_g)[1]
    vec_params = [vec(R["dcb"][0], conv_b, m_conv_b, v_conv_b),
                  vec(R["dlam"][0], lru_lambda, m_lru_lambda, v_lru_lambda),
                  vec(R["dps"][0], pool_scale, m_pool_scale, v_pool_scale),
                  vec(_OFF_MID, ple_norm_g, m_ple_norm_g, v_ple_norm_g),
                  vec(_OFF_MID + SUB, final_g, m_final_g, v_final_g)]
    mat_params = [mat(R["dwa"][0], lru_w_a, m_lru_w_a, v_lru_w_a), mat(R["dba"][0], lru_b_a, m_lru_b_a, v_lru_b_a),
                  mat(R["dwx"][0], lru_w_x, m_lru_w_x, v_lru_w_x), mat(R["dbx"][0], lru_b_x, m_lru_b_x, v_lru_b_x),
                  mat(R["dpw"][0], pool_w, m_pool_w, v_pool_w)]
    conv_wmv = (conv_w[0], m_conv_w[0], v_conv_w[0])
    small_out, loss_row = _adam_small(sec_idx.reshape(1), gsmall, g0_parts, norm_wmv, vec_params, mat_params, conv_wmv)
    loss = loss_row[0, 0]

    res = {}
    names_small = ["norm_g", "conv_b", "lru_lambda", "pool_scale", "ple_norm_g", "final_g",
                   "lru_w_a", "lru_b_a", "lru_w_x", "lru_b_x", "pool_w", "conv_w"]
    shapes = {"norm_g": norm_g, "conv_b": conv_b, "lru_lambda": lru_lambda, "pool_scale": pool_scale,
              "ple_norm_g": ple_norm_g, "final_g": final_g, "lru_w_a": lru_w_a, "lru_b_a": lru_b_a, "lru_w_x": lru_w_x,
              "lru_b_x": lru_b_x, "pool_w": pool_w, "conv_w": conv_w}
    for name, quad in zip(names_small, small_out):
        res[name] = [a.reshape(shapes[name].shape) for a in quad]
    names_big = ["w_in", "w_proj_lru", "w_proj_pool", "w_out", "w_ple_gate", "w_ple_proj"]
    for name, quad, (w, _, _) in zip(names_big, big, wmv):
        res[name] = [a.reshape((1,) + w.shape) for a in quad]
    order = ["norm_g", "w_in", "conv_w", "conv_b", "lru_w_a", "lru_b_a", "lru_w_x", "lru_b_x", "lru_lambda", "pool_w",
             "pool_scale", "w_proj_lru", "w_proj_pool", "w_out", "ple_norm_g", "w_ple_gate", "w_ple_proj", "final_g"]
    out = [loss, grad_x.reshape(nb, s, d)]
    for kind in range(4):
        out += [res[name][kind] for name in order]
    return tuple(out)
```

```python
import jax
import jax.numpy as jnp
from jax import lax
from jax.experimental import pallas as pl
from jax.experimental.pallas import tpu as pltpu

F32 = jnp.float32
_MM = jnp.bfloat16
_WIRE = jnp.bfloat16

EPS = 1e-6
LRU_C = 8.0
MULT_SQ_FLOOR = 1e-30
POOL_WINDOWS = (2, 4, 8, 16)
N_HEADS = 8
HEAD = 128
HALO = 16
SUB = 8

ADAM_LR = 0.001
ADAM_B1 = 0.9
ADAM_B2 = 0.999
ADAM_EPS = 1e-08
ADAM_WD = 0.01
ADAM_STEP = 10

N_DEV = 8
MESH = pl.DeviceIdType.MESH
VMEM_LIMIT = 60 * 1024 * 1024
TILE_M = 256
N_SEQ_SAVED = 10

_SEQ_ROWS = {"dwa": (0, 1024), "dwx": (1024, 1024), "dpw": (2048, 512), "dba": (2560, 8), "dbx": (2568, 8),
             "dcb": (2576, 8), "dlam": (2584, 8), "dcw": (2592, 32), "dps": (2624, 8)}
SEQ_PACK_ROWS = 2632
MID_PACK_ROWS = 24
MID_LOSS_ROW = 16
INP_PACK_ROWS = 8
SMALL_ROWS = 3072
SMALL_SEC = SMALL_ROWS // N_DEV
_OFF_MID = SEQ_PACK_ROWS


def _dot(a, b):
    return jnp.dot(a, b, preferred_element_type=F32)


def _dot_nt(a, b):
    return lax.dot_general(a, b, (((1,), (1,)), ((), ())), preferred_element_type=F32)


def _dot_tn(a, b):
    return lax.dot_general(a, b, (((0,), (0,)), ((), ())), preferred_element_type=F32)


def _sigmoid(v):
    return 0.5 * jnp.tanh(0.5 * v) + 0.5


def _lru_mult(log_a, a):
    m2 = jnp.maximum(-jnp.tanh(log_a) * (1.0 + a * a), MULT_SQ_FLOOR)
    inv = lax.rsqrt(m2)
    return m2 * inv, inv


def _rows_back(v, n):
    return pltpu.roll(v, n, 0) if n else v


def _rows_ahead(v, n):
    return pltpu.roll(v, v.shape[0] - n, 0) if n else v


def _softplus_neg(lam):
    e = jnp.exp(-jnp.abs(lam))
    w = 1.0 + e
    l1p = jnp.where(w == 1.0, e, jnp.log(w) * (e / (w - 1.0)))
    return jnp.maximum(-lam, 0.0) + l1p


def _rsqrt_mean_sq(v):
    return lax.rsqrt(jnp.mean(v * v, axis=-1, keepdims=True) + EPS)


def _colsum(v):
    return jnp.sum(v, axis=0, keepdims=True)


def _const_spec(shape):
    nd = len(shape)
    return pl.BlockSpec(shape, lambda *_: (0,) * nd, pipeline_mode=pl.Buffered(1))


HBM_SPEC = pl.BlockSpec(memory_space=pl.ANY)


def _my_pos():
    return lax.axis_index("x"), lax.axis_index("y"), lax.axis_index("c")


def _other_chips(x, y):
    return [(1 - x, y), (x, 1 - y), (1 - x, 1 - y)]


def _gather_sems(nw):
    return [pltpu.SemaphoreType.DMA((nw, 7)), pltpu.SemaphoreType.DMA((nw, 7)), pltpu.SemaphoreType.DMA((nw,))]


def _gather_phases(srcs, outs, slicers, send_sems, recv_sems, local_sems):
    nw = len(srcs)
    x, y, c = _my_pos()
    me, sibling = (x, y, c), (x, y, 1 - c)
    chips = _other_chips(x, y)

    def part(w, pos):
        return slicers[w](outs[w], 4 * pos[0] + 2 * pos[1] + pos[2])

    def copy(w, k, block, to, src=None):
        return pltpu.make_async_remote_copy(
            src_ref=part(w, block) if src is None else src, dst_ref=part(w, block),
            send_sem=send_sems.at[w, k], recv_sem=recv_sems.at[w, k], device_id=to, device_id_type=MESH)

    def mine():
        return [pltpu.make_async_copy(srcs[w], part(w, me), local_sems.at[w]) for w in range(nw)]

    def first():
        out = []
        for w in range(nw):
            out.append(copy(w, 0, me, sibling, src=srcs[w]))
            out += [copy(w, 1 + j, me, (*chips[j], c), src=srcs[w]) for j in range(2)]
        return out

    def relay(w, j):
        return copy(w, 3, (*chips[j], c), (*chips[1 - j], c))

    def passed():
        return [copy(w, 4 + j, (*chip, c), sibling) for j, chip in enumerate(chips) for w in range(nw)]

    def start():
        for cp in mine() + first():
            cp.start()

    def forward():
        for j, chip in enumerate(chips):
            for w in range(nw):
                copy(w, 1 + j, (*chip, c), me).wait_recv()
                copy(w, 4 + j, (*chip, c), sibling).start()
            if j < 2:
                @pl.when(c == j)
                def _():
                    for w in range(nw):
                        relay(w, j).start()

    def finish():
        for w in range(nw):
            copy(w, 0, sibling, me).wait_recv()
            for j, chip in enumerate(chips):
                copy(w, 4 + j, (*chip, 1 - c), me).wait_recv()
        for cp in first() + [relay(w, 0) for w in range(nw)] + passed():
            cp.wait_send()
        for cp in mine():
            cp.wait()

    return start, forward, finish


def _sibling_copies(srcs, outs, send_sems, recv_sems):
    x, y, c = _my_pos()
    return [pltpu.make_async_remote_copy(
        src_ref=srcs[w].at[2 * q + (1 - c)], dst_ref=outs[w].at[q], send_sem=send_sems.at[w, q],
        recv_sem=recv_sems.at[w, q], device_id=(x, y, 1 - c), device_id_type=MESH)
        for w in range(len(srcs)) for q in range(4)]


def _chip_copies(srcs, outs, send_sems, recv_sems, which=None):
    x, y, c = _my_pos()
    return [pltpu.make_async_remote_copy(
        src_ref=srcs[w].at[2 * px + py], dst_ref=outs[w].at[k], send_sem=send_sems.at[w, k],
        recv_sem=recv_sems.at[w, k], device_id=(px, py, c), device_id_type=MESH)
        for w in (range(len(srcs)) if which is None else which) for k, (px, py) in enumerate(_other_chips(x, y))]


def _lru_gates(xc, h, wa_ref, ba_ref, wx_ref, bx_ref, neg_c_sp):
    xc16 = xc.astype(_MM)
    r = _sigmoid(_dot(xc16, wa_ref[h]) + ba_ref[pl.ds(h, 1), :])
    i = _sigmoid(_dot(xc16, wx_ref[h]) + bx_ref[pl.ds(h, 1), :])
    return r, i, neg_c_sp * r


def _conv_head(xa_ext, cw_ref, cb_ref, cols, tm):
    ext = xa_ext[:, cols]
    xc = cb_ref[:, cols] + cw_ref[pl.ds(3, 1), cols] * ext[HALO:HALO + tm]
    for k in range(3):
        xc = xc + cw_ref[pl.ds(k, 1), cols] * _rows_back(ext, 3 - k)[HALO:HALO + tm]
    return xc


def _window_sum(ext, k, shift):
    n = 1
    while n < k:
        ext = ext + shift(ext, n)
        n *= 2
    return ext


def _pool_diff(xb_ext, g, k, pos, tm):
    cols = slice(g * HEAD, (g + 1) * HEAD)
    ext = xb_ext[:, cols]
    ws = _window_sum(ext, k, _rows_back)[HALO:HALO + tm]
    inv_cnt = 1.0 / jnp.minimum(pos + 1, k).astype(F32)
    return ws * inv_cnt - ext[HALO:HALO + tm], inv_cnt


def _staged_sources(shards, as_operand):
    in_specs = [_const_spec(a.shape) if c else HBM_SPEC for a, c in zip(shards, as_operand)]
    stages = [pltpu.VMEM(a.shape, _MM) for a, c in zip(shards, as_operand) if c]
    dtypes = [_MM if c else a.dtype for a, c in zip(shards, as_operand)]
    return in_specs, stages, dtypes


def _gather_sources(srcs, stages, as_operand):
    stages = iter(stages)
    return [next(stages) if c else src for src, c in zip(srcs, as_operand)]


def _fill_stages(srcs, staged, as_operand):
    for src, dst, c in zip(srcs, staged, as_operand):
        if c:
            dst[...] = src[...].astype(dst.dtype)


def _pre(x, g0, shards, as_operand, shapes, slicers, tm):
    nb, s, d = x.shape
    nt = s // tm
    nl = len(shards)
    shard_specs, stage_shapes, wire = _staged_sources(shards, as_operand)

    def body(x_ref, g0_ref, *rest):
        src, rest = rest[:nl], rest[nl:]
        h_ref, ht_ref = rest[:2]
        out, rest = rest[2:2 + nl], rest[2 + nl:]
        (send_sems, recv_sems, local_sems), stages = rest[:3], rest[3:]
        step_no = pl.program_id(0) * nt + pl.program_id(1)
        staged = _gather_sources(src, stages, as_operand)
        g_start, g_forward, g_finish = _gather_phases(staged, out, slicers, send_sems, recv_sems, local_sems)

        @pl.when(step_no == 0)
        def _():
            _fill_stages(src, staged, as_operand)
            g_start()

        xv = x_ref[...]
        h16 = ((xv * _rsqrt_mean_sq(xv)) * g0_ref[...]).astype(_MM)
        h_ref[...] = h16
        ht_ref[...] = h16.T

        @pl.when(step_no == nb * nt - 1)
        def _():
            g_forward()
            g_finish()

    tile = pl.BlockSpec((None, tm, d), lambda b, j: (b, j, 0))
    return pl.pallas_call(
        body, name="pre", grid=(nb, nt),
        in_specs=[tile, _const_spec(g0.shape)] + shard_specs,
        out_specs=(tile, pl.BlockSpec((None, d, tm), lambda b, j: (b, 0, j))) + (HBM_SPEC,) * nl,
        out_shape=(jax.ShapeDtypeStruct((nb, s, d), _MM), jax.ShapeDtypeStruct((nb, d, s), _MM))
        + tuple(jax.ShapeDtypeStruct(shp, dt) for shp, dt in zip(shapes, wire)),
        scratch_shapes=_gather_sems(nl) + stage_shapes,
        compiler_params=pltpu.CompilerParams(dimension_semantics=("arbitrary", "arbitrary")),
    )(x, g0, *shards)


def _fwd_seq(h, win, cw, cb, wa, ba, wx, bx, lam, pw, ps, late_shards, late_shapes, late_slicers, tm):
    nb, s, d = h.shape
    lw = cw.shape[1]
    pwid = ps.shape[1]
    nc = 2 * lw + 2 * pwid
    nt = s // tm
    nl = len(late_shards)
    n_steps = nb * nt
    as_operand = [True] * nl
    shard_specs, stage_shapes, wire = _staged_sources(late_shards, as_operand)

    def body(h_ref, win_ref, cw_ref, cb_ref, wa_ref, ba_ref, wx_ref, bx_ref, lam_ref, pw_ref, ps_ref, *rest):
        late_src, rest = rest[:nl], rest[nl:]
        z_ref, hl_ref, ya_ref, yb_ref, xc_ref, r_ref, i_ref, a_buf, diff_ref, yp_ref = rest[:N_SEQ_SAVED]
        late_out, rest = rest[N_SEQ_SAVED:N_SEQ_SAVED + nl], rest[N_SEQ_SAVED + nl:]
        (xa_ext, xb_ext, carry, send_sems, recv_sems, local_sems), stages = rest[:6], rest[6:]
        j = pl.program_id(1)
        step_no = pl.program_id(0) * nt + j
        staged = _gather_sources(late_src, stages, as_operand)
        g_start, g_forward, g_finish = _gather_phases(staged, late_out, late_slicers, send_sems, recv_sems, local_sems)

        @pl.when(step_no == 0)
        def _():
            _fill_stages(late_src, staged, as_operand)
            g_start()

        pl.when(step_no == n_steps // 2)(g_forward)

        @pl.when(j == 0)
        def _():
            xa_ext[pl.ds(0, HALO), :] = jnp.zeros((HALO, lw), F32)
            xb_ext[pl.ds(0, HALO), :] = jnp.zeros((HALO, pwid), F32)
            carry[...] = jnp.zeros_like(carry)

        h16 = h_ref[...]
        xa = _dot(h16, win_ref[:, 0:lw])
        z_ref[:, 0:lw] = xa
        xa_ext[pl.ds(HALO, tm), :] = xa
        for c0 in range(lw, nc, lw):
            z_ref[:, c0:c0 + lw] = _dot(h16, win_ref[:, c0:c0 + lw])

        for h in range(N_HEADS):
            cols = slice(h * HEAD, (h + 1) * HEAD)
            xc = _conv_head(xa_ext, cw_ref, cb_ref, cols, tm)
            neg_c_sp = -LRU_C * _softplus_neg(lam_ref[:, cols])
            r, i, log_a = _lru_gates(xc, h, wa_ref, ba_ref, wx_ref, bx_ref, neg_c_sp)
            a = jnp.exp(log_a)
            xc_ref[:, cols] = xc
            r_ref[:, cols] = r
            i_ref[:, cols] = i
            a_buf[:, cols] = a
            hl_ref[:, cols] = _lru_mult(log_a, a)[0] * (i * xc)

        rows = lax.broadcasted_iota(jnp.int32, (SUB, lw), 0)

        def step(c, car):
            i0 = pl.multiple_of(c * SUB, SUB)
            av = a_buf[pl.ds(i0, SUB), :]
            bv = hl_ref[pl.ds(i0, SUB), :]
            for sh in (1, 2, 4):
                m = rows >= sh
                a_sh = jnp.where(m, pltpu.roll(av, sh, 0), 1.0)
                b_sh = jnp.where(m, pltpu.roll(bv, sh, 0), 0.0)
                bv = av * b_sh + bv
                av = av * a_sh
            hv = av * car + bv
            hl_ref[pl.ds(i0, SUB), :] = hv
            return jnp.broadcast_to(hv[SUB - 1:SUB, :], (SUB, lw))

        carry[...] = lax.fori_loop(0, tm // SUB, step, carry[...])

        ga = z_ref[:, lw:2 * lw]
        ya_ref[...] = (hl_ref[...] * (ga * _sigmoid(ga))).astype(ya_ref.dtype)

        xb_ext[pl.ds(HALO, tm), :] = z_ref[:, 2 * lw:2 * lw + pwid]
        pos = j * tm + lax.broadcasted_iota(jnp.int32, (tm, HEAD), 0)
        for g, k in enumerate(POOL_WINDOWS):
            cols = slice(g * HEAD, (g + 1) * HEAD)
            diff16 = _pool_diff(xb_ext, g, k, pos, tm)[0].astype(_MM)
            yp = _dot(diff16, pw_ref[g])
            diff_ref[:, cols] = diff16
            yp_ref[:, cols] = yp
            gb = z_ref[:, 2 * lw + pwid + g * HEAD:2 * lw + pwid + (g + 1) * HEAD]
            yb_ref[:, cols] = ((yp * ps_ref[:, cols]) * (gb * _sigmoid(gb))).astype(yb_ref.dtype)

        xa_ext[pl.ds(0, HALO), :] = xa_ext[pl.ds(tm, HALO), :]
        xb_ext[pl.ds(0, HALO), :] = xb_ext[pl.ds(tm, HALO), :]
        pl.when(step_no == n_steps - 1)(g_finish)

    tile = lambda w: pl.BlockSpec((None, tm, w), lambda b, j: (b, j, 0))
    return pl.pallas_call(
        body, name="fwd_seq", grid=(nb, nt),
        in_specs=[tile(d), _const_spec((d, nc)), _const_spec(cw.shape), _const_spec(cb.shape),
                  _const_spec(wa.shape), _const_spec(ba.shape), _const_spec(wx.shape), _const_spec(bx.shape),
                  _const_spec(lam.shape), _const_spec(pw.shape), _const_spec(ps.shape)] + shard_specs,
        out_specs=(tile(nc), tile(lw), tile(lw), tile(pwid), tile(lw), tile(lw), tile(lw), tile(lw), tile(pwid),
                   tile(pwid)) + (HBM_SPEC,) * nl,
        out_shape=(jax.ShapeDtypeStruct((nb, s, nc), F32), jax.ShapeDtypeStruct((nb, s, lw), F32),
                   jax.ShapeDtypeStruct((nb, s, lw), _MM), jax.ShapeDtypeStruct((nb, s, pwid), _MM))
        + (jax.ShapeDtypeStruct((nb, s, lw), F32),) * 4
        + (jax.ShapeDtypeStruct((nb, s, pwid), _MM), jax.ShapeDtypeStruct((nb, s, pwid), F32))
        + tuple(jax.ShapeDtypeStruct(shp, dt) for shp, dt in zip(late_shapes, wire)),
        scratch_shapes=[pltpu.VMEM((tm + HALO, lw), F32), pltpu.VMEM((tm + HALO, pwid), F32),
                        pltpu.VMEM((SUB, lw), F32)] + _gather_sems(nl) + stage_shapes,
        compiler_params=pltpu.CompilerParams(dimension_semantics=("arbitrary", "arbitrary"), vmem_limit_bytes=VMEM_LIMIT),
    )(h, win, cw, cb, wa, ba, wx, bx, lam, pw, ps, *late_shards)


def _store_chunks(ref, row0, vec):
    for h in range(vec.shape[1] // HEAD):
        ref[pl.ds(row0 + h, 1), :] += vec[:, h * HEAD:(h + 1) * HEAD]


def _mid(ya, yb, h, win, x, p, tgt, wpl, wpp, wout, wpg, wpe, g1, gf, tm):
    t, d = x.shape
    lw = ya.shape[1]
    pwid = yb.shape[1]
    pdim = p.shape[1]
    n = t // tm
    sec_pp = wpp.shape[1] // N_DEV
    sec_pe = wpe.shape[1] // N_DEV
    n_win_blocks = win.shape[1] // d
    win_cols = lambda c: pl.BlockSpec((d, d), lambda i: (0, c), pipeline_mode=pl.Buffered(1))

    def body(ya_ref, yb_ref, h_ref, wma_ref, wmb_ref, x_ref, p_ref, tgt_ref, wpl_ref, wpp_ref, wout_ref, wpg_ref, wpe_ref,
             g1_ref, gf_ref,
             dya_ref, dyb_ref, dm_ref, dx1_ref, gpl_ref, gpp_ref, gout_ref, gpg_ref, gpe_ref, pack_ref,
             acc_pl, acc_pp, acc_out, acc_pg, acc_pe):
        i = pl.program_id(0)

        @pl.when(i == 0)
        def _():
            for acc in (acc_pl, acc_pp, acc_out, acc_pg, acc_pe):
                acc[...] = jnp.zeros_like(acc)
            pack_ref[...] = jnp.zeros_like(pack_ref)

        ya16 = ya_ref[...]
        yb16 = yb_ref[...]
        br_a = _dot(ya16, wpl_ref[...])
        br_b = _dot(yb16, wpp_ref[...])
        h16 = h_ref[...]
        sa = _sigmoid(_dot(h16, wma_ref[...]))
        sb = _sigmoid(_dot(h16, wmb_ref[...]))
        mg16 = (sa * br_a + sb * br_b).astype(_MM)
        x1 = x_ref[...] + _dot(mg16, wout_ref[...])
        r1 = _rsqrt_mean_sq(x1)
        n1 = x1 * r1
        h116 = (n1 * g1_ref[...]).astype(_MM)
        gate = _sigmoid(_dot(h116, wpg_ref[...]))
        p16 = p_ref[...].astype(_MM)
        pe = _dot(p16, wpe_ref[...])
        x2 = x1 + gate * pe
        r2 = _rsqrt_mean_sq(x2)
        n2 = x2 * r2
        err = n2 * gf_ref[...] - tgt_ref[...]
        sq = jnp.sum(_colsum(err * err), axis=1, keepdims=True)
        pack_ref[pl.ds(MID_LOSS_ROW, 1), :] += jnp.broadcast_to(sq * (0.5 / d), (1, HEAD))

        dy = err * (1.0 / d)
        _store_chunks(pack_ref, SUB, _colsum(dy * n2))
        dn2 = dy * gf_ref[...]
        dx2 = r2 * (dn2 - n2 * jnp.mean(dn2 * n2, axis=-1, keepdims=True))
        dpe16 = (dx2 * gate).astype(_MM)
        dpg16 = ((dx2 * pe) * (gate * (1.0 - gate))).astype(_MM)
        acc_pe[...] += _dot_tn(p16, dpe16)
        acc_pg[...] += _dot_tn(h116, dpg16)
        dh1 = _dot_nt(dpg16, wpg_ref[...])
        _store_chunks(pack_ref, 0, _colsum(dh1 * n1))
        dn1 = dh1 * g1_ref[...]
        dx1 = dx2 + r1 * (dn1 - n1 * jnp.mean(dn1 * n1, axis=-1, keepdims=True))
        dx1_ref[...] = dx1
        dx116 = dx1.astype(_MM)
        acc_out[...] += _dot_tn(mg16, dx116)
        dmg = _dot_nt(dx116, wout_ref[...])
        da16 = (dmg * sa).astype(_MM)
        db16 = (dmg * sb).astype(_MM)
        dm_ref[:, 0:d] = ((dmg * br_a) * (sa * (1.0 - sa))).astype(dm_ref.dtype)
        dm_ref[:, d:2 * d] = ((dmg * br_b) * (sb * (1.0 - sb))).astype(dm_ref.dtype)
        acc_pl[...] += _dot_tn(ya16, da16)
        acc_pp[...] += _dot_tn(yb16, db16)
        dya_ref[...] = _dot_nt(da16, wpl_ref[...])
        dyb_ref[...] = _dot_nt(db16, wpp_ref[...])

        @pl.when(i == n - 1)
        def _():
            pltpu.sync_copy(acc_pl, gpl_ref)
            pltpu.sync_copy(acc_out, gout_ref)
            pltpu.sync_copy(acc_pg, gpg_ref)
            for k in range(N_DEV):
                pltpu.sync_copy(acc_pp.at[:, pl.ds(k * sec_pp, sec_pp)], gpp_ref.at[k])
                pltpu.sync_copy(acc_pe.at[:, pl.ds(k * sec_pe, sec_pe)], gpe_ref.at[k])

    tile = lambda w: pl.BlockSpec((tm, w), lambda i: (i, 0))
    any_spec = pl.BlockSpec(memory_space=pl.ANY)
    full = lambda shape: pl.BlockSpec(shape, lambda i: (0,) * len(shape))
    return pl.pallas_call(
        body, name="mid", grid=(n,),
        in_specs=[tile(lw), tile(pwid), tile(d), win_cols(n_win_blocks - 2), win_cols(n_win_blocks - 1),
                  tile(d), tile(pdim), tile(d),
                  _const_spec(wpl.shape), _const_spec(wpp.shape), _const_spec(wout.shape), _const_spec(wpg.shape),
                  _const_spec(wpe.shape), _const_spec(g1.shape), _const_spec(gf.shape)],
        out_specs=(tile(lw), tile(pwid), tile(2 * d), tile(d), any_spec, any_spec, any_spec, any_spec, any_spec,
                   full((MID_PACK_ROWS, HEAD))),
        out_shape=(jax.ShapeDtypeStruct((t, lw), F32), jax.ShapeDtypeStruct((t, pwid), F32),
                   jax.ShapeDtypeStruct((t, 2 * d), _MM), jax.ShapeDtypeStruct((t, d), F32),
                   jax.ShapeDtypeStruct(wpl.shape, F32), jax.ShapeDtypeStruct((N_DEV, wpp.shape[0], sec_pp), F32),
                   jax.ShapeDtypeStruct(wout.shape, F32), jax.ShapeDtypeStruct(wpg.shape, F32),
                   jax.ShapeDtypeStruct((N_DEV, wpe.shape[0], sec_pe), F32),
                   jax.ShapeDtypeStruct((MID_PACK_ROWS, HEAD), F32)),
        scratch_shapes=[pltpu.VMEM(wpl.shape, F32), pltpu.VMEM(wpp.shape, F32), pltpu.VMEM(wout.shape, F32),
                        pltpu.VMEM(wpg.shape, F32), pltpu.VMEM(wpe.shape, F32)],
        compiler_params=pltpu.CompilerParams(dimension_semantics=("arbitrary",), vmem_limit_bytes=VMEM_LIMIT),
    )(ya, yb, h, win, win, x, p, tgt, wpl, wpp, wout, wpg, wpe, g1, gf)


def _bwd_seq(z, hl, saved, dya, dyb, dm, cw, wa, wx, lam, pw, ps, pack_mid, sib_grads, tm):
    nb, s, zw = z.shape
    lw = cw.shape[1]
    pwid = ps.shape[1]
    nt = s // tm
    nc = zw + dm.shape[2]
    R = _SEQ_ROWS
    ns = len(sib_grads)

    def body(zq_ref, hl_ref, hlh_ref, xc_buf, r_buf, i_buf, a_ref, diff_ref, yp_ref, dya_ref, dyb_ref, dm_ref,
             cw_ref, wa_ref, wx_ref, lam_ref, pw_ref, ps_ref, pack_mid_ref, *rest):
        sib_src, rest = rest[:ns], rest[ns:]
        dz_ref, pack_ref = rest[:2]
        sib_out, rest = rest[2:2 + ns], rest[2 + ns:]
        a_ext, an_buf, hl_ext, dh_buf, dxc_ext, q_ext, carry, send_sems, recv_sems = rest
        b = pl.program_id(0)
        j = pl.program_id(1)
        jr = nt - 1 - j
        has_prev = jr > 0

        @pl.when((b == 0) & (j == 0))
        def _():
            for cp in _sibling_copies(sib_src, sib_out, send_sems, recv_sems):
                cp.start()

        dz_ref[:, zw:nc] = dm_ref[...]

        @pl.when((b == 0) & (j == 0))
        def _():
            pack_ref[...] = jnp.zeros_like(pack_ref)

        @pl.when(j == 0)
        def _():
            a_ext[pl.ds(tm, SUB), :] = jnp.zeros((SUB, lw), F32)
            dxc_ext[pl.ds(tm, HALO), :] = jnp.zeros((HALO, lw), F32)
            q_ext[pl.ds(tm, HALO), :] = jnp.zeros((HALO, pwid), F32)
            carry[...] = jnp.zeros_like(carry)

        hl_ext[pl.ds(0, SUB), :] = jnp.where(has_prev, hlh_ref[...], 0.0)
        hl_ext[pl.ds(SUB, tm), :] = hl_ref[...]
        a_ext[pl.ds(0, tm), :] = a_ref[...]
        for h in range(N_HEADS):
            cols = slice(h * HEAD, (h + 1) * HEAD)
            ga = zq_ref[:, lw + h * HEAD:lw + (h + 1) * HEAD]
            sg = _sigmoid(ga)
            dyav = dya_ref[:, cols]
            dh_buf[:, cols] = dyav * (ga * sg)
            dga = (dyav * hl_ext[pl.ds(SUB, tm), cols]) * (sg * (1.0 + ga * (1.0 - sg)))
            dz_ref[:, lw + h * HEAD:lw + (h + 1) * HEAD] = dga.astype(dz_ref.dtype)
        an_buf[...] = _rows_ahead(a_ext[...], 1)[0:tm]

        rows = lax.broadcasted_iota(jnp.int32, (SUB, lw), 0)
        nch = tm // SUB

        def step(c, car):
            i0 = pl.multiple_of((nch - 1 - c) * SUB, SUB)
            cv = an_buf[pl.ds(i0, SUB), :]
            bv = dh_buf[pl.ds(i0, SUB), :]
            for sh in (1, 2, 4):
                m = rows < SUB - sh
                c_sh = jnp.where(m, pltpu.roll(cv, SUB - sh, 0), 1.0)
                b_sh = jnp.where(m, pltpu.roll(bv, SUB - sh, 0), 0.0)
                bv = cv * b_sh + bv
                cv = cv * c_sh
            hv = cv * car + bv
            dh_buf[pl.ds(i0, SUB), :] = hv
            return jnp.broadcast_to(hv[0:1, :], (SUB, lw))

        carry[...] = lax.fori_loop(0, nch, step, carry[...])

        for h in range(N_HEADS):
            cols = slice(h * HEAD, (h + 1) * HEAD)
            lam_h = lam_ref[:, cols]
            neg_c_sp = -LRU_C * _softplus_neg(lam_h)
            xc = xc_buf[:, cols]
            r = r_buf[:, cols]
            i = i_buf[:, cols]
            a = a_ref[:, cols]
            mult, inv_mult = _lru_mult(neg_c_sp * r, a)
            dh = dh_buf[:, cols]
            dhx = dh * xc
            di = dhx * mult
            h_prev = _rows_back(hl_ext[:, cols], 1)[SUB:SUB + tm]
            dlog_a = (dh * h_prev) * a - (dhx * i) * ((a * a) * inv_mult)
            pack_ref[pl.ds(R["dlam"][0] + h, 1), :] += _colsum(dlog_a * r) * (LRU_C * _sigmoid(-lam_h))
            dpa = (dlog_a * neg_c_sp) * (r * (1.0 - r))
            dpx = di * (i * (1.0 - i))
            dpa16 = dpa.astype(_MM)
            dpx16 = dpx.astype(_MM)
            xc16 = xc.astype(_MM)
            dxc = dh * (mult * i) + _dot_nt(dpa16, wa_ref[h]) + _dot_nt(dpx16, wx_ref[h])
            pack_ref[pl.ds(R["dwa"][0] + h * HEAD, HEAD), :] += _dot_tn(xc16, dpa16)
            pack_ref[pl.ds(R["dwx"][0] + h * HEAD, HEAD), :] += _dot_tn(xc16, dpx16)
            pack_ref[pl.ds(R["dba"][0] + h, 1), :] += _colsum(dpa)
            pack_ref[pl.ds(R["dbx"][0] + h, 1), :] += _colsum(dpx)
            pack_ref[pl.ds(R["dcb"][0] + h, 1), :] += _colsum(dxc)
            dxc_ext[pl.ds(0, tm), cols] = dxc

        for h in range(N_HEADS):
            cols = slice(h * HEAD, (h + 1) * HEAD)
            dxc_all = dxc_ext[:, cols]
            xa = zq_ref[:, cols]
            dxa = None
            for k in range(4):
                dxc_k = _rows_ahead(dxc_all, 3 - k)[0:tm]
                pack_ref[pl.ds(R["dcw"][0] + SUB * k + h, 1), :] += _colsum(dxc_k * xa)
                term = cw_ref[pl.ds(k, 1), cols] * dxc_k
                dxa = term if dxa is None else dxa + term
            dz_ref[:, cols] = dxa.astype(dz_ref.dtype)

        pos = jr * tm + lax.broadcasted_iota(jnp.int32, (tm, HEAD), 0)
        for g, k in enumerate(POOL_WINDOWS):
            cols = slice(g * HEAD, (g + 1) * HEAD)
            inv_cnt = 1.0 / jnp.minimum(pos + 1, k).astype(F32)
            diff16 = diff_ref[:, cols]
            yp = yp_ref[:, cols]
            sc = ps_ref[:, cols]
            gb = zq_ref[:, 2 * lw + pwid + g * HEAD:2 * lw + pwid + (g + 1) * HEAD]
            sgb = _sigmoid(gb)
            dybv = dyb_ref[:, cols]
            dy_pool = dybv * (gb * sgb)
            dgb = (dybv * (yp * sc)) * (sgb * (1.0 + gb * (1.0 - sgb)))
            pack_ref[pl.ds(R["dps"][0] + g, 1), :] += _colsum(dy_pool * yp)
            dyp16 = (dy_pool * sc).astype(_MM)
            pack_ref[pl.ds(R["dpw"][0] + g * HEAD, HEAD), :] += _dot_tn(diff16, dyp16)
            ddiff = _dot_nt(dyp16, pw_ref[g])
            q_ext[pl.ds(0, tm), cols] = ddiff * inv_cnt
            dxb = _window_sum(q_ext[:, cols], k, _rows_ahead)[0:tm] - ddiff
            dz_ref[:, 2 * lw + g * HEAD:2 * lw + (g + 1) * HEAD] = dxb.astype(dz_ref.dtype)
            dz_ref[:, 2 * lw + pwid + g * HEAD:2 * lw + pwid + (g + 1) * HEAD] = dgb.astype(dz_ref.dtype)

        a_ext[pl.ds(tm, SUB), :] = a_ext[pl.ds(0, SUB), :]
        dxc_ext[pl.ds(tm, HALO), :] = dxc_ext[pl.ds(0, HALO), :]
        q_ext[pl.ds(tm, HALO), :] = q_ext[pl.ds(0, HALO), :]

        @pl.when((b == nb - 1) & (j == nt - 1))
        def _():
            pack_ref[pl.ds(_OFF_MID, MID_PACK_ROWS), :] = pack_mid_ref[...]
            for cp in _sibling_copies(sib_src, sib_out, send_sems, recv_sems):
                cp.wait()

    rev = lambda w: pl.BlockSpec((None, tm, w), lambda b, j: (b, nt - 1 - j, 0))
    prev_rows = lambda rows, w: pl.BlockSpec(
        (None, rows, w), lambda b, j: (b, jnp.maximum((nt - 1 - j) * (tm // rows) - 1, 0), 0))
    return pl.pallas_call(
        body, name="bwd_seq", grid=(nb, nt),
        in_specs=[rev(zw), rev(lw), prev_rows(SUB, lw), rev(lw), rev(lw), rev(lw), rev(lw), rev(pwid), rev(pwid),
                  rev(lw), rev(pwid), rev(nc - zw),
                  _const_spec(cw.shape), _const_spec(wa.shape), _const_spec(wx.shape), _const_spec(lam.shape),
                  _const_spec(pw.shape), _const_spec(ps.shape), _const_spec(pack_mid.shape)] + [HBM_SPEC] * ns,
        out_specs=(rev(nc), pl.BlockSpec((SMALL_ROWS, HEAD), lambda b, j: (0, 0))) + (HBM_SPEC,) * ns,
        out_shape=(jax.ShapeDtypeStruct((nb, s, nc), _MM), jax.ShapeDtypeStruct((SMALL_ROWS, HEAD), F32))
        + tuple(jax.ShapeDtypeStruct((4,) + g.shape[1:], g.dtype) for g in sib_grads),
        scratch_shapes=[pltpu.VMEM((tm + SUB, lw), F32), pltpu.VMEM((tm, lw), F32), pltpu.VMEM((tm + SUB, lw), F32),
                        pltpu.VMEM((tm, lw), F32), pltpu.VMEM((tm + HALO, lw), F32),
                        pltpu.VMEM((tm + HALO, pwid), F32), pltpu.VMEM((SUB, lw), F32),
                        pltpu.SemaphoreType.DMA((ns, 4)), pltpu.SemaphoreType.DMA((ns, 4))],
        compiler_params=pltpu.CompilerParams(dimension_semantics=("arbitrary", "arbitrary"), vmem_limit_bytes=VMEM_LIMIT),
    )(z, hl, hl, *saved, dya, dyb, dm, cw, wa, wx, lam, pw, ps, pack_mid, *sib_grads)


def _bwd_win(ht, dz, chip_srcs, small):
    nb, d, s = ht.shape
    nc = dz.shape[2]
    sec = nc // N_DEV
    n_chips = N_DEV // 2
    nm = len(chip_srcs)

    def body(ht_ref, dz_ref, *rest):
        chip_src, small_src = rest[:nm], rest[nm]
        gwin_ref, recv_ref = rest[nm + 1:nm + 3]
        chip_out, small_out = rest[nm + 3:2 * nm + 3], rest[2 * nm + 3]
        acc, local_sems, sib_send, sib_recv, chip_send, chip_recv, small_send, small_recv = rest[2 * nm + 4:]
        q = pl.program_id(0)
        b = pl.program_id(1)
        slot = q % 2
        x, y, c = _my_pos()

        def half(sl, core):
            return acc.at[sl, :, pl.ds(pl.multiple_of(core * sec, HEAD), sec)]

        def local_copy(qq, sl):
            return pltpu.make_async_copy(half(sl, c), gwin_ref.at[2 * qq + c], local_sems.at[sl])

        def sib_copy(qq, sl):
            return pltpu.make_async_remote_copy(
                src_ref=half(sl, 1 - c), dst_ref=recv_ref.at[qq], send_sem=sib_send.at[qq],
                recv_sem=sib_recv.at[qq], device_id=(x, y, 1 - c), device_id_type=MESH)

        def others():
            return (_chip_copies(chip_src, chip_out, chip_send, chip_recv)
                    + _sibling_copies([small_src], [small_out], small_send, small_recv))

        @pl.when((q == 0) & (b == 0))
        def _():
            for cp in others():
                cp.start()

        @pl.when((q >= 2) & (b == 0))
        def _():
            local_copy(q - 2, slot).wait()
            sib_copy(q - 2, slot).wait_send()

        part = _dot(ht_ref[b], dz_ref[...])

        @pl.when(b == 0)
        def _():
            acc[slot] = part

        @pl.when(b != 0)
        def _():
            acc[slot] += part

        @pl.when(b == nb - 1)
        def _():
            local_copy(q, slot).start()
            sib_copy(q, slot).start()

        @pl.when((q == n_chips - 1) & (b == nb - 1))
        def _():
            for qq in (n_chips - 2, n_chips - 1):
                local_copy(qq, qq % 2).wait()
                sib_copy(qq, qq % 2).wait_send()
            for qq in range(n_chips):
                sib_copy(qq, 0).wait_recv()
            for cp in others():
                cp.wait()

    return pl.pallas_call(
        body, name="bwd_win", grid=(n_chips, nb),
        in_specs=[_const_spec(ht.shape), pl.BlockSpec((None, s, 2 * sec), lambda q, b: (b, 0, q))]
        + [HBM_SPEC] * (nm + 1),
        out_specs=(HBM_SPEC,) * (nm + 3),
        out_shape=(jax.ShapeDtypeStruct((N_DEV, d, sec), F32), jax.ShapeDtypeStruct((n_chips, d, sec), F32))
        + tuple(jax.ShapeDtypeStruct((3,) + g.shape[1:], g.dtype) for g in chip_srcs)
        + (jax.ShapeDtypeStruct((4,) + small.shape[1:], small.dtype),),
        scratch_shapes=[pltpu.VMEM((2, d, 2 * sec), F32), pltpu.SemaphoreType.DMA((2,)),
                        pltpu.SemaphoreType.DMA((N_DEV // 2,)), pltpu.SemaphoreType.DMA((N_DEV // 2,)),
                        pltpu.SemaphoreType.DMA((nm, 3)), pltpu.SemaphoreType.DMA((nm, 3)),
                        pltpu.SemaphoreType.DMA((1, 4)), pltpu.SemaphoreType.DMA((1, 4))],
        compiler_params=pltpu.CompilerParams(dimension_semantics=("arbitrary", "arbitrary"), vmem_limit_bytes=VMEM_LIMIT),
    )(ht, dz, *chip_srcs, small)


def _bwd_dx(dz, x, dx1, win, g0, small, sib_small, chip_srcs, tm):
    t, d = x.shape
    nc = win.shape[1]
    n = t // tm
    nm = len(chip_srcs)
    sec_rows = small.shape[1]
    n_pieces = 5

    def body(dz_ref, x_ref, dx1_ref, win_ref, g0_ref, small_ref, sib_small_ref, *rest):
        chip_src = rest[:nm]
        gx_ref = rest[nm]
        chip_out = rest[nm + 1:2 * nm + 1]
        gsmall_ref, parts_ref = rest[2 * nm + 1:2 * nm + 3]
        (pack, pieces, total, chip_send, chip_recv, piece_sems, g_send, g_recv, g_local,
         p_send, p_recv, p_local) = rest[2 * nm + 3:]
        i = pl.program_id(0)
        x_, y_, c_ = _my_pos()
        sec_idx = 4 * x_ + 2 * y_ + c_
        chip_copies = lambda which=None: _chip_copies(chip_src, chip_out, chip_send, chip_recv, which)
        g_start, g_forward, g_finish = _gather_phases([total], [gsmall_ref], [_row_slicer(sec_rows)], g_send, g_recv, g_local)

        @pl.when(i == 0)
        def _():
            pack[...] = jnp.zeros_like(pack)
            for cp in chip_copies([0]):
                cp.start()

        @pl.when(i == min(1, n - 1))
        def _():
            for cp in chip_copies([0]):
                cp.wait_recv()
            srcs = [small_ref.at[sec_idx], sib_small_ref.at[2 * x_ + y_]] + [chip_out[0].at[k] for k in range(3)]
            loads = [pltpu.make_async_copy(src, pieces.at[k], piece_sems.at[k]) for k, src in enumerate(srcs)]
            for cp in loads:
                cp.start()
            for cp in loads:
                cp.wait()
            acc = pieces[0] + pieces[1]
            for k in range(2, n_pieces):
                acc = acc + pieces[k]
            total[...] = acc
            g_start()
            for cp in chip_copies(range(1, nm)):
                cp.start()

        pl.when(i == max(n // 4, min(1, n - 1)))(g_forward)

        xv = x_ref[...]
        r0 = _rsqrt_mean_sq(xv)
        xh = xv * r0
        dh = _dot_nt(dz_ref[...], win_ref[...])
        _store_chunks(pack, 0, _colsum(dh * xh))
        dxh = dh * g0_ref[...]
        gx_ref[...] = dx1_ref[...] + r0 * (dxh - xh * jnp.mean(dxh * xh, axis=-1, keepdims=True))

        @pl.when(i == n - 1)
        def _():
            g_finish()
            for cp in chip_copies([0]):
                cp.wait_send()
            for cp in chip_copies(range(1, nm)):
                cp.wait()
            mine = parts_ref.at[pl.ds(pl.multiple_of(sec_idx * INP_PACK_ROWS, SUB), INP_PACK_ROWS), :]
            local = pltpu.make_async_copy(pack, mine, p_local)
            local.start()
            sends = []
            for k in range(1, N_DEV):
                peer = tuple(1 - v if (k >> bit) & 1 else v for v, bit in ((x_, 2), (y_, 1), (c_, 0)))
                sends.append(pltpu.make_async_remote_copy(
                    src_ref=pack, dst_ref=mine, send_sem=p_send.at[k - 1], recv_sem=p_recv.at[k - 1],
                    device_id=peer, device_id_type=MESH))
            for cp in sends:
                cp.start()
            for cp in sends:
                cp.wait()
            local.wait()

    tile = lambda w: pl.BlockSpec((tm, w), lambda i: (i, 0))
    return pl.pallas_call(
        body, name="bwd_dx", grid=(n,),
        in_specs=[tile(nc), tile(d), tile(d), _const_spec(win.shape), _const_spec(g0.shape)] + [HBM_SPEC] * (nm + 2),
        out_specs=(tile(d),) + (HBM_SPEC,) * (nm + 2),
        out_shape=(jax.ShapeDtypeStruct((t, d), F32),)
        + tuple(jax.ShapeDtypeStruct((3,) + g.shape[1:], g.dtype) for g in chip_srcs)
        + (jax.ShapeDtypeStruct((N_DEV * sec_rows, HEAD), F32), jax.ShapeDtypeStruct((N_DEV * INP_PACK_ROWS, HEAD), F32)),
        scratch_shapes=[pltpu.VMEM((INP_PACK_ROWS, HEAD), F32), pltpu.VMEM((n_pieces, sec_rows, HEAD), F32),
                        pltpu.VMEM((sec_rows, HEAD), F32),
                        pltpu.SemaphoreType.DMA((nm, 3)), pltpu.SemaphoreType.DMA((nm, 3)),
                        pltpu.SemaphoreType.DMA((n_pieces,))] + _gather_sems(1)
        + [pltpu.SemaphoreType.DMA((N_DEV - 1,)), pltpu.SemaphoreType.DMA((N_DEV - 1,)), pltpu.SemaphoreType.DMA],
        compiler_params=pltpu.CompilerParams(dimension_semantics=("arbitrary",), vmem_limit_bytes=VMEM_LIMIT),
    )(dz, x, dx1, win, g0, small, sib_small, *chip_srcs)


ROW_BLOCKS = 8


def _chip_sums(core_idx, grads, recvs, dtypes, name):
    nw = len(grads)

    def body(idx_ref, *refs):
        gs, rs, outs = refs[:nw], refs[nw:2 * nw], refs[2 * nw:]
        for g, r, o in zip(gs, rs, outs):
            o[...] = (g[...] + r[...]).astype(o.dtype)

    def blk(g):
        return (None,) + g.shape[1:]

    return pl.pallas_call(
        body, name=name,
        grid_spec=pltpu.PrefetchScalarGridSpec(
            num_scalar_prefetch=1, grid=(4,),
            in_specs=[pl.BlockSpec(blk(g), lambda q, s: (2 * q + s[0], 0, 0)) for g in grads]
            + [pl.BlockSpec(blk(g), lambda q, s: (q, 0, 0)) for g in grads],
            out_specs=tuple(pl.BlockSpec(blk(g), lambda q, s: (q, 0, 0)) for g in grads)),
        out_shape=tuple(jax.ShapeDtypeStruct((4,) + g.shape[1:], dt) for g, dt in zip(grads, dtypes)),
    )(core_idx, *grads, *recvs)


def _adamw(w, g, m, v):
    m = ADAM_B1 * m + (1.0 - ADAM_B1) * g
    v = ADAM_B2 * v + (1.0 - ADAM_B2) * (g * g)
    m_hat = m / (1.0 - ADAM_B1 ** ADAM_STEP)
    v_hat = v / (1.0 - ADAM_B2 ** ADAM_STEP)
    delta = -ADAM_LR * (m_hat / (jnp.sqrt(v_hat) + ADAM_EPS) + ADAM_WD * w)
    return delta, m, v


def _adam_sections(sec_idx, grads, recv_sib, recv_chips, wmv):
    nw = len(wmv)

    def body(idx_ref, *refs):
        gs, rs, cs = refs[:nw], refs[nw:2 * nw], refs[2 * nw:3 * nw]
        params = refs[3 * nw:6 * nw]
        outs = refs[6 * nw:]
        for w in range(nw):
            g = gs[w][...] + rs[w][...]
            for k in range(3):
                g = g + cs[w][k].astype(F32)
            wv, mv, vv = (params[3 * w + t][...] for t in range(3))
            delta, m_new, v_new = _adamw(wv, g, mv, vv)
            outs[4 * w][...] = g
            outs[4 * w + 1][...] = delta
            outs[4 * w + 2][...] = m_new
            outs[4 * w + 3][...] = v_new

    def rb(g):
        return g.shape[1] // ROW_BLOCKS

    in_specs = [pl.BlockSpec((None, rb(g), g.shape[2]), lambda i, s: (s[0], i, 0)) for g in grads]
    in_specs += [pl.BlockSpec((None, rb(g), g.shape[2]), lambda i, s: (s[1], i, 0)) for g in grads]
    in_specs += [pl.BlockSpec((3, rb(g), g.shape[2]), lambda i, s: (0, i, 0)) for g in grads]
    sec = lambda g: pl.BlockSpec((rb(g), g.shape[2]), lambda i, s: (i, 0))
    for w in range(nw):
        in_specs += [sec(grads[w])] * 3
    out_specs, out_shape = [], []
    for g in grads:
        out_specs += [sec(g)] * 4
        out_shape += [jax.ShapeDtypeStruct(g.shape[1:], F32)] * 4
    flat = [a for t in wmv for a in t]
    outs = pl.pallas_call(
        body, name="adam_sections",
        grid_spec=pltpu.PrefetchScalarGridSpec(num_scalar_prefetch=1, grid=(ROW_BLOCKS,), in_specs=in_specs,
                                               out_specs=tuple(out_specs)),
        out_shape=tuple(out_shape),
    )(sec_idx, *grads, *recv_sib, *recv_chips, *flat)
    return [outs[4 * w:4 * w + 4] for w in range(nw)]


def _adam_small(sec_idx, gpack, parts, parts_wmv, vec_params, mat_params, conv_wmv):
    items = [(0, parts_wmv, "parts")]
    items += [(r0, t, "vec") for r0, t in vec_params] + [(r0, t, "mat") for r0, t in mat_params]
    items.append((_SEQ_ROWS["dcw"][0], conv_wmv, "conv"))

    def body(idx_ref, g_ref, parts_ref, *refs):
        ins, outs = refs[:3 * len(items)], refs[3 * len(items):]
        for n, (r0, _, kind) in enumerate(items):
            w_ref, m_ref, v_ref = ins[3 * n:3 * n + 3]
            o = outs[4 * n:4 * n + 4]
            if kind == "mat":
                g = g_ref[pl.ds(r0, w_ref.shape[0]), :]
                res = (g,) + _adamw(w_ref[...], g, m_ref[...], v_ref[...])
                for ref, val in zip(o, res):
                    ref[...] = val
            elif kind in ("vec", "parts"):
                for h in range(w_ref.shape[1] // HEAD):
                    cols = slice(h * HEAD, (h + 1) * HEAD)
                    if kind == "vec":
                        g = g_ref[pl.ds(r0 + h, 1), :]
                    else:
                        g = parts_ref[pl.ds(h, 1), :]
                        for dev in range(1, N_DEV):
                            g = g + parts_ref[pl.ds(SUB * dev + h, 1), :]
                    res = (g,) + _adamw(w_ref[:, cols], g, m_ref[:, cols], v_ref[:, cols])
                    for ref, val in zip(o, res):
                        ref[:, cols] = val
            else:
                rows = lax.broadcasted_iota(jnp.int32, (SUB, HEAD), 0)
                for k in range(4):
                    blk = g_ref[pl.ds(r0 + SUB * k, SUB), :]
                    g = jnp.sum(jnp.where(rows == idx_ref[0], blk, 0.0), axis=0, keepdims=True)
                    row = pl.ds(k, 1)
                    res = (g,) + _adamw(w_ref[row, :], g, m_ref[row, :], v_ref[row, :])
                    for ref, val in zip(o, res):
                        ref[row, :] = val
        outs[4 * len(items)][...] = g_ref[pl.ds(_OFF_MID + MID_LOSS_ROW, 1), :]

    flat = [a for _, t, _ in items for a in t]
    full = lambda a: pl.BlockSpec(a.shape, lambda i, s: (0,) * a.ndim)
    out_specs, out_shape = [], []
    for _, t, _ in items:
        out_specs += [full(t[0])] * 4
        out_shape += [jax.ShapeDtypeStruct(t[0].shape, F32)] * 4
    loss_row = jax.ShapeDtypeStruct((1, HEAD), F32)
    out_specs.append(full(loss_row))
    out_shape.append(loss_row)
    outs = pl.pallas_call(
        body, name="adam_small",
        grid_spec=pltpu.PrefetchScalarGridSpec(num_scalar_prefetch=1, grid=(1,),
                                               in_specs=[full(gpack), full(parts)] + [full(a) for a in flat],
                                               out_specs=tuple(out_specs)),
        out_shape=tuple(out_shape),
    )(sec_idx, gpack, parts, *flat)
    return [outs[4 * n:4 * n + 4] for n in range(len(items))], outs[4 * len(items)]


def _col_slicer(width):
    return lambda ref, idx: ref.at[:, pl.ds(pl.multiple_of(idx * width, HEAD), width)]


def _row_slicer(rows):
    return lambda ref, idx: ref.at[pl.ds(pl.multiple_of(idx * rows, SUB), rows), :]


def kernel(x, p, norm_g, w_in, conv_w, conv_b, lru_w_a, lru_b_a, lru_w_x, lru_b_x, lru_lambda, pool_w, pool_scale, w_proj_lru, w_proj_pool, w_out, ple_norm_g, w_ple_gate, w_ple_proj, final_g, loss_target, m_norm_g, m_w_in, m_conv_w, m_conv_b, m_lru_w_a, m_lru_b_a, m_lru_w_x, m_lru_b_x, m_lru_lambda, m_pool_w, m_pool_scale, m_w_proj_lru, m_w_proj_pool, m_w_out, m_ple_norm_g, m_w_ple_gate, m_w_ple_proj, m_final_g, v_norm_g, v_w_in, v_conv_w, v_conv_b, v_lru_w_a, v_lru_b_a, v_lru_w_x, v_lru_b_x, v_lru_lambda, v_pool_w, v_pool_scale, v_w_proj_lru, v_w_proj_pool, v_w_out, v_ple_norm_g, v_w_ple_gate, v_w_ple_proj, v_final_g):
    nb, s, d = x.shape
    t = nb * s
    tm = min(TILE_M, s)
    sec_idx = (4 * lax.axis_index("x") + 2 * lax.axis_index("y") + lax.axis_index("c")).astype(jnp.int32)
    chip_idx = (2 * lax.axis_index("x") + lax.axis_index("y")).astype(jnp.int32)
    core_idx = lax.axis_index("c").astype(jnp.int32)

    h, ht, win, cw = _pre(
        x, norm_g, [w_in[0], conv_w[0]], [True, False],
        [(d, N_DEV * w_in.shape[2]), (conv_w.shape[1], N_DEV * conv_w.shape[2])],
        [_col_slicer(w_in.shape[2]), _col_slicer(conv_w.shape[2])], tm)
    late_shards = [w_proj_lru[0], w_proj_pool[0], w_out[0], w_ple_gate[0], w_ple_proj[0]]
    late_shapes = [(N_DEV * w_proj_lru.shape[1], d), (w_proj_pool.shape[1], N_DEV * w_proj_pool.shape[2]),
                   (N_DEV * w_out.shape[1], d), (N_DEV * w_ple_gate.shape[1], d),
                   (w_ple_proj.shape[1], N_DEV * w_ple_proj.shape[2])]
    late_slicers = [_row_slicer(w_proj_lru.shape[1]), _col_slicer(w_proj_pool.shape[2]), _row_slicer(w_out.shape[1]),
                    _row_slicer(w_ple_gate.shape[1]), _col_slicer(w_ple_proj.shape[2])]

    lw = cw.shape[1]
    pwid = pool_scale.shape[1]
    wa = lru_w_a[0].astype(_MM)
    wx = lru_w_x[0].astype(_MM)
    pw = pool_w[0].astype(_MM)
    ba, bx = lru_b_a[0], lru_b_x[0]
    gf = final_g.reshape(1, d)
    rows_of = lambda g: g.reshape(N_DEV, g.shape[0] // N_DEV, g.shape[1])
    core = core_idx.reshape(1)

    fwd_out = _fwd_seq(
        h, win, cw, conv_b, wa, ba, wx, bx, lru_lambda, pw, pool_scale, late_shards, late_shapes, late_slicers, tm)
    z, hl, ya, yb = fwd_out[:4]
    saved = fwd_out[4:N_SEQ_SAVED]
    wpl, wpp, wout, wpg, wpe = fwd_out[N_SEQ_SAVED:]
    x2d = x.reshape(t, d)
    (dya, dyb, dm, dx1, g_pl, g_pp, g_out, g_pg, g_pe, pack_mid) = _mid(
        ya.reshape(t, lw), yb.reshape(t, pwid), h.reshape(t, d), win, x2d, p[0].reshape(t, -1),
        loss_target.reshape(t, d), wpl, wpp, wout, wpg, wpe, ple_norm_g, gf, tm)
    grads_mid = [rows_of(g_pl), g_pp, rows_of(g_out), rows_of(g_pg), g_pe]
    seq_out = _bwd_seq(z, hl, saved, dya.reshape(nb, s, lw), dyb.reshape(nb, s, pwid), dm.reshape(nb, s, -1), cw,
                       wa, wx, lru_lambda, pw, pool_scale, pack_mid, grads_mid, tm)
    dz, sib_mid = seq_out[0], list(seq_out[2:])
    small = seq_out[1].reshape(N_DEV, SMALL_SEC, HEAD)
    sums_mid = _chip_sums(core, grads_mid, sib_mid, [_WIRE] * len(grads_mid), "chip_sums_mid")
    win_out = _bwd_win(ht, dz, sums_mid, small)
    g_in, sib_in, chips_mid, sib_small = win_out[0], win_out[1], list(win_out[2:-1]), win_out[-1]
    sums_small, sums_in = _chip_sums(core, [small, g_in], [sib_small, sib_in], [F32, _WIRE], "chip_sums_in")
    grad_x, _, chips_in, gsmall, g0_parts = _bwd_dx(
        dz.reshape(t, -1), x2d, dx1, win, norm_g, small, sib_small, [sums_small, sums_in], tm)

    grads = [g_in] + grads_mid
    recv_sib = [sib_in] + sib_mid
    recv_chips = [chips_in] + chips_mid
    wmv = [(w_in[0], m_w_in[0], v_w_in[0]), (w_proj_lru[0], m_w_proj_lru[0], v_w_proj_lru[0]),
           (w_proj_pool[0], m_w_proj_pool[0], v_w_proj_pool[0]), (w_out[0], m_w_out[0], v_w_out[0]),
           (w_ple_gate[0], m_w_ple_gate[0], v_w_ple_gate[0]), (w_ple_proj[0], m_w_ple_proj[0], v_w_ple_proj[0])]
    big = _adam_sections(jnp.stack([sec_idx, chip_idx]), grads, recv_sib, recv_chips, wmv)

    R = _SEQ_ROWS
    vec = lambda r0, *t: (r0, tuple(a.reshape(1, -1) for a in t))
    mat = lambda r0, *t: (r0, tuple(a.reshape(-1, HEAD) for a in t))
    norm_wmv = vec(0, norm_g, m_norm_g, v_norm_g)[1]
    vec_params = [vec(R["dcb"][0], conv_b, m_conv_b, v_conv_b),
                  vec(R["dlam"][0], lru_lambda, m_lru_lambda, v_lru_lambda),
                  vec(R["dps"][0], pool_scale, m_pool_scale, v_pool_scale),
                  vec(_OFF_MID, ple_norm_g, m_ple_norm_g, v_ple_norm_g),
                  vec(_OFF_MID + SUB, final_g, m_final_g, v_final_g)]
    mat_params = [mat(R["dwa"][0], lru_w_a, m_lru_w_a, v_lru_w_a), mat(R["dba"][0], lru_b_a, m_lru_b_a, v_lru_b_a),
                  mat(R["dwx"][0], lru_w_x, m_lru_w_x, v_lru_w_x), mat(R["dbx"][0], lru_b_x, m_lru_b_x, v_lru_b_x),
                  mat(R["dpw"][0], pool_w, m_pool_w, v_pool_w)]
    conv_wmv = (conv_w[0], m_conv_w[0], v_conv_w[0])
    small_out, loss_row = _adam_small(sec_idx.reshape(1), gsmall, g0_parts, norm_wmv, vec_params, mat_params, conv_wmv)
    loss = loss_row[0, 0]

    res = {}
    names_small = ["norm_g", "conv_b", "lru_lambda", "pool_scale", "ple_norm_g", "final_g",
                   "lru_w_a", "lru_b_a", "lru_w_x", "lru_b_x", "pool_w", "conv_w"]
    shapes = {"norm_g": norm_g, "conv_b": conv_b, "lru_lambda": lru_lambda, "pool_scale": pool_scale,
              "ple_norm_g": ple_norm_g, "final_g": final_g, "lru_w_a": lru_w_a, "lru_b_a": lru_b_a, "lru_w_x": lru_w_x,
              "lru_b_x": lru_b_x, "pool_w": pool_w, "conv_w": conv_w}
    for name, quad in zip(names_small, small_out):
        res[name] = [a.reshape(shapes[name].shape) for a in quad]
    names_big = ["w_in", "w_proj_lru", "w_proj_pool", "w_out", "w_ple_gate", "w_ple_proj"]
    for name, quad, (w, _, _) in zip(names_big, big, wmv):
        res[name] = [a.reshape((1,) + w.shape) for a in quad]
    order = ["norm_g", "w_in", "conv_w", "conv_b", "lru_w_a", "lru_b_a", "lru_w_x", "lru_b_x", "lru_lambda", "pool_w",
             "pool_scale", "w_proj_lru", "w_proj_pool", "w_out", "ple_norm_g", "w_ple_gate", "w_ple_proj", "final_g"]
    out = [loss, grad_x.reshape(nb, s, d)]
    for kind in range(4):
        out += [res[name][kind] for name in order]
    return tuple(out)
```

```python
import jax
import jax.numpy as jnp
from jax import lax
from jax.experimental import pallas as pl
from jax.experimental.pallas import tpu as pltpu

F32 = jnp.float32
_MM = jnp.bfloat16
_WIRE = jnp.bfloat16

EPS = 1e-6
LRU_C = 8.0
MULT_SQ_FLOOR = 1e-30
POOL_WINDOWS = (2, 4, 8, 16)
N_HEADS = 8
HEAD = 128
HALO = 16
SUB = 8

ADAM_LR = 0.001
ADAM_B1 = 0.9
ADAM_B2 = 0.999
ADAM_EPS = 1e-08
ADAM_WD = 0.01
ADAM_STEP = 10

N_DEV = 8
MESH = pl.DeviceIdType.MESH
VMEM_LIMIT = 60 * 1024 * 1024
TILE_M = 256
N_SEQ_SAVED = 10

_SEQ_ROWS = {"dwa": (0, 1024), "dwx": (1024, 1024), "dpw": (2048, 512), "dba": (2560, 8), "dbx": (2568, 8),
             "dcb": (2576, 8), "dlam": (2584, 8), "dcw": (2592, 32), "dps": (2624, 8)}
SEQ_PACK_ROWS = 2632
MID_PACK_ROWS = 24
MID_LOSS_ROW = 16
INP_PACK_ROWS = 8
SMALL_ROWS = 3072
SMALL_SEC = SMALL_ROWS // N_DEV
_OFF_MID = SEQ_PACK_ROWS


def _dot(a, b):
    return jnp.dot(a, b, preferred_element_type=F32)


def _dot_nt(a, b):
    return lax.dot_general(a, b, (((1,), (1,)), ((), ())), preferred_element_type=F32)


def _dot_tn(a, b):
    return lax.dot_general(a, b, (((0,), (0,)), ((), ())), preferred_element_type=F32)


def _sigmoid(v):
    return 0.5 * jnp.tanh(0.5 * v) + 0.5


def _lru_mult(log_a, a):
    m2 = jnp.maximum(-jnp.tanh(log_a) * (1.0 + a * a), MULT_SQ_FLOOR)
    inv = lax.rsqrt(m2)
    return m2 * inv, inv


def _rows_back(v, n):
    return pltpu.roll(v, n, 0) if n else v


def _rows_ahead(v, n):
    return pltpu.roll(v, v.shape[0] - n, 0) if n else v


def _softplus_neg(lam):
    e = jnp.exp(-jnp.abs(lam))
    w = 1.0 + e
    l1p = jnp.where(w == 1.0, e, jnp.log(w) * (e / (w - 1.0)))
    return jnp.maximum(-lam, 0.0) + l1p


def _rsqrt_mean_sq(v):
    return lax.rsqrt(jnp.mean(v * v, axis=-1, keepdims=True) + EPS)


def _colsum(v):
    return jnp.sum(v, axis=0, keepdims=True)


def _const_spec(shape):
    nd = len(shape)
    return pl.BlockSpec(shape, lambda *_: (0,) * nd, pipeline_mode=pl.Buffered(1))


HBM_SPEC = pl.BlockSpec(memory_space=pl.ANY)


def _my_pos():
    return lax.axis_index("x"), lax.axis_index("y"), lax.axis_index("c")


def _other_chips(x, y):
    return [(1 - x, y), (x, 1 - y), (1 - x, 1 - y)]


def _gather_sems(nw):
    return [pltpu.SemaphoreType.DMA((nw, 7)), pltpu.SemaphoreType.DMA((nw, 7)), pltpu.SemaphoreType.DMA((nw,))]


def _gather_phases(srcs, outs, slicers, send_sems, recv_sems, local_sems, relay_diagonal=False):
    n_direct = 2 if relay_diagonal else 3
    nw = len(srcs)
    x, y, c = _my_pos()
    me, sibling = (x, y, c), (x, y, 1 - c)
    chips = _other_chips(x, y)

    def part(w, pos):
        return slicers[w](outs[w], 4 * pos[0] + 2 * pos[1] + pos[2])

    def copy(w, k, block, to, src=None):
        return pltpu.make_async_remote_copy(
            src_ref=part(w, block) if src is None else src, dst_ref=part(w, block),
            send_sem=send_sems.at[w, k], recv_sem=recv_sems.at[w, k], device_id=to, device_id_type=MESH)

    def mine():
        return [pltpu.make_async_copy(srcs[w], part(w, me), local_sems.at[w]) for w in range(nw)]

    def first():
        out = []
        for w in range(nw):
            out.append(copy(w, 0, me, sibling, src=srcs[w]))
            out += [copy(w, 1 + j, me, (*chips[j], c), src=srcs[w]) for j in range(n_direct)]
        return out

    def relay(w, j):
        return copy(w, 3, (*chips[j], c), (*chips[1 - j], c))

    def relays():
        return [relay(w, 0) for w in range(nw)] if relay_diagonal else []

    def passed():
        return [copy(w, 4 + j, (*chip, c), sibling) for j, chip in enumerate(chips) for w in range(nw)]

    def start():
        for cp in mine() + first():
            cp.start()

    def forward():
        for j, chip in enumerate(chips):
            for w in range(nw):
                copy(w, 1 + j, (*chip, c), me).wait_recv()
                copy(w, 4 + j, (*chip, c), sibling).start()
            if relay_diagonal and j < 2:
                @pl.when(c == j)
                def _():
                    for w in range(nw):
                        relay(w, j).start()

    def finish():
        for w in range(nw):
            copy(w, 0, sibling, me).wait_recv()
            for j, chip in enumerate(chips):
                copy(w, 4 + j, (*chip, 1 - c), me).wait_recv()
        for cp in first() + relays() + passed():
            cp.wait_send()
        for cp in mine():
            cp.wait()

    return start, forward, finish


def _sibling_copies(srcs, outs, send_sems, recv_sems):
    x, y, c = _my_pos()
    return [pltpu.make_async_remote_copy(
        src_ref=srcs[w].at[2 * q + (1 - c)], dst_ref=outs[w].at[q], send_sem=send_sems.at[w, q],
        recv_sem=recv_sems.at[w, q], device_id=(x, y, 1 - c), device_id_type=MESH)
        for w in range(len(srcs)) for q in range(4)]


def _chip_copies(srcs, outs, send_sems, recv_sems, which=None):
    x, y, c = _my_pos()
    return [pltpu.make_async_remote_copy(
        src_ref=srcs[w].at[2 * px + py], dst_ref=outs[w].at[k], send_sem=send_sems.at[w, k],
        recv_sem=recv_sems.at[w, k], device_id=(px, py, c), device_id_type=MESH)
        for w in (range(len(srcs)) if which is None else which) for k, (px, py) in enumerate(_other_chips(x, y))]


def _lru_gates(xc, h, wa_ref, ba_ref, wx_ref, bx_ref, neg_c_sp):
    xc16 = xc.astype(_MM)
    r = _sigmoid(_dot(xc16, wa_ref[h]) + ba_ref[pl.ds(h, 1), :])
    i = _sigmoid(_dot(xc16, wx_ref[h]) + bx_ref[pl.ds(h, 1), :])
    return r, i, neg_c_sp * r


def _conv_head(xa_ext, cw_ref, cb_ref, cols, tm):
    ext = xa_ext[:, cols]
    xc = cb_ref[:, cols] + cw_ref[pl.ds(3, 1), cols] * ext[HALO:HALO + tm]
    for k in range(3):
        xc = xc + cw_ref[pl.ds(k, 1), cols] * _rows_back(ext, 3 - k)[HALO:HALO + tm]
    return xc


def _window_sum(ext, k, shift):
    n = 1
    while n < k:
        ext = ext + shift(ext, n)
        n *= 2
    return ext


def _pool_diff(xb_ext, g, k, pos, tm):
    cols = slice(g * HEAD, (g + 1) * HEAD)
    ext = xb_ext[:, cols]
    ws = _window_sum(ext, k, _rows_back)[HALO:HALO + tm]
    inv_cnt = 1.0 / jnp.minimum(pos + 1, k).astype(F32)
    return ws * inv_cnt - ext[HALO:HALO + tm], inv_cnt


def _staged_sources(shards, as_operand):
    in_specs = [_const_spec(a.shape) if c else HBM_SPEC for a, c in zip(shards, as_operand)]
    stages = [pltpu.VMEM(a.shape, _MM) for a, c in zip(shards, as_operand) if c]
    dtypes = [_MM if c else a.dtype for a, c in zip(shards, as_operand)]
    return in_specs, stages, dtypes


def _gather_sources(srcs, stages, as_operand):
    stages = iter(stages)
    return [next(stages) if c else src for src, c in zip(srcs, as_operand)]


def _fill_stages(srcs, staged, as_operand):
    for src, dst, c in zip(srcs, staged, as_operand):
        if c:
            dst[...] = src[...].astype(dst.dtype)


def _pre(x, g0, shards, as_operand, shapes, slicers, tm):
    nb, s, d = x.shape
    nt = s // tm
    nl = len(shards)
    shard_specs, stage_shapes, wire = _staged_sources(shards, as_operand)

    def body(x_ref, g0_ref, *rest):
        src, rest = rest[:nl], rest[nl:]
        h_ref, ht_ref = rest[:2]
        out, rest = rest[2:2 + nl], rest[2 + nl:]
        (send_sems, recv_sems, local_sems), stages = rest[:3], rest[3:]
        step_no = pl.program_id(0) * nt + pl.program_id(1)
        staged = _gather_sources(src, stages, as_operand)
        g_start, g_forward, g_finish = _gather_phases(staged, out, slicers, send_sems, recv_sems, local_sems,
                                                      relay_diagonal=True)

        @pl.when(step_no == 0)
        def _():
            _fill_stages(src, staged, as_operand)
            g_start()

        xv = x_ref[...]
        h16 = ((xv * _rsqrt_mean_sq(xv)) * g0_ref[...]).astype(_MM)
        h_ref[...] = h16
        ht_ref[...] = h16.T

        @pl.when(step_no == nb * nt - 1)
        def _():
            g_forward()
            g_finish()

    tile = pl.BlockSpec((None, tm, d), lambda b, j: (b, j, 0))
    return pl.pallas_call(
        body, name="pre", grid=(nb, nt),
        in_specs=[tile, _const_spec(g0.shape)] + shard_specs,
        out_specs=(tile, pl.BlockSpec((None, d, tm), lambda b, j: (b, 0, j))) + (HBM_SPEC,) * nl,
        out_shape=(jax.ShapeDtypeStruct((nb, s, d), _MM), jax.ShapeDtypeStruct((nb, d, s), _MM))
        + tuple(jax.ShapeDtypeStruct(shp, dt) for shp, dt in zip(shapes, wire)),
        scratch_shapes=_gather_sems(nl) + stage_shapes,
        compiler_params=pltpu.CompilerParams(dimension_semantics=("arbitrary", "arbitrary")),
    )(x, g0, *shards)


def _fwd_seq(h, win, cw, cb, wa, ba, wx, bx, lam, pw, ps, late_shards, late_shapes, late_slicers, tm):
    nb, s, d = h.shape
    lw = cw.shape[1]
    pwid = ps.shape[1]
    nc = 2 * lw + 2 * pwid
    nt = s // tm
    nl = len(late_shards)
    n_steps = nb * nt
    as_operand = [True] * nl
    shard_specs, stage_shapes, wire = _staged_sources(late_shards, as_operand)

    def body(h_ref, win_ref, cw_ref, cb_ref, wa_ref, ba_ref, wx_ref, bx_ref, lam_ref, pw_ref, ps_ref, *rest):
        late_src, rest = rest[:nl], rest[nl:]
        z_ref, hl_ref, ya_ref, yb_ref, xc_ref, r_ref, i_ref, a_buf, diff_ref, yp_ref = rest[:N_SEQ_SAVED]
        late_out, rest = rest[N_SEQ_SAVED:N_SEQ_SAVED + nl], rest[N_SEQ_SAVED + nl:]
        (xa_ext, xb_ext, carry, send_sems, recv_sems, local_sems), stages = rest[:6], rest[6:]
        j = pl.program_id(1)
        step_no = pl.program_id(0) * nt + j
        staged = _gather_sources(late_src, stages, as_operand)
        g_start, g_forward, g_finish = _gather_phases(staged, late_out, late_slicers, send_sems, recv_sems, local_sems)

        @pl.when(step_no == 0)
        def _():
            _fill_stages(late_src, staged, as_operand)
            g_start()

        pl.when(step_no == n_steps // 2)(g_forward)

        @pl.when(j == 0)
        def _():
            xa_ext[pl.ds(0, HALO), :] = jnp.zeros((HALO, lw), F32)
            xb_ext[pl.ds(0, HALO), :] = jnp.zeros((HALO, pwid), F32)
            carry[...] = jnp.zeros_like(carry)

        h16 = h_ref[...]
        xa = _dot(h16, win_ref[:, 0:lw])
        z_ref[:, 0:lw] = xa
        xa_ext[pl.ds(HALO, tm), :] = xa
        for c0 in range(lw, nc, lw):
            z_ref[:, c0:c0 + lw] = _dot(h16, win_ref[:, c0:c0 + lw])

        for h in range(N_HEADS):
            cols = slice(h * HEAD, (h + 1) * HEAD)
            xc = _conv_head(xa_ext, cw_ref, cb_ref, cols, tm)
            neg_c_sp = -LRU_C * _softplus_neg(lam_ref[:, cols])
            r, i, log_a = _lru_gates(xc, h, wa_ref, ba_ref, wx_ref, bx_ref, neg_c_sp)
            a = jnp.exp(log_a)
            xc_ref[:, cols] = xc
            r_ref[:, cols] = r
            i_ref[:, cols] = i
            a_buf[:, cols] = a
            hl_ref[:, cols] = _lru_mult(log_a, a)[0] * (i * xc)

        rows = lax.broadcasted_iota(jnp.int32, (SUB, lw), 0)

        def step(c, car):
            i0 = pl.multiple_of(c * SUB, SUB)
            av = a_buf[pl.ds(i0, SUB), :]
            bv = hl_ref[pl.ds(i0, SUB), :]
            for sh in (1, 2, 4):
                m = rows >= sh
                a_sh = jnp.where(m, pltpu.roll(av, sh, 0), 1.0)
                b_sh = jnp.where(m, pltpu.roll(bv, sh, 0), 0.0)
                bv = av * b_sh + bv
                av = av * a_sh
            hv = av * car + bv
            hl_ref[pl.ds(i0, SUB), :] = hv
            return jnp.broadcast_to(hv[SUB - 1:SUB, :], (SUB, lw))

        carry[...] = lax.fori_loop(0, tm // SUB, step, carry[...])

        ga = z_ref[:, lw:2 * lw]
        ya_ref[...] = (hl_ref[...] * (ga * _sigmoid(ga))).astype(ya_ref.dtype)

        xb_ext[pl.ds(HALO, tm), :] = z_ref[:, 2 * lw:2 * lw + pwid]
        pos = j * tm + lax.broadcasted_iota(jnp.int32, (tm, HEAD), 0)
        for g, k in enumerate(POOL_WINDOWS):
            cols = slice(g * HEAD, (g + 1) * HEAD)
            diff16 = _pool_diff(xb_ext, g, k, pos, tm)[0].astype(_MM)
            yp = _dot(diff16, pw_ref[g])
            diff_ref[:, cols] = diff16
            yp_ref[:, cols] = yp
            gb = z_ref[:, 2 * lw + pwid + g * HEAD:2 * lw + pwid + (g + 1) * HEAD]
            yb_ref[:, cols] = ((yp * ps_ref[:, cols]) * (gb * _sigmoid(gb))).astype(yb_ref.dtype)

        xa_ext[pl.ds(0, HALO), :] = xa_ext[pl.ds(tm, HALO), :]
        xb_ext[pl.ds(0, HALO), :] = xb_ext[pl.ds(tm, HALO), :]
        pl.when(step_no == n_steps - 1)(g_finish)

    tile = lambda w: pl.BlockSpec((None, tm, w), lambda b, j: (b, j, 0))
    return pl.pallas_call(
        body, name="fwd_seq", grid=(nb, nt),
        in_specs=[tile(d), _const_spec((d, nc)), _const_spec(cw.shape), _const_spec(cb.shape),
                  _const_spec(wa.shape), _const_spec(ba.shape), _const_spec(wx.shape), _const_spec(bx.shape),
                  _const_spec(lam.shape), _const_spec(pw.shape), _const_spec(ps.shape)] + shard_specs,
        out_specs=(tile(nc), tile(lw), tile(lw), tile(pwid), tile(lw), tile(lw), tile(lw), tile(lw), tile(pwid),
                   tile(pwid)) + (HBM_SPEC,) * nl,
        out_shape=(jax.ShapeDtypeStruct((nb, s, nc), F32), jax.ShapeDtypeStruct((nb, s, lw), F32),
                   jax.ShapeDtypeStruct((nb, s, lw), _MM), jax.ShapeDtypeStruct((nb, s, pwid), _MM))
        + (jax.ShapeDtypeStruct((nb, s, lw), F32),) * 4
        + (jax.ShapeDtypeStruct((nb, s, pwid), _MM), jax.ShapeDtypeStruct((nb, s, pwid), F32))
        + tuple(jax.ShapeDtypeStruct(shp, dt) for shp, dt in zip(late_shapes, wire)),
        scratch_shapes=[pltpu.VMEM((tm + HALO, lw), F32), pltpu.VMEM((tm + HALO, pwid), F32),
                        pltpu.VMEM((SUB, lw), F32)] + _gather_sems(nl) + stage_shapes,
        compiler_params=pltpu.CompilerParams(dimension_semantics=("arbitrary", "arbitrary"), vmem_limit_bytes=VMEM_LIMIT),
    )(h, win, cw, cb, wa, ba, wx, bx, lam, pw, ps, *late_shards)


def _store_chunks(ref, row0, vec):
    for h in range(vec.shape[1] // HEAD):
        ref[pl.ds(row0 + h, 1), :] += vec[:, h * HEAD:(h + 1) * HEAD]


def _mid(ya, yb, h, win, x, p, tgt, wpl, wpp, wout, wpg, wpe, g1, gf, tm):
    t, d = x.shape
    lw = ya.shape[1]
    pwid = yb.shape[1]
    pdim = p.shape[1]
    n = t // tm
    sec_pp = wpp.shape[1] // N_DEV
    sec_pe = wpe.shape[1] // N_DEV
    n_win_blocks = win.shape[1] // d
    win_cols = lambda c: pl.BlockSpec((d, d), lambda i: (0, c), pipeline_mode=pl.Buffered(1))

    def body(ya_ref, yb_ref, h_ref, wma_ref, wmb_ref, x_ref, p_ref, tgt_ref, wpl_ref, wpp_ref, wout_ref, wpg_ref, wpe_ref,
             g1_ref, gf_ref,
             dya_ref, dyb_ref, dm_ref, dx1_ref, gpl_ref, gpp_ref, gout_ref, gpg_ref, gpe_ref, pack_ref,
             acc_pl, acc_pp, acc_out, acc_pg, acc_pe):
        i = pl.program_id(0)

        @pl.when(i == 0)
        def _():
            for acc in (acc_pl, acc_pp, acc_out, acc_pg, acc_pe):
                acc[...] = jnp.zeros_like(acc)
            pack_ref[...] = jnp.zeros_like(pack_ref)

        ya16 = ya_ref[...]
        yb16 = yb_ref[...]
        br_a = _dot(ya16, wpl_ref[...])
        br_b = _dot(yb16, wpp_ref[...])
        h16 = h_ref[...]
        sa = _sigmoid(_dot(h16, wma_ref[...]))
        sb = _sigmoid(_dot(h16, wmb_ref[...]))
        mg16 = (sa * br_a + sb * br_b).astype(_MM)
        x1 = x_ref[...] + _dot(mg16, wout_ref[...])
        r1 = _rsqrt_mean_sq(x1)
        n1 = x1 * r1
        h116 = (n1 * g1_ref[...]).astype(_MM)
        gate = _sigmoid(_dot(h116, wpg_ref[...]))
        p16 = p_ref[...].astype(_MM)
        pe = _dot(p16, wpe_ref[...])
        x2 = x1 + gate * pe
        r2 = _rsqrt_mean_sq(x2)
        n2 = x2 * r2
        err = n2 * gf_ref[...] - tgt_ref[...]
        sq = jnp.sum(_colsum(err * err), axis=1, keepdims=True)
        pack_ref[pl.ds(MID_LOSS_ROW, 1), :] += jnp.broadcast_to(sq * (0.5 / d), (1, HEAD))

        dy = err * (1.0 / d)
        _store_chunks(pack_ref, SUB, _colsum(dy * n2))
        dn2 = dy * gf_ref[...]
        dx2 = r2 * (dn2 - n2 * jnp.mean(dn2 * n2, axis=-1, keepdims=True))
        dpe16 = (dx2 * gate).astype(_MM)
        dpg16 = ((dx2 * pe) * (gate * (1.0 - gate))).astype(_MM)
        acc_pe[...] += _dot_tn(p16, dpe16)
        acc_pg[...] += _dot_tn(h116, dpg16)
        dh1 = _dot_nt(dpg16, wpg_ref[...])
        _store_chunks(pack_ref, 0, _colsum(dh1 * n1))
        dn1 = dh1 * g1_ref[...]
        dx1 = dx2 + r1 * (dn1 - n1 * jnp.mean(dn1 * n1, axis=-1, keepdims=True))
        dx1_ref[...] = dx1
        dx116 = dx1.astype(_MM)
        acc_out[...] += _dot_tn(mg16, dx116)
        dmg = _dot_nt(dx116, wout_ref[...])
        da16 = (dmg * sa).astype(_MM)
        db16 = (dmg * sb).astype(_MM)
        dm_ref[:, 0:d] = ((dmg * br_a) * (sa * (1.0 - sa))).astype(dm_ref.dtype)
        dm_ref[:, d:2 * d] = ((dmg * br_b) * (sb * (1.0 - sb))).astype(dm_ref.dtype)
        acc_pl[...] += _dot_tn(ya16, da16)
        acc_pp[...] += _dot_tn(yb16, db16)
        dya_ref[...] = _dot_nt(da16, wpl_ref[...])
        dyb_ref[...] = _dot_nt(db16, wpp_ref[...])

        @pl.when(i == n - 1)
        def _():
            pltpu.sync_copy(acc_pl, gpl_ref)
            pltpu.sync_copy(acc_out, gout_ref)
            pltpu.sync_copy(acc_pg, gpg_ref)
            for k in range(N_DEV):
                pltpu.sync_copy(acc_pp.at[:, pl.ds(k * sec_pp, sec_pp)], gpp_ref.at[k])
                pltpu.sync_copy(acc_pe.at[:, pl.ds(k * sec_pe, sec_pe)], gpe_ref.at[k])

    tile = lambda w: pl.BlockSpec((tm, w), lambda i: (i, 0))
    any_spec = pl.BlockSpec(memory_space=pl.ANY)
    full = lambda shape: pl.BlockSpec(shape, lambda i: (0,) * len(shape))
    return pl.pallas_call(
        body, name="mid", grid=(n,),
        in_specs=[tile(lw), tile(pwid), tile(d), win_cols(n_win_blocks - 2), win_cols(n_win_blocks - 1),
                  tile(d), tile(pdim), tile(d),
                  _const_spec(wpl.shape), _const_spec(wpp.shape), _const_spec(wout.shape), _const_spec(wpg.shape),
                  _const_spec(wpe.shape), _const_spec(g1.shape), _const_spec(gf.shape)],
        out_specs=(tile(lw), tile(pwid), tile(2 * d), tile(d), any_spec, any_spec, any_spec, any_spec, any_spec,
                   full((MID_PACK_ROWS, HEAD))),
        out_shape=(jax.ShapeDtypeStruct((t, lw), F32), jax.ShapeDtypeStruct((t, pwid), F32),
                   jax.ShapeDtypeStruct((t, 2 * d), _MM), jax.ShapeDtypeStruct((t, d), F32),
                   jax.ShapeDtypeStruct(wpl.shape, F32), jax.ShapeDtypeStruct((N_DEV, wpp.shape[0], sec_pp), F32),
                   jax.ShapeDtypeStruct(wout.shape, F32), jax.ShapeDtypeStruct(wpg.shape, F32),
                   jax.ShapeDtypeStruct((N_DEV, wpe.shape[0], sec_pe), F32),
                   jax.ShapeDtypeStruct((MID_PACK_ROWS, HEAD), F32)),
        scratch_shapes=[pltpu.VMEM(wpl.shape, F32), pltpu.VMEM(wpp.shape, F32), pltpu.VMEM(wout.shape, F32),
                        pltpu.VMEM(wpg.shape, F32), pltpu.VMEM(wpe.shape, F32)],
        compiler_params=pltpu.CompilerParams(dimension_semantics=("arbitrary",), vmem_limit_bytes=VMEM_LIMIT),
    )(ya, yb, h, win, win, x, p, tgt, wpl, wpp, wout, wpg, wpe, g1, gf)


def _bwd_seq(z, hl, saved, dya, dyb, dm, cw, wa, wx, lam, pw, ps, pack_mid, sib_grads, tm):
    nb, s, zw = z.shape
    lw = cw.shape[1]
    pwid = ps.shape[1]
    nt = s // tm
    nc = zw + dm.shape[2]
    R = _SEQ_ROWS
    ns = len(sib_grads)

    def body(zq_ref, hl_ref, hlh_ref, xc_buf, r_buf, i_buf, a_ref, diff_ref, yp_ref, dya_ref, dyb_ref, dm_ref,
             cw_ref, wa_ref, wx_ref, lam_ref, pw_ref, ps_ref, pack_mid_ref, *rest):
        sib_src, rest = rest[:ns], rest[ns:]
        dz_ref, pack_ref = rest[:2]
        sib_out, rest = rest[2:2 + ns], rest[2 + ns:]
        a_ext, an_buf, hl_ext, dh_buf, dxc_ext, q_ext, carry, send_sems, recv_sems = rest
        b = pl.program_id(0)
        j = pl.program_id(1)
        jr = nt - 1 - j
        has_prev = jr > 0

        @pl.when((b == 0) & (j == 0))
        def _():
            for cp in _sibling_copies(sib_src, sib_out, send_sems, recv_sems):
                cp.start()

        dz_ref[:, zw:nc] = dm_ref[...]

        @pl.when((b == 0) & (j == 0))
        def _():
            pack_ref[...] = jnp.zeros_like(pack_ref)

        @pl.when(j == 0)
        def _():
            a_ext[pl.ds(tm, SUB), :] = jnp.zeros((SUB, lw), F32)
            dxc_ext[pl.ds(tm, HALO), :] = jnp.zeros((HALO, lw), F32)
            q_ext[pl.ds(tm, HALO), :] = jnp.zeros((HALO, pwid), F32)
            carry[...] = jnp.zeros_like(carry)

        hl_ext[pl.ds(0, SUB), :] = jnp.where(has_prev, hlh_ref[...], 0.0)
        hl_ext[pl.ds(SUB, tm), :] = hl_ref[...]
        a_ext[pl.ds(0, tm), :] = a_ref[...]
        for h in range(N_HEADS):
            cols = slice(h * HEAD, (h + 1) * HEAD)
            ga = zq_ref[:, lw + h * HEAD:lw + (h + 1) * HEAD]
            sg = _sigmoid(ga)
            dyav = dya_ref[:, cols]
            dh_buf[:, cols] = dyav * (ga * sg)
            dga = (dyav * hl_ext[pl.ds(SUB, tm), cols]) * (sg * (1.0 + ga * (1.0 - sg)))
            dz_ref[:, lw + h * HEAD:lw + (h + 1) * HEAD] = dga.astype(dz_ref.dtype)
        an_buf[...] = _rows_ahead(a_ext[...], 1)[0:tm]

        rows = lax.broadcasted_iota(jnp.int32, (SUB, lw), 0)
        nch = tm // SUB

        def step(c, car):
            i0 = pl.multiple_of((nch - 1 - c) * SUB, SUB)
            cv = an_buf[pl.ds(i0, SUB), :]
            bv = dh_buf[pl.ds(i0, SUB), :]
            for sh in (1, 2, 4):
                m = rows < SUB - sh
                c_sh = jnp.where(m, pltpu.roll(cv, SUB - sh, 0), 1.0)
                b_sh = jnp.where(m, pltpu.roll(bv, SUB - sh, 0), 0.0)
                bv = cv * b_sh + bv
                cv = cv * c_sh
            hv = cv * car + bv
            dh_buf[pl.ds(i0, SUB), :] = hv
            return jnp.broadcast_to(hv[0:1, :], (SUB, lw))

        carry[...] = lax.fori_loop(0, nch, step, carry[...])

        for h in range(N_HEADS):
            cols = slice(h * HEAD, (h + 1) * HEAD)
            lam_h = lam_ref[:, cols]
            neg_c_sp = -LRU_C * _softplus_neg(lam_h)
            xc = xc_buf[:, cols]
            r = r_buf[:, cols]
            i = i_buf[:, cols]
            a = a_ref[:, cols]
            mult, inv_mult = _lru_mult(neg_c_sp * r, a)
            dh = dh_buf[:, cols]
            dhx = dh * xc
            di = dhx * mult
            h_prev = _rows_back(hl_ext[:, cols], 1)[SUB:SUB + tm]
            dlog_a = (dh * h_prev) * a - (dhx * i) * ((a * a) * inv_mult)
            pack_ref[pl.ds(R["dlam"][0] + h, 1), :] += _colsum(dlog_a * r) * (LRU_C * _sigmoid(-lam_h))
            dpa = (dlog_a * neg_c_sp) * (r * (1.0 - r))
            dpx = di * (i * (1.0 - i))
            dpa16 = dpa.astype(_MM)
            dpx16 = dpx.astype(_MM)
            xc16 = xc.astype(_MM)
            dxc = dh * (mult * i) + _dot_nt(dpa16, wa_ref[h]) + _dot_nt(dpx16, wx_ref[h])
            pack_ref[pl.ds(R["dwa"][0] + h * HEAD, HEAD), :] += _dot_tn(xc16, dpa16)
            pack_ref[pl.ds(R["dwx"][0] + h * HEAD, HEAD), :] += _dot_tn(xc16, dpx16)
            pack_ref[pl.ds(R["dba"][0] + h, 1), :] += _colsum(dpa)
            pack_ref[pl.ds(R["dbx"][0] + h, 1), :] += _colsum(dpx)
            pack_ref[pl.ds(R["dcb"][0] + h, 1), :] += _colsum(dxc)
            dxc_ext[pl.ds(0, tm), cols] = dxc

        for h in range(N_HEADS):
            cols = slice(h * HEAD, (h + 1) * HEAD)
            dxc_all = dxc_ext[:, cols]
            xa = zq_ref[:, cols]
            dxa = None
            for k in range(4):
                dxc_k = _rows_ahead(dxc_all, 3 - k)[0:tm]
                pack_ref[pl.ds(R["dcw"][0] + SUB * k + h, 1), :] += _colsum(dxc_k * xa)
                term = cw_ref[pl.ds(k, 1), cols] * dxc_k
                dxa = term if dxa is None else dxa + term
            dz_ref[:, cols] = dxa.astype(dz_ref.dtype)

        pos = jr * tm + lax.broadcasted_iota(jnp.int32, (tm, HEAD), 0)
        for g, k in enumerate(POOL_WINDOWS):
            cols = slice(g * HEAD, (g + 1) * HEAD)
            inv_cnt = 1.0 / jnp.minimum(pos + 1, k).astype(F32)
            diff16 = diff_ref[:, cols]
            yp = yp_ref[:, cols]
            sc = ps_ref[:, cols]
            gb = zq_ref[:, 2 * lw + pwid + g * HEAD:2 * lw + pwid + (g + 1) * HEAD]
            sgb = _sigmoid(gb)
            dybv = dyb_ref[:, cols]
            dy_pool = dybv * (gb * sgb)
            dgb = (dybv * (yp * sc)) * (sgb * (1.0 + gb * (1.0 - sgb)))
            pack_ref[pl.ds(R["dps"][0] + g, 1), :] += _colsum(dy_pool * yp)
            dyp16 = (dy_pool * sc).astype(_MM)
            pack_ref[pl.ds(R["dpw"][0] + g * HEAD, HEAD), :] += _dot_tn(diff16, dyp16)
            ddiff = _dot_nt(dyp16, pw_ref[g])
            q_ext[pl.ds(0, tm), cols] = ddiff * inv_cnt
            dxb = _window_sum(q_ext[:, cols], k, _rows_ahead)[0:tm] - ddiff
            dz_ref[:, 2 * lw + g * HEAD:2 * lw + (g + 1) * HEAD] = dxb.astype(dz_ref.dtype)
            dz_ref[:, 2 * lw + pwid + g * HEAD:2 * lw + pwid + (g + 1) * HEAD] = dgb.astype(dz_ref.dtype)

        a_ext[pl.ds(tm, SUB), :] = a_ext[pl.ds(0, SUB), :]
        dxc_ext[pl.ds(tm, HALO), :] = dxc_ext[pl.ds(0, HALO), :]
        q_ext[pl.ds(tm, HALO), :] = q_ext[pl.ds(0, HALO), :]

        @pl.when((b == nb - 1) & (j == nt - 1))
        def _():
            pack_ref[pl.ds(_OFF_MID, MID_PACK_ROWS), :] = pack_mid_ref[...]
            for cp in _sibling_copies(sib_src, sib_out, send_sems, recv_sems):
                cp.wait()

    rev = lambda w: pl.BlockSpec((None, tm, w), lambda b, j: (b, nt - 1 - j, 0))
    prev_rows = lambda rows, w: pl.BlockSpec(
        (None, rows, w), lambda b, j: (b, jnp.maximum((nt - 1 - j) * (tm // rows) - 1, 0), 0))
    return pl.pallas_call(
        body, name="bwd_seq", grid=(nb, nt),
        in_specs=[rev(zw), rev(lw), prev_rows(SUB, lw), rev(lw), rev(lw), rev(lw), rev(lw), rev(pwid), rev(pwid),
                  rev(lw), rev(pwid), rev(nc - zw),
                  _const_spec(cw.shape), _const_spec(wa.shape), _const_spec(wx.shape), _const_spec(lam.shape),
                  _const_spec(pw.shape), _const_spec(ps.shape), _const_spec(pack_mid.shape)] + [HBM_SPEC] * ns,
        out_specs=(rev(nc), pl.BlockSpec((SMALL_ROWS, HEAD), lambda b, j: (0, 0))) + (HBM_SPEC,) * ns,
        out_shape=(jax.ShapeDtypeStruct((nb, s, nc), _MM), jax.ShapeDtypeStruct((SMALL_ROWS, HEAD), F32))
        + tuple(jax.ShapeDtypeStruct((4,) + g.shape[1:], g.dtype) for g in sib_grads),
        scratch_shapes=[pltpu.VMEM((tm + SUB, lw), F32), pltpu.VMEM((tm, lw), F32), pltpu.VMEM((tm + SUB, lw), F32),
                        pltpu.VMEM((tm, lw), F32), pltpu.VMEM((tm + HALO, lw), F32),
                        pltpu.VMEM((tm + HALO, pwid), F32), pltpu.VMEM((SUB, lw), F32),
                        pltpu.SemaphoreType.DMA((ns, 4)), pltpu.SemaphoreType.DMA((ns, 4))],
        compiler_params=pltpu.CompilerParams(dimension_semantics=("arbitrary", "arbitrary"), vmem_limit_bytes=VMEM_LIMIT),
    )(z, hl, hl, *saved, dya, dyb, dm, cw, wa, wx, lam, pw, ps, pack_mid, *sib_grads)


def _bwd_win(ht, dz, chip_srcs, small):
    nb, d, s = ht.shape
    nc = dz.shape[2]
    sec = nc // N_DEV
    n_chips = N_DEV // 2
    nm = len(chip_srcs)

    def body(ht_ref, dz_ref, *rest):
        chip_src, small_src = rest[:nm], rest[nm]
        gwin_ref, recv_ref = rest[nm + 1:nm + 3]
        chip_out, small_out = rest[nm + 3:2 * nm + 3], rest[2 * nm + 3]
        acc, local_sems, sib_send, sib_recv, chip_send, chip_recv, small_send, small_recv = rest[2 * nm + 4:]
        q = pl.program_id(0)
        b = pl.program_id(1)
        slot = q % 2
        x, y, c = _my_pos()

        def half(sl, core):
            return acc.at[sl, :, pl.ds(pl.multiple_of(core * sec, HEAD), sec)]

        def local_copy(qq, sl):
            return pltpu.make_async_copy(half(sl, c), gwin_ref.at[2 * qq + c], local_sems.at[sl])

        def sib_copy(qq, sl):
            return pltpu.make_async_remote_copy(
                src_ref=half(sl, 1 - c), dst_ref=recv_ref.at[qq], send_sem=sib_send.at[qq],
                recv_sem=sib_recv.at[qq], device_id=(x, y, 1 - c), device_id_type=MESH)

        def others():
            return (_chip_copies(chip_src, chip_out, chip_send, chip_recv)
                    + _sibling_copies([small_src], [small_out], small_send, small_recv))

        @pl.when((q == 0) & (b == 0))
        def _():
            for cp in others():
                cp.start()

        @pl.when((q >= 2) & (b == 0))
        def _():
            local_copy(q - 2, slot).wait()
            sib_copy(q - 2, slot).wait_send()

        part = _dot(ht_ref[b], dz_ref[...])

        @pl.when(b == 0)
        def _():
            acc[slot] = part

        @pl.when(b != 0)
        def _():
            acc[slot] += part

        @pl.when(b == nb - 1)
        def _():
            local_copy(q, slot).start()
            sib_copy(q, slot).start()

        @pl.when((q == n_chips - 1) & (b == nb - 1))
        def _():
            for qq in (n_chips - 2, n_chips - 1):
                local_copy(qq, qq % 2).wait()
                sib_copy(qq, qq % 2).wait_send()
            for qq in range(n_chips):
                sib_copy(qq, 0).wait_recv()
            for cp in others():
                cp.wait()

    return pl.pallas_call(
        body, name="bwd_win", grid=(n_chips, nb),
        in_specs=[_const_spec(ht.shape), pl.BlockSpec((None, s, 2 * sec), lambda q, b: (b, 0, q))]
        + [HBM_SPEC] * (nm + 1),
        out_specs=(HBM_SPEC,) * (nm + 3),
        out_shape=(jax.ShapeDtypeStruct((N_DEV, d, sec), F32), jax.ShapeDtypeStruct((n_chips, d, sec), F32))
        + tuple(jax.ShapeDtypeStruct((3,) + g.shape[1:], g.dtype) for g in chip_srcs)
        + (jax.ShapeDtypeStruct((4,) + small.shape[1:], small.dtype),),
        scratch_shapes=[pltpu.VMEM((2, d, 2 * sec), F32), pltpu.SemaphoreType.DMA((2,)),
                        pltpu.SemaphoreType.DMA((N_DEV // 2,)), pltpu.SemaphoreType.DMA((N_DEV // 2,)),
                        pltpu.SemaphoreType.DMA((nm, 3)), pltpu.SemaphoreType.DMA((nm, 3)),
                        pltpu.SemaphoreType.DMA((1, 4)), pltpu.SemaphoreType.DMA((1, 4))],
        compiler_params=pltpu.CompilerParams(dimension_semantics=("arbitrary", "arbitrary"), vmem_limit_bytes=VMEM_LIMIT),
    )(ht, dz, *chip_srcs, small)


def _bwd_dx(dz, x, dx1, win, g0, small, sib_small, chip_srcs, tm):
    t, d = x.shape
    nc = win.shape[1]
    n = t // tm
    nm = len(chip_srcs)
    sec_rows = small.shape[1]
    n_pieces = 5

    def body(dz_ref, x_ref, dx1_ref, win_ref, g0_ref, small_ref, sib_small_ref, *rest):
        chip_src = rest[:nm]
        gx_ref = rest[nm]
        chip_out = rest[nm + 1:2 * nm + 1]
        gsmall_ref, parts_ref = rest[2 * nm + 1:2 * nm + 3]
        (pack, pieces, total, chip_send, chip_recv, piece_sems, g_send, g_recv, g_local,
         p_send, p_recv, p_local) = rest[2 * nm + 3:]
        i = pl.program_id(0)
        x_, y_, c_ = _my_pos()
        sec_idx = 4 * x_ + 2 * y_ + c_
        chip_copies = lambda which=None: _chip_copies(chip_src, chip_out, chip_send, chip_recv, which)
        g_start, g_forward, g_finish = _gather_phases([total], [gsmall_ref], [_row_slicer(sec_rows)], g_send, g_recv, g_local)

        @pl.when(i == 0)
        def _():
            pack[...] = jnp.zeros_like(pack)
            for cp in chip_copies([0]):
                cp.start()

        @pl.when(i == min(1, n - 1))
        def _():
            for cp in chip_copies([0]):
                cp.wait_recv()
            srcs = [small_ref.at[sec_idx], sib_small_ref.at[2 * x_ + y_]] + [chip_out[0].at[k] for k in range(3)]
            loads = [pltpu.make_async_copy(src, pieces.at[k], piece_sems.at[k]) for k, src in enumerate(srcs)]
            for cp in loads:
                cp.start()
            for cp in loads:
                cp.wait()
            acc = pieces[0] + pieces[1]
            for k in range(2, n_pieces):
                acc = acc + pieces[k]
            total[...] = acc
            g_start()
            for cp in chip_copies(range(1, nm)):
                cp.start()

        pl.when(i == max(n // 4, min(1, n - 1)))(g_forward)

        xv = x_ref[...]
        r0 = _rsqrt_mean_sq(xv)
        xh = xv * r0
        dh = _dot_nt(dz_ref[...], win_ref[...])
        _store_chunks(pack, 0, _colsum(dh * xh))
        dxh = dh * g0_ref[...]
        gx_ref[...] = dx1_ref[...] + r0 * (dxh - xh * jnp.mean(dxh * xh, axis=-1, keepdims=True))

        @pl.when(i == n - 1)
        def _():
            g_finish()
            for cp in chip_copies([0]):
                cp.wait_send()
            for cp in chip_copies(range(1, nm)):
                cp.wait()
            mine = parts_ref.at[pl.ds(pl.multiple_of(sec_idx * INP_PACK_ROWS, SUB), INP_PACK_ROWS), :]
            local = pltpu.make_async_copy(pack, mine, p_local)
            local.start()
            sends = []
            for k in range(1, N_DEV):
                peer = tuple(1 - v if (k >> bit) & 1 else v for v, bit in ((x_, 2), (y_, 1), (c_, 0)))
                sends.append(pltpu.make_async_remote_copy(
                    src_ref=pack, dst_ref=mine, send_sem=p_send.at[k - 1], recv_sem=p_recv.at[k - 1],
                    device_id=peer, device_id_type=MESH))
            for cp in sends:
                cp.start()
            for cp in sends:
                cp.wait()
            local.wait()

    tile = lambda w: pl.BlockSpec((tm, w), lambda i: (i, 0))
    return pl.pallas_call(
        body, name="bwd_dx", grid=(n,),
        in_specs=[tile(nc), tile(d), tile(d), _const_spec(win.shape), _const_spec(g0.shape)] + [HBM_SPEC] * (nm + 2),
        out_specs=(tile(d),) + (HBM_SPEC,) * (nm + 2),
        out_shape=(jax.ShapeDtypeStruct((t, d), F32),)
        + tuple(jax.ShapeDtypeStruct((3,) + g.shape[1:], g.dtype) for g in chip_srcs)
        + (jax.ShapeDtypeStruct((N_DEV * sec_rows, HEAD), F32), jax.ShapeDtypeStruct((N_DEV * INP_PACK_ROWS, HEAD), F32)),
        scratch_shapes=[pltpu.VMEM((INP_PACK_ROWS, HEAD), F32), pltpu.VMEM((n_pieces, sec_rows, HEAD), F32),
                        pltpu.VMEM((sec_rows, HEAD), F32),
                        pltpu.SemaphoreType.DMA((nm, 3)), pltpu.SemaphoreType.DMA((nm, 3)),
                        pltpu.SemaphoreType.DMA((n_pieces,))] + _gather_sems(1)
        + [pltpu.SemaphoreType.DMA((N_DEV - 1,)), pltpu.SemaphoreType.DMA((N_DEV - 1,)), pltpu.SemaphoreType.DMA],
        compiler_params=pltpu.CompilerParams(dimension_semantics=("arbitrary",), vmem_limit_bytes=VMEM_LIMIT),
    )(dz, x, dx1, win, g0, small, sib_small, *chip_srcs)


ROW_BLOCKS = 8


def _chip_sums(core_idx, grads, recvs, dtypes, name):
    nw = len(grads)

    def body(idx_ref, *refs):
        gs, rs, outs = refs[:nw], refs[nw:2 * nw], refs[2 * nw:]
        for g, r, o in zip(gs, rs, outs):
            o[...] = (g[...] + r[...]).astype(o.dtype)

    def blk(g):
        return (None,) + g.shape[1:]

    return pl.pallas_call(
        body, name=name,
        grid_spec=pltpu.PrefetchScalarGridSpec(
            num_scalar_prefetch=1, grid=(4,),
            in_specs=[pl.BlockSpec(blk(g), lambda q, s: (2 * q + s[0], 0, 0)) for g in grads]
            + [pl.BlockSpec(blk(g), lambda q, s: (q, 0, 0)) for g in grads],
            out_specs=tuple(pl.BlockSpec(blk(g), lambda q, s: (q, 0, 0)) for g in grads)),
        out_shape=tuple(jax.ShapeDtypeStruct((4,) + g.shape[1:], dt) for g, dt in zip(grads, dtypes)),
    )(core_idx, *grads, *recvs)


def _adamw(w, g, m, v):
    m = ADAM_B1 * m + (1.0 - ADAM_B1) * g
    v = ADAM_B2 * v + (1.0 - ADAM_B2) * (g * g)
    m_hat = m / (1.0 - ADAM_B1 ** ADAM_STEP)
    v_hat = v / (1.0 - ADAM_B2 ** ADAM_STEP)
    delta = -ADAM_LR * (m_hat / (jnp.sqrt(v_hat) + ADAM_EPS) + ADAM_WD * w)
    return delta, m, v


def _adam_sections(sec_idx, grads, recv_sib, recv_chips, wmv):
    nw = len(wmv)

    def body(idx_ref, *refs):
        gs, rs, cs = refs[:nw], refs[nw:2 * nw], refs[2 * nw:3 * nw]
        params = refs[3 * nw:6 * nw]
        outs = refs[6 * nw:]
        for w in range(nw):
            g = gs[w][...] + rs[w][...]
            for k in range(3):
                g = g + cs[w][k].astype(F32)
            wv, mv, vv = (params[3 * w + t][...] for t in range(3))
            delta, m_new, v_new = _adamw(wv, g, mv, vv)
            outs[4 * w][...] = g
            outs[4 * w + 1][...] = delta
            outs[4 * w + 2][...] = m_new
            outs[4 * w + 3][...] = v_new

    def rb(g):
        return g.shape[1] // ROW_BLOCKS

    in_specs = [pl.BlockSpec((None, rb(g), g.shape[2]), lambda i, s: (s[0], i, 0)) for g in grads]
    in_specs += [pl.BlockSpec((None, rb(g), g.shape[2]), lambda i, s: (s[1], i, 0)) for g in grads]
    in_specs += [pl.BlockSpec((3, rb(g), g.shape[2]), lambda i, s: (0, i, 0)) for g in grads]
    sec = lambda g: pl.BlockSpec((rb(g), g.shape[2]), lambda i, s: (i, 0))
    for w in range(nw):
        in_specs += [sec(grads[w])] * 3
    out_specs, out_shape = [], []
    for g in grads:
        out_specs += [sec(g)] * 4
        out_shape += [jax.ShapeDtypeStruct(g.shape[1:], F32)] * 4
    flat = [a for t in wmv for a in t]
    outs = pl.pallas_call(
        body, name="adam_sections",
        grid_spec=pltpu.PrefetchScalarGridSpec(num_scalar_prefetch=1, grid=(ROW_BLOCKS,), in_specs=in_specs,
                                               out_specs=tuple(out_specs)),
        out_shape=tuple(out_shape),
    )(sec_idx, *grads, *recv_sib, *recv_chips, *flat)
    return [outs[4 * w:4 * w + 4] for w in range(nw)]


def _adam_small(sec_idx, gpack, parts, parts_wmv, vec_params, mat_params, conv_wmv):
    items = [(0, parts_wmv, "parts")]
    items += [(r0, t, "vec") for r0, t in vec_params] + [(r0, t, "mat") for r0, t in mat_params]
    items.append((_SEQ_ROWS["dcw"][0], conv_wmv, "conv"))

    def body(idx_ref, g_ref, parts_ref, *refs):
        ins, outs = refs[:3 * len(items)], refs[3 * len(items):]
        for n, (r0, _, kind) in enumerate(items):
            w_ref, m_ref, v_ref = ins[3 * n:3 * n + 3]
            o = outs[4 * n:4 * n + 4]
            if kind == "mat":
                g = g_ref[pl.ds(r0, w_ref.shape[0]), :]
                res = (g,) + _adamw(w_ref[...], g, m_ref[...], v_ref[...])
                for ref, val in zip(o, res):
                    ref[...] = val
            elif kind in ("vec", "parts"):
                for h in range(w_ref.shape[1] // HEAD):
                    cols = slice(h * HEAD, (h + 1) * HEAD)
                    if kind == "vec":
                        g = g_ref[pl.ds(r0 + h, 1), :]
                    else:
                        g = parts_ref[pl.ds(h, 1), :]
                        for dev in range(1, N_DEV):
                            g = g + parts_ref[pl.ds(SUB * dev + h, 1), :]
                    res = (g,) + _adamw(w_ref[:, cols], g, m_ref[:, cols], v_ref[:, cols])
                    for ref, val in zip(o, res):
                        ref[:, cols] = val
            else:
                rows = lax.broadcasted_iota(jnp.int32, (SUB, HEAD), 0)
                for k in range(4):
                    blk = g_ref[pl.ds(r0 + SUB * k, SUB), :]
                    g = jnp.sum(jnp.where(rows == idx_ref[0], blk, 0.0), axis=0, keepdims=True)
                    row = pl.ds(k, 1)
                    res = (g,) + _adamw(w_ref[row, :], g, m_ref[row, :], v_ref[row, :])
                    for ref, val in zip(o, res):
                        ref[row, :] = val
        outs[4 * len(items)][...] = g_ref[pl.ds(_OFF_MID + MID_LOSS_ROW, 1), :]

    flat = [a for _, t, _ in items for a in t]
    full = lambda a: pl.BlockSpec(a.shape, lambda i, s: (0,) * a.ndim)
    out_specs, out_shape = [], []
    for _, t, _ in items:
        out_specs += [full(t[0])] * 4
        out_shape += [jax.ShapeDtypeStruct(t[0].shape, F32)] * 4
    loss_row = jax.ShapeDtypeStruct((1, HEAD), F32)
    out_specs.append(full(loss_row))
    out_shape.append(loss_row)
    outs = pl.pallas_call(
        body, name="adam_small",
        grid_spec=pltpu.PrefetchScalarGridSpec(num_scalar_prefetch=1, grid=(1,),
                                               in_specs=[full(gpack), full(parts)] + [full(a) for a in flat],
                                               out_specs=tuple(out_specs)),
        out_shape=tuple(out_shape),
    )(sec_idx, gpack, parts, *flat)
    return [outs[4 * n:4 * n + 4] for n in range(len(items))], outs[4 * len(items)]


def _col_slicer(width):
    return lambda ref, idx: ref.at[:, pl.ds(pl.multiple_of(idx * width, HEAD), width)]


def _row_slicer(rows):
    return lambda ref, idx: ref.at[pl.ds(pl.multiple_of(idx * rows, SUB), rows), :]


def kernel(x, p, norm_g, w_in, conv_w, conv_b, lru_w_a, lru_b_a, lru_w_x, lru_b_x, lru_lambda, pool_w, pool_scale, w_proj_lru, w_proj_pool, w_out, ple_norm_g, w_ple_gate, w_ple_proj, final_g, loss_target, m_norm_g, m_w_in, m_conv_w, m_conv_b, m_lru_w_a, m_lru_b_a, m_lru_w_x, m_lru_b_x, m_lru_lambda, m_pool_w, m_pool_scale, m_w_proj_lru, m_w_proj_pool, m_w_out, m_ple_norm_g, m_w_ple_gate, m_w_ple_proj, m_final_g, v_norm_g, v_w_in, v_conv_w, v_conv_b, v_lru_w_a, v_lru_b_a, v_lru_w_x, v_lru_b_x, v_lru_lambda, v_pool_w, v_pool_scale, v_w_proj_lru, v_w_proj_pool, v_w_out, v_ple_norm_g, v_w_ple_gate, v_w_ple_proj, v_final_g):
    nb, s, d = x.shape
    t = nb * s
    tm = min(TILE_M, s)
    sec_idx = (4 * lax.axis_index("x") + 2 * lax.axis_index("y") + lax.axis_index("c")).astype(jnp.int32)
    chip_idx = (2 * lax.axis_index("x") + lax.axis_index("y")).astype(jnp.int32)
    core_idx = lax.axis_index("c").astype(jnp.int32)

    h, ht, win, cw = _pre(
        x, norm_g, [w_in[0], conv_w[0]], [True, False],
        [(d, N_DEV * w_in.shape[2]), (conv_w.shape[1], N_DEV * conv_w.shape[2])],
        [_col_slicer(w_in.shape[2]), _col_slicer(conv_w.shape[2])], tm)
    late_shards = [w_proj_lru[0], w_proj_pool[0], w_out[0], w_ple_gate[0], w_ple_proj[0]]
    late_shapes = [(N_DEV * w_proj_lru.shape[1], d), (w_proj_pool.shape[1], N_DEV * w_proj_pool.shape[2]),
                   (N_DEV * w_out.shape[1], d), (N_DEV * w_ple_gate.shape[1], d),
                   (w_ple_proj.shape[1], N_DEV * w_ple_proj.shape[2])]
    late_slicers = [_row_slicer(w_proj_lru.shape[1]), _col_slicer(w_proj_pool.shape[2]), _row_slicer(w_out.shape[1]),
                    _row_slicer(w_ple_gate.shape[1]), _col_slicer(w_ple_proj.shape[2])]

    lw = cw.shape[1]
    pwid = pool_scale.shape[1]
    wa = lru_w_a[0].astype(_MM)
    wx = lru_w_x[0].astype(_MM)
    pw = pool_w[0].astype(_MM)
    ba, bx = lru_b_a[0], lru_b_x[0]
    gf = final_g.reshape(1, d)
    rows_of = lambda g: g.reshape(N_DEV, g.shape[0] // N_DEV, g.shape[1])
    core = core_idx.reshape(1)

    fwd_out = _fwd_seq(
        h, win, cw, conv_b, wa, ba, wx, bx, lru_lambda, pw, pool_scale, late_shards, late_shapes, late_slicers, tm)
    z, hl, ya, yb = fwd_out[:4]
    saved = fwd_out[4:N_SEQ_SAVED]
    wpl, wpp, wout, wpg, wpe = fwd_out[N_SEQ_SAVED:]
    x2d = x.reshape(t, d)
    (dya, dyb, dm, dx1, g_pl, g_pp, g_out, g_pg, g_pe, pack_mid) = _mid(
        ya.reshape(t, lw), yb.reshape(t, pwid), h.reshape(t, d), win, x2d, p[0].reshape(t, -1),
        loss_target.reshape(t, d), wpl, wpp, wout, wpg, wpe, ple_norm_g, gf, tm)
    grads_mid = [rows_of(g_pl), g_pp, rows_of(g_out), rows_of(g_pg), g_pe]
    seq_out = _bwd_seq(z, hl, saved, dya.reshape(nb, s, lw), dyb.reshape(nb, s, pwid), dm.reshape(nb, s, -1), cw,
                       wa, wx, lru_lambda, pw, pool_scale, pack_mid, grads_mid, tm)
    dz, sib_mid = seq_out[0], list(seq_out[2:])
    small = seq_out[1].reshape(N_DEV, SMALL_SEC, HEAD)
    sums_mid = _chip_sums(core, grads_mid, sib_mid, [_WIRE] * len(grads_mid), "chip_sums_mid")
    win_out = _bwd_win(ht, dz, sums_mid, small)
    g_in, sib_in, chips_mid, sib_small = win_out[0], win_out[1], list(win_out[2:-1]), win_out[-1]
    sums_small, sums_in = _chip_sums(core, [small, g_in], [sib_small, sib_in], [F32, _WIRE], "chip_sums_in")
    grad_x, _, chips_in, gsmall, g0_parts = _bwd_dx(
        dz.reshape(t, -1), x2d, dx1, win, norm_g, small, sib_small, [sums_small, sums_in], tm)

    grads = [g_in] + grads_mid
    recv_sib = [sib_in] + sib_mid
    recv_chips = [chips_in] + chips_mid
    wmv = [(w_in[0], m_w_in[0], v_w_in[0]), (w_proj_lru[0], m_w_proj_lru[0], v_w_proj_lru[0]),
           (w_proj_pool[0], m_w_proj_pool[0], v_w_proj_pool[0]), (w_out[0], m_w_out[0], v_w_out[0]),
           (w_ple_gate[0], m_w_ple_gate[0], v_w_ple_gate[0]), (w_ple_proj[0], m_w_ple_proj[0], v_w_ple_proj[0])]
    big = _adam_sections(jnp.stack([sec_idx, chip_idx]), grads, recv_sib, recv_chips, wmv)

    R = _SEQ_ROWS
    vec = lambda r0, *t: (r0, tuple(a.reshape(1, -1) for a in t))
    mat = lambda r0, *t: (r0, tuple(a.reshape(-1, HEAD) for a in t))
    norm_wmv = vec(0, norm_g, m_norm_g, v_norm_g)[1]
    vec_params = [vec(R["dcb"][0], conv_b, m_conv_b, v_conv_b),
                  vec(R["dlam"][0], lru_lambda, m_lru_lambda, v_lru_lambda),
                  vec(R["dps"][0], pool_scale, m_pool_scale, v_pool_scale),
                  vec(_OFF_MID, ple_norm_g, m_ple_norm_g, v_ple_norm_g),
                  vec(_OFF_MID + SUB, final_g, m_final_g, v_final_g)]
    mat_params = [mat(R["dwa"][0], lru_w_a, m_lru_w_a, v_lru_w_a), mat(R["dba"][0], lru_b_a, m_lru_b_a, v_lru_b_a),
                  mat(R["dwx"][0], lru_w_x, m_lru_w_x, v_lru_w_x), mat(R["dbx"][0], lru_b_x, m_lru_b_x, v_lru_b_x),
                  mat(R["dpw"][0], pool_w, m_pool_w, v_pool_w)]
    conv_wmv = (conv_w[0], m_conv_w[0], v_conv_w[0])
    small_out, loss_row = _adam_small(sec_idx.reshape(1), gsmall, g0_parts, norm_wmv, vec_params, mat_params, conv_wmv)
    loss = loss_row[0, 0]

    res = {}
    names_small = ["norm_g", "conv_b", "lru_lambda", "pool_scale", "ple_norm_g", "final_g",
                   "lru_w_a", "lru_b_a", "lru_w_x", "lru_b_x", "pool_w", "conv_w"]
    shapes = {"norm_g": norm_g, "conv_b": conv_b, "lru_lambda": lru_lambda, "pool_scale": pool_scale,
              "ple_norm_g": ple_norm_g, "final_g": final_g, "lru_w_a": lru_w_a, "lru_b_a": lru_b_a, "lru_w_x": lru_w_x,
              "lru_b_x": lru_b_x, "pool_w": pool_w, "conv_w": conv_w}
    for name, quad in zip(names_small, small_out):
        res[name] = [a.reshape(shapes[name].shape) for a in quad]
    names_big = ["w_in", "w_proj_lru", "w_proj_pool", "w_out", "w_ple_gate", "w_ple_proj"]
    for name, quad, (w, _, _) in zip(names_big, big, wmv):
        res[name] = [a.reshape((1,) + w.shape) for a in quad]
    order = ["norm_g", "w_in", "conv_w", "conv_b", "lru_w_a", "lru_b_a", "lru_w_x", "lru_b_x", "lru_lambda", "pool_w",
             "pool_scale", "w_proj_lru", "w_proj_pool", "w_out", "ple_norm_g", "w_ple_gate", "w_ple_proj", "final_g"]
    out = [loss, grad_x.reshape(nb, s, d)]
    for kind in range(4):
        out += [res[name][kind] for name in order]
    return tuple(out)
```

```python
import jax
import jax.numpy as jnp
from jax import lax
from jax.experimental import pallas as pl
from jax.experimental.pallas import tpu as pltpu

F32 = jnp.float32
_MM = jnp.bfloat16
_WIRE = jnp.bfloat16

EPS = 1e-6
LRU_C = 8.0
MULT_SQ_FLOOR = 1e-30
POOL_WINDOWS = (2, 4, 8, 16)
N_HEADS = 8
HEAD = 128
HALO = 16
SUB = 8

ADAM_LR = 0.001
ADAM_B1 = 0.9
ADAM_B2 = 0.999
ADAM_EPS = 1e-08
ADAM_WD = 0.01
ADAM_STEP = 10

N_DEV = 8
MESH = pl.DeviceIdType.MESH
VMEM_LIMIT = 60 * 1024 * 1024
TILE_M = 256
N_SEQ_SAVED = 10

_SEQ_ROWS = {"dwa": (0, 1024), "dwx": (1024, 1024), "dpw": (2048, 512), "dba": (2560, 8), "dbx": (2568, 8),
             "dcb": (2576, 8), "dlam": (2584, 8), "dcw": (2592, 32), "dps": (2624, 8)}
SEQ_PACK_ROWS = 2632
MID_PACK_ROWS = 24
MID_LOSS_ROW = 16
INP_PACK_ROWS = 8
SMALL_ROWS = 3072
SMALL_SEC = SMALL_ROWS // N_DEV
_OFF_MID = SEQ_PACK_ROWS


def _dot(a, b):
    return jnp.dot(a, b, preferred_element_type=F32)


def _dot_nt(a, b):
    return lax.dot_general(a, b, (((1,), (1,)), ((), ())), preferred_element_type=F32)


def _dot_tn(a, b):
    return lax.dot_general(a, b, (((0,), (0,)), ((), ())), preferred_element_type=F32)


def _sigmoid(v):
    return 0.5 * jnp.tanh(0.5 * v) + 0.5


def _lru_mult(log_a, a):
    m2 = jnp.maximum(-jnp.tanh(log_a) * (1.0 + a * a), MULT_SQ_FLOOR)
    inv = lax.rsqrt(m2)
    return m2 * inv, inv


def _rows_back(v, n):
    return pltpu.roll(v, n, 0) if n else v


def _rows_ahead(v, n):
    return pltpu.roll(v, v.shape[0] - n, 0) if n else v


def _softplus_neg(lam):
    e = jnp.exp(-jnp.abs(lam))
    w = 1.0 + e
    l1p = jnp.where(w == 1.0, e, jnp.log(w) * (e / (w - 1.0)))
    return jnp.maximum(-lam, 0.0) + l1p


def _rsqrt_mean_sq(v):
    return lax.rsqrt(jnp.mean(v * v, axis=-1, keepdims=True) + EPS)


def _colsum(v):
    return jnp.sum(v, axis=0, keepdims=True)


def _const_spec(shape):
    nd = len(shape)
    return pl.BlockSpec(shape, lambda *_: (0,) * nd, pipeline_mode=pl.Buffered(1))


HBM_SPEC = pl.BlockSpec(memory_space=pl.ANY)


def _my_pos():
    return lax.axis_index("x"), lax.axis_index("y"), lax.axis_index("c")


def _other_chips(x, y):
    return [(1 - x, y), (x, 1 - y), (1 - x, 1 - y)]


def _gather_sems(nw):
    return [pltpu.SemaphoreType.DMA((nw, 7)), pltpu.SemaphoreType.DMA((nw, 7)), pltpu.SemaphoreType.DMA((nw,))]


def _gather_phases(srcs, outs, slicers, send_sems, recv_sems, local_sems, relay_diagonal=False):
    n_direct = 2 if relay_diagonal else 3
    nw = len(srcs)
    x, y, c = _my_pos()
    me, sibling = (x, y, c), (x, y, 1 - c)
    chips = _other_chips(x, y)

    def part(w, pos):
        return slicers[w](outs[w], 4 * pos[0] + 2 * pos[1] + pos[2])

    def copy(w, k, block, to, src=None):
        return pltpu.make_async_remote_copy(
            src_ref=part(w, block) if src is None else src, dst_ref=part(w, block),
            send_sem=send_sems.at[w, k], recv_sem=recv_sems.at[w, k], device_id=to, device_id_type=MESH)

    def mine():
        return [pltpu.make_async_copy(srcs[w], part(w, me), local_sems.at[w]) for w in range(nw)]

    def first():
        out = []
        for w in range(nw):
            out.append(copy(w, 0, me, sibling, src=srcs[w]))
            out += [copy(w, 1 + j, me, (*chips[j], c), src=srcs[w]) for j in range(n_direct)]
        return out

    def relay(w, j):
        return copy(w, 3, (*chips[j], c), (*chips[1 - j], c))

    def relays():
        return [relay(w, 0) for w in range(nw)] if relay_diagonal else []

    def passed():
        return [copy(w, 4 + j, (*chip, c), sibling) for j, chip in enumerate(chips) for w in range(nw)]

    def start():
        for cp in mine() + first():
            cp.start()

    def forward():
        for j, chip in enumerate(chips):
            for w in range(nw):
                copy(w, 1 + j, (*chip, c), me).wait_recv()
                copy(w, 4 + j, (*chip, c), sibling).start()
            if relay_diagonal and j < 2:
                @pl.when(c == j)
                def _():
                    for w in range(nw):
                        relay(w, j).start()

    def finish():
        for w in range(nw):
            copy(w, 0, sibling, me).wait_recv()
            for j, chip in enumerate(chips):
                copy(w, 4 + j, (*chip, 1 - c), me).wait_recv()
        for cp in first() + relays() + passed():
            cp.wait_send()
        for cp in mine():
            cp.wait()

    return start, forward, finish


def _sibling_copies(srcs, outs, send_sems, recv_sems):
    x, y, c = _my_pos()
    return [pltpu.make_async_remote_copy(
        src_ref=srcs[w].at[2 * q + (1 - c)], dst_ref=outs[w].at[q], send_sem=send_sems.at[w, q],
        recv_sem=recv_sems.at[w, q], device_id=(x, y, 1 - c), device_id_type=MESH)
        for w in range(len(srcs)) for q in range(4)]


def _chip_copies(srcs, outs, send_sems, recv_sems, which=None):
    x, y, c = _my_pos()
    return [pltpu.make_async_remote_copy(
        src_ref=srcs[w].at[2 * px + py], dst_ref=outs[w].at[k], send_sem=send_sems.at[w, k],
        recv_sem=recv_sems.at[w, k], device_id=(px, py, c), device_id_type=MESH)
        for w in (range(len(srcs)) if which is None else which) for k, (px, py) in enumerate(_other_chips(x, y))]


def _lru_gates(xc, h, wa_ref, ba_ref, wx_ref, bx_ref, neg_c_sp):
    xc16 = xc.astype(_MM)
    r = _sigmoid(_dot(xc16, wa_ref[h]) + ba_ref[pl.ds(h, 1), :])
    i = _sigmoid(_dot(xc16, wx_ref[h]) + bx_ref[pl.ds(h, 1), :])
    return r, i, neg_c_sp * r


def _conv_head(xa_ext, cw_ref, cb_ref, cols, tm):
    ext = xa_ext[:, cols]
    xc = cb_ref[:, cols] + cw_ref[pl.ds(3, 1), cols] * ext[HALO:HALO + tm]
    for k in range(3):
        xc = xc + cw_ref[pl.ds(k, 1), cols] * _rows_back(ext, 3 - k)[HALO:HALO + tm]
    return xc


def _window_sum(ext, k, shift):
    n = 1
    while n < k:
        ext = ext + shift(ext, n)
        n *= 2
    return ext


def _pool_diff(xb_ext, g, k, pos, tm):
    cols = slice(g * HEAD, (g + 1) * HEAD)
    ext = xb_ext[:, cols]
    ws = _window_sum(ext, k, _rows_back)[HALO:HALO + tm]
    inv_cnt = 1.0 / jnp.minimum(pos + 1, k).astype(F32)
    return ws * inv_cnt - ext[HALO:HALO + tm], inv_cnt


def _staged_sources(shards, as_operand):
    in_specs = [_const_spec(a.shape) if c else HBM_SPEC for a, c in zip(shards, as_operand)]
    stages = [pltpu.VMEM(a.shape, _MM) for a, c in zip(shards, as_operand) if c]
    dtypes = [_MM if c else a.dtype for a, c in zip(shards, as_operand)]
    return in_specs, stages, dtypes


def _gather_sources(srcs, stages, as_operand):
    stages = iter(stages)
    return [next(stages) if c else src for src, c in zip(srcs, as_operand)]


def _fill_stages(srcs, staged, as_operand):
    for src, dst, c in zip(srcs, staged, as_operand):
        if c:
            dst[...] = src[...].astype(dst.dtype)


def _pre(x, g0, shards, as_operand, shapes, slicers, tm):
    nb, s, d = x.shape
    nt = s // tm
    nl = len(shards)
    shard_specs, stage_shapes, wire = _staged_sources(shards, as_operand)

    def body(x_ref, g0_ref, *rest):
        src, rest = rest[:nl], rest[nl:]
        h_ref, ht_ref = rest[:2]
        out, rest = rest[2:2 + nl], rest[2 + nl:]
        (send_sems, recv_sems, local_sems), stages = rest[:3], rest[3:]
        step_no = pl.program_id(0) * nt + pl.program_id(1)
        staged = _gather_sources(src, stages, as_operand)
        g_start, g_forward, g_finish = _gather_phases(staged, out, slicers, send_sems, recv_sems, local_sems,
                                                      relay_diagonal=True)

        @pl.when(step_no == 0)
        def _():
            _fill_stages(src, staged, as_operand)
            g_start()

        xv = x_ref[...]
        h16 = ((xv * _rsqrt_mean_sq(xv)) * g0_ref[...]).astype(_MM)
        h_ref[...] = h16
        ht_ref[...] = h16.T

        @pl.when(step_no == nb * nt - 1)
        def _():
            g_forward()
            g_finish()

    tile = pl.BlockSpec((None, tm, d), lambda b, j: (b, j, 0))
    return pl.pallas_call(
        body, name="pre", grid=(nb, nt),
        in_specs=[tile, _const_spec(g0.shape)] + shard_specs,
        out_specs=(tile, pl.BlockSpec((None, d, tm), lambda b, j: (b, 0, j))) + (HBM_SPEC,) * nl,
        out_shape=(jax.ShapeDtypeStruct((nb, s, d), _MM), jax.ShapeDtypeStruct((nb, d, s), _MM))
        + tuple(jax.ShapeDtypeStruct(shp, dt) for shp, dt in zip(shapes, wire)),
        scratch_shapes=_gather_sems(nl) + stage_shapes,
        compiler_params=pltpu.CompilerParams(dimension_semantics=("arbitrary", "arbitrary")),
    )(x, g0, *shards)


def _fwd_seq(h, win, cw, cb, wa, ba, wx, bx, lam, pw, ps, late_shards, late_shapes, late_slicers, tm):
    nb, s, d = h.shape
    lw = cw.shape[1]
    pwid = ps.shape[1]
    nc = 2 * lw + 2 * pwid
    nt = s // tm
    nl = len(late_shards)
    n_steps = nb * nt
    as_operand = [True] * nl
    shard_specs, stage_shapes, wire = _staged_sources(late_shards, as_operand)

    def body(h_ref, win_ref, cw_ref, cb_ref, wa_ref, ba_ref, wx_ref, bx_ref, lam_ref, pw_ref, ps_ref, *rest):
        late_src, rest = rest[:nl], rest[nl:]
        z_ref, hl_ref, ya_ref, yb_ref, xc_ref, r_ref, i_ref, a_buf, diff_ref, yp_ref = rest[:N_SEQ_SAVED]
        late_out, rest = rest[N_SEQ_SAVED:N_SEQ_SAVED + nl], rest[N_SEQ_SAVED + nl:]
        (xa_ext, xb_ext, carry, send_sems, recv_sems, local_sems), stages = rest[:6], rest[6:]
        j = pl.program_id(1)
        step_no = pl.program_id(0) * nt + j
        staged = _gather_sources(late_src, stages, as_operand)
        g_start, g_forward, g_finish = _gather_phases(staged, late_out, late_slicers, send_sems, recv_sems, local_sems)

        @pl.when(step_no == 0)
        def _():
            _fill_stages(late_src, staged, as_operand)
            g_start()

        pl.when(step_no == n_steps // 2)(g_forward)

        @pl.when(j == 0)
        def _():
            xa_ext[pl.ds(0, HALO), :] = jnp.zeros((HALO, lw), F32)
            xb_ext[pl.ds(0, HALO), :] = jnp.zeros((HALO, pwid), F32)
            carry[...] = jnp.zeros_like(carry)

        h16 = h_ref[...]
        xa = _dot(h16, win_ref[:, 0:lw])
        z_ref[:, 0:lw] = xa
        xa_ext[pl.ds(HALO, tm), :] = xa
        for c0 in range(lw, nc, lw):
            z_ref[:, c0:c0 + lw] = _dot(h16, win_ref[:, c0:c0 + lw])

        for h in range(N_HEADS):
            cols = slice(h * HEAD, (h + 1) * HEAD)
            xc = _conv_head(xa_ext, cw_ref, cb_ref, cols, tm)
            neg_c_sp = -LRU_C * _softplus_neg(lam_ref[:, cols])
            r, i, log_a = _lru_gates(xc, h, wa_ref, ba_ref, wx_ref, bx_ref, neg_c_sp)
            a = jnp.exp(log_a)
            xc_ref[:, cols] = xc
            r_ref[:, cols] = r
            i_ref[:, cols] = i
            a_buf[:, cols] = a
            hl_ref[:, cols] = _lru_mult(log_a, a)[0] * (i * xc)

        rows = lax.broadcasted_iota(jnp.int32, (SUB, lw), 0)

        def step(c, car):
            i0 = pl.multiple_of(c * SUB, SUB)
            av = a_buf[pl.ds(i0, SUB), :]
            bv = hl_ref[pl.ds(i0, SUB), :]
            for sh in (1, 2, 4):
                m = rows >= sh
                a_sh = jnp.where(m, pltpu.roll(av, sh, 0), 1.0)
                b_sh = jnp.where(m, pltpu.roll(bv, sh, 0), 0.0)
                bv = av * b_sh + bv
                av = av * a_sh
            hv = av * car + bv
            hl_ref[pl.ds(i0, SUB), :] = hv
            return jnp.broadcast_to(hv[SUB - 1:SUB, :], (SUB, lw))

        carry[...] = lax.fori_loop(0, tm // SUB, step, carry[...])

        ga = z_ref[:, lw:2 * lw]
        ya_ref[...] = (hl_ref[...] * (ga * _sigmoid(ga))).astype(ya_ref.dtype)

        xb_ext[pl.ds(HALO, tm), :] = z_ref[:, 2 * lw:2 * lw + pwid]
        pos = j * tm + lax.broadcasted_iota(jnp.int32, (tm, HEAD), 0)
        for g, k in enumerate(POOL_WINDOWS):
            cols = slice(g * HEAD, (g + 1) * HEAD)
            diff16 = _pool_diff(xb_ext, g, k, pos, tm)[0].astype(_MM)
            yp = _dot(diff16, pw_ref[g])
            diff_ref[:, cols] = diff16
            yp_ref[:, cols] = yp
            gb = z_ref[:, 2 * lw + pwid + g * HEAD:2 * lw + pwid + (g + 1) * HEAD]
            yb_ref[:, cols] = ((yp * ps_ref[:, cols]) * (gb * _sigmoid(gb))).astype(yb_ref.dtype)

        xa_ext[pl.ds(0, HALO), :] = xa_ext[pl.ds(tm, HALO), :]
        xb_ext[pl.ds(0, HALO), :] = xb_ext[pl.ds(tm, HALO), :]
        pl.when(step_no == n_steps - 1)(g_finish)

    tile = lambda w: pl.BlockSpec((None, tm, w), lambda b, j: (b, j, 0))
    return pl.pallas_call(
        body, name="fwd_seq", grid=(nb, nt),
        in_specs=[tile(d), _const_spec((d, nc)), _const_spec(cw.shape), _const_spec(cb.shape),
                  _const_spec(wa.shape), _const_spec(ba.shape), _const_spec(wx.shape), _const_spec(bx.shape),
                  _const_spec(lam.shape), _const_spec(pw.shape), _const_spec(ps.shape)] + shard_specs,
        out_specs=(tile(nc), tile(lw), tile(lw), tile(pwid), tile(lw), tile(lw), tile(lw), tile(lw), tile(pwid),
                   tile(pwid)) + (HBM_SPEC,) * nl,
        out_shape=(jax.ShapeDtypeStruct((nb, s, nc), F32), jax.ShapeDtypeStruct((nb, s, lw), F32),
                   jax.ShapeDtypeStruct((nb, s, lw), _MM), jax.ShapeDtypeStruct((nb, s, pwid), _MM))
        + (jax.ShapeDtypeStruct((nb, s, lw), F32),) * 4
        + (jax.ShapeDtypeStruct((nb, s, pwid), _MM), jax.ShapeDtypeStruct((nb, s, pwid), F32))
        + tuple(jax.ShapeDtypeStruct(shp, dt) for shp, dt in zip(late_shapes, wire)),
        scratch_shapes=[pltpu.VMEM((tm + HALO, lw), F32), pltpu.VMEM((tm + HALO, pwid), F32),
                        pltpu.VMEM((SUB, lw), F32)] + _gather_sems(nl) + stage_shapes,
        compiler_params=pltpu.CompilerParams(dimension_semantics=("arbitrary", "arbitrary"), vmem_limit_bytes=VMEM_LIMIT),
    )(h, win, cw, cb, wa, ba, wx, bx, lam, pw, ps, *late_shards)


def _store_chunks(ref, row0, vec):
    for h in range(vec.shape[1] // HEAD):
        ref[pl.ds(row0 + h, 1), :] += vec[:, h * HEAD:(h + 1) * HEAD]


def _mid(ya, yb, h, win, x, p, tgt, wpl, wpp, wout, wpg, wpe, g1, gf, tm):
    t, d = x.shape
    lw = ya.shape[1]
    pwid = yb.shape[1]
    pdim = p.shape[1]
    n = t // tm
    sec_pp = wpp.shape[1] // N_DEV
    sec_pe = wpe.shape[1] // N_DEV
    n_win_blocks = win.shape[1] // d
    win_cols = lambda c: pl.BlockSpec((d, d), lambda i: (0, c), pipeline_mode=pl.Buffered(1))

    def body(ya_ref, yb_ref, h_ref, wma_ref, wmb_ref, x_ref, p_ref, tgt_ref, wpl_ref, wpp_ref, wout_ref, wpg_ref, wpe_ref,
             g1_ref, gf_ref,
             dya_ref, dyb_ref, dm_ref, dx1_ref, gpl_ref, gpp_ref, gout_ref, gpg_ref, gpe_ref, pack_ref,
             acc_pl, acc_pp, acc_out, acc_pg, acc_pe):
        i = pl.program_id(0)

        @pl.when(i == 0)
        def _():
            for acc in (acc_pl, acc_pp, acc_out, acc_pg, acc_pe):
                acc[...] = jnp.zeros_like(acc)
            pack_ref[...] = jnp.zeros_like(pack_ref)

        ya16 = ya_ref[...]
        yb16 = yb_ref[...]
        br_a = _dot(ya16, wpl_ref[...])
        br_b = _dot(yb16, wpp_ref[...])
        h16 = h_ref[...]
        sa = _sigmoid(_dot(h16, wma_ref[...]))
        sb = _sigmoid(_dot(h16, wmb_ref[...]))
        mg16 = (sa * br_a + sb * br_b).astype(_MM)
        x1 = x_ref[...] + _dot(mg16, wout_ref[...])
        r1 = _rsqrt_mean_sq(x1)
        n1 = x1 * r1
        h116 = (n1 * g1_ref[...]).astype(_MM)
        gate = _sigmoid(_dot(h116, wpg_ref[...]))
        p16 = p_ref[...].astype(_MM)
        pe = _dot(p16, wpe_ref[...])
        x2 = x1 + gate * pe
        r2 = _rsqrt_mean_sq(x2)
        n2 = x2 * r2
        err = n2 * gf_ref[...] - tgt_ref[...]
        sq = jnp.sum(_colsum(err * err), axis=1, keepdims=True)
        pack_ref[pl.ds(MID_LOSS_ROW, 1), :] += jnp.broadcast_to(sq * (0.5 / d), (1, HEAD))

        dy = err * (1.0 / d)
        _store_chunks(pack_ref, SUB, _colsum(dy * n2))
        dn2 = dy * gf_ref[...]
        dx2 = r2 * (dn2 - n2 * jnp.mean(dn2 * n2, axis=-1, keepdims=True))
        dpe16 = (dx2 * gate).astype(_MM)
        dpg16 = ((dx2 * pe) * (gate * (1.0 - gate))).astype(_MM)
        acc_pe[...] += _dot_tn(p16, dpe16)
        acc_pg[...] += _dot_tn(h116, dpg16)
        dh1 = _dot_nt(dpg16, wpg_ref[...])
        _store_chunks(pack_ref, 0, _colsum(dh1 * n1))
        dn1 = dh1 * g1_ref[...]
        dx1 = dx2 + r1 * (dn1 - n1 * jnp.mean(dn1 * n1, axis=-1, keepdims=True))
        dx1_ref[...] = dx1
        dx116 = dx1.astype(_MM)
        acc_out[...] += _dot_tn(mg16, dx116)
        dmg = _dot_nt(dx116, wout_ref[...])
        da16 = (dmg * sa).astype(_MM)
        db16 = (dmg * sb).astype(_MM)
        dm_ref[:, 0:d] = ((dmg * br_a) * (sa * (1.0 - sa))).astype(dm_ref.dtype)
        dm_ref[:, d:2 * d] = ((dmg * br_b) * (sb * (1.0 - sb))).astype(dm_ref.dtype)
        acc_pl[...] += _dot_tn(ya16, da16)
        acc_pp[...] += _dot_tn(yb16, db16)
        dya_ref[...] = _dot_nt(da16, wpl_ref[...])
        dyb_ref[...] = _dot_nt(db16, wpp_ref[...])

        @pl.when(i == n - 1)
        def _():
            pltpu.sync_copy(acc_pl, gpl_ref)
            pltpu.sync_copy(acc_out, gout_ref)
            pltpu.sync_copy(acc_pg, gpg_ref)
            for k in range(N_DEV):
                pltpu.sync_copy(acc_pp.at[:, pl.ds(k * sec_pp, sec_pp)], gpp_ref.at[k])
                pltpu.sync_copy(acc_pe.at[:, pl.ds(k * sec_pe, sec_pe)], gpe_ref.at[k])

    tile = lambda w: pl.BlockSpec((tm, w), lambda i: (i, 0))
    any_spec = pl.BlockSpec(memory_space=pl.ANY)
    full = lambda shape: pl.BlockSpec(shape, lambda i: (0,) * len(shape))
    return pl.pallas_call(
        body, name="mid", grid=(n,),
        in_specs=[tile(lw), tile(pwid), tile(d), win_cols(n_win_blocks - 2), win_cols(n_win_blocks - 1),
                  tile(d), tile(pdim), tile(d),
                  _const_spec(wpl.shape), _const_spec(wpp.shape), _const_spec(wout.shape), _const_spec(wpg.shape),
                  _const_spec(wpe.shape), _const_spec(g1.shape), _const_spec(gf.shape)],
        out_specs=(tile(lw), tile(pwid), tile(2 * d), tile(d), any_spec, any_spec, any_spec, any_spec, any_spec,
                   full((MID_PACK_ROWS, HEAD))),
        out_shape=(jax.ShapeDtypeStruct((t, lw), F32), jax.ShapeDtypeStruct((t, pwid), F32),
                   jax.ShapeDtypeStruct((t, 2 * d), _MM), jax.ShapeDtypeStruct((t, d), F32),
                   jax.ShapeDtypeStruct(wpl.shape, F32), jax.ShapeDtypeStruct((N_DEV, wpp.shape[0], sec_pp), F32),
                   jax.ShapeDtypeStruct(wout.shape, F32), jax.ShapeDtypeStruct(wpg.shape, F32),
                   jax.ShapeDtypeStruct((N_DEV, wpe.shape[0], sec_pe), F32),
                   jax.ShapeDtypeStruct((MID_PACK_ROWS, HEAD), F32)),
        scratch_shapes=[pltpu.VMEM(wpl.shape, F32), pltpu.VMEM(wpp.shape, F32), pltpu.VMEM(wout.shape, F32),
                        pltpu.VMEM(wpg.shape, F32), pltpu.VMEM(wpe.shape, F32)],
        compiler_params=pltpu.CompilerParams(dimension_semantics=("arbitrary",), vmem_limit_bytes=VMEM_LIMIT),
    )(ya, yb, h, win, win, x, p, tgt, wpl, wpp, wout, wpg, wpe, g1, gf)


def _bwd_seq(z, hl, saved, dya, dyb, dm, cw, wa, wx, lam, pw, ps, pack_mid, sib_grads, tm):
    nb, s, zw = z.shape
    lw = cw.shape[1]
    pwid = ps.shape[1]
    nt = s // tm
    nc = zw + dm.shape[2]
    R = _SEQ_ROWS
    ns = len(sib_grads)

    def body(zq_ref, hl_ref, hlh_ref, xc_buf, r_buf, i_buf, a_ref, diff_ref, yp_ref, dya_ref, dyb_ref, dm_ref,
             cw_ref, wa_ref, wx_ref, lam_ref, pw_ref, ps_ref, pack_mid_ref, *rest):
        sib_src, rest = rest[:ns], rest[ns:]
        dz_ref, pack_ref = rest[:2]
        sib_out, rest = rest[2:2 + ns], rest[2 + ns:]
        a_ext, an_buf, hl_ext, dh_buf, dxc_ext, q_ext, carry, send_sems, recv_sems = rest
        b = pl.program_id(0)
        j = pl.program_id(1)
        jr = nt - 1 - j
        has_prev = jr > 0

        @pl.when((b == 0) & (j == 0))
        def _():
            for cp in _sibling_copies(sib_src, sib_out, send_sems, recv_sems):
                cp.start()

        dz_ref[:, zw:nc] = dm_ref[...]

        @pl.when((b == 0) & (j == 0))
        def _():
            pack_ref[...] = jnp.zeros_like(pack_ref)

        @pl.when(j == 0)
        def _():
            a_ext[pl.ds(tm, SUB), :] = jnp.zeros((SUB, lw), F32)
            dxc_ext[pl.ds(tm, HALO), :] = jnp.zeros((HALO, lw), F32)
            q_ext[pl.ds(tm, HALO), :] = jnp.zeros((HALO, pwid), F32)
            carry[...] = jnp.zeros_like(carry)

        hl_ext[pl.ds(0, SUB), :] = jnp.where(has_prev, hlh_ref[...], 0.0)
        hl_ext[pl.ds(SUB, tm), :] = hl_ref[...]
        a_ext[pl.ds(0, tm), :] = a_ref[...]
        for h in range(N_HEADS):
            cols = slice(h * HEAD, (h + 1) * HEAD)
            ga = zq_ref[:, lw + h * HEAD:lw + (h + 1) * HEAD]
            sg = _sigmoid(ga)
            dyav = dya_ref[:, cols]
            dh_buf[:, cols] = dyav * (ga * sg)
            dga = (dyav * hl_ext[pl.ds(SUB, tm), cols]) * (sg * (1.0 + ga * (1.0 - sg)))
            dz_ref[:, lw + h * HEAD:lw + (h + 1) * HEAD] = dga.astype(dz_ref.dtype)
        an_buf[...] = _rows_ahead(a_ext[...], 1)[0:tm]

        rows = lax.broadcasted_iota(jnp.int32, (SUB, lw), 0)
        nch = tm // SUB

        def step(c, car):
            i0 = pl.multiple_of((nch - 1 - c) * SUB, SUB)
            cv = an_buf[pl.ds(i0, SUB), :]
            bv = dh_buf[pl.ds(i0, SUB), :]
            for sh in (1, 2, 4):
                m = rows < SUB - sh
                c_sh = jnp.where(m, pltpu.roll(cv, SUB - sh, 0), 1.0)
                b_sh = jnp.where(m, pltpu.roll(bv, SUB - sh, 0), 0.0)
                bv = cv * b_sh + bv
                cv = cv * c_sh
            hv = cv * car + bv
            dh_buf[pl.ds(i0, SUB), :] = hv
            return jnp.broadcast_to(hv[0:1, :], (SUB, lw))

        carry[...] = lax.fori_loop(0, nch, step, carry[...])

        for h in range(N_HEADS):
            cols = slice(h * HEAD, (h + 1) * HEAD)
            lam_h = lam_ref[:, cols]
            neg_c_sp = -LRU_C * _softplus_neg(lam_h)
            xc = xc_buf[:, cols]
            r = r_buf[:, cols]
            i = i_buf[:, cols]
            a = a_ref[:, cols]
            mult, inv_mult = _lru_mult(neg_c_sp * r, a)
            dh = dh_buf[:, cols]
            dhx = dh * xc
            di = dhx * mult
            h_prev = _rows_back(hl_ext[:, cols], 1)[SUB:SUB + tm]
            dlog_a = (dh * h_prev) * a - (dhx * i) * ((a * a) * inv_mult)
            pack_ref[pl.ds(R["dlam"][0] + h, 1), :] += _colsum(dlog_a * r) * (LRU_C * _sigmoid(-lam_h))
            dpa = (dlog_a * neg_c_sp) * (r * (1.0 - r))
            dpx = di * (i * (1.0 - i))
            dpa16 = dpa.astype(_MM)
            dpx16 = dpx.astype(_MM)
            xc16 = xc.astype(_MM)
            dxc = dh * (mult * i) + _dot_nt(dpa16, wa_ref[h]) + _dot_nt(dpx16, wx_ref[h])
            pack_ref[pl.ds(R["dwa"][0] + h * HEAD, HEAD), :] += _dot_tn(xc16, dpa16)
            pack_ref[pl.ds(R["dwx"][0] + h * HEAD, HEAD), :] += _dot_tn(xc16, dpx16)
            pack_ref[pl.ds(R["dba"][0] + h, 1), :] += _colsum(dpa)
            pack_ref[pl.ds(R["dbx"][0] + h, 1), :] += _colsum(dpx)
            pack_ref[pl.ds(R["dcb"][0] + h, 1), :] += _colsum(dxc)
            dxc_ext[pl.ds(0, tm), cols] = dxc

        for h in range(N_HEADS):
            cols = slice(h * HEAD, (h + 1) * HEAD)
            dxc_all = dxc_ext[:, cols]
            xa = zq_ref[:, cols]
            dxa = None
            for k in range(4):
                dxc_k = _rows_ahead(dxc_all, 3 - k)[0:tm]
                pack_ref[pl.ds(R["dcw"][0] + SUB * k + h, 1), :] += _colsum(dxc_k * xa)
                term = cw_ref[pl.ds(k, 1), cols] * dxc_k
                dxa = term if dxa is None else dxa + term
            dz_ref[:, cols] = dxa.astype(dz_ref.dtype)

        pos = jr * tm + lax.broadcasted_iota(jnp.int32, (tm, HEAD), 0)
        for g, k in enumerate(POOL_WINDOWS):
            cols = slice(g * HEAD, (g + 1) * HEAD)
            inv_cnt = 1.0 / jnp.minimum(pos + 1, k).astype(F32)
            diff16 = diff_ref[:, cols]
            yp = yp_ref[:, cols]
            sc = ps_ref[:, cols]
            gb = zq_ref[:, 2 * lw + pwid + g * HEAD:2 * lw + pwid + (g + 1) * HEAD]
            sgb = _sigmoid(gb)
            dybv = dyb_ref[:, cols]
            dy_pool = dybv * (gb * sgb)
            dgb = (dybv * (yp * sc)) * (sgb * (1.0 + gb * (1.0 - sgb)))
            pack_ref[pl.ds(R["dps"][0] + g, 1), :] += _colsum(dy_pool * yp)
            dyp16 = (dy_pool * sc).astype(_MM)
            pack_ref[pl.ds(R["dpw"][0] + g * HEAD, HEAD), :] += _dot_tn(diff16, dyp16)
            ddiff = _dot_nt(dyp16, pw_ref[g])
            q_ext[pl.ds(0, tm), cols] = ddiff * inv_cnt
            dxb = _window_sum(q_ext[:, cols], k, _rows_ahead)[0:tm] - ddiff
            dz_ref[:, 2 * lw + g * HEAD:2 * lw + (g + 1) * HEAD] = dxb.astype(dz_ref.dtype)
            dz_ref[:, 2 * lw + pwid + g * HEAD:2 * lw + pwid + (g + 1) * HEAD] = dgb.astype(dz_ref.dtype)

        a_ext[pl.ds(tm, SUB), :] = a_ext[pl.ds(0, SUB), :]
        dxc_ext[pl.ds(tm, HALO), :] = dxc_ext[pl.ds(0, HALO), :]
        q_ext[pl.ds(tm, HALO), :] = q_ext[pl.ds(0, HALO), :]

        @pl.when((b == nb - 1) & (j == nt - 1))
        def _():
            pack_ref[pl.ds(_OFF_MID, MID_PACK_ROWS), :] = pack_mid_ref[...]
            for cp in _sibling_copies(sib_src, sib_out, send_sems, recv_sems):
                cp.wait()

    rev = lambda w: pl.BlockSpec((None, tm, w), lambda b, j: (b, nt - 1 - j, 0))
    prev_rows = lambda rows, w: pl.BlockSpec(
        (None, rows, w), lambda b, j: (b, jnp.maximum((nt - 1 - j) * (tm // rows) - 1, 0), 0))
    return pl.pallas_call(
        body, name="bwd_seq", grid=(nb, nt),
        in_specs=[rev(zw), rev(lw), prev_rows(SUB, lw), rev(lw), rev(lw), rev(lw), rev(lw), rev(pwid), rev(pwid),
                  rev(lw), rev(pwid), rev(nc - zw),
                  _const_spec(cw.shape), _const_spec(wa.shape), _const_spec(wx.shape), _const_spec(lam.shape),
                  _const_spec(pw.shape), _const_spec(ps.shape), _const_spec(pack_mid.shape)] + [HBM_SPEC] * ns,
        out_specs=(rev(nc), pl.BlockSpec((SMALL_ROWS, HEAD), lambda b, j: (0, 0))) + (HBM_SPEC,) * ns,
        out_shape=(jax.ShapeDtypeStruct((nb, s, nc), _MM), jax.ShapeDtypeStruct((SMALL_ROWS, HEAD), F32))
        + tuple(jax.ShapeDtypeStruct((4,) + g.shape[1:], g.dtype) for g in sib_grads),
        scratch_shapes=[pltpu.VMEM((tm + SUB, lw), F32), pltpu.VMEM((tm, lw), F32), pltpu.VMEM((tm + SUB, lw), F32),
                        pltpu.VMEM((tm, lw), F32), pltpu.VMEM((tm + HALO, lw), F32),
                        pltpu.VMEM((tm + HALO, pwid), F32), pltpu.VMEM((SUB, lw), F32),
                        pltpu.SemaphoreType.DMA((ns, 4)), pltpu.SemaphoreType.DMA((ns, 4))],
        compiler_params=pltpu.CompilerParams(dimension_semantics=("arbitrary", "arbitrary"), vmem_limit_bytes=VMEM_LIMIT),
    )(z, hl, hl, *saved, dya, dyb, dm, cw, wa, wx, lam, pw, ps, pack_mid, *sib_grads)


def _bwd_win(ht, dz, chip_srcs, small):
    nb, d, s = ht.shape
    nc = dz.shape[2]
    sec = nc // N_DEV
    n_chips = N_DEV // 2
    nm = len(chip_srcs)

    def body(ht_ref, dz_ref, *rest):
        chip_src, small_src = rest[:nm], rest[nm]
        gwin_ref, recv_ref = rest[nm + 1:nm + 3]
        chip_out = rest[nm + 3:2 * nm + 3]
        small_out, small_chips_out, gsmall_ref = rest[2 * nm + 3:2 * nm + 6]
        (acc, own4, sib4, csum, got3, total, local_sems, sib_send, sib_recv, chip_send, chip_recv, small_send, small_recv,
         sc_send, sc_recv, ld_sems, g_send, g_recv, g_local) = rest[2 * nm + 6:]
        q = pl.program_id(0)
        b = pl.program_id(1)
        slot = q % 2
        x, y, c = _my_pos()

        def half(sl, core):
            return acc.at[sl, :, pl.ds(pl.multiple_of(core * sec, HEAD), sec)]

        def local_copy(qq, sl):
            return pltpu.make_async_copy(half(sl, c), gwin_ref.at[2 * qq + c], local_sems.at[sl])

        def sib_copy(qq, sl):
            return pltpu.make_async_remote_copy(
                src_ref=half(sl, 1 - c), dst_ref=recv_ref.at[qq], send_sem=sib_send.at[qq],
                recv_sem=sib_recv.at[qq], device_id=(x, y, 1 - c), device_id_type=MESH)

        step_no = q * nb + b
        big = lambda: _chip_copies(chip_src, chip_out, chip_send, chip_recv)
        small_sib = lambda: _sibling_copies([small_src], [small_out], small_send, small_recv)
        small_chip = lambda: _chip_copies([csum], [small_chips_out], sc_send, sc_recv)
        g_start, g_forward, g_finish = _gather_phases([total], [gsmall_ref], [_row_slicer(small.shape[1])],
                                                      g_send, g_recv, g_local)

        def load(srcs, dst):
            cps = [pltpu.make_async_copy(src, dst.at[k], ld_sems.at[k]) for k, src in enumerate(srcs)]
            for cp in cps:
                cp.start()
            for cp in cps:
                cp.wait()

        @pl.when(step_no == 0)
        def _():
            for cp in small_sib():
                cp.start()

        @pl.when(step_no == 1)
        def _():
            for cp in small_sib():
                cp.wait_recv()
            load([small_src.at[2 * qq + c] for qq in range(n_chips)], own4)
            load([small_out.at[qq] for qq in range(n_chips)], sib4)
            csum[...] = own4[...] + sib4[...]
            for cp in small_chip():
                cp.start()

        @pl.when(step_no == 2)
        def _():
            for cp in small_chip():
                cp.wait_recv()
            load([small_chips_out.at[k] for k in range(3)], got3)
            total[...] = ((csum[2 * x + y] + got3[0]) + got3[1]) + got3[2]
            g_start()
            for cp in big():
                cp.start()

        pl.when(step_no == 4)(g_forward)

        @pl.when((q >= 2) & (b == 0))
        def _():
            local_copy(q - 2, slot).wait()
            sib_copy(q - 2, slot).wait_send()

        part = _dot(ht_ref[b], dz_ref[...])

        @pl.when(b == 0)
        def _():
            acc[slot] = part

        @pl.when(b != 0)
        def _():
            acc[slot] += part

        @pl.when(b == nb - 1)
        def _():
            local_copy(q, slot).start()
            sib_copy(q, slot).start()

        @pl.when((q == n_chips - 1) & (b == nb - 1))
        def _():
            for qq in (n_chips - 2, n_chips - 1):
                local_copy(qq, qq % 2).wait()
                sib_copy(qq, qq % 2).wait_send()
            for qq in range(n_chips):
                sib_copy(qq, 0).wait_recv()
            g_finish()
            for cp in small_sib() + small_chip():
                cp.wait_send()
            for cp in big():
                cp.wait()

    return pl.pallas_call(
        body, name="bwd_win", grid=(n_chips, nb),
        in_specs=[_const_spec(ht.shape), pl.BlockSpec((None, s, 2 * sec), lambda q, b: (b, 0, q))]
        + [HBM_SPEC] * (nm + 1),
        out_specs=(HBM_SPEC,) * (nm + 5),
        out_shape=(jax.ShapeDtypeStruct((N_DEV, d, sec), F32), jax.ShapeDtypeStruct((n_chips, d, sec), F32))
        + tuple(jax.ShapeDtypeStruct((3,) + g.shape[1:], g.dtype) for g in chip_srcs)
        + (jax.ShapeDtypeStruct((n_chips,) + small.shape[1:], small.dtype),
           jax.ShapeDtypeStruct((3,) + small.shape[1:], small.dtype),
           jax.ShapeDtypeStruct((N_DEV * small.shape[1], small.shape[2]), small.dtype)),
        scratch_shapes=[pltpu.VMEM((2, d, 2 * sec), F32)]
        + [pltpu.VMEM((n_chips,) + small.shape[1:], F32)] * 3
        + [pltpu.VMEM((3,) + small.shape[1:], F32), pltpu.VMEM(small.shape[1:], F32), pltpu.SemaphoreType.DMA((2,)),
           pltpu.SemaphoreType.DMA((n_chips,)), pltpu.SemaphoreType.DMA((n_chips,)),
           pltpu.SemaphoreType.DMA((nm, 3)), pltpu.SemaphoreType.DMA((nm, 3)),
           pltpu.SemaphoreType.DMA((1, 4)), pltpu.SemaphoreType.DMA((1, 4)),
           pltpu.SemaphoreType.DMA((1, 3)), pltpu.SemaphoreType.DMA((1, 3)), pltpu.SemaphoreType.DMA((n_chips,))]
        + _gather_sems(1),
        compiler_params=pltpu.CompilerParams(dimension_semantics=("arbitrary", "arbitrary"), vmem_limit_bytes=VMEM_LIMIT),
    )(ht, dz, *chip_srcs, small)


def _bwd_dx(dz, x, dx1, win, g0, chip_srcs, tm):
    t, d = x.shape
    nc = win.shape[1]
    n = t // tm
    nm = len(chip_srcs)

    def body(dz_ref, x_ref, dx1_ref, win_ref, g0_ref, *rest):
        chip_src = rest[:nm]
        gx_ref = rest[nm]
        chip_out = rest[nm + 1:2 * nm + 1]
        parts_ref = rest[2 * nm + 1]
        pack, chip_send, chip_recv, p_send, p_recv, p_local = rest[2 * nm + 2:]
        i = pl.program_id(0)
        x_, y_, c_ = _my_pos()
        chip_copies = lambda: _chip_copies(chip_src, chip_out, chip_send, chip_recv)

        @pl.when(i == 0)
        def _():
            pack[...] = jnp.zeros_like(pack)
            for cp in chip_copies():
                cp.start()

        xv = x_ref[...]
        r0 = _rsqrt_mean_sq(xv)
        xh = xv * r0
        dh = _dot_nt(dz_ref[...], win_ref[...])
        _store_chunks(pack, 0, _colsum(dh * xh))
        dxh = dh * g0_ref[...]
        gx_ref[...] = dx1_ref[...] + r0 * (dxh - xh * jnp.mean(dxh * xh, axis=-1, keepdims=True))

        @pl.when(i == n - 1)
        def _():
            for cp in chip_copies():
                cp.wait()
            row0 = pl.multiple_of((4 * x_ + 2 * y_ + c_) * INP_PACK_ROWS, SUB)
            mine = parts_ref.at[pl.ds(row0, INP_PACK_ROWS), :]
            local = pltpu.make_async_copy(pack, mine, p_local)
            local.start()
            sends = []
            for k in range(1, N_DEV):
                peer = tuple(1 - v if (k >> bit) & 1 else v for v, bit in ((x_, 2), (y_, 1), (c_, 0)))
                sends.append(pltpu.make_async_remote_copy(
                    src_ref=pack, dst_ref=mine, send_sem=p_send.at[k - 1], recv_sem=p_recv.at[k - 1],
                    device_id=peer, device_id_type=MESH))
            for cp in sends:
                cp.start()
            for cp in sends:
                cp.wait()
            local.wait()

    tile = lambda w: pl.BlockSpec((tm, w), lambda i: (i, 0))
    return pl.pallas_call(
        body, name="bwd_dx", grid=(n,),
        in_specs=[tile(nc), tile(d), tile(d), _const_spec(win.shape), _const_spec(g0.shape)] + [HBM_SPEC] * nm,
        out_specs=(tile(d),) + (HBM_SPEC,) * (nm + 1),
        out_shape=(jax.ShapeDtypeStruct((t, d), F32),)
        + tuple(jax.ShapeDtypeStruct((3,) + g.shape[1:], g.dtype) for g in chip_srcs)
        + (jax.ShapeDtypeStruct((N_DEV * INP_PACK_ROWS, HEAD), F32),),
        scratch_shapes=[pltpu.VMEM((INP_PACK_ROWS, HEAD), F32),
                        pltpu.SemaphoreType.DMA((nm, 3)), pltpu.SemaphoreType.DMA((nm, 3)),
                        pltpu.SemaphoreType.DMA((N_DEV - 1,)), pltpu.SemaphoreType.DMA((N_DEV - 1,)),
                        pltpu.SemaphoreType.DMA],
        compiler_params=pltpu.CompilerParams(dimension_semantics=("arbitrary",), vmem_limit_bytes=VMEM_LIMIT),
    )(dz, x, dx1, win, g0, *chip_srcs)


ROW_BLOCKS = 8


def _chip_sums(core_idx, grads, recvs, dtypes, name):
    nw = len(grads)

    def body(idx_ref, *refs):
        gs, rs, outs = refs[:nw], refs[nw:2 * nw], refs[2 * nw:]
        for g, r, o in zip(gs, rs, outs):
            o[...] = (g[...] + r[...]).astype(o.dtype)

    def blk(g):
        return (None,) + g.shape[1:]

    return pl.pallas_call(
        body, name=name,
        grid_spec=pltpu.PrefetchScalarGridSpec(
            num_scalar_prefetch=1, grid=(4,),
            in_specs=[pl.BlockSpec(blk(g), lambda q, s: (2 * q + s[0], 0, 0)) for g in grads]
            + [pl.BlockSpec(blk(g), lambda q, s: (q, 0, 0)) for g in grads],
            out_specs=tuple(pl.BlockSpec(blk(g), lambda q, s: (q, 0, 0)) for g in grads)),
        out_shape=tuple(jax.ShapeDtypeStruct((4,) + g.shape[1:], dt) for g, dt in zip(grads, dtypes)),
    )(core_idx, *grads, *recvs)


def _adamw(w, g, m, v):
    m = ADAM_B1 * m + (1.0 - ADAM_B1) * g
    v = ADAM_B2 * v + (1.0 - ADAM_B2) * (g * g)
    m_hat = m / (1.0 - ADAM_B1 ** ADAM_STEP)
    v_hat = v / (1.0 - ADAM_B2 ** ADAM_STEP)
    delta = -ADAM_LR * (m_hat / (jnp.sqrt(v_hat) + ADAM_EPS) + ADAM_WD * w)
    return delta, m, v


def _adam_sections(sec_idx, grads, recv_sib, recv_chips, wmv):
    nw = len(wmv)

    def body(idx_ref, *refs):
        gs, rs, cs = refs[:nw], refs[nw:2 * nw], refs[2 * nw:3 * nw]
        params = refs[3 * nw:6 * nw]
        outs = refs[6 * nw:]
        for w in range(nw):
            g = gs[w][...] + rs[w][...]
            for k in range(3):
                g = g + cs[w][k].astype(F32)
            wv, mv, vv = (params[3 * w + t][...] for t in range(3))
            delta, m_new, v_new = _adamw(wv, g, mv, vv)
            outs[4 * w][...] = g
            outs[4 * w + 1][...] = delta
            outs[4 * w + 2][...] = m_new
            outs[4 * w + 3][...] = v_new

    def rb(g):
        return g.shape[1] // ROW_BLOCKS

    in_specs = [pl.BlockSpec((None, rb(g), g.shape[2]), lambda i, s: (s[0], i, 0)) for g in grads]
    in_specs += [pl.BlockSpec((None, rb(g), g.shape[2]), lambda i, s: (s[1], i, 0)) for g in grads]
    in_specs += [pl.BlockSpec((3, rb(g), g.shape[2]), lambda i, s: (0, i, 0)) for g in grads]
    sec = lambda g: pl.BlockSpec((rb(g), g.shape[2]), lambda i, s: (i, 0))
    for w in range(nw):
        in_specs += [sec(grads[w])] * 3
    out_specs, out_shape = [], []
    for g in grads:
        out_specs += [sec(g)] * 4
        out_shape += [jax.ShapeDtypeStruct(g.shape[1:], F32)] * 4
    flat = [a for t in wmv for a in t]
    outs = pl.pallas_call(
        body, name="adam_sections",
        grid_spec=pltpu.PrefetchScalarGridSpec(num_scalar_prefetch=1, grid=(ROW_BLOCKS,), in_specs=in_specs,
                                               out_specs=tuple(out_specs)),
        out_shape=tuple(out_shape),
    )(sec_idx, *grads, *recv_sib, *recv_chips, *flat)
    return [outs[4 * w:4 * w + 4] for w in range(nw)]


def _adam_small(sec_idx, gpack, parts, parts_wmv, vec_params, mat_params, conv_wmv):
    items = [(0, parts_wmv, "parts")]
    items += [(r0, t, "vec") for r0, t in vec_params] + [(r0, t, "mat") for r0, t in mat_params]
    items.append((_SEQ_ROWS["dcw"][0], conv_wmv, "conv"))

    def body(idx_ref, g_ref, parts_ref, *refs):
        ins, outs = refs[:3 * len(items)], refs[3 * len(items):]
        for n, (r0, _, kind) in enumerate(items):
            w_ref, m_ref, v_ref = ins[3 * n:3 * n + 3]
            o = outs[4 * n:4 * n + 4]
            if kind == "mat":
                g = g_ref[pl.ds(r0, w_ref.shape[0]), :]
                res = (g,) + _adamw(w_ref[...], g, m_ref[...], v_ref[...])
                for ref, val in zip(o, res):
                    ref[...] = val
            elif kind in ("vec", "parts"):
                for h in range(w_ref.shape[1] // HEAD):
                    cols = slice(h * HEAD, (h + 1) * HEAD)
                    if kind == "vec":
                        g = g_ref[pl.ds(r0 + h, 1), :]
                    else:
                        g = parts_ref[pl.ds(h, 1), :]
                        for dev in range(1, N_DEV):
                            g = g + parts_ref[pl.ds(SUB * dev + h, 1), :]
                    res = (g,) + _adamw(w_ref[:, cols], g, m_ref[:, cols], v_ref[:, cols])
                    for ref, val in zip(o, res):
                        ref[:, cols] = val
            else:
                rows = lax.broadcasted_iota(jnp.int32, (SUB, HEAD), 0)
                for k in range(4):
                    blk = g_ref[pl.ds(r0 + SUB * k, SUB), :]
                    g = jnp.sum(jnp.where(rows == idx_ref[0], blk, 0.0), axis=0, keepdims=True)
                    row = pl.ds(k, 1)
                    res = (g,) + _adamw(w_ref[row, :], g, m_ref[row, :], v_ref[row, :])
                    for ref, val in zip(o, res):
                        ref[row, :] = val
        outs[4 * len(items)][...] = g_ref[pl.ds(_OFF_MID + MID_LOSS_ROW, 1), :]

    flat = [a for _, t, _ in items for a in t]
    full = lambda a: pl.BlockSpec(a.shape, lambda i, s: (0,) * a.ndim)
    out_specs, out_shape = [], []
    for _, t, _ in items:
        out_specs += [full(t[0])] * 4
        out_shape += [jax.ShapeDtypeStruct(t[0].shape, F32)] * 4
    loss_row = jax.ShapeDtypeStruct((1, HEAD), F32)
    out_specs.append(full(loss_row))
    out_shape.append(loss_row)
    outs = pl.pallas_call(
        body, name="adam_small",
        grid_spec=pltpu.PrefetchScalarGridSpec(num_scalar_prefetch=1, grid=(1,),
                                               in_specs=[full(gpack), full(parts)] + [full(a) for a in flat],
                                               out_specs=tuple(out_specs)),
        out_shape=tuple(out_shape),
    )(sec_idx, gpack, parts, *flat)
    return [outs[4 * n:4 * n + 4] for n in range(len(items))], outs[4 * len(items)]


def _col_slicer(width):
    return lambda ref, idx: ref.at[:, pl.ds(pl.multiple_of(idx * width, HEAD), width)]


def _row_slicer(rows):
    return lambda ref, idx: ref.at[pl.ds(pl.multiple_of(idx * rows, SUB), rows), :]


def kernel(x, p, norm_g, w_in, conv_w, conv_b, lru_w_a, lru_b_a, lru_w_x, lru_b_x, lru_lambda, pool_w, pool_scale, w_proj_lru, w_proj_pool, w_out, ple_norm_g, w_ple_gate, w_ple_proj, final_g, loss_target, m_norm_g, m_w_in, m_conv_w, m_conv_b, m_lru_w_a, m_lru_b_a, m_lru_w_x, m_lru_b_x, m_lru_lambda, m_pool_w, m_pool_scale, m_w_proj_lru, m_w_proj_pool, m_w_out, m_ple_norm_g, m_w_ple_gate, m_w_ple_proj, m_final_g, v_norm_g, v_w_in, v_conv_w, v_conv_b, v_lru_w_a, v_lru_b_a, v_lru_w_x, v_lru_b_x, v_lru_lambda, v_pool_w, v_pool_scale, v_w_proj_lru, v_w_proj_pool, v_w_out, v_ple_norm_g, v_w_ple_gate, v_w_ple_proj, v_final_g):
    nb, s, d = x.shape
    t = nb * s
    tm = min(TILE_M, s)
    sec_idx = (4 * lax.axis_index("x") + 2 * lax.axis_index("y") + lax.axis_index("c")).astype(jnp.int32)
    chip_idx = (2 * lax.axis_index("x") + lax.axis_index("y")).astype(jnp.int32)
    core_idx = lax.axis_index("c").astype(jnp.int32)

    h, ht, win, cw = _pre(
        x, norm_g, [w_in[0], conv_w[0]], [True, False],
        [(d, N_DEV * w_in.shape[2]), (conv_w.shape[1], N_DEV * conv_w.shape[2])],
        [_col_slicer(w_in.shape[2]), _col_slicer(conv_w.shape[2])], tm)
    late_shards = [w_proj_lru[0], w_proj_pool[0], w_out[0], w_ple_gate[0], w_ple_proj[0]]
    late_shapes = [(N_DEV * w_proj_lru.shape[1], d), (w_proj_pool.shape[1], N_DEV * w_proj_pool.shape[2]),
                   (N_DEV * w_out.shape[1], d), (N_DEV * w_ple_gate.shape[1], d),
                   (w_ple_proj.shape[1], N_DEV * w_ple_proj.shape[2])]
    late_slicers = [_row_slicer(w_proj_lru.shape[1]), _col_slicer(w_proj_pool.shape[2]), _row_slicer(w_out.shape[1]),
                    _row_slicer(w_ple_gate.shape[1]), _col_slicer(w_ple_proj.shape[2])]

    lw = cw.shape[1]
    pwid = pool_scale.shape[1]
    wa = lru_w_a[0].astype(_MM)
    wx = lru_w_x[0].astype(_MM)
    pw = pool_w[0].astype(_MM)
    ba, bx = lru_b_a[0], lru_b_x[0]
    gf = final_g.reshape(1, d)
    rows_of = lambda g: g.reshape(N_DEV, g.shape[0] // N_DEV, g.shape[1])
    core = core_idx.reshape(1)

    fwd_out = _fwd_seq(
        h, win, cw, conv_b, wa, ba, wx, bx, lru_lambda, pw, pool_scale, late_shards, late_shapes, late_slicers, tm)
    z, hl, ya, yb = fwd_out[:4]
    saved = fwd_out[4:N_SEQ_SAVED]
    wpl, wpp, wout, wpg, wpe = fwd_out[N_SEQ_SAVED:]
    x2d = x.reshape(t, d)
    (dya, dyb, dm, dx1, g_pl, g_pp, g_out, g_pg, g_pe, pack_mid) = _mid(
        ya.reshape(t, lw), yb.reshape(t, pwid), h.reshape(t, d), win, x2d, p[0].reshape(t, -1),
        loss_target.reshape(t, d), wpl, wpp, wout, wpg, wpe, ple_norm_g, gf, tm)
    grads_mid = [rows_of(g_pl), g_pp, rows_of(g_out), rows_of(g_pg), g_pe]
    seq_out = _bwd_seq(z, hl, saved, dya.reshape(nb, s, lw), dyb.reshape(nb, s, pwid), dm.reshape(nb, s, -1), cw,
                       wa, wx, lru_lambda, pw, pool_scale, pack_mid, grads_mid, tm)
    dz, sib_mid = seq_out[0], list(seq_out[2:])
    small = seq_out[1].reshape(N_DEV, SMALL_SEC, HEAD)
    sums_mid = _chip_sums(core, grads_mid, sib_mid, [_WIRE] * len(grads_mid), "chip_sums_mid")
    win_out = _bwd_win(ht, dz, sums_mid, small)
    g_in, sib_in, chips_mid, gsmall = win_out[0], win_out[1], list(win_out[2:-3]), win_out[-1]
    (sums_in,) = _chip_sums(core, [g_in], [sib_in], [_WIRE], "chip_sums_in")
    grad_x, chips_in, g0_parts = _bwd_dx(dz.reshape(t, -1), x2d, dx1, win, norm_g, [sums_in], tm)

    grads = [g_in] + grads_mid
    recv_sib = [sib_in] + sib_mid
    recv_chips = [chips_in] + chips_mid
    wmv = [(w_in[0], m_w_in[0], v_w_in[0]), (w_proj_lru[0], m_w_proj_lru[0], v_w_proj_lru[0]),
           (w_proj_pool[0], m_w_proj_pool[0], v_w_proj_pool[0]), (w_out[0], m_w_out[0], v_w_out[0]),
           (w_ple_gate[0], m_w_ple_gate[0], v_w_ple_gate[0]), (w_ple_proj[0], m_w_ple_proj[0], v_w_ple_proj[0])]
    big = _adam_sections(jnp.stack([sec_idx, chip_idx]), grads, recv_sib, recv_chips, wmv)

    R = _SEQ_ROWS
    vec = lambda r0, *t: (r0, tuple(a.reshape(1, -1) for a in t))
    mat = lambda r0, *t: (r0, tuple(a.reshape(-1, HEAD) for a in t))
    norm_wmv = vec(0, norm_g, m_norm_g, v_norm_g)[1]
    vec_params = [vec(R["dcb"][0], conv_b, m_conv_b, v_conv_b),
                  vec(R["dlam"][0], lru_lambda, m_lru_lambda, v_lru_lambda),
                  vec(R["dps"][0], pool_scale, m_pool_scale, v_pool_scale),
                  vec(_OFF_MID, ple_norm_g, m_ple_norm_g, v_ple_norm_g),
                  vec(_OFF_MID + SUB, final_g, m_final_g, v_final_g)]
    mat_params = [mat(R["dwa"][0], lru_w_a, m_lru_w_a, v_lru_w_a), mat(R["dba"][0], lru_b_a, m_lru_b_a, v_lru_b_a),
                  mat(R["dwx"][0], lru_w_x, m_lru_w_x, v_lru_w_x), mat(R["dbx"][0], lru_b_x, m_lru_b_x, v_lru_b_x),
                  mat(R["dpw"][0], pool_w, m_pool_w, v_pool_w)]
    conv_wmv = (conv_w[0], m_conv_w[0], v_conv_w[0])
    small_out, loss_row = _adam_small(sec_idx.reshape(1), gsmall, g0_parts, norm_wmv, vec_params, mat_params, conv_wmv)
    loss = loss_row[0, 0]

    res = {}
    names_small = ["norm_g", "conv_b", "lru_lambda", "pool_scale", "ple_norm_g", "final_g",
                   "lru_w_a", "lru_b_a", "lru_w_x", "lru_b_x", "pool_w", "conv_w"]
    shapes = {"norm_g": norm_g, "conv_b": conv_b, "lru_lambda": lru_lambda, "pool_scale": pool_scale,
              "ple_norm_g": ple_norm_g, "final_g": final_g, "lru_w_a": lru_w_a, "lru_b_a": lru_b_a, "lru_w_x": lru_w_x,
              "lru_b_x": lru_b_x, "pool_w": pool_w, "conv_w": conv_w}
    for name, quad in zip(names_small, small_out):
        res[name] = [a.reshape(shapes[name].shape) for a in quad]
    names_big = ["w_in", "w_proj_lru", "w_proj_pool", "w_out", "w_ple_gate", "w_ple_proj"]
    for name, quad, (w, _, _) in zip(names_big, big, wmv):
        res[name] = [a.reshape((1,) + w.shape) for a in quad]
    order = ["norm_g", "w_in", "conv_w", "conv_b", "lru_w_a", "lru_b_a", "lru_w_x", "lru_b_x", "lru_lambda", "pool_w",
             "pool_scale", "w_proj_lru", "w_proj_pool", "w_out", "ple_norm_g", "w_ple_gate", "w_ple_proj", "final_g"]
    out = [loss, grad_x.reshape(nb, s, d)]
    for kind in range(4):
        out += [res[name][kind] for name in order]
    return tuple(out)
```

```python
import jax
import jax.numpy as jnp
from jax import lax
from jax.experimental import pallas as pl
from jax.experimental.pallas import tpu as pltpu

F32 = jnp.float32
_MM = jnp.bfloat16
_WIRE = jnp.bfloat16

EPS = 1e-6
LRU_C = 8.0
MULT_SQ_FLOOR = 1e-30
POOL_WINDOWS = (2, 4, 8, 16)
N_HEADS = 8
HEAD = 128
HALO = 16
SUB = 8

ADAM_LR = 0.001
ADAM_B1 = 0.9
ADAM_B2 = 0.999
ADAM_EPS = 1e-08
ADAM_WD = 0.01
ADAM_STEP = 10

N_DEV = 8
MESH = pl.DeviceIdType.MESH
VMEM_LIMIT = 60 * 1024 * 1024
TILE_M = 256
N_SEQ_SAVED = 10

_SEQ_ROWS = {"dwa": (0, 1024), "dwx": (1024, 1024), "dpw": (2048, 512), "dba": (2560, 8), "dbx": (2568, 8),
             "dcb": (2576, 8), "dlam": (2584, 8), "dcw": (2592, 32), "dps": (2624, 8)}
SEQ_PACK_ROWS = 2632
MID_PACK_ROWS = 24
MID_LOSS_ROW = 16
INP_PACK_ROWS = 8
SMALL_ROWS = 3072
SMALL_SEC = SMALL_ROWS // N_DEV
_OFF_MID = SEQ_PACK_ROWS


def _dot(a, b):
    return jnp.dot(a, b, preferred_element_type=F32)


def _dot_nt(a, b):
    return lax.dot_general(a, b, (((1,), (1,)), ((), ())), preferred_element_type=F32)


def _dot_tn(a, b):
    return lax.dot_general(a, b, (((0,), (0,)), ((), ())), preferred_element_type=F32)


def _sigmoid(v):
    return 0.5 * jnp.tanh(0.5 * v) + 0.5


def _lru_mult(log_a, a):
    m2 = jnp.maximum(-jnp.tanh(log_a) * (1.0 + a * a), MULT_SQ_FLOOR)
    inv = lax.rsqrt(m2)
    return m2 * inv, inv


def _rows_back(v, n):
    return pltpu.roll(v, n, 0) if n else v


def _rows_ahead(v, n):
    return pltpu.roll(v, v.shape[0] - n, 0) if n else v


def _softplus_neg(lam):
    e = jnp.exp(-jnp.abs(lam))
    w = 1.0 + e
    l1p = jnp.where(w == 1.0, e, jnp.log(w) * (e / (w - 1.0)))
    return jnp.maximum(-lam, 0.0) + l1p


def _rsqrt_mean_sq(v):
    return lax.rsqrt(jnp.mean(v * v, axis=-1, keepdims=True) + EPS)


def _colsum(v):
    return jnp.sum(v, axis=0, keepdims=True)


def _const_spec(shape):
    nd = len(shape)
    return pl.BlockSpec(shape, lambda *_: (0,) * nd, pipeline_mode=pl.Buffered(1))


HBM_SPEC = pl.BlockSpec(memory_space=pl.ANY)


def _my_pos():
    return lax.axis_index("x"), lax.axis_index("y"), lax.axis_index("c")


def _other_chips(x, y):
    return [(1 - x, y), (x, 1 - y), (1 - x, 1 - y)]


def _gather_sems(nw):
    return [pltpu.SemaphoreType.DMA((nw, 7)), pltpu.SemaphoreType.DMA((nw, 7)), pltpu.SemaphoreType.DMA((nw,))]


def _gather_phases(srcs, outs, slicers, send_sems, recv_sems, local_sems, relay_diagonal=False):
    n_direct = 2 if relay_diagonal else 3
    nw = len(srcs)
    x, y, c = _my_pos()
    me, sibling = (x, y, c), (x, y, 1 - c)
    chips = _other_chips(x, y)

    def part(w, pos):
        return slicers[w](outs[w], 4 * pos[0] + 2 * pos[1] + pos[2])

    def copy(w, k, block, to, src=None):
        return pltpu.make_async_remote_copy(
            src_ref=part(w, block) if src is None else src, dst_ref=part(w, block),
            send_sem=send_sems.at[w, k], recv_sem=recv_sems.at[w, k], device_id=to, device_id_type=MESH)

    def mine():
        return [pltpu.make_async_copy(srcs[w], part(w, me), local_sems.at[w]) for w in range(nw)]

    def first():
        out = []
        for w in range(nw):
            out.append(copy(w, 0, me, sibling, src=srcs[w]))
            out += [copy(w, 1 + j, me, (*chips[j], c), src=srcs[w]) for j in range(n_direct)]
        return out

    def relay(w, j):
        return copy(w, 3, (*chips[j], c), (*chips[1 - j], c))

    def relays():
        return [relay(w, 0) for w in range(nw)] if relay_diagonal else []

    def passed():
        return [copy(w, 4 + j, (*chip, c), sibling) for j, chip in enumerate(chips) for w in range(nw)]

    def start():
        for cp in mine() + first():
            cp.start()

    def forward():
        for j, chip in enumerate(chips):
            for w in range(nw):
                copy(w, 1 + j, (*chip, c), me).wait_recv()
                copy(w, 4 + j, (*chip, c), sibling).start()
            if relay_diagonal and j < 2:
                @pl.when(c == j)
                def _():
                    for w in range(nw):
                        relay(w, j).start()

    def finish():
        for w in range(nw):
            copy(w, 0, sibling, me).wait_recv()
            for j, chip in enumerate(chips):
                copy(w, 4 + j, (*chip, 1 - c), me).wait_recv()
        for cp in first() + relays() + passed():
            cp.wait_send()
        for cp in mine():
            cp.wait()

    return start, forward, finish


def _sibling_copies(srcs, outs, send_sems, recv_sems):
    x, y, c = _my_pos()
    return [pltpu.make_async_remote_copy(
        src_ref=srcs[w].at[2 * q + (1 - c)], dst_ref=outs[w].at[q], send_sem=send_sems.at[w, q],
        recv_sem=recv_sems.at[w, q], device_id=(x, y, 1 - c), device_id_type=MESH)
        for w in range(len(srcs)) for q in range(4)]


def _chip_copies(srcs, outs, send_sems, recv_sems, which=None):
    x, y, c = _my_pos()
    return [pltpu.make_async_remote_copy(
        src_ref=srcs[w].at[2 * px + py], dst_ref=outs[w].at[k], send_sem=send_sems.at[w, k],
        recv_sem=recv_sems.at[w, k], device_id=(px, py, c), device_id_type=MESH)
        for w in (range(len(srcs)) if which is None else which) for k, (px, py) in enumerate(_other_chips(x, y))]


def _lru_gates(xc, h, wa_ref, ba_ref, wx_ref, bx_ref, neg_c_sp):
    xc16 = xc.astype(_MM)
    r = _sigmoid(_dot(xc16, wa_ref[h]) + ba_ref[pl.ds(h, 1), :])
    i = _sigmoid(_dot(xc16, wx_ref[h]) + bx_ref[pl.ds(h, 1), :])
    return r, i, neg_c_sp * r


def _conv_head(xa_ext, cw_ref, cb_ref, cols, tm):
    ext = xa_ext[:, cols]
    xc = cb_ref[:, cols] + cw_ref[pl.ds(3, 1), cols] * ext[HALO:HALO + tm]
    for k in range(3):
        xc = xc + cw_ref[pl.ds(k, 1), cols] * _rows_back(ext, 3 - k)[HALO:HALO + tm]
    return xc


def _window_sum(ext, k, shift):
    n = 1
    while n < k:
        ext = ext + shift(ext, n)
        n *= 2
    return ext


def _pool_diff(xb_ext, g, k, pos, tm):
    cols = slice(g * HEAD, (g + 1) * HEAD)
    ext = xb_ext[:, cols]
    ws = _window_sum(ext, k, _rows_back)[HALO:HALO + tm]
    inv_cnt = 1.0 / jnp.minimum(pos + 1, k).astype(F32)
    return ws * inv_cnt - ext[HALO:HALO + tm], inv_cnt


def _staged_sources(shards, as_operand):
    in_specs = [_const_spec(a.shape) if c else HBM_SPEC for a, c in zip(shards, as_operand)]
    stages = [pltpu.VMEM(a.shape, _MM) for a, c in zip(shards, as_operand) if c]
    dtypes = [_MM if c else a.dtype for a, c in zip(shards, as_operand)]
    return in_specs, stages, dtypes


def _gather_sources(srcs, stages, as_operand):
    stages = iter(stages)
    return [next(stages) if c else src for src, c in zip(srcs, as_operand)]


def _fill_stages(srcs, staged, as_operand):
    for src, dst, c in zip(srcs, staged, as_operand):
        if c:
            dst[...] = src[...].astype(dst.dtype)


def _pre(x, g0, shards, as_operand, shapes, slicers, tm):
    nb, s, d = x.shape
    nt = s // tm
    nl = len(shards)
    shard_specs, stage_shapes, wire = _staged_sources(shards, as_operand)

    def body(x_ref, g0_ref, *rest):
        src, rest = rest[:nl], rest[nl:]
        h_ref, ht_ref = rest[:2]
        out, rest = rest[2:2 + nl], rest[2 + nl:]
        (send_sems, recv_sems, local_sems), stages = rest[:3], rest[3:]
        step_no = pl.program_id(0) * nt + pl.program_id(1)
        staged = _gather_sources(src, stages, as_operand)
        g_start, g_forward, g_finish = _gather_phases(staged, out, slicers, send_sems, recv_sems, local_sems,
                                                      relay_diagonal=True)

        @pl.when(step_no == 0)
        def _():
            _fill_stages(src, staged, as_operand)
            g_start()

        xv = x_ref[...]
        h16 = ((xv * _rsqrt_mean_sq(xv)) * g0_ref[...]).astype(_MM)
        h_ref[...] = h16
        ht_ref[...] = h16.T

        @pl.when(step_no == nb * nt - 1)
        def _():
            g_forward()
            g_finish()

    tile = pl.BlockSpec((None, tm, d), lambda b, j: (b, j, 0))
    return pl.pallas_call(
        body, name="pre", grid=(nb, nt),
        in_specs=[tile, _const_spec(g0.shape)] + shard_specs,
        out_specs=(tile, pl.BlockSpec((None, d, tm), lambda b, j: (b, 0, j))) + (HBM_SPEC,) * nl,
        out_shape=(jax.ShapeDtypeStruct((nb, s, d), _MM), jax.ShapeDtypeStruct((nb, d, s), _MM))
        + tuple(jax.ShapeDtypeStruct(shp, dt) for shp, dt in zip(shapes, wire)),
        scratch_shapes=_gather_sems(nl) + stage_shapes,
        compiler_params=pltpu.CompilerParams(dimension_semantics=("arbitrary", "arbitrary")),
    )(x, g0, *shards)


def _fwd_seq(h, win, cw, cb, wa, ba, wx, bx, lam, pw, ps, late_shards, late_shapes, late_slicers, tm):
    nb, s, d = h.shape
    lw = cw.shape[1]
    pwid = ps.shape[1]
    nc = 2 * lw + 2 * pwid
    nt = s // tm
    nl = len(late_shards)
    n_steps = nb * nt
    as_operand = [True] * nl
    shard_specs, stage_shapes, wire = _staged_sources(late_shards, as_operand)

    def body(h_ref, win_ref, cw_ref, cb_ref, wa_ref, ba_ref, wx_ref, bx_ref, lam_ref, pw_ref, ps_ref, *rest):
        late_src, rest = rest[:nl], rest[nl:]
        z_ref, hl_ref, ya_ref, yb_ref, xc_ref, r_ref, i_ref, a_buf, diff_ref, yp_ref = rest[:N_SEQ_SAVED]
        late_out, rest = rest[N_SEQ_SAVED:N_SEQ_SAVED + nl], rest[N_SEQ_SAVED + nl:]
        (xa_ext, xb_ext, carry, send_sems, recv_sems, local_sems), stages = rest[:6], rest[6:]
        j = pl.program_id(1)
        step_no = pl.program_id(0) * nt + j
        staged = _gather_sources(late_src, stages, as_operand)
        g_start, g_forward, g_finish = _gather_phases(staged, late_out, late_slicers, send_sems, recv_sems, local_sems)

        @pl.when(step_no == 0)
        def _():
            _fill_stages(late_src, staged, as_operand)
            g_start()

        pl.when(step_no == n_steps // 2)(g_forward)

        @pl.when(j == 0)
        def _():
            xa_ext[pl.ds(0, HALO), :] = jnp.zeros((HALO, lw), F32)
            xb_ext[pl.ds(0, HALO), :] = jnp.zeros((HALO, pwid), F32)
            carry[...] = jnp.zeros_like(carry)

        h16 = h_ref[...]
        xa = _dot(h16, win_ref[:, 0:lw])
        z_ref[:, 0:lw] = xa
        xa_ext[pl.ds(HALO, tm), :] = xa
        for c0 in range(lw, nc, lw):
            z_ref[:, c0:c0 + lw] = _dot(h16, win_ref[:, c0:c0 + lw])

        for h in range(N_HEADS):
            cols = slice(h * HEAD, (h + 1) * HEAD)
            xc = _conv_head(xa_ext, cw_ref, cb_ref, cols, tm)
            neg_c_sp = -LRU_C * _softplus_neg(lam_ref[:, cols])
            r, i, log_a = _lru_gates(xc, h, wa_ref, ba_ref, wx_ref, bx_ref, neg_c_sp)
            a = jnp.exp(log_a)
            xc_ref[:, cols] = xc
            r_ref[:, cols] = r
            i_ref[:, cols] = i
            a_buf[:, cols] = a
            hl_ref[:, cols] = _lru_mult(log_a, a)[0] * (i * xc)

        rows = lax.broadcasted_iota(jnp.int32, (SUB, lw), 0)

        def step(c, car):
            i0 = pl.multiple_of(c * SUB, SUB)
            av = a_buf[pl.ds(i0, SUB), :]
            bv = hl_ref[pl.ds(i0, SUB), :]
            for sh in (1, 2, 4):
                m = rows >= sh
                a_sh = jnp.where(m, pltpu.roll(av, sh, 0), 1.0)
                b_sh = jnp.where(m, pltpu.roll(bv, sh, 0), 0.0)
                bv = av * b_sh + bv
                av = av * a_sh
            hv = av * car + bv
            hl_ref[pl.ds(i0, SUB), :] = hv
            return jnp.broadcast_to(hv[SUB - 1:SUB, :], (SUB, lw))

        carry[...] = lax.fori_loop(0, tm // SUB, step, carry[...])

        ga = z_ref[:, lw:2 * lw]
        ya_ref[...] = (hl_ref[...] * (ga * _sigmoid(ga))).astype(ya_ref.dtype)

        xb_ext[pl.ds(HALO, tm), :] = z_ref[:, 2 * lw:2 * lw + pwid]
        pos = j * tm + lax.broadcasted_iota(jnp.int32, (tm, HEAD), 0)
        for g, k in enumerate(POOL_WINDOWS):
            cols = slice(g * HEAD, (g + 1) * HEAD)
            diff16 = _pool_diff(xb_ext, g, k, pos, tm)[0].astype(_MM)
            yp = _dot(diff16, pw_ref[g])
            diff_ref[:, cols] = diff16
            yp_ref[:, cols] = yp
            gb = z_ref[:, 2 * lw + pwid + g * HEAD:2 * lw + pwid + (g + 1) * HEAD]
            yb_ref[:, cols] = ((yp * ps_ref[:, cols]) * (gb * _sigmoid(gb))).astype(yb_ref.dtype)

        xa_ext[pl.ds(0, HALO), :] = xa_ext[pl.ds(tm, HALO), :]
        xb_ext[pl.ds(0, HALO), :] = xb_ext[pl.ds(tm, HALO), :]
        pl.when(step_no == n_steps - 1)(g_finish)

    tile = lambda w: pl.BlockSpec((None, tm, w), lambda b, j: (b, j, 0))
    return pl.pallas_call(
        body, name="fwd_seq", grid=(nb, nt),
        in_specs=[tile(d), _const_spec((d, nc)), _const_spec(cw.shape), _const_spec(cb.shape),
                  _const_spec(wa.shape), _const_spec(ba.shape), _const_spec(wx.shape), _const_spec(bx.shape),
                  _const_spec(lam.shape), _const_spec(pw.shape), _const_spec(ps.shape)] + shard_specs,
        out_specs=(tile(nc), tile(lw), tile(lw), tile(pwid), tile(lw), tile(lw), tile(lw), tile(lw), tile(pwid),
                   tile(pwid)) + (HBM_SPEC,) * nl,
        out_shape=(jax.ShapeDtypeStruct((nb, s, nc), F32), jax.ShapeDtypeStruct((nb, s, lw), F32),
                   jax.ShapeDtypeStruct((nb, s, lw), _MM), jax.ShapeDtypeStruct((nb, s, pwid), _MM))
        + (jax.ShapeDtypeStruct((nb, s, lw), F32),) * 4
        + (jax.ShapeDtypeStruct((nb, s, pwid), _MM), jax.ShapeDtypeStruct((nb, s, pwid), F32))
        + tuple(jax.ShapeDtypeStruct(shp, dt) for shp, dt in zip(late_shapes, wire)),
        scratch_shapes=[pltpu.VMEM((tm + HALO, lw), F32), pltpu.VMEM((tm + HALO, pwid), F32),
                        pltpu.VMEM((SUB, lw), F32)] + _gather_sems(nl) + stage_shapes,
        compiler_params=pltpu.CompilerParams(dimension_semantics=("arbitrary", "arbitrary"), vmem_limit_bytes=VMEM_LIMIT),
    )(h, win, cw, cb, wa, ba, wx, bx, lam, pw, ps, *late_shards)


def _store_chunks(ref, row0, vec):
    for h in range(vec.shape[1] // HEAD):
        ref[pl.ds(row0 + h, 1), :] += vec[:, h * HEAD:(h + 1) * HEAD]


def _mid(ya, yb, h, win, x, p, tgt, wpl, wpp, wout, wpg, wpe, g1, gf, tm):
    t, d = x.shape
    lw = ya.shape[1]
    pwid = yb.shape[1]
    pdim = p.shape[1]
    n = t // tm
    sec_pp = wpp.shape[1] // N_DEV
    sec_pe = wpe.shape[1] // N_DEV
    sec_rows = wpl.shape[0] // N_DEV
    grad_secs = [(sec_rows, d), (wpp.shape[0], sec_pp), (sec_rows, d), (sec_rows, d), (wpe.shape[0], sec_pe)]
    n_win_blocks = win.shape[1] // d
    win_cols = lambda c: pl.BlockSpec((d, d), lambda i: (0, c), pipeline_mode=pl.Buffered(1))

    def body(ya_ref, yb_ref, h_ref, wma_ref, wmb_ref, x_ref, p_ref, tgt_ref, wpl_ref, wpp_ref, wout_ref, wpg_ref, wpe_ref,
             g1_ref, gf_ref,
             dya_ref, dyb_ref, dm_ref, dx1_ref, gpl_ref, gpp_ref, gout_ref, gpg_ref, gpe_ref, pack_ref,
             sib_pl, sib_pp, sib_out, sib_pg, sib_pe,
             acc_pl, acc_pp, acc_out, acc_pg, acc_pe, out_sems, sib_send, sib_recv):
        i = pl.program_id(0)

        @pl.when(i == 0)
        def _():
            for acc in (acc_pl, acc_pp, acc_out, acc_pg, acc_pe):
                acc[...] = jnp.zeros_like(acc)
            pack_ref[...] = jnp.zeros_like(pack_ref)

        ya16 = ya_ref[...]
        yb16 = yb_ref[...]
        br_a = _dot(ya16, wpl_ref[...])
        br_b = _dot(yb16, wpp_ref[...])
        h16 = h_ref[...]
        sa = _sigmoid(_dot(h16, wma_ref[...]))
        sb = _sigmoid(_dot(h16, wmb_ref[...]))
        mg16 = (sa * br_a + sb * br_b).astype(_MM)
        x1 = x_ref[...] + _dot(mg16, wout_ref[...])
        r1 = _rsqrt_mean_sq(x1)
        n1 = x1 * r1
        h116 = (n1 * g1_ref[...]).astype(_MM)
        gate = _sigmoid(_dot(h116, wpg_ref[...]))
        p16 = p_ref[...].astype(_MM)
        pe = _dot(p16, wpe_ref[...])
        x2 = x1 + gate * pe
        r2 = _rsqrt_mean_sq(x2)
        n2 = x2 * r2
        err = n2 * gf_ref[...] - tgt_ref[...]
        sq = jnp.sum(_colsum(err * err), axis=1, keepdims=True)
        pack_ref[pl.ds(MID_LOSS_ROW, 1), :] += jnp.broadcast_to(sq * (0.5 / d), (1, HEAD))

        dy = err * (1.0 / d)
        _store_chunks(pack_ref, SUB, _colsum(dy * n2))
        dn2 = dy * gf_ref[...]
        dx2 = r2 * (dn2 - n2 * jnp.mean(dn2 * n2, axis=-1, keepdims=True))
        dpe16 = (dx2 * gate).astype(_MM)
        dpg16 = ((dx2 * pe) * (gate * (1.0 - gate))).astype(_MM)
        acc_pe[...] += _dot_tn(p16, dpe16)
        acc_pg[...] += _dot_tn(h116, dpg16)
        dh1 = _dot_nt(dpg16, wpg_ref[...])
        _store_chunks(pack_ref, 0, _colsum(dh1 * n1))
        dn1 = dh1 * g1_ref[...]
        dx1 = dx2 + r1 * (dn1 - n1 * jnp.mean(dn1 * n1, axis=-1, keepdims=True))
        dx1_ref[...] = dx1
        dx116 = dx1.astype(_MM)
        acc_out[...] += _dot_tn(mg16, dx116)
        dmg = _dot_nt(dx116, wout_ref[...])
        da16 = (dmg * sa).astype(_MM)
        db16 = (dmg * sb).astype(_MM)
        dm_ref[:, 0:d] = ((dmg * br_a) * (sa * (1.0 - sa))).astype(dm_ref.dtype)
        dm_ref[:, d:2 * d] = ((dmg * br_b) * (sb * (1.0 - sb))).astype(dm_ref.dtype)
        acc_pl[...] += _dot_tn(ya16, da16)
        acc_pp[...] += _dot_tn(yb16, db16)
        dya_ref[...] = _dot_nt(da16, wpl_ref[...])
        dyb_ref[...] = _dot_nt(db16, wpp_ref[...])

        @pl.when(i == n - 1)
        def _():
            writes = []
            for k in range(N_DEV):
                rows = pl.ds(k * sec_rows, sec_rows)
                for acc, out in ((acc_pl, gpl_ref), (acc_out, gout_ref), (acc_pg, gpg_ref)):
                    writes.append((acc.at[rows, :], out.at[k]))
                writes.append((acc_pp.at[:, pl.ds(k * sec_pp, sec_pp)], gpp_ref.at[k]))
                writes.append((acc_pe.at[:, pl.ds(k * sec_pe, sec_pe)], gpe_ref.at[k]))
            writes = [pltpu.make_async_copy(src, dst, out_sems.at[n]) for n, (src, dst) in enumerate(writes)]
            for cp in writes:
                cp.start()
            for cp in writes:
                cp.wait()
            to_sibling = _sibling_copies([gpl_ref, gpp_ref, gout_ref, gpg_ref, gpe_ref],
                                         [sib_pl, sib_pp, sib_out, sib_pg, sib_pe], sib_send, sib_recv)
            for cp in to_sibling:
                cp.start()
            for cp in to_sibling:
                cp.wait()

    tile = lambda w: pl.BlockSpec((tm, w), lambda i: (i, 0))
    any_spec = pl.BlockSpec(memory_space=pl.ANY)
    full = lambda shape: pl.BlockSpec(shape, lambda i: (0,) * len(shape))
    return pl.pallas_call(
        body, name="mid", grid=(n,),
        in_specs=[tile(lw), tile(pwid), tile(d), win_cols(n_win_blocks - 2), win_cols(n_win_blocks - 1),
                  tile(d), tile(pdim), tile(d),
                  _const_spec(wpl.shape), _const_spec(wpp.shape), _const_spec(wout.shape), _const_spec(wpg.shape),
                  _const_spec(wpe.shape), _const_spec(g1.shape), _const_spec(gf.shape)],
        out_specs=(tile(lw), tile(pwid), tile(2 * d), tile(d), any_spec, any_spec, any_spec, any_spec, any_spec,
                   full((MID_PACK_ROWS, HEAD))) + (any_spec,) * 5,
        out_shape=(jax.ShapeDtypeStruct((t, lw), F32), jax.ShapeDtypeStruct((t, pwid), F32),
                   jax.ShapeDtypeStruct((t, 2 * d), _MM), jax.ShapeDtypeStruct((t, d), F32),
                   *(jax.ShapeDtypeStruct((N_DEV,) + sec, F32) for sec in grad_secs),
                   jax.ShapeDtypeStruct((MID_PACK_ROWS, HEAD), F32),
                   *(jax.ShapeDtypeStruct((N_DEV // 2,) + sec, F32) for sec in grad_secs)),
        scratch_shapes=[pltpu.VMEM(wpl.shape, F32), pltpu.VMEM(wpp.shape, F32), pltpu.VMEM(wout.shape, F32),
                        pltpu.VMEM(wpg.shape, F32), pltpu.VMEM(wpe.shape, F32),
                        pltpu.SemaphoreType.DMA((5 * N_DEV,)), pltpu.SemaphoreType.DMA((5, N_DEV // 2)),
                        pltpu.SemaphoreType.DMA((5, N_DEV // 2))],
        compiler_params=pltpu.CompilerParams(dimension_semantics=("arbitrary",), vmem_limit_bytes=VMEM_LIMIT),
    )(ya, yb, h, win, win, x, p, tgt, wpl, wpp, wout, wpg, wpe, g1, gf)


def _bwd_seq(z, hl, saved, dya, dyb, dm, cw, wa, wx, lam, pw, ps, pack_mid, sib_grads, tm):
    nb, s, zw = z.shape
    lw = cw.shape[1]
    pwid = ps.shape[1]
    nt = s // tm
    nc = zw + dm.shape[2]
    R = _SEQ_ROWS
    ns = len(sib_grads)

    def body(zq_ref, hl_ref, hlh_ref, xc_buf, r_buf, i_buf, a_ref, diff_ref, yp_ref, dya_ref, dyb_ref, dm_ref,
             cw_ref, wa_ref, wx_ref, lam_ref, pw_ref, ps_ref, pack_mid_ref, *rest):
        sib_src, rest = rest[:ns], rest[ns:]
        dz_ref, pack_ref = rest[:2]
        sib_out, rest = rest[2:2 + ns], rest[2 + ns:]
        a_ext, an_buf, hl_ext, dh_buf, dxc_ext, q_ext, carry, send_sems, recv_sems = rest
        b = pl.program_id(0)
        j = pl.program_id(1)
        jr = nt - 1 - j
        has_prev = jr > 0

        @pl.when((b == 0) & (j == 0))
        def _():
            for cp in _chip_copies(sib_src, sib_out, send_sems, recv_sems):
                cp.start()

        dz_ref[:, zw:nc] = dm_ref[...]

        @pl.when((b == 0) & (j == 0))
        def _():
            pack_ref[...] = jnp.zeros_like(pack_ref)

        @pl.when(j == 0)
        def _():
            a_ext[pl.ds(tm, SUB), :] = jnp.zeros((SUB, lw), F32)
            dxc_ext[pl.ds(tm, HALO), :] = jnp.zeros((HALO, lw), F32)
            q_ext[pl.ds(tm, HALO), :] = jnp.zeros((HALO, pwid), F32)
            carry[...] = jnp.zeros_like(carry)

        hl_ext[pl.ds(0, SUB), :] = jnp.where(has_prev, hlh_ref[...], 0.0)
        hl_ext[pl.ds(SUB, tm), :] = hl_ref[...]
        a_ext[pl.ds(0, tm), :] = a_ref[...]
        for h in range(N_HEADS):
            cols = slice(h * HEAD, (h + 1) * HEAD)
            ga = zq_ref[:, lw + h * HEAD:lw + (h + 1) * HEAD]
            sg = _sigmoid(ga)
            dyav = dya_ref[:, cols]
            dh_buf[:, cols] = dyav * (ga * sg)
            dga = (dyav * hl_ext[pl.ds(SUB, tm), cols]) * (sg * (1.0 + ga * (1.0 - sg)))
            dz_ref[:, lw + h * HEAD:lw + (h + 1) * HEAD] = dga.astype(dz_ref.dtype)
        an_buf[...] = _rows_ahead(a_ext[...], 1)[0:tm]

        rows = lax.broadcasted_iota(jnp.int32, (SUB, lw), 0)
        nch = tm // SUB

        def step(c, car):
            i0 = pl.multiple_of((nch - 1 - c) * SUB, SUB)
            cv = an_buf[pl.ds(i0, SUB), :]
            bv = dh_buf[pl.ds(i0, SUB), :]
            for sh in (1, 2, 4):
                m = rows < SUB - sh
                c_sh = jnp.where(m, pltpu.roll(cv, SUB - sh, 0), 1.0)
                b_sh = jnp.where(m, pltpu.roll(bv, SUB - sh, 0), 0.0)
                bv = cv * b_sh + bv
                cv = cv * c_sh
            hv = cv * car + bv
            dh_buf[pl.ds(i0, SUB), :] = hv
            return jnp.broadcast_to(hv[0:1, :], (SUB, lw))

        carry[...] = lax.fori_loop(0, nch, step, carry[...])

        for h in range(N_HEADS):
            cols = slice(h * HEAD, (h + 1) * HEAD)
            lam_h = lam_ref[:, cols]
            neg_c_sp = -LRU_C * _softplus_neg(lam_h)
            xc = xc_buf[:, cols]
            r = r_buf[:, cols]
            i = i_buf[:, cols]
            a = a_ref[:, cols]
            mult, inv_mult = _lru_mult(neg_c_sp * r, a)
            dh = dh_buf[:, cols]
            dhx = dh * xc
            di = dhx * mult
            h_prev = _rows_back(hl_ext[:, cols], 1)[SUB:SUB + tm]
            dlog_a = (dh * h_prev) * a - (dhx * i) * ((a * a) * inv_mult)
            pack_ref[pl.ds(R["dlam"][0] + h, 1), :] += _colsum(dlog_a * r) * (LRU_C * _sigmoid(-lam_h))
            dpa = (dlog_a * neg_c_sp) * (r * (1.0 - r))
            dpx = di * (i * (1.0 - i))
            dpa16 = dpa.astype(_MM)
            dpx16 = dpx.astype(_MM)
            xc16 = xc.astype(_MM)
            dxc = dh * (mult * i) + _dot_nt(dpa16, wa_ref[h]) + _dot_nt(dpx16, wx_ref[h])
            pack_ref[pl.ds(R["dwa"][0] + h * HEAD, HEAD), :] += _dot_tn(xc16, dpa16)
            pack_ref[pl.ds(R["dwx"][0] + h * HEAD, HEAD), :] += _dot_tn(xc16, dpx16)
            pack_ref[pl.ds(R["dba"][0] + h, 1), :] += _colsum(dpa)
            pack_ref[pl.ds(R["dbx"][0] + h, 1), :] += _colsum(dpx)
            pack_ref[pl.ds(R["dcb"][0] + h, 1), :] += _colsum(dxc)
            dxc_ext[pl.ds(0, tm), cols] = dxc

        for h in range(N_HEADS):
            cols = slice(h * HEAD, (h + 1) * HEAD)
            dxc_all = dxc_ext[:, cols]
            xa = zq_ref[:, cols]
            dxa = None
            for k in range(4):
                dxc_k = _rows_ahead(dxc_all, 3 - k)[0:tm]
                pack_ref[pl.ds(R["dcw"][0] + SUB * k + h, 1), :] += _colsum(dxc_k * xa)
                term = cw_ref[pl.ds(k, 1), cols] * dxc_k
                dxa = term if dxa is None else dxa + term
            dz_ref[:, cols] = dxa.astype(dz_ref.dtype)

        pos = jr * tm + lax.broadcasted_iota(jnp.int32, (tm, HEAD), 0)
        for g, k in enumerate(POOL_WINDOWS):
            cols = slice(g * HEAD, (g + 1) * HEAD)
            inv_cnt = 1.0 / jnp.minimum(pos + 1, k).astype(F32)
            diff16 = diff_ref[:, cols]
            yp = yp_ref[:, cols]
            sc = ps_ref[:, cols]
            gb = zq_ref[:, 2 * lw + pwid + g * HEAD:2 * lw + pwid + (g + 1) * HEAD]
            sgb = _sigmoid(gb)
            dybv = dyb_ref[:, cols]
            dy_pool = dybv * (gb * sgb)
            dgb = (dybv * (yp * sc)) * (sgb * (1.0 + gb * (1.0 - sgb)))
            pack_ref[pl.ds(R["dps"][0] + g, 1), :] += _colsum(dy_pool * yp)
            dyp16 = (dy_pool * sc).astype(_MM)
            pack_ref[pl.ds(R["dpw"][0] + g * HEAD, HEAD), :] += _dot_tn(diff16, dyp16)
            ddiff = _dot_nt(dyp16, pw_ref[g])
            q_ext[pl.ds(0, tm), cols] = ddiff * inv_cnt
            dxb = _window_sum(q_ext[:, cols], k, _rows_ahead)[0:tm] - ddiff
            dz_ref[:, 2 * lw + g * HEAD:2 * lw + (g + 1) * HEAD] = dxb.astype(dz_ref.dtype)
            dz_ref[:, 2 * lw + pwid + g * HEAD:2 * lw + pwid + (g + 1) * HEAD] = dgb.astype(dz_ref.dtype)

        a_ext[pl.ds(tm, SUB), :] = a_ext[pl.ds(0, SUB), :]
        dxc_ext[pl.ds(tm, HALO), :] = dxc_ext[pl.ds(0, HALO), :]
        q_ext[pl.ds(tm, HALO), :] = q_ext[pl.ds(0, HALO), :]

        @pl.when((b == nb - 1) & (j == nt - 1))
        def _():
            pack_ref[pl.ds(_OFF_MID, MID_PACK_ROWS), :] = pack_mid_ref[...]
            for cp in _chip_copies(sib_src, sib_out, send_sems, recv_sems):
                cp.wait()

    rev = lambda w: pl.BlockSpec((None, tm, w), lambda b, j: (b, nt - 1 - j, 0))
    prev_rows = lambda rows, w: pl.BlockSpec(
        (None, rows, w), lambda b, j: (b, jnp.maximum((nt - 1 - j) * (tm // rows) - 1, 0), 0))
    return pl.pallas_call(
        body, name="bwd_seq", grid=(nb, nt),
        in_specs=[rev(zw), rev(lw), prev_rows(SUB, lw), rev(lw), rev(lw), rev(lw), rev(lw), rev(pwid), rev(pwid),
                  rev(lw), rev(pwid), rev(nc - zw),
                  _const_spec(cw.shape), _const_spec(wa.shape), _const_spec(wx.shape), _const_spec(lam.shape),
                  _const_spec(pw.shape), _const_spec(ps.shape), _const_spec(pack_mid.shape)] + [HBM_SPEC] * ns,
        out_specs=(rev(nc), pl.BlockSpec((SMALL_ROWS, HEAD), lambda b, j: (0, 0))) + (HBM_SPEC,) * ns,
        out_shape=(jax.ShapeDtypeStruct((nb, s, nc), _MM), jax.ShapeDtypeStruct((SMALL_ROWS, HEAD), F32))
        + tuple(jax.ShapeDtypeStruct((3,) + g.shape[1:], g.dtype) for g in sib_grads),
        scratch_shapes=[pltpu.VMEM((tm + SUB, lw), F32), pltpu.VMEM((tm, lw), F32), pltpu.VMEM((tm + SUB, lw), F32),
                        pltpu.VMEM((tm, lw), F32), pltpu.VMEM((tm + HALO, lw), F32),
                        pltpu.VMEM((tm + HALO, pwid), F32), pltpu.VMEM((SUB, lw), F32),
                        pltpu.SemaphoreType.DMA((ns, 3)), pltpu.SemaphoreType.DMA((ns, 3))],
        compiler_params=pltpu.CompilerParams(dimension_semantics=("arbitrary", "arbitrary"), vmem_limit_bytes=VMEM_LIMIT),
    )(z, hl, hl, *saved, dya, dyb, dm, cw, wa, wx, lam, pw, ps, pack_mid, *sib_grads)


def _bwd_win(ht, dz, small):
    nb, d, s = ht.shape
    nc = dz.shape[2]
    sec = nc // N_DEV
    n_chips = N_DEV // 2

    def body(ht_ref, dz_ref, small_src, gwin_ref, recv_ref, small_out, small_chips_out, gsmall_ref,
             acc, own4, sib4, csum, got3, total, local_sems, sib_send, sib_recv, small_send, small_recv,
             sc_send, sc_recv, ld_sems, g_send, g_recv, g_local):
        q = pl.program_id(0)
        b = pl.program_id(1)
        slot = q % 2
        x, y, c = _my_pos()

        def half(sl, core):
            return acc.at[sl, :, pl.ds(pl.multiple_of(core * sec, HEAD), sec)]

        def local_copy(qq, sl):
            return pltpu.make_async_copy(half(sl, c), gwin_ref.at[2 * qq + c], local_sems.at[sl])

        def sib_copy(qq, sl):
            return pltpu.make_async_remote_copy(
                src_ref=half(sl, 1 - c), dst_ref=recv_ref.at[qq], send_sem=sib_send.at[qq],
                recv_sem=sib_recv.at[qq], device_id=(x, y, 1 - c), device_id_type=MESH)

        step_no = q * nb + b
        small_sib = lambda: _sibling_copies([small_src], [small_out], small_send, small_recv)
        small_chip = lambda: _chip_copies([csum], [small_chips_out], sc_send, sc_recv)
        g_start, g_forward, g_finish = _gather_phases([total], [gsmall_ref], [_row_slicer(small.shape[1])],
                                                      g_send, g_recv, g_local)

        def load(srcs, dst):
            cps = [pltpu.make_async_copy(src, dst.at[k], ld_sems.at[k]) for k, src in enumerate(srcs)]
            for cp in cps:
                cp.start()
            for cp in cps:
                cp.wait()

        @pl.when(step_no == 0)
        def _():
            for cp in small_sib():
                cp.start()

        @pl.when(step_no == 1)
        def _():
            for cp in small_sib():
                cp.wait_recv()
            load([small_src.at[2 * qq + c] for qq in range(n_chips)], own4)
            load([small_out.at[qq] for qq in range(n_chips)], sib4)
            csum[...] = own4[...] + sib4[...]
            for cp in small_chip():
                cp.start()

        @pl.when(step_no == 2)
        def _():
            for cp in small_chip():
                cp.wait_recv()
            load([small_chips_out.at[k] for k in range(3)], got3)
            total[...] = ((csum[2 * x + y] + got3[0]) + got3[1]) + got3[2]
            g_start()

        pl.when(step_no == 4)(g_forward)

        @pl.when((q >= 2) & (b == 0))
        def _():
            local_copy(q - 2, slot).wait()
            sib_copy(q - 2, slot).wait_send()

        part = _dot(ht_ref[b], dz_ref[...])

        @pl.when(b == 0)
        def _():
            acc[slot] = part

        @pl.when(b != 0)
        def _():
            acc[slot] += part

        @pl.when(b == nb - 1)
        def _():
            local_copy(q, slot).start()
            sib_copy(q, slot).start()

        @pl.when((q == n_chips - 1) & (b == nb - 1))
        def _():
            for qq in (n_chips - 2, n_chips - 1):
                local_copy(qq, qq % 2).wait()
                sib_copy(qq, qq % 2).wait_send()
            for qq in range(n_chips):
                sib_copy(qq, 0).wait_recv()
            g_finish()
            for cp in small_sib() + small_chip():
                cp.wait_send()

    return pl.pallas_call(
        body, name="bwd_win", grid=(n_chips, nb),
        in_specs=[_const_spec(ht.shape), pl.BlockSpec((None, s, 2 * sec), lambda q, b: (b, 0, q))]
        + [HBM_SPEC],
        out_specs=(HBM_SPEC,) * 5,
        out_shape=(jax.ShapeDtypeStruct((N_DEV, d, sec), F32), jax.ShapeDtypeStruct((n_chips, d, sec), F32))
        + (jax.ShapeDtypeStruct((n_chips,) + small.shape[1:], small.dtype),
           jax.ShapeDtypeStruct((3,) + small.shape[1:], small.dtype),
           jax.ShapeDtypeStruct((N_DEV * small.shape[1], small.shape[2]), small.dtype)),
        scratch_shapes=[pltpu.VMEM((2, d, 2 * sec), F32)]
        + [pltpu.VMEM((n_chips,) + small.shape[1:], F32)] * 3
        + [pltpu.VMEM((3,) + small.shape[1:], F32), pltpu.VMEM(small.shape[1:], F32), pltpu.SemaphoreType.DMA((2,)),
           pltpu.SemaphoreType.DMA((n_chips,)), pltpu.SemaphoreType.DMA((n_chips,)),
           pltpu.SemaphoreType.DMA((1, 4)), pltpu.SemaphoreType.DMA((1, 4)),
           pltpu.SemaphoreType.DMA((1, 3)), pltpu.SemaphoreType.DMA((1, 3)), pltpu.SemaphoreType.DMA((n_chips,))]
        + _gather_sems(1),
        compiler_params=pltpu.CompilerParams(dimension_semantics=("arbitrary", "arbitrary"), vmem_limit_bytes=VMEM_LIMIT),
    )(ht, dz, small)


def _bwd_dx(dz, x, dx1, win, g0, chip_srcs, tm):
    t, d = x.shape
    nc = win.shape[1]
    n = t // tm
    nm = len(chip_srcs)

    def body(dz_ref, x_ref, dx1_ref, win_ref, g0_ref, *rest):
        chip_src = rest[:nm]
        gx_ref = rest[nm]
        chip_out = rest[nm + 1:2 * nm + 1]
        parts_ref = rest[2 * nm + 1]
        pack, chip_send, chip_recv, p_send, p_recv, p_local = rest[2 * nm + 2:]
        i = pl.program_id(0)
        x_, y_, c_ = _my_pos()
        chip_copies = lambda: _chip_copies(chip_src, chip_out, chip_send, chip_recv)

        @pl.when(i == 0)
        def _():
            pack[...] = jnp.zeros_like(pack)
            for cp in chip_copies():
                cp.start()

        xv = x_ref[...]
        r0 = _rsqrt_mean_sq(xv)
        xh = xv * r0
        dh = _dot_nt(dz_ref[...], win_ref[...])
        _store_chunks(pack, 0, _colsum(dh * xh))
        dxh = dh * g0_ref[...]
        gx_ref[...] = dx1_ref[...] + r0 * (dxh - xh * jnp.mean(dxh * xh, axis=-1, keepdims=True))

        @pl.when(i == n - 1)
        def _():
            for cp in chip_copies():
                cp.wait()
            row0 = pl.multiple_of((4 * x_ + 2 * y_ + c_) * INP_PACK_ROWS, SUB)
            mine = parts_ref.at[pl.ds(row0, INP_PACK_ROWS), :]
            local = pltpu.make_async_copy(pack, mine, p_local)
            local.start()
            sends = []
            for k in range(1, N_DEV):
                peer = tuple(1 - v if (k >> bit) & 1 else v for v, bit in ((x_, 2), (y_, 1), (c_, 0)))
                sends.append(pltpu.make_async_remote_copy(
                    src_ref=pack, dst_ref=mine, send_sem=p_send.at[k - 1], recv_sem=p_recv.at[k - 1],
                    device_id=peer, device_id_type=MESH))
            for cp in sends:
                cp.start()
            for cp in sends:
                cp.wait()
            local.wait()

    tile = lambda w: pl.BlockSpec((tm, w), lambda i: (i, 0))
    return pl.pallas_call(
        body, name="bwd_dx", grid=(n,),
        in_specs=[tile(nc), tile(d), tile(d), _const_spec(win.shape), _const_spec(g0.shape)] + [HBM_SPEC] * nm,
        out_specs=(tile(d),) + (HBM_SPEC,) * (nm + 1),
        out_shape=(jax.ShapeDtypeStruct((t, d), F32),)
        + tuple(jax.ShapeDtypeStruct((3,) + g.shape[1:], g.dtype) for g in chip_srcs)
        + (jax.ShapeDtypeStruct((N_DEV * INP_PACK_ROWS, HEAD), F32),),
        scratch_shapes=[pltpu.VMEM((INP_PACK_ROWS, HEAD), F32),
                        pltpu.SemaphoreType.DMA((nm, 3)), pltpu.SemaphoreType.DMA((nm, 3)),
                        pltpu.SemaphoreType.DMA((N_DEV - 1,)), pltpu.SemaphoreType.DMA((N_DEV - 1,)),
                        pltpu.SemaphoreType.DMA],
        compiler_params=pltpu.CompilerParams(dimension_semantics=("arbitrary",), vmem_limit_bytes=VMEM_LIMIT),
    )(dz, x, dx1, win, g0, *chip_srcs)


ROW_BLOCKS = 8


def _chip_sums(core_idx, grads, recvs, dtypes, name):
    nw = len(grads)

    def body(idx_ref, *refs):
        gs, rs, outs = refs[:nw], refs[nw:2 * nw], refs[2 * nw:]
        for g, r, o in zip(gs, rs, outs):
            o[...] = (g[...] + r[...]).astype(o.dtype)

    def blk(g):
        return (None,) + g.shape[1:]

    return pl.pallas_call(
        body, name=name,
        grid_spec=pltpu.PrefetchScalarGridSpec(
            num_scalar_prefetch=1, grid=(4,),
            in_specs=[pl.BlockSpec(blk(g), lambda q, s: (2 * q + s[0], 0, 0)) for g in grads]
            + [pl.BlockSpec(blk(g), lambda q, s: (q, 0, 0)) for g in grads],
            out_specs=tuple(pl.BlockSpec(blk(g), lambda q, s: (q, 0, 0)) for g in grads)),
        out_shape=tuple(jax.ShapeDtypeStruct((4,) + g.shape[1:], dt) for g, dt in zip(grads, dtypes)),
    )(core_idx, *grads, *recvs)


def _adamw(w, g, m, v):
    m = ADAM_B1 * m + (1.0 - ADAM_B1) * g
    v = ADAM_B2 * v + (1.0 - ADAM_B2) * (g * g)
    m_hat = m / (1.0 - ADAM_B1 ** ADAM_STEP)
    v_hat = v / (1.0 - ADAM_B2 ** ADAM_STEP)
    delta = -ADAM_LR * (m_hat / (jnp.sqrt(v_hat) + ADAM_EPS) + ADAM_WD * w)
    return delta, m, v


def _adam_sections(sec_idx, grads, recv_sib, recv_chips, wmv):
    nw = len(wmv)

    def body(idx_ref, *refs):
        gs, rs, cs = refs[:nw], refs[nw:2 * nw], refs[2 * nw:3 * nw]
        params = refs[3 * nw:6 * nw]
        outs = refs[6 * nw:]
        for w in range(nw):
            g = gs[w][...] + rs[w][...]
            for k in range(3):
                g = g + cs[w][k].astype(F32)
            wv, mv, vv = (params[3 * w + t][...] for t in range(3))
            delta, m_new, v_new = _adamw(wv, g, mv, vv)
            outs[4 * w][...] = g
            outs[4 * w + 1][...] = delta
            outs[4 * w + 2][...] = m_new
            outs[4 * w + 3][...] = v_new

    def rb(g):
        return g.shape[1] // ROW_BLOCKS

    in_specs = [pl.BlockSpec((None, rb(g), g.shape[2]), lambda i, s: (s[0], i, 0)) for g in grads]
    in_specs += [pl.BlockSpec((None, rb(g), g.shape[2]), lambda i, s: (s[1], i, 0)) for g in grads]
    in_specs += [pl.BlockSpec((3, rb(g), g.shape[2]), lambda i, s: (0, i, 0)) for g in grads]
    sec = lambda g: pl.BlockSpec((rb(g), g.shape[2]), lambda i, s: (i, 0))
    for w in range(nw):
        in_specs += [sec(grads[w])] * 3
    out_specs, out_shape = [], []
    for g in grads:
        out_specs += [sec(g)] * 4
        out_shape += [jax.ShapeDtypeStruct(g.shape[1:], F32)] * 4
    flat = [a for t in wmv for a in t]
    outs = pl.pallas_call(
        body, name="adam_sections",
        grid_spec=pltpu.PrefetchScalarGridSpec(num_scalar_prefetch=1, grid=(ROW_BLOCKS,), in_specs=in_specs,
                                               out_specs=tuple(out_specs)),
        out_shape=tuple(out_shape),
    )(sec_idx, *grads, *recv_sib, *recv_chips, *flat)
    return [outs[4 * w:4 * w + 4] for w in range(nw)]


def _adam_small(sec_idx, gpack, parts, parts_wmv, vec_params, mat_params, conv_wmv):
    items = [(0, parts_wmv, "parts")]
    items += [(r0, t, "vec") for r0, t in vec_params] + [(r0, t, "mat") for r0, t in mat_params]
    items.append((_SEQ_ROWS["dcw"][0], conv_wmv, "conv"))

    def body(idx_ref, g_ref, parts_ref, *refs):
        ins, outs = refs[:3 * len(items)], refs[3 * len(items):]
        for n, (r0, _, kind) in enumerate(items):
            w_ref, m_ref, v_ref = ins[3 * n:3 * n + 3]
            o = outs[4 * n:4 * n + 4]
            if kind == "mat":
                g = g_ref[pl.ds(r0, w_ref.shape[0]), :]
                res = (g,) + _adamw(w_ref[...], g, m_ref[...], v_ref[...])
                for ref, val in zip(o, res):
                    ref[...] = val
            elif kind in ("vec", "parts"):
                for h in range(w_ref.shape[1] // HEAD):
                    cols = slice(h * HEAD, (h + 1) * HEAD)
                    if kind == "vec":
                        g = g_ref[pl.ds(r0 + h, 1), :]
                    else:
                        g = parts_ref[pl.ds(h, 1), :]
                        for dev in range(1, N_DEV):
                            g = g + parts_ref[pl.ds(SUB * dev + h, 1), :]
                    res = (g,) + _adamw(w_ref[:, cols], g, m_ref[:, cols], v_ref[:, cols])
                    for ref, val in zip(o, res):
                        ref[:, cols] = val
            else:
                rows = lax.broadcasted_iota(jnp.int32, (SUB, HEAD), 0)
                for k in range(4):
                    blk = g_ref[pl.ds(r0 + SUB * k, SUB), :]
                    g = jnp.sum(jnp.where(rows == idx_ref[0], blk, 0.0), axis=0, keepdims=True)
                    row = pl.ds(k, 1)
                    res = (g,) + _adamw(w_ref[row, :], g, m_ref[row, :], v_ref[row, :])
                    for ref, val in zip(o, res):
                        ref[row, :] = val
        outs[4 * len(items)][...] = g_ref[pl.ds(_OFF_MID + MID_LOSS_ROW, 1), :]

    flat = [a for _, t, _ in items for a in t]
    full = lambda a: pl.BlockSpec(a.shape, lambda i, s: (0,) * a.ndim)
    out_specs, out_shape = [], []
    for _, t, _ in items:
        out_specs += [full(t[0])] * 4
        out_shape += [jax.ShapeDtypeStruct(t[0].shape, F32)] * 4
    loss_row = jax.ShapeDtypeStruct((1, HEAD), F32)
    out_specs.append(full(loss_row))
    out_shape.append(loss_row)
    outs = pl.pallas_call(
        body, name="adam_small",
        grid_spec=pltpu.PrefetchScalarGridSpec(num_scalar_prefetch=1, grid=(1,),
                                               in_specs=[full(gpack), full(parts)] + [full(a) for a in flat],
                                               out_specs=tuple(out_specs)),
        out_shape=tuple(out_shape),
    )(sec_idx, gpack, parts, *flat)
    return [outs[4 * n:4 * n + 4] for n in range(len(items))], outs[4 * len(items)]


def _col_slicer(width):
    return lambda ref, idx: ref.at[:, pl.ds(pl.multiple_of(idx * width, HEAD), width)]


def _row_slicer(rows):
    return lambda ref, idx: ref.at[pl.ds(pl.multiple_of(idx * rows, SUB), rows), :]


def kernel(x, p, norm_g, w_in, conv_w, conv_b, lru_w_a, lru_b_a, lru_w_x, lru_b_x, lru_lambda, pool_w, pool_scale, w_proj_lru, w_proj_pool, w_out, ple_norm_g, w_ple_gate, w_ple_proj, final_g, loss_target, m_norm_g, m_w_in, m_conv_w, m_conv_b, m_lru_w_a, m_lru_b_a, m_lru_w_x, m_lru_b_x, m_lru_lambda, m_pool_w, m_pool_scale, m_w_proj_lru, m_w_proj_pool, m_w_out, m_ple_norm_g, m_w_ple_gate, m_w_ple_proj, m_final_g, v_norm_g, v_w_in, v_conv_w, v_conv_b, v_lru_w_a, v_lru_b_a, v_lru_w_x, v_lru_b_x, v_lru_lambda, v_pool_w, v_pool_scale, v_w_proj_lru, v_w_proj_pool, v_w_out, v_ple_norm_g, v_w_ple_gate, v_w_ple_proj, v_final_g):
    nb, s, d = x.shape
    t = nb * s
    tm = min(TILE_M, s)
    sec_idx = (4 * lax.axis_index("x") + 2 * lax.axis_index("y") + lax.axis_index("c")).astype(jnp.int32)
    chip_idx = (2 * lax.axis_index("x") + lax.axis_index("y")).astype(jnp.int32)
    core_idx = lax.axis_index("c").astype(jnp.int32)

    h, ht, win, cw = _pre(
        x, norm_g, [w_in[0], conv_w[0]], [True, False],
        [(d, N_DEV * w_in.shape[2]), (conv_w.shape[1], N_DEV * conv_w.shape[2])],
        [_col_slicer(w_in.shape[2]), _col_slicer(conv_w.shape[2])], tm)
    late_shards = [w_proj_lru[0], w_proj_pool[0], w_out[0], w_ple_gate[0], w_ple_proj[0]]
    late_shapes = [(N_DEV * w_proj_lru.shape[1], d), (w_proj_pool.shape[1], N_DEV * w_proj_pool.shape[2]),
                   (N_DEV * w_out.shape[1], d), (N_DEV * w_ple_gate.shape[1], d),
                   (w_ple_proj.shape[1], N_DEV * w_ple_proj.shape[2])]
    late_slicers = [_row_slicer(w_proj_lru.shape[1]), _col_slicer(w_proj_pool.shape[2]), _row_slicer(w_out.shape[1]),
                    _row_slicer(w_ple_gate.shape[1]), _col_slicer(w_ple_proj.shape[2])]

    lw = cw.shape[1]
    pwid = pool_scale.shape[1]
    wa = lru_w_a[0].astype(_MM)
    wx = lru_w_x[0].astype(_MM)
    pw = pool_w[0].astype(_MM)
    ba, bx = lru_b_a[0], lru_b_x[0]
    gf = final_g.reshape(1, d)
    core = core_idx.reshape(1)

    fwd_out = _fwd_seq(
        h, win, cw, conv_b, wa, ba, wx, bx, lru_lambda, pw, pool_scale, late_shards, late_shapes, late_slicers, tm)
    z, hl, ya, yb = fwd_out[:4]
    saved = fwd_out[4:N_SEQ_SAVED]
    wpl, wpp, wout, wpg, wpe = fwd_out[N_SEQ_SAVED:]
    x2d = x.reshape(t, d)
    mid_out = _mid(
        ya.reshape(t, lw), yb.reshape(t, pwid), h.reshape(t, d), win, x2d, p[0].reshape(t, -1),
        loss_target.reshape(t, d), wpl, wpp, wout, wpg, wpe, ple_norm_g, gf, tm)
    dya, dyb, dm, dx1 = mid_out[:4]
    grads_mid, pack_mid, sib_mid = list(mid_out[4:9]), mid_out[9], list(mid_out[10:])
    sums_mid = _chip_sums(core, grads_mid, sib_mid, [_WIRE] * len(grads_mid), "chip_sums_mid")
    seq_out = _bwd_seq(z, hl, saved, dya.reshape(nb, s, lw), dyb.reshape(nb, s, pwid), dm.reshape(nb, s, -1), cw,
                       wa, wx, lru_lambda, pw, pool_scale, pack_mid, sums_mid, tm)
    dz, chips_mid = seq_out[0], list(seq_out[2:])
    small = seq_out[1].reshape(N_DEV, SMALL_SEC, HEAD)
    g_in, sib_in, _, _, gsmall = _bwd_win(ht, dz, small)
    (sums_in,) = _chip_sums(core, [g_in], [sib_in], [_WIRE], "chip_sums_in")
    grad_x, chips_in, g0_parts = _bwd_dx(dz.reshape(t, -1), x2d, dx1, win, norm_g, [sums_in], tm)

    grads = [g_in] + grads_mid
    recv_sib = [sib_in] + sib_mid
    recv_chips = [chips_in] + chips_mid
    wmv = [(w_in[0], m_w_in[0], v_w_in[0]), (w_proj_lru[0], m_w_proj_lru[0], v_w_proj_lru[0]),
           (w_proj_pool[0], m_w_proj_pool[0], v_w_proj_pool[0]), (w_out[0], m_w_out[0], v_w_out[0]),
           (w_ple_gate[0], m_w_ple_gate[0], v_w_ple_gate[0]), (w_ple_proj[0], m_w_ple_proj[0], v_w_ple_proj[0])]
    big = _adam_sections(jnp.stack([sec_idx, chip_idx]), grads, recv_sib, recv_chips, wmv)

    R = _SEQ_ROWS
    vec = lambda r0, *t: (r0, tuple(a.reshape(1, -1) for a in t))
    mat = lambda r0, *t: (r0, tuple(a.reshape(-1, HEAD) for a in t))
    norm_wmv = vec(0, norm_g, m_norm_g, v_norm_g)[1]
    vec_params = [vec(R["dcb"][0], conv_b, m_conv_b, v_conv_b),
                  vec(R["dlam"][0], lru_lambda, m_lru_lambda, v_lru_lambda),
                  vec(R["dps"][0], pool_scale, m_pool_scale, v_pool_scale),
                  vec(_OFF_MID, ple_norm_g, m_ple_norm_g, v_ple_norm_g),
                  vec(_OFF_MID + SUB, final_g, m_final_g, v_final_g)]
    mat_params = [mat(R["dwa"][0], lru_w_a, m_lru_w_a, v_lru_w_a), mat(R["dba"][0], lru_b_a, m_lru_b_a, v_lru_b_a),
                  mat(R["dwx"][0], lru_w_x, m_lru_w_x, v_lru_w_x), mat(R["dbx"][0], lru_b_x, m_lru_b_x, v_lru_b_x),
                  mat(R["dpw"][0], pool_w, m_pool_w, v_pool_w)]
    conv_wmv = (conv_w[0], m_conv_w[0], v_conv_w[0])
    small_out, loss_row = _adam_small(sec_idx.reshape(1), gsmall, g0_parts, norm_wmv, vec_params, mat_params, conv_wmv)
    loss = loss_row[0, 0]

    res = {}
    names_small = ["norm_g", "conv_b", "lru_lambda", "pool_scale", "ple_norm_g", "final_g",
                   "lru_w_a", "lru_b_a", "lru_w_x", "lru_b_x", "pool_w", "conv_w"]
    shapes = {"norm_g": norm_g, "conv_b": conv_b, "lru_lambda": lru_lambda, "pool_scale": pool_scale,
              "ple_norm_g": ple_norm_g, "final_g": final_g, "lru_w_a": lru_w_a, "lru_b_a": lru_b_a, "lru_w_x": lru_w_x,
              "lru_b_x": lru_b_x, "pool_w": pool_w, "conv_w": conv_w}
    for name, quad in zip(names_small, small_out):
        res[name] = [a.reshape(shapes[name].shape) for a in quad]
    names_big = ["w_in", "w_proj_lru", "w_proj_pool", "w_out", "w_ple_gate", "w_ple_proj"]
    for name, quad, (w, _, _) in zip(names_big, big, wmv):
        res[name] = [a.reshape((1,) + w.shape) for a in quad]
    order = ["norm_g", "w_in", "conv_w", "conv_b", "lru_w_a", "lru_b_a", "lru_w_x", "lru_b_x", "lru_lambda", "pool_w",
             "pool_scale", "w_proj_lru", "w_proj_pool", "w_out", "ple_norm_g", "w_ple_gate", "w_ple_proj", "final_g"]
    out = [loss, grad_x.reshape(nb, s, d)]
    for kind in range(4):
        out += [res[name][kind] for name in order]
    return tuple(out)
```

```python
import jax
import jax.numpy as jnp
from jax import lax
from jax.experimental import pallas as pl
from jax.experimental.pallas import tpu as pltpu

F32 = jnp.float32
_MM = jnp.bfloat16
_WIRE = jnp.bfloat16

EPS = 1e-6
LRU_C = 8.0
MULT_SQ_FLOOR = 1e-30
POOL_WINDOWS = (2, 4, 8, 16)
N_HEADS = 8
HEAD = 128
HALO = 16
SUB = 8

ADAM_LR = 0.001
ADAM_B1 = 0.9
ADAM_B2 = 0.999
ADAM_EPS = 1e-08
ADAM_WD = 0.01
ADAM_STEP = 10

N_DEV = 8
MESH = pl.DeviceIdType.MESH
VMEM_LIMIT = 60 * 1024 * 1024
TILE_M = 256
N_SEQ_SAVED = 10

_SEQ_ROWS = {"dwa": (0, 1024), "dwx": (1024, 1024), "dpw": (2048, 512), "dba": (2560, 8), "dbx": (2568, 8),
             "dcb": (2576, 8), "dlam": (2584, 8), "dcw": (2592, 32), "dps": (2624, 8)}
SEQ_PACK_ROWS = 2632
MID_PACK_ROWS = 24
MID_LOSS_ROW = 16
INP_PACK_ROWS = 8
SMALL_ROWS = 3072
SMALL_SEC = SMALL_ROWS // N_DEV
_OFF_MID = SEQ_PACK_ROWS


def _dot(a, b):
    return jnp.dot(a, b, preferred_element_type=F32)


def _dot_nt(a, b):
    return lax.dot_general(a, b, (((1,), (1,)), ((), ())), preferred_element_type=F32)


def _dot_tn(a, b):
    return lax.dot_general(a, b, (((0,), (0,)), ((), ())), preferred_element_type=F32)


def _sigmoid(v):
    return 0.5 * jnp.tanh(0.5 * v) + 0.5


def _lru_mult(log_a, a):
    m2 = jnp.maximum(-jnp.tanh(log_a) * (1.0 + a * a), MULT_SQ_FLOOR)
    inv = lax.rsqrt(m2)
    return m2 * inv, inv


def _rows_back(v, n):
    return pltpu.roll(v, n, 0) if n else v


def _rows_ahead(v, n):
    return pltpu.roll(v, v.shape[0] - n, 0) if n else v


def _softplus_neg(lam):
    e = jnp.exp(-jnp.abs(lam))
    w = 1.0 + e
    l1p = jnp.where(w == 1.0, e, jnp.log(w) * (e / (w - 1.0)))
    return jnp.maximum(-lam, 0.0) + l1p


def _rsqrt_mean_sq(v):
    return lax.rsqrt(jnp.mean(v * v, axis=-1, keepdims=True) + EPS)


def _colsum(v):
    return jnp.sum(v, axis=0, keepdims=True)


def _const_spec(shape):
    nd = len(shape)
    return pl.BlockSpec(shape, lambda *_: (0,) * nd, pipeline_mode=pl.Buffered(1))


HBM_SPEC = pl.BlockSpec(memory_space=pl.ANY)


def _my_pos():
    return lax.axis_index("x"), lax.axis_index("y"), lax.axis_index("c")


def _other_chips(x, y):
    return [(1 - x, y), (x, 1 - y), (1 - x, 1 - y)]


def _gather_sems(nw):
    return [pltpu.SemaphoreType.DMA((nw, 7)), pltpu.SemaphoreType.DMA((nw, 7)), pltpu.SemaphoreType.DMA((nw,))]


def _gather_phases(srcs, outs, slicers, send_sems, recv_sems, local_sems, relay_diagonal=False):
    n_direct = 2 if relay_diagonal else 3
    nw = len(srcs)
    x, y, c = _my_pos()
    me, sibling = (x, y, c), (x, y, 1 - c)
    chips = _other_chips(x, y)

    def part(w, pos):
        return slicers[w](outs[w], 4 * pos[0] + 2 * pos[1] + pos[2])

    def copy(w, k, block, to, src=None):
        return pltpu.make_async_remote_copy(
            src_ref=part(w, block) if src is None else src, dst_ref=part(w, block),
            send_sem=send_sems.at[w, k], recv_sem=recv_sems.at[w, k], device_id=to, device_id_type=MESH)

    def mine():
        return [pltpu.make_async_copy(srcs[w], part(w, me), local_sems.at[w]) for w in range(nw)]

    def first():
        out = []
        for w in range(nw):
            out.append(copy(w, 0, me, sibling, src=srcs[w]))
            out += [copy(w, 1 + j, me, (*chips[j], c), src=srcs[w]) for j in range(n_direct)]
        return out

    def relay(w, j):
        return copy(w, 3, (*chips[j], c), (*chips[1 - j], c))

    def relays():
        return [relay(w, 0) for w in range(nw)] if relay_diagonal else []

    def passed():
        return [copy(w, 4 + j, (*chip, c), sibling) for j, chip in enumerate(chips) for w in range(nw)]

    def start():
        for cp in mine() + first():
            cp.start()

    def forward():
        for j, chip in enumerate(chips):
            for w in range(nw):
                copy(w, 1 + j, (*chip, c), me).wait_recv()
                copy(w, 4 + j, (*chip, c), sibling).start()
            if relay_diagonal and j < 2:
                @pl.when(c == j)
                def _():
                    for w in range(nw):
                        relay(w, j).start()

    def finish():
        for w in range(nw):
            copy(w, 0, sibling, me).wait_recv()
            for j, chip in enumerate(chips):
                copy(w, 4 + j, (*chip, 1 - c), me).wait_recv()
        for cp in first() + relays() + passed():
            cp.wait_send()
        for cp in mine():
            cp.wait()

    return start, forward, finish


def _sibling_copies(srcs, outs, send_sems, recv_sems):
    x, y, c = _my_pos()
    return [pltpu.make_async_remote_copy(
        src_ref=srcs[w].at[2 * q + (1 - c)], dst_ref=outs[w].at[q], send_sem=send_sems.at[w, q],
        recv_sem=recv_sems.at[w, q], device_id=(x, y, 1 - c), device_id_type=MESH)
        for w in range(len(srcs)) for q in range(4)]


def _chip_copies(srcs, outs, send_sems, recv_sems, which=None):
    x, y, c = _my_pos()
    return [pltpu.make_async_remote_copy(
        src_ref=srcs[w].at[2 * px + py], dst_ref=outs[w].at[k], send_sem=send_sems.at[w, k],
        recv_sem=recv_sems.at[w, k], device_id=(px, py, c), device_id_type=MESH)
        for w in (range(len(srcs)) if which is None else which) for k, (px, py) in enumerate(_other_chips(x, y))]


def _lru_gates(xc, h, wa_ref, ba_ref, wx_ref, bx_ref, neg_c_sp):
    xc16 = xc.astype(_MM)
    r = _sigmoid(_dot(xc16, wa_ref[h]) + ba_ref[pl.ds(h, 1), :])
    i = _sigmoid(_dot(xc16, wx_ref[h]) + bx_ref[pl.ds(h, 1), :])
    return r, i, neg_c_sp * r


def _conv_head(xa_ext, cw_ref, cb_ref, cols, tm):
    ext = xa_ext[:, cols]
    xc = cb_ref[:, cols] + cw_ref[pl.ds(3, 1), cols] * ext[HALO:HALO + tm]
    for k in range(3):
        xc = xc + cw_ref[pl.ds(k, 1), cols] * _rows_back(ext, 3 - k)[HALO:HALO + tm]
    return xc


def _window_sum(ext, k, shift):
    n = 1
    while n < k:
        ext = ext + shift(ext, n)
        n *= 2
    return ext


def _pool_diff(xb_ext, g, k, pos, tm):
    cols = slice(g * HEAD, (g + 1) * HEAD)
    ext = xb_ext[:, cols]
    ws = _window_sum(ext, k, _rows_back)[HALO:HALO + tm]
    inv_cnt = 1.0 / jnp.minimum(pos + 1, k).astype(F32)
    return ws * inv_cnt - ext[HALO:HALO + tm], inv_cnt


def _staged_sources(shards, as_operand):
    in_specs = [_const_spec(a.shape) if c else HBM_SPEC for a, c in zip(shards, as_operand)]
    stages = [pltpu.VMEM(a.shape, _MM) for a, c in zip(shards, as_operand) if c]
    dtypes = [_MM if c else a.dtype for a, c in zip(shards, as_operand)]
    return in_specs, stages, dtypes


def _gather_sources(srcs, stages, as_operand):
    stages = iter(stages)
    return [next(stages) if c else src for src, c in zip(srcs, as_operand)]


def _fill_stages(srcs, staged, as_operand):
    for src, dst, c in zip(srcs, staged, as_operand):
        if c:
            dst[...] = src[...].astype(dst.dtype)


def _pre(x, g0, shards, as_operand, shapes, slicers, tm):
    nb, s, d = x.shape
    nt = s // tm
    nl = len(shards)
    shard_specs, stage_shapes, wire = _staged_sources(shards, as_operand)

    def body(x_ref, g0_ref, *rest):
        src, rest = rest[:nl], rest[nl:]
        h_ref, ht_ref = rest[:2]
        out, rest = rest[2:2 + nl], rest[2 + nl:]
        (send_sems, recv_sems, local_sems), stages = rest[:3], rest[3:]
        step_no = pl.program_id(0) * nt + pl.program_id(1)
        staged = _gather_sources(src, stages, as_operand)
        g_start, g_forward, g_finish = _gather_phases(staged, out, slicers, send_sems, recv_sems, local_sems,
                                                      relay_diagonal=True)

        @pl.when(step_no == 0)
        def _():
            _fill_stages(src, staged, as_operand)
            g_start()

        xv = x_ref[...]
        h16 = ((xv * _rsqrt_mean_sq(xv)) * g0_ref[...]).astype(_MM)
        h_ref[...] = h16
        ht_ref[...] = h16.T

        @pl.when(step_no == nb * nt - 1)
        def _():
            g_forward()
            g_finish()

    tile = pl.BlockSpec((None, tm, d), lambda b, j: (b, j, 0))
    return pl.pallas_call(
        body, name="pre", grid=(nb, nt),
        in_specs=[tile, _const_spec(g0.shape)] + shard_specs,
        out_specs=(tile, pl.BlockSpec((None, d, tm), lambda b, j: (b, 0, j))) + (HBM_SPEC,) * nl,
        out_shape=(jax.ShapeDtypeStruct((nb, s, d), _MM), jax.ShapeDtypeStruct((nb, d, s), _MM))
        + tuple(jax.ShapeDtypeStruct(shp, dt) for shp, dt in zip(shapes, wire)),
        scratch_shapes=_gather_sems(nl) + stage_shapes,
        compiler_params=pltpu.CompilerParams(dimension_semantics=("arbitrary", "arbitrary")),
    )(x, g0, *shards)


def _fwd_seq(h, win, cw, cb, wa, ba, wx, bx, lam, pw, ps, late_shards, late_shapes, late_slicers, tm):
    nb, s, d = h.shape
    lw = cw.shape[1]
    pwid = ps.shape[1]
    nc = 2 * lw + 2 * pwid
    nt = s // tm
    nl = len(late_shards)
    n_steps = nb * nt
    as_operand = [True] * nl
    shard_specs, stage_shapes, wire = _staged_sources(late_shards, as_operand)

    def body(h_ref, win_ref, cw_ref, cb_ref, wa_ref, ba_ref, wx_ref, bx_ref, lam_ref, pw_ref, ps_ref, *rest):
        late_src, rest = rest[:nl], rest[nl:]
        z_ref, hl_ref, ya_ref, yb_ref, xc_ref, r_ref, i_ref, a_buf, diff_ref, yp_ref = rest[:N_SEQ_SAVED]
        late_out, rest = rest[N_SEQ_SAVED:N_SEQ_SAVED + nl], rest[N_SEQ_SAVED + nl:]
        (xa_ext, xb_ext, carry, send_sems, recv_sems, local_sems), stages = rest[:6], rest[6:]
        j = pl.program_id(1)
        step_no = pl.program_id(0) * nt + j
        staged = _gather_sources(late_src, stages, as_operand)
        g_start, g_forward, g_finish = _gather_phases(staged, late_out, late_slicers, send_sems, recv_sems, local_sems)

        @pl.when(step_no == 0)
        def _():
            _fill_stages(late_src, staged, as_operand)
            g_start()

        pl.when(step_no == n_steps // 2)(g_forward)

        @pl.when(j == 0)
        def _():
            xa_ext[pl.ds(0, HALO), :] = jnp.zeros((HALO, lw), F32)
            xb_ext[pl.ds(0, HALO), :] = jnp.zeros((HALO, pwid), F32)
            carry[...] = jnp.zeros_like(carry)

        h16 = h_ref[...]
        xa = _dot(h16, win_ref[:, 0:lw])
        z_ref[:, 0:lw] = xa
        xa_ext[pl.ds(HALO, tm), :] = xa
        for c0 in range(lw, nc, lw):
            z_ref[:, c0:c0 + lw] = _dot(h16, win_ref[:, c0:c0 + lw])

        for h in range(N_HEADS):
            cols = slice(h * HEAD, (h + 1) * HEAD)
            xc = _conv_head(xa_ext, cw_ref, cb_ref, cols, tm)
            neg_c_sp = -LRU_C * _softplus_neg(lam_ref[:, cols])
            r, i, log_a = _lru_gates(xc, h, wa_ref, ba_ref, wx_ref, bx_ref, neg_c_sp)
            a = jnp.exp(log_a)
            xc_ref[:, cols] = xc
            r_ref[:, cols] = r
            i_ref[:, cols] = i
            a_buf[:, cols] = a
            hl_ref[:, cols] = _lru_mult(log_a, a)[0] * (i * xc)

        rows = lax.broadcasted_iota(jnp.int32, (SUB, lw), 0)

        def step(c, car):
            i0 = pl.multiple_of(c * SUB, SUB)
            av = a_buf[pl.ds(i0, SUB), :]
            bv = hl_ref[pl.ds(i0, SUB), :]
            for sh in (1, 2, 4):
                m = rows >= sh
                a_sh = jnp.where(m, pltpu.roll(av, sh, 0), 1.0)
                b_sh = jnp.where(m, pltpu.roll(bv, sh, 0), 0.0)
                bv = av * b_sh + bv
                av = av * a_sh
            hv = av * car + bv
            hl_ref[pl.ds(i0, SUB), :] = hv
            return jnp.broadcast_to(hv[SUB - 1:SUB, :], (SUB, lw))

        carry[...] = lax.fori_loop(0, tm // SUB, step, carry[...])

        ga = z_ref[:, lw:2 * lw]
        ya_ref[...] = (hl_ref[...] * (ga * _sigmoid(ga))).astype(ya_ref.dtype)

        xb_ext[pl.ds(HALO, tm), :] = z_ref[:, 2 * lw:2 * lw + pwid]
        pos = j * tm + lax.broadcasted_iota(jnp.int32, (tm, HEAD), 0)
        for g, k in enumerate(POOL_WINDOWS):
            cols = slice(g * HEAD, (g + 1) * HEAD)
            diff16 = _pool_diff(xb_ext, g, k, pos, tm)[0].astype(_MM)
            yp = _dot(diff16, pw_ref[g])
            diff_ref[:, cols] = diff16
            yp_ref[:, cols] = yp
            gb = z_ref[:, 2 * lw + pwid + g * HEAD:2 * lw + pwid + (g + 1) * HEAD]
            yb_ref[:, cols] = ((yp * ps_ref[:, cols]) * (gb * _sigmoid(gb))).astype(yb_ref.dtype)

        xa_ext[pl.ds(0, HALO), :] = xa_ext[pl.ds(tm, HALO), :]
        xb_ext[pl.ds(0, HALO), :] = xb_ext[pl.ds(tm, HALO), :]
        pl.when(step_no == n_steps - 1)(g_finish)

    tile = lambda w: pl.BlockSpec((None, tm, w), lambda b, j: (b, j, 0))
    return pl.pallas_call(
        body, name="fwd_seq", grid=(nb, nt),
        in_specs=[tile(d), _const_spec((d, nc)), _const_spec(cw.shape), _const_spec(cb.shape),
                  _const_spec(wa.shape), _const_spec(ba.shape), _const_spec(wx.shape), _const_spec(bx.shape),
                  _const_spec(lam.shape), _const_spec(pw.shape), _const_spec(ps.shape)] + shard_specs,
        out_specs=(tile(nc), tile(lw), tile(lw), tile(pwid), tile(lw), tile(lw), tile(lw), tile(lw), tile(pwid),
                   tile(pwid)) + (HBM_SPEC,) * nl,
        out_shape=(jax.ShapeDtypeStruct((nb, s, nc), F32), jax.ShapeDtypeStruct((nb, s, lw), F32),
                   jax.ShapeDtypeStruct((nb, s, lw), _MM), jax.ShapeDtypeStruct((nb, s, pwid), _MM))
        + (jax.ShapeDtypeStruct((nb, s, lw), F32),) * 4
        + (jax.ShapeDtypeStruct((nb, s, pwid), _MM), jax.ShapeDtypeStruct((nb, s, pwid), F32))
        + tuple(jax.ShapeDtypeStruct(shp, dt) for shp, dt in zip(late_shapes, wire)),
        scratch_shapes=[pltpu.VMEM((tm + HALO, lw), F32), pltpu.VMEM((tm + HALO, pwid), F32),
                        pltpu.VMEM((SUB, lw), F32)] + _gather_sems(nl) + stage_shapes,
        compiler_params=pltpu.CompilerParams(dimension_semantics=("arbitrary", "arbitrary"), vmem_limit_bytes=VMEM_LIMIT),
    )(h, win, cw, cb, wa, ba, wx, bx, lam, pw, ps, *late_shards)


def _store_chunks(ref, row0, vec):
    for h in range(vec.shape[1] // HEAD):
        ref[pl.ds(row0 + h, 1), :] += vec[:, h * HEAD:(h + 1) * HEAD]


def _mid(ya, yb, h, win, x, p, tgt, wpl, wpp, wout, wpg, wpe, g1, gf, tm):
    t, d = x.shape
    lw = ya.shape[1]
    pwid = yb.shape[1]
    pdim = p.shape[1]
    n = t // tm
    sec_pp = wpp.shape[1] // N_DEV
    sec_pe = wpe.shape[1] // N_DEV
    sec_rows = wpl.shape[0] // N_DEV
    grad_secs = [(sec_rows, d), (wpp.shape[0], sec_pp), (sec_rows, d), (sec_rows, d), (wpe.shape[0], sec_pe)]
    n_win_blocks = win.shape[1] // d
    win_cols = lambda c: pl.BlockSpec((d, d), lambda i: (0, c), pipeline_mode=pl.Buffered(1))

    def body(ya_ref, yb_ref, h_ref, wma_ref, wmb_ref, x_ref, p_ref, tgt_ref, wpl_ref, wpp_ref, wout_ref, wpg_ref, wpe_ref,
             g1_ref, gf_ref,
             dya_ref, dyb_ref, dm_ref, dx1_ref, gpl_ref, gpp_ref, gout_ref, gpg_ref, gpe_ref, pack_ref,
             sib_pl, sib_pp, sib_out, sib_pg, sib_pe,
             acc_pl, acc_pp, acc_out, acc_pg, acc_pe, out_sems, sib_send, sib_recv):
        i = pl.program_id(0)

        @pl.when(i == 0)
        def _():
            for acc in (acc_pl, acc_pp, acc_out, acc_pg, acc_pe):
                acc[...] = jnp.zeros_like(acc)
            pack_ref[...] = jnp.zeros_like(pack_ref)

        ya16 = ya_ref[...]
        yb16 = yb_ref[...]
        br_a = _dot(ya16, wpl_ref[...])
        br_b = _dot(yb16, wpp_ref[...])
        h16 = h_ref[...]
        sa = _sigmoid(_dot(h16, wma_ref[...]))
        sb = _sigmoid(_dot(h16, wmb_ref[...]))
        mg16 = (sa * br_a + sb * br_b).astype(_MM)
        x1 = x_ref[...] + _dot(mg16, wout_ref[...])
        r1 = _rsqrt_mean_sq(x1)
        n1 = x1 * r1
        h116 = (n1 * g1_ref[...]).astype(_MM)
        gate = _sigmoid(_dot(h116, wpg_ref[...]))
        p16 = p_ref[...].astype(_MM)
        pe = _dot(p16, wpe_ref[...])
        x2 = x1 + gate * pe
        r2 = _rsqrt_mean_sq(x2)
        n2 = x2 * r2
        err = n2 * gf_ref[...] - tgt_ref[...]
        sq = jnp.sum(_colsum(err * err), axis=1, keepdims=True)
        pack_ref[pl.ds(MID_LOSS_ROW, 1), :] += jnp.broadcast_to(sq * (0.5 / d), (1, HEAD))

        dy = err * (1.0 / d)
        _store_chunks(pack_ref, SUB, _colsum(dy * n2))
        dn2 = dy * gf_ref[...]
        dx2 = r2 * (dn2 - n2 * jnp.mean(dn2 * n2, axis=-1, keepdims=True))
        dpe16 = (dx2 * gate).astype(_MM)
        dpg16 = ((dx2 * pe) * (gate * (1.0 - gate))).astype(_MM)
        acc_pe[...] += _dot_tn(p16, dpe16)
        acc_pg[...] += _dot_tn(h116, dpg16)
        dh1 = _dot_nt(dpg16, wpg_ref[...])
        _store_chunks(pack_ref, 0, _colsum(dh1 * n1))
        dn1 = dh1 * g1_ref[...]
        dx1 = dx2 + r1 * (dn1 - n1 * jnp.mean(dn1 * n1, axis=-1, keepdims=True))
        dx1_ref[...] = dx1
        dx116 = dx1.astype(_MM)
        acc_out[...] += _dot_tn(mg16, dx116)
        dmg = _dot_nt(dx116, wout_ref[...])
        da16 = (dmg * sa).astype(_MM)
        db16 = (dmg * sb).astype(_MM)
        dm_ref[:, 0:d] = ((dmg * br_a) * (sa * (1.0 - sa))).astype(dm_ref.dtype)
        dm_ref[:, d:2 * d] = ((dmg * br_b) * (sb * (1.0 - sb))).astype(dm_ref.dtype)
        acc_pl[...] += _dot_tn(ya16, da16)
        acc_pp[...] += _dot_tn(yb16, db16)
        dya_ref[...] = _dot_nt(da16, wpl_ref[...])
        dyb_ref[...] = _dot_nt(db16, wpp_ref[...])

        @pl.when(i == n - 1)
        def _():
            writes = []
            for k in range(N_DEV):
                rows = pl.ds(k * sec_rows, sec_rows)
                for acc, out in ((acc_pl, gpl_ref), (acc_out, gout_ref), (acc_pg, gpg_ref)):
                    writes.append((acc.at[rows, :], out.at[k]))
                writes.append((acc_pp.at[:, pl.ds(k * sec_pp, sec_pp)], gpp_ref.at[k]))
                writes.append((acc_pe.at[:, pl.ds(k * sec_pe, sec_pe)], gpe_ref.at[k]))
            writes = [pltpu.make_async_copy(src, dst, out_sems.at[n]) for n, (src, dst) in enumerate(writes)]
            for cp in writes:
                cp.start()
            for cp in writes:
                cp.wait()
            to_sibling = _sibling_copies([gpl_ref, gpp_ref, gout_ref, gpg_ref, gpe_ref],
                                         [sib_pl, sib_pp, sib_out, sib_pg, sib_pe], sib_send, sib_recv)
            for cp in to_sibling:
                cp.start()
            for cp in to_sibling:
                cp.wait()

    tile = lambda w: pl.BlockSpec((tm, w), lambda i: (i, 0))
    any_spec = pl.BlockSpec(memory_space=pl.ANY)
    full = lambda shape: pl.BlockSpec(shape, lambda i: (0,) * len(shape))
    return pl.pallas_call(
        body, name="mid", grid=(n,),
        in_specs=[tile(lw), tile(pwid), tile(d), win_cols(n_win_blocks - 2), win_cols(n_win_blocks - 1),
                  tile(d), tile(pdim), tile(d),
                  _const_spec(wpl.shape), _const_spec(wpp.shape), _const_spec(wout.shape), _const_spec(wpg.shape),
                  _const_spec(wpe.shape), _const_spec(g1.shape), _const_spec(gf.shape)],
        out_specs=(tile(lw), tile(pwid), tile(2 * d), tile(d), any_spec, any_spec, any_spec, any_spec, any_spec,
                   full((MID_PACK_ROWS, HEAD))) + (any_spec,) * 5,
        out_shape=(jax.ShapeDtypeStruct((t, lw), F32), jax.ShapeDtypeStruct((t, pwid), F32),
                   jax.ShapeDtypeStruct((t, 2 * d), _MM), jax.ShapeDtypeStruct((t, d), F32),
                   *(jax.ShapeDtypeStruct((N_DEV,) + sec, F32) for sec in grad_secs),
                   jax.ShapeDtypeStruct((MID_PACK_ROWS, HEAD), F32),
                   *(jax.ShapeDtypeStruct((N_DEV // 2,) + sec, F32) for sec in grad_secs)),
        scratch_shapes=[pltpu.VMEM(wpl.shape, F32), pltpu.VMEM(wpp.shape, F32), pltpu.VMEM(wout.shape, F32),
                        pltpu.VMEM(wpg.shape, F32), pltpu.VMEM(wpe.shape, F32),
                        pltpu.SemaphoreType.DMA((5 * N_DEV,)), pltpu.SemaphoreType.DMA((5, N_DEV // 2)),
                        pltpu.SemaphoreType.DMA((5, N_DEV // 2))],
        compiler_params=pltpu.CompilerParams(dimension_semantics=("arbitrary",), vmem_limit_bytes=VMEM_LIMIT),
    )(ya, yb, h, win, win, x, p, tgt, wpl, wpp, wout, wpg, wpe, g1, gf)


def _bwd_seq(z, hl, saved, dya, dyb, dm, cw, wa, wx, lam, pw, ps, pack_mid, sib_grads, tm):
    nb, s, zw = z.shape
    lw = cw.shape[1]
    pwid = ps.shape[1]
    nt = s // tm
    nc = zw + dm.shape[2]
    R = _SEQ_ROWS
    ns = len(sib_grads)

    def body(zq_ref, hl_ref, hlh_ref, xc_buf, r_buf, i_buf, a_ref, diff_ref, yp_ref, dya_ref, dyb_ref, dm_ref,
             cw_ref, wa_ref, wx_ref, lam_ref, pw_ref, ps_ref, pack_mid_ref, *rest):
        sib_src, rest = rest[:ns], rest[ns:]
        dz_ref, pack_ref = rest[:2]
        sib_out, rest = rest[2:2 + ns], rest[2 + ns:]
        a_ext, an_buf, hl_ext, dh_buf, dxc_ext, q_ext, carry, send_sems, recv_sems = rest
        b = pl.program_id(0)
        j = pl.program_id(1)
        jr = nt - 1 - j
        has_prev = jr > 0

        @pl.when((b == 0) & (j == 0))
        def _():
            for cp in _chip_copies(sib_src, sib_out, send_sems, recv_sems):
                cp.start()

        dz_ref[:, zw:nc] = dm_ref[...]

        @pl.when((b == 0) & (j == 0))
        def _():
            pack_ref[...] = jnp.zeros_like(pack_ref)

        @pl.when(j == 0)
        def _():
            a_ext[pl.ds(tm, SUB), :] = jnp.zeros((SUB, lw), F32)
            dxc_ext[pl.ds(tm, HALO), :] = jnp.zeros((HALO, lw), F32)
            q_ext[pl.ds(tm, HALO), :] = jnp.zeros((HALO, pwid), F32)
            carry[...] = jnp.zeros_like(carry)

        hl_ext[pl.ds(0, SUB), :] = jnp.where(has_prev, hlh_ref[...], 0.0)
        hl_ext[pl.ds(SUB, tm), :] = hl_ref[...]
        a_ext[pl.ds(0, tm), :] = a_ref[...]
        for h in range(N_HEADS):
            cols = slice(h * HEAD, (h + 1) * HEAD)
            ga = zq_ref[:, lw + h * HEAD:lw + (h + 1) * HEAD]
            sg = _sigmoid(ga)
            dyav = dya_ref[:, cols]
            dh_buf[:, cols] = dyav * (ga * sg)
            dga = (dyav * hl_ext[pl.ds(SUB, tm), cols]) * (sg * (1.0 + ga * (1.0 - sg)))
            dz_ref[:, lw + h * HEAD:lw + (h + 1) * HEAD] = dga.astype(dz_ref.dtype)
        an_buf[...] = _rows_ahead(a_ext[...], 1)[0:tm]

        rows = lax.broadcasted_iota(jnp.int32, (SUB, lw), 0)
        nch = tm // SUB

        def step(c, car):
            i0 = pl.multiple_of((nch - 1 - c) * SUB, SUB)
            cv = an_buf[pl.ds(i0, SUB), :]
            bv = dh_buf[pl.ds(i0, SUB), :]
            for sh in (1, 2, 4):
                m = rows < SUB - sh
                c_sh = jnp.where(m, pltpu.roll(cv, SUB - sh, 0), 1.0)
                b_sh = jnp.where(m, pltpu.roll(bv, SUB - sh, 0), 0.0)
                bv = cv * b_sh + bv
                cv = cv * c_sh
            hv = cv * car + bv
            dh_buf[pl.ds(i0, SUB), :] = hv
            return jnp.broadcast_to(hv[0:1, :], (SUB, lw))

        carry[...] = lax.fori_loop(0, nch, step, carry[...])

        for h in range(N_HEADS):
            cols = slice(h * HEAD, (h + 1) * HEAD)
            lam_h = lam_ref[:, cols]
            neg_c_sp = -LRU_C * _softplus_neg(lam_h)
            xc = xc_buf[:, cols]
            r = r_buf[:, cols]
            i = i_buf[:, cols]
            a = a_ref[:, cols]
            mult, inv_mult = _lru_mult(neg_c_sp * r, a)
            dh = dh_buf[:, cols]
            dhx = dh * xc
            di = dhx * mult
            h_prev = _rows_back(hl_ext[:, cols], 1)[SUB:SUB + tm]
            dlog_a = (dh * h_prev) * a - (dhx * i) * ((a * a) * inv_mult)
            pack_ref[pl.ds(R["dlam"][0] + h, 1), :] += _colsum(dlog_a * r) * (LRU_C * _sigmoid(-lam_h))
            dpa = (dlog_a * neg_c_sp) * (r * (1.0 - r))
            dpx = di * (i * (1.0 - i))
            dpa16 = dpa.astype(_MM)
            dpx16 = dpx.astype(_MM)
            xc16 = xc.astype(_MM)
            dxc = dh * (mult * i) + _dot_nt(dpa16, wa_ref[h]) + _dot_nt(dpx16, wx_ref[h])
            pack_ref[pl.ds(R["dwa"][0] + h * HEAD, HEAD), :] += _dot_tn(xc16, dpa16)
            pack_ref[pl.ds(R["dwx"][0] + h * HEAD, HEAD), :] += _dot_tn(xc16, dpx16)
            pack_ref[pl.ds(R["dba"][0] + h, 1), :] += _colsum(dpa)
            pack_ref[pl.ds(R["dbx"][0] + h, 1), :] += _colsum(dpx)
            pack_ref[pl.ds(R["dcb"][0] + h, 1), :] += _colsum(dxc)
            dxc_ext[pl.ds(0, tm), cols] = dxc

        for h in range(N_HEADS):
            cols = slice(h * HEAD, (h + 1) * HEAD)
            dxc_all = dxc_ext[:, cols]
            xa = zq_ref[:, cols]
            dxa = None
            for k in range(4):
                dxc_k = _rows_ahead(dxc_all, 3 - k)[0:tm]
                pack_ref[pl.ds(R["dcw"][0] + SUB * k + h, 1), :] += _colsum(dxc_k * xa)
                term = cw_ref[pl.ds(k, 1), cols] * dxc_k
                dxa = term if dxa is None else dxa + term
            dz_ref[:, cols] = dxa.astype(dz_ref.dtype)

        pos = jr * tm + lax.broadcasted_iota(jnp.int32, (tm, HEAD), 0)
        for g, k in enumerate(POOL_WINDOWS):
            cols = slice(g * HEAD, (g + 1) * HEAD)
            inv_cnt = 1.0 / jnp.minimum(pos + 1, k).astype(F32)
            diff16 = diff_ref[:, cols]
            yp = yp_ref[:, cols]
            sc = ps_ref[:, cols]
            gb = zq_ref[:, 2 * lw + pwid + g * HEAD:2 * lw + pwid + (g + 1) * HEAD]
            sgb = _sigmoid(gb)
            dybv = dyb_ref[:, cols]
            dy_pool = dybv * (gb * sgb)
            dgb = (dybv * (yp * sc)) * (sgb * (1.0 + gb * (1.0 - sgb)))
            pack_ref[pl.ds(R["dps"][0] + g, 1), :] += _colsum(dy_pool * yp)
            dyp16 = (dy_pool * sc).astype(_MM)
            pack_ref[pl.ds(R["dpw"][0] + g * HEAD, HEAD), :] += _dot_tn(diff16, dyp16)
            ddiff = _dot_nt(dyp16, pw_ref[g])
            q_ext[pl.ds(0, tm), cols] = ddiff * inv_cnt
            dxb = _window_sum(q_ext[:, cols], k, _rows_ahead)[0:tm] - ddiff
            dz_ref[:, 2 * lw + g * HEAD:2 * lw + (g + 1) * HEAD] = dxb.astype(dz_ref.dtype)
            dz_ref[:, 2 * lw + pwid + g * HEAD:2 * lw + pwid + (g + 1) * HEAD] = dgb.astype(dz_ref.dtype)

        a_ext[pl.ds(tm, SUB), :] = a_ext[pl.ds(0, SUB), :]
        dxc_ext[pl.ds(tm, HALO), :] = dxc_ext[pl.ds(0, HALO), :]
        q_ext[pl.ds(tm, HALO), :] = q_ext[pl.ds(0, HALO), :]

        @pl.when((b == nb - 1) & (j == nt - 1))
        def _():
            pack_ref[pl.ds(_OFF_MID, MID_PACK_ROWS), :] = pack_mid_ref[...]
            for cp in _chip_copies(sib_src, sib_out, send_sems, recv_sems):
                cp.wait()

    rev = lambda w: pl.BlockSpec((None, tm, w), lambda b, j: (b, nt - 1 - j, 0))
    prev_rows = lambda rows, w: pl.BlockSpec(
        (None, rows, w), lambda b, j: (b, jnp.maximum((nt - 1 - j) * (tm // rows) - 1, 0), 0))
    return pl.pallas_call(
        body, name="bwd_seq", grid=(nb, nt),
        in_specs=[rev(zw), rev(lw), prev_rows(SUB, lw), rev(lw), rev(lw), rev(lw), rev(lw), rev(pwid), rev(pwid),
                  rev(lw), rev(pwid), rev(nc - zw),
                  _const_spec(cw.shape), _const_spec(wa.shape), _const_spec(wx.shape), _const_spec(lam.shape),
                  _const_spec(pw.shape), _const_spec(ps.shape), _const_spec(pack_mid.shape)] + [HBM_SPEC] * ns,
        out_specs=(rev(nc), pl.BlockSpec((SMALL_ROWS, HEAD), lambda b, j: (0, 0))) + (HBM_SPEC,) * ns,
        out_shape=(jax.ShapeDtypeStruct((nb, s, nc), _MM), jax.ShapeDtypeStruct((SMALL_ROWS, HEAD), F32))
        + tuple(jax.ShapeDtypeStruct((3,) + g.shape[1:], g.dtype) for g in sib_grads),
        scratch_shapes=[pltpu.VMEM((tm + SUB, lw), F32), pltpu.VMEM((tm, lw), F32), pltpu.VMEM((tm + SUB, lw), F32),
                        pltpu.VMEM((tm, lw), F32), pltpu.VMEM((tm + HALO, lw), F32),
                        pltpu.VMEM((tm + HALO, pwid), F32), pltpu.VMEM((SUB, lw), F32),
                        pltpu.SemaphoreType.DMA((ns, 3)), pltpu.SemaphoreType.DMA((ns, 3))],
        compiler_params=pltpu.CompilerParams(dimension_semantics=("arbitrary", "arbitrary"), vmem_limit_bytes=VMEM_LIMIT),
    )(z, hl, hl, *saved, dya, dyb, dm, cw, wa, wx, lam, pw, ps, pack_mid, *sib_grads)


def _bwd_win(ht, dz, small):
    nb, d, s = ht.shape
    nc = dz.shape[2]
    sec = nc // N_DEV
    n_chips = N_DEV // 2

    def body(ht_ref, dz_ref, small_src, gwin_ref, recv_ref, small_out, small_chips_out, gsmall_ref,
             acc, own4, sib4, csum, got3, total, local_sems, sib_send, sib_recv, small_send, small_recv,
             sc_send, sc_recv, ld_sems, g_send, g_recv, g_local):
        q = pl.program_id(0)
        b = pl.program_id(1)
        slot = q % 2
        x, y, c = _my_pos()

        def half(sl, core):
            return acc.at[sl, :, pl.ds(pl.multiple_of(core * sec, HEAD), sec)]

        def local_copy(qq, sl):
            return pltpu.make_async_copy(half(sl, c), gwin_ref.at[2 * qq + c], local_sems.at[sl])

        def sib_copy(qq, sl):
            return pltpu.make_async_remote_copy(
                src_ref=half(sl, 1 - c), dst_ref=recv_ref.at[qq], send_sem=sib_send.at[qq],
                recv_sem=sib_recv.at[qq], device_id=(x, y, 1 - c), device_id_type=MESH)

        step_no = q * nb + b
        n_steps = n_chips * nb
        sum_step = min(3, n_steps - 1)
        fwd_step = max(sum_step, min(6, n_steps - 1))
        small_sib = lambda: _sibling_copies([small_src], [small_out], small_send, small_recv)
        small_chip = lambda: _chip_copies([csum], [small_chips_out], sc_send, sc_recv)
        g_start, g_forward, g_finish = _gather_phases([total], [gsmall_ref], [_row_slicer(small.shape[1])],
                                                      g_send, g_recv, g_local)

        def load(srcs, dst):
            cps = [pltpu.make_async_copy(src, dst.at[k], ld_sems.at[k]) for k, src in enumerate(srcs)]
            for cp in cps:
                cp.start()
            for cp in cps:
                cp.wait()

        @pl.when(step_no == 0)
        def _():
            for cp in small_sib():
                cp.start()

        @pl.when(step_no == 1)
        def _():
            for cp in small_sib():
                cp.wait_recv()
            load([small_src.at[2 * qq + c] for qq in range(n_chips)], own4)
            load([small_out.at[qq] for qq in range(n_chips)], sib4)
            csum[...] = own4[...] + sib4[...]
            for cp in small_chip():
                cp.start()

        @pl.when(step_no == sum_step)
        def _():
            for cp in small_chip():
                cp.wait_recv()
            load([small_chips_out.at[k] for k in range(3)], got3)
            total[...] = ((csum[2 * x + y] + got3[0]) + got3[1]) + got3[2]
            g_start()

        pl.when(step_no == fwd_step)(g_forward)

        @pl.when((q >= 2) & (b == 0))
        def _():
            local_copy(q - 2, slot).wait()
            sib_copy(q - 2, slot).wait_send()

        part = _dot(ht_ref[b], dz_ref[...])

        @pl.when(b == 0)
        def _():
            acc[slot] = part

        @pl.when(b != 0)
        def _():
            acc[slot] += part

        @pl.when(b == nb - 1)
        def _():
            local_copy(q, slot).start()
            sib_copy(q, slot).start()

        @pl.when((q == n_chips - 1) & (b == nb - 1))
        def _():
            for qq in (n_chips - 2, n_chips - 1):
                local_copy(qq, qq % 2).wait()
                sib_copy(qq, qq % 2).wait_send()
            for qq in range(n_chips):
                sib_copy(qq, 0).wait_recv()
            g_finish()
            for cp in small_sib() + small_chip():
                cp.wait_send()

    return pl.pallas_call(
        body, name="bwd_win", grid=(n_chips, nb),
        in_specs=[_const_spec(ht.shape), pl.BlockSpec((None, s, 2 * sec), lambda q, b: (b, 0, q))]
        + [HBM_SPEC],
        out_specs=(HBM_SPEC,) * 5,
        out_shape=(jax.ShapeDtypeStruct((N_DEV, d, sec), F32), jax.ShapeDtypeStruct((n_chips, d, sec), F32))
        + (jax.ShapeDtypeStruct((n_chips,) + small.shape[1:], small.dtype),
           jax.ShapeDtypeStruct((3,) + small.shape[1:], small.dtype),
           jax.ShapeDtypeStruct((N_DEV * small.shape[1], small.shape[2]), small.dtype)),
        scratch_shapes=[pltpu.VMEM((2, d, 2 * sec), F32)]
        + [pltpu.VMEM((n_chips,) + small.shape[1:], F32)] * 3
        + [pltpu.VMEM((3,) + small.shape[1:], F32), pltpu.VMEM(small.shape[1:], F32), pltpu.SemaphoreType.DMA((2,)),
           pltpu.SemaphoreType.DMA((n_chips,)), pltpu.SemaphoreType.DMA((n_chips,)),
           pltpu.SemaphoreType.DMA((1, 4)), pltpu.SemaphoreType.DMA((1, 4)),
           pltpu.SemaphoreType.DMA((1, 3)), pltpu.SemaphoreType.DMA((1, 3)), pltpu.SemaphoreType.DMA((n_chips,))]
        + _gather_sems(1),
        compiler_params=pltpu.CompilerParams(dimension_semantics=("arbitrary", "arbitrary"), vmem_limit_bytes=VMEM_LIMIT),
    )(ht, dz, small)


def _bwd_dx(dz, x, dx1, win, g0, chip_srcs, tm):
    t, d = x.shape
    nc = win.shape[1]
    n = t // tm
    nm = len(chip_srcs)

    def body(dz_ref, x_ref, dx1_ref, win_ref, g0_ref, *rest):
        chip_src = rest[:nm]
        gx_ref = rest[nm]
        chip_out = rest[nm + 1:2 * nm + 1]
        parts_ref = rest[2 * nm + 1]
        pack, chip_send, chip_recv, p_send, p_recv, p_local = rest[2 * nm + 2:]
        i = pl.program_id(0)
        x_, y_, c_ = _my_pos()
        chip_copies = lambda: _chip_copies(chip_src, chip_out, chip_send, chip_recv)

        @pl.when(i == 0)
        def _():
            pack[...] = jnp.zeros_like(pack)
            for cp in chip_copies():
                cp.start()

        xv = x_ref[...]
        r0 = _rsqrt_mean_sq(xv)
        xh = xv * r0
        dh = _dot_nt(dz_ref[...], win_ref[...])
        _store_chunks(pack, 0, _colsum(dh * xh))
        dxh = dh * g0_ref[...]
        gx_ref[...] = dx1_ref[...] + r0 * (dxh - xh * jnp.mean(dxh * xh, axis=-1, keepdims=True))

        @pl.when(i == n - 1)
        def _():
            for cp in chip_copies():
                cp.wait()
            row0 = pl.multiple_of((4 * x_ + 2 * y_ + c_) * INP_PACK_ROWS, SUB)
            mine = parts_ref.at[pl.ds(row0, INP_PACK_ROWS), :]
            local = pltpu.make_async_copy(pack, mine, p_local)
            local.start()
            sends = []
            for k in range(1, N_DEV):
                peer = tuple(1 - v if (k >> bit) & 1 else v for v, bit in ((x_, 2), (y_, 1), (c_, 0)))
                sends.append(pltpu.make_async_remote_copy(
                    src_ref=pack, dst_ref=mine, send_sem=p_send.at[k - 1], recv_sem=p_recv.at[k - 1],
                    device_id=peer, device_id_type=MESH))
            for cp in sends:
                cp.start()
            for cp in sends:
                cp.wait()
            local.wait()

    tile = lambda w: pl.BlockSpec((tm, w), lambda i: (i, 0))
    return pl.pallas_call(
        body, name="bwd_dx", grid=(n,),
        in_specs=[tile(nc), tile(d), tile(d), _const_spec(win.shape), _const_spec(g0.shape)] + [HBM_SPEC] * nm,
        out_specs=(tile(d),) + (HBM_SPEC,) * (nm + 1),
        out_shape=(jax.ShapeDtypeStruct((t, d), F32),)
        + tuple(jax.ShapeDtypeStruct((3,) + g.shape[1:], g.dtype) for g in chip_srcs)
        + (jax.ShapeDtypeStruct((N_DEV * INP_PACK_ROWS, HEAD), F32),),
        scratch_shapes=[pltpu.VMEM((INP_PACK_ROWS, HEAD), F32),
                        pltpu.SemaphoreType.DMA((nm, 3)), pltpu.SemaphoreType.DMA((nm, 3)),
                        pltpu.SemaphoreType.DMA((N_DEV - 1,)), pltpu.SemaphoreType.DMA((N_DEV - 1,)),
                        pltpu.SemaphoreType.DMA],
        compiler_params=pltpu.CompilerParams(dimension_semantics=("arbitrary",), vmem_limit_bytes=VMEM_LIMIT),
    )(dz, x, dx1, win, g0, *chip_srcs)


ROW_BLOCKS = 8


def _chip_sums(core_idx, grads, recvs, dtypes, name):
    nw = len(grads)

    def body(idx_ref, *refs):
        gs, rs, outs = refs[:nw], refs[nw:2 * nw], refs[2 * nw:]
        for g, r, o in zip(gs, rs, outs):
            o[...] = (g[...] + r[...]).astype(o.dtype)

    def blk(g):
        return (None,) + g.shape[1:]

    return pl.pallas_call(
        body, name=name,
        grid_spec=pltpu.PrefetchScalarGridSpec(
            num_scalar_prefetch=1, grid=(4,),
            in_specs=[pl.BlockSpec(blk(g), lambda q, s: (2 * q + s[0], 0, 0)) for g in grads]
            + [pl.BlockSpec(blk(g), lambda q, s: (q, 0, 0)) for g in grads],
            out_specs=tuple(pl.BlockSpec(blk(g), lambda q, s: (q, 0, 0)) for g in grads)),
        out_shape=tuple(jax.ShapeDtypeStruct((4,) + g.shape[1:], dt) for g, dt in zip(grads, dtypes)),
    )(core_idx, *grads, *recvs)


def _adamw(w, g, m, v):
    m = ADAM_B1 * m + (1.0 - ADAM_B1) * g
    v = ADAM_B2 * v + (1.0 - ADAM_B2) * (g * g)
    m_hat = m / (1.0 - ADAM_B1 ** ADAM_STEP)
    v_hat = v / (1.0 - ADAM_B2 ** ADAM_STEP)
    delta = -ADAM_LR * (m_hat / (jnp.sqrt(v_hat) + ADAM_EPS) + ADAM_WD * w)
    return delta, m, v


def _adam_sections(sec_idx, grads, recv_sib, recv_chips, wmv):
    nw = len(wmv)

    def body(idx_ref, *refs):
        gs, rs, cs = refs[:nw], refs[nw:2 * nw], refs[2 * nw:3 * nw]
        params = refs[3 * nw:6 * nw]
        outs = refs[6 * nw:]
        for w in range(nw):
            g = gs[w][...] + rs[w][...]
            for k in range(3):
                g = g + cs[w][k].astype(F32)
            wv, mv, vv = (params[3 * w + t][...] for t in range(3))
            delta, m_new, v_new = _adamw(wv, g, mv, vv)
            outs[4 * w][...] = g
            outs[4 * w + 1][...] = delta
            outs[4 * w + 2][...] = m_new
            outs[4 * w + 3][...] = v_new

    def rb(g):
        return g.shape[1] // ROW_BLOCKS

    in_specs = [pl.BlockSpec((None, rb(g), g.shape[2]), lambda i, s: (s[0], i, 0)) for g in grads]
    in_specs += [pl.BlockSpec((None, rb(g), g.shape[2]), lambda i, s: (s[1], i, 0)) for g in grads]
    in_specs += [pl.BlockSpec((3, rb(g), g.shape[2]), lambda i, s: (0, i, 0)) for g in grads]
    sec = lambda g: pl.BlockSpec((rb(g), g.shape[2]), lambda i, s: (i, 0))
    for w in range(nw):
        in_specs += [sec(grads[w])] * 3
    out_specs, out_shape = [], []
    for g in grads:
        out_specs += [sec(g)] * 4
        out_shape += [jax.ShapeDtypeStruct(g.shape[1:], F32)] * 4
    flat = [a for t in wmv for a in t]
    outs = pl.pallas_call(
        body, name="adam_sections",
        grid_spec=pltpu.PrefetchScalarGridSpec(num_scalar_prefetch=1, grid=(ROW_BLOCKS,), in_specs=in_specs,
                                               out_specs=tuple(out_specs)),
        out_shape=tuple(out_shape),
    )(sec_idx, *grads, *recv_sib, *recv_chips, *flat)
    return [outs[4 * w:4 * w + 4] for w in range(nw)]


def _adam_small(sec_idx, gpack, parts, parts_wmv, vec_params, mat_params, conv_wmv):
    items = [(0, parts_wmv, "parts")]
    items += [(r0, t, "vec") for r0, t in vec_params] + [(r0, t, "mat") for r0, t in mat_params]
    items.append((_SEQ_ROWS["dcw"][0], conv_wmv, "conv"))

    def body(idx_ref, g_ref, parts_ref, *refs):
        ins, outs = refs[:3 * len(items)], refs[3 * len(items):]
        for n, (r0, _, kind) in enumerate(items):
            w_ref, m_ref, v_ref = ins[3 * n:3 * n + 3]
            o = outs[4 * n:4 * n + 4]
            if kind == "mat":
                g = g_ref[pl.ds(r0, w_ref.shape[0]), :]
                res = (g,) + _adamw(w_ref[...], g, m_ref[...], v_ref[...])
                for ref, val in zip(o, res):
                    ref[...] = val
            elif kind in ("vec", "parts"):
                for h in range(w_ref.shape[1] // HEAD):
                    cols = slice(h * HEAD, (h + 1) * HEAD)
                    if kind == "vec":
                        g = g_ref[pl.ds(r0 + h, 1), :]
                    else:
                        g = parts_ref[pl.ds(h, 1), :]
                        for dev in range(1, N_DEV):
                            g = g + parts_ref[pl.ds(SUB * dev + h, 1), :]
                    res = (g,) + _adamw(w_ref[:, cols], g, m_ref[:, cols], v_ref[:, cols])
                    for ref, val in zip(o, res):
                        ref[:, cols] = val
            else:
                rows = lax.broadcasted_iota(jnp.int32, (SUB, HEAD), 0)
                for k in range(4):
                    blk = g_ref[pl.ds(r0 + SUB * k, SUB), :]
                    g = jnp.sum(jnp.where(rows == idx_ref[0], blk, 0.0), axis=0, keepdims=True)
                    row = pl.ds(k, 1)
                    res = (g,) + _adamw(w_ref[row, :], g, m_ref[row, :], v_ref[row, :])
                    for ref, val in zip(o, res):
                        ref[row, :] = val
        outs[4 * len(items)][...] = g_ref[pl.ds(_OFF_MID + MID_LOSS_ROW, 1), :]

    flat = [a for _, t, _ in items for a in t]
    full = lambda a: pl.BlockSpec(a.shape, lambda i, s: (0,) * a.ndim)
    out_specs, out_shape = [], []
    for _, t, _ in items:
        out_specs += [full(t[0])] * 4
        out_shape += [jax.ShapeDtypeStruct(t[0].shape, F32)] * 4
    loss_row = jax.ShapeDtypeStruct((1, HEAD), F32)
    out_specs.append(full(loss_row))
    out_shape.append(loss_row)
    outs = pl.pallas_call(
        body, name="adam_small",
        grid_spec=pltpu.PrefetchScalarGridSpec(num_scalar_prefetch=1, grid=(1,),
                                               in_specs=[full(gpack), full(parts)] + [full(a) for a in flat],
                                               out_specs=tuple(out_specs)),
        out_shape=tuple(out_shape),
    )(sec_idx, gpack, parts, *flat)
    return [outs[4 * n:4 * n + 4] for n in range(len(items))], outs[4 * len(items)]


def _col_slicer(width):
    return lambda ref, idx: ref.at[:, pl.ds(pl.multiple_of(idx * width, HEAD), width)]


def _row_slicer(rows):
    return lambda ref, idx: ref.at[pl.ds(pl.multiple_of(idx * rows, SUB), rows), :]


def kernel(x, p, norm_g, w_in, conv_w, conv_b, lru_w_a, lru_b_a, lru_w_x, lru_b_x, lru_lambda, pool_w, pool_scale, w_proj_lru, w_proj_pool, w_out, ple_norm_g, w_ple_gate, w_ple_proj, final_g, loss_target, m_norm_g, m_w_in, m_conv_w, m_conv_b, m_lru_w_a, m_lru_b_a, m_lru_w_x, m_lru_b_x, m_lru_lambda, m_pool_w, m_pool_scale, m_w_proj_lru, m_w_proj_pool, m_w_out, m_ple_norm_g, m_w_ple_gate, m_w_ple_proj, m_final_g, v_norm_g, v_w_in, v_conv_w, v_conv_b, v_lru_w_a, v_lru_b_a, v_lru_w_x, v_lru_b_x, v_lru_lambda, v_pool_w, v_pool_scale, v_w_proj_lru, v_w_proj_pool, v_w_out, v_ple_norm_g, v_w_ple_gate, v_w_ple_proj, v_final_g):
    nb, s, d = x.shape
    t = nb * s
    tm = min(TILE_M, s)
    sec_idx = (4 * lax.axis_index("x") + 2 * lax.axis_index("y") + lax.axis_index("c")).astype(jnp.int32)
    chip_idx = (2 * lax.axis_index("x") + lax.axis_index("y")).astype(jnp.int32)
    core_idx = lax.axis_index("c").astype(jnp.int32)

    h, ht, win, cw = _pre(
        x, norm_g, [w_in[0], conv_w[0]], [True, False],
        [(d, N_DEV * w_in.shape[2]), (conv_w.shape[1], N_DEV * conv_w.shape[2])],
        [_col_slicer(w_in.shape[2]), _col_slicer(conv_w.shape[2])], tm)
    late_shards = [w_proj_lru[0], w_proj_pool[0], w_out[0], w_ple_gate[0], w_ple_proj[0]]
    late_shapes = [(N_DEV * w_proj_lru.shape[1], d), (w_proj_pool.shape[1], N_DEV * w_proj_pool.shape[2]),
                   (N_DEV * w_out.shape[1], d), (N_DEV * w_ple_gate.shape[1], d),
                   (w_ple_proj.shape[1], N_DEV * w_ple_proj.shape[2])]
    late_slicers = [_row_slicer(w_proj_lru.shape[1]), _col_slicer(w_proj_pool.shape[2]), _row_slicer(w_out.shape[1]),
                    _row_slicer(w_ple_gate.shape[1]), _col_slicer(w_ple_proj.shape[2])]

    lw = cw.shape[1]
    pwid = pool_scale.shape[1]
    wa = lru_w_a[0].astype(_MM)
    wx = lru_w_x[0].astype(_MM)
    pw = pool_w[0].astype(_MM)
    ba, bx = lru_b_a[0], lru_b_x[0]
    gf = final_g.reshape(1, d)
    core = core_idx.reshape(1)

    fwd_out = _fwd_seq(
        h, win, cw, conv_b, wa, ba, wx, bx, lru_lambda, pw, pool_scale, late_shards, late_shapes, late_slicers, tm)
    z, hl, ya, yb = fwd_out[:4]
    saved = fwd_out[4:N_SEQ_SAVED]
    wpl, wpp, wout, wpg, wpe = fwd_out[N_SEQ_SAVED:]
    x2d = x.reshape(t, d)
    mid_out = _mid(
        ya.reshape(t, lw), yb.reshape(t, pwid), h.reshape(t, d), win, x2d, p[0].reshape(t, -1),
        loss_target.reshape(t, d), wpl, wpp, wout, wpg, wpe, ple_norm_g, gf, tm)
    dya, dyb, dm, dx1 = mid_out[:4]
    grads_mid, pack_mid, sib_mid = list(mid_out[4:9]), mid_out[9], list(mid_out[10:])
    sums_mid = _chip_sums(core, grads_mid, sib_mid, [_WIRE] * len(grads_mid), "chip_sums_mid")
    seq_out = _bwd_seq(z, hl, saved, dya.reshape(nb, s, lw), dyb.reshape(nb, s, pwid), dm.reshape(nb, s, -1), cw,
                       wa, wx, lru_lambda, pw, pool_scale, pack_mid, sums_mid, tm)
    dz, chips_mid = seq_out[0], list(seq_out[2:])
    small = seq_out[1].reshape(N_DEV, SMALL_SEC, HEAD)
    g_in, sib_in, _, _, gsmall = _bwd_win(ht, dz, small)
    (sums_in,) = _chip_sums(core, [g_in], [sib_in], [_WIRE], "chip_sums_in")
    grad_x, chips_in, g0_parts = _bwd_dx(dz.reshape(t, -1), x2d, dx1, win, norm_g, [sums_in], tm)

    grads = [g_in] + grads_mid
    recv_sib = [sib_in] + sib_mid
    recv_chips = [chips_in] + chips_mid
    wmv = [(w_in[0], m_w_in[0], v_w_in[0]), (w_proj_lru[0], m_w_proj_lru[0], v_w_proj_lru[0]),
           (w_proj_pool[0], m_w_proj_pool[0], v_w_proj_pool[0]), (w_out[0], m_w_out[0], v_w_out[0]),
           (w_ple_gate[0], m_w_ple_gate[0], v_w_ple_gate[0]), (w_ple_proj[0], m_w_ple_proj[0], v_w_ple_proj[0])]
    big = _adam_sections(jnp.stack([sec_idx, chip_idx]), grads, recv_sib, recv_chips, wmv)

    R = _SEQ_ROWS
    vec = lambda r0, *t: (r0, tuple(a.reshape(1, -1) for a in t))
    mat = lambda r0, *t: (r0, tuple(a.reshape(-1, HEAD) for a in t))
    norm_wmv = vec(0, norm_g, m_norm_g, v_norm_g)[1]
    vec_params = [vec(R["dcb"][0], conv_b, m_conv_b, v_conv_b),
                  vec(R["dlam"][0], lru_lambda, m_lru_lambda, v_lru_lambda),
                  vec(R["dps"][0], pool_scale, m_pool_scale, v_pool_scale),
                  vec(_OFF_MID, ple_norm_g, m_ple_norm_g, v_ple_norm_g),
                  vec(_OFF_MID + SUB, final_g, m_final_g, v_final_g)]
    mat_params = [mat(R["dwa"][0], lru_w_a, m_lru_w_a, v_lru_w_a), mat(R["dba"][0], lru_b_a, m_lru_b_a, v_lru_b_a),
                  mat(R["dwx"][0], lru_w_x, m_lru_w_x, v_lru_w_x), mat(R["dbx"][0], lru_b_x, m_lru_b_x, v_lru_b_x),
                  mat(R["dpw"][0], pool_w, m_pool_w, v_pool_w)]
    conv_wmv = (conv_w[0], m_conv_w[0], v_conv_w[0])
    small_out, loss_row = _adam_small(sec_idx.reshape(1), gsmall, g0_parts, norm_wmv, vec_params, mat_params, conv_wmv)
    loss = loss_row[0, 0]

    res = {}
    names_small = ["norm_g", "conv_b", "lru_lambda", "pool_scale", "ple_norm_g", "final_g",
                   "lru_w_a", "lru_b_a", "lru_w_x", "lru_b_x", "pool_w", "conv_w"]
    shapes = {"norm_g": norm_g, "conv_b": conv_b, "lru_lambda": lru_lambda, "pool_scale": pool_scale,
              "ple_norm_g": ple_norm_g, "final_g": final_g, "lru_w_a": lru_w_a, "lru_b_a": lru_b_a, "lru_w_x": lru_w_x,
              "lru_b_x": lru_b_x, "pool_w": pool_w, "conv_w": conv_w}
    for name, quad in zip(names_small, small_out):
        res[name] = [a.reshape(shapes[name].shape) for a in quad]
    names_big = ["w_in", "w_proj_lru", "w_proj_pool", "w_out", "w_ple_gate", "w_ple_proj"]
    for name, quad, (w, _, _) in zip(names_big, big, wmv):
        res[name] = [a.reshape((1,) + w.shape) for a in quad]
    order = ["norm_g", "w_in", "conv_w", "conv_b", "lru_w_a", "lru_b_a", "lru_w_x", "lru_b_x", "lru_lambda", "pool_w",
             "pool_scale", "w_proj_lru", "w_proj_pool", "w_out", "ple_norm_g", "w_ple_gate", "w_ple_proj", "final_g"]
    out = [loss, grad_x.reshape(nb, s, d)]
    for kind in range(4):
        out += [res[name][kind] for name in order]
    return tuple(out)
```

```python
import jax
import jax.numpy as jnp
from jax import lax
from jax.experimental import pallas as pl
from jax.experimental.pallas import tpu as pltpu

F32 = jnp.float32
_MM = jnp.bfloat16
_WIRE = jnp.bfloat16

EPS = 1e-6
LRU_C = 8.0
MULT_SQ_FLOOR = 1e-30
POOL_WINDOWS = (2, 4, 8, 16)
N_HEADS = 8
HEAD = 128
HALO = 16
SUB = 8

ADAM_LR = 0.001
ADAM_B1 = 0.9
ADAM_B2 = 0.999
ADAM_EPS = 1e-08
ADAM_WD = 0.01
ADAM_STEP = 10

N_DEV = 8
MESH = pl.DeviceIdType.MESH
VMEM_LIMIT = 60 * 1024 * 1024
TILE_M = 256
N_SEQ_SAVED = 10

_SEQ_ROWS = {"dwa": (0, 1024), "dwx": (1024, 1024), "dpw": (2048, 512), "dba": (2560, 8), "dbx": (2568, 8),
             "dcb": (2576, 8), "dlam": (2584, 8), "dcw": (2592, 32), "dps": (2624, 8)}
SEQ_PACK_ROWS = 2632
MID_PACK_ROWS = 24
MID_LOSS_ROW = 16
INP_PACK_ROWS = 8
SMALL_ROWS = 3072
SMALL_SEC = SMALL_ROWS // N_DEV
_OFF_MID = SEQ_PACK_ROWS


def _dot(a, b):
    return jnp.dot(a, b, preferred_element_type=F32)


def _dot_nt(a, b):
    return lax.dot_general(a, b, (((1,), (1,)), ((), ())), preferred_element_type=F32)


def _dot_tn(a, b):
    return lax.dot_general(a, b, (((0,), (0,)), ((), ())), preferred_element_type=F32)


def _sigmoid(v):
    return 0.5 * jnp.tanh(0.5 * v) + 0.5


def _lru_mult(log_a, a):
    m2 = jnp.maximum(-jnp.tanh(log_a) * (1.0 + a * a), MULT_SQ_FLOOR)
    inv = lax.rsqrt(m2)
    return m2 * inv, inv


def _rows_back(v, n):
    return pltpu.roll(v, n, 0) if n else v


def _rows_ahead(v, n):
    return pltpu.roll(v, v.shape[0] - n, 0) if n else v


def _softplus_neg(lam):
    e = jnp.exp(-jnp.abs(lam))
    w = 1.0 + e
    l1p = jnp.where(w == 1.0, e, jnp.log(w) * (e / (w - 1.0)))
    return jnp.maximum(-lam, 0.0) + l1p


def _rsqrt_mean_sq(v):
    return lax.rsqrt(jnp.mean(v * v, axis=-1, keepdims=True) + EPS)


def _colsum(v):
    return jnp.sum(v, axis=0, keepdims=True)


def _const_spec(shape):
    nd = len(shape)
    return pl.BlockSpec(shape, lambda *_: (0,) * nd, pipeline_mode=pl.Buffered(1))


HBM_SPEC = pl.BlockSpec(memory_space=pl.ANY)


def _my_pos():
    return lax.axis_index("x"), lax.axis_index("y"), lax.axis_index("c")


def _other_chips(x, y):
    return [(1 - x, y), (x, 1 - y), (1 - x, 1 - y)]


def _gather_sems(nw):
    return [pltpu.SemaphoreType.DMA((nw, 7)), pltpu.SemaphoreType.DMA((nw, 7)), pltpu.SemaphoreType.DMA((nw,))]


def _gather_phases(srcs, outs, slicers, send_sems, recv_sems, local_sems, relay_diagonal=False):
    n_direct = 2 if relay_diagonal else 3
    nw = len(srcs)
    x, y, c = _my_pos()
    me, sibling = (x, y, c), (x, y, 1 - c)
    chips = _other_chips(x, y)

    def part(w, pos):
        return slicers[w](outs[w], 4 * pos[0] + 2 * pos[1] + pos[2])

    def copy(w, k, block, to, src=None):
        return pltpu.make_async_remote_copy(
            src_ref=part(w, block) if src is None else src, dst_ref=part(w, block),
            send_sem=send_sems.at[w, k], recv_sem=recv_sems.at[w, k], device_id=to, device_id_type=MESH)

    def mine():
        return [pltpu.make_async_copy(srcs[w], part(w, me), local_sems.at[w]) for w in range(nw)]

    def first():
        out = []
        for w in range(nw):
            out.append(copy(w, 0, me, sibling, src=srcs[w]))
            out += [copy(w, 1 + j, me, (*chips[j], c), src=srcs[w]) for j in range(n_direct)]
        return out

    def relay(w, j):
        return copy(w, 3, (*chips[j], c), (*chips[1 - j], c))

    def relays():
        return [relay(w, 0) for w in range(nw)] if relay_diagonal else []

    def passed():
        return [copy(w, 4 + j, (*chip, c), sibling) for j, chip in enumerate(chips) for w in range(nw)]

    def start():
        for cp in mine() + first():
            cp.start()

    def forward():
        for j, chip in enumerate(chips):
            for w in range(nw):
                copy(w, 1 + j, (*chip, c), me).wait_recv()
                copy(w, 4 + j, (*chip, c), sibling).start()
            if relay_diagonal and j < 2:
                @pl.when(c == j)
                def _():
                    for w in range(nw):
                        relay(w, j).start()

    def finish():
        for w in range(nw):
            copy(w, 0, sibling, me).wait_recv()
            for j, chip in enumerate(chips):
                copy(w, 4 + j, (*chip, 1 - c), me).wait_recv()
        for cp in first() + relays() + passed():
            cp.wait_send()
        for cp in mine():
            cp.wait()

    return start, forward, finish


def _sibling_copies(srcs, outs, send_sems, recv_sems):
    x, y, c = _my_pos()
    return [pltpu.make_async_remote_copy(
        src_ref=srcs[w].at[2 * q + (1 - c)], dst_ref=outs[w].at[q], send_sem=send_sems.at[w, q],
        recv_sem=recv_sems.at[w, q], device_id=(x, y, 1 - c), device_id_type=MESH)
        for w in range(len(srcs)) for q in range(4)]


def _chip_copies(srcs, outs, send_sems, recv_sems, which=None):
    x, y, c = _my_pos()
    return [pltpu.make_async_remote_copy(
        src_ref=srcs[w].at[2 * px + py], dst_ref=outs[w].at[k], send_sem=send_sems.at[w, k],
        recv_sem=recv_sems.at[w, k], device_id=(px, py, c), device_id_type=MESH)
        for w in (range(len(srcs)) if which is None else which) for k, (px, py) in enumerate(_other_chips(x, y))]


def _lru_gates(xc, h, wa_ref, ba_ref, wx_ref, bx_ref, neg_c_sp):
    xc16 = xc.astype(_MM)
    r = _sigmoid(_dot(xc16, wa_ref[h]) + ba_ref[pl.ds(h, 1), :])
    i = _sigmoid(_dot(xc16, wx_ref[h]) + bx_ref[pl.ds(h, 1), :])
    return r, i, neg_c_sp * r


def _conv_head(xa_ext, cw_ref, cb_ref, cols, tm):
    ext = xa_ext[:, cols]
    xc = cb_ref[:, cols] + cw_ref[pl.ds(3, 1), cols] * ext[HALO:HALO + tm]
    for k in range(3):
        xc = xc + cw_ref[pl.ds(k, 1), cols] * _rows_back(ext, 3 - k)[HALO:HALO + tm]
    return xc


def _window_sum(ext, k, shift):
    n = 1
    while n < k:
        ext = ext + shift(ext, n)
        n *= 2
    return ext


def _pool_diff(xb_ext, g, k, pos, tm):
    cols = slice(g * HEAD, (g + 1) * HEAD)
    ext = xb_ext[:, cols]
    ws = _window_sum(ext, k, _rows_back)[HALO:HALO + tm]
    inv_cnt = 1.0 / jnp.minimum(pos + 1, k).astype(F32)
    return ws * inv_cnt - ext[HALO:HALO + tm], inv_cnt


def _staged_sources(shards, as_operand):
    in_specs = [_const_spec(a.shape) if c else HBM_SPEC for a, c in zip(shards, as_operand)]
    stages = [pltpu.VMEM(a.shape, _MM) for a, c in zip(shards, as_operand) if c]
    dtypes = [_MM if c else a.dtype for a, c in zip(shards, as_operand)]
    return in_specs, stages, dtypes


def _gather_sources(srcs, stages, as_operand):
    stages = iter(stages)
    return [next(stages) if c else src for src, c in zip(srcs, as_operand)]


def _fill_stages(srcs, staged, as_operand):
    for src, dst, c in zip(srcs, staged, as_operand):
        if c:
            dst[...] = src[...].astype(dst.dtype)


def _pre(x, g0, shards, as_operand, shapes, slicers, tm):
    nb, s, d = x.shape
    nt = s // tm
    nl = len(shards)
    shard_specs, stage_shapes, wire = _staged_sources(shards, as_operand)

    def body(x_ref, g0_ref, *rest):
        src, rest = rest[:nl], rest[nl:]
        h_ref, ht_ref = rest[:2]
        out, rest = rest[2:2 + nl], rest[2 + nl:]
        (send_sems, recv_sems, local_sems), stages = rest[:3], rest[3:]
        step_no = pl.program_id(0) * nt + pl.program_id(1)
        staged = _gather_sources(src, stages, as_operand)
        g_start, g_forward, g_finish = _gather_phases(staged, out, slicers, send_sems, recv_sems, local_sems,
                                                      relay_diagonal=True)

        @pl.when(step_no == 0)
        def _():
            _fill_stages(src, staged, as_operand)
            g_start()

        xv = x_ref[...]
        h16 = ((xv * _rsqrt_mean_sq(xv)) * g0_ref[...]).astype(_MM)
        h_ref[...] = h16
        ht_ref[...] = h16.T

        @pl.when(step_no == nb * nt - 1)
        def _():
            g_forward()
            g_finish()

    tile = pl.BlockSpec((None, tm, d), lambda b, j: (b, j, 0))
    return pl.pallas_call(
        body, name="pre", grid=(nb, nt),
        in_specs=[tile, _const_spec(g0.shape)] + shard_specs,
        out_specs=(tile, pl.BlockSpec((None, d, tm), lambda b, j: (b, 0, j))) + (HBM_SPEC,) * nl,
        out_shape=(jax.ShapeDtypeStruct((nb, s, d), _MM), jax.ShapeDtypeStruct((nb, d, s), _MM))
        + tuple(jax.ShapeDtypeStruct(shp, dt) for shp, dt in zip(shapes, wire)),
        scratch_shapes=_gather_sems(nl) + stage_shapes,
        compiler_params=pltpu.CompilerParams(dimension_semantics=("arbitrary", "arbitrary")),
    )(x, g0, *shards)


def _fwd_seq(h, win, cw, cb, wa, ba, wx, bx, lam, pw, ps, late_shards, late_shapes, late_slicers, tm):
    nb, s, d = h.shape
    lw = cw.shape[1]
    pwid = ps.shape[1]
    nc = 2 * lw + 2 * pwid
    nt = s // tm
    nl = len(late_shards)
    n_steps = nb * nt
    as_operand = [True] * nl
    shard_specs, stage_shapes, wire = _staged_sources(late_shards, as_operand)

    def body(h_ref, win_ref, cw_ref, cb_ref, wa_ref, ba_ref, wx_ref, bx_ref, lam_ref, pw_ref, ps_ref, *rest):
        late_src, rest = rest[:nl], rest[nl:]
        z_ref, hl_ref, ya_ref, yb_ref, xc_ref, r_ref, i_ref, a_buf, diff_ref, yp_ref = rest[:N_SEQ_SAVED]
        late_out, rest = rest[N_SEQ_SAVED:N_SEQ_SAVED + nl], rest[N_SEQ_SAVED + nl:]
        (xa_ext, xb_ext, carry, send_sems, recv_sems, local_sems), stages = rest[:6], rest[6:]
        j = pl.program_id(1)
        step_no = pl.program_id(0) * nt + j
        staged = _gather_sources(late_src, stages, as_operand)
        g_start, g_forward, g_finish = _gather_phases(staged, late_out, late_slicers, send_sems, recv_sems, local_sems)

        @pl.when(step_no == 0)
        def _():
            _fill_stages(late_src, staged, as_operand)
            g_start()

        pl.when(step_no == (3 * n_steps) // 4)(g_forward)

        @pl.when(j == 0)
        def _():
            xa_ext[pl.ds(0, HALO), :] = jnp.zeros((HALO, lw), F32)
            xb_ext[pl.ds(0, HALO), :] = jnp.zeros((HALO, pwid), F32)
            carry[...] = jnp.zeros_like(carry)

        h16 = h_ref[...]
        xa = _dot(h16, win_ref[:, 0:lw])
        z_ref[:, 0:lw] = xa
        xa_ext[pl.ds(HALO, tm), :] = xa
        for c0 in range(lw, nc, lw):
            z_ref[:, c0:c0 + lw] = _dot(h16, win_ref[:, c0:c0 + lw])

        for h in range(N_HEADS):
            cols = slice(h * HEAD, (h + 1) * HEAD)
            xc = _conv_head(xa_ext, cw_ref, cb_ref, cols, tm)
            neg_c_sp = -LRU_C * _softplus_neg(lam_ref[:, cols])
            r, i, log_a = _lru_gates(xc, h, wa_ref, ba_ref, wx_ref, bx_ref, neg_c_sp)
            a = jnp.exp(log_a)
            xc_ref[:, cols] = xc
            r_ref[:, cols] = r
            i_ref[:, cols] = i
            a_buf[:, cols] = a
            hl_ref[:, cols] = _lru_mult(log_a, a)[0] * (i * xc)

        rows = lax.broadcasted_iota(jnp.int32, (SUB, lw), 0)

        def step(c, car):
            i0 = pl.multiple_of(c * SUB, SUB)
            av = a_buf[pl.ds(i0, SUB), :]
            bv = hl_ref[pl.ds(i0, SUB), :]
            for sh in (1, 2, 4):
                m = rows >= sh
                a_sh = jnp.where(m, pltpu.roll(av, sh, 0), 1.0)
                b_sh = jnp.where(m, pltpu.roll(bv, sh, 0), 0.0)
                bv = av * b_sh + bv
                av = av * a_sh
            hv = av * car + bv
            hl_ref[pl.ds(i0, SUB), :] = hv
            return jnp.broadcast_to(hv[SUB - 1:SUB, :], (SUB, lw))

        carry[...] = lax.fori_loop(0, tm // SUB, step, carry[...])

        ga = z_ref[:, lw:2 * lw]
        ya_ref[...] = (hl_ref[...] * (ga * _sigmoid(ga))).astype(ya_ref.dtype)

        xb_ext[pl.ds(HALO, tm), :] = z_ref[:, 2 * lw:2 * lw + pwid]
        pos = j * tm + lax.broadcasted_iota(jnp.int32, (tm, HEAD), 0)
        for g, k in enumerate(POOL_WINDOWS):
            cols = slice(g * HEAD, (g + 1) * HEAD)
            diff16 = _pool_diff(xb_ext, g, k, pos, tm)[0].astype(_MM)
            yp = _dot(diff16, pw_ref[g])
            diff_ref[:, cols] = diff16
            yp_ref[:, cols] = yp
            gb = z_ref[:, 2 * lw + pwid + g * HEAD:2 * lw + pwid + (g + 1) * HEAD]
            yb_ref[:, cols] = ((yp * ps_ref[:, cols]) * (gb * _sigmoid(gb))).astype(yb_ref.dtype)

        xa_ext[pl.ds(0, HALO), :] = xa_ext[pl.ds(tm, HALO), :]
        xb_ext[pl.ds(0, HALO), :] = xb_ext[pl.ds(tm, HALO), :]
        pl.when(step_no == n_steps - 1)(g_finish)

    tile = lambda w: pl.BlockSpec((None, tm, w), lambda b, j: (b, j, 0))
    return pl.pallas_call(
        body, name="fwd_seq", grid=(nb, nt),
        in_specs=[tile(d), _const_spec((d, nc)), _const_spec(cw.shape), _const_spec(cb.shape),
                  _const_spec(wa.shape), _const_spec(ba.shape), _const_spec(wx.shape), _const_spec(bx.shape),
                  _const_spec(lam.shape), _const_spec(pw.shape), _const_spec(ps.shape)] + shard_specs,
        out_specs=(tile(nc), tile(lw), tile(lw), tile(pwid), tile(lw), tile(lw), tile(lw), tile(lw), tile(pwid),
                   tile(pwid)) + (HBM_SPEC,) * nl,
        out_shape=(jax.ShapeDtypeStruct((nb, s, nc), F32), jax.ShapeDtypeStruct((nb, s, lw), F32),
                   jax.ShapeDtypeStruct((nb, s, lw), _MM), jax.ShapeDtypeStruct((nb, s, pwid), _MM))
        + (jax.ShapeDtypeStruct((nb, s, lw), F32),) * 4
        + (jax.ShapeDtypeStruct((nb, s, pwid), _MM), jax.ShapeDtypeStruct((nb, s, pwid), F32))
        + tuple(jax.ShapeDtypeStruct(shp, dt) for shp, dt in zip(late_shapes, wire)),
        scratch_shapes=[pltpu.VMEM((tm + HALO, lw), F32), pltpu.VMEM((tm + HALO, pwid), F32),
                        pltpu.VMEM((SUB, lw), F32)] + _gather_sems(nl) + stage_shapes,
        compiler_params=pltpu.CompilerParams(dimension_semantics=("arbitrary", "arbitrary"), vmem_limit_bytes=VMEM_LIMIT),
    )(h, win, cw, cb, wa, ba, wx, bx, lam, pw, ps, *late_shards)


def _store_chunks(ref, row0, vec):
    for h in range(vec.shape[1] // HEAD):
        ref[pl.ds(row0 + h, 1), :] += vec[:, h * HEAD:(h + 1) * HEAD]


def _mid(ya, yb, h, win, x, p, tgt, wpl, wpp, wout, wpg, wpe, g1, gf, tm):
    t, d = x.shape
    lw = ya.shape[1]
    pwid = yb.shape[1]
    pdim = p.shape[1]
    n = t // tm
    sec_pp = wpp.shape[1] // N_DEV
    sec_pe = wpe.shape[1] // N_DEV
    sec_rows = wpl.shape[0] // N_DEV
    grad_secs = [(sec_rows, d), (wpp.shape[0], sec_pp), (sec_rows, d), (sec_rows, d), (wpe.shape[0], sec_pe)]
    n_win_blocks = win.shape[1] // d
    win_cols = lambda c: pl.BlockSpec((d, d), lambda i: (0, c), pipeline_mode=pl.Buffered(1))

    def body(ya_ref, yb_ref, h_ref, wma_ref, wmb_ref, x_ref, p_ref, tgt_ref, wpl_ref, wpp_ref, wout_ref, wpg_ref, wpe_ref,
             g1_ref, gf_ref,
             dya_ref, dyb_ref, dm_ref, dx1_ref, gpl_ref, gpp_ref, gout_ref, gpg_ref, gpe_ref, pack_ref,
             sib_pl, sib_pp, sib_out, sib_pg, sib_pe,
             acc_pl, acc_pp, acc_out, acc_pg, acc_pe, out_sems, sib_send, sib_recv):
        i = pl.program_id(0)

        @pl.when(i == 0)
        def _():
            for acc in (acc_pl, acc_pp, acc_out, acc_pg, acc_pe):
                acc[...] = jnp.zeros_like(acc)
            pack_ref[...] = jnp.zeros_like(pack_ref)

        ya16 = ya_ref[...]
        yb16 = yb_ref[...]
        br_a = _dot(ya16, wpl_ref[...])
        br_b = _dot(yb16, wpp_ref[...])
        h16 = h_ref[...]
        sa = _sigmoid(_dot(h16, wma_ref[...]))
        sb = _sigmoid(_dot(h16, wmb_ref[...]))
        mg16 = (sa * br_a + sb * br_b).astype(_MM)
        x1 = x_ref[...] + _dot(mg16, wout_ref[...])
        r1 = _rsqrt_mean_sq(x1)
        n1 = x1 * r1
        h116 = (n1 * g1_ref[...]).astype(_MM)
        gate = _sigmoid(_dot(h116, wpg_ref[...]))
        p16 = p_ref[...].astype(_MM)
        pe = _dot(p16, wpe_ref[...])
        x2 = x1 + gate * pe
        r2 = _rsqrt_mean_sq(x2)
        n2 = x2 * r2
        err = n2 * gf_ref[...] - tgt_ref[...]
        sq = jnp.sum(_colsum(err * err), axis=1, keepdims=True)
        pack_ref[pl.ds(MID_LOSS_ROW, 1), :] += jnp.broadcast_to(sq * (0.5 / d), (1, HEAD))

        dy = err * (1.0 / d)
        _store_chunks(pack_ref, SUB, _colsum(dy * n2))
        dn2 = dy * gf_ref[...]
        dx2 = r2 * (dn2 - n2 * jnp.mean(dn2 * n2, axis=-1, keepdims=True))
        dpe16 = (dx2 * gate).astype(_MM)
        dpg16 = ((dx2 * pe) * (gate * (1.0 - gate))).astype(_MM)
        acc_pe[...] += _dot_tn(p16, dpe16)
        acc_pg[...] += _dot_tn(h116, dpg16)
        dh1 = _dot_nt(dpg16, wpg_ref[...])
        _store_chunks(pack_ref, 0, _colsum(dh1 * n1))
        dn1 = dh1 * g1_ref[...]
        dx1 = dx2 + r1 * (dn1 - n1 * jnp.mean(dn1 * n1, axis=-1, keepdims=True))
        dx1_ref[...] = dx1
        dx116 = dx1.astype(_MM)
        acc_out[...] += _dot_tn(mg16, dx116)
        dmg = _dot_nt(dx116, wout_ref[...])
        da16 = (dmg * sa).astype(_MM)
        db16 = (dmg * sb).astype(_MM)
        dm_ref[:, 0:d] = ((dmg * br_a) * (sa * (1.0 - sa))).astype(dm_ref.dtype)
        dm_ref[:, d:2 * d] = ((dmg * br_b) * (sb * (1.0 - sb))).astype(dm_ref.dtype)
        acc_pl[...] += _dot_tn(ya16, da16)
        acc_pp[...] += _dot_tn(yb16, db16)
        dya_ref[...] = _dot_nt(da16, wpl_ref[...])
        dyb_ref[...] = _dot_nt(db16, wpp_ref[...])

        @pl.when(i == n - 1)
        def _():
            writes = []
            for k in range(N_DEV):
                rows = pl.ds(k * sec_rows, sec_rows)
                for acc, out in ((acc_pl, gpl_ref), (acc_out, gout_ref), (acc_pg, gpg_ref)):
                    writes.append((acc.at[rows, :], out.at[k]))
                writes.append((acc_pp.at[:, pl.ds(k * sec_pp, sec_pp)], gpp_ref.at[k]))
                writes.append((acc_pe.at[:, pl.ds(k * sec_pe, sec_pe)], gpe_ref.at[k]))
            writes = [pltpu.make_async_copy(src, dst, out_sems.at[n]) for n, (src, dst) in enumerate(writes)]
            for cp in writes:
                cp.start()
            for cp in writes:
                cp.wait()
            to_sibling = _sibling_copies([gpl_ref, gpp_ref, gout_ref, gpg_ref, gpe_ref],
                                         [sib_pl, sib_pp, sib_out, sib_pg, sib_pe], sib_send, sib_recv)
            for cp in to_sibling:
                cp.start()
            for cp in to_sibling:
                cp.wait()

    tile = lambda w: pl.BlockSpec((tm, w), lambda i: (i, 0))
    any_spec = pl.BlockSpec(memory_space=pl.ANY)
    full = lambda shape: pl.BlockSpec(shape, lambda i: (0,) * len(shape))
    return pl.pallas_call(
        body, name="mid", grid=(n,),
        in_specs=[tile(lw), tile(pwid), tile(d), win_cols(n_win_blocks - 2), win_cols(n_win_blocks - 1),
                  tile(d), tile(pdim), tile(d),
                  _const_spec(wpl.shape), _const_spec(wpp.shape), _const_spec(wout.shape), _const_spec(wpg.shape),
                  _const_spec(wpe.shape), _const_spec(g1.shape), _const_spec(gf.shape)],
        out_specs=(tile(lw), tile(pwid), tile(2 * d), tile(d), any_spec, any_spec, any_spec, any_spec, any_spec,
                   full((MID_PACK_ROWS, HEAD))) + (any_spec,) * 5,
        out_shape=(jax.ShapeDtypeStruct((t, lw), F32), jax.ShapeDtypeStruct((t, pwid), F32),
                   jax.ShapeDtypeStruct((t, 2 * d), _MM), jax.ShapeDtypeStruct((t, d), F32),
                   *(jax.ShapeDtypeStruct((N_DEV,) + sec, F32) for sec in grad_secs),
                   jax.ShapeDtypeStruct((MID_PACK_ROWS, HEAD), F32),
                   *(jax.ShapeDtypeStruct((N_DEV // 2,) + sec, F32) for sec in grad_secs)),
        scratch_shapes=[pltpu.VMEM(wpl.shape, F32), pltpu.VMEM(wpp.shape, F32), pltpu.VMEM(wout.shape, F32),
                        pltpu.VMEM(wpg.shape, F32), pltpu.VMEM(wpe.shape, F32),
                        pltpu.SemaphoreType.DMA((5 * N_DEV,)), pltpu.SemaphoreType.DMA((5, N_DEV // 2)),
                        pltpu.SemaphoreType.DMA((5, N_DEV // 2))],
        compiler_params=pltpu.CompilerParams(dimension_semantics=("arbitrary",), vmem_limit_bytes=VMEM_LIMIT),
    )(ya, yb, h, win, win, x, p, tgt, wpl, wpp, wout, wpg, wpe, g1, gf)


def _bwd_seq(z, hl, saved, dya, dyb, dm, cw, wa, wx, lam, pw, ps, pack_mid, sib_grads, tm):
    nb, s, zw = z.shape
    lw = cw.shape[1]
    pwid = ps.shape[1]
    nt = s // tm
    nc = zw + dm.shape[2]
    R = _SEQ_ROWS
    ns = len(sib_grads)

    def body(zq_ref, hl_ref, hlh_ref, xc_buf, r_buf, i_buf, a_ref, diff_ref, yp_ref, dya_ref, dyb_ref, dm_ref,
             cw_ref, wa_ref, wx_ref, lam_ref, pw_ref, ps_ref, pack_mid_ref, *rest):
        sib_src, rest = rest[:ns], rest[ns:]
        dz_ref, pack_ref = rest[:2]
        sib_out, rest = rest[2:2 + ns], rest[2 + ns:]
        a_ext, an_buf, hl_ext, dh_buf, dxc_ext, q_ext, carry, send_sems, recv_sems = rest
        b = pl.program_id(0)
        j = pl.program_id(1)
        jr = nt - 1 - j
        has_prev = jr > 0

        @pl.when((b == 0) & (j == 0))
        def _():
            for cp in _chip_copies(sib_src, sib_out, send_sems, recv_sems):
                cp.start()

        dz_ref[:, zw:nc] = dm_ref[...]

        @pl.when((b == 0) & (j == 0))
        def _():
            pack_ref[...] = jnp.zeros_like(pack_ref)

        @pl.when(j == 0)
        def _():
            a_ext[pl.ds(tm, SUB), :] = jnp.zeros((SUB, lw), F32)
            dxc_ext[pl.ds(tm, HALO), :] = jnp.zeros((HALO, lw), F32)
            q_ext[pl.ds(tm, HALO), :] = jnp.zeros((HALO, pwid), F32)
            carry[...] = jnp.zeros_like(carry)

        hl_ext[pl.ds(0, SUB), :] = jnp.where(has_prev, hlh_ref[...], 0.0)
        hl_ext[pl.ds(SUB, tm), :] = hl_ref[...]
        a_ext[pl.ds(0, tm), :] = a_ref[...]
        for h in range(N_HEADS):
            cols = slice(h * HEAD, (h + 1) * HEAD)
            ga = zq_ref[:, lw + h * HEAD:lw + (h + 1) * HEAD]
            sg = _sigmoid(ga)
            dyav = dya_ref[:, cols]
            dh_buf[:, cols] = dyav * (ga * sg)
            dga = (dyav * hl_ext[pl.ds(SUB, tm), cols]) * (sg * (1.0 + ga * (1.0 - sg)))
            dz_ref[:, lw + h * HEAD:lw + (h + 1) * HEAD] = dga.astype(dz_ref.dtype)
        an_buf[...] = _rows_ahead(a_ext[...], 1)[0:tm]

        rows = lax.broadcasted_iota(jnp.int32, (SUB, lw), 0)
        nch = tm // SUB

        def step(c, car):
            i0 = pl.multiple_of((nch - 1 - c) * SUB, SUB)
            cv = an_buf[pl.ds(i0, SUB), :]
            bv = dh_buf[pl.ds(i0, SUB), :]
            for sh in (1, 2, 4):
                m = rows < SUB - sh
                c_sh = jnp.where(m, pltpu.roll(cv, SUB - sh, 0), 1.0)
                b_sh = jnp.where(m, pltpu.roll(bv, SUB - sh, 0), 0.0)
                bv = cv * b_sh + bv
                cv = cv * c_sh
            hv = cv * car + bv
            dh_buf[pl.ds(i0, SUB), :] = hv
            return jnp.broadcast_to(hv[0:1, :], (SUB, lw))

        carry[...] = lax.fori_loop(0, nch, step, carry[...])

        for h in range(N_HEADS):
            cols = slice(h * HEAD, (h + 1) * HEAD)
            lam_h = lam_ref[:, cols]
            neg_c_sp = -LRU_C * _softplus_neg(lam_h)
            xc = xc_buf[:, cols]
            r = r_buf[:, cols]
            i = i_buf[:, cols]
            a = a_ref[:, cols]
            mult, inv_mult = _lru_mult(neg_c_sp * r, a)
            dh = dh_buf[:, cols]
            dhx = dh * xc
            di = dhx * mult
            h_prev = _rows_back(hl_ext[:, cols], 1)[SUB:SUB + tm]
            dlog_a = (dh * h_prev) * a - (dhx * i) * ((a * a) * inv_mult)
            pack_ref[pl.ds(R["dlam"][0] + h, 1), :] += _colsum(dlog_a * r) * (LRU_C * _sigmoid(-lam_h))
            dpa = (dlog_a * neg_c_sp) * (r * (1.0 - r))
            dpx = di * (i * (1.0 - i))
            dpa16 = dpa.astype(_MM)
            dpx16 = dpx.astype(_MM)
            xc16 = xc.astype(_MM)
            dxc = dh * (mult * i) + _dot_nt(dpa16, wa_ref[h]) + _dot_nt(dpx16, wx_ref[h])
            pack_ref[pl.ds(R["dwa"][0] + h * HEAD, HEAD), :] += _dot_tn(xc16, dpa16)
            pack_ref[pl.ds(R["dwx"][0] + h * HEAD, HEAD), :] += _dot_tn(xc16, dpx16)
            pack_ref[pl.ds(R["dba"][0] + h, 1), :] += _colsum(dpa)
            pack_ref[pl.ds(R["dbx"][0] + h, 1), :] += _colsum(dpx)
            pack_ref[pl.ds(R["dcb"][0] + h, 1), :] += _colsum(dxc)
            dxc_ext[pl.ds(0, tm), cols] = dxc

        for h in range(N_HEADS):
            cols = slice(h * HEAD, (h + 1) * HEAD)
            dxc_all = dxc_ext[:, cols]
            xa = zq_ref[:, cols]
            dxa = None
            for k in range(4):
                dxc_k = _rows_ahead(dxc_all, 3 - k)[0:tm]
                pack_ref[pl.ds(R["dcw"][0] + SUB * k + h, 1), :] += _colsum(dxc_k * xa)
                term = cw_ref[pl.ds(k, 1), cols] * dxc_k
                dxa = term if dxa is None else dxa + term
            dz_ref[:, cols] = dxa.astype(dz_ref.dtype)

        pos = jr * tm + lax.broadcasted_iota(jnp.int32, (tm, HEAD), 0)
        for g, k in enumerate(POOL_WINDOWS):
            cols = slice(g * HEAD, (g + 1) * HEAD)
            inv_cnt = 1.0 / jnp.minimum(pos + 1, k).astype(F32)
            diff16 = diff_ref[:, cols]
            yp = yp_ref[:, cols]
            sc = ps_ref[:, cols]
            gb = zq_ref[:, 2 * lw + pwid + g * HEAD:2 * lw + pwid + (g + 1) * HEAD]
            sgb = _sigmoid(gb)
            dybv = dyb_ref[:, cols]
            dy_pool = dybv * (gb * sgb)
            dgb = (dybv * (yp * sc)) * (sgb * (1.0 + gb * (1.0 - sgb)))
            pack_ref[pl.ds(R["dps"][0] + g, 1), :] += _colsum(dy_pool * yp)
            dyp16 = (dy_pool * sc).astype(_MM)
            pack_ref[pl.ds(R["dpw"][0] + g * HEAD, HEAD), :] += _dot_tn(diff16, dyp16)
            ddiff = _dot_nt(dyp16, pw_ref[g])
            q_ext[pl.ds(0, tm), cols] = ddiff * inv_cnt
            dxb = _window_sum(q_ext[:, cols], k, _rows_ahead)[0:tm] - ddiff
            dz_ref[:, 2 * lw + g * HEAD:2 * lw + (g + 1) * HEAD] = dxb.astype(dz_ref.dtype)
            dz_ref[:, 2 * lw + pwid + g * HEAD:2 * lw + pwid + (g + 1) * HEAD] = dgb.astype(dz_ref.dtype)

        a_ext[pl.ds(tm, SUB), :] = a_ext[pl.ds(0, SUB), :]
        dxc_ext[pl.ds(tm, HALO), :] = dxc_ext[pl.ds(0, HALO), :]
        q_ext[pl.ds(tm, HALO), :] = q_ext[pl.ds(0, HALO), :]

        @pl.when((b == nb - 1) & (j == nt - 1))
        def _():
            pack_ref[pl.ds(_OFF_MID, MID_PACK_ROWS), :] = pack_mid_ref[...]
            for cp in _chip_copies(sib_src, sib_out, send_sems, recv_sems):
                cp.wait()

    rev = lambda w: pl.BlockSpec((None, tm, w), lambda b, j: (b, nt - 1 - j, 0))
    prev_rows = lambda rows, w: pl.BlockSpec(
        (None, rows, w), lambda b, j: (b, jnp.maximum((nt - 1 - j) * (tm // rows) - 1, 0), 0))
    return pl.pallas_call(
        body, name="bwd_seq", grid=(nb, nt),
        in_specs=[rev(zw), rev(lw), prev_rows(SUB, lw), rev(lw), rev(lw), rev(lw), rev(lw), rev(pwid), rev(pwid),
                  rev(lw), rev(pwid), rev(nc - zw),
                  _const_spec(cw.shape), _const_spec(wa.shape), _const_spec(wx.shape), _const_spec(lam.shape),
                  _const_spec(pw.shape), _const_spec(ps.shape), _const_spec(pack_mid.shape)] + [HBM_SPEC] * ns,
        out_specs=(rev(nc), pl.BlockSpec((SMALL_ROWS, HEAD), lambda b, j: (0, 0))) + (HBM_SPEC,) * ns,
        out_shape=(jax.ShapeDtypeStruct((nb, s, nc), _MM), jax.ShapeDtypeStruct((SMALL_ROWS, HEAD), F32))
        + tuple(jax.ShapeDtypeStruct((3,) + g.shape[1:], g.dtype) for g in sib_grads),
        scratch_shapes=[pltpu.VMEM((tm + SUB, lw), F32), pltpu.VMEM((tm, lw), F32), pltpu.VMEM((tm + SUB, lw), F32),
                        pltpu.VMEM((tm, lw), F32), pltpu.VMEM((tm + HALO, lw), F32),
                        pltpu.VMEM((tm + HALO, pwid), F32), pltpu.VMEM((SUB, lw), F32),
                        pltpu.SemaphoreType.DMA((ns, 3)), pltpu.SemaphoreType.DMA((ns, 3))],
        compiler_params=pltpu.CompilerParams(dimension_semantics=("arbitrary", "arbitrary"), vmem_limit_bytes=VMEM_LIMIT),
    )(z, hl, hl, *saved, dya, dyb, dm, cw, wa, wx, lam, pw, ps, pack_mid, *sib_grads)


def _bwd_win(ht, dz, small):
    nb, d, s = ht.shape
    nc = dz.shape[2]
    sec = nc // N_DEV
    n_chips = N_DEV // 2

    def body(ht_ref, dz_ref, small_src, gwin_ref, recv_ref, small_out, small_chips_out, gsmall_ref,
             acc, own4, sib4, csum, got3, total, local_sems, sib_send, sib_recv, small_send, small_recv,
             sc_send, sc_recv, ld_sems, g_send, g_recv, g_local):
        q = pl.program_id(0)
        b = pl.program_id(1)
        slot = q % 2
        x, y, c = _my_pos()

        def half(sl, core):
            return acc.at[sl, :, pl.ds(pl.multiple_of(core * sec, HEAD), sec)]

        def local_copy(qq, sl):
            return pltpu.make_async_copy(half(sl, c), gwin_ref.at[2 * qq + c], local_sems.at[sl])

        def sib_copy(qq, sl):
            return pltpu.make_async_remote_copy(
                src_ref=half(sl, 1 - c), dst_ref=recv_ref.at[qq], send_sem=sib_send.at[qq],
                recv_sem=sib_recv.at[qq], device_id=(x, y, 1 - c), device_id_type=MESH)

        step_no = q * nb + b
        n_steps = n_chips * nb
        sum_step = min(3, n_steps - 1)
        fwd_step = max(sum_step, min(6, n_steps - 1))
        small_sib = lambda: _sibling_copies([small_src], [small_out], small_send, small_recv)
        small_chip = lambda: _chip_copies([csum], [small_chips_out], sc_send, sc_recv)
        g_start, g_forward, g_finish = _gather_phases([total], [gsmall_ref], [_row_slicer(small.shape[1])],
                                                      g_send, g_recv, g_local)

        def load(srcs, dst):
            cps = [pltpu.make_async_copy(src, dst.at[k], ld_sems.at[k]) for k, src in enumerate(srcs)]
            for cp in cps:
                cp.start()
            for cp in cps:
                cp.wait()

        @pl.when(step_no == 0)
        def _():
            for cp in small_sib():
                cp.start()

        @pl.when(step_no == 1)
        def _():
            for cp in small_sib():
                cp.wait_recv()
            load([small_src.at[2 * qq + c] for qq in range(n_chips)], own4)
            load([small_out.at[qq] for qq in range(n_chips)], sib4)
            csum[...] = own4[...] + sib4[...]
            for cp in small_chip():
                cp.start()

        @pl.when(step_no == sum_step)
        def _():
            for cp in small_chip():
                cp.wait_recv()
            load([small_chips_out.at[k] for k in range(3)], got3)
            total[...] = ((csum[2 * x + y] + got3[0]) + got3[1]) + got3[2]
            g_start()

        pl.when(step_no == fwd_step)(g_forward)

        @pl.when((q >= 2) & (b == 0))
        def _():
            local_copy(q - 2, slot).wait()
            sib_copy(q - 2, slot).wait_send()

        part = _dot(ht_ref[b], dz_ref[...])

        @pl.when(b == 0)
        def _():
            acc[slot] = part

        @pl.when(b != 0)
        def _():
            acc[slot] += part

        @pl.when(b == nb - 1)
        def _():
            local_copy(q, slot).start()
            sib_copy(q, slot).start()

        @pl.when((q == n_chips - 1) & (b == nb - 1))
        def _():
            for qq in (n_chips - 2, n_chips - 1):
                local_copy(qq, qq % 2).wait()
                sib_copy(qq, qq % 2).wait_send()
            for qq in range(n_chips):
                sib_copy(qq, 0).wait_recv()
            g_finish()
            for cp in small_sib() + small_chip():
                cp.wait_send()

    return pl.pallas_call(
        body, name="bwd_win", grid=(n_chips, nb),
        in_specs=[_const_spec(ht.shape), pl.BlockSpec((None, s, 2 * sec), lambda q, b: (b, 0, q))]
        + [HBM_SPEC],
        out_specs=(HBM_SPEC,) * 5,
        out_shape=(jax.ShapeDtypeStruct((N_DEV, d, sec), F32), jax.ShapeDtypeStruct((n_chips, d, sec), F32))
        + (jax.ShapeDtypeStruct((n_chips,) + small.shape[1:], small.dtype),
           jax.ShapeDtypeStruct((3,) + small.shape[1:], small.dtype),
           jax.ShapeDtypeStruct((N_DEV * small.shape[1], small.shape[2]), small.dtype)),
        scratch_shapes=[pltpu.VMEM((2, d, 2 * sec), F32)]
        + [pltpu.VMEM((n_chips,) + small.shape[1:], F32)] * 3
        + [pltpu.VMEM((3,) + small.shape[1:], F32), pltpu.VMEM(small.shape[1:], F32), pltpu.SemaphoreType.DMA((2,)),
           pltpu.SemaphoreType.DMA((n_chips,)), pltpu.SemaphoreType.DMA((n_chips,)),
           pltpu.SemaphoreType.DMA((1, 4)), pltpu.SemaphoreType.DMA((1, 4)),
           pltpu.SemaphoreType.DMA((1, 3)), pltpu.SemaphoreType.DMA((1, 3)), pltpu.SemaphoreType.DMA((n_chips,))]
        + _gather_sems(1),
        compiler_params=pltpu.CompilerParams(dimension_semantics=("arbitrary", "arbitrary"), vmem_limit_bytes=VMEM_LIMIT),
    )(ht, dz, small)


def _bwd_dx(dz, x, dx1, win, g0, chip_srcs, tm):
    t, d = x.shape
    nc = win.shape[1]
    n = t // tm
    nm = len(chip_srcs)

    def body(dz_ref, x_ref, dx1_ref, win_ref, g0_ref, *rest):
        chip_src = rest[:nm]
        gx_ref = rest[nm]
        chip_out = rest[nm + 1:2 * nm + 1]
        parts_ref = rest[2 * nm + 1]
        pack, chip_send, chip_recv, p_send, p_recv, p_local = rest[2 * nm + 2:]
        i = pl.program_id(0)
        x_, y_, c_ = _my_pos()
        chip_copies = lambda: _chip_copies(chip_src, chip_out, chip_send, chip_recv)

        @pl.when(i == 0)
        def _():
            pack[...] = jnp.zeros_like(pack)
            for cp in chip_copies():
                cp.start()

        xv = x_ref[...]
        r0 = _rsqrt_mean_sq(xv)
        xh = xv * r0
        dh = _dot_nt(dz_ref[...], win_ref[...])
        _store_chunks(pack, 0, _colsum(dh * xh))
        dxh = dh * g0_ref[...]
        gx_ref[...] = dx1_ref[...] + r0 * (dxh - xh * jnp.mean(dxh * xh, axis=-1, keepdims=True))

        @pl.when(i == n - 1)
        def _():
            for cp in chip_copies():
                cp.wait()
            row0 = pl.multiple_of((4 * x_ + 2 * y_ + c_) * INP_PACK_ROWS, SUB)
            mine = parts_ref.at[pl.ds(row0, INP_PACK_ROWS), :]
            local = pltpu.make_async_copy(pack, mine, p_local)
            local.start()
            sends = []
            for k in range(1, N_DEV):
                peer = tuple(1 - v if (k >> bit) & 1 else v for v, bit in ((x_, 2), (y_, 1), (c_, 0)))
                sends.append(pltpu.make_async_remote_copy(
                    src_ref=pack, dst_ref=mine, send_sem=p_send.at[k - 1], recv_sem=p_recv.at[k - 1],
                    device_id=peer, device_id_type=MESH))
            for cp in sends:
                cp.start()
            for cp in sends:
                cp.wait()
            local.wait()

    tile = lambda w: pl.BlockSpec((tm, w), lambda i: (i, 0))
    return pl.pallas_call(
        body, name="bwd_dx", grid=(n,),
        in_specs=[tile(nc), tile(d), tile(d), _const_spec(win.shape), _const_spec(g0.shape)] + [HBM_SPEC] * nm,
        out_specs=(tile(d),) + (HBM_SPEC,) * (nm + 1),
        out_shape=(jax.ShapeDtypeStruct((t, d), F32),)
        + tuple(jax.ShapeDtypeStruct((3,) + g.shape[1:], g.dtype) for g in chip_srcs)
        + (jax.ShapeDtypeStruct((N_DEV * INP_PACK_ROWS, HEAD), F32),),
        scratch_shapes=[pltpu.VMEM((INP_PACK_ROWS, HEAD), F32),
                        pltpu.SemaphoreType.DMA((nm, 3)), pltpu.SemaphoreType.DMA((nm, 3)),
                        pltpu.SemaphoreType.DMA((N_DEV - 1,)), pltpu.SemaphoreType.DMA((N_DEV - 1,)),
                        pltpu.SemaphoreType.DMA],
        compiler_params=pltpu.CompilerParams(dimension_semantics=("arbitrary",), vmem_limit_bytes=VMEM_LIMIT),
    )(dz, x, dx1, win, g0, *chip_srcs)


ROW_BLOCKS = 8


def _chip_sums(core_idx, grads, recvs, dtypes, name):
    nw = len(grads)

    def body(idx_ref, *refs):
        gs, rs, outs = refs[:nw], refs[nw:2 * nw], refs[2 * nw:]
        for g, r, o in zip(gs, rs, outs):
            o[...] = (g[...] + r[...]).astype(o.dtype)

    def blk(g):
        return (None,) + g.shape[1:]

    return pl.pallas_call(
        body, name=name,
        grid_spec=pltpu.PrefetchScalarGridSpec(
            num_scalar_prefetch=1, grid=(4,),
            in_specs=[pl.BlockSpec(blk(g), lambda q, s: (2 * q + s[0], 0, 0)) for g in grads]
            + [pl.BlockSpec(blk(g), lambda q, s: (q, 0, 0)) for g in grads],
            out_specs=tuple(pl.BlockSpec(blk(g), lambda q, s: (q, 0, 0)) for g in grads)),
        out_shape=tuple(jax.ShapeDtypeStruct((4,) + g.shape[1:], dt) for g, dt in zip(grads, dtypes)),
    )(core_idx, *grads, *recvs)


def _adamw(w, g, m, v):
    m = ADAM_B1 * m + (1.0 - ADAM_B1) * g
    v = ADAM_B2 * v + (1.0 - ADAM_B2) * (g * g)
    m_hat = m / (1.0 - ADAM_B1 ** ADAM_STEP)
    v_hat = v / (1.0 - ADAM_B2 ** ADAM_STEP)
    delta = -ADAM_LR * (m_hat / (jnp.sqrt(v_hat) + ADAM_EPS) + ADAM_WD * w)
    return delta, m, v


def _adam_sections(sec_idx, grads, recv_sib, recv_chips, wmv):
    nw = len(wmv)

    def body(idx_ref, *refs):
        gs, rs, cs = refs[:nw], refs[nw:2 * nw], refs[2 * nw:3 * nw]
        params = refs[3 * nw:6 * nw]
        outs = refs[6 * nw:]
        for w in range(nw):
            g = gs[w][...] + rs[w][...]
            for k in range(3):
                g = g + cs[w][k].astype(F32)
            wv, mv, vv = (params[3 * w + t][...] for t in range(3))
            delta, m_new, v_new = _adamw(wv, g, mv, vv)
            outs[4 * w][...] = g
            outs[4 * w + 1][...] = delta
            outs[4 * w + 2][...] = m_new
            outs[4 * w + 3][...] = v_new

    def rb(g):
        return g.shape[1] // ROW_BLOCKS

    in_specs = [pl.BlockSpec((None, rb(g), g.shape[2]), lambda i, s: (s[0], i, 0)) for g in grads]
    in_specs += [pl.BlockSpec((None, rb(g), g.shape[2]), lambda i, s: (s[1], i, 0)) for g in grads]
    in_specs += [pl.BlockSpec((3, rb(g), g.shape[2]), lambda i, s: (0, i, 0)) for g in grads]
    sec = lambda g: pl.BlockSpec((rb(g), g.shape[2]), lambda i, s: (i, 0))
    for w in range(nw):
        in_specs += [sec(grads[w])] * 3
    out_specs, out_shape = [], []
    for g in grads:
        out_specs += [sec(g)] * 4
        out_shape += [jax.ShapeDtypeStruct(g.shape[1:], F32)] * 4
    flat = [a for t in wmv for a in t]
    outs = pl.pallas_call(
        body, name="adam_sections",
        grid_spec=pltpu.PrefetchScalarGridSpec(num_scalar_prefetch=1, grid=(ROW_BLOCKS,), in_specs=in_specs,
                                               out_specs=tuple(out_specs)),
        out_shape=tuple(out_shape),
    )(sec_idx, *grads, *recv_sib, *recv_chips, *flat)
    return [outs[4 * w:4 * w + 4] for w in range(nw)]


def _adam_small(sec_idx, gpack, parts, parts_wmv, vec_params, mat_params, conv_wmv):
    items = [(0, parts_wmv, "parts")]
    items += [(r0, t, "vec") for r0, t in vec_params] + [(r0, t, "mat") for r0, t in mat_params]
    items.append((_SEQ_ROWS["dcw"][0], conv_wmv, "conv"))

    def body(idx_ref, g_ref, parts_ref, *refs):
        ins, outs = refs[:3 * len(items)], refs[3 * len(items):]
        for n, (r0, _, kind) in enumerate(items):
            w_ref, m_ref, v_ref = ins[3 * n:3 * n + 3]
            o = outs[4 * n:4 * n + 4]
            if kind == "mat":
                g = g_ref[pl.ds(r0, w_ref.shape[0]), :]
                res = (g,) + _adamw(w_ref[...], g, m_ref[...], v_ref[...])
                for ref, val in zip(o, res):
                    ref[...] = val
            elif kind in ("vec", "parts"):
                for h in range(w_ref.shape[1] // HEAD):
                    cols = slice(h * HEAD, (h + 1) * HEAD)
                    if kind == "vec":
                        g = g_ref[pl.ds(r0 + h, 1), :]
                    else:
                        g = parts_ref[pl.ds(h, 1), :]
                        for dev in range(1, N_DEV):
                            g = g + parts_ref[pl.ds(SUB * dev + h, 1), :]
                    res = (g,) + _adamw(w_ref[:, cols], g, m_ref[:, cols], v_ref[:, cols])
                    for ref, val in zip(o, res):
                        ref[:, cols] = val
            else:
                rows = lax.broadcasted_iota(jnp.int32, (SUB, HEAD), 0)
                for k in range(4):
                    blk = g_ref[pl.ds(r0 + SUB * k, SUB), :]
                    g = jnp.sum(jnp.where(rows == idx_ref[0], blk, 0.0), axis=0, keepdims=True)
                    row = pl.ds(k, 1)
                    res = (g,) + _adamw(w_ref[row, :], g, m_ref[row, :], v_ref[row, :])
                    for ref, val in zip(o, res):
                        ref[row, :] = val
        outs[4 * len(items)][...] = g_ref[pl.ds(_OFF_MID + MID_LOSS_ROW, 1), :]

    flat = [a for _, t, _ in items for a in t]
    full = lambda a: pl.BlockSpec(a.shape, lambda i, s: (0,) * a.ndim)
    out_specs, out_shape = [], []
    for _, t, _ in items:
        out_specs += [full(t[0])] * 4
        out_shape += [jax.ShapeDtypeStruct(t[0].shape, F32)] * 4
    loss_row = jax.ShapeDtypeStruct((1, HEAD), F32)
    out_specs.append(full(loss_row))
    out_shape.append(loss_row)
    outs = pl.pallas_call(
        body, name="adam_small",
        grid_spec=pltpu.PrefetchScalarGridSpec(num_scalar_prefetch=1, grid=(1,),
                                               in_specs=[full(gpack), full(parts)] + [full(a) for a in flat],
                                               out_specs=tuple(out_specs)),
        out_shape=tuple(out_shape),
    )(sec_idx, gpack, parts, *flat)
    return [outs[4 * n:4 * n + 4] for n in range(len(items))], outs[4 * len(items)]


def _col_slicer(width):
    return lambda ref, idx: ref.at[:, pl.ds(pl.multiple_of(idx * width, HEAD), width)]


def _row_slicer(rows):
    return lambda ref, idx: ref.at[pl.ds(pl.multiple_of(idx * rows, SUB), rows), :]


def kernel(x, p, norm_g, w_in, conv_w, conv_b, lru_w_a, lru_b_a, lru_w_x, lru_b_x, lru_lambda, pool_w, pool_scale, w_proj_lru, w_proj_pool, w_out, ple_norm_g, w_ple_gate, w_ple_proj, final_g, loss_target, m_norm_g, m_w_in, m_conv_w, m_conv_b, m_lru_w_a, m_lru_b_a, m_lru_w_x, m_lru_b_x, m_lru_lambda, m_pool_w, m_pool_scale, m_w_proj_lru, m_w_proj_pool, m_w_out, m_ple_norm_g, m_w_ple_gate, m_w_ple_proj, m_final_g, v_norm_g, v_w_in, v_conv_w, v_conv_b, v_lru_w_a, v_lru_b_a, v_lru_w_x, v_lru_b_x, v_lru_lambda, v_pool_w, v_pool_scale, v_w_proj_lru, v_w_proj_pool, v_w_out, v_ple_norm_g, v_w_ple_gate, v_w_ple_proj, v_final_g):
    nb, s, d = x.shape
    t = nb * s
    tm = min(TILE_M, s)
    sec_idx = (4 * lax.axis_index("x") + 2 * lax.axis_index("y") + lax.axis_index("c")).astype(jnp.int32)
    chip_idx = (2 * lax.axis_index("x") + lax.axis_index("y")).astype(jnp.int32)
    core_idx = lax.axis_index("c").astype(jnp.int32)

    h, ht, win, cw = _pre(
        x, norm_g, [w_in[0], conv_w[0]], [True, False],
        [(d, N_DEV * w_in.shape[2]), (conv_w.shape[1], N_DEV * conv_w.shape[2])],
        [_col_slicer(w_in.shape[2]), _col_slicer(conv_w.shape[2])], tm)
    late_shards = [w_proj_lru[0], w_proj_pool[0], w_out[0], w_ple_gate[0], w_ple_proj[0]]
    late_shapes = [(N_DEV * w_proj_lru.shape[1], d), (w_proj_pool.shape[1], N_DEV * w_proj_pool.shape[2]),
                   (N_DEV * w_out.shape[1], d), (N_DEV * w_ple_gate.shape[1], d),
                   (w_ple_proj.shape[1], N_DEV * w_ple_proj.shape[2])]
    late_slicers = [_row_slicer(w_proj_lru.shape[1]), _col_slicer(w_proj_pool.shape[2]), _row_slicer(w_out.shape[1]),
                    _row_slicer(w_ple_gate.shape[1]), _col_slicer(w_ple_proj.shape[2])]

    lw = cw.shape[1]
    pwid = pool_scale.shape[1]
    wa = lru_w_a[0].astype(_MM)
    wx = lru_w_x[0].astype(_MM)
    pw = pool_w[0].astype(_MM)
    ba, bx = lru_b_a[0], lru_b_x[0]
    gf = final_g.reshape(1, d)
    core = core_idx.reshape(1)

    fwd_out = _fwd_seq(
        h, win, cw, conv_b, wa, ba, wx, bx, lru_lambda, pw, pool_scale, late_shards, late_shapes, late_slicers, tm)
    z, hl, ya, yb = fwd_out[:4]
    saved = fwd_out[4:N_SEQ_SAVED]
    wpl, wpp, wout, wpg, wpe = fwd_out[N_SEQ_SAVED:]
    x2d = x.reshape(t, d)
    mid_out = _mid(
        ya.reshape(t, lw), yb.reshape(t, pwid), h.reshape(t, d), win, x2d, p[0].reshape(t, -1),
        loss_target.reshape(t, d), wpl, wpp, wout, wpg, wpe, ple_norm_g, gf, tm)
    dya, dyb, dm, dx1 = mid_out[:4]
    grads_mid, pack_mid, sib_mid = list(mid_out[4:9]), mid_out[9], list(mid_out[10:])
    sums_mid = _chip_sums(core, grads_mid, sib_mid, [_WIRE] * len(grads_mid), "chip_sums_mid")
    seq_out = _bwd_seq(z, hl, saved, dya.reshape(nb, s, lw), dyb.reshape(nb, s, pwid), dm.reshape(nb, s, -1), cw,
                       wa, wx, lru_lambda, pw, pool_scale, pack_mid, sums_mid, tm)
    dz, chips_mid = seq_out[0], list(seq_out[2:])
    small = seq_out[1].reshape(N_DEV, SMALL_SEC, HEAD)
    g_in, sib_in, _, _, gsmall = _bwd_win(ht, dz, small)
    (sums_in,) = _chip_sums(core, [g_in], [sib_in], [_WIRE], "chip_sums_in")
    grad_x, chips_in, g0_parts = _bwd_dx(dz.reshape(t, -1), x2d, dx1, win, norm_g, [sums_in], tm)

    grads = [g_in] + grads_mid
    recv_sib = [sib_in] + sib_mid
    recv_chips = [chips_in] + chips_mid
    wmv = [(w_in[0], m_w_in[0], v_w_in[0]), (w_proj_lru[0], m_w_proj_lru[0], v_w_proj_lru[0]),
           (w_proj_pool[0], m_w_proj_pool[0], v_w_proj_pool[0]), (w_out[0], m_w_out[0], v_w_out[0]),
           (w_ple_gate[0], m_w_ple_gate[0], v_w_ple_gate[0]), (w_ple_proj[0], m_w_ple_proj[0], v_w_ple_proj[0])]
    big = _adam_sections(jnp.stack([sec_idx, chip_idx]), grads, recv_sib, recv_chips, wmv)

    R = _SEQ_ROWS
    vec = lambda r0, *t: (r0, tuple(a.reshape(1, -1) for a in t))
    mat = lambda r0, *t: (r0, tuple(a.reshape(-1, HEAD) for a in t))
    norm_wmv = vec(0, norm_g, m_norm_g, v_norm_g)[1]
    vec_params = [vec(R["dcb"][0], conv_b, m_conv_b, v_conv_b),
                  vec(R["dlam"][0], lru_lambda, m_lru_lambda, v_lru_lambda),
                  vec(R["dps"][0], pool_scale, m_pool_scale, v_pool_scale),
                  vec(_OFF_MID, ple_norm_g, m_ple_norm_g, v_ple_norm_g),
                  vec(_OFF_MID + SUB, final_g, m_final_g, v_final_g)]
    mat_params = [mat(R["dwa"][0], lru_w_a, m_lru_w_a, v_lru_w_a), mat(R["dba"][0], lru_b_a, m_lru_b_a, v_lru_b_a),
                  mat(R["dwx"][0], lru_w_x, m_lru_w_x, v_lru_w_x), mat(R["dbx"][0], lru_b_x, m_lru_b_x, v_lru_b_x),
                  mat(R["dpw"][0], pool_w, m_pool_w, v_pool_w)]
    conv_wmv = (conv_w[0], m_conv_w[0], v_conv_w[0])
    small_out, loss_row = _adam_small(sec_idx.reshape(1), gsmall, g0_parts, norm_wmv, vec_params, mat_params, conv_wmv)
    loss = loss_row[0, 0]

    res = {}
    names_small = ["norm_g", "conv_b", "lru_lambda", "pool_scale", "ple_norm_g", "final_g",
                   "lru_w_a", "lru_b_a", "lru_w_x", "lru_b_x", "pool_w", "conv_w"]
    shapes = {"norm_g": norm_g, "conv_b": conv_b, "lru_lambda": lru_lambda, "pool_scale": pool_scale,
              "ple_norm_g": ple_norm_g, "final_g": final_g, "lru_w_a": lru_w_a, "lru_b_a": lru_b_a, "lru_w_x": lru_w_x,
              "lru_b_x": lru_b_x, "pool_w": pool_w, "conv_w": conv_w}
    for name, quad in zip(names_small, small_out):
        res[name] = [a.reshape(shapes[name].shape) for a in quad]
    names_big = ["w_in", "w_proj_lru", "w_proj_pool", "w_out", "w_ple_gate", "w_ple_proj"]
    for name, quad, (w, _, _) in zip(names_big, big, wmv):
        res[name] = [a.reshape((1,) + w.shape) for a in quad]
    order = ["norm_g", "w_in", "conv_w", "conv_b", "lru_w_a", "lru_b_a", "lru_w_x", "lru_b_x", "lru_lambda", "pool_w",
             "pool_scale", "w_proj_lru", "w_proj_pool", "w_out", "ple_norm_g", "w_ple_gate", "w_ple_proj", "final_g"]
    out = [loss, grad_x.reshape(nb, s, d)]
    for kind in range(4):
        out += [res[name][kind] for name in order]
    return tuple(out)
```

```python
import jax
import jax.numpy as jnp
from jax import lax
from jax.experimental import pallas as pl
from jax.experimental.pallas import tpu as pltpu

F32 = jnp.float32
_MM = jnp.bfloat16
_WIRE = jnp.bfloat16

EPS = 1e-6
LRU_C = 8.0
MULT_SQ_FLOOR = 1e-30
POOL_WINDOWS = (2, 4, 8, 16)
N_HEADS = 8
HEAD = 128
HALO = 16
SUB = 8

ADAM_LR = 0.001
ADAM_B1 = 0.9
ADAM_B2 = 0.999
ADAM_EPS = 1e-08
ADAM_WD = 0.01
ADAM_STEP = 10

N_DEV = 8
MESH = pl.DeviceIdType.MESH
VMEM_LIMIT = 60 * 1024 * 1024
TILE_M = 256
N_SEQ_SAVED = 12

_SEQ_ROWS = {"dwa": (0, 1024), "dwx": (1024, 1024), "dpw": (2048, 512), "dba": (2560, 8), "dbx": (2568, 8),
             "dcb": (2576, 8), "dlam": (2584, 8), "dcw": (2592, 32), "dps": (2624, 8)}
SEQ_PACK_ROWS = 2632
MID_PACK_ROWS = 24
MID_LOSS_ROW = 16
INP_PACK_ROWS = 8
SMALL_ROWS = 3072
SMALL_SEC = SMALL_ROWS // N_DEV
_OFF_MID = SEQ_PACK_ROWS


def _dot(a, b):
    return jnp.dot(a, b, preferred_element_type=F32)


def _dot_nt(a, b):
    return lax.dot_general(a, b, (((1,), (1,)), ((), ())), preferred_element_type=F32)


def _dot_tn(a, b):
    return lax.dot_general(a, b, (((0,), (0,)), ((), ())), preferred_element_type=F32)


def _sigmoid(v):
    return 0.5 * jnp.tanh(0.5 * v) + 0.5


def _lru_mult(log_a, a):
    m2 = jnp.maximum(-jnp.tanh(log_a) * (1.0 + a * a), MULT_SQ_FLOOR)
    inv = lax.rsqrt(m2)
    return m2 * inv, inv


def _rows_back(v, n):
    return pltpu.roll(v, n, 0) if n else v


def _rows_ahead(v, n):
    return pltpu.roll(v, v.shape[0] - n, 0) if n else v


def _softplus_neg(lam):
    e = jnp.exp(-jnp.abs(lam))
    w = 1.0 + e
    l1p = jnp.where(w == 1.0, e, jnp.log(w) * (e / (w - 1.0)))
    return jnp.maximum(-lam, 0.0) + l1p


def _rsqrt_mean_sq(v):
    return lax.rsqrt(jnp.mean(v * v, axis=-1, keepdims=True) + EPS)


def _colsum(v):
    return jnp.sum(v, axis=0, keepdims=True)


def _const_spec(shape):
    nd = len(shape)
    return pl.BlockSpec(shape, lambda *_: (0,) * nd, pipeline_mode=pl.Buffered(1))


HBM_SPEC = pl.BlockSpec(memory_space=pl.ANY)


def _my_pos():
    return lax.axis_index("x"), lax.axis_index("y"), lax.axis_index("c")


def _other_chips(x, y):
    return [(1 - x, y), (x, 1 - y), (1 - x, 1 - y)]


def _gather_sems(nw):
    return [pltpu.SemaphoreType.DMA((nw, 7)), pltpu.SemaphoreType.DMA((nw, 7)), pltpu.SemaphoreType.DMA((nw,))]


def _gather_phases(srcs, outs, slicers, send_sems, recv_sems, local_sems, relay_diagonal=False):
    n_direct = 2 if relay_diagonal else 3
    nw = len(srcs)
    x, y, c = _my_pos()
    me, sibling = (x, y, c), (x, y, 1 - c)
    chips = _other_chips(x, y)

    def part(w, pos):
        return slicers[w](outs[w], 4 * pos[0] + 2 * pos[1] + pos[2])

    def copy(w, k, block, to, src=None):
        return pltpu.make_async_remote_copy(
            src_ref=part(w, block) if src is None else src, dst_ref=part(w, block),
            send_sem=send_sems.at[w, k], recv_sem=recv_sems.at[w, k], device_id=to, device_id_type=MESH)

    def mine():
        return [pltpu.make_async_copy(srcs[w], part(w, me), local_sems.at[w]) for w in range(nw)]

    def first():
        out = []
        for w in range(nw):
            out.append(copy(w, 0, me, sibling, src=srcs[w]))
            out += [copy(w, 1 + j, me, (*chips[j], c), src=srcs[w]) for j in range(n_direct)]
        return out

    def relay(w, j):
        return copy(w, 3, (*chips[j], c), (*chips[1 - j], c))

    def relays():
        return [relay(w, 0) for w in range(nw)] if relay_diagonal else []

    def passed():
        return [copy(w, 4 + j, (*chip, c), sibling) for j, chip in enumerate(chips) for w in range(nw)]

    def start():
        for cp in mine() + first():
            cp.start()

    def forward():
        for j, chip in enumerate(chips):
            for w in range(nw):
                copy(w, 1 + j, (*chip, c), me).wait_recv()
                copy(w, 4 + j, (*chip, c), sibling).start()
            if relay_diagonal and j < 2:
                @pl.when(c == j)
                def _():
                    for w in range(nw):
                        relay(w, j).start()

    def finish():
        for w in range(nw):
            copy(w, 0, sibling, me).wait_recv()
            for j, chip in enumerate(chips):
                copy(w, 4 + j, (*chip, 1 - c), me).wait_recv()
        for cp in first() + relays() + passed():
            cp.wait_send()
        for cp in mine():
            cp.wait()

    return start, forward, finish


def _sibling_copies(srcs, outs, send_sems, recv_sems):
    x, y, c = _my_pos()
    return [pltpu.make_async_remote_copy(
        src_ref=srcs[w].at[2 * q + (1 - c)], dst_ref=outs[w].at[q], send_sem=send_sems.at[w, q],
        recv_sem=recv_sems.at[w, q], device_id=(x, y, 1 - c), device_id_type=MESH)
        for w in range(len(srcs)) for q in range(4)]


def _chip_copies(srcs, outs, send_sems, recv_sems, which=None):
    x, y, c = _my_pos()
    return [pltpu.make_async_remote_copy(
        src_ref=srcs[w].at[2 * px + py], dst_ref=outs[w].at[k], send_sem=send_sems.at[w, k],
        recv_sem=recv_sems.at[w, k], device_id=(px, py, c), device_id_type=MESH)
        for w in (range(len(srcs)) if which is None else which) for k, (px, py) in enumerate(_other_chips(x, y))]


def _lru_gates(xc, h, wa_ref, ba_ref, wx_ref, bx_ref, neg_c_sp):
    xc16 = xc.astype(_MM)
    r = _sigmoid(_dot(xc16, wa_ref[h]) + ba_ref[pl.ds(h, 1), :])
    i = _sigmoid(_dot(xc16, wx_ref[h]) + bx_ref[pl.ds(h, 1), :])
    return r, i, neg_c_sp * r


def _conv_head(xa_ext, cw_ref, cb_ref, cols, tm):
    ext = xa_ext[:, cols]
    xc = cb_ref[:, cols] + cw_ref[pl.ds(3, 1), cols] * ext[HALO:HALO + tm]
    for k in range(3):
        xc = xc + cw_ref[pl.ds(k, 1), cols] * _rows_back(ext, 3 - k)[HALO:HALO + tm]
    return xc


def _window_sum(ext, k, shift):
    n = 1
    while n < k:
        ext = ext + shift(ext, n)
        n *= 2
    return ext


def _pool_diff(xb_ext, g, k, pos, tm):
    cols = slice(g * HEAD, (g + 1) * HEAD)
    ext = xb_ext[:, cols]
    ws = _window_sum(ext, k, _rows_back)[HALO:HALO + tm]
    inv_cnt = 1.0 / jnp.minimum(pos + 1, k).astype(F32)
    return ws * inv_cnt - ext[HALO:HALO + tm], inv_cnt


def _staged_sources(shards, as_operand):
    in_specs = [_const_spec(a.shape) if c else HBM_SPEC for a, c in zip(shards, as_operand)]
    stages = [pltpu.VMEM(a.shape, _MM) for a, c in zip(shards, as_operand) if c]
    dtypes = [_MM if c else a.dtype for a, c in zip(shards, as_operand)]
    return in_specs, stages, dtypes


def _gather_sources(srcs, stages, as_operand):
    stages = iter(stages)
    return [next(stages) if c else src for src, c in zip(srcs, as_operand)]


def _fill_stages(srcs, staged, as_operand):
    for src, dst, c in zip(srcs, staged, as_operand):
        if c:
            dst[...] = src[...].astype(dst.dtype)


def _pre(x, g0, shards, as_operand, shapes, slicers, tm):
    nb, s, d = x.shape
    nt = s // tm
    nl = len(shards)
    shard_specs, stage_shapes, wire = _staged_sources(shards, as_operand)

    def body(x_ref, g0_ref, *rest):
        src, rest = rest[:nl], rest[nl:]
        h_ref, ht_ref = rest[:2]
        out, rest = rest[2:2 + nl], rest[2 + nl:]
        (send_sems, recv_sems, local_sems), stages = rest[:3], rest[3:]
        step_no = pl.program_id(0) * nt + pl.program_id(1)
        staged = _gather_sources(src, stages, as_operand)
        g_start, g_forward, g_finish = _gather_phases(staged, out, slicers, send_sems, recv_sems, local_sems,
                                                      relay_diagonal=True)

        @pl.when(step_no == 0)
        def _():
            _fill_stages(src, staged, as_operand)
            g_start()

        xv = x_ref[...]
        h16 = ((xv * _rsqrt_mean_sq(xv)) * g0_ref[...]).astype(_MM)
        h_ref[...] = h16
        ht_ref[...] = h16.T

        @pl.when(step_no == nb * nt - 1)
        def _():
            g_forward()
            g_finish()

    tile = pl.BlockSpec((None, tm, d), lambda b, j: (b, j, 0))
    return pl.pallas_call(
        body, name="pre", grid=(nb, nt),
        in_specs=[tile, _const_spec(g0.shape)] + shard_specs,
        out_specs=(tile, pl.BlockSpec((None, d, tm), lambda b, j: (b, 0, j))) + (HBM_SPEC,) * nl,
        out_shape=(jax.ShapeDtypeStruct((nb, s, d), _MM), jax.ShapeDtypeStruct((nb, d, s), _MM))
        + tuple(jax.ShapeDtypeStruct(shp, dt) for shp, dt in zip(shapes, wire)),
        scratch_shapes=_gather_sems(nl) + stage_shapes,
        compiler_params=pltpu.CompilerParams(dimension_semantics=("arbitrary", "arbitrary")),
    )(x, g0, *shards)


def _fwd_seq(h, win, cw, cb, wa, ba, wx, bx, lam, pw, ps, late_shards, late_shapes, late_slicers, tm):
    nb, s, d = h.shape
    lw = cw.shape[1]
    pwid = ps.shape[1]
    nc = 2 * lw + 2 * pwid
    nt = s // tm
    nl = len(late_shards)
    n_steps = nb * nt
    as_operand = [True] * nl
    shard_specs, stage_shapes, wire = _staged_sources(late_shards, as_operand)

    def body(h_ref, win_ref, cw_ref, cb_ref, wa_ref, ba_ref, wx_ref, bx_ref, lam_ref, pw_ref, ps_ref, *rest):
        late_src, rest = rest[:nl], rest[nl:]
        (z_ref, hl_ref, ya_ref, yb_ref, xc_ref, r_ref, i_ref, a_buf, mult_ref, imult_ref, diff_ref,
         yp_ref) = rest[:N_SEQ_SAVED]
        late_out, rest = rest[N_SEQ_SAVED:N_SEQ_SAVED + nl], rest[N_SEQ_SAVED + nl:]
        (xa_ext, xb_ext, carry, send_sems, recv_sems, local_sems), stages = rest[:6], rest[6:]
        j = pl.program_id(1)
        step_no = pl.program_id(0) * nt + j
        staged = _gather_sources(late_src, stages, as_operand)
        g_start, g_forward, g_finish = _gather_phases(staged, late_out, late_slicers, send_sems, recv_sems, local_sems)

        @pl.when(step_no == 0)
        def _():
            _fill_stages(late_src, staged, as_operand)
            g_start()

        pl.when(step_no == (3 * n_steps) // 4)(g_forward)

        @pl.when(j == 0)
        def _():
            xa_ext[pl.ds(0, HALO), :] = jnp.zeros((HALO, lw), F32)
            xb_ext[pl.ds(0, HALO), :] = jnp.zeros((HALO, pwid), F32)
            carry[...] = jnp.zeros_like(carry)

        h16 = h_ref[...]
        xa = _dot(h16, win_ref[:, 0:lw])
        z_ref[:, 0:lw] = xa
        xa_ext[pl.ds(HALO, tm), :] = xa
        for c0 in range(lw, nc, lw):
            z_ref[:, c0:c0 + lw] = _dot(h16, win_ref[:, c0:c0 + lw])

        for h in range(N_HEADS):
            cols = slice(h * HEAD, (h + 1) * HEAD)
            xc = _conv_head(xa_ext, cw_ref, cb_ref, cols, tm)
            neg_c_sp = -LRU_C * _softplus_neg(lam_ref[:, cols])
            r, i, log_a = _lru_gates(xc, h, wa_ref, ba_ref, wx_ref, bx_ref, neg_c_sp)
            a = jnp.exp(log_a)
            xc_ref[:, cols] = xc
            r_ref[:, cols] = r
            i_ref[:, cols] = i
            a_buf[:, cols] = a
            mult, inv_mult = _lru_mult(log_a, a)
            mult_ref[:, cols] = mult
            imult_ref[:, cols] = inv_mult
            hl_ref[:, cols] = mult * (i * xc)

        rows = lax.broadcasted_iota(jnp.int32, (SUB, lw), 0)

        def step(c, car):
            i0 = pl.multiple_of(c * SUB, SUB)
            av = a_buf[pl.ds(i0, SUB), :]
            bv = hl_ref[pl.ds(i0, SUB), :]
            for sh in (1, 2, 4):
                m = rows >= sh
                a_sh = jnp.where(m, pltpu.roll(av, sh, 0), 1.0)
                b_sh = jnp.where(m, pltpu.roll(bv, sh, 0), 0.0)
                bv = av * b_sh + bv
                av = av * a_sh
            hv = av * car + bv
            hl_ref[pl.ds(i0, SUB), :] = hv
            return jnp.broadcast_to(hv[SUB - 1:SUB, :], (SUB, lw))

        carry[...] = lax.fori_loop(0, tm // SUB, step, carry[...])

        ga = z_ref[:, lw:2 * lw]
        ya_ref[...] = (hl_ref[...] * (ga * _sigmoid(ga))).astype(ya_ref.dtype)

        xb_ext[pl.ds(HALO, tm), :] = z_ref[:, 2 * lw:2 * lw + pwid]
        pos = j * tm + lax.broadcasted_iota(jnp.int32, (tm, HEAD), 0)
        for g, k in enumerate(POOL_WINDOWS):
            cols = slice(g * HEAD, (g + 1) * HEAD)
            diff16 = _pool_diff(xb_ext, g, k, pos, tm)[0].astype(_MM)
            yp = _dot(diff16, pw_ref[g])
            diff_ref[:, cols] = diff16
            yp_ref[:, cols] = yp
            gb = z_ref[:, 2 * lw + pwid + g * HEAD:2 * lw + pwid + (g + 1) * HEAD]
            yb_ref[:, cols] = ((yp * ps_ref[:, cols]) * (gb * _sigmoid(gb))).astype(yb_ref.dtype)

        xa_ext[pl.ds(0, HALO), :] = xa_ext[pl.ds(tm, HALO), :]
        xb_ext[pl.ds(0, HALO), :] = xb_ext[pl.ds(tm, HALO), :]
        pl.when(step_no == n_steps - 1)(g_finish)

    tile = lambda w: pl.BlockSpec((None, tm, w), lambda b, j: (b, j, 0))
    return pl.pallas_call(
        body, name="fwd_seq", grid=(nb, nt),
        in_specs=[tile(d), _const_spec((d, nc)), _const_spec(cw.shape), _const_spec(cb.shape),
                  _const_spec(wa.shape), _const_spec(ba.shape), _const_spec(wx.shape), _const_spec(bx.shape),
                  _const_spec(lam.shape), _const_spec(pw.shape), _const_spec(ps.shape)] + shard_specs,
        out_specs=(tile(nc), tile(lw), tile(lw), tile(pwid)) + (tile(lw),) * 6 + (tile(pwid), tile(pwid))
        + (HBM_SPEC,) * nl,
        out_shape=(jax.ShapeDtypeStruct((nb, s, nc), F32), jax.ShapeDtypeStruct((nb, s, lw), F32),
                   jax.ShapeDtypeStruct((nb, s, lw), _MM), jax.ShapeDtypeStruct((nb, s, pwid), _MM))
        + (jax.ShapeDtypeStruct((nb, s, lw), F32),) * 6
        + (jax.ShapeDtypeStruct((nb, s, pwid), _MM), jax.ShapeDtypeStruct((nb, s, pwid), F32))
        + tuple(jax.ShapeDtypeStruct(shp, dt) for shp, dt in zip(late_shapes, wire)),
        scratch_shapes=[pltpu.VMEM((tm + HALO, lw), F32), pltpu.VMEM((tm + HALO, pwid), F32),
                        pltpu.VMEM((SUB, lw), F32)] + _gather_sems(nl) + stage_shapes,
        compiler_params=pltpu.CompilerParams(dimension_semantics=("arbitrary", "arbitrary"), vmem_limit_bytes=VMEM_LIMIT),
    )(h, win, cw, cb, wa, ba, wx, bx, lam, pw, ps, *late_shards)


def _store_chunks(ref, row0, vec):
    for h in range(vec.shape[1] // HEAD):
        ref[pl.ds(row0 + h, 1), :] += vec[:, h * HEAD:(h + 1) * HEAD]


def _mid(ya, yb, h, win, x, p, tgt, wpl, wpp, wout, wpg, wpe, g1, gf, tm):
    t, d = x.shape
    lw = ya.shape[1]
    pwid = yb.shape[1]
    pdim = p.shape[1]
    n = t // tm
    sec_pp = wpp.shape[1] // N_DEV
    sec_pe = wpe.shape[1] // N_DEV
    sec_rows = wpl.shape[0] // N_DEV
    grad_secs = [(sec_rows, d), (wpp.shape[0], sec_pp), (sec_rows, d), (sec_rows, d), (wpe.shape[0], sec_pe)]
    n_win_blocks = win.shape[1] // d
    win_cols = lambda c: pl.BlockSpec((d, d), lambda i: (0, c), pipeline_mode=pl.Buffered(1))

    def body(ya_ref, yb_ref, h_ref, wma_ref, wmb_ref, x_ref, p_ref, tgt_ref, wpl_ref, wpp_ref, wout_ref, wpg_ref, wpe_ref,
             g1_ref, gf_ref,
             dya_ref, dyb_ref, dm_ref, dx1_ref, gpl_ref, gpp_ref, gout_ref, gpg_ref, gpe_ref, pack_ref,
             sib_pl, sib_pp, sib_out, sib_pg, sib_pe,
             acc_pl, acc_pp, acc_out, acc_pg, acc_pe, out_sems, sib_send, sib_recv):
        i = pl.program_id(0)

        @pl.when(i == 0)
        def _():
            for acc in (acc_pl, acc_pp, acc_out, acc_pg, acc_pe):
                acc[...] = jnp.zeros_like(acc)
            pack_ref[...] = jnp.zeros_like(pack_ref)

        ya16 = ya_ref[...]
        yb16 = yb_ref[...]
        br_a = _dot(ya16, wpl_ref[...])
        br_b = _dot(yb16, wpp_ref[...])
        h16 = h_ref[...]
        sa = _sigmoid(_dot(h16, wma_ref[...]))
        sb = _sigmoid(_dot(h16, wmb_ref[...]))
        mg16 = (sa * br_a + sb * br_b).astype(_MM)
        x1 = x_ref[...] + _dot(mg16, wout_ref[...])
        r1 = _rsqrt_mean_sq(x1)
        n1 = x1 * r1
        h116 = (n1 * g1_ref[...]).astype(_MM)
        gate = _sigmoid(_dot(h116, wpg_ref[...]))
        p16 = p_ref[...].astype(_MM)
        pe = _dot(p16, wpe_ref[...])
        x2 = x1 + gate * pe
        r2 = _rsqrt_mean_sq(x2)
        n2 = x2 * r2
        err = n2 * gf_ref[...] - tgt_ref[...]
        sq = jnp.sum(_colsum(err * err), axis=1, keepdims=True)
        pack_ref[pl.ds(MID_LOSS_ROW, 1), :] += jnp.broadcast_to(sq * (0.5 / d), (1, HEAD))

        dy = err * (1.0 / d)
        _store_chunks(pack_ref, SUB, _colsum(dy * n2))
        dn2 = dy * gf_ref[...]
        dx2 = r2 * (dn2 - n2 * jnp.mean(dn2 * n2, axis=-1, keepdims=True))
        dpe16 = (dx2 * gate).astype(_MM)
        dpg16 = ((dx2 * pe) * (gate * (1.0 - gate))).astype(_MM)
        acc_pe[...] += _dot_tn(p16, dpe16)
        acc_pg[...] += _dot_tn(h116, dpg16)
        dh1 = _dot_nt(dpg16, wpg_ref[...])
        _store_chunks(pack_ref, 0, _colsum(dh1 * n1))
        dn1 = dh1 * g1_ref[...]
        dx1 = dx2 + r1 * (dn1 - n1 * jnp.mean(dn1 * n1, axis=-1, keepdims=True))
        dx1_ref[...] = dx1
        dx116 = dx1.astype(_MM)
        acc_out[...] += _dot_tn(mg16, dx116)
        dmg = _dot_nt(dx116, wout_ref[...])
        da16 = (dmg * sa).astype(_MM)
        db16 = (dmg * sb).astype(_MM)
        dm_ref[:, 0:d] = ((dmg * br_a) * (sa * (1.0 - sa))).astype(dm_ref.dtype)
        dm_ref[:, d:2 * d] = ((dmg * br_b) * (sb * (1.0 - sb))).astype(dm_ref.dtype)
        acc_pl[...] += _dot_tn(ya16, da16)
        acc_pp[...] += _dot_tn(yb16, db16)
        dya_ref[...] = _dot_nt(da16, wpl_ref[...])
        dyb_ref[...] = _dot_nt(db16, wpp_ref[...])

        @pl.when(i == n - 1)
        def _():
            writes = []
            for k in range(N_DEV):
                rows = pl.ds(k * sec_rows, sec_rows)
                for acc, out in ((acc_pl, gpl_ref), (acc_out, gout_ref), (acc_pg, gpg_ref)):
                    writes.append((acc.at[rows, :], out.at[k]))
                writes.append((acc_pp.at[:, pl.ds(k * sec_pp, sec_pp)], gpp_ref.at[k]))
                writes.append((acc_pe.at[:, pl.ds(k * sec_pe, sec_pe)], gpe_ref.at[k]))
            writes = [pltpu.make_async_copy(src, dst, out_sems.at[n]) for n, (src, dst) in enumerate(writes)]
            for cp in writes:
                cp.start()
            for cp in writes:
                cp.wait()
            to_sibling = _sibling_copies([gpl_ref, gpp_ref, gout_ref, gpg_ref, gpe_ref],
                                         [sib_pl, sib_pp, sib_out, sib_pg, sib_pe], sib_send, sib_recv)
            for cp in to_sibling:
                cp.start()
            for cp in to_sibling:
                cp.wait()

    tile = lambda w: pl.BlockSpec((tm, w), lambda i: (i, 0))
    any_spec = pl.BlockSpec(memory_space=pl.ANY)
    full = lambda shape: pl.BlockSpec(shape, lambda i: (0,) * len(shape))
    return pl.pallas_call(
        body, name="mid", grid=(n,),
        in_specs=[tile(lw), tile(pwid), tile(d), win_cols(n_win_blocks - 2), win_cols(n_win_blocks - 1),
                  tile(d), tile(pdim), tile(d),
                  _const_spec(wpl.shape), _const_spec(wpp.shape), _const_spec(wout.shape), _const_spec(wpg.shape),
                  _const_spec(wpe.shape), _const_spec(g1.shape), _const_spec(gf.shape)],
        out_specs=(tile(lw), tile(pwid), tile(2 * d), tile(d), any_spec, any_spec, any_spec, any_spec, any_spec,
                   full((MID_PACK_ROWS, HEAD))) + (any_spec,) * 5,
        out_shape=(jax.ShapeDtypeStruct((t, lw), F32), jax.ShapeDtypeStruct((t, pwid), F32),
                   jax.ShapeDtypeStruct((t, 2 * d), _MM), jax.ShapeDtypeStruct((t, d), F32),
                   *(jax.ShapeDtypeStruct((N_DEV,) + sec, F32) for sec in grad_secs),
                   jax.ShapeDtypeStruct((MID_PACK_ROWS, HEAD), F32),
                   *(jax.ShapeDtypeStruct((N_DEV // 2,) + sec, F32) for sec in grad_secs)),
        scratch_shapes=[pltpu.VMEM(wpl.shape, F32), pltpu.VMEM(wpp.shape, F32), pltpu.VMEM(wout.shape, F32),
                        pltpu.VMEM(wpg.shape, F32), pltpu.VMEM(wpe.shape, F32),
                        pltpu.SemaphoreType.DMA((5 * N_DEV,)), pltpu.SemaphoreType.DMA((5, N_DEV // 2)),
                        pltpu.SemaphoreType.DMA((5, N_DEV // 2))],
        compiler_params=pltpu.CompilerParams(dimension_semantics=("arbitrary",), vmem_limit_bytes=VMEM_LIMIT),
    )(ya, yb, h, win, win, x, p, tgt, wpl, wpp, wout, wpg, wpe, g1, gf)


def _bwd_seq(z, hl, saved, dya, dyb, dm, cw, wa, wx, lam, pw, ps, pack_mid, sib_grads, tm):
    nb, s, zw = z.shape
    lw = cw.shape[1]
    pwid = ps.shape[1]
    nt = s // tm
    nc = zw + dm.shape[2]
    R = _SEQ_ROWS
    ns = len(sib_grads)

    def body(zq_ref, hl_ref, hlh_ref, xc_buf, r_buf, i_buf, a_ref, mult_ref, imult_ref, diff_ref, yp_ref,
             dya_ref, dyb_ref, dm_ref,
             cw_ref, wa_ref, wx_ref, lam_ref, pw_ref, ps_ref, pack_mid_ref, *rest):
        sib_src, rest = rest[:ns], rest[ns:]
        dz_ref, pack_ref = rest[:2]
        sib_out, rest = rest[2:2 + ns], rest[2 + ns:]
        a_ext, an_buf, hl_ext, dh_buf, dxc_ext, q_ext, carry, send_sems, recv_sems = rest
        b = pl.program_id(0)
        j = pl.program_id(1)
        jr = nt - 1 - j
        has_prev = jr > 0

        @pl.when((b == 0) & (j == 0))
        def _():
            for cp in _chip_copies(sib_src, sib_out, send_sems, recv_sems):
                cp.start()

        dz_ref[:, zw:nc] = dm_ref[...]

        @pl.when((b == 0) & (j == 0))
        def _():
            pack_ref[...] = jnp.zeros_like(pack_ref)

        @pl.when(j == 0)
        def _():
            a_ext[pl.ds(tm, SUB), :] = jnp.zeros((SUB, lw), F32)
            dxc_ext[pl.ds(tm, HALO), :] = jnp.zeros((HALO, lw), F32)
            q_ext[pl.ds(tm, HALO), :] = jnp.zeros((HALO, pwid), F32)
            carry[...] = jnp.zeros_like(carry)

        hl_ext[pl.ds(0, SUB), :] = jnp.where(has_prev, hlh_ref[...], 0.0)
        hl_ext[pl.ds(SUB, tm), :] = hl_ref[...]
        a_ext[pl.ds(0, tm), :] = a_ref[...]
        for h in range(N_HEADS):
            cols = slice(h * HEAD, (h + 1) * HEAD)
            ga = zq_ref[:, lw + h * HEAD:lw + (h + 1) * HEAD]
            sg = _sigmoid(ga)
            dyav = dya_ref[:, cols]
            dh_buf[:, cols] = dyav * (ga * sg)
            dga = (dyav * hl_ext[pl.ds(SUB, tm), cols]) * (sg * (1.0 + ga * (1.0 - sg)))
            dz_ref[:, lw + h * HEAD:lw + (h + 1) * HEAD] = dga.astype(dz_ref.dtype)
        an_buf[...] = _rows_ahead(a_ext[...], 1)[0:tm]

        rows = lax.broadcasted_iota(jnp.int32, (SUB, lw), 0)
        nch = tm // SUB

        def step(c, car):
            i0 = pl.multiple_of((nch - 1 - c) * SUB, SUB)
            cv = an_buf[pl.ds(i0, SUB), :]
            bv = dh_buf[pl.ds(i0, SUB), :]
            for sh in (1, 2, 4):
                m = rows < SUB - sh
                c_sh = jnp.where(m, pltpu.roll(cv, SUB - sh, 0), 1.0)
                b_sh = jnp.where(m, pltpu.roll(bv, SUB - sh, 0), 0.0)
                bv = cv * b_sh + bv
                cv = cv * c_sh
            hv = cv * car + bv
            dh_buf[pl.ds(i0, SUB), :] = hv
            return jnp.broadcast_to(hv[0:1, :], (SUB, lw))

        carry[...] = lax.fori_loop(0, nch, step, carry[...])

        for h in range(N_HEADS):
            cols = slice(h * HEAD, (h + 1) * HEAD)
            lam_h = lam_ref[:, cols]
            neg_c_sp = -LRU_C * _softplus_neg(lam_h)
            xc = xc_buf[:, cols]
            r = r_buf[:, cols]
            i = i_buf[:, cols]
            a = a_ref[:, cols]
            mult = mult_ref[:, cols]
            inv_mult = imult_ref[:, cols]
            dh = dh_buf[:, cols]
            dhx = dh * xc
            di = dhx * mult
            h_prev = _rows_back(hl_ext[:, cols], 1)[SUB:SUB + tm]
            dlog_a = (dh * h_prev) * a - (dhx * i) * ((a * a) * inv_mult)
            pack_ref[pl.ds(R["dlam"][0] + h, 1), :] += _colsum(dlog_a * r) * (LRU_C * _sigmoid(-lam_h))
            dpa = (dlog_a * neg_c_sp) * (r * (1.0 - r))
            dpx = di * (i * (1.0 - i))
            dpa16 = dpa.astype(_MM)
            dpx16 = dpx.astype(_MM)
            xc16 = xc.astype(_MM)
            dxc = dh * (mult * i) + _dot_nt(dpa16, wa_ref[h]) + _dot_nt(dpx16, wx_ref[h])
            pack_ref[pl.ds(R["dwa"][0] + h * HEAD, HEAD), :] += _dot_tn(xc16, dpa16)
            pack_ref[pl.ds(R["dwx"][0] + h * HEAD, HEAD), :] += _dot_tn(xc16, dpx16)
            pack_ref[pl.ds(R["dba"][0] + h, 1), :] += _colsum(dpa)
            pack_ref[pl.ds(R["dbx"][0] + h, 1), :] += _colsum(dpx)
            pack_ref[pl.ds(R["dcb"][0] + h, 1), :] += _colsum(dxc)
            dxc_ext[pl.ds(0, tm), cols] = dxc

        for h in range(N_HEADS):
            cols = slice(h * HEAD, (h + 1) * HEAD)
            dxc_all = dxc_ext[:, cols]
            xa = zq_ref[:, cols]
            dxa = None
            for k in range(4):
                dxc_k = _rows_ahead(dxc_all, 3 - k)[0:tm]
                pack_ref[pl.ds(R["dcw"][0] + SUB * k + h, 1), :] += _colsum(dxc_k * xa)
                term = cw_ref[pl.ds(k, 1), cols] * dxc_k
                dxa = term if dxa is None else dxa + term
            dz_ref[:, cols] = dxa.astype(dz_ref.dtype)

        pos = jr * tm + lax.broadcasted_iota(jnp.int32, (tm, HEAD), 0)
        for g, k in enumerate(POOL_WINDOWS):
            cols = slice(g * HEAD, (g + 1) * HEAD)
            inv_cnt = 1.0 / jnp.minimum(pos + 1, k).astype(F32)
            diff16 = diff_ref[:, cols]
            yp = yp_ref[:, cols]
            sc = ps_ref[:, cols]
            gb = zq_ref[:, 2 * lw + pwid + g * HEAD:2 * lw + pwid + (g + 1) * HEAD]
            sgb = _sigmoid(gb)
            dybv = dyb_ref[:, cols]
            dy_pool = dybv * (gb * sgb)
            dgb = (dybv * (yp * sc)) * (sgb * (1.0 + gb * (1.0 - sgb)))
            pack_ref[pl.ds(R["dps"][0] + g, 1), :] += _colsum(dy_pool * yp)
            dyp16 = (dy_pool * sc).astype(_MM)
            pack_ref[pl.ds(R["dpw"][0] + g * HEAD, HEAD), :] += _dot_tn(diff16, dyp16)
            ddiff = _dot_nt(dyp16, pw_ref[g])
            q_ext[pl.ds(0, tm), cols] = ddiff * inv_cnt
            dxb = _window_sum(q_ext[:, cols], k, _rows_ahead)[0:tm] - ddiff
            dz_ref[:, 2 * lw + g * HEAD:2 * lw + (g + 1) * HEAD] = dxb.astype(dz_ref.dtype)
            dz_ref[:, 2 * lw + pwid + g * HEAD:2 * lw + pwid + (g + 1) * HEAD] = dgb.astype(dz_ref.dtype)

        a_ext[pl.ds(tm, SUB), :] = a_ext[pl.ds(0, SUB), :]
        dxc_ext[pl.ds(tm, HALO), :] = dxc_ext[pl.ds(0, HALO), :]
        q_ext[pl.ds(tm, HALO), :] = q_ext[pl.ds(0, HALO), :]

        @pl.when((b == nb - 1) & (j == nt - 1))
        def _():
            pack_ref[pl.ds(_OFF_MID, MID_PACK_ROWS), :] = pack_mid_ref[...]
            for cp in _chip_copies(sib_src, sib_out, send_sems, recv_sems):
                cp.wait()

    rev = lambda w: pl.BlockSpec((None, tm, w), lambda b, j: (b, nt - 1 - j, 0))
    prev_rows = lambda rows, w: pl.BlockSpec(
        (None, rows, w), lambda b, j: (b, jnp.maximum((nt - 1 - j) * (tm // rows) - 1, 0), 0))
    return pl.pallas_call(
        body, name="bwd_seq", grid=(nb, nt),
        in_specs=[rev(zw), rev(lw), prev_rows(SUB, lw)] + [rev(lw)] * 6 + [rev(pwid), rev(pwid),
                  rev(lw), rev(pwid), rev(nc - zw),
                  _const_spec(cw.shape), _const_spec(wa.shape), _const_spec(wx.shape), _const_spec(lam.shape),
                  _const_spec(pw.shape), _const_spec(ps.shape), _const_spec(pack_mid.shape)] + [HBM_SPEC] * ns,
        out_specs=(rev(nc), pl.BlockSpec((SMALL_ROWS, HEAD), lambda b, j: (0, 0))) + (HBM_SPEC,) * ns,
        out_shape=(jax.ShapeDtypeStruct((nb, s, nc), _MM), jax.ShapeDtypeStruct((SMALL_ROWS, HEAD), F32))
        + tuple(jax.ShapeDtypeStruct((3,) + g.shape[1:], g.dtype) for g in sib_grads),
        scratch_shapes=[pltpu.VMEM((tm + SUB, lw), F32), pltpu.VMEM((tm, lw), F32), pltpu.VMEM((tm + SUB, lw), F32),
                        pltpu.VMEM((tm, lw), F32), pltpu.VMEM((tm + HALO, lw), F32),
                        pltpu.VMEM((tm + HALO, pwid), F32), pltpu.VMEM((SUB, lw), F32),
                        pltpu.SemaphoreType.DMA((ns, 3)), pltpu.SemaphoreType.DMA((ns, 3))],
        compiler_params=pltpu.CompilerParams(dimension_semantics=("arbitrary", "arbitrary"), vmem_limit_bytes=VMEM_LIMIT),
    )(z, hl, hl, *saved, dya, dyb, dm, cw, wa, wx, lam, pw, ps, pack_mid, *sib_grads)


def _bwd_win(ht, dz, small):
    nb, d, s = ht.shape
    nc = dz.shape[2]
    sec = nc // N_DEV
    n_chips = N_DEV // 2

    def body(ht_ref, dz_ref, small_src, gwin_ref, recv_ref, small_out, small_chips_out, gsmall_ref,
             acc, own4, sib4, csum, got3, total, local_sems, sib_send, sib_recv, small_send, small_recv,
             sc_send, sc_recv, ld_sems, g_send, g_recv, g_local):
        q = pl.program_id(0)
        b = pl.program_id(1)
        slot = q % 2
        x, y, c = _my_pos()

        def half(sl, core):
            return acc.at[sl, :, pl.ds(pl.multiple_of(core * sec, HEAD), sec)]

        def local_copy(qq, sl):
            return pltpu.make_async_copy(half(sl, c), gwin_ref.at[2 * qq + c], local_sems.at[sl])

        def sib_copy(qq, sl):
            return pltpu.make_async_remote_copy(
                src_ref=half(sl, 1 - c), dst_ref=recv_ref.at[qq], send_sem=sib_send.at[qq],
                recv_sem=sib_recv.at[qq], device_id=(x, y, 1 - c), device_id_type=MESH)

        step_no = q * nb + b
        n_steps = n_chips * nb
        sum_step = min(3, n_steps - 1)
        fwd_step = max(sum_step, min(6, n_steps - 1))
        small_sib = lambda: _sibling_copies([small_src], [small_out], small_send, small_recv)
        small_chip = lambda: _chip_copies([csum], [small_chips_out], sc_send, sc_recv)
        g_start, g_forward, g_finish = _gather_phases([total], [gsmall_ref], [_row_slicer(small.shape[1])],
                                                      g_send, g_recv, g_local)

        def load(srcs, dst):
            cps = [pltpu.make_async_copy(src, dst.at[k], ld_sems.at[k]) for k, src in enumerate(srcs)]
            for cp in cps:
                cp.start()
            for cp in cps:
                cp.wait()

        @pl.when(step_no == 0)
        def _():
            for cp in small_sib():
                cp.start()

        @pl.when(step_no == 1)
        def _():
            for cp in small_sib():
                cp.wait_recv()
            load([small_src.at[2 * qq + c] for qq in range(n_chips)], own4)
            load([small_out.at[qq] for qq in range(n_chips)], sib4)
            csum[...] = own4[...] + sib4[...]
            for cp in small_chip():
                cp.start()

        @pl.when(step_no == sum_step)
        def _():
            for cp in small_chip():
                cp.wait_recv()
            load([small_chips_out.at[k] for k in range(3)], got3)
            total[...] = ((csum[2 * x + y] + got3[0]) + got3[1]) + got3[2]
            g_start()

        pl.when(step_no == fwd_step)(g_forward)

        @pl.when((q >= 2) & (b == 0))
        def _():
            local_copy(q - 2, slot).wait()
            sib_copy(q - 2, slot).wait_send()

        part = _dot(ht_ref[b], dz_ref[...])

        @pl.when(b == 0)
        def _():
            acc[slot] = part

        @pl.when(b != 0)
        def _():
            acc[slot] += part

        @pl.when(b == nb - 1)
        def _():
            local_copy(q, slot).start()
            sib_copy(q, slot).start()

        @pl.when((q == n_chips - 1) & (b == nb - 1))
        def _():
            for qq in (n_chips - 2, n_chips - 1):
                local_copy(qq, qq % 2).wait()
                sib_copy(qq, qq % 2).wait_send()
            for qq in range(n_chips):
                sib_copy(qq, 0).wait_recv()
            g_finish()
            for cp in small_sib() + small_chip():
                cp.wait_send()

    return pl.pallas_call(
        body, name="bwd_win", grid=(n_chips, nb),
        in_specs=[_const_spec(ht.shape), pl.BlockSpec((None, s, 2 * sec), lambda q, b: (b, 0, q))]
        + [HBM_SPEC],
        out_specs=(HBM_SPEC,) * 5,
        out_shape=(jax.ShapeDtypeStruct((N_DEV, d, sec), F32), jax.ShapeDtypeStruct((n_chips, d, sec), F32))
        + (jax.ShapeDtypeStruct((n_chips,) + small.shape[1:], small.dtype),
           jax.ShapeDtypeStruct((3,) + small.shape[1:], small.dtype),
           jax.ShapeDtypeStruct((N_DEV * small.shape[1], small.shape[2]), small.dtype)),
        scratch_shapes=[pltpu.VMEM((2, d, 2 * sec), F32)]
        + [pltpu.VMEM((n_chips,) + small.shape[1:], F32)] * 3
        + [pltpu.VMEM((3,) + small.shape[1:], F32), pltpu.VMEM(small.shape[1:], F32), pltpu.SemaphoreType.DMA((2,)),
           pltpu.SemaphoreType.DMA((n_chips,)), pltpu.SemaphoreType.DMA((n_chips,)),
           pltpu.SemaphoreType.DMA((1, 4)), pltpu.SemaphoreType.DMA((1, 4)),
           pltpu.SemaphoreType.DMA((1, 3)), pltpu.SemaphoreType.DMA((1, 3)), pltpu.SemaphoreType.DMA((n_chips,))]
        + _gather_sems(1),
        compiler_params=pltpu.CompilerParams(dimension_semantics=("arbitrary", "arbitrary"), vmem_limit_bytes=VMEM_LIMIT),
    )(ht, dz, small)


def _bwd_dx(dz, x, dx1, win, g0, chip_srcs, tm):
    t, d = x.shape
    nc = win.shape[1]
    n = t // tm
    nm = len(chip_srcs)

    def body(dz_ref, x_ref, dx1_ref, win_ref, g0_ref, *rest):
        chip_src = rest[:nm]
        gx_ref = rest[nm]
        chip_out = rest[nm + 1:2 * nm + 1]
        parts_ref = rest[2 * nm + 1]
        pack, chip_send, chip_recv, p_send, p_recv, p_local = rest[2 * nm + 2:]
        i = pl.program_id(0)
        x_, y_, c_ = _my_pos()
        chip_copies = lambda: _chip_copies(chip_src, chip_out, chip_send, chip_recv)

        @pl.when(i == 0)
        def _():
            pack[...] = jnp.zeros_like(pack)
            for cp in chip_copies():
                cp.start()

        xv = x_ref[...]
        r0 = _rsqrt_mean_sq(xv)
        xh = xv * r0
        dh = _dot_nt(dz_ref[...], win_ref[...])
        _store_chunks(pack, 0, _colsum(dh * xh))
        dxh = dh * g0_ref[...]
        gx_ref[...] = dx1_ref[...] + r0 * (dxh - xh * jnp.mean(dxh * xh, axis=-1, keepdims=True))

        @pl.when(i == n - 1)
        def _():
            for cp in chip_copies():
                cp.wait()
            row0 = pl.multiple_of((4 * x_ + 2 * y_ + c_) * INP_PACK_ROWS, SUB)
            mine = parts_ref.at[pl.ds(row0, INP_PACK_ROWS), :]
            local = pltpu.make_async_copy(pack, mine, p_local)
            local.start()
            sends = []
            for k in range(1, N_DEV):
                peer = tuple(1 - v if (k >> bit) & 1 else v for v, bit in ((x_, 2), (y_, 1), (c_, 0)))
                sends.append(pltpu.make_async_remote_copy(
                    src_ref=pack, dst_ref=mine, send_sem=p_send.at[k - 1], recv_sem=p_recv.at[k - 1],
                    device_id=peer, device_id_type=MESH))
            for cp in sends:
                cp.start()
            for cp in sends:
                cp.wait()
            local.wait()

    tile = lambda w: pl.BlockSpec((tm, w), lambda i: (i, 0))
    return pl.pallas_call(
        body, name="bwd_dx", grid=(n,),
        in_specs=[tile(nc), tile(d), tile(d), _const_spec(win.shape), _const_spec(g0.shape)] + [HBM_SPEC] * nm,
        out_specs=(tile(d),) + (HBM_SPEC,) * (nm + 1),
        out_shape=(jax.ShapeDtypeStruct((t, d), F32),)
        + tuple(jax.ShapeDtypeStruct((3,) + g.shape[1:], g.dtype) for g in chip_srcs)
        + (jax.ShapeDtypeStruct((N_DEV * INP_PACK_ROWS, HEAD), F32),),
        scratch_shapes=[pltpu.VMEM((INP_PACK_ROWS, HEAD), F32),
                        pltpu.SemaphoreType.DMA((nm, 3)), pltpu.SemaphoreType.DMA((nm, 3)),
                        pltpu.SemaphoreType.DMA((N_DEV - 1,)), pltpu.SemaphoreType.DMA((N_DEV - 1,)),
                        pltpu.SemaphoreType.DMA],
        compiler_params=pltpu.CompilerParams(dimension_semantics=("arbitrary",), vmem_limit_bytes=VMEM_LIMIT),
    )(dz, x, dx1, win, g0, *chip_srcs)


ROW_BLOCKS = 8


def _chip_sums(core_idx, grads, recvs, dtypes, name):
    nw = len(grads)

    def body(idx_ref, *refs):
        gs, rs, outs = refs[:nw], refs[nw:2 * nw], refs[2 * nw:]
        for g, r, o in zip(gs, rs, outs):
            o[...] = (g[...] + r[...]).astype(o.dtype)

    def blk(g):
        return (None,) + g.shape[1:]

    return pl.pallas_call(
        body, name=name,
        grid_spec=pltpu.PrefetchScalarGridSpec(
            num_scalar_prefetch=1, grid=(4,),
            in_specs=[pl.BlockSpec(blk(g), lambda q, s: (2 * q + s[0], 0, 0)) for g in grads]
            + [pl.BlockSpec(blk(g), lambda q, s: (q, 0, 0)) for g in grads],
            out_specs=tuple(pl.BlockSpec(blk(g), lambda q, s: (q, 0, 0)) for g in grads)),
        out_shape=tuple(jax.ShapeDtypeStruct((4,) + g.shape[1:], dt) for g, dt in zip(grads, dtypes)),
    )(core_idx, *grads, *recvs)


def _adamw(w, g, m, v):
    m = ADAM_B1 * m + (1.0 - ADAM_B1) * g
    v = ADAM_B2 * v + (1.0 - ADAM_B2) * (g * g)
    m_hat = m / (1.0 - ADAM_B1 ** ADAM_STEP)
    v_hat = v / (1.0 - ADAM_B2 ** ADAM_STEP)
    delta = -ADAM_LR * (m_hat / (jnp.sqrt(v_hat) + ADAM_EPS) + ADAM_WD * w)
    return delta, m, v


def _adam_sections(sec_idx, grads, recv_sib, recv_chips, wmv):
    nw = len(wmv)

    def body(idx_ref, *refs):
        gs, rs, cs = refs[:nw], refs[nw:2 * nw], refs[2 * nw:3 * nw]
        params = refs[3 * nw:6 * nw]
        outs = refs[6 * nw:]
        for w in range(nw):
            g = gs[w][...] + rs[w][...]
            for k in range(3):
                g = g + cs[w][k].astype(F32)
            wv, mv, vv = (params[3 * w + t][...] for t in range(3))
            delta, m_new, v_new = _adamw(wv, g, mv, vv)
            outs[4 * w][...] = g
            outs[4 * w + 1][...] = delta
            outs[4 * w + 2][...] = m_new
            outs[4 * w + 3][...] = v_new

    def rb(g):
        return g.shape[1] // ROW_BLOCKS

    in_specs = [pl.BlockSpec((None, rb(g), g.shape[2]), lambda i, s: (s[0], i, 0)) for g in grads]
    in_specs += [pl.BlockSpec((None, rb(g), g.shape[2]), lambda i, s: (s[1], i, 0)) for g in grads]
    in_specs += [pl.BlockSpec((3, rb(g), g.shape[2]), lambda i, s: (0, i, 0)) for g in grads]
    sec = lambda g: pl.BlockSpec((rb(g), g.shape[2]), lambda i, s: (i, 0))
    for w in range(nw):
        in_specs += [sec(grads[w])] * 3
    out_specs, out_shape = [], []
    for g in grads:
        out_specs += [sec(g)] * 4
        out_shape += [jax.ShapeDtypeStruct(g.shape[1:], F32)] * 4
    flat = [a for t in wmv for a in t]
    outs = pl.pallas_call(
        body, name="adam_sections",
        grid_spec=pltpu.PrefetchScalarGridSpec(num_scalar_prefetch=1, grid=(ROW_BLOCKS,), in_specs=in_specs,
                                               out_specs=tuple(out_specs)),
        out_shape=tuple(out_shape),
    )(sec_idx, *grads, *recv_sib, *recv_chips, *flat)
    return [outs[4 * w:4 * w + 4] for w in range(nw)]


def _adam_small(sec_idx, gpack, parts, parts_wmv, vec_params, mat_params, conv_wmv):
    items = [(0, parts_wmv, "parts")]
    items += [(r0, t, "vec") for r0, t in vec_params] + [(r0, t, "mat") for r0, t in mat_params]
    items.append((_SEQ_ROWS["dcw"][0], conv_wmv, "conv"))

    def body(idx_ref, g_ref, parts_ref, *refs):
        ins, outs = refs[:3 * len(items)], refs[3 * len(items):]
        for n, (r0, _, kind) in enumerate(items):
            w_ref, m_ref, v_ref = ins[3 * n:3 * n + 3]
            o = outs[4 * n:4 * n + 4]
            if kind == "mat":
                g = g_ref[pl.ds(r0, w_ref.shape[0]), :]
                res = (g,) + _adamw(w_ref[...], g, m_ref[...], v_ref[...])
                for ref, val in zip(o, res):
                    ref[...] = val
            elif kind in ("vec", "parts"):
                for h in range(w_ref.shape[1] // HEAD):
                    cols = slice(h * HEAD, (h + 1) * HEAD)
                    if kind == "vec":
                        g = g_ref[pl.ds(r0 + h, 1), :]
                    else:
                        g = parts_ref[pl.ds(h, 1), :]
                        for dev in range(1, N_DEV):
                            g = g + parts_ref[pl.ds(SUB * dev + h, 1), :]
                    res = (g,) + _adamw(w_ref[:, cols], g, m_ref[:, cols], v_ref[:, cols])
                    for ref, val in zip(o, res):
                        ref[:, cols] = val
            else:
                rows = lax.broadcasted_iota(jnp.int32, (SUB, HEAD), 0)
                for k in range(4):
                    blk = g_ref[pl.ds(r0 + SUB * k, SUB), :]
                    g = jnp.sum(jnp.where(rows == idx_ref[0], blk, 0.0), axis=0, keepdims=True)
                    row = pl.ds(k, 1)
                    res = (g,) + _adamw(w_ref[row, :], g, m_ref[row, :], v_ref[row, :])
                    for ref, val in zip(o, res):
                        ref[row, :] = val
        outs[4 * len(items)][...] = g_ref[pl.ds(_OFF_MID + MID_LOSS_ROW, 1), :]

    flat = [a for _, t, _ in items for a in t]
    full = lambda a: pl.BlockSpec(a.shape, lambda i, s: (0,) * a.ndim)
    out_specs, out_shape = [], []
    for _, t, _ in items:
        out_specs += [full(t[0])] * 4
        out_shape += [jax.ShapeDtypeStruct(t[0].shape, F32)] * 4
    loss_row = jax.ShapeDtypeStruct((1, HEAD), F32)
    out_specs.append(full(loss_row))
    out_shape.append(loss_row)
    outs = pl.pallas_call(
        body, name="adam_small",
        grid_spec=pltpu.PrefetchScalarGridSpec(num_scalar_prefetch=1, grid=(1,),
                                               in_specs=[full(gpack), full(parts)] + [full(a) for a in flat],
                                               out_specs=tuple(out_specs)),
        out_shape=tuple(out_shape),
    )(sec_idx, gpack, parts, *flat)
    return [outs[4 * n:4 * n + 4] for n in range(len(items))], outs[4 * len(items)]


def _col_slicer(width):
    return lambda ref, idx: ref.at[:, pl.ds(pl.multiple_of(idx * width, HEAD), width)]


def _row_slicer(rows):
    return lambda ref, idx: ref.at[pl.ds(pl.multiple_of(idx * rows, SUB), rows), :]


def kernel(x, p, norm_g, w_in, conv_w, conv_b, lru_w_a, lru_b_a, lru_w_x, lru_b_x, lru_lambda, pool_w, pool_scale, w_proj_lru, w_proj_pool, w_out, ple_norm_g, w_ple_gate, w_ple_proj, final_g, loss_target, m_norm_g, m_w_in, m_conv_w, m_conv_b, m_lru_w_a, m_lru_b_a, m_lru_w_x, m_lru_b_x, m_lru_lambda, m_pool_w, m_pool_scale, m_w_proj_lru, m_w_proj_pool, m_w_out, m_ple_norm_g, m_w_ple_gate, m_w_ple_proj, m_final_g, v_norm_g, v_w_in, v_conv_w, v_conv_b, v_lru_w_a, v_lru_b_a, v_lru_w_x, v_lru_b_x, v_lru_lambda, v_pool_w, v_pool_scale, v_w_proj_lru, v_w_proj_pool, v_w_out, v_ple_norm_g, v_w_ple_gate, v_w_ple_proj, v_final_g):
    nb, s, d = x.shape
    t = nb * s
    tm = min(TILE_M, s)
    sec_idx = (4 * lax.axis_index("x") + 2 * lax.axis_index("y") + lax.axis_index("c")).astype(jnp.int32)
    chip_idx = (2 * lax.axis_index("x") + lax.axis_index("y")).astype(jnp.int32)
    core_idx = lax.axis_index("c").astype(jnp.int32)

    h, ht, win, cw = _pre(
        x, norm_g, [w_in[0], conv_w[0]], [True, False],
        [(d, N_DEV * w_in.shape[2]), (conv_w.shape[1], N_DEV * conv_w.shape[2])],
        [_col_slicer(w_in.shape[2]), _col_slicer(conv_w.shape[2])], tm)
    late_shards = [w_proj_lru[0], w_proj_pool[0], w_out[0], w_ple_gate[0], w_ple_proj[0]]
    late_shapes = [(N_DEV * w_proj_lru.shape[1], d), (w_proj_pool.shape[1], N_DEV * w_proj_pool.shape[2]),
                   (N_DEV * w_out.shape[1], d), (N_DEV * w_ple_gate.shape[1], d),
                   (w_ple_proj.shape[1], N_DEV * w_ple_proj.shape[2])]
    late_slicers = [_row_slicer(w_proj_lru.shape[1]), _col_slicer(w_proj_pool.shape[2]), _row_slicer(w_out.shape[1]),
                    _row_slicer(w_ple_gate.shape[1]), _col_slicer(w_ple_proj.shape[2])]

    lw = cw.shape[1]
    pwid = pool_scale.shape[1]
    wa = lru_w_a[0].astype(_MM)
    wx = lru_w_x[0].astype(_MM)
    pw = pool_w[0].astype(_MM)
    ba, bx = lru_b_a[0], lru_b_x[0]
    gf = final_g.reshape(1, d)
    core = core_idx.reshape(1)

    fwd_out = _fwd_seq(
        h, win, cw, conv_b, wa, ba, wx, bx, lru_lambda, pw, pool_scale, late_shards, late_shapes, late_slicers, tm)
    z, hl, ya, yb = fwd_out[:4]
    saved = fwd_out[4:N_SEQ_SAVED]
    wpl, wpp, wout, wpg, wpe = fwd_out[N_SEQ_SAVED:]
    x2d = x.reshape(t, d)
    mid_out = _mid(
        ya.reshape(t, lw), yb.reshape(t, pwid), h.reshape(t, d), win, x2d, p[0].reshape(t, -1),
        loss_target.reshape(t, d), wpl, wpp, wout, wpg, wpe, ple_norm_g, gf, tm)
    dya, dyb, dm, dx1 = mid_out[:4]
    grads_mid, pack_mid, sib_mid = list(mid_out[4:9]), mid_out[9], list(mid_out[10:])
    sums_mid = _chip_sums(core, grads_mid, sib_mid, [_WIRE] * len(grads_mid), "chip_sums_mid")
    seq_out = _bwd_seq(z, hl, saved, dya.reshape(nb, s, lw), dyb.reshape(nb, s, pwid), dm.reshape(nb, s, -1), cw,
                       wa, wx, lru_lambda, pw, pool_scale, pack_mid, sums_mid, tm)
    dz, chips_mid = seq_out[0], list(seq_out[2:])
    small = seq_out[1].reshape(N_DEV, SMALL_SEC, HEAD)
    g_in, sib_in, _, _, gsmall = _bwd_win(ht, dz, small)
    (sums_in,) = _chip_sums(core, [g_in], [sib_in], [_WIRE], "chip_sums_in")
    grad_x, chips_in, g0_parts = _bwd_dx(dz.reshape(t, -1), x2d, dx1, win, norm_g, [sums_in], tm)

    grads = [g_in] + grads_mid
    recv_sib = [sib_in] + sib_mid
    recv_chips = [chips_in] + chips_mid
    wmv = [(w_in[0], m_w_in[0], v_w_in[0]), (w_proj_lru[0], m_w_proj_lru[0], v_w_proj_lru[0]),
           (w_proj_pool[0], m_w_proj_pool[0], v_w_proj_pool[0]), (w_out[0], m_w_out[0], v_w_out[0]),
           (w_ple_gate[0], m_w_ple_gate[0], v_w_ple_gate[0]), (w_ple_proj[0], m_w_ple_proj[0], v_w_ple_proj[0])]
    big = _adam_sections(jnp.stack([sec_idx, chip_idx]), grads, recv_sib, recv_chips, wmv)

    R = _SEQ_ROWS
    vec = lambda r0, *t: (r0, tuple(a.reshape(1, -1) for a in t))
    mat = lambda r0, *t: (r0, tuple(a.reshape(-1, HEAD) for a in t))
    norm_wmv = vec(0, norm_g, m_norm_g, v_norm_g)[1]
    vec_params = [vec(R["dcb"][0], conv_b, m_conv_b, v_conv_b),
                  vec(R["dlam"][0], lru_lambda, m_lru_lambda, v_lru_lambda),
                  vec(R["dps"][0], pool_scale, m_pool_scale, v_pool_scale),
                  vec(_OFF_MID, ple_norm_g, m_ple_norm_g, v_ple_norm_g),
                  vec(_OFF_MID + SUB, final_g, m_final_g, v_final_g)]
    mat_params = [mat(R["dwa"][0], lru_w_a, m_lru_w_a, v_lru_w_a), mat(R["dba"][0], lru_b_a, m_lru_b_a, v_lru_b_a),
                  mat(R["dwx"][0], lru_w_x, m_lru_w_x, v_lru_w_x), mat(R["dbx"][0], lru_b_x, m_lru_b_x, v_lru_b_x),
                  mat(R["dpw"][0], pool_w, m_pool_w, v_pool_w)]
    conv_wmv = (conv_w[0], m_conv_w[0], v_conv_w[0])
    small_out, loss_row = _adam_small(sec_idx.reshape(1), gsmall, g0_parts, norm_wmv, vec_params, mat_params, conv_wmv)
    loss = loss_row[0, 0]

    res = {}
    names_small = ["norm_g", "conv_b", "lru_lambda", "pool_scale", "ple_norm_g", "final_g",
                   "lru_w_a", "lru_b_a", "lru_w_x", "lru_b_x", "pool_w", "conv_w"]
    shapes = {"norm_g": norm_g, "conv_b": conv_b, "lru_lambda": lru_lambda, "pool_scale": pool_scale,
              "ple_norm_g": ple_norm_g, "final_g": final_g, "lru_w_a": lru_w_a, "lru_b_a": lru_b_a, "lru_w_x": lru_w_x,
              "lru_b_x": lru_b_x, "pool_w": pool_w, "conv_w": conv_w}
    for name, quad in zip(names_small, small_out):
        res[name] = [a.reshape(shapes[name].shape) for a in quad]
    names_big = ["w_in", "w_proj_lru", "w_proj_pool", "w_out", "w_ple_gate", "w_ple_proj"]
    for name, quad, (w, _, _) in zip(names_big, big, wmv):
        res[name] = [a.reshape((1,) + w.shape) for a in quad]
    order = ["norm_g", "w_in", "conv_w", "conv_b", "lru_w_a", "lru_b_a", "lru_w_x", "lru_b_x", "lru_lambda", "pool_w",
             "pool_scale", "w_proj_lru", "w_proj_pool", "w_out", "ple_norm_g", "w_ple_gate", "w_ple_proj", "final_g"]
    out = [loss, grad_x.reshape(nb, s, d)]
    for kind in range(4):
        out += [res[name][kind] for name in order]
    return tuple(out)
```

```python
import jax
import jax.numpy as jnp
from jax import lax
from jax.experimental import pallas as pl
from jax.experimental.pallas import tpu as pltpu

F32 = jnp.float32
_MM = jnp.bfloat16
_WIRE = jnp.bfloat16

EPS = 1e-6
LRU_C = 8.0
MULT_SQ_FLOOR = 1e-30
POOL_WINDOWS = (2, 4, 8, 16)
N_HEADS = 8
HEAD = 128
HALO = 16
SUB = 8

ADAM_LR = 0.001
ADAM_B1 = 0.9
ADAM_B2 = 0.999
ADAM_EPS = 1e-08
ADAM_WD = 0.01
ADAM_STEP = 10

N_DEV = 8
MESH = pl.DeviceIdType.MESH
VMEM_LIMIT = 60 * 1024 * 1024
TILE_M = 256
N_SEQ_SAVED = 9

_SEQ_ROWS = {"dwa": (0, 1024), "dwx": (1024, 1024), "dpw": (2048, 512), "dba": (2560, 8), "dbx": (2568, 8),
             "dcb": (2576, 8), "dlam": (2584, 8), "dcw": (2592, 32), "dps": (2624, 8)}
SEQ_PACK_ROWS = 2632
MID_PACK_ROWS = 24
MID_LOSS_ROW = 16
INP_PACK_ROWS = 8
SMALL_ROWS = 3072
SMALL_SEC = SMALL_ROWS // N_DEV
_OFF_MID = SEQ_PACK_ROWS


def _dot(a, b):
    return jnp.dot(a, b, preferred_element_type=F32)


def _dot_nt(a, b):
    return lax.dot_general(a, b, (((1,), (1,)), ((), ())), preferred_element_type=F32)


def _dot_tn(a, b):
    return lax.dot_general(a, b, (((0,), (0,)), ((), ())), preferred_element_type=F32)


def _sigmoid(v):
    return 0.5 * jnp.tanh(0.5 * v) + 0.5


def _lru_mult(log_a, a):
    m2 = jnp.maximum(-jnp.tanh(log_a) * (1.0 + a * a), MULT_SQ_FLOOR)
    inv = lax.rsqrt(m2)
    return m2 * inv, inv


def _rows_back(v, n):
    return pltpu.roll(v, n, 0) if n else v


def _rows_ahead(v, n):
    return pltpu.roll(v, v.shape[0] - n, 0) if n else v


def _softplus_neg(lam):
    e = jnp.exp(-jnp.abs(lam))
    w = 1.0 + e
    l1p = jnp.where(w == 1.0, e, jnp.log(w) * (e / (w - 1.0)))
    return jnp.maximum(-lam, 0.0) + l1p


def _rsqrt_mean_sq(v):
    return lax.rsqrt(jnp.mean(v * v, axis=-1, keepdims=True) + EPS)


def _colsum(v):
    return jnp.sum(v, axis=0, keepdims=True)


def _const_spec(shape):
    nd = len(shape)
    return pl.BlockSpec(shape, lambda *_: (0,) * nd, pipeline_mode=pl.Buffered(1))


HBM_SPEC = pl.BlockSpec(memory_space=pl.ANY)


def _my_pos():
    return lax.axis_index("x"), lax.axis_index("y"), lax.axis_index("c")


def _other_chips(x, y):
    return [(1 - x, y), (x, 1 - y), (1 - x, 1 - y)]


def _gather_sems(nw):
    return [pltpu.SemaphoreType.DMA((nw, 7)), pltpu.SemaphoreType.DMA((nw, 7)), pltpu.SemaphoreType.DMA((nw,))]


def _gather_phases(srcs, outs, slicers, send_sems, recv_sems, local_sems, relay_diagonal=False):
    n_direct = 2 if relay_diagonal else 3
    nw = len(srcs)
    x, y, c = _my_pos()
    me, sibling = (x, y, c), (x, y, 1 - c)
    chips = _other_chips(x, y)

    def part(w, pos):
        return slicers[w](outs[w], 4 * pos[0] + 2 * pos[1] + pos[2])

    def copy(w, k, block, to, src=None):
        return pltpu.make_async_remote_copy(
            src_ref=part(w, block) if src is None else src, dst_ref=part(w, block),
            send_sem=send_sems.at[w, k], recv_sem=recv_sems.at[w, k], device_id=to, device_id_type=MESH)

    def mine():
        return [pltpu.make_async_copy(srcs[w], part(w, me), local_sems.at[w]) for w in range(nw)]

    def first():
        out = []
        for w in range(nw):
            out.append(copy(w, 0, me, sibling, src=srcs[w]))
            out += [copy(w, 1 + j, me, (*chips[j], c), src=srcs[w]) for j in range(n_direct)]
        return out

    def relay(w, j):
        return copy(w, 3, (*chips[j], c), (*chips[1 - j], c))

    def relays():
        return [relay(w, 0) for w in range(nw)] if relay_diagonal else []

    def passed():
        return [copy(w, 4 + j, (*chip, c), sibling) for j, chip in enumerate(chips) for w in range(nw)]

    def start():
        for cp in mine() + first():
            cp.start()

    def forward():
        for j, chip in enumerate(chips):
            for w in range(nw):
                copy(w, 1 + j, (*chip, c), me).wait_recv()
                copy(w, 4 + j, (*chip, c), sibling).start()
            if relay_diagonal and j < 2:
                @pl.when(c == j)
                def _():
                    for w in range(nw):
                        relay(w, j).start()

    def finish():
        for w in range(nw):
            copy(w, 0, sibling, me).wait_recv()
            for j, chip in enumerate(chips):
                copy(w, 4 + j, (*chip, 1 - c), me).wait_recv()
        for cp in first() + relays() + passed():
            cp.wait_send()
        for cp in mine():
            cp.wait()

    return start, forward, finish


def _sibling_copies(srcs, outs, send_sems, recv_sems):
    x, y, c = _my_pos()
    return [pltpu.make_async_remote_copy(
        src_ref=srcs[w].at[2 * q + (1 - c)], dst_ref=outs[w].at[q], send_sem=send_sems.at[w, q],
        recv_sem=recv_sems.at[w, q], device_id=(x, y, 1 - c), device_id_type=MESH)
        for w in range(len(srcs)) for q in range(4)]


def _chip_copies(srcs, outs, send_sems, recv_sems, which=None):
    x, y, c = _my_pos()
    return [pltpu.make_async_remote_copy(
        src_ref=srcs[w].at[2 * px + py], dst_ref=outs[w].at[k], send_sem=send_sems.at[w, k],
        recv_sem=recv_sems.at[w, k], device_id=(px, py, c), device_id_type=MESH)
        for w in (range(len(srcs)) if which is None else which) for k, (px, py) in enumerate(_other_chips(x, y))]


def _lru_gates(xc, h, wa_ref, ba_ref, wx_ref, bx_ref, neg_c_sp):
    xc16 = xc.astype(_MM)
    r = _sigmoid(_dot(xc16, wa_ref[h]) + ba_ref[pl.ds(h, 1), :])
    i = _sigmoid(_dot(xc16, wx_ref[h]) + bx_ref[pl.ds(h, 1), :])
    return r, i, neg_c_sp * r


def _conv_head(xa_ext, cw_ref, cb_ref, cols, tm):
    ext = xa_ext[:, cols]
    xc = cb_ref[:, cols] + cw_ref[pl.ds(3, 1), cols] * ext[HALO:HALO + tm]
    for k in range(3):
        xc = xc + cw_ref[pl.ds(k, 1), cols] * _rows_back(ext, 3 - k)[HALO:HALO + tm]
    return xc


def _window_sum(ext, k, shift):
    n = 1
    while n < k:
        ext = ext + shift(ext, n)
        n *= 2
    return ext


def _pool_diff(xb_ext, g, k, pos, tm):
    cols = slice(g * HEAD, (g + 1) * HEAD)
    ext = xb_ext[:, cols]
    ws = _window_sum(ext, k, _rows_back)[HALO:HALO + tm]
    inv_cnt = 1.0 / jnp.minimum(pos + 1, k).astype(F32)
    return ws * inv_cnt - ext[HALO:HALO + tm], inv_cnt


def _staged_sources(shards, as_operand):
    in_specs = [_const_spec(a.shape) if c else HBM_SPEC for a, c in zip(shards, as_operand)]
    stages = [pltpu.VMEM(a.shape, _MM) for a, c in zip(shards, as_operand) if c]
    dtypes = [_MM if c else a.dtype for a, c in zip(shards, as_operand)]
    return in_specs, stages, dtypes


def _gather_sources(srcs, stages, as_operand):
    stages = iter(stages)
    return [next(stages) if c else src for src, c in zip(srcs, as_operand)]


def _fill_stages(srcs, staged, as_operand):
    for src, dst, c in zip(srcs, staged, as_operand):
        if c:
            dst[...] = src[...].astype(dst.dtype)


def _pre(x, g0, shards, as_operand, shapes, slicers, tm):
    nb, s, d = x.shape
    nt = s // tm
    nl = len(shards)
    shard_specs, stage_shapes, wire = _staged_sources(shards, as_operand)

    def body(x_ref, g0_ref, *rest):
        src, rest = rest[:nl], rest[nl:]
        h_ref, ht_ref = rest[:2]
        out, rest = rest[2:2 + nl], rest[2 + nl:]
        (send_sems, recv_sems, local_sems), stages = rest[:3], rest[3:]
        step_no = pl.program_id(0) * nt + pl.program_id(1)
        staged = _gather_sources(src, stages, as_operand)
        g_start, g_forward, g_finish = _gather_phases(staged, out, slicers, send_sems, recv_sems, local_sems,
                                                      relay_diagonal=True)

        @pl.when(step_no == 0)
        def _():
            _fill_stages(src, staged, as_operand)
            g_start()

        xv = x_ref[...]
        h16 = ((xv * _rsqrt_mean_sq(xv)) * g0_ref[...]).astype(_MM)
        h_ref[...] = h16
        ht_ref[...] = h16.T

        @pl.when(step_no == nb * nt - 1)
        def _():
            g_forward()
            g_finish()

    tile = pl.BlockSpec((None, tm, d), lambda b, j: (b, j, 0))
    return pl.pallas_call(
        body, name="pre", grid=(nb, nt),
        in_specs=[tile, _const_spec(g0.shape)] + shard_specs,
        out_specs=(tile, pl.BlockSpec((None, d, tm), lambda b, j: (b, 0, j))) + (HBM_SPEC,) * nl,
        out_shape=(jax.ShapeDtypeStruct((nb, s, d), _MM), jax.ShapeDtypeStruct((nb, d, s), _MM))
        + tuple(jax.ShapeDtypeStruct(shp, dt) for shp, dt in zip(shapes, wire)),
        scratch_shapes=_gather_sems(nl) + stage_shapes,
        compiler_params=pltpu.CompilerParams(dimension_semantics=("arbitrary", "arbitrary")),
    )(x, g0, *shards)


def _fwd_seq(h, win, cw, cb, wa, ba, wx, bx, lam, pw, ps, late_shards, late_shapes, late_slicers, tm):
    nb, s, d = h.shape
    lw = cw.shape[1]
    pwid = ps.shape[1]
    nc = 2 * lw + 2 * pwid
    nt = s // tm
    nl = len(late_shards)
    n_steps = nb * nt
    as_operand = [True] * nl
    shard_specs, stage_shapes, wire = _staged_sources(late_shards, as_operand)

    def body(h_ref, win_ref, cw_ref, cb_ref, wa_ref, ba_ref, wx_ref, bx_ref, lam_ref, pw_ref, ps_ref, *rest):
        late_src, rest = rest[:nl], rest[nl:]
        z_ref, hl_ref, ya_ref, yb_ref, xc_ref, r_ref, i_ref, diff_ref, yp_ref = rest[:N_SEQ_SAVED]
        late_out, rest = rest[N_SEQ_SAVED:N_SEQ_SAVED + nl], rest[N_SEQ_SAVED + nl:]
        (xa_ext, xb_ext, a_buf, carry, send_sems, recv_sems, local_sems), stages = rest[:7], rest[7:]
        j = pl.program_id(1)
        step_no = pl.program_id(0) * nt + j
        staged = _gather_sources(late_src, stages, as_operand)
        g_start, g_forward, g_finish = _gather_phases(staged, late_out, late_slicers, send_sems, recv_sems, local_sems)

        @pl.when(step_no == 0)
        def _():
            _fill_stages(late_src, staged, as_operand)
            g_start()

        pl.when(step_no == (3 * n_steps) // 4)(g_forward)

        @pl.when(j == 0)
        def _():
            xa_ext[pl.ds(0, HALO), :] = jnp.zeros((HALO, lw), F32)
            xb_ext[pl.ds(0, HALO), :] = jnp.zeros((HALO, pwid), F32)
            carry[...] = jnp.zeros_like(carry)

        h16 = h_ref[...]
        xa = _dot(h16, win_ref[:, 0:lw])
        z_ref[:, 0:lw] = xa
        xa_ext[pl.ds(HALO, tm), :] = xa
        for c0 in range(lw, nc, lw):
            z_ref[:, c0:c0 + lw] = _dot(h16, win_ref[:, c0:c0 + lw])

        for h in range(N_HEADS):
            cols = slice(h * HEAD, (h + 1) * HEAD)
            xc = _conv_head(xa_ext, cw_ref, cb_ref, cols, tm)
            neg_c_sp = -LRU_C * _softplus_neg(lam_ref[:, cols])
            r, i, log_a = _lru_gates(xc, h, wa_ref, ba_ref, wx_ref, bx_ref, neg_c_sp)
            a = jnp.exp(log_a)
            xc_ref[:, cols] = xc
            r_ref[:, cols] = r
            i_ref[:, cols] = i
            a_buf[:, cols] = a
            hl_ref[:, cols] = _lru_mult(log_a, a)[0] * (i * xc)

        rows = lax.broadcasted_iota(jnp.int32, (SUB, lw), 0)

        def step(c, car):
            i0 = pl.multiple_of(c * SUB, SUB)
            av = a_buf[pl.ds(i0, SUB), :]
            bv = hl_ref[pl.ds(i0, SUB), :]
            for sh in (1, 2, 4):
                m = rows >= sh
                a_sh = jnp.where(m, pltpu.roll(av, sh, 0), 1.0)
                b_sh = jnp.where(m, pltpu.roll(bv, sh, 0), 0.0)
                bv = av * b_sh + bv
                av = av * a_sh
            hv = av * car + bv
            hl_ref[pl.ds(i0, SUB), :] = hv
            return jnp.broadcast_to(hv[SUB - 1:SUB, :], (SUB, lw))

        carry[...] = lax.fori_loop(0, tm // SUB, step, carry[...])

        ga = z_ref[:, lw:2 * lw]
        ya_ref[...] = (hl_ref[...] * (ga * _sigmoid(ga))).astype(ya_ref.dtype)

        xb_ext[pl.ds(HALO, tm), :] = z_ref[:, 2 * lw:2 * lw + pwid]
        pos = j * tm + lax.broadcasted_iota(jnp.int32, (tm, HEAD), 0)
        for g, k in enumerate(POOL_WINDOWS):
            cols = slice(g * HEAD, (g + 1) * HEAD)
            diff16 = _pool_diff(xb_ext, g, k, pos, tm)[0].astype(_MM)
            yp = _dot(diff16, pw_ref[g])
            diff_ref[:, cols] = diff16
            yp_ref[:, cols] = yp
            gb = z_ref[:, 2 * lw + pwid + g * HEAD:2 * lw + pwid + (g + 1) * HEAD]
            yb_ref[:, cols] = ((yp * ps_ref[:, cols]) * (gb * _sigmoid(gb))).astype(yb_ref.dtype)

        xa_ext[pl.ds(0, HALO), :] = xa_ext[pl.ds(tm, HALO), :]
        xb_ext[pl.ds(0, HALO), :] = xb_ext[pl.ds(tm, HALO), :]
        pl.when(step_no == n_steps - 1)(g_finish)

    tile = lambda w: pl.BlockSpec((None, tm, w), lambda b, j: (b, j, 0))
    return pl.pallas_call(
        body, name="fwd_seq", grid=(nb, nt),
        in_specs=[tile(d), _const_spec((d, nc)), _const_spec(cw.shape), _const_spec(cb.shape),
                  _const_spec(wa.shape), _const_spec(ba.shape), _const_spec(wx.shape), _const_spec(bx.shape),
                  _const_spec(lam.shape), _const_spec(pw.shape), _const_spec(ps.shape)] + shard_specs,
        out_specs=(tile(nc), tile(lw), tile(lw), tile(pwid), tile(lw), tile(lw), tile(lw), tile(pwid), tile(pwid))
        + (HBM_SPEC,) * nl,
        out_shape=(jax.ShapeDtypeStruct((nb, s, nc), F32), jax.ShapeDtypeStruct((nb, s, lw), F32),
                   jax.ShapeDtypeStruct((nb, s, lw), _MM), jax.ShapeDtypeStruct((nb, s, pwid), _MM))
        + (jax.ShapeDtypeStruct((nb, s, lw), F32),) * 3
        + (jax.ShapeDtypeStruct((nb, s, pwid), _MM), jax.ShapeDtypeStruct((nb, s, pwid), F32))
        + tuple(jax.ShapeDtypeStruct(shp, dt) for shp, dt in zip(late_shapes, wire)),
        scratch_shapes=[pltpu.VMEM((tm + HALO, lw), F32), pltpu.VMEM((tm + HALO, pwid), F32),
                        pltpu.VMEM((tm, lw), F32), pltpu.VMEM((SUB, lw), F32)] + _gather_sems(nl) + stage_shapes,
        compiler_params=pltpu.CompilerParams(dimension_semantics=("arbitrary", "arbitrary"), vmem_limit_bytes=VMEM_LIMIT),
    )(h, win, cw, cb, wa, ba, wx, bx, lam, pw, ps, *late_shards)


def _store_chunks(ref, row0, vec):
    for h in range(vec.shape[1] // HEAD):
        ref[pl.ds(row0 + h, 1), :] += vec[:, h * HEAD:(h + 1) * HEAD]


def _mid(ya, yb, h, win, x, p, tgt, wpl, wpp, wout, wpg, wpe, g1, gf, tm):
    t, d = x.shape
    lw = ya.shape[1]
    pwid = yb.shape[1]
    pdim = p.shape[1]
    n = t // tm
    sec_pp = wpp.shape[1] // N_DEV
    sec_pe = wpe.shape[1] // N_DEV
    sec_rows = wpl.shape[0] // N_DEV
    grad_secs = [(sec_rows, d), (wpp.shape[0], sec_pp), (sec_rows, d), (sec_rows, d), (wpe.shape[0], sec_pe)]
    n_win_blocks = win.shape[1] // d
    win_cols = lambda c: pl.BlockSpec((d, d), lambda i: (0, c), pipeline_mode=pl.Buffered(1))

    def body(ya_ref, yb_ref, h_ref, wma_ref, wmb_ref, x_ref, p_ref, tgt_ref, wpl_ref, wpp_ref, wout_ref, wpg_ref, wpe_ref,
             g1_ref, gf_ref,
             dya_ref, dyb_ref, dm_ref, dx1_ref, gpl_ref, gpp_ref, gout_ref, gpg_ref, gpe_ref, pack_ref,
             sib_pl, sib_pp, sib_out, sib_pg, sib_pe,
             acc_pl, acc_pp, acc_out, acc_pg, acc_pe, out_sems, sib_send, sib_recv):
        i = pl.program_id(0)

        @pl.when(i == 0)
        def _():
            for acc in (acc_pl, acc_pp, acc_out, acc_pg, acc_pe):
                acc[...] = jnp.zeros_like(acc)
            pack_ref[...] = jnp.zeros_like(pack_ref)

        ya16 = ya_ref[...]
        yb16 = yb_ref[...]
        br_a = _dot(ya16, wpl_ref[...])
        br_b = _dot(yb16, wpp_ref[...])
        h16 = h_ref[...]
        sa = _sigmoid(_dot(h16, wma_ref[...]))
        sb = _sigmoid(_dot(h16, wmb_ref[...]))
        mg16 = (sa * br_a + sb * br_b).astype(_MM)
        x1 = x_ref[...] + _dot(mg16, wout_ref[...])
        r1 = _rsqrt_mean_sq(x1)
        n1 = x1 * r1
        h116 = (n1 * g1_ref[...]).astype(_MM)
        gate = _sigmoid(_dot(h116, wpg_ref[...]))
        p16 = p_ref[...].astype(_MM)
        pe = _dot(p16, wpe_ref[...])
        x2 = x1 + gate * pe
        r2 = _rsqrt_mean_sq(x2)
        n2 = x2 * r2
        err = n2 * gf_ref[...] - tgt_ref[...]
        sq = jnp.sum(_colsum(err * err), axis=1, keepdims=True)
        pack_ref[pl.ds(MID_LOSS_ROW, 1), :] += jnp.broadcast_to(sq * (0.5 / d), (1, HEAD))

        dy = err * (1.0 / d)
        _store_chunks(pack_ref, SUB, _colsum(dy * n2))
        dn2 = dy * gf_ref[...]
        dx2 = r2 * (dn2 - n2 * jnp.mean(dn2 * n2, axis=-1, keepdims=True))
        dpe16 = (dx2 * gate).astype(_MM)
        dpg16 = ((dx2 * pe) * (gate * (1.0 - gate))).astype(_MM)
        acc_pe[...] += _dot_tn(p16, dpe16)
        acc_pg[...] += _dot_tn(h116, dpg16)
        dh1 = _dot_nt(dpg16, wpg_ref[...])
        _store_chunks(pack_ref, 0, _colsum(dh1 * n1))
        dn1 = dh1 * g1_ref[...]
        dx1 = dx2 + r1 * (dn1 - n1 * jnp.mean(dn1 * n1, axis=-1, keepdims=True))
        dx1_ref[...] = dx1
        dx116 = dx1.astype(_MM)
        acc_out[...] += _dot_tn(mg16, dx116)
        dmg = _dot_nt(dx116, wout_ref[...])
        da16 = (dmg * sa).astype(_MM)
        db16 = (dmg * sb).astype(_MM)
        dm_ref[:, 0:d] = ((dmg * br_a) * (sa * (1.0 - sa))).astype(dm_ref.dtype)
        dm_ref[:, d:2 * d] = ((dmg * br_b) * (sb * (1.0 - sb))).astype(dm_ref.dtype)
        acc_pl[...] += _dot_tn(ya16, da16)
        acc_pp[...] += _dot_tn(yb16, db16)
        dya_ref[...] = _dot_nt(da16, wpl_ref[...])
        dyb_ref[...] = _dot_nt(db16, wpp_ref[...])

        @pl.when(i == n - 1)
        def _():
            writes = []
            for k in range(N_DEV):
                rows = pl.ds(k * sec_rows, sec_rows)
                for acc, out in ((acc_pl, gpl_ref), (acc_out, gout_ref), (acc_pg, gpg_ref)):
                    writes.append((acc.at[rows, :], out.at[k]))
                writes.append((acc_pp.at[:, pl.ds(k * sec_pp, sec_pp)], gpp_ref.at[k]))
                writes.append((acc_pe.at[:, pl.ds(k * sec_pe, sec_pe)], gpe_ref.at[k]))
            writes = [pltpu.make_async_copy(src, dst, out_sems.at[n]) for n, (src, dst) in enumerate(writes)]
            for cp in writes:
                cp.start()
            for cp in writes:
                cp.wait()
            to_sibling = _sibling_copies([gpl_ref, gpp_ref, gout_ref, gpg_ref, gpe_ref],
                                         [sib_pl, sib_pp, sib_out, sib_pg, sib_pe], sib_send, sib_recv)
            for cp in to_sibling:
                cp.start()
            for cp in to_sibling:
                cp.wait()

    tile = lambda w: pl.BlockSpec((tm, w), lambda i: (i, 0))
    any_spec = pl.BlockSpec(memory_space=pl.ANY)
    full = lambda shape: pl.BlockSpec(shape, lambda i: (0,) * len(shape))
    return pl.pallas_call(
        body, name="mid", grid=(n,),
        in_specs=[tile(lw), tile(pwid), tile(d), win_cols(n_win_blocks - 2), win_cols(n_win_blocks - 1),
                  tile(d), tile(pdim), tile(d),
                  _const_spec(wpl.shape), _const_spec(wpp.shape), _const_spec(wout.shape), _const_spec(wpg.shape),
                  _const_spec(wpe.shape), _const_spec(g1.shape), _const_spec(gf.shape)],
        out_specs=(tile(lw), tile(pwid), tile(2 * d), tile(d), any_spec, any_spec, any_spec, any_spec, any_spec,
                   full((MID_PACK_ROWS, HEAD))) + (any_spec,) * 5,
        out_shape=(jax.ShapeDtypeStruct((t, lw), F32), jax.ShapeDtypeStruct((t, pwid), F32),
                   jax.ShapeDtypeStruct((t, 2 * d), _MM), jax.ShapeDtypeStruct((t, d), F32),
                   *(jax.ShapeDtypeStruct((N_DEV,) + sec, F32) for sec in grad_secs),
                   jax.ShapeDtypeStruct((MID_PACK_ROWS, HEAD), F32),
                   *(jax.ShapeDtypeStruct((N_DEV // 2,) + sec, F32) for sec in grad_secs)),
        scratch_shapes=[pltpu.VMEM(wpl.shape, F32), pltpu.VMEM(wpp.shape, F32), pltpu.VMEM(wout.shape, F32),
                        pltpu.VMEM(wpg.shape, F32), pltpu.VMEM(wpe.shape, F32),
                        pltpu.SemaphoreType.DMA((5 * N_DEV,)), pltpu.SemaphoreType.DMA((5, N_DEV // 2)),
                        pltpu.SemaphoreType.DMA((5, N_DEV // 2))],
        compiler_params=pltpu.CompilerParams(dimension_semantics=("arbitrary",), vmem_limit_bytes=VMEM_LIMIT),
    )(ya, yb, h, win, win, x, p, tgt, wpl, wpp, wout, wpg, wpe, g1, gf)


def _bwd_seq(z, hl, saved, dya, dyb, dm, cw, wa, wx, lam, pw, ps, pack_mid, sib_grads, tm):
    nb, s, zw = z.shape
    lw = cw.shape[1]
    pwid = ps.shape[1]
    nt = s // tm
    nc = zw + dm.shape[2]
    R = _SEQ_ROWS
    ns = len(sib_grads)

    def body(zq_ref, zgb_ref, hl_ref, hlh_ref, xc_buf, r_buf, i_buf, diff_ref, yp_ref, dya_ref, dyb_ref, dm_ref,
             cw_ref, wa_ref, wx_ref, lam_ref, pw_ref, ps_ref, pack_mid_ref, *rest):
        sib_src, rest = rest[:ns], rest[ns:]
        dz_ref, pack_ref = rest[:2]
        sib_out, rest = rest[2:2 + ns], rest[2 + ns:]
        a_ext, an_buf, hl_ext, dh_buf, dxc_ext, q_ext, carry, send_sems, recv_sems = rest
        b = pl.program_id(0)
        j = pl.program_id(1)
        jr = nt - 1 - j
        has_prev = jr > 0

        @pl.when((b == 0) & (j == 0))
        def _():
            for cp in _chip_copies(sib_src, sib_out, send_sems, recv_sems):
                cp.start()

        dz_ref[:, zw:nc] = dm_ref[...]

        @pl.when((b == 0) & (j == 0))
        def _():
            pack_ref[...] = jnp.zeros_like(pack_ref)

        @pl.when(j == 0)
        def _():
            a_ext[pl.ds(tm, SUB), :] = jnp.zeros((SUB, lw), F32)
            dxc_ext[pl.ds(tm, HALO), :] = jnp.zeros((HALO, lw), F32)
            q_ext[pl.ds(tm, HALO), :] = jnp.zeros((HALO, pwid), F32)
            carry[...] = jnp.zeros_like(carry)

        hl_ext[pl.ds(0, SUB), :] = jnp.where(has_prev, hlh_ref[...], 0.0)
        hl_ext[pl.ds(SUB, tm), :] = hl_ref[...]
        for h in range(N_HEADS):
            cols = slice(h * HEAD, (h + 1) * HEAD)
            a_ext[pl.ds(0, tm), cols] = jnp.exp((-LRU_C * _softplus_neg(lam_ref[:, cols])) * r_buf[:, cols])
            ga = zq_ref[:, lw + h * HEAD:lw + (h + 1) * HEAD]
            sg = _sigmoid(ga)
            dyav = dya_ref[:, cols]
            dh_buf[:, cols] = dyav * (ga * sg)
            dga = (dyav * hl_ext[pl.ds(SUB, tm), cols]) * (sg * (1.0 + ga * (1.0 - sg)))
            dz_ref[:, lw + h * HEAD:lw + (h + 1) * HEAD] = dga.astype(dz_ref.dtype)
        an_buf[...] = _rows_ahead(a_ext[...], 1)[0:tm]

        rows = lax.broadcasted_iota(jnp.int32, (SUB, lw), 0)
        nch = tm // SUB

        def step(c, car):
            i0 = pl.multiple_of((nch - 1 - c) * SUB, SUB)
            cv = an_buf[pl.ds(i0, SUB), :]
            bv = dh_buf[pl.ds(i0, SUB), :]
            for sh in (1, 2, 4):
                m = rows < SUB - sh
                c_sh = jnp.where(m, pltpu.roll(cv, SUB - sh, 0), 1.0)
                b_sh = jnp.where(m, pltpu.roll(bv, SUB - sh, 0), 0.0)
                bv = cv * b_sh + bv
                cv = cv * c_sh
            hv = cv * car + bv
            dh_buf[pl.ds(i0, SUB), :] = hv
            return jnp.broadcast_to(hv[0:1, :], (SUB, lw))

        carry[...] = lax.fori_loop(0, nch, step, carry[...])

        for h in range(N_HEADS):
            cols = slice(h * HEAD, (h + 1) * HEAD)
            lam_h = lam_ref[:, cols]
            neg_c_sp = -LRU_C * _softplus_neg(lam_h)
            xc = xc_buf[:, cols]
            r = r_buf[:, cols]
            i = i_buf[:, cols]
            a = a_ext[pl.ds(0, tm), cols]
            mult, inv_mult = _lru_mult(neg_c_sp * r, a)
            dh = dh_buf[:, cols]
            dhx = dh * xc
            di = dhx * mult
            h_prev = _rows_back(hl_ext[:, cols], 1)[SUB:SUB + tm]
            dlog_a = (dh * h_prev) * a - (dhx * i) * ((a * a) * inv_mult)
            pack_ref[pl.ds(R["dlam"][0] + h, 1), :] += _colsum(dlog_a * r) * (LRU_C * _sigmoid(-lam_h))
            dpa = (dlog_a * neg_c_sp) * (r * (1.0 - r))
            dpx = di * (i * (1.0 - i))
            dpa16 = dpa.astype(_MM)
            dpx16 = dpx.astype(_MM)
            xc16 = xc.astype(_MM)
            dxc = dh * (mult * i) + _dot_nt(dpa16, wa_ref[h]) + _dot_nt(dpx16, wx_ref[h])
            pack_ref[pl.ds(R["dwa"][0] + h * HEAD, HEAD), :] += _dot_tn(xc16, dpa16)
            pack_ref[pl.ds(R["dwx"][0] + h * HEAD, HEAD), :] += _dot_tn(xc16, dpx16)
            pack_ref[pl.ds(R["dba"][0] + h, 1), :] += _colsum(dpa)
            pack_ref[pl.ds(R["dbx"][0] + h, 1), :] += _colsum(dpx)
            pack_ref[pl.ds(R["dcb"][0] + h, 1), :] += _colsum(dxc)
            dxc_ext[pl.ds(0, tm), cols] = dxc

        for h in range(N_HEADS):
            cols = slice(h * HEAD, (h + 1) * HEAD)
            dxc_all = dxc_ext[:, cols]
            xa = zq_ref[:, cols]
            dxa = None
            for k in range(4):
                dxc_k = _rows_ahead(dxc_all, 3 - k)[0:tm]
                pack_ref[pl.ds(R["dcw"][0] + SUB * k + h, 1), :] += _colsum(dxc_k * xa)
                term = cw_ref[pl.ds(k, 1), cols] * dxc_k
                dxa = term if dxa is None else dxa + term
            dz_ref[:, cols] = dxa.astype(dz_ref.dtype)

        pos = jr * tm + lax.broadcasted_iota(jnp.int32, (tm, HEAD), 0)
        for g, k in enumerate(POOL_WINDOWS):
            cols = slice(g * HEAD, (g + 1) * HEAD)
            inv_cnt = 1.0 / jnp.minimum(pos + 1, k).astype(F32)
            diff16 = diff_ref[:, cols]
            yp = yp_ref[:, cols]
            sc = ps_ref[:, cols]
            gb = zgb_ref[:, cols]
            sgb = _sigmoid(gb)
            dybv = dyb_ref[:, cols]
            dy_pool = dybv * (gb * sgb)
            dgb = (dybv * (yp * sc)) * (sgb * (1.0 + gb * (1.0 - sgb)))
            pack_ref[pl.ds(R["dps"][0] + g, 1), :] += _colsum(dy_pool * yp)
            dyp16 = (dy_pool * sc).astype(_MM)
            pack_ref[pl.ds(R["dpw"][0] + g * HEAD, HEAD), :] += _dot_tn(diff16, dyp16)
            ddiff = _dot_nt(dyp16, pw_ref[g])
            q_ext[pl.ds(0, tm), cols] = ddiff * inv_cnt
            dxb = _window_sum(q_ext[:, cols], k, _rows_ahead)[0:tm] - ddiff
            dz_ref[:, 2 * lw + g * HEAD:2 * lw + (g + 1) * HEAD] = dxb.astype(dz_ref.dtype)
            dz_ref[:, 2 * lw + pwid + g * HEAD:2 * lw + pwid + (g + 1) * HEAD] = dgb.astype(dz_ref.dtype)

        a_ext[pl.ds(tm, SUB), :] = a_ext[pl.ds(0, SUB), :]
        dxc_ext[pl.ds(tm, HALO), :] = dxc_ext[pl.ds(0, HALO), :]
        q_ext[pl.ds(tm, HALO), :] = q_ext[pl.ds(0, HALO), :]

        @pl.when((b == nb - 1) & (j == nt - 1))
        def _():
            pack_ref[pl.ds(_OFF_MID, MID_PACK_ROWS), :] = pack_mid_ref[...]
            for cp in _chip_copies(sib_src, sib_out, send_sems, recv_sems):
                cp.wait()

    rev = lambda w: pl.BlockSpec((None, tm, w), lambda b, j: (b, nt - 1 - j, 0))
    prev_rows = lambda rows, w: pl.BlockSpec(
        (None, rows, w), lambda b, j: (b, jnp.maximum((nt - 1 - j) * (tm // rows) - 1, 0), 0))
    return pl.pallas_call(
        body, name="bwd_seq", grid=(nb, nt),
        in_specs=[rev(2 * lw), pl.BlockSpec((None, tm, pwid), lambda b, j: (b, nt - 1 - j, zw // pwid - 1)),
                  rev(lw), prev_rows(SUB, lw), rev(lw), rev(lw), rev(lw), rev(pwid), rev(pwid),
                  rev(lw), rev(pwid), rev(nc - zw),
                  _const_spec(cw.shape), _const_spec(wa.shape), _const_spec(wx.shape), _const_spec(lam.shape),
                  _const_spec(pw.shape), _const_spec(ps.shape), _const_spec(pack_mid.shape)] + [HBM_SPEC] * ns,
        out_specs=(rev(nc), pl.BlockSpec((SMALL_ROWS, HEAD), lambda b, j: (0, 0))) + (HBM_SPEC,) * ns,
        out_shape=(jax.ShapeDtypeStruct((nb, s, nc), _MM), jax.ShapeDtypeStruct((SMALL_ROWS, HEAD), F32))
        + tuple(jax.ShapeDtypeStruct((3,) + g.shape[1:], g.dtype) for g in sib_grads),
        scratch_shapes=[pltpu.VMEM((tm + SUB, lw), F32), pltpu.VMEM((tm, lw), F32), pltpu.VMEM((tm + SUB, lw), F32),
                        pltpu.VMEM((tm, lw), F32), pltpu.VMEM((tm + HALO, lw), F32),
                        pltpu.VMEM((tm + HALO, pwid), F32), pltpu.VMEM((SUB, lw), F32),
                        pltpu.SemaphoreType.DMA((ns, 3)), pltpu.SemaphoreType.DMA((ns, 3))],
        compiler_params=pltpu.CompilerParams(dimension_semantics=("arbitrary", "arbitrary"), vmem_limit_bytes=VMEM_LIMIT),
    )(z, z, hl, hl, *saved, dya, dyb, dm, cw, wa, wx, lam, pw, ps, pack_mid, *sib_grads)


def _bwd_win(ht, dz, small):
    nb, d, s = ht.shape
    nc = dz.shape[2]
    sec = nc // N_DEV
    n_chips = N_DEV // 2

    def body(ht_ref, dz_ref, small_src, gwin_ref, recv_ref, small_out, small_chips_out, gsmall_ref,
             acc, own4, sib4, csum, got3, total, local_sems, sib_send, sib_recv, small_send, small_recv,
             sc_send, sc_recv, ld_sems, g_send, g_recv, g_local):
        q = pl.program_id(0)
        b = pl.program_id(1)
        slot = q % 2
        x, y, c = _my_pos()

        def half(sl, core):
            return acc.at[sl, :, pl.ds(pl.multiple_of(core * sec, HEAD), sec)]

        def local_copy(qq, sl):
            return pltpu.make_async_copy(half(sl, c), gwin_ref.at[2 * qq + c], local_sems.at[sl])

        def sib_copy(qq, sl):
            return pltpu.make_async_remote_copy(
                src_ref=half(sl, 1 - c), dst_ref=recv_ref.at[qq], send_sem=sib_send.at[qq],
                recv_sem=sib_recv.at[qq], device_id=(x, y, 1 - c), device_id_type=MESH)

        step_no = q * nb + b
        n_steps = n_chips * nb
        sum_step = min(3, n_steps - 1)
        fwd_step = max(sum_step, min(6, n_steps - 1))
        small_sib = lambda: _sibling_copies([small_src], [small_out], small_send, small_recv)
        small_chip = lambda: _chip_copies([csum], [small_chips_out], sc_send, sc_recv)
        g_start, g_forward, g_finish = _gather_phases([total], [gsmall_ref], [_row_slicer(small.shape[1])],
                                                      g_send, g_recv, g_local)

        def load(srcs, dst):
            cps = [pltpu.make_async_copy(src, dst.at[k], ld_sems.at[k]) for k, src in enumerate(srcs)]
            for cp in cps:
                cp.start()
            for cp in cps:
                cp.wait()

        @pl.when(step_no == 0)
        def _():
            for cp in small_sib():
                cp.start()

        @pl.when(step_no == 1)
        def _():
            for cp in small_sib():
                cp.wait_recv()
            load([small_src.at[2 * qq + c] for qq in range(n_chips)], own4)
            load([small_out.at[qq] for qq in range(n_chips)], sib4)
            csum[...] = own4[...] + sib4[...]
            for cp in small_chip():
                cp.start()

        @pl.when(step_no == sum_step)
        def _():
            for cp in small_chip():
                cp.wait_recv()
            load([small_chips_out.at[k] for k in range(3)], got3)
            total[...] = ((csum[2 * x + y] + got3[0]) + got3[1]) + got3[2]
            g_start()

        pl.when(step_no == fwd_step)(g_forward)

        @pl.when((q >= 2) & (b == 0))
        def _():
            local_copy(q - 2, slot).wait()
            sib_copy(q - 2, slot).wait_send()

        part = _dot(ht_ref[b], dz_ref[...])

        @pl.when(b == 0)
        def _():
            acc[slot] = part

        @pl.when(b != 0)
        def _():
            acc[slot] += part

        @pl.when(b == nb - 1)
        def _():
            local_copy(q, slot).start()
            sib_copy(q, slot).start()

        @pl.when((q == n_chips - 1) & (b == nb - 1))
        def _():
            for qq in (n_chips - 2, n_chips - 1):
                local_copy(qq, qq % 2).wait()
                sib_copy(qq, qq % 2).wait_send()
            for qq in range(n_chips):
                sib_copy(qq, 0).wait_recv()
            g_finish()
            for cp in small_sib() + small_chip():
                cp.wait_send()

    return pl.pallas_call(
        body, name="bwd_win", grid=(n_chips, nb),
        in_specs=[_const_spec(ht.shape), pl.BlockSpec((None, s, 2 * sec), lambda q, b: (b, 0, q))]
        + [HBM_SPEC],
        out_specs=(HBM_SPEC,) * 5,
        out_shape=(jax.ShapeDtypeStruct((N_DEV, d, sec), F32), jax.ShapeDtypeStruct((n_chips, d, sec), F32))
        + (jax.ShapeDtypeStruct((n_chips,) + small.shape[1:], small.dtype),
           jax.ShapeDtypeStruct((3,) + small.shape[1:], small.dtype),
           jax.ShapeDtypeStruct((N_DEV * small.shape[1], small.shape[2]), small.dtype)),
        scratch_shapes=[pltpu.VMEM((2, d, 2 * sec), F32)]
        + [pltpu.VMEM((n_chips,) + small.shape[1:], F32)] * 3
        + [pltpu.VMEM((3,) + small.shape[1:], F32), pltpu.VMEM(small.shape[1:], F32), pltpu.SemaphoreType.DMA((2,)),
           pltpu.SemaphoreType.DMA((n_chips,)), pltpu.SemaphoreType.DMA((n_chips,)),
           pltpu.SemaphoreType.DMA((1, 4)), pltpu.SemaphoreType.DMA((1, 4)),
           pltpu.SemaphoreType.DMA((1, 3)), pltpu.SemaphoreType.DMA((1, 3)), pltpu.SemaphoreType.DMA((n_chips,))]
        + _gather_sems(1),
        compiler_params=pltpu.CompilerParams(dimension_semantics=("arbitrary", "arbitrary"), vmem_limit_bytes=VMEM_LIMIT),
    )(ht, dz, small)


def _bwd_dx(dz, x, dx1, win, g0, chip_srcs, tm):
    t, d = x.shape
    nc = win.shape[1]
    n = t // tm
    nm = len(chip_srcs)

    def body(dz_ref, x_ref, dx1_ref, win_ref, g0_ref, *rest):
        chip_src = rest[:nm]
        gx_ref = rest[nm]
        chip_out = rest[nm + 1:2 * nm + 1]
        parts_ref = rest[2 * nm + 1]
        pack, chip_send, chip_recv, p_send, p_recv, p_local = rest[2 * nm + 2:]
        i = pl.program_id(0)
        x_, y_, c_ = _my_pos()
        chip_copies = lambda: _chip_copies(chip_src, chip_out, chip_send, chip_recv)

        @pl.when(i == 0)
        def _():
            pack[...] = jnp.zeros_like(pack)
            for cp in chip_copies():
                cp.start()

        xv = x_ref[...]
        r0 = _rsqrt_mean_sq(xv)
        xh = xv * r0
        dh = _dot_nt(dz_ref[...], win_ref[...])
        _store_chunks(pack, 0, _colsum(dh * xh))
        dxh = dh * g0_ref[...]
        gx_ref[...] = dx1_ref[...] + r0 * (dxh - xh * jnp.mean(dxh * xh, axis=-1, keepdims=True))

        @pl.when(i == n - 1)
        def _():
            for cp in chip_copies():
                cp.wait()
            row0 = pl.multiple_of((4 * x_ + 2 * y_ + c_) * INP_PACK_ROWS, SUB)
            mine = parts_ref.at[pl.ds(row0, INP_PACK_ROWS), :]
            local = pltpu.make_async_copy(pack, mine, p_local)
            local.start()
            sends = []
            for k in range(1, N_DEV):
                peer = tuple(1 - v if (k >> bit) & 1 else v for v, bit in ((x_, 2), (y_, 1), (c_, 0)))
                sends.append(pltpu.make_async_remote_copy(
                    src_ref=pack, dst_ref=mine, send_sem=p_send.at[k - 1], recv_sem=p_recv.at[k - 1],
                    device_id=peer, device_id_type=MESH))
            for cp in sends:
                cp.start()
            for cp in sends:
                cp.wait()
            local.wait()

    tile = lambda w: pl.BlockSpec((tm, w), lambda i: (i, 0))
    return pl.pallas_call(
        body, name="bwd_dx", grid=(n,),
        in_specs=[tile(nc), tile(d), tile(d), _const_spec(win.shape), _const_spec(g0.shape)] + [HBM_SPEC] * nm,
        out_specs=(tile(d),) + (HBM_SPEC,) * (nm + 1),
        out_shape=(jax.ShapeDtypeStruct((t, d), F32),)
        + tuple(jax.ShapeDtypeStruct((3,) + g.shape[1:], g.dtype) for g in chip_srcs)
        + (jax.ShapeDtypeStruct((N_DEV * INP_PACK_ROWS, HEAD), F32),),
        scratch_shapes=[pltpu.VMEM((INP_PACK_ROWS, HEAD), F32),
                        pltpu.SemaphoreType.DMA((nm, 3)), pltpu.SemaphoreType.DMA((nm, 3)),
                        pltpu.SemaphoreType.DMA((N_DEV - 1,)), pltpu.SemaphoreType.DMA((N_DEV - 1,)),
                        pltpu.SemaphoreType.DMA],
        compiler_params=pltpu.CompilerParams(dimension_semantics=("arbitrary",), vmem_limit_bytes=VMEM_LIMIT),
    )(dz, x, dx1, win, g0, *chip_srcs)


ROW_BLOCKS = 8


def _chip_sums(core_idx, grads, recvs, dtypes, name):
    nw = len(grads)

    def body(idx_ref, *refs):
        gs, rs, outs = refs[:nw], refs[nw:2 * nw], refs[2 * nw:]
        for g, r, o in zip(gs, rs, outs):
            o[...] = (g[...] + r[...]).astype(o.dtype)

    def blk(g):
        return (None,) + g.shape[1:]

    return pl.pallas_call(
        body, name=name,
        grid_spec=pltpu.PrefetchScalarGridSpec(
            num_scalar_prefetch=1, grid=(4,),
            in_specs=[pl.BlockSpec(blk(g), lambda q, s: (2 * q + s[0], 0, 0)) for g in grads]
            + [pl.BlockSpec(blk(g), lambda q, s: (q, 0, 0)) for g in grads],
            out_specs=tuple(pl.BlockSpec(blk(g), lambda q, s: (q, 0, 0)) for g in grads)),
        out_shape=tuple(jax.ShapeDtypeStruct((4,) + g.shape[1:], dt) for g, dt in zip(grads, dtypes)),
    )(core_idx, *grads, *recvs)


def _adamw(w, g, m, v):
    m = ADAM_B1 * m + (1.0 - ADAM_B1) * g
    v = ADAM_B2 * v + (1.0 - ADAM_B2) * (g * g)
    m_hat = m / (1.0 - ADAM_B1 ** ADAM_STEP)
    v_hat = v / (1.0 - ADAM_B2 ** ADAM_STEP)
    delta = -ADAM_LR * (m_hat / (jnp.sqrt(v_hat) + ADAM_EPS) + ADAM_WD * w)
    return delta, m, v


def _adam_sections(sec_idx, grads, recv_sib, recv_chips, wmv):
    nw = len(wmv)

    def body(idx_ref, *refs):
        gs, rs, cs = refs[:nw], refs[nw:2 * nw], refs[2 * nw:3 * nw]
        params = refs[3 * nw:6 * nw]
        outs = refs[6 * nw:]
        for w in range(nw):
            g = gs[w][...] + rs[w][...]
            for k in range(3):
                g = g + cs[w][k].astype(F32)
            wv, mv, vv = (params[3 * w + t][...] for t in range(3))
            delta, m_new, v_new = _adamw(wv, g, mv, vv)
            outs[4 * w][...] = g
            outs[4 * w + 1][...] = delta
            outs[4 * w + 2][...] = m_new
            outs[4 * w + 3][...] = v_new

    def rb(g):
        return g.shape[1] // ROW_BLOCKS

    in_specs = [pl.BlockSpec((None, rb(g), g.shape[2]), lambda i, s: (s[0], i, 0)) for g in grads]
    in_specs += [pl.BlockSpec((None, rb(g), g.shape[2]), lambda i, s: (s[1], i, 0)) for g in grads]
    in_specs += [pl.BlockSpec((3, rb(g), g.shape[2]), lambda i, s: (0, i, 0)) for g in grads]
    sec = lambda g: pl.BlockSpec((rb(g), g.shape[2]), lambda i, s: (i, 0))
    for w in range(nw):
        in_specs += [sec(grads[w])] * 3
    out_specs, out_shape = [], []
    for g in grads:
        out_specs += [sec(g)] * 4
        out_shape += [jax.ShapeDtypeStruct(g.shape[1:], F32)] * 4
    flat = [a for t in wmv for a in t]
    outs = pl.pallas_call(
        body, name="adam_sections",
        grid_spec=pltpu.PrefetchScalarGridSpec(num_scalar_prefetch=1, grid=(ROW_BLOCKS,), in_specs=in_specs,
                                               out_specs=tuple(out_specs)),
        out_shape=tuple(out_shape),
    )(sec_idx, *grads, *recv_sib, *recv_chips, *flat)
    return [outs[4 * w:4 * w + 4] for w in range(nw)]


def _adam_small(sec_idx, gpack, parts, parts_wmv, vec_params, mat_params, conv_wmv):
    items = [(0, parts_wmv, "parts")]
    items += [(r0, t, "vec") for r0, t in vec_params] + [(r0, t, "mat") for r0, t in mat_params]
    items.append((_SEQ_ROWS["dcw"][0], conv_wmv, "conv"))

    def body(idx_ref, g_ref, parts_ref, *refs):
        ins, outs = refs[:3 * len(items)], refs[3 * len(items):]
        for n, (r0, _, kind) in enumerate(items):
            w_ref, m_ref, v_ref = ins[3 * n:3 * n + 3]
            o = outs[4 * n:4 * n + 4]
            if kind == "mat":
                g = g_ref[pl.ds(r0, w_ref.shape[0]), :]
                res = (g,) + _adamw(w_ref[...], g, m_ref[...], v_ref[...])
                for ref, val in zip(o, res):
                    ref[...] = val
            elif kind in ("vec", "parts"):
                for h in range(w_ref.shape[1] // HEAD):
                    cols = slice(h * HEAD, (h + 1) * HEAD)
                    if kind == "vec":
                        g = g_ref[pl.ds(r0 + h, 1), :]
                    else:
                        g = parts_ref[pl.ds(h, 1), :]
                        for dev in range(1, N_DEV):
                            g = g + parts_ref[pl.ds(SUB * dev + h, 1), :]
                    res = (g,) + _adamw(w_ref[:, cols], g, m_ref[:, cols], v_ref[:, cols])
                    for ref, val in zip(o, res):
                        ref[:, cols] = val
            else:
                rows = lax.broadcasted_iota(jnp.int32, (SUB, HEAD), 0)
                for k in range(4):
                    blk = g_ref[pl.ds(r0 + SUB * k, SUB), :]
                    g = jnp.sum(jnp.where(rows == idx_ref[0], blk, 0.0), axis=0, keepdims=True)
                    row = pl.ds(k, 1)
                    res = (g,) + _adamw(w_ref[row, :], g, m_ref[row, :], v_ref[row, :])
                    for ref, val in zip(o, res):
                        ref[row, :] = val
        outs[4 * len(items)][...] = g_ref[pl.ds(_OFF_MID + MID_LOSS_ROW, 1), :]

    flat = [a for _, t, _ in items for a in t]
    full = lambda a: pl.BlockSpec(a.shape, lambda i, s: (0,) * a.ndim)
    out_specs, out_shape = [], []
    for _, t, _ in items:
        out_specs += [full(t[0])] * 4
        out_shape += [jax.ShapeDtypeStruct(t[0].shape, F32)] * 4
    loss_row = jax.ShapeDtypeStruct((1, HEAD), F32)
    out_specs.append(full(loss_row))
    out_shape.append(loss_row)
    outs = pl.pallas_call(
        body, name="adam_small",
        grid_spec=pltpu.PrefetchScalarGridSpec(num_scalar_prefetch=1, grid=(1,),
                                               in_specs=[full(gpack), full(parts)] + [full(a) for a in flat],
                                               out_specs=tuple(out_specs)),
        out_shape=tuple(out_shape),
    )(sec_idx, gpack, parts, *flat)
    return [outs[4 * n:4 * n + 4] for n in range(len(items))], outs[4 * len(items)]


def _col_slicer(width):
    return lambda ref, idx: ref.at[:, pl.ds(pl.multiple_of(idx * width, HEAD), width)]


def _row_slicer(rows):
    return lambda ref, idx: ref.at[pl.ds(pl.multiple_of(idx * rows, SUB), rows), :]


def kernel(x, p, norm_g, w_in, conv_w, conv_b, lru_w_a, lru_b_a, lru_w_x, lru_b_x, lru_lambda, pool_w, pool_scale, w_proj_lru, w_proj_pool, w_out, ple_norm_g, w_ple_gate, w_ple_proj, final_g, loss_target, m_norm_g, m_w_in, m_conv_w, m_conv_b, m_lru_w_a, m_lru_b_a, m_lru_w_x, m_lru_b_x, m_lru_lambda, m_pool_w, m_pool_scale, m_w_proj_lru, m_w_proj_pool, m_w_out, m_ple_norm_g, m_w_ple_gate, m_w_ple_proj, m_final_g, v_norm_g, v_w_in, v_conv_w, v_conv_b, v_lru_w_a, v_lru_b_a, v_lru_w_x, v_lru_b_x, v_lru_lambda, v_pool_w, v_pool_scale, v_w_proj_lru, v_w_proj_pool, v_w_out, v_ple_norm_g, v_w_ple_gate, v_w_ple_proj, v_final_g):
    nb, s, d = x.shape
    t = nb * s
    tm = min(TILE_M, s)
    sec_idx = (4 * lax.axis_index("x") + 2 * lax.axis_index("y") + lax.axis_index("c")).astype(jnp.int32)
    chip_idx = (2 * lax.axis_index("x") + lax.axis_index("y")).astype(jnp.int32)
    core_idx = lax.axis_index("c").astype(jnp.int32)

    h, ht, win, cw = _pre(
        x, norm_g, [w_in[0], conv_w[0]], [True, False],
        [(d, N_DEV * w_in.shape[2]), (conv_w.shape[1], N_DEV * conv_w.shape[2])],
        [_col_slicer(w_in.shape[2]), _col_slicer(conv_w.shape[2])], tm)
    late_shards = [w_proj_lru[0], w_proj_pool[0], w_out[0], w_ple_gate[0], w_ple_proj[0]]
    late_shapes = [(N_DEV * w_proj_lru.shape[1], d), (w_proj_pool.shape[1], N_DEV * w_proj_pool.shape[2]),
                   (N_DEV * w_out.shape[1], d), (N_DEV * w_ple_gate.shape[1], d),
                   (w_ple_proj.shape[1], N_DEV * w_ple_proj.shape[2])]
    late_slicers = [_row_slicer(w_proj_lru.shape[1]), _col_slicer(w_proj_pool.shape[2]), _row_slicer(w_out.shape[1]),
                    _row_slicer(w_ple_gate.shape[1]), _col_slicer(w_ple_proj.shape[2])]

    lw = cw.shape[1]
    pwid = pool_scale.shape[1]
    wa = lru_w_a[0].astype(_MM)
    wx = lru_w_x[0].astype(_MM)
    pw = pool_w[0].astype(_MM)
    ba, bx = lru_b_a[0], lru_b_x[0]
    gf = final_g.reshape(1, d)
    core = core_idx.reshape(1)

    fwd_out = _fwd_seq(
        h, win, cw, conv_b, wa, ba, wx, bx, lru_lambda, pw, pool_scale, late_shards, late_shapes, late_slicers, tm)
    z, hl, ya, yb = fwd_out[:4]
    saved = fwd_out[4:N_SEQ_SAVED]
    wpl, wpp, wout, wpg, wpe = fwd_out[N_SEQ_SAVED:]
    x2d = x.reshape(t, d)
    mid_out = _mid(
        ya.reshape(t, lw), yb.reshape(t, pwid), h.reshape(t, d), win, x2d, p[0].reshape(t, -1),
        loss_target.reshape(t, d), wpl, wpp, wout, wpg, wpe, ple_norm_g, gf, tm)
    dya, dyb, dm, dx1 = mid_out[:4]
    grads_mid, pack_mid, sib_mid = list(mid_out[4:9]), mid_out[9], list(mid_out[10:])
    sums_mid = _chip_sums(core, grads_mid, sib_mid, [_WIRE] * len(grads_mid), "chip_sums_mid")
    seq_out = _bwd_seq(z, hl, saved, dya.reshape(nb, s, lw), dyb.reshape(nb, s, pwid), dm.reshape(nb, s, -1), cw,
                       wa, wx, lru_lambda, pw, pool_scale, pack_mid, sums_mid, tm)
    dz, chips_mid = seq_out[0], list(seq_out[2:])
    small = seq_out[1].reshape(N_DEV, SMALL_SEC, HEAD)
    g_in, sib_in, _, _, gsmall = _bwd_win(ht, dz, small)
    (sums_in,) = _chip_sums(core, [g_in], [sib_in], [_WIRE], "chip_sums_in")
    grad_x, chips_in, g0_parts = _bwd_dx(dz.reshape(t, -1), x2d, dx1, win, norm_g, [sums_in], tm)

    grads = [g_in] + grads_mid
    recv_sib = [sib_in] + sib_mid
    recv_chips = [chips_in] + chips_mid
    wmv = [(w_in[0], m_w_in[0], v_w_in[0]), (w_proj_lru[0], m_w_proj_lru[0], v_w_proj_lru[0]),
           (w_proj_pool[0], m_w_proj_pool[0], v_w_proj_pool[0]), (w_out[0], m_w_out[0], v_w_out[0]),
           (w_ple_gate[0], m_w_ple_gate[0], v_w_ple_gate[0]), (w_ple_proj[0], m_w_ple_proj[0], v_w_ple_proj[0])]
    big = _adam_sections(jnp.stack([sec_idx, chip_idx]), grads, recv_sib, recv_chips, wmv)

    R = _SEQ_ROWS
    vec = lambda r0, *t: (r0, tuple(a.reshape(1, -1) for a in t))
    mat = lambda r0, *t: (r0, tuple(a.reshape(-1, HEAD) for a in t))
    norm_wmv = vec(0, norm_g, m_norm_g, v_norm_g)[1]
    vec_params = [vec(R["dcb"][0], conv_b, m_conv_b, v_conv_b),
                  vec(R["dlam"][0], lru_lambda, m_lru_lambda, v_lru_lambda),
                  vec(R["dps"][0], pool_scale, m_pool_scale, v_pool_scale),
                  vec(_OFF_MID, ple_norm_g, m_ple_norm_g, v_ple_norm_g),
                  vec(_OFF_MID + SUB, final_g, m_final_g, v_final_g)]
    mat_params = [mat(R["dwa"][0], lru_w_a, m_lru_w_a, v_lru_w_a), mat(R["dba"][0], lru_b_a, m_lru_b_a, v_lru_b_a),
                  mat(R["dwx"][0], lru_w_x, m_lru_w_x, v_lru_w_x), mat(R["dbx"][0], lru_b_x, m_lru_b_x, v_lru_b_x),
                  mat(R["dpw"][0], pool_w, m_pool_w, v_pool_w)]
    conv_wmv = (conv_w[0], m_conv_w[0], v_conv_w[0])
    small_out, loss_row = _adam_small(sec_idx.reshape(1), gsmall, g0_parts, norm_wmv, vec_params, mat_params, conv_wmv)
    loss = loss_row[0, 0]

    res = {}
    names_small = ["norm_g", "conv_b", "lru_lambda", "pool_scale", "ple_norm_g", "final_g",
                   "lru_w_a", "lru_b_a", "lru_w_x", "lru_b_x", "pool_w", "conv_w"]
    shapes = {"norm_g": norm_g, "conv_b": conv_b, "lru_lambda": lru_lambda, "pool_scale": pool_scale,
              "ple_norm_g": ple_norm_g, "final_g": final_g, "lru_w_a": lru_w_a, "lru_b_a": lru_b_a, "lru_w_x": lru_w_x,
              "lru_b_x": lru_b_x, "pool_w": pool_w, "conv_w": conv_w}
    for name, quad in zip(names_small, small_out):
        res[name] = [a.reshape(shapes[name].shape) for a in quad]
    names_big = ["w_in", "w_proj_lru", "w_proj_pool", "w_out", "w_ple_gate", "w_ple_proj"]
    for name, quad, (w, _, _) in zip(names_big, big, wmv):
        res[name] = [a.reshape((1,) + w.shape) for a in quad]
    order = ["norm_g", "w_in", "conv_w", "conv_b", "lru_w_a", "lru_b_a", "lru_w_x", "lru_b_x", "lru_lambda", "pool_w",
             "pool_scale", "w_proj_lru", "w_proj_pool", "w_out", "ple_norm_g", "w_ple_gate", "w_ple_proj", "final_g"]
    out = [loss, grad_x.reshape(nb, s, d)]
    for kind in range(4):
        out += [res[name][kind] for name in order]
    return tuple(out)
```

```python
import jax
import jax.numpy as jnp
from jax import lax
from jax.experimental import pallas as pl
from jax.experimental.pallas import tpu as pltpu

F32 = jnp.float32
_MM = jnp.bfloat16
_WIRE = jnp.bfloat16

EPS = 1e-6
LRU_C = 8.0
MULT_SQ_FLOOR = 1e-30
POOL_WINDOWS = (2, 4, 8, 16)
N_HEADS = 8
HEAD = 128
HALO = 16
SUB = 8

ADAM_LR = 0.001
ADAM_B1 = 0.9
ADAM_B2 = 0.999
ADAM_EPS = 1e-08
ADAM_WD = 0.01
ADAM_STEP = 10

N_DEV = 8
MESH = pl.DeviceIdType.MESH
VMEM_LIMIT = 60 * 1024 * 1024
TILE_M = 256
N_SEQ_SAVED = 10

_SEQ_ROWS = {"dwa": (0, 1024), "dwx": (1024, 1024), "dpw": (2048, 512), "dba": (2560, 8), "dbx": (2568, 8),
             "dcb": (2576, 8), "dlam": (2584, 8), "dcw": (2592, 32), "dps": (2624, 8)}
SEQ_PACK_ROWS = 2632
MID_PACK_ROWS = 24
MID_LOSS_ROW = 16
INP_PACK_ROWS = 8
SMALL_ROWS = 3072
SMALL_SEC = SMALL_ROWS // N_DEV
_OFF_MID = SEQ_PACK_ROWS


def _dot(a, b):
    return jnp.dot(a, b, preferred_element_type=F32)


def _dot_nt(a, b):
    return lax.dot_general(a, b, (((1,), (1,)), ((), ())), preferred_element_type=F32)


def _dot_tn(a, b):
    return lax.dot_general(a, b, (((0,), (0,)), ((), ())), preferred_element_type=F32)


def _sigmoid(v):
    return 0.5 * jnp.tanh(0.5 * v) + 0.5


def _lru_mult(log_a, a):
    m2 = jnp.maximum(-jnp.tanh(log_a) * (1.0 + a * a), MULT_SQ_FLOOR)
    inv = lax.rsqrt(m2)
    return m2 * inv, inv


def _rows_back(v, n):
    return pltpu.roll(v, n, 0) if n else v


def _rows_ahead(v, n):
    return pltpu.roll(v, v.shape[0] - n, 0) if n else v


def _softplus_neg(lam):
    e = jnp.exp(-jnp.abs(lam))
    w = 1.0 + e
    l1p = jnp.where(w == 1.0, e, jnp.log(w) * (e / (w - 1.0)))
    return jnp.maximum(-lam, 0.0) + l1p


def _rsqrt_mean_sq(v):
    return lax.rsqrt(jnp.mean(v * v, axis=-1, keepdims=True) + EPS)


def _colsum(v):
    return jnp.sum(v, axis=0, keepdims=True)


def _const_spec(shape):
    nd = len(shape)
    return pl.BlockSpec(shape, lambda *_: (0,) * nd, pipeline_mode=pl.Buffered(1))


HBM_SPEC = pl.BlockSpec(memory_space=pl.ANY)


def _my_pos():
    return lax.axis_index("x"), lax.axis_index("y"), lax.axis_index("c")


def _other_chips(x, y):
    return [(1 - x, y), (x, 1 - y), (1 - x, 1 - y)]


def _gather_sems(nw):
    return [pltpu.SemaphoreType.DMA((nw, 7)), pltpu.SemaphoreType.DMA((nw, 7)), pltpu.SemaphoreType.DMA((nw,))]


def _gather_phases(srcs, outs, slicers, send_sems, recv_sems, local_sems, relay_diagonal=False):
    n_direct = 2 if relay_diagonal else 3
    nw = len(srcs)
    x, y, c = _my_pos()
    me, sibling = (x, y, c), (x, y, 1 - c)
    chips = _other_chips(x, y)

    def part(w, pos):
        return slicers[w](outs[w], 4 * pos[0] + 2 * pos[1] + pos[2])

    def copy(w, k, block, to, src=None):
        return pltpu.make_async_remote_copy(
            src_ref=part(w, block) if src is None else src, dst_ref=part(w, block),
            send_sem=send_sems.at[w, k], recv_sem=recv_sems.at[w, k], device_id=to, device_id_type=MESH)

    def mine():
        return [pltpu.make_async_copy(srcs[w], part(w, me), local_sems.at[w]) for w in range(nw)]

    def first():
        out = []
        for w in range(nw):
            out.append(copy(w, 0, me, sibling, src=srcs[w]))
            out += [copy(w, 1 + j, me, (*chips[j], c), src=srcs[w]) for j in range(n_direct)]
        return out

    def relay(w, j):
        return copy(w, 3, (*chips[j], c), (*chips[1 - j], c))

    def relays():
        return [relay(w, 0) for w in range(nw)] if relay_diagonal else []

    def passed():
        return [copy(w, 4 + j, (*chip, c), sibling) for j, chip in enumerate(chips) for w in range(nw)]

    def start():
        for cp in mine() + first():
            cp.start()

    def forward():
        for j, chip in enumerate(chips):
            for w in range(nw):
                copy(w, 1 + j, (*chip, c), me).wait_recv()
                copy(w, 4 + j, (*chip, c), sibling).start()
            if relay_diagonal and j < 2:
                @pl.when(c == j)
                def _():
                    for w in range(nw):
                        relay(w, j).start()

    def finish():
        for w in range(nw):
            copy(w, 0, sibling, me).wait_recv()
            for j, chip in enumerate(chips):
                copy(w, 4 + j, (*chip, 1 - c), me).wait_recv()
        for cp in first() + relays() + passed():
            cp.wait_send()
        for cp in mine():
            cp.wait()

    return start, forward, finish


def _sibling_copies(srcs, outs, send_sems, recv_sems):
    x, y, c = _my_pos()
    return [pltpu.make_async_remote_copy(
        src_ref=srcs[w].at[2 * q + (1 - c)], dst_ref=outs[w].at[q], send_sem=send_sems.at[w, q],
        recv_sem=recv_sems.at[w, q], device_id=(x, y, 1 - c), device_id_type=MESH)
        for w in range(len(srcs)) for q in range(4)]


def _chip_copies(srcs, outs, send_sems, recv_sems, which=None):
    x, y, c = _my_pos()
    return [pltpu.make_async_remote_copy(
        src_ref=srcs[w].at[2 * px + py], dst_ref=outs[w].at[k], send_sem=send_sems.at[w, k],
        recv_sem=recv_sems.at[w, k], device_id=(px, py, c), device_id_type=MESH)
        for w in (range(len(srcs)) if which is None else which) for k, (px, py) in enumerate(_other_chips(x, y))]


def _lru_gates(xc, h, wa_ref, ba_ref, wx_ref, bx_ref, neg_c_sp):
    xc16 = xc.astype(_MM)
    r = _sigmoid(_dot(xc16, wa_ref[h]) + ba_ref[pl.ds(h, 1), :])
    i = _sigmoid(_dot(xc16, wx_ref[h]) + bx_ref[pl.ds(h, 1), :])
    return r, i, neg_c_sp * r


def _conv_head(xa_ext, cw_ref, cb_ref, cols, tm):
    ext = xa_ext[:, cols]
    xc = cb_ref[:, cols] + cw_ref[pl.ds(3, 1), cols] * ext[HALO:HALO + tm]
    for k in range(3):
        xc = xc + cw_ref[pl.ds(k, 1), cols] * _rows_back(ext, 3 - k)[HALO:HALO + tm]
    return xc


def _window_sum(ext, k, shift):
    n = 1
    while n < k:
        ext = ext + shift(ext, n)
        n *= 2
    return ext


def _pool_diff(xb_ext, g, k, pos, tm):
    cols = slice(g * HEAD, (g + 1) * HEAD)
    ext = xb_ext[:, cols]
    ws = _window_sum(ext, k, _rows_back)[HALO:HALO + tm]
    inv_cnt = 1.0 / jnp.minimum(pos + 1, k).astype(F32)
    return ws * inv_cnt - ext[HALO:HALO + tm], inv_cnt


def _staged_sources(shards, as_operand):
    in_specs = [_const_spec(a.shape) if c else HBM_SPEC for a, c in zip(shards, as_operand)]
    stages = [pltpu.VMEM(a.shape, _MM) for a, c in zip(shards, as_operand) if c]
    dtypes = [_MM if c else a.dtype for a, c in zip(shards, as_operand)]
    return in_specs, stages, dtypes


def _gather_sources(srcs, stages, as_operand):
    stages = iter(stages)
    return [next(stages) if c else src for src, c in zip(srcs, as_operand)]


def _fill_stages(srcs, staged, as_operand):
    for src, dst, c in zip(srcs, staged, as_operand):
        if c:
            dst[...] = src[...].astype(dst.dtype)


def _pre(x, g0, shards, as_operand, shapes, slicers, tm):
    nb, s, d = x.shape
    nt = s // tm
    nl = len(shards)
    shard_specs, stage_shapes, wire = _staged_sources(shards, as_operand)

    def body(x_ref, g0_ref, *rest):
        src, rest = rest[:nl], rest[nl:]
        h_ref, ht_ref = rest[:2]
        out, rest = rest[2:2 + nl], rest[2 + nl:]
        (send_sems, recv_sems, local_sems), stages = rest[:3], rest[3:]
        step_no = pl.program_id(0) * nt + pl.program_id(1)
        staged = _gather_sources(src, stages, as_operand)
        g_start, g_forward, g_finish = _gather_phases(staged, out, slicers, send_sems, recv_sems, local_sems,
                                                      relay_diagonal=True)

        @pl.when(step_no == 0)
        def _():
            _fill_stages(src, staged, as_operand)
            g_start()

        xv = x_ref[...]
        h16 = ((xv * _rsqrt_mean_sq(xv)) * g0_ref[...]).astype(_MM)
        h_ref[...] = h16
        ht_ref[...] = h16.T

        @pl.when(step_no == nb * nt - 1)
        def _():
            g_forward()
            g_finish()

    tile = pl.BlockSpec((None, tm, d), lambda b, j: (b, j, 0))
    return pl.pallas_call(
        body, name="pre", grid=(nb, nt),
        in_specs=[tile, _const_spec(g0.shape)] + shard_specs,
        out_specs=(tile, pl.BlockSpec((None, d, tm), lambda b, j: (b, 0, j))) + (HBM_SPEC,) * nl,
        out_shape=(jax.ShapeDtypeStruct((nb, s, d), _MM), jax.ShapeDtypeStruct((nb, d, s), _MM))
        + tuple(jax.ShapeDtypeStruct(shp, dt) for shp, dt in zip(shapes, wire)),
        scratch_shapes=_gather_sems(nl) + stage_shapes,
        compiler_params=pltpu.CompilerParams(dimension_semantics=("arbitrary", "arbitrary")),
    )(x, g0, *shards)


def _fwd_seq(h, win, cw, cb, wa, ba, wx, bx, lam, pw, ps, late_shards, late_shapes, late_slicers, tm):
    nb, s, d = h.shape
    lw = cw.shape[1]
    pwid = ps.shape[1]
    nc = 2 * lw + 2 * pwid
    nt = s // tm
    nl = len(late_shards)
    n_steps = nb * nt
    as_operand = [True] * nl
    shard_specs, stage_shapes, wire = _staged_sources(late_shards, as_operand)

    def body(h_ref, win_ref, cw_ref, cb_ref, wa_ref, ba_ref, wx_ref, bx_ref, lam_ref, pw_ref, ps_ref, *rest):
        late_src, rest = rest[:nl], rest[nl:]
        z_ref, hl_ref, ya_ref, yb_ref, xc_ref, r_ref, i_ref, a_buf, diff_ref, yp_ref = rest[:N_SEQ_SAVED]
        late_out, rest = rest[N_SEQ_SAVED:N_SEQ_SAVED + nl], rest[N_SEQ_SAVED + nl:]
        (xa_ext, xb_ext, carry, send_sems, recv_sems, local_sems), stages = rest[:6], rest[6:]
        j = pl.program_id(1)
        step_no = pl.program_id(0) * nt + j
        staged = _gather_sources(late_src, stages, as_operand)
        g_start, g_forward, g_finish = _gather_phases(staged, late_out, late_slicers, send_sems, recv_sems, local_sems)

        @pl.when(step_no == 0)
        def _():
            _fill_stages(late_src, staged, as_operand)
            g_start()

        pl.when(step_no == (3 * n_steps) // 4)(g_forward)

        @pl.when(j == 0)
        def _():
            xa_ext[pl.ds(0, HALO), :] = jnp.zeros((HALO, lw), F32)
            xb_ext[pl.ds(0, HALO), :] = jnp.zeros((HALO, pwid), F32)
            carry[...] = jnp.zeros_like(carry)

        h16 = h_ref[...]
        xa = _dot(h16, win_ref[:, 0:lw])
        z_ref[:, 0:lw] = xa
        xa_ext[pl.ds(HALO, tm), :] = xa
        for c0 in range(lw, nc, lw):
            z_ref[:, c0:c0 + lw] = _dot(h16, win_ref[:, c0:c0 + lw])

        for h in range(N_HEADS):
            cols = slice(h * HEAD, (h + 1) * HEAD)
            xc = _conv_head(xa_ext, cw_ref, cb_ref, cols, tm)
            neg_c_sp = -LRU_C * _softplus_neg(lam_ref[:, cols])
            r, i, log_a = _lru_gates(xc, h, wa_ref, ba_ref, wx_ref, bx_ref, neg_c_sp)
            a = jnp.exp(log_a)
            xc_ref[:, cols] = xc
            r_ref[:, cols] = r
            i_ref[:, cols] = i
            a_buf[:, cols] = a
            hl_ref[:, cols] = _lru_mult(log_a, a)[0] * (i * xc)

        rows = lax.broadcasted_iota(jnp.int32, (SUB, lw), 0)

        def step(c, car):
            i0 = pl.multiple_of(c * SUB, SUB)
            av = a_buf[pl.ds(i0, SUB), :]
            bv = hl_ref[pl.ds(i0, SUB), :]
            for sh in (1, 2, 4):
                m = rows >= sh
                a_sh = jnp.where(m, pltpu.roll(av, sh, 0), 1.0)
                b_sh = jnp.where(m, pltpu.roll(bv, sh, 0), 0.0)
                bv = av * b_sh + bv
                av = av * a_sh
            hv = av * car + bv
            hl_ref[pl.ds(i0, SUB), :] = hv
            return jnp.broadcast_to(hv[SUB - 1:SUB, :], (SUB, lw))

        carry[...] = lax.fori_loop(0, tm // SUB, step, carry[...])

        ga = z_ref[:, lw:2 * lw]
        ya_ref[...] = (hl_ref[...] * (ga * _sigmoid(ga))).astype(ya_ref.dtype)

        xb_ext[pl.ds(HALO, tm), :] = z_ref[:, 2 * lw:2 * lw + pwid]
        pos = j * tm + lax.broadcasted_iota(jnp.int32, (tm, HEAD), 0)
        for g, k in enumerate(POOL_WINDOWS):
            cols = slice(g * HEAD, (g + 1) * HEAD)
            diff16 = _pool_diff(xb_ext, g, k, pos, tm)[0].astype(_MM)
            yp = _dot(diff16, pw_ref[g])
            diff_ref[:, cols] = diff16
            yp_ref[:, cols] = yp
            gb = z_ref[:, 2 * lw + pwid + g * HEAD:2 * lw + pwid + (g + 1) * HEAD]
            yb_ref[:, cols] = ((yp * ps_ref[:, cols]) * (gb * _sigmoid(gb))).astype(yb_ref.dtype)

        xa_ext[pl.ds(0, HALO), :] = xa_ext[pl.ds(tm, HALO), :]
        xb_ext[pl.ds(0, HALO), :] = xb_ext[pl.ds(tm, HALO), :]
        pl.when(step_no == n_steps - 1)(g_finish)

    tile = lambda w: pl.BlockSpec((None, tm, w), lambda b, j: (b, j, 0))
    return pl.pallas_call(
        body, name="fwd_seq", grid=(nb, nt),
        in_specs=[tile(d), _const_spec((d, nc)), _const_spec(cw.shape), _const_spec(cb.shape),
                  _const_spec(wa.shape), _const_spec(ba.shape), _const_spec(wx.shape), _const_spec(bx.shape),
                  _const_spec(lam.shape), _const_spec(pw.shape), _const_spec(ps.shape)] + shard_specs,
        out_specs=(tile(nc), tile(lw), tile(lw), tile(pwid), tile(lw), tile(lw), tile(lw), tile(lw), tile(pwid),
                   tile(pwid)) + (HBM_SPEC,) * nl,
        out_shape=(jax.ShapeDtypeStruct((nb, s, nc), F32), jax.ShapeDtypeStruct((nb, s, lw), F32),
                   jax.ShapeDtypeStruct((nb, s, lw), _MM), jax.ShapeDtypeStruct((nb, s, pwid), _MM))
        + (jax.ShapeDtypeStruct((nb, s, lw), F32),) * 4
        + (jax.ShapeDtypeStruct((nb, s, pwid), _MM), jax.ShapeDtypeStruct((nb, s, pwid), F32))
        + tuple(jax.ShapeDtypeStruct(shp, dt) for shp, dt in zip(late_shapes, wire)),
        scratch_shapes=[pltpu.VMEM((tm + HALO, lw), F32), pltpu.VMEM((tm + HALO, pwid), F32),
                        pltpu.VMEM((SUB, lw), F32)] + _gather_sems(nl) + stage_shapes,
        compiler_params=pltpu.CompilerParams(dimension_semantics=("arbitrary", "arbitrary"), vmem_limit_bytes=VMEM_LIMIT),
    )(h, win, cw, cb, wa, ba, wx, bx, lam, pw, ps, *late_shards)


def _store_chunks(ref, row0, vec):
    for h in range(vec.shape[1] // HEAD):
        ref[pl.ds(row0 + h, 1), :] += vec[:, h * HEAD:(h + 1) * HEAD]


def _mid(ya, yb, h, win, x, p, tgt, wpl, wpp, wout, wpg, wpe, g1, gf, tm):
    t, d = x.shape
    lw = ya.shape[1]
    pwid = yb.shape[1]
    pdim = p.shape[1]
    n = t // tm
    sec_pp = wpp.shape[1] // N_DEV
    sec_pe = wpe.shape[1] // N_DEV
    sec_rows = wpl.shape[0] // N_DEV
    grad_secs = [(sec_rows, d), (wpp.shape[0], sec_pp), (sec_rows, d), (sec_rows, d), (wpe.shape[0], sec_pe)]
    n_win_blocks = win.shape[1] // d
    win_cols = lambda c: pl.BlockSpec((d, d), lambda i: (0, c), pipeline_mode=pl.Buffered(1))

    def body(ya_ref, yb_ref, h_ref, wma_ref, wmb_ref, x_ref, p_ref, tgt_ref, wpl_ref, wpp_ref, wout_ref, wpg_ref, wpe_ref,
             g1_ref, gf_ref,
             dya_ref, dyb_ref, dm_ref, dx1_ref, gpl_ref, gpp_ref, gout_ref, gpg_ref, gpe_ref, pack_ref,
             sib_pl, sib_pp, sib_out, sib_pg, sib_pe,
             acc_pl, acc_pp, acc_out, acc_pg, acc_pe, out_sems, sib_send, sib_recv):
        i = pl.program_id(0)

        @pl.when(i == 0)
        def _():
            for acc in (acc_pl, acc_pp, acc_out, acc_pg, acc_pe):
                acc[...] = jnp.zeros_like(acc)
            pack_ref[...] = jnp.zeros_like(pack_ref)

        ya16 = ya_ref[...]
        yb16 = yb_ref[...]
        br_a = _dot(ya16, wpl_ref[...])
        br_b = _dot(yb16, wpp_ref[...])
        h16 = h_ref[...]
        sa = _sigmoid(_dot(h16, wma_ref[...]))
        sb = _sigmoid(_dot(h16, wmb_ref[...]))
        mg16 = (sa * br_a + sb * br_b).astype(_MM)
        x1 = x_ref[...] + _dot(mg16, wout_ref[...])
        r1 = _rsqrt_mean_sq(x1)
        n1 = x1 * r1
        h116 = (n1 * g1_ref[...]).astype(_MM)
        gate = _sigmoid(_dot(h116, wpg_ref[...]))
        p16 = p_ref[...].astype(_MM)
        pe = _dot(p16, wpe_ref[...])
        x2 = x1 + gate * pe
        r2 = _rsqrt_mean_sq(x2)
        n2 = x2 * r2
        err = n2 * gf_ref[...] - tgt_ref[...]
        sq = jnp.sum(_colsum(err * err), axis=1, keepdims=True)
        pack_ref[pl.ds(MID_LOSS_ROW, 1), :] += jnp.broadcast_to(sq * (0.5 / d), (1, HEAD))

        dy = err * (1.0 / d)
        _store_chunks(pack_ref, SUB, _colsum(dy * n2))
        dn2 = dy * gf_ref[...]
        dx2 = r2 * (dn2 - n2 * jnp.mean(dn2 * n2, axis=-1, keepdims=True))
        dpe16 = (dx2 * gate).astype(_MM)
        dpg16 = ((dx2 * pe) * (gate * (1.0 - gate))).astype(_MM)
        acc_pe[...] += _dot_tn(p16, dpe16)
        acc_pg[...] += _dot_tn(h116, dpg16)
        dh1 = _dot_nt(dpg16, wpg_ref[...])
        _store_chunks(pack_ref, 0, _colsum(dh1 * n1))
        dn1 = dh1 * g1_ref[...]
        dx1 = dx2 + r1 * (dn1 - n1 * jnp.mean(dn1 * n1, axis=-1, keepdims=True))
        dx1_ref[...] = dx1
        dx116 = dx1.astype(_MM)
        acc_out[...] += _dot_tn(mg16, dx116)
        dmg = _dot_nt(dx116, wout_ref[...])
        da16 = (dmg * sa).astype(_MM)
        db16 = (dmg * sb).astype(_MM)
        dm_ref[:, 0:d] = ((dmg * br_a) * (sa * (1.0 - sa))).astype(dm_ref.dtype)
        dm_ref[:, d:2 * d] = ((dmg * br_b) * (sb * (1.0 - sb))).astype(dm_ref.dtype)
        acc_pl[...] += _dot_tn(ya16, da16)
        acc_pp[...] += _dot_tn(yb16, db16)
        dya_ref[...] = _dot_nt(da16, wpl_ref[...])
        dyb_ref[...] = _dot_nt(db16, wpp_ref[...])

        @pl.when(i == n - 1)
        def _():
            px, py, c = _my_pos()
            accs = (acc_pl, acc_pp, acc_out, acc_pg, acc_pe)
            outs = (gpl_ref, gpp_ref, gout_ref, gpg_ref, gpe_ref)
            sibs = (sib_pl, sib_pp, sib_out, sib_pg, sib_pe)

            def section(w, k):
                if w == 1:
                    return acc_pp.at[:, pl.ds(k * sec_pp, sec_pp)]
                if w == 4:
                    return acc_pe.at[:, pl.ds(k * sec_pe, sec_pe)]
                return accs[w].at[pl.ds(k * sec_rows, sec_rows), :]

            for cc in range(2):
                @pl.when(c == cc)
                def _(cc=cc):
                    copies = []
                    for w in range(5):
                        for q in range(4):
                            copies.append(pltpu.make_async_remote_copy(
                                src_ref=section(w, 2 * q + 1 - cc), dst_ref=sibs[w].at[q], send_sem=sib_send.at[w, q],
                                recv_sem=sib_recv.at[w, q], device_id=(px, py, 1 - cc), device_id_type=MESH))
                            copies.append(pltpu.make_async_copy(
                                section(w, 2 * q + cc), outs[w].at[2 * q + cc], out_sems.at[4 * w + q]))
                    for cp in copies:
                        cp.start()
                    for cp in copies:
                        cp.wait()

    tile = lambda w: pl.BlockSpec((tm, w), lambda i: (i, 0))
    any_spec = pl.BlockSpec(memory_space=pl.ANY)
    full = lambda shape: pl.BlockSpec(shape, lambda i: (0,) * len(shape))
    return pl.pallas_call(
        body, name="mid", grid=(n,),
        in_specs=[tile(lw), tile(pwid), tile(d), win_cols(n_win_blocks - 2), win_cols(n_win_blocks - 1),
                  tile(d), tile(pdim), tile(d),
                  _const_spec(wpl.shape), _const_spec(wpp.shape), _const_spec(wout.shape), _const_spec(wpg.shape),
                  _const_spec(wpe.shape), _const_spec(g1.shape), _const_spec(gf.shape)],
        out_specs=(tile(lw), tile(pwid), tile(2 * d), tile(d), any_spec, any_spec, any_spec, any_spec, any_spec,
                   full((MID_PACK_ROWS, HEAD))) + (any_spec,) * 5,
        out_shape=(jax.ShapeDtypeStruct((t, lw), F32), jax.ShapeDtypeStruct((t, pwid), F32),
                   jax.ShapeDtypeStruct((t, 2 * d), _MM), jax.ShapeDtypeStruct((t, d), F32),
                   *(jax.ShapeDtypeStruct((N_DEV,) + sec, F32) for sec in grad_secs),
                   jax.ShapeDtypeStruct((MID_PACK_ROWS, HEAD), F32),
                   *(jax.ShapeDtypeStruct((N_DEV // 2,) + sec, F32) for sec in grad_secs)),
        scratch_shapes=[pltpu.VMEM(wpl.shape, F32), pltpu.VMEM(wpp.shape, F32), pltpu.VMEM(wout.shape, F32),
                        pltpu.VMEM(wpg.shape, F32), pltpu.VMEM(wpe.shape, F32),
                        pltpu.SemaphoreType.DMA((5 * N_DEV // 2,)), pltpu.SemaphoreType.DMA((5, N_DEV // 2)),
                        pltpu.SemaphoreType.DMA((5, N_DEV // 2))],
        compiler_params=pltpu.CompilerParams(dimension_semantics=("arbitrary",), vmem_limit_bytes=VMEM_LIMIT),
    )(ya, yb, h, win, win, x, p, tgt, wpl, wpp, wout, wpg, wpe, g1, gf)


def _bwd_seq(z, hl, saved, dya, dyb, dm, cw, wa, wx, lam, pw, ps, pack_mid, sib_grads, tm):
    nb, s, zw = z.shape
    lw = cw.shape[1]
    pwid = ps.shape[1]
    nt = s // tm
    nc = zw + dm.shape[2]
    R = _SEQ_ROWS
    ns = len(sib_grads)

    def body(zq_ref, hl_ref, hlh_ref, xc_buf, r_buf, i_buf, a_ref, diff_ref, yp_ref, dya_ref, dyb_ref, dm_ref,
             cw_ref, wa_ref, wx_ref, lam_ref, pw_ref, ps_ref, pack_mid_ref, *rest):
        sib_src, rest = rest[:ns], rest[ns:]
        dz_ref, pack_ref = rest[:2]
        sib_out, rest = rest[2:2 + ns], rest[2 + ns:]
        a_ext, an_buf, hl_ext, dh_buf, dxc_ext, q_ext, carry, send_sems, recv_sems = rest
        b = pl.program_id(0)
        j = pl.program_id(1)
        jr = nt - 1 - j
        has_prev = jr > 0

        @pl.when((b == 0) & (j == 0))
        def _():
            for cp in _chip_copies(sib_src, sib_out, send_sems, recv_sems):
                cp.start()

        dz_ref[:, zw:nc] = dm_ref[...]

        @pl.when((b == 0) & (j == 0))
        def _():
            pack_ref[...] = jnp.zeros_like(pack_ref)

        @pl.when(j == 0)
        def _():
            a_ext[pl.ds(tm, SUB), :] = jnp.zeros((SUB, lw), F32)
            dxc_ext[pl.ds(tm, HALO), :] = jnp.zeros((HALO, lw), F32)
            q_ext[pl.ds(tm, HALO), :] = jnp.zeros((HALO, pwid), F32)
            carry[...] = jnp.zeros_like(carry)

        hl_ext[pl.ds(0, SUB), :] = jnp.where(has_prev, hlh_ref[...], 0.0)
        hl_ext[pl.ds(SUB, tm), :] = hl_ref[...]
        a_ext[pl.ds(0, tm), :] = a_ref[...]
        for h in range(N_HEADS):
            cols = slice(h * HEAD, (h + 1) * HEAD)
            ga = zq_ref[:, lw + h * HEAD:lw + (h + 1) * HEAD]
            sg = _sigmoid(ga)
            dyav = dya_ref[:, cols]
            dh_buf[:, cols] = dyav * (ga * sg)
            dga = (dyav * hl_ext[pl.ds(SUB, tm), cols]) * (sg * (1.0 + ga * (1.0 - sg)))
            dz_ref[:, lw + h * HEAD:lw + (h + 1) * HEAD] = dga.astype(dz_ref.dtype)
        an_buf[...] = _rows_ahead(a_ext[...], 1)[0:tm]

        rows = lax.broadcasted_iota(jnp.int32, (SUB, lw), 0)
        nch = tm // SUB

        def step(c, car):
            i0 = pl.multiple_of((nch - 1 - c) * SUB, SUB)
            cv = an_buf[pl.ds(i0, SUB), :]
            bv = dh_buf[pl.ds(i0, SUB), :]
            for sh in (1, 2, 4):
                m = rows < SUB - sh
                c_sh = jnp.where(m, pltpu.roll(cv, SUB - sh, 0), 1.0)
                b_sh = jnp.where(m, pltpu.roll(bv, SUB - sh, 0), 0.0)
                bv = cv * b_sh + bv
                cv = cv * c_sh
            hv = cv * car + bv
            dh_buf[pl.ds(i0, SUB), :] = hv
            return jnp.broadcast_to(hv[0:1, :], (SUB, lw))

        carry[...] = lax.fori_loop(0, nch, step, carry[...])

        for h in range(N_HEADS):
            cols = slice(h * HEAD, (h + 1) * HEAD)
            lam_h = lam_ref[:, cols]
            neg_c_sp = -LRU_C * _softplus_neg(lam_h)
            xc = xc_buf[:, cols]
            r = r_buf[:, cols]
            i = i_buf[:, cols]
            a = a_ref[:, cols]
            mult, inv_mult = _lru_mult(neg_c_sp * r, a)
            dh = dh_buf[:, cols]
            dhx = dh * xc
            di = dhx * mult
            h_prev = _rows_back(hl_ext[:, cols], 1)[SUB:SUB + tm]
            dlog_a = (dh * h_prev) * a - (dhx * i) * ((a * a) * inv_mult)
            pack_ref[pl.ds(R["dlam"][0] + h, 1), :] += _colsum(dlog_a * r) * (LRU_C * _sigmoid(-lam_h))
            dpa = (dlog_a * neg_c_sp) * (r * (1.0 - r))
            dpx = di * (i * (1.0 - i))
            dpa16 = dpa.astype(_MM)
            dpx16 = dpx.astype(_MM)
            xc16 = xc.astype(_MM)
            dxc = dh * (mult * i) + _dot_nt(dpa16, wa_ref[h]) + _dot_nt(dpx16, wx_ref[h])
            pack_ref[pl.ds(R["dwa"][0] + h * HEAD, HEAD), :] += _dot_tn(xc16, dpa16)
            pack_ref[pl.ds(R["dwx"][0] + h * HEAD, HEAD), :] += _dot_tn(xc16, dpx16)
            pack_ref[pl.ds(R["dba"][0] + h, 1), :] += _colsum(dpa)
            pack_ref[pl.ds(R["dbx"][0] + h, 1), :] += _colsum(dpx)
            pack_ref[pl.ds(R["dcb"][0] + h, 1), :] += _colsum(dxc)
            dxc_ext[pl.ds(0, tm), cols] = dxc

        for h in range(N_HEADS):
            cols = slice(h * HEAD, (h + 1) * HEAD)
            dxc_all = dxc_ext[:, cols]
            xa = zq_ref[:, cols]
            dxa = None
            for k in range(4):
                dxc_k = _rows_ahead(dxc_all, 3 - k)[0:tm]
                pack_ref[pl.ds(R["dcw"][0] + SUB * k + h, 1), :] += _colsum(dxc_k * xa)
                term = cw_ref[pl.ds(k, 1), cols] * dxc_k
                dxa = term if dxa is None else dxa + term
            dz_ref[:, cols] = dxa.astype(dz_ref.dtype)

        pos = jr * tm + lax.broadcasted_iota(jnp.int32, (tm, HEAD), 0)
        for g, k in enumerate(POOL_WINDOWS):
            cols = slice(g * HEAD, (g + 1) * HEAD)
            inv_cnt = 1.0 / jnp.minimum(pos + 1, k).astype(F32)
            diff16 = diff_ref[:, cols]
            yp = yp_ref[:, cols]
            sc = ps_ref[:, cols]
            gb = zq_ref[:, 2 * lw + pwid + g * HEAD:2 * lw + pwid + (g + 1) * HEAD]
            sgb = _sigmoid(gb)
            dybv = dyb_ref[:, cols]
            dy_pool = dybv * (gb * sgb)
            dgb = (dybv * (yp * sc)) * (sgb * (1.0 + gb * (1.0 - sgb)))
            pack_ref[pl.ds(R["dps"][0] + g, 1), :] += _colsum(dy_pool * yp)
            dyp16 = (dy_pool * sc).astype(_MM)
            pack_ref[pl.ds(R["dpw"][0] + g * HEAD, HEAD), :] += _dot_tn(diff16, dyp16)
            ddiff = _dot_nt(dyp16, pw_ref[g])
            q_ext[pl.ds(0, tm), cols] = ddiff * inv_cnt
            dxb = _window_sum(q_ext[:, cols], k, _rows_ahead)[0:tm] - ddiff
            dz_ref[:, 2 * lw + g * HEAD:2 * lw + (g + 1) * HEAD] = dxb.astype(dz_ref.dtype)
            dz_ref[:, 2 * lw + pwid + g * HEAD:2 * lw + pwid + (g + 1) * HEAD] = dgb.astype(dz_ref.dtype)

        a_ext[pl.ds(tm, SUB), :] = a_ext[pl.ds(0, SUB), :]
        dxc_ext[pl.ds(tm, HALO), :] = dxc_ext[pl.ds(0, HALO), :]
        q_ext[pl.ds(tm, HALO), :] = q_ext[pl.ds(0, HALO), :]

        @pl.when((b == nb - 1) & (j == nt - 1))
        def _():
            pack_ref[pl.ds(_OFF_MID, MID_PACK_ROWS), :] = pack_mid_ref[...]
            for cp in _chip_copies(sib_src, sib_out, send_sems, recv_sems):
                cp.wait()

    rev = lambda w: pl.BlockSpec((None, tm, w), lambda b, j: (b, nt - 1 - j, 0))
    prev_rows = lambda rows, w: pl.BlockSpec(
        (None, rows, w), lambda b, j: (b, jnp.maximum((nt - 1 - j) * (tm // rows) - 1, 0), 0))
    return pl.pallas_call(
        body, name="bwd_seq", grid=(nb, nt),
        in_specs=[rev(zw), rev(lw), prev_rows(SUB, lw), rev(lw), rev(lw), rev(lw), rev(lw), rev(pwid), rev(pwid),
                  rev(lw), rev(pwid), rev(nc - zw),
                  _const_spec(cw.shape), _const_spec(wa.shape), _const_spec(wx.shape), _const_spec(lam.shape),
                  _const_spec(pw.shape), _const_spec(ps.shape), _const_spec(pack_mid.shape)] + [HBM_SPEC] * ns,
        out_specs=(rev(nc), pl.BlockSpec((SMALL_ROWS, HEAD), lambda b, j: (0, 0))) + (HBM_SPEC,) * ns,
        out_shape=(jax.ShapeDtypeStruct((nb, s, nc), _MM), jax.ShapeDtypeStruct((SMALL_ROWS, HEAD), F32))
        + tuple(jax.ShapeDtypeStruct((3,) + g.shape[1:], g.dtype) for g in sib_grads),
        scratch_shapes=[pltpu.VMEM((tm + SUB, lw), F32), pltpu.VMEM((tm, lw), F32), pltpu.VMEM((tm + SUB, lw), F32),
                        pltpu.VMEM((tm, lw), F32), pltpu.VMEM((tm + HALO, lw), F32),
                        pltpu.VMEM((tm + HALO, pwid), F32), pltpu.VMEM((SUB, lw), F32),
                        pltpu.SemaphoreType.DMA((ns, 3)), pltpu.SemaphoreType.DMA((ns, 3))],
        compiler_params=pltpu.CompilerParams(dimension_semantics=("arbitrary", "arbitrary"), vmem_limit_bytes=VMEM_LIMIT),
    )(z, hl, hl, *saved, dya, dyb, dm, cw, wa, wx, lam, pw, ps, pack_mid, *sib_grads)


def _bwd_win(ht, dz, small):
    nb, d, s = ht.shape
    nc = dz.shape[2]
    sec = nc // N_DEV
    n_chips = N_DEV // 2

    def body(ht_ref, dz_ref, small_src, gwin_ref, recv_ref, small_out, small_chips_out, gsmall_ref,
             acc, own4, sib4, csum, got3, total, local_sems, sib_send, sib_recv, small_send, small_recv,
             sc_send, sc_recv, ld_sems, g_send, g_recv, g_local):
        q = pl.program_id(0)
        b = pl.program_id(1)
        slot = q % 2
        x, y, c = _my_pos()

        def half(sl, core):
            return acc.at[sl, :, pl.ds(pl.multiple_of(core * sec, HEAD), sec)]

        def local_copy(qq, sl):
            return pltpu.make_async_copy(half(sl, c), gwin_ref.at[2 * qq + c], local_sems.at[sl])

        def sib_copy(qq, sl):
            return pltpu.make_async_remote_copy(
                src_ref=half(sl, 1 - c), dst_ref=recv_ref.at[qq], send_sem=sib_send.at[qq],
                recv_sem=sib_recv.at[qq], device_id=(x, y, 1 - c), device_id_type=MESH)

        step_no = q * nb + b
        n_steps = n_chips * nb
        sum_step = min(3, n_steps - 1)
        fwd_step = max(sum_step, min(6, n_steps - 1))
        small_sib = lambda: _sibling_copies([small_src], [small_out], small_send, small_recv)
        small_chip = lambda: _chip_copies([csum], [small_chips_out], sc_send, sc_recv)
        g_start, g_forward, g_finish = _gather_phases([total], [gsmall_ref], [_row_slicer(small.shape[1])],
                                                      g_send, g_recv, g_local)

        def load(srcs, dst):
            cps = [pltpu.make_async_copy(src, dst.at[k], ld_sems.at[k]) for k, src in enumerate(srcs)]
            for cp in cps:
                cp.start()
            for cp in cps:
                cp.wait()

        @pl.when(step_no == 0)
        def _():
            for cp in small_sib():
                cp.start()

        @pl.when(step_no == 1)
        def _():
            for cp in small_sib():
                cp.wait_recv()
            load([small_src.at[2 * qq + c] for qq in range(n_chips)], own4)
            load([small_out.at[qq] for qq in range(n_chips)], sib4)
            csum[...] = own4[...] + sib4[...]
            for cp in small_chip():
                cp.start()

        @pl.when(step_no == sum_step)
        def _():
            for cp in small_chip():
                cp.wait_recv()
            load([small_chips_out.at[k] for k in range(3)], got3)
            total[...] = ((csum[2 * x + y] + got3[0]) + got3[1]) + got3[2]
            g_start()

        pl.when(step_no == fwd_step)(g_forward)

        @pl.when((q >= 2) & (b == 0))
        def _():
            local_copy(q - 2, slot).wait()
            sib_copy(q - 2, slot).wait_send()

        part = _dot(ht_ref[b], dz_ref[...])

        @pl.when(b == 0)
        def _():
            acc[slot] = part

        @pl.when(b != 0)
        def _():
            acc[slot] += part

        @pl.when(b == nb - 1)
        def _():
            local_copy(q, slot).start()
            sib_copy(q, slot).start()

        @pl.when((q == n_chips - 1) & (b == nb - 1))
        def _():
            for qq in (n_chips - 2, n_chips - 1):
                local_copy(qq, qq % 2).wait()
                sib_copy(qq, qq % 2).wait_send()
            for qq in range(n_chips):
                sib_copy(qq, 0).wait_recv()
            g_finish()
            for cp in small_sib() + small_chip():
                cp.wait_send()

    return pl.pallas_call(
        body, name="bwd_win", grid=(n_chips, nb),
        in_specs=[_const_spec(ht.shape), pl.BlockSpec((None, s, 2 * sec), lambda q, b: (b, 0, q))]
        + [HBM_SPEC],
        out_specs=(HBM_SPEC,) * 5,
        out_shape=(jax.ShapeDtypeStruct((N_DEV, d, sec), F32), jax.ShapeDtypeStruct((n_chips, d, sec), F32))
        + (jax.ShapeDtypeStruct((n_chips,) + small.shape[1:], small.dtype),
           jax.ShapeDtypeStruct((3,) + small.shape[1:], small.dtype),
           jax.ShapeDtypeStruct((N_DEV * small.shape[1], small.shape[2]), small.dtype)),
        scratch_shapes=[pltpu.VMEM((2, d, 2 * sec), F32)]
        + [pltpu.VMEM((n_chips,) + small.shape[1:], F32)] * 3
        + [pltpu.VMEM((3,) + small.shape[1:], F32), pltpu.VMEM(small.shape[1:], F32), pltpu.SemaphoreType.DMA((2,)),
           pltpu.SemaphoreType.DMA((n_chips,)), pltpu.SemaphoreType.DMA((n_chips,)),
           pltpu.SemaphoreType.DMA((1, 4)), pltpu.SemaphoreType.DMA((1, 4)),
           pltpu.SemaphoreType.DMA((1, 3)), pltpu.SemaphoreType.DMA((1, 3)), pltpu.SemaphoreType.DMA((n_chips,))]
        + _gather_sems(1),
        compiler_params=pltpu.CompilerParams(dimension_semantics=("arbitrary", "arbitrary"), vmem_limit_bytes=VMEM_LIMIT),
    )(ht, dz, small)


def _bwd_dx(dz, x, dx1, win, g0, chip_srcs, tm):
    t, d = x.shape
    nc = win.shape[1]
    n = t // tm
    nm = len(chip_srcs)

    def body(dz_ref, x_ref, dx1_ref, win_ref, g0_ref, *rest):
        chip_src = rest[:nm]
        gx_ref = rest[nm]
        chip_out = rest[nm + 1:2 * nm + 1]
        parts_ref = rest[2 * nm + 1]
        pack, chip_send, chip_recv, p_send, p_recv, p_local = rest[2 * nm + 2:]
        i = pl.program_id(0)
        x_, y_, c_ = _my_pos()
        chip_copies = lambda: _chip_copies(chip_src, chip_out, chip_send, chip_recv)

        @pl.when(i == 0)
        def _():
            pack[...] = jnp.zeros_like(pack)
            for cp in chip_copies():
                cp.start()

        xv = x_ref[...]
        r0 = _rsqrt_mean_sq(xv)
        xh = xv * r0
        dh = _dot_nt(dz_ref[...], win_ref[...])
        _store_chunks(pack, 0, _colsum(dh * xh))
        dxh = dh * g0_ref[...]
        gx_ref[...] = dx1_ref[...] + r0 * (dxh - xh * jnp.mean(dxh * xh, axis=-1, keepdims=True))

        @pl.when(i == n - 1)
        def _():
            for cp in chip_copies():
                cp.wait()
            row0 = pl.multiple_of((4 * x_ + 2 * y_ + c_) * INP_PACK_ROWS, SUB)
            mine = parts_ref.at[pl.ds(row0, INP_PACK_ROWS), :]
            local = pltpu.make_async_copy(pack, mine, p_local)
            local.start()
            sends = []
            for k in range(1, N_DEV):
                peer = tuple(1 - v if (k >> bit) & 1 else v for v, bit in ((x_, 2), (y_, 1), (c_, 0)))
                sends.append(pltpu.make_async_remote_copy(
                    src_ref=pack, dst_ref=mine, send_sem=p_send.at[k - 1], recv_sem=p_recv.at[k - 1],
                    device_id=peer, device_id_type=MESH))
            for cp in sends:
                cp.start()
            for cp in sends:
                cp.wait()
            local.wait()

    tile = lambda w: pl.BlockSpec((tm, w), lambda i: (i, 0))
    return pl.pallas_call(
        body, name="bwd_dx", grid=(n,),
        in_specs=[tile(nc), tile(d), tile(d), _const_spec(win.shape), _const_spec(g0.shape)] + [HBM_SPEC] * nm,
        out_specs=(tile(d),) + (HBM_SPEC,) * (nm + 1),
        out_shape=(jax.ShapeDtypeStruct((t, d), F32),)
        + tuple(jax.ShapeDtypeStruct((3,) + g.shape[1:], g.dtype) for g in chip_srcs)
        + (jax.ShapeDtypeStruct((N_DEV * INP_PACK_ROWS, HEAD), F32),),
        scratch_shapes=[pltpu.VMEM((INP_PACK_ROWS, HEAD), F32),
                        pltpu.SemaphoreType.DMA((nm, 3)), pltpu.SemaphoreType.DMA((nm, 3)),
                        pltpu.SemaphoreType.DMA((N_DEV - 1,)), pltpu.SemaphoreType.DMA((N_DEV - 1,)),
                        pltpu.SemaphoreType.DMA],
        compiler_params=pltpu.CompilerParams(dimension_semantics=("arbitrary",), vmem_limit_bytes=VMEM_LIMIT),
    )(dz, x, dx1, win, g0, *chip_srcs)


ROW_BLOCKS = 8


def _chip_sums(core_idx, grads, recvs, dtypes, name):
    nw = len(grads)

    def body(idx_ref, *refs):
        gs, rs, outs = refs[:nw], refs[nw:2 * nw], refs[2 * nw:]
        for g, r, o in zip(gs, rs, outs):
            o[...] = (g[...] + r[...]).astype(o.dtype)

    def blk(g):
        return (None,) + g.shape[1:]

    return pl.pallas_call(
        body, name=name,
        grid_spec=pltpu.PrefetchScalarGridSpec(
            num_scalar_prefetch=1, grid=(4,),
            in_specs=[pl.BlockSpec(blk(g), lambda q, s: (2 * q + s[0], 0, 0)) for g in grads]
            + [pl.BlockSpec(blk(g), lambda q, s: (q, 0, 0)) for g in grads],
            out_specs=tuple(pl.BlockSpec(blk(g), lambda q, s: (q, 0, 0)) for g in grads)),
        out_shape=tuple(jax.ShapeDtypeStruct((4,) + g.shape[1:], dt) for g, dt in zip(grads, dtypes)),
    )(core_idx, *grads, *recvs)


def _adamw(w, g, m, v):
    m = ADAM_B1 * m + (1.0 - ADAM_B1) * g
    v = ADAM_B2 * v + (1.0 - ADAM_B2) * (g * g)
    m_hat = m / (1.0 - ADAM_B1 ** ADAM_STEP)
    v_hat = v / (1.0 - ADAM_B2 ** ADAM_STEP)
    delta = -ADAM_LR * (m_hat / (jnp.sqrt(v_hat) + ADAM_EPS) + ADAM_WD * w)
    return delta, m, v


def _adam_sections(sec_idx, grads, recv_sib, recv_chips, wmv):
    nw = len(wmv)

    def body(idx_ref, *refs):
        gs, rs, cs = refs[:nw], refs[nw:2 * nw], refs[2 * nw:3 * nw]
        params = refs[3 * nw:6 * nw]
        outs = refs[6 * nw:]
        for w in range(nw):
            g = gs[w][...] + rs[w][...]
            for k in range(3):
                g = g + cs[w][k].astype(F32)
            wv, mv, vv = (params[3 * w + t][...] for t in range(3))
            delta, m_new, v_new = _adamw(wv, g, mv, vv)
            outs[4 * w][...] = g
            outs[4 * w + 1][...] = delta
            outs[4 * w + 2][...] = m_new
            outs[4 * w + 3][...] = v_new

    def rb(g):
        return g.shape[1] // ROW_BLOCKS

    in_specs = [pl.BlockSpec((None, rb(g), g.shape[2]), lambda i, s: (s[0], i, 0)) for g in grads]
    in_specs += [pl.BlockSpec((None, rb(g), g.shape[2]), lambda i, s: (s[1], i, 0)) for g in grads]
    in_specs += [pl.BlockSpec((3, rb(g), g.shape[2]), lambda i, s: (0, i, 0)) for g in grads]
    sec = lambda g: pl.BlockSpec((rb(g), g.shape[2]), lambda i, s: (i, 0))
    for w in range(nw):
        in_specs += [sec(grads[w])] * 3
    out_specs, out_shape = [], []
    for g in grads:
        out_specs += [sec(g)] * 4
        out_shape += [jax.ShapeDtypeStruct(g.shape[1:], F32)] * 4
    flat = [a for t in wmv for a in t]
    outs = pl.pallas_call(
        body, name="adam_sections",
        grid_spec=pltpu.PrefetchScalarGridSpec(num_scalar_prefetch=1, grid=(ROW_BLOCKS,), in_specs=in_specs,
                                               out_specs=tuple(out_specs)),
        out_shape=tuple(out_shape),
    )(sec_idx, *grads, *recv_sib, *recv_chips, *flat)
    return [outs[4 * w:4 * w + 4] for w in range(nw)]


def _adam_small(sec_idx, gpack, parts, parts_wmv, vec_params, mat_params, conv_wmv):
    items = [(0, parts_wmv, "parts")]
    items += [(r0, t, "vec") for r0, t in vec_params] + [(r0, t, "mat") for r0, t in mat_params]
    items.append((_SEQ_ROWS["dcw"][0], conv_wmv, "conv"))

    def body(idx_ref, g_ref, parts_ref, *refs):
        ins, outs = refs[:3 * len(items)], refs[3 * len(items):]
        for n, (r0, _, kind) in enumerate(items):
            w_ref, m_ref, v_ref = ins[3 * n:3 * n + 3]
            o = outs[4 * n:4 * n + 4]
            if kind == "mat":
                g = g_ref[pl.ds(r0, w_ref.shape[0]), :]
                res = (g,) + _adamw(w_ref[...], g, m_ref[...], v_ref[...])
                for ref, val in zip(o, res):
                    ref[...] = val
            elif kind in ("vec", "parts"):
                for h in range(w_ref.shape[1] // HEAD):
                    cols = slice(h * HEAD, (h + 1) * HEAD)
                    if kind == "vec":
                        g = g_ref[pl.ds(r0 + h, 1), :]
                    else:
                        g = parts_ref[pl.ds(h, 1), :]
                        for dev in range(1, N_DEV):
                            g = g + parts_ref[pl.ds(SUB * dev + h, 1), :]
                    res = (g,) + _adamw(w_ref[:, cols], g, m_ref[:, cols], v_ref[:, cols])
                    for ref, val in zip(o, res):
                        ref[:, cols] = val
            else:
                rows = lax.broadcasted_iota(jnp.int32, (SUB, HEAD), 0)
                for k in range(4):
                    blk = g_ref[pl.ds(r0 + SUB * k, SUB), :]
                    g = jnp.sum(jnp.where(rows == idx_ref[0], blk, 0.0), axis=0, keepdims=True)
                    row = pl.ds(k, 1)
                    res = (g,) + _adamw(w_ref[row, :], g, m_ref[row, :], v_ref[row, :])
                    for ref, val in zip(o, res):
                        ref[row, :] = val
        outs[4 * len(items)][...] = g_ref[pl.ds(_OFF_MID + MID_LOSS_ROW, 1), :]

    flat = [a for _, t, _ in items for a in t]
    full = lambda a: pl.BlockSpec(a.shape, lambda i, s: (0,) * a.ndim)
    out_specs, out_shape = [], []
    for _, t, _ in items:
        out_specs += [full(t[0])] * 4
        out_shape += [jax.ShapeDtypeStruct(t[0].shape, F32)] * 4
    loss_row = jax.ShapeDtypeStruct((1, HEAD), F32)
    out_specs.append(full(loss_row))
    out_shape.append(loss_row)
    outs = pl.pallas_call(
        body, name="adam_small",
        grid_spec=pltpu.PrefetchScalarGridSpec(num_scalar_prefetch=1, grid=(1,),
                                               in_specs=[full(gpack), full(parts)] + [full(a) for a in flat],
                                               out_specs=tuple(out_specs)),
        out_shape=tuple(out_shape),
    )(sec_idx, gpack, parts, *flat)
    return [outs[4 * n:4 * n + 4] for n in range(len(items))], outs[4 * len(items)]


def _col_slicer(width):
    return lambda ref, idx: ref.at[:, pl.ds(pl.multiple_of(idx * width, HEAD), width)]


def _row_slicer(rows):
    return lambda ref, idx: ref.at[pl.ds(pl.multiple_of(idx * rows, SUB), rows), :]


def kernel(x, p, norm_g, w_in, conv_w, conv_b, lru_w_a, lru_b_a, lru_w_x, lru_b_x, lru_lambda, pool_w, pool_scale, w_proj_lru, w_proj_pool, w_out, ple_norm_g, w_ple_gate, w_ple_proj, final_g, loss_target, m_norm_g, m_w_in, m_conv_w, m_conv_b, m_lru_w_a, m_lru_b_a, m_lru_w_x, m_lru_b_x, m_lru_lambda, m_pool_w, m_pool_scale, m_w_proj_lru, m_w_proj_pool, m_w_out, m_ple_norm_g, m_w_ple_gate, m_w_ple_proj, m_final_g, v_norm_g, v_w_in, v_conv_w, v_conv_b, v_lru_w_a, v_lru_b_a, v_lru_w_x, v_lru_b_x, v_lru_lambda, v_pool_w, v_pool_scale, v_w_proj_lru, v_w_proj_pool, v_w_out, v_ple_norm_g, v_w_ple_gate, v_w_ple_proj, v_final_g):
    nb, s, d = x.shape
    t = nb * s
    tm = min(TILE_M, s)
    sec_idx = (4 * lax.axis_index("x") + 2 * lax.axis_index("y") + lax.axis_index("c")).astype(jnp.int32)
    chip_idx = (2 * lax.axis_index("x") + lax.axis_index("y")).astype(jnp.int32)
    core_idx = lax.axis_index("c").astype(jnp.int32)

    h, ht, win, cw = _pre(
        x, norm_g, [w_in[0], conv_w[0]], [True, False],
        [(d, N_DEV * w_in.shape[2]), (conv_w.shape[1], N_DEV * conv_w.shape[2])],
        [_col_slicer(w_in.shape[2]), _col_slicer(conv_w.shape[2])], tm)
    late_shards = [w_proj_lru[0], w_proj_pool[0], w_out[0], w_ple_gate[0], w_ple_proj[0]]
    late_shapes = [(N_DEV * w_proj_lru.shape[1], d), (w_proj_pool.shape[1], N_DEV * w_proj_pool.shape[2]),
                   (N_DEV * w_out.shape[1], d), (N_DEV * w_ple_gate.shape[1], d),
                   (w_ple_proj.shape[1], N_DEV * w_ple_proj.shape[2])]
    late_slicers = [_row_slicer(w_proj_lru.shape[1]), _col_slicer(w_proj_pool.shape[2]), _row_slicer(w_out.shape[1]),
                    _row_slicer(w_ple_gate.shape[1]), _col_slicer(w_ple_proj.shape[2])]

    lw = cw.shape[1]
    pwid = pool_scale.shape[1]
    wa = lru_w_a[0].astype(_MM)
    wx = lru_w_x[0].astype(_MM)
    pw = pool_w[0].astype(_MM)
    ba, bx = lru_b_a[0], lru_b_x[0]
    gf = final_g.reshape(1, d)
    core = core_idx.reshape(1)

    fwd_out = _fwd_seq(
        h, win, cw, conv_b, wa, ba, wx, bx, lru_lambda, pw, pool_scale, late_shards, late_shapes, late_slicers, tm)
    z, hl, ya, yb = fwd_out[:4]
    saved = fwd_out[4:N_SEQ_SAVED]
    wpl, wpp, wout, wpg, wpe = fwd_out[N_SEQ_SAVED:]
    x2d = x.reshape(t, d)
    mid_out = _mid(
        ya.reshape(t, lw), yb.reshape(t, pwid), h.reshape(t, d), win, x2d, p[0].reshape(t, -1),
        loss_target.reshape(t, d), wpl, wpp, wout, wpg, wpe, ple_norm_g, gf, tm)
    dya, dyb, dm, dx1 = mid_out[:4]
    grads_mid, pack_mid, sib_mid = list(mid_out[4:9]), mid_out[9], list(mid_out[10:])
    sums_mid = _chip_sums(core, grads_mid, sib_mid, [_WIRE] * len(grads_mid), "chip_sums_mid")
    seq_out = _bwd_seq(z, hl, saved, dya.reshape(nb, s, lw), dyb.reshape(nb, s, pwid), dm.reshape(nb, s, -1), cw,
                       wa, wx, lru_lambda, pw, pool_scale, pack_mid, sums_mid, tm)
    dz, chips_mid = seq_out[0], list(seq_out[2:])
    small = seq_out[1].reshape(N_DEV, SMALL_SEC, HEAD)
    g_in, sib_in, _, _, gsmall = _bwd_win(ht, dz, small)
    (sums_in,) = _chip_sums(core, [g_in], [sib_in], [_WIRE], "chip_sums_in")
    grad_x, chips_in, g0_parts = _bwd_dx(dz.reshape(t, -1), x2d, dx1, win, norm_g, [sums_in], tm)

    grads = [g_in] + grads_mid
    recv_sib = [sib_in] + sib_mid
    recv_chips = [chips_in] + chips_mid
    wmv = [(w_in[0], m_w_in[0], v_w_in[0]), (w_proj_lru[0], m_w_proj_lru[0], v_w_proj_lru[0]),
           (w_proj_pool[0], m_w_proj_pool[0], v_w_proj_pool[0]), (w_out[0], m_w_out[0], v_w_out[0]),
           (w_ple_gate[0], m_w_ple_gate[0], v_w_ple_gate[0]), (w_ple_proj[0], m_w_ple_proj[0], v_w_ple_proj[0])]
    big = _adam_sections(jnp.stack([sec_idx, chip_idx]), grads, recv_sib, recv_chips, wmv)

    R = _SEQ_ROWS
    vec = lambda r0, *t: (r0, tuple(a.reshape(1, -1) for a in t))
    mat = lambda r0, *t: (r0, tuple(a.reshape(-1, HEAD) for a in t))
    norm_wmv = vec(0, norm_g, m_norm_g, v_norm_g)[1]
    vec_params = [vec(R["dcb"][0], conv_b, m_conv_b, v_conv_b),
                  vec(R["dlam"][0], lru_lambda, m_lru_lambda, v_lru_lambda),
                  vec(R["dps"][0], pool_scale, m_pool_scale, v_pool_scale),
                  vec(_OFF_MID, ple_norm_g, m_ple_norm_g, v_ple_norm_g),
                  vec(_OFF_MID + SUB, final_g, m_final_g, v_final_g)]
    mat_params = [mat(R["dwa"][0], lru_w_a, m_lru_w_a, v_lru_w_a), mat(R["dba"][0], lru_b_a, m_lru_b_a, v_lru_b_a),
                  mat(R["dwx"][0], lru_w_x, m_lru_w_x, v_lru_w_x), mat(R["dbx"][0], lru_b_x, m_lru_b_x, v_lru_b_x),
                  mat(R["dpw"][0], pool_w, m_pool_w, v_pool_w)]
    conv_wmv = (conv_w[0], m_conv_w[0], v_conv_w[0])
    small_out, loss_row = _adam_small(sec_idx.reshape(1), gsmall, g0_parts, norm_wmv, vec_params, mat_params, conv_wmv)
    loss = loss_row[0, 0]

    res = {}
    names_small = ["norm_g", "conv_b", "lru_lambda", "pool_scale", "ple_norm_g", "final_g",
                   "lru_w_a", "lru_b_a", "lru_w_x", "lru_b_x", "pool_w", "conv_w"]
    shapes = {"norm_g": norm_g, "conv_b": conv_b, "lru_lambda": lru_lambda, "pool_scale": pool_scale,
              "ple_norm_g": ple_norm_g, "final_g": final_g, "lru_w_a": lru_w_a, "lru_b_a": lru_b_a, "lru_w_x": lru_w_x,
              "lru_b_x": lru_b_x, "pool_w": pool_w, "conv_w": conv_w}
    for name, quad in zip(names_small, small_out):
        res[name] = [a.reshape(shapes[name].shape) for a in quad]
    names_big = ["w_in", "w_proj_lru", "w_proj_pool", "w_out", "w_ple_gate", "w_ple_proj"]
    for name, quad, (w, _, _) in zip(names_big, big, wmv):
        res[name] = [a.reshape((1,) + w.shape) for a in quad]
    order = ["norm_g", "w_in", "conv_w", "conv_b", "lru_w_a", "lru_b_a", "lru_w_x", "lru_b_x", "lru_lambda", "pool_w",
             "pool_scale", "w_proj_lru", "w_proj_pool", "w_out", "ple_norm_g", "w_ple_gate", "w_ple_proj", "final_g"]
    out = [loss, grad_x.reshape(nb, s, d)]
    for kind in range(4):
        out += [res[name][kind] for name in order]
    return tuple(out)
```

```python
import jax
import jax.numpy as jnp
from jax import lax
from jax.experimental import pallas as pl
from jax.experimental.pallas import tpu as pltpu

F32 = jnp.float32
_MM = jnp.bfloat16
_WIRE = jnp.bfloat16

EPS = 1e-6
LRU_C = 8.0
MULT_SQ_FLOOR = 1e-30
POOL_WINDOWS = (2, 4, 8, 16)
N_HEADS = 8
HEAD = 128
HALO = 16
SUB = 8

ADAM_LR = 0.001
ADAM_B1 = 0.9
ADAM_B2 = 0.999
ADAM_EPS = 1e-08
ADAM_WD = 0.01
ADAM_STEP = 10

N_DEV = 8
MESH = pl.DeviceIdType.MESH
VMEM_LIMIT = 60 * 1024 * 1024
TILE_M = 256
N_SEQ_SAVED = 10

_SEQ_ROWS = {"dwa": (0, 1024), "dwx": (1024, 1024), "dpw": (2048, 512), "dba": (2560, 8), "dbx": (2568, 8),
             "dcb": (2576, 8), "dlam": (2584, 8), "dcw": (2592, 32), "dps": (2624, 8)}
SEQ_PACK_ROWS = 2632
MID_PACK_ROWS = 24
MID_LOSS_ROW = 16
INP_PACK_ROWS = 8
SMALL_ROWS = 3072
SMALL_SEC = SMALL_ROWS // N_DEV
_OFF_MID = SEQ_PACK_ROWS


def _dot(a, b):
    return jnp.dot(a, b, preferred_element_type=F32)


def _dot_nt(a, b):
    return lax.dot_general(a, b, (((1,), (1,)), ((), ())), preferred_element_type=F32)


def _dot_tn(a, b):
    return lax.dot_general(a, b, (((0,), (0,)), ((), ())), preferred_element_type=F32)


def _sigmoid(v):
    return 0.5 * jnp.tanh(0.5 * v) + 0.5


def _lru_mult(log_a, a):
    m2 = jnp.maximum(-jnp.tanh(log_a) * (1.0 + a * a), MULT_SQ_FLOOR)
    inv = lax.rsqrt(m2)
    return m2 * inv, inv


def _rows_back(v, n):
    return pltpu.roll(v, n, 0) if n else v


def _rows_ahead(v, n):
    return pltpu.roll(v, v.shape[0] - n, 0) if n else v


def _softplus_neg(lam):
    e = jnp.exp(-jnp.abs(lam))
    w = 1.0 + e
    l1p = jnp.where(w == 1.0, e, jnp.log(w) * (e / (w - 1.0)))
    return jnp.maximum(-lam, 0.0) + l1p


def _rsqrt_mean_sq(v):
    return lax.rsqrt(jnp.mean(v * v, axis=-1, keepdims=True) + EPS)


def _colsum(v):
    return jnp.sum(v, axis=0, keepdims=True)


def _const_spec(shape):
    nd = len(shape)
    return pl.BlockSpec(shape, lambda *_: (0,) * nd, pipeline_mode=pl.Buffered(1))


HBM_SPEC = pl.BlockSpec(memory_space=pl.ANY)


def _my_pos():
    return lax.axis_index("x"), lax.axis_index("y"), lax.axis_index("c")


def _other_chips(x, y):
    return [(1 - x, y), (x, 1 - y), (1 - x, 1 - y)]


def _gather_sems(nw):
    return [pltpu.SemaphoreType.DMA((nw, 7)), pltpu.SemaphoreType.DMA((nw, 7)), pltpu.SemaphoreType.DMA((nw,))]


def _gather_phases(srcs, outs, slicers, send_sems, recv_sems, local_sems, relay_diagonal=False):
    n_direct = 2 if relay_diagonal else 3
    nw = len(srcs)
    x, y, c = _my_pos()
    me, sibling = (x, y, c), (x, y, 1 - c)
    chips = _other_chips(x, y)

    def part(w, pos):
        return slicers[w](outs[w], 4 * pos[0] + 2 * pos[1] + pos[2])

    def copy(w, k, block, to, src=None):
        return pltpu.make_async_remote_copy(
            src_ref=part(w, block) if src is None else src, dst_ref=part(w, block),
            send_sem=send_sems.at[w, k], recv_sem=recv_sems.at[w, k], device_id=to, device_id_type=MESH)

    def mine():
        return [pltpu.make_async_copy(srcs[w], part(w, me), local_sems.at[w]) for w in range(nw)]

    def first():
        out = []
        for w in range(nw):
            out.append(copy(w, 0, me, sibling, src=srcs[w]))
            out += [copy(w, 1 + j, me, (*chips[j], c), src=srcs[w]) for j in range(n_direct)]
        return out

    def relay(w, j):
        return copy(w, 3, (*chips[j], c), (*chips[1 - j], c))

    def relays():
        return [relay(w, 0) for w in range(nw)] if relay_diagonal else []

    def passed():
        return [copy(w, 4 + j, (*chip, c), sibling) for j, chip in enumerate(chips) for w in range(nw)]

    def start():
        for cp in mine() + first():
            cp.start()

    def forward():
        for j, chip in enumerate(chips):
            for w in range(nw):
                copy(w, 1 + j, (*chip, c), me).wait_recv()
                copy(w, 4 + j, (*chip, c), sibling).start()
            if relay_diagonal and j < 2:
                @pl.when(c == j)
                def _():
                    for w in range(nw):
                        relay(w, j).start()

    def finish():
        for w in range(nw):
            copy(w, 0, sibling, me).wait_recv()
            for j, chip in enumerate(chips):
                copy(w, 4 + j, (*chip, 1 - c), me).wait_recv()
        for cp in first() + relays() + passed():
            cp.wait_send()
        for cp in mine():
            cp.wait()

    return start, forward, finish


def _sibling_copies(srcs, outs, send_sems, recv_sems):
    x, y, c = _my_pos()
    return [pltpu.make_async_remote_copy(
        src_ref=srcs[w].at[2 * q + (1 - c)], dst_ref=outs[w].at[q], send_sem=send_sems.at[w, q],
        recv_sem=recv_sems.at[w, q], device_id=(x, y, 1 - c), device_id_type=MESH)
        for w in range(len(srcs)) for q in range(4)]


def _chip_copies(srcs, outs, send_sems, recv_sems, which=None):
    x, y, c = _my_pos()
    return [pltpu.make_async_remote_copy(
        src_ref=srcs[w].at[2 * px + py], dst_ref=outs[w].at[k], send_sem=send_sems.at[w, k],
        recv_sem=recv_sems.at[w, k], device_id=(px, py, c), device_id_type=MESH)
        for w in (range(len(srcs)) if which is None else which) for k, (px, py) in enumerate(_other_chips(x, y))]


def _lru_gates(xc, h, wa_ref, ba_ref, wx_ref, bx_ref, neg_c_sp):
    xc16 = xc.astype(_MM)
    r = _sigmoid(_dot(xc16, wa_ref[h]) + ba_ref[pl.ds(h, 1), :])
    i = _sigmoid(_dot(xc16, wx_ref[h]) + bx_ref[pl.ds(h, 1), :])
    return r, i, neg_c_sp * r


def _conv_head(xa_ext, cw_ref, cb_ref, cols, tm):
    ext = xa_ext[:, cols]
    xc = cb_ref[:, cols] + cw_ref[pl.ds(3, 1), cols] * ext[HALO:HALO + tm]
    for k in range(3):
        xc = xc + cw_ref[pl.ds(k, 1), cols] * _rows_back(ext, 3 - k)[HALO:HALO + tm]
    return xc


def _window_sum(ext, k, shift):
    n = 1
    while n < k:
        ext = ext + shift(ext, n)
        n *= 2
    return ext


def _pool_diff(xb_ext, g, k, pos, tm):
    cols = slice(g * HEAD, (g + 1) * HEAD)
    ext = xb_ext[:, cols]
    ws = _window_sum(ext, k, _rows_back)[HALO:HALO + tm]
    inv_cnt = 1.0 / jnp.minimum(pos + 1, k).astype(F32)
    return ws * inv_cnt - ext[HALO:HALO + tm], inv_cnt


def _staged_sources(shards, as_operand):
    in_specs = [_const_spec(a.shape) if c else HBM_SPEC for a, c in zip(shards, as_operand)]
    stages = [pltpu.VMEM(a.shape, _MM) for a, c in zip(shards, as_operand) if c]
    dtypes = [_MM if c else a.dtype for a, c in zip(shards, as_operand)]
    return in_specs, stages, dtypes


def _gather_sources(srcs, stages, as_operand):
    stages = iter(stages)
    return [next(stages) if c else src for src, c in zip(srcs, as_operand)]


def _fill_stages(srcs, staged, as_operand):
    for src, dst, c in zip(srcs, staged, as_operand):
        if c:
            dst[...] = src[...].astype(dst.dtype)


def _pre(x, g0, shards, as_operand, shapes, slicers, tm):
    nb, s, d = x.shape
    nt = s // tm
    nl = len(shards)
    shard_specs, stage_shapes, wire = _staged_sources(shards, as_operand)

    def body(x_ref, g0_ref, *rest):
        src, rest = rest[:nl], rest[nl:]
        h_ref, ht_ref = rest[:2]
        out, rest = rest[2:2 + nl], rest[2 + nl:]
        (send_sems, recv_sems, local_sems), stages = rest[:3], rest[3:]
        step_no = pl.program_id(0) * nt + pl.program_id(1)
        staged = _gather_sources(src, stages, as_operand)
        g_start, g_forward, g_finish = _gather_phases(staged, out, slicers, send_sems, recv_sems, local_sems,
                                                      relay_diagonal=True)

        @pl.when(step_no == 0)
        def _():
            _fill_stages(src, staged, as_operand)
            g_start()

        xv = x_ref[...]
        h16 = ((xv * _rsqrt_mean_sq(xv)) * g0_ref[...]).astype(_MM)
        h_ref[...] = h16
        ht_ref[...] = h16.T

        @pl.when(step_no == nb * nt - 1)
        def _():
            g_forward()
            g_finish()

    tile = pl.BlockSpec((None, tm, d), lambda b, j: (b, j, 0))
    return pl.pallas_call(
        body, name="pre", grid=(nb, nt),
        in_specs=[tile, _const_spec(g0.shape)] + shard_specs,
        out_specs=(tile, pl.BlockSpec((None, d, tm), lambda b, j: (b, 0, j))) + (HBM_SPEC,) * nl,
        out_shape=(jax.ShapeDtypeStruct((nb, s, d), _MM), jax.ShapeDtypeStruct((nb, d, s), _MM))
        + tuple(jax.ShapeDtypeStruct(shp, dt) for shp, dt in zip(shapes, wire)),
        scratch_shapes=_gather_sems(nl) + stage_shapes,
        compiler_params=pltpu.CompilerParams(dimension_semantics=("arbitrary", "arbitrary")),
    )(x, g0, *shards)


def _fwd_seq(h, win, cw, cb, wa, ba, wx, bx, lam, pw, ps, late_shards, late_shapes, late_slicers, tm):
    nb, s, d = h.shape
    lw = cw.shape[1]
    pwid = ps.shape[1]
    nc = 2 * lw + 2 * pwid
    nt = s // tm
    nl = len(late_shards)
    n_steps = nb * nt
    as_operand = [True] * nl
    shard_specs, stage_shapes, wire = _staged_sources(late_shards, as_operand)

    def body(h_ref, win_ref, cw_ref, cb_ref, wa_ref, ba_ref, wx_ref, bx_ref, lam_ref, pw_ref, ps_ref, *rest):
        late_src, rest = rest[:nl], rest[nl:]
        z_ref, hl_ref, ya_ref, yb_ref, xc_ref, r_ref, i_ref, a_buf, diff_ref, yp_ref = rest[:N_SEQ_SAVED]
        late_out, rest = rest[N_SEQ_SAVED:N_SEQ_SAVED + nl], rest[N_SEQ_SAVED + nl:]
        (xa_ext, xb_ext, carry, send_sems, recv_sems, local_sems), stages = rest[:6], rest[6:]
        j = pl.program_id(1)
        step_no = pl.program_id(0) * nt + j
        staged = _gather_sources(late_src, stages, as_operand)
        g_start, g_forward, g_finish = _gather_phases(staged, late_out, late_slicers, send_sems, recv_sems, local_sems)

        @pl.when(step_no == 0)
        def _():
            _fill_stages(late_src, staged, as_operand)
            g_start()

        pl.when(step_no == (3 * n_steps) // 4)(g_forward)

        @pl.when(j == 0)
        def _():
            xa_ext[pl.ds(0, HALO), :] = jnp.zeros((HALO, lw), F32)
            xb_ext[pl.ds(0, HALO), :] = jnp.zeros((HALO, pwid), F32)
            carry[...] = jnp.zeros_like(carry)

        h16 = h_ref[...]
        xa = _dot(h16, win_ref[:, 0:lw])
        z_ref[:, 0:lw] = xa
        xa_ext[pl.ds(HALO, tm), :] = xa
        for c0 in range(lw, nc, lw):
            z_ref[:, c0:c0 + lw] = _dot(h16, win_ref[:, c0:c0 + lw])

        for h in range(N_HEADS):
            cols = slice(h * HEAD, (h + 1) * HEAD)
            xc = _conv_head(xa_ext, cw_ref, cb_ref, cols, tm)
            neg_c_sp = -LRU_C * _softplus_neg(lam_ref[:, cols])
            r, i, log_a = _lru_gates(xc, h, wa_ref, ba_ref, wx_ref, bx_ref, neg_c_sp)
            a = jnp.exp(log_a)
            xc_ref[:, cols] = xc
            r_ref[:, cols] = r
            i_ref[:, cols] = i
            a_buf[:, cols] = a
            hl_ref[:, cols] = _lru_mult(log_a, a)[0] * (i * xc)

        rows = lax.broadcasted_iota(jnp.int32, (SUB, lw), 0)

        def step(c, car):
            i0 = pl.multiple_of(c * SUB, SUB)
            av = a_buf[pl.ds(i0, SUB), :]
            bv = hl_ref[pl.ds(i0, SUB), :]
            for sh in (1, 2, 4):
                m = rows >= sh
                a_sh = jnp.where(m, pltpu.roll(av, sh, 0), 1.0)
                b_sh = jnp.where(m, pltpu.roll(bv, sh, 0), 0.0)
                bv = av * b_sh + bv
                av = av * a_sh
            hv = av * car + bv
            hl_ref[pl.ds(i0, SUB), :] = hv
            return jnp.broadcast_to(hv[SUB - 1:SUB, :], (SUB, lw))

        carry[...] = lax.fori_loop(0, tm // SUB, step, carry[...])

        ga = z_ref[:, lw:2 * lw]
        ya_ref[...] = (hl_ref[...] * (ga * _sigmoid(ga))).astype(ya_ref.dtype)

        xb_ext[pl.ds(HALO, tm), :] = z_ref[:, 2 * lw:2 * lw + pwid]
        pos = j * tm + lax.broadcasted_iota(jnp.int32, (tm, HEAD), 0)
        for g, k in enumerate(POOL_WINDOWS):
            cols = slice(g * HEAD, (g + 1) * HEAD)
            diff16 = _pool_diff(xb_ext, g, k, pos, tm)[0].astype(_MM)
            yp = _dot(diff16, pw_ref[g])
            diff_ref[:, cols] = diff16
            yp_ref[:, cols] = yp
            gb = z_ref[:, 2 * lw + pwid + g * HEAD:2 * lw + pwid + (g + 1) * HEAD]
            yb_ref[:, cols] = ((yp * ps_ref[:, cols]) * (gb * _sigmoid(gb))).astype(yb_ref.dtype)

        xa_ext[pl.ds(0, HALO), :] = xa_ext[pl.ds(tm, HALO), :]
        xb_ext[pl.ds(0, HALO), :] = xb_ext[pl.ds(tm, HALO), :]
        pl.when(step_no == n_steps - 1)(g_finish)

    tile = lambda w: pl.BlockSpec((None, tm, w), lambda b, j: (b, j, 0))
    return pl.pallas_call(
        body, name="fwd_seq", grid=(nb, nt),
        in_specs=[tile(d), _const_spec((d, nc)), _const_spec(cw.shape), _const_spec(cb.shape),
                  _const_spec(wa.shape), _const_spec(ba.shape), _const_spec(wx.shape), _const_spec(bx.shape),
                  _const_spec(lam.shape), _const_spec(pw.shape), _const_spec(ps.shape)] + shard_specs,
        out_specs=(tile(nc), tile(lw), tile(lw), tile(pwid), tile(lw), tile(lw), tile(lw), tile(lw), tile(pwid),
                   tile(pwid)) + (HBM_SPEC,) * nl,
        out_shape=(jax.ShapeDtypeStruct((nb, s, nc), F32), jax.ShapeDtypeStruct((nb, s, lw), F32),
                   jax.ShapeDtypeStruct((nb, s, lw), _MM), jax.ShapeDtypeStruct((nb, s, pwid), _MM))
        + (jax.ShapeDtypeStruct((nb, s, lw), F32),) * 4
        + (jax.ShapeDtypeStruct((nb, s, pwid), _MM), jax.ShapeDtypeStruct((nb, s, pwid), F32))
        + tuple(jax.ShapeDtypeStruct(shp, dt) for shp, dt in zip(late_shapes, wire)),
        scratch_shapes=[pltpu.VMEM((tm + HALO, lw), F32), pltpu.VMEM((tm + HALO, pwid), F32),
                        pltpu.VMEM((SUB, lw), F32)] + _gather_sems(nl) + stage_shapes,
        compiler_params=pltpu.CompilerParams(dimension_semantics=("arbitrary", "arbitrary"), vmem_limit_bytes=VMEM_LIMIT),
    )(h, win, cw, cb, wa, ba, wx, bx, lam, pw, ps, *late_shards)


def _store_chunks(ref, row0, vec):
    for h in range(vec.shape[1] // HEAD):
        ref[pl.ds(row0 + h, 1), :] += vec[:, h * HEAD:(h + 1) * HEAD]


def _mid(ya, yb, h, win, x, p, tgt, wpl, wpp, wout, wpg, wpe, g1, gf, tm):
    t, d = x.shape
    lw = ya.shape[1]
    pwid = yb.shape[1]
    pdim = p.shape[1]
    n = t // tm
    sec_pp = wpp.shape[1] // N_DEV
    sec_pe = wpe.shape[1] // N_DEV
    sec_rows = wpl.shape[0] // N_DEV
    grad_secs = [(sec_rows, d), (wpp.shape[0], sec_pp), (sec_rows, d), (sec_rows, d), (wpe.shape[0], sec_pe)]
    n_win_blocks = win.shape[1] // d
    win_cols = lambda c: pl.BlockSpec((d, d), lambda i: (0, c), pipeline_mode=pl.Buffered(1))

    def body(ya_ref, yb_ref, h_ref, wma_ref, wmb_ref, x_ref, p_ref, tgt_ref, wpl_ref, wpp_ref, wout_ref, wpg_ref, wpe_ref,
             g1_ref, gf_ref,
             dya_ref, dyb_ref, dm_ref, dx1_ref, gpl_ref, gpp_ref, gout_ref, gpg_ref, gpe_ref, pack_ref,
             sib_pl, sib_pp, sib_out, sib_pg, sib_pe,
             acc_pl, acc_pp, acc_out, acc_pg, acc_pe, out_sems, sib_send, sib_recv):
        i = pl.program_id(0)

        @pl.when(i == 0)
        def _():
            for acc in (acc_pl, acc_pp, acc_out, acc_pg, acc_pe):
                acc[...] = jnp.zeros_like(acc)
            pack_ref[...] = jnp.zeros_like(pack_ref)

        accs = (acc_pl, acc_pp, acc_out, acc_pg, acc_pe)
        outs = (gpl_ref, gpp_ref, gout_ref, gpg_ref, gpe_ref)
        sibs = (sib_pl, sib_pp, sib_out, sib_pg, sib_pe)

        def section(w, k):
            if w == 1:
                return acc_pp.at[:, pl.ds(k * sec_pp, sec_pp)]
            if w == 4:
                return acc_pe.at[:, pl.ds(k * sec_pe, sec_pe)]
            return accs[w].at[pl.ds(k * sec_rows, sec_rows), :]

        def leave(start, wait):
            px, py, c = _my_pos()

            def copies(ws, cc):
                out = []
                for w in ws:
                    for q in range(4):
                        out.append(pltpu.make_async_remote_copy(
                            src_ref=section(w, 2 * q + 1 - cc), dst_ref=sibs[w].at[q], send_sem=sib_send.at[w, q],
                            recv_sem=sib_recv.at[w, q], device_id=(px, py, 1 - cc), device_id_type=MESH))
                        out.append(pltpu.make_async_copy(
                            section(w, 2 * q + cc), outs[w].at[2 * q + cc], out_sems.at[4 * w + q]))
                return out

            for cc in range(2):
                @pl.when((i == n - 1) & (c == cc))
                def _(cc=cc):
                    for cp in copies(start, cc):
                        cp.start()
                    for cp in copies(wait, cc):
                        cp.wait()

        ya16 = ya_ref[...]
        yb16 = yb_ref[...]
        br_a = _dot(ya16, wpl_ref[...])
        br_b = _dot(yb16, wpp_ref[...])
        h16 = h_ref[...]
        sa = _sigmoid(_dot(h16, wma_ref[...]))
        sb = _sigmoid(_dot(h16, wmb_ref[...]))
        mg16 = (sa * br_a + sb * br_b).astype(_MM)
        x1 = x_ref[...] + _dot(mg16, wout_ref[...])
        r1 = _rsqrt_mean_sq(x1)
        n1 = x1 * r1
        h116 = (n1 * g1_ref[...]).astype(_MM)
        gate = _sigmoid(_dot(h116, wpg_ref[...]))
        p16 = p_ref[...].astype(_MM)
        pe = _dot(p16, wpe_ref[...])
        x2 = x1 + gate * pe
        r2 = _rsqrt_mean_sq(x2)
        n2 = x2 * r2
        err = n2 * gf_ref[...] - tgt_ref[...]
        sq = jnp.sum(_colsum(err * err), axis=1, keepdims=True)
        pack_ref[pl.ds(MID_LOSS_ROW, 1), :] += jnp.broadcast_to(sq * (0.5 / d), (1, HEAD))

        dy = err * (1.0 / d)
        _store_chunks(pack_ref, SUB, _colsum(dy * n2))
        dn2 = dy * gf_ref[...]
        dx2 = r2 * (dn2 - n2 * jnp.mean(dn2 * n2, axis=-1, keepdims=True))
        dpe16 = (dx2 * gate).astype(_MM)
        dpg16 = ((dx2 * pe) * (gate * (1.0 - gate))).astype(_MM)
        acc_pe[...] += _dot_tn(p16, dpe16)
        acc_pg[...] += _dot_tn(h116, dpg16)
        dh1 = _dot_nt(dpg16, wpg_ref[...])
        _store_chunks(pack_ref, 0, _colsum(dh1 * n1))
        dn1 = dh1 * g1_ref[...]
        dx1 = dx2 + r1 * (dn1 - n1 * jnp.mean(dn1 * n1, axis=-1, keepdims=True))
        dx1_ref[...] = dx1
        dx116 = dx1.astype(_MM)
        acc_out[...] += _dot_tn(mg16, dx116)
        leave(start=(2, 3, 4), wait=())
        dmg = _dot_nt(dx116, wout_ref[...])
        da16 = (dmg * sa).astype(_MM)
        db16 = (dmg * sb).astype(_MM)
        dm_ref[:, 0:d] = ((dmg * br_a) * (sa * (1.0 - sa))).astype(dm_ref.dtype)
        dm_ref[:, d:2 * d] = ((dmg * br_b) * (sb * (1.0 - sb))).astype(dm_ref.dtype)
        acc_pl[...] += _dot_tn(ya16, da16)
        acc_pp[...] += _dot_tn(yb16, db16)
        dya_ref[...] = _dot_nt(da16, wpl_ref[...])
        dyb_ref[...] = _dot_nt(db16, wpp_ref[...])

        leave(start=(0, 1), wait=(0, 1, 2, 3, 4))

    tile = lambda w: pl.BlockSpec((tm, w), lambda i: (i, 0))
    any_spec = pl.BlockSpec(memory_space=pl.ANY)
    full = lambda shape: pl.BlockSpec(shape, lambda i: (0,) * len(shape))
    return pl.pallas_call(
        body, name="mid", grid=(n,),
        in_specs=[tile(lw), tile(pwid), tile(d), win_cols(n_win_blocks - 2), win_cols(n_win_blocks - 1),
                  tile(d), tile(pdim), tile(d),
                  _const_spec(wpl.shape), _const_spec(wpp.shape), _const_spec(wout.shape), _const_spec(wpg.shape),
                  _const_spec(wpe.shape), _const_spec(g1.shape), _const_spec(gf.shape)],
        out_specs=(tile(lw), tile(pwid), tile(2 * d), tile(d), any_spec, any_spec, any_spec, any_spec, any_spec,
                   full((MID_PACK_ROWS, HEAD))) + (any_spec,) * 5,
        out_shape=(jax.ShapeDtypeStruct((t, lw), F32), jax.ShapeDtypeStruct((t, pwid), F32),
                   jax.ShapeDtypeStruct((t, 2 * d), _MM), jax.ShapeDtypeStruct((t, d), F32),
                   *(jax.ShapeDtypeStruct((N_DEV,) + sec, F32) for sec in grad_secs),
                   jax.ShapeDtypeStruct((MID_PACK_ROWS, HEAD), F32),
                   *(jax.ShapeDtypeStruct((N_DEV // 2,) + sec, F32) for sec in grad_secs)),
        scratch_shapes=[pltpu.VMEM(wpl.shape, F32), pltpu.VMEM(wpp.shape, F32), pltpu.VMEM(wout.shape, F32),
                        pltpu.VMEM(wpg.shape, F32), pltpu.VMEM(wpe.shape, F32),
                        pltpu.SemaphoreType.DMA((5 * N_DEV // 2,)), pltpu.SemaphoreType.DMA((5, N_DEV // 2)),
                        pltpu.SemaphoreType.DMA((5, N_DEV // 2))],
        compiler_params=pltpu.CompilerParams(dimension_semantics=("arbitrary",), vmem_limit_bytes=VMEM_LIMIT),
    )(ya, yb, h, win, win, x, p, tgt, wpl, wpp, wout, wpg, wpe, g1, gf)


def _bwd_seq(z, hl, saved, dya, dyb, dm, cw, wa, wx, lam, pw, ps, pack_mid, sib_grads, tm):
    nb, s, zw = z.shape
    lw = cw.shape[1]
    pwid = ps.shape[1]
    nt = s // tm
    nc = zw + dm.shape[2]
    R = _SEQ_ROWS
    ns = len(sib_grads)

    def body(zq_ref, hl_ref, hlh_ref, xc_buf, r_buf, i_buf, a_ref, diff_ref, yp_ref, dya_ref, dyb_ref, dm_ref,
             cw_ref, wa_ref, wx_ref, lam_ref, pw_ref, ps_ref, pack_mid_ref, *rest):
        sib_src, rest = rest[:ns], rest[ns:]
        dz_ref, pack_ref = rest[:2]
        sib_out, rest = rest[2:2 + ns], rest[2 + ns:]
        a_ext, an_buf, hl_ext, dh_buf, dxc_ext, q_ext, carry, send_sems, recv_sems = rest
        b = pl.program_id(0)
        j = pl.program_id(1)
        jr = nt - 1 - j
        has_prev = jr > 0

        @pl.when((b == 0) & (j == 0))
        def _():
            for cp in _chip_copies(sib_src, sib_out, send_sems, recv_sems):
                cp.start()

        dz_ref[:, zw:nc] = dm_ref[...]

        @pl.when((b == 0) & (j == 0))
        def _():
            pack_ref[...] = jnp.zeros_like(pack_ref)

        @pl.when(j == 0)
        def _():
            a_ext[pl.ds(tm, SUB), :] = jnp.zeros((SUB, lw), F32)
            dxc_ext[pl.ds(tm, HALO), :] = jnp.zeros((HALO, lw), F32)
            q_ext[pl.ds(tm, HALO), :] = jnp.zeros((HALO, pwid), F32)
            carry[...] = jnp.zeros_like(carry)

        hl_ext[pl.ds(0, SUB), :] = jnp.where(has_prev, hlh_ref[...], 0.0)
        hl_ext[pl.ds(SUB, tm), :] = hl_ref[...]
        a_ext[pl.ds(0, tm), :] = a_ref[...]
        for h in range(N_HEADS):
            cols = slice(h * HEAD, (h + 1) * HEAD)
            ga = zq_ref[:, lw + h * HEAD:lw + (h + 1) * HEAD]
            sg = _sigmoid(ga)
            dyav = dya_ref[:, cols]
            dh_buf[:, cols] = dyav * (ga * sg)
            dga = (dyav * hl_ext[pl.ds(SUB, tm), cols]) * (sg * (1.0 + ga * (1.0 - sg)))
            dz_ref[:, lw + h * HEAD:lw + (h + 1) * HEAD] = dga.astype(dz_ref.dtype)
        an_buf[...] = _rows_ahead(a_ext[...], 1)[0:tm]

        rows = lax.broadcasted_iota(jnp.int32, (SUB, lw), 0)
        nch = tm // SUB

        def step(c, car):
            i0 = pl.multiple_of((nch - 1 - c) * SUB, SUB)
            cv = an_buf[pl.ds(i0, SUB), :]
            bv = dh_buf[pl.ds(i0, SUB), :]
            for sh in (1, 2, 4):
                m = rows < SUB - sh
                c_sh = jnp.where(m, pltpu.roll(cv, SUB - sh, 0), 1.0)
                b_sh = jnp.where(m, pltpu.roll(bv, SUB - sh, 0), 0.0)
                bv = cv * b_sh + bv
                cv = cv * c_sh
            hv = cv * car + bv
            dh_buf[pl.ds(i0, SUB), :] = hv
            return jnp.broadcast_to(hv[0:1, :], (SUB, lw))

        carry[...] = lax.fori_loop(0, nch, step, carry[...])

        for h in range(N_HEADS):
            cols = slice(h * HEAD, (h + 1) * HEAD)
            lam_h = lam_ref[:, cols]
            neg_c_sp = -LRU_C * _softplus_neg(lam_h)
            xc = xc_buf[:, cols]
            r = r_buf[:, cols]
            i = i_buf[:, cols]
            a = a_ref[:, cols]
            mult, inv_mult = _lru_mult(neg_c_sp * r, a)
            dh = dh_buf[:, cols]
            dhx = dh * xc
            di = dhx * mult
            h_prev = _rows_back(hl_ext[:, cols], 1)[SUB:SUB + tm]
            dlog_a = (dh * h_prev) * a - (dhx * i) * ((a * a) * inv_mult)
            pack_ref[pl.ds(R["dlam"][0] + h, 1), :] += _colsum(dlog_a * r) * (LRU_C * _sigmoid(-lam_h))
            dpa = (dlog_a * neg_c_sp) * (r * (1.0 - r))
            dpx = di * (i * (1.0 - i))
            dpa16 = dpa.astype(_MM)
            dpx16 = dpx.astype(_MM)
            xc16 = xc.astype(_MM)
            dxc = dh * (mult * i) + _dot_nt(dpa16, wa_ref[h]) + _dot_nt(dpx16, wx_ref[h])
            pack_ref[pl.ds(R["dwa"][0] + h * HEAD, HEAD), :] += _dot_tn(xc16, dpa16)
            pack_ref[pl.ds(R["dwx"][0] + h * HEAD, HEAD), :] += _dot_tn(xc16, dpx16)
            pack_ref[pl.ds(R["dba"][0] + h, 1), :] += _colsum(dpa)
            pack_ref[pl.ds(R["dbx"][0] + h, 1), :] += _colsum(dpx)
            pack_ref[pl.ds(R["dcb"][0] + h, 1), :] += _colsum(dxc)
            dxc_ext[pl.ds(0, tm), cols] = dxc

        for h in range(N_HEADS):
            cols = slice(h * HEAD, (h + 1) * HEAD)
            dxc_all = dxc_ext[:, cols]
            xa = zq_ref[:, cols]
            dxa = None
            for k in range(4):
                dxc_k = _rows_ahead(dxc_all, 3 - k)[0:tm]
                pack_ref[pl.ds(R["dcw"][0] + SUB * k + h, 1), :] += _colsum(dxc_k * xa)
                term = cw_ref[pl.ds(k, 1), cols] * dxc_k
                dxa = term if dxa is None else dxa + term
            dz_ref[:, cols] = dxa.astype(dz_ref.dtype)

        pos = jr * tm + lax.broadcasted_iota(jnp.int32, (tm, HEAD), 0)
        for g, k in enumerate(POOL_WINDOWS):
            cols = slice(g * HEAD, (g + 1) * HEAD)
            inv_cnt = 1.0 / jnp.minimum(pos + 1, k).astype(F32)
            diff16 = diff_ref[:, cols]
            yp = yp_ref[:, cols]
            sc = ps_ref[:, cols]
            gb = zq_ref[:, 2 * lw + pwid + g * HEAD:2 * lw + pwid + (g + 1) * HEAD]
            sgb = _sigmoid(gb)
            dybv = dyb_ref[:, cols]
            dy_pool = dybv * (gb * sgb)
            dgb = (dybv * (yp * sc)) * (sgb * (1.0 + gb * (1.0 - sgb)))
            pack_ref[pl.ds(R["dps"][0] + g, 1), :] += _colsum(dy_pool * yp)
            dyp16 = (dy_pool * sc).astype(_MM)
            pack_ref[pl.ds(R["dpw"][0] + g * HEAD, HEAD), :] += _dot_tn(diff16, dyp16)
            ddiff = _dot_nt(dyp16, pw_ref[g])
            q_ext[pl.ds(0, tm), cols] = ddiff * inv_cnt
            dxb = _window_sum(q_ext[:, cols], k, _rows_ahead)[0:tm] - ddiff
            dz_ref[:, 2 * lw + g * HEAD:2 * lw + (g + 1) * HEAD] = dxb.astype(dz_ref.dtype)
            dz_ref[:, 2 * lw + pwid + g * HEAD:2 * lw + pwid + (g + 1) * HEAD] = dgb.astype(dz_ref.dtype)

        a_ext[pl.ds(tm, SUB), :] = a_ext[pl.ds(0, SUB), :]
        dxc_ext[pl.ds(tm, HALO), :] = dxc_ext[pl.ds(0, HALO), :]
        q_ext[pl.ds(tm, HALO), :] = q_ext[pl.ds(0, HALO), :]

        @pl.when((b == nb - 1) & (j == nt - 1))
        def _():
            pack_ref[pl.ds(_OFF_MID, MID_PACK_ROWS), :] = pack_mid_ref[...]
            for cp in _chip_copies(sib_src, sib_out, send_sems, recv_sems):
                cp.wait()

    rev = lambda w: pl.BlockSpec((None, tm, w), lambda b, j: (b, nt - 1 - j, 0))
    prev_rows = lambda rows, w: pl.BlockSpec(
        (None, rows, w), lambda b, j: (b, jnp.maximum((nt - 1 - j) * (tm // rows) - 1, 0), 0))
    return pl.pallas_call(
        body, name="bwd_seq", grid=(nb, nt),
        in_specs=[rev(zw), rev(lw), prev_rows(SUB, lw), rev(lw), rev(lw), rev(lw), rev(lw), rev(pwid), rev(pwid),
                  rev(lw), rev(pwid), rev(nc - zw),
                  _const_spec(cw.shape), _const_spec(wa.shape), _const_spec(wx.shape), _const_spec(lam.shape),
                  _const_spec(pw.shape), _const_spec(ps.shape), _const_spec(pack_mid.shape)] + [HBM_SPEC] * ns,
        out_specs=(rev(nc), pl.BlockSpec((SMALL_ROWS, HEAD), lambda b, j: (0, 0))) + (HBM_SPEC,) * ns,
        out_shape=(jax.ShapeDtypeStruct((nb, s, nc), _MM), jax.ShapeDtypeStruct((SMALL_ROWS, HEAD), F32))
        + tuple(jax.ShapeDtypeStruct((3,) + g.shape[1:], g.dtype) for g in sib_grads),
        scratch_shapes=[pltpu.VMEM((tm + SUB, lw), F32), pltpu.VMEM((tm, lw), F32), pltpu.VMEM((tm + SUB, lw), F32),
                        pltpu.VMEM((tm, lw), F32), pltpu.VMEM((tm + HALO, lw), F32),
                        pltpu.VMEM((tm + HALO, pwid), F32), pltpu.VMEM((SUB, lw), F32),
                        pltpu.SemaphoreType.DMA((ns, 3)), pltpu.SemaphoreType.DMA((ns, 3))],
        compiler_params=pltpu.CompilerParams(dimension_semantics=("arbitrary", "arbitrary"), vmem_limit_bytes=VMEM_LIMIT),
    )(z, hl, hl, *saved, dya, dyb, dm, cw, wa, wx, lam, pw, ps, pack_mid, *sib_grads)


def _bwd_win(ht, dz, small):
    nb, d, s = ht.shape
    nc = dz.shape[2]
    sec = nc // N_DEV
    n_chips = N_DEV // 2

    def body(ht_ref, dz_ref, small_src, gwin_ref, recv_ref, small_out, small_chips_out, gsmall_ref,
             acc, own4, sib4, csum, got3, total, local_sems, sib_send, sib_recv, small_send, small_recv,
             sc_send, sc_recv, ld_sems, g_send, g_recv, g_local):
        q = pl.program_id(0)
        b = pl.program_id(1)
        slot = q % 2
        x, y, c = _my_pos()

        def half(sl, core):
            return acc.at[sl, :, pl.ds(pl.multiple_of(core * sec, HEAD), sec)]

        def local_copy(qq, sl):
            return pltpu.make_async_copy(half(sl, c), gwin_ref.at[2 * qq + c], local_sems.at[sl])

        def sib_copy(qq, sl):
            return pltpu.make_async_remote_copy(
                src_ref=half(sl, 1 - c), dst_ref=recv_ref.at[qq], send_sem=sib_send.at[qq],
                recv_sem=sib_recv.at[qq], device_id=(x, y, 1 - c), device_id_type=MESH)

        step_no = q * nb + b
        n_steps = n_chips * nb
        sum_step = min(3, n_steps - 1)
        fwd_step = max(sum_step, min(6, n_steps - 1))
        small_sib = lambda: _sibling_copies([small_src], [small_out], small_send, small_recv)
        small_chip = lambda: _chip_copies([csum], [small_chips_out], sc_send, sc_recv)
        g_start, g_forward, g_finish = _gather_phases([total], [gsmall_ref], [_row_slicer(small.shape[1])],
                                                      g_send, g_recv, g_local)

        def load(srcs, dst):
            cps = [pltpu.make_async_copy(src, dst.at[k], ld_sems.at[k]) for k, src in enumerate(srcs)]
            for cp in cps:
                cp.start()
            for cp in cps:
                cp.wait()

        @pl.when(step_no == 0)
        def _():
            for cp in small_sib():
                cp.start()

        @pl.when(step_no == 1)
        def _():
            for cp in small_sib():
                cp.wait_recv()
            load([small_src.at[2 * qq + c] for qq in range(n_chips)], own4)
            load([small_out.at[qq] for qq in range(n_chips)], sib4)
            csum[...] = own4[...] + sib4[...]
            for cp in small_chip():
                cp.start()

        @pl.when(step_no == sum_step)
        def _():
            for cp in small_chip():
                cp.wait_recv()
            load([small_chips_out.at[k] for k in range(3)], got3)
            total[...] = ((csum[2 * x + y] + got3[0]) + got3[1]) + got3[2]
            g_start()

        pl.when(step_no == fwd_step)(g_forward)

        @pl.when((q >= 2) & (b == 0))
        def _():
            local_copy(q - 2, slot).wait()
            sib_copy(q - 2, slot).wait_send()

        part = _dot(ht_ref[b], dz_ref[...])

        @pl.when(b == 0)
        def _():
            acc[slot] = part

        @pl.when(b != 0)
        def _():
            acc[slot] += part

        @pl.when(b == nb - 1)
        def _():
            local_copy(q, slot).start()
            sib_copy(q, slot).start()

        @pl.when((q == n_chips - 1) & (b == nb - 1))
        def _():
            for qq in (n_chips - 2, n_chips - 1):
                local_copy(qq, qq % 2).wait()
                sib_copy(qq, qq % 2).wait_send()
            for qq in range(n_chips):
                sib_copy(qq, 0).wait_recv()
            g_finish()
            for cp in small_sib() + small_chip():
                cp.wait_send()

    return pl.pallas_call(
        body, name="bwd_win", grid=(n_chips, nb),
        in_specs=[_const_spec(ht.shape), pl.BlockSpec((None, s, 2 * sec), lambda q, b: (b, 0, q))]
        + [HBM_SPEC],
        out_specs=(HBM_SPEC,) * 5,
        out_shape=(jax.ShapeDtypeStruct((N_DEV, d, sec), F32), jax.ShapeDtypeStruct((n_chips, d, sec), F32))
        + (jax.ShapeDtypeStruct((n_chips,) + small.shape[1:], small.dtype),
           jax.ShapeDtypeStruct((3,) + small.shape[1:], small.dtype),
           jax.ShapeDtypeStruct((N_DEV * small.shape[1], small.shape[2]), small.dtype)),
        scratch_shapes=[pltpu.VMEM((2, d, 2 * sec), F32)]
        + [pltpu.VMEM((n_chips,) + small.shape[1:], F32)] * 3
        + [pltpu.VMEM((3,) + small.shape[1:], F32), pltpu.VMEM(small.shape[1:], F32), pltpu.SemaphoreType.DMA((2,)),
           pltpu.SemaphoreType.DMA((n_chips,)), pltpu.SemaphoreType.DMA((n_chips,)),
           pltpu.SemaphoreType.DMA((1, 4)), pltpu.SemaphoreType.DMA((1, 4)),
           pltpu.SemaphoreType.DMA((1, 3)), pltpu.SemaphoreType.DMA((1, 3)), pltpu.SemaphoreType.DMA((n_chips,))]
        + _gather_sems(1),
        compiler_params=pltpu.CompilerParams(dimension_semantics=("arbitrary", "arbitrary"), vmem_limit_bytes=VMEM_LIMIT),
    )(ht, dz, small)


def _bwd_dx(dz, x, dx1, win, g0, chip_srcs, tm):
    t, d = x.shape
    nc = win.shape[1]
    n = t // tm
    nm = len(chip_srcs)

    def body(dz_ref, x_ref, dx1_ref, win_ref, g0_ref, *rest):
        chip_src = rest[:nm]
        gx_ref = rest[nm]
        chip_out = rest[nm + 1:2 * nm + 1]
        parts_ref = rest[2 * nm + 1]
        pack, chip_send, chip_recv, p_send, p_recv, p_local = rest[2 * nm + 2:]
        i = pl.program_id(0)
        x_, y_, c_ = _my_pos()
        chip_copies = lambda: _chip_copies(chip_src, chip_out, chip_send, chip_recv)

        @pl.when(i == 0)
        def _():
            pack[...] = jnp.zeros_like(pack)
            for cp in chip_copies():
                cp.start()

        xv = x_ref[...]
        r0 = _rsqrt_mean_sq(xv)
        xh = xv * r0
        dh = _dot_nt(dz_ref[...], win_ref[...])
        _store_chunks(pack, 0, _colsum(dh * xh))
        dxh = dh * g0_ref[...]
        gx_ref[...] = dx1_ref[...] + r0 * (dxh - xh * jnp.mean(dxh * xh, axis=-1, keepdims=True))

        @pl.when(i == n - 1)
        def _():
            for cp in chip_copies():
                cp.wait()
            row0 = pl.multiple_of((4 * x_ + 2 * y_ + c_) * INP_PACK_ROWS, SUB)
            mine = parts_ref.at[pl.ds(row0, INP_PACK_ROWS), :]
            local = pltpu.make_async_copy(pack, mine, p_local)
            local.start()
            sends = []
            for k in range(1, N_DEV):
                peer = tuple(1 - v if (k >> bit) & 1 else v for v, bit in ((x_, 2), (y_, 1), (c_, 0)))
                sends.append(pltpu.make_async_remote_copy(
                    src_ref=pack, dst_ref=mine, send_sem=p_send.at[k - 1], recv_sem=p_recv.at[k - 1],
                    device_id=peer, device_id_type=MESH))
            for cp in sends:
                cp.start()
            for cp in sends:
                cp.wait()
            local.wait()

    tile = lambda w: pl.BlockSpec((tm, w), lambda i: (i, 0))
    return pl.pallas_call(
        body, name="bwd_dx", grid=(n,),
        in_specs=[tile(nc), tile(d), tile(d), _const_spec(win.shape), _const_spec(g0.shape)] + [HBM_SPEC] * nm,
        out_specs=(tile(d),) + (HBM_SPEC,) * (nm + 1),
        out_shape=(jax.ShapeDtypeStruct((t, d), F32),)
        + tuple(jax.ShapeDtypeStruct((3,) + g.shape[1:], g.dtype) for g in chip_srcs)
        + (jax.ShapeDtypeStruct((N_DEV * INP_PACK_ROWS, HEAD), F32),),
        scratch_shapes=[pltpu.VMEM((INP_PACK_ROWS, HEAD), F32),
                        pltpu.SemaphoreType.DMA((nm, 3)), pltpu.SemaphoreType.DMA((nm, 3)),
                        pltpu.SemaphoreType.DMA((N_DEV - 1,)), pltpu.SemaphoreType.DMA((N_DEV - 1,)),
                        pltpu.SemaphoreType.DMA],
        compiler_params=pltpu.CompilerParams(dimension_semantics=("arbitrary",), vmem_limit_bytes=VMEM_LIMIT),
    )(dz, x, dx1, win, g0, *chip_srcs)


ROW_BLOCKS = 8


def _chip_sums(core_idx, grads, recvs, dtypes, name):
    nw = len(grads)

    def body(idx_ref, *refs):
        gs, rs, outs = refs[:nw], refs[nw:2 * nw], refs[2 * nw:]
        for g, r, o in zip(gs, rs, outs):
            o[...] = (g[...] + r[...]).astype(o.dtype)

    def blk(g):
        return (None,) + g.shape[1:]

    return pl.pallas_call(
        body, name=name,
        grid_spec=pltpu.PrefetchScalarGridSpec(
            num_scalar_prefetch=1, grid=(4,),
            in_specs=[pl.BlockSpec(blk(g), lambda q, s: (2 * q + s[0], 0, 0)) for g in grads]
            + [pl.BlockSpec(blk(g), lambda q, s: (q, 0, 0)) for g in grads],
            out_specs=tuple(pl.BlockSpec(blk(g), lambda q, s: (q, 0, 0)) for g in grads)),
        out_shape=tuple(jax.ShapeDtypeStruct((4,) + g.shape[1:], dt) for g, dt in zip(grads, dtypes)),
    )(core_idx, *grads, *recvs)


def _adamw(w, g, m, v):
    m = ADAM_B1 * m + (1.0 - ADAM_B1) * g
    v = ADAM_B2 * v + (1.0 - ADAM_B2) * (g * g)
    m_hat = m / (1.0 - ADAM_B1 ** ADAM_STEP)
    v_hat = v / (1.0 - ADAM_B2 ** ADAM_STEP)
    delta = -ADAM_LR * (m_hat / (jnp.sqrt(v_hat) + ADAM_EPS) + ADAM_WD * w)
    return delta, m, v


def _adam_sections(sec_idx, grads, recv_sib, recv_chips, wmv):
    nw = len(wmv)

    def body(idx_ref, *refs):
        gs, rs, cs = refs[:nw], refs[nw:2 * nw], refs[2 * nw:3 * nw]
        params = refs[3 * nw:6 * nw]
        outs = refs[6 * nw:]
        for w in range(nw):
            g = gs[w][...] + rs[w][...]
            for k in range(3):
                g = g + cs[w][k].astype(F32)
            wv, mv, vv = (params[3 * w + t][...] for t in range(3))
            delta, m_new, v_new = _adamw(wv, g, mv, vv)
            outs[4 * w][...] = g
            outs[4 * w + 1][...] = delta
            outs[4 * w + 2][...] = m_new
            outs[4 * w + 3][...] = v_new

    def rb(g):
        return g.shape[1] // ROW_BLOCKS

    in_specs = [pl.BlockSpec((None, rb(g), g.shape[2]), lambda i, s: (s[0], i, 0)) for g in grads]
    in_specs += [pl.BlockSpec((None, rb(g), g.shape[2]), lambda i, s: (s[1], i, 0)) for g in grads]
    in_specs += [pl.BlockSpec((3, rb(g), g.shape[2]), lambda i, s: (0, i, 0)) for g in grads]
    sec = lambda g: pl.BlockSpec((rb(g), g.shape[2]), lambda i, s: (i, 0))
    for w in range(nw):
        in_specs += [sec(grads[w])] * 3
    out_specs, out_shape = [], []
    for g in grads:
        out_specs += [sec(g)] * 4
        out_shape += [jax.ShapeDtypeStruct(g.shape[1:], F32)] * 4
    flat = [a for t in wmv for a in t]
    outs = pl.pallas_call(
        body, name="adam_sections",
        grid_spec=pltpu.PrefetchScalarGridSpec(num_scalar_prefetch=1, grid=(ROW_BLOCKS,), in_specs=in_specs,
                                               out_specs=tuple(out_specs)),
        out_shape=tuple(out_shape),
    )(sec_idx, *grads, *recv_sib, *recv_chips, *flat)
    return [outs[4 * w:4 * w + 4] for w in range(nw)]


def _adam_small(sec_idx, gpack, parts, parts_wmv, vec_params, mat_params, conv_wmv):
    items = [(0, parts_wmv, "parts")]
    items += [(r0, t, "vec") for r0, t in vec_params] + [(r0, t, "mat") for r0, t in mat_params]
    items.append((_SEQ_ROWS["dcw"][0], conv_wmv, "conv"))

    def body(idx_ref, g_ref, parts_ref, *refs):
        ins, outs = refs[:3 * len(items)], refs[3 * len(items):]
        for n, (r0, _, kind) in enumerate(items):
            w_ref, m_ref, v_ref = ins[3 * n:3 * n + 3]
            o = outs[4 * n:4 * n + 4]
            if kind == "mat":
                g = g_ref[pl.ds(r0, w_ref.shape[0]), :]
                res = (g,) + _adamw(w_ref[...], g, m_ref[...], v_ref[...])
                for ref, val in zip(o, res):
                    ref[...] = val
            elif kind in ("vec", "parts"):
                for h in range(w_ref.shape[1] // HEAD):
                    cols = slice(h * HEAD, (h + 1) * HEAD)
                    if kind == "vec":
                        g = g_ref[pl.ds(r0 + h, 1), :]
                    else:
                        g = parts_ref[pl.ds(h, 1), :]
                        for dev in range(1, N_DEV):
                            g = g + parts_ref[pl.ds(SUB * dev + h, 1), :]
                    res = (g,) + _adamw(w_ref[:, cols], g, m_ref[:, cols], v_ref[:, cols])
                    for ref, val in zip(o, res):
                        ref[:, cols] = val
            else:
                rows = lax.broadcasted_iota(jnp.int32, (SUB, HEAD), 0)
                for k in range(4):
                    blk = g_ref[pl.ds(r0 + SUB * k, SUB), :]
                    g = jnp.sum(jnp.where(rows == idx_ref[0], blk, 0.0), axis=0, keepdims=True)
                    row = pl.ds(k, 1)
                    res = (g,) + _adamw(w_ref[row, :], g, m_ref[row, :], v_ref[row, :])
                    for ref, val in zip(o, res):
                        ref[row, :] = val
        outs[4 * len(items)][...] = g_ref[pl.ds(_OFF_MID + MID_LOSS_ROW, 1), :]

    flat = [a for _, t, _ in items for a in t]
    full = lambda a: pl.BlockSpec(a.shape, lambda i, s: (0,) * a.ndim)
    out_specs, out_shape = [], []
    for _, t, _ in items:
        out_specs += [full(t[0])] * 4
        out_shape += [jax.ShapeDtypeStruct(t[0].shape, F32)] * 4
    loss_row = jax.ShapeDtypeStruct((1, HEAD), F32)
    out_specs.append(full(loss_row))
    out_shape.append(loss_row)
    outs = pl.pallas_call(
        body, name="adam_small",
        grid_spec=pltpu.PrefetchScalarGridSpec(num_scalar_prefetch=1, grid=(1,),
                                               in_specs=[full(gpack), full(parts)] + [full(a) for a in flat],
                                               out_specs=tuple(out_specs)),
        out_shape=tuple(out_shape),
    )(sec_idx, gpack, parts, *flat)
    return [outs[4 * n:4 * n + 4] for n in range(len(items))], outs[4 * len(items)]


def _col_slicer(width):
    return lambda ref, idx: ref.at[:, pl.ds(pl.multiple_of(idx * width, HEAD), width)]


def _row_slicer(rows):
    return lambda ref, idx: ref.at[pl.ds(pl.multiple_of(idx * rows, SUB), rows), :]


def kernel(x, p, norm_g, w_in, conv_w, conv_b, lru_w_a, lru_b_a, lru_w_x, lru_b_x, lru_lambda, pool_w, pool_scale, w_proj_lru, w_proj_pool, w_out, ple_norm_g, w_ple_gate, w_ple_proj, final_g, loss_target, m_norm_g, m_w_in, m_conv_w, m_conv_b, m_lru_w_a, m_lru_b_a, m_lru_w_x, m_lru_b_x, m_lru_lambda, m_pool_w, m_pool_scale, m_w_proj_lru, m_w_proj_pool, m_w_out, m_ple_norm_g, m_w_ple_gate, m_w_ple_proj, m_final_g, v_norm_g, v_w_in, v_conv_w, v_conv_b, v_lru_w_a, v_lru_b_a, v_lru_w_x, v_lru_b_x, v_lru_lambda, v_pool_w, v_pool_scale, v_w_proj_lru, v_w_proj_pool, v_w_out, v_ple_norm_g, v_w_ple_gate, v_w_ple_proj, v_final_g):
    nb, s, d = x.shape
    t = nb * s
    tm = min(TILE_M, s)
    sec_idx = (4 * lax.axis_index("x") + 2 * lax.axis_index("y") + lax.axis_index("c")).astype(jnp.int32)
    chip_idx = (2 * lax.axis_index("x") + lax.axis_index("y")).astype(jnp.int32)
    core_idx = lax.axis_index("c").astype(jnp.int32)

    h, ht, win, cw = _pre(
        x, norm_g, [w_in[0], conv_w[0]], [True, False],
        [(d, N_DEV * w_in.shape[2]), (conv_w.shape[1], N_DEV * conv_w.shape[2])],
        [_col_slicer(w_in.shape[2]), _col_slicer(conv_w.shape[2])], tm)
    late_shards = [w_proj_lru[0], w_proj_pool[0], w_out[0], w_ple_gate[0], w_ple_proj[0]]
    late_shapes = [(N_DEV * w_proj_lru.shape[1], d), (w_proj_pool.shape[1], N_DEV * w_proj_pool.shape[2]),
                   (N_DEV * w_out.shape[1], d), (N_DEV * w_ple_gate.shape[1], d),
                   (w_ple_proj.shape[1], N_DEV * w_ple_proj.shape[2])]
    late_slicers = [_row_slicer(w_proj_lru.shape[1]), _col_slicer(w_proj_pool.shape[2]), _row_slicer(w_out.shape[1]),
                    _row_slicer(w_ple_gate.shape[1]), _col_slicer(w_ple_proj.shape[2])]

    lw = cw.shape[1]
    pwid = pool_scale.shape[1]
    wa = lru_w_a[0].astype(_MM)
    wx = lru_w_x[0].astype(_MM)
    pw = pool_w[0].astype(_MM)
    ba, bx = lru_b_a[0], lru_b_x[0]
    gf = final_g.reshape(1, d)
    core = core_idx.reshape(1)

    fwd_out = _fwd_seq(
        h, win, cw, conv_b, wa, ba, wx, bx, lru_lambda, pw, pool_scale, late_shards, late_shapes, late_slicers, tm)
    z, hl, ya, yb = fwd_out[:4]
    saved = fwd_out[4:N_SEQ_SAVED]
    wpl, wpp, wout, wpg, wpe = fwd_out[N_SEQ_SAVED:]
    x2d = x.reshape(t, d)
    mid_out = _mid(
        ya.reshape(t, lw), yb.reshape(t, pwid), h.reshape(t, d), win, x2d, p[0].reshape(t, -1),
        loss_target.reshape(t, d), wpl, wpp, wout, wpg, wpe, ple_norm_g, gf, tm)
    dya, dyb, dm, dx1 = mid_out[:4]
    grads_mid, pack_mid, sib_mid = list(mid_out[4:9]), mid_out[9], list(mid_out[10:])
    sums_mid = _chip_sums(core, grads_mid, sib_mid, [_WIRE] * len(grads_mid), "chip_sums_mid")
    seq_out = _bwd_seq(z, hl, saved, dya.reshape(nb, s, lw), dyb.reshape(nb, s, pwid), dm.reshape(nb, s, -1), cw,
                       wa, wx, lru_lambda, pw, pool_scale, pack_mid, sums_mid, tm)
    dz, chips_mid = seq_out[0], list(seq_out[2:])
    small = seq_out[1].reshape(N_DEV, SMALL_SEC, HEAD)
    g_in, sib_in, _, _, gsmall = _bwd_win(ht, dz, small)
    (sums_in,) = _chip_sums(core, [g_in], [sib_in], [_WIRE], "chip_sums_in")
    grad_x, chips_in, g0_parts = _bwd_dx(dz.reshape(t, -1), x2d, dx1, win, norm_g, [sums_in], tm)

    grads = [g_in] + grads_mid
    recv_sib = [sib_in] + sib_mid
    recv_chips = [chips_in] + chips_mid
    wmv = [(w_in[0], m_w_in[0], v_w_in[0]), (w_proj_lru[0], m_w_proj_lru[0], v_w_proj_lru[0]),
           (w_proj_pool[0], m_w_proj_pool[0], v_w_proj_pool[0]), (w_out[0], m_w_out[0], v_w_out[0]),
           (w_ple_gate[0], m_w_ple_gate[0], v_w_ple_gate[0]), (w_ple_proj[0], m_w_ple_proj[0], v_w_ple_proj[0])]
    big = _adam_sections(jnp.stack([sec_idx, chip_idx]), grads, recv_sib, recv_chips, wmv)

    R = _SEQ_ROWS
    vec = lambda r0, *t: (r0, tuple(a.reshape(1, -1) for a in t))
    mat = lambda r0, *t: (r0, tuple(a.reshape(-1, HEAD) for a in t))
    norm_wmv = vec(0, norm_g, m_norm_g, v_norm_g)[1]
    vec_params = [vec(R["dcb"][0], conv_b, m_conv_b, v_conv_b),
                  vec(R["dlam"][0], lru_lambda, m_lru_lambda, v_lru_lambda),
                  vec(R["dps"][0], pool_scale, m_pool_scale, v_pool_scale),
                  vec(_OFF_MID, ple_norm_g, m_ple_norm_g, v_ple_norm_g),
                  vec(_OFF_MID + SUB, final_g, m_final_g, v_final_g)]
    mat_params = [mat(R["dwa"][0], lru_w_a, m_lru_w_a, v_lru_w_a), mat(R["dba"][0], lru_b_a, m_lru_b_a, v_lru_b_a),
                  mat(R["dwx"][0], lru_w_x, m_lru_w_x, v_lru_w_x), mat(R["dbx"][0], lru_b_x, m_lru_b_x, v_lru_b_x),
                  mat(R["dpw"][0], pool_w, m_pool_w, v_pool_w)]
    conv_wmv = (conv_w[0], m_conv_w[0], v_conv_w[0])
    small_out, loss_row = _adam_small(sec_idx.reshape(1), gsmall, g0_parts, norm_wmv, vec_params, mat_params, conv_wmv)
    loss = loss_row[0, 0]

    res = {}
    names_small = ["norm_g", "conv_b", "lru_lambda", "pool_scale", "ple_norm_g", "final_g",
                   "lru_w_a", "lru_b_a", "lru_w_x", "lru_b_x", "pool_w", "conv_w"]
    shapes = {"norm_g": norm_g, "conv_b": conv_b, "lru_lambda": lru_lambda, "pool_scale": pool_scale,
              "ple_norm_g": ple_norm_g, "final_g": final_g, "lru_w_a": lru_w_a, "lru_b_a": lru_b_a, "lru_w_x": lru_w_x,
              "lru_b_x": lru_b_x, "pool_w": pool_w, "conv_w": conv_w}
    for name, quad in zip(names_small, small_out):
        res[name] = [a.reshape(shapes[name].shape) for a in quad]
    names_big = ["w_in", "w_proj_lru", "w_proj_pool", "w_out", "w_ple_gate", "w_ple_proj"]
    for name, quad, (w, _, _) in zip(names_big, big, wmv):
        res[name] = [a.reshape((1,) + w.shape) for a in quad]
    order = ["norm_g", "w_in", "conv_w", "conv_b", "lru_w_a", "lru_b_a", "lru_w_x", "lru_b_x", "lru_lambda", "pool_w",
             "pool_scale", "w_proj_lru", "w_proj_pool", "w_out", "ple_norm_g", "w_ple_gate", "w_ple_proj", "final_g"]
    out = [loss, grad_x.reshape(nb, s, d)]
    for kind in range(4):
        out += [res[name][kind] for name in order]
    return tuple(out)
```

```python
import jax
import jax.numpy as jnp
from jax import lax
from jax.experimental import pallas as pl
from jax.experimental.pallas import tpu as pltpu

F32 = jnp.float32
_MM = jnp.bfloat16
_WIRE = jnp.bfloat16

EPS = 1e-6
LRU_C = 8.0
MULT_SQ_FLOOR = 1e-30
POOL_WINDOWS = (2, 4, 8, 16)
N_HEADS = 8
HEAD = 128
HALO = 16
SUB = 8

ADAM_LR = 0.001
ADAM_B1 = 0.9
ADAM_B2 = 0.999
ADAM_EPS = 1e-08
ADAM_WD = 0.01
ADAM_STEP = 10

N_DEV = 8
MESH = pl.DeviceIdType.MESH
VMEM_LIMIT = 60 * 1024 * 1024
TILE_M = 256
N_SEQ_SAVED = 10

_SEQ_ROWS = {"dwa": (0, 1024), "dwx": (1024, 1024), "dpw": (2048, 512), "dba": (2560, 8), "dbx": (2568, 8),
             "dcb": (2576, 8), "dlam": (2584, 8), "dcw": (2592, 32), "dps": (2624, 8)}
SEQ_PACK_ROWS = 2632
MID_PACK_ROWS = 24
MID_LOSS_ROW = 16
INP_PACK_ROWS = 8
SMALL_ROWS = 3072
SMALL_SEC = SMALL_ROWS // N_DEV
_OFF_MID = SEQ_PACK_ROWS


def _dot(a, b):
    return jnp.dot(a, b, preferred_element_type=F32)


def _dot_nt(a, b):
    return lax.dot_general(a, b, (((1,), (1,)), ((), ())), preferred_element_type=F32)


def _dot_tn(a, b):
    return lax.dot_general(a, b, (((0,), (0,)), ((), ())), preferred_element_type=F32)


def _sigmoid(v):
    return 0.5 * jnp.tanh(0.5 * v) + 0.5


def _lru_mult(log_a, a):
    m2 = jnp.maximum(-jnp.tanh(log_a) * (1.0 + a * a), MULT_SQ_FLOOR)
    inv = lax.rsqrt(m2)
    return m2 * inv, inv


def _rows_back(v, n):
    return pltpu.roll(v, n, 0) if n else v


def _rows_ahead(v, n):
    return pltpu.roll(v, v.shape[0] - n, 0) if n else v


def _softplus_neg(lam):
    e = jnp.exp(-jnp.abs(lam))
    w = 1.0 + e
    l1p = jnp.where(w == 1.0, e, jnp.log(w) * (e / (w - 1.0)))
    return jnp.maximum(-lam, 0.0) + l1p


def _rsqrt_mean_sq(v):
    return lax.rsqrt(jnp.mean(v * v, axis=-1, keepdims=True) + EPS)


def _colsum(v):
    return jnp.sum(v, axis=0, keepdims=True)


def _const_spec(shape):
    nd = len(shape)
    return pl.BlockSpec(shape, lambda *_: (0,) * nd, pipeline_mode=pl.Buffered(1))


HBM_SPEC = pl.BlockSpec(memory_space=pl.ANY)


def _my_pos():
    return lax.axis_index("x"), lax.axis_index("y"), lax.axis_index("c")


def _other_chips(x, y):
    return [(1 - x, y), (x, 1 - y), (1 - x, 1 - y)]


def _gather_sems(nw):
    return [pltpu.SemaphoreType.DMA((nw, 7)), pltpu.SemaphoreType.DMA((nw, 7)), pltpu.SemaphoreType.DMA((nw,))]


def _gather_phases(srcs, outs, slicers, send_sems, recv_sems, local_sems, relay_diagonal=False):
    n_direct = 2 if relay_diagonal else 3
    nw = len(srcs)
    x, y, c = _my_pos()
    me, sibling = (x, y, c), (x, y, 1 - c)
    chips = _other_chips(x, y)

    def part(w, pos):
        return slicers[w](outs[w], 4 * pos[0] + 2 * pos[1] + pos[2])

    def copy(w, k, block, to, src=None):
        return pltpu.make_async_remote_copy(
            src_ref=part(w, block) if src is None else src, dst_ref=part(w, block),
            send_sem=send_sems.at[w, k], recv_sem=recv_sems.at[w, k], device_id=to, device_id_type=MESH)

    def mine():
        return [pltpu.make_async_copy(srcs[w], part(w, me), local_sems.at[w]) for w in range(nw)]

    def first():
        out = []
        for w in range(nw):
            out.append(copy(w, 0, me, sibling, src=srcs[w]))
            out += [copy(w, 1 + j, me, (*chips[j], c), src=srcs[w]) for j in range(n_direct)]
        return out

    def relay(w, j):
        return copy(w, 3, (*chips[j], c), (*chips[1 - j], c))

    def relays():
        return [relay(w, 0) for w in range(nw)] if relay_diagonal else []

    def passed():
        return [copy(w, 4 + j, (*chip, c), sibling) for j, chip in enumerate(chips) for w in range(nw)]

    def start():
        for cp in mine() + first():
            cp.start()

    def forward():
        for j, chip in enumerate(chips):
            for w in range(nw):
                copy(w, 1 + j, (*chip, c), me).wait_recv()
                copy(w, 4 + j, (*chip, c), sibling).start()
            if relay_diagonal and j < 2:
                @pl.when(c == j)
                def _():
                    for w in range(nw):
                        relay(w, j).start()

    def finish():
        for w in range(nw):
            copy(w, 0, sibling, me).wait_recv()
            for j, chip in enumerate(chips):
                copy(w, 4 + j, (*chip, 1 - c), me).wait_recv()
        for cp in first() + relays() + passed():
            cp.wait_send()
        for cp in mine():
            cp.wait()

    return start, forward, finish


def _sibling_copies(srcs, outs, send_sems, recv_sems):
    x, y, c = _my_pos()
    return [pltpu.make_async_remote_copy(
        src_ref=srcs[w].at[2 * q + (1 - c)], dst_ref=outs[w].at[q], send_sem=send_sems.at[w, q],
        recv_sem=recv_sems.at[w, q], device_id=(x, y, 1 - c), device_id_type=MESH)
        for w in range(len(srcs)) for q in range(4)]


def _chip_copies(srcs, outs, send_sems, recv_sems, which=None):
    x, y, c = _my_pos()
    return [pltpu.make_async_remote_copy(
        src_ref=srcs[w].at[2 * px + py], dst_ref=outs[w].at[k], send_sem=send_sems.at[w, k],
        recv_sem=recv_sems.at[w, k], device_id=(px, py, c), device_id_type=MESH)
        for w in (range(len(srcs)) if which is None else which) for k, (px, py) in enumerate(_other_chips(x, y))]


def _lru_gates(xc, h, wa_ref, ba_ref, wx_ref, bx_ref, neg_c_sp):
    xc16 = xc.astype(_MM)
    r = _sigmoid(_dot(xc16, wa_ref[h]) + ba_ref[pl.ds(h, 1), :])
    i = _sigmoid(_dot(xc16, wx_ref[h]) + bx_ref[pl.ds(h, 1), :])
    return r, i, neg_c_sp * r


def _conv_head(xa_ext, cw_ref, cb_ref, cols, tm):
    ext = xa_ext[:, cols]
    xc = cb_ref[:, cols] + cw_ref[pl.ds(3, 1), cols] * ext[HALO:HALO + tm]
    for k in range(3):
        xc = xc + cw_ref[pl.ds(k, 1), cols] * _rows_back(ext, 3 - k)[HALO:HALO + tm]
    return xc


def _window_sum(ext, k, shift):
    n = 1
    while n < k:
        ext = ext + shift(ext, n)
        n *= 2
    return ext


def _pool_diff(xb_ext, g, k, pos, tm):
    cols = slice(g * HEAD, (g + 1) * HEAD)
    ext = xb_ext[:, cols]
    ws = _window_sum(ext, k, _rows_back)[HALO:HALO + tm]
    inv_cnt = 1.0 / jnp.minimum(pos + 1, k).astype(F32)
    return ws * inv_cnt - ext[HALO:HALO + tm], inv_cnt


def _staged_sources(shards, as_operand):
    in_specs = [_const_spec(a.shape) if c else HBM_SPEC for a, c in zip(shards, as_operand)]
    stages = [pltpu.VMEM(a.shape, _MM) for a, c in zip(shards, as_operand) if c]
    dtypes = [_MM if c else a.dtype for a, c in zip(shards, as_operand)]
    return in_specs, stages, dtypes


def _gather_sources(srcs, stages, as_operand):
    stages = iter(stages)
    return [next(stages) if c else src for src, c in zip(srcs, as_operand)]


def _fill_stages(srcs, staged, as_operand):
    for src, dst, c in zip(srcs, staged, as_operand):
        if c:
            dst[...] = src[...].astype(dst.dtype)


def _pre(x, g0, shards, as_operand, shapes, slicers, tm):
    nb, s, d = x.shape
    nt = s // tm
    nl = len(shards)
    shard_specs, stage_shapes, wire = _staged_sources(shards, as_operand)

    def body(x_ref, g0_ref, *rest):
        src, rest = rest[:nl], rest[nl:]
        h_ref, ht_ref = rest[:2]
        out, rest = rest[2:2 + nl], rest[2 + nl:]
        (send_sems, recv_sems, local_sems), stages = rest[:3], rest[3:]
        step_no = pl.program_id(0) * nt + pl.program_id(1)
        staged = _gather_sources(src, stages, as_operand)
        g_start, g_forward, g_finish = _gather_phases(staged, out, slicers, send_sems, recv_sems, local_sems,
                                                      relay_diagonal=True)

        @pl.when(step_no == 0)
        def _():
            _fill_stages(src, staged, as_operand)
            g_start()

        xv = x_ref[...]
        h16 = ((xv * _rsqrt_mean_sq(xv)) * g0_ref[...]).astype(_MM)
        h_ref[...] = h16
        ht_ref[...] = h16.T

        @pl.when(step_no == nb * nt - 1)
        def _():
            g_forward()
            g_finish()

    tile = pl.BlockSpec((None, tm, d), lambda b, j: (b, j, 0))
    return pl.pallas_call(
        body, name="pre", grid=(nb, nt),
        in_specs=[tile, _const_spec(g0.shape)] + shard_specs,
        out_specs=(tile, pl.BlockSpec((None, d, tm), lambda b, j: (b, 0, j))) + (HBM_SPEC,) * nl,
        out_shape=(jax.ShapeDtypeStruct((nb, s, d), _MM), jax.ShapeDtypeStruct((nb, d, s), _MM))
        + tuple(jax.ShapeDtypeStruct(shp, dt) for shp, dt in zip(shapes, wire)),
        scratch_shapes=_gather_sems(nl) + stage_shapes,
        compiler_params=pltpu.CompilerParams(dimension_semantics=("arbitrary", "arbitrary")),
    )(x, g0, *shards)


def _fwd_seq(h, win, cw, cb, wa, ba, wx, bx, lam, pw, ps, late_shards, late_shapes, late_slicers, tm):
    nb, s, d = h.shape
    lw = cw.shape[1]
    pwid = ps.shape[1]
    nc = 2 * lw + 2 * pwid
    nt = s // tm
    nl = len(late_shards)
    n_steps = nb * nt
    as_operand = [True] * nl
    shard_specs, stage_shapes, wire = _staged_sources(late_shards, as_operand)

    def body(h_ref, win_ref, cw_ref, cb_ref, wa_ref, ba_ref, wx_ref, bx_ref, lam_ref, pw_ref, ps_ref, *rest):
        late_src, rest = rest[:nl], rest[nl:]
        z_ref, hl_ref, ya_ref, yb_ref, xc_ref, r_ref, i_ref, a_buf, diff_ref, yp_ref = rest[:N_SEQ_SAVED]
        late_out, rest = rest[N_SEQ_SAVED:N_SEQ_SAVED + nl], rest[N_SEQ_SAVED + nl:]
        (xa_ext, xb_ext, carry, send_sems, recv_sems, local_sems), stages = rest[:6], rest[6:]
        j = pl.program_id(1)
        step_no = pl.program_id(0) * nt + j
        staged = _gather_sources(late_src, stages, as_operand)
        g_start, g_forward, g_finish = _gather_phases(staged, late_out, late_slicers, send_sems, recv_sems, local_sems)

        @pl.when(step_no == 0)
        def _():
            _fill_stages(late_src, staged, as_operand)
            g_start()

        pl.when(step_no == (3 * n_steps) // 4)(g_forward)

        @pl.when(j == 0)
        def _():
            xa_ext[pl.ds(0, HALO), :] = jnp.zeros((HALO, lw), F32)
            xb_ext[pl.ds(0, HALO), :] = jnp.zeros((HALO, pwid), F32)
            carry[...] = jnp.zeros_like(carry)

        h16 = h_ref[...]
        xa = _dot(h16, win_ref[:, 0:lw])
        z_ref[:, 0:lw] = xa
        xa_ext[pl.ds(HALO, tm), :] = xa
        for c0 in range(lw, nc, lw):
            z_ref[:, c0:c0 + lw] = _dot(h16, win_ref[:, c0:c0 + lw])

        for h in range(N_HEADS):
            cols = slice(h * HEAD, (h + 1) * HEAD)
            xc = _conv_head(xa_ext, cw_ref, cb_ref, cols, tm)
            neg_c_sp = -LRU_C * _softplus_neg(lam_ref[:, cols])
            r, i, log_a = _lru_gates(xc, h, wa_ref, ba_ref, wx_ref, bx_ref, neg_c_sp)
            a = jnp.exp(log_a)
            xc_ref[:, cols] = xc
            r_ref[:, cols] = r
            i_ref[:, cols] = i
            a_buf[:, cols] = a
            hl_ref[:, cols] = _lru_mult(log_a, a)[0] * (i * xc)

        rows = lax.broadcasted_iota(jnp.int32, (SUB, lw), 0)

        def step(c, car):
            i0 = pl.multiple_of(c * SUB, SUB)
            av = a_buf[pl.ds(i0, SUB), :]
            bv = hl_ref[pl.ds(i0, SUB), :]
            for sh in (1, 2, 4):
                m = rows >= sh
                a_sh = jnp.where(m, pltpu.roll(av, sh, 0), 1.0)
                b_sh = jnp.where(m, pltpu.roll(bv, sh, 0), 0.0)
                bv = av * b_sh + bv
                av = av * a_sh
            hv = av * car + bv
            hl_ref[pl.ds(i0, SUB), :] = hv
            return jnp.broadcast_to(hv[SUB - 1:SUB, :], (SUB, lw))

        carry[...] = lax.fori_loop(0, tm // SUB, step, carry[...])

        ga = z_ref[:, lw:2 * lw]
        ya_ref[...] = (hl_ref[...] * (ga * _sigmoid(ga))).astype(ya_ref.dtype)

        xb_ext[pl.ds(HALO, tm), :] = z_ref[:, 2 * lw:2 * lw + pwid]
        pos = j * tm + lax.broadcasted_iota(jnp.int32, (tm, HEAD), 0)
        for g, k in enumerate(POOL_WINDOWS):
            cols = slice(g * HEAD, (g + 1) * HEAD)
            diff16 = _pool_diff(xb_ext, g, k, pos, tm)[0].astype(_MM)
            yp = _dot(diff16, pw_ref[g])
            diff_ref[:, cols] = diff16
            yp_ref[:, cols] = yp
            gb = z_ref[:, 2 * lw + pwid + g * HEAD:2 * lw + pwid + (g + 1) * HEAD]
            yb_ref[:, cols] = ((yp * ps_ref[:, cols]) * (gb * _sigmoid(gb))).astype(yb_ref.dtype)

        xa_ext[pl.ds(0, HALO), :] = xa_ext[pl.ds(tm, HALO), :]
        xb_ext[pl.ds(0, HALO), :] = xb_ext[pl.ds(tm, HALO), :]
        pl.when(step_no == n_steps - 1)(g_finish)

    tile = lambda w: pl.BlockSpec((None, tm, w), lambda b, j: (b, j, 0))
    return pl.pallas_call(
        body, name="fwd_seq", grid=(nb, nt),
        in_specs=[tile(d), _const_spec((d, nc)), _const_spec(cw.shape), _const_spec(cb.shape),
                  _const_spec(wa.shape), _const_spec(ba.shape), _const_spec(wx.shape), _const_spec(bx.shape),
                  _const_spec(lam.shape), _const_spec(pw.shape), _const_spec(ps.shape)] + shard_specs,
        out_specs=(tile(nc), tile(lw), tile(lw), tile(pwid), tile(lw), tile(lw), tile(lw), tile(lw), tile(pwid),
                   tile(pwid)) + (HBM_SPEC,) * nl,
        out_shape=(jax.ShapeDtypeStruct((nb, s, nc), F32), jax.ShapeDtypeStruct((nb, s, lw), F32),
                   jax.ShapeDtypeStruct((nb, s, lw), _MM), jax.ShapeDtypeStruct((nb, s, pwid), _MM))
        + (jax.ShapeDtypeStruct((nb, s, lw), F32),) * 4
        + (jax.ShapeDtypeStruct((nb, s, pwid), _MM), jax.ShapeDtypeStruct((nb, s, pwid), F32))
        + tuple(jax.ShapeDtypeStruct(shp, dt) for shp, dt in zip(late_shapes, wire)),
        scratch_shapes=[pltpu.VMEM((tm + HALO, lw), F32), pltpu.VMEM((tm + HALO, pwid), F32),
                        pltpu.VMEM((SUB, lw), F32)] + _gather_sems(nl) + stage_shapes,
        compiler_params=pltpu.CompilerParams(dimension_semantics=("arbitrary", "arbitrary"), vmem_limit_bytes=VMEM_LIMIT),
    )(h, win, cw, cb, wa, ba, wx, bx, lam, pw, ps, *late_shards)


def _store_chunks(ref, row0, vec):
    for h in range(vec.shape[1] // HEAD):
        ref[pl.ds(row0 + h, 1), :] += vec[:, h * HEAD:(h + 1) * HEAD]


def _mid(ya, yb, h, win, x, p, tgt, wpl, wpp, wout, wpg, wpe, g1, gf, tm):
    t, d = x.shape
    lw = ya.shape[1]
    pwid = yb.shape[1]
    pdim = p.shape[1]
    n = t // tm
    sec_pp = wpp.shape[1] // N_DEV
    sec_pe = wpe.shape[1] // N_DEV
    sec_rows = wpl.shape[0] // N_DEV
    grad_secs = [(sec_rows, d), (wpp.shape[0], sec_pp), (sec_rows, d), (sec_rows, d), (wpe.shape[0], sec_pe)]
    n_win_blocks = win.shape[1] // d
    win_cols = lambda c: pl.BlockSpec((d, d), lambda i: (0, c), pipeline_mode=pl.Buffered(1))

    def body(ya_ref, yb_ref, h_ref, wma_ref, wmb_ref, x_ref, p_ref, tgt_ref, wpl_ref, wpp_ref, wout_ref, wpg_ref, wpe_ref,
             g1_ref, gf_ref,
             dya_ref, dyb_ref, dm_ref, dx1_ref, gpl_ref, gpp_ref, gout_ref, gpg_ref, gpe_ref, pack_ref,
             sib_pl, sib_pp, sib_out, sib_pg, sib_pe,
             acc_pl, acc_pp, acc_out, acc_pg, acc_pe, out_sems, sib_send, sib_recv):
        i = pl.program_id(0)

        @pl.when(i == 0)
        def _():
            for acc in (acc_pl, acc_pp, acc_out, acc_pg, acc_pe):
                acc[...] = jnp.zeros_like(acc)
            pack_ref[...] = jnp.zeros_like(pack_ref)

        ya16 = ya_ref[...]
        yb16 = yb_ref[...]
        br_a = _dot(ya16, wpl_ref[...])
        br_b = _dot(yb16, wpp_ref[...])
        h16 = h_ref[...]
        sa = _sigmoid(_dot(h16, wma_ref[...]))
        sb = _sigmoid(_dot(h16, wmb_ref[...]))
        mg16 = (sa * br_a + sb * br_b).astype(_MM)
        x1 = x_ref[...] + _dot(mg16, wout_ref[...])
        r1 = _rsqrt_mean_sq(x1)
        n1 = x1 * r1
        h116 = (n1 * g1_ref[...]).astype(_MM)
        gate = _sigmoid(_dot(h116, wpg_ref[...]))
        p16 = p_ref[...].astype(_MM)
        pe = _dot(p16, wpe_ref[...])
        x2 = x1 + gate * pe
        r2 = _rsqrt_mean_sq(x2)
        n2 = x2 * r2
        err = n2 * gf_ref[...] - tgt_ref[...]
        sq = jnp.sum(_colsum(err * err), axis=1, keepdims=True)
        pack_ref[pl.ds(MID_LOSS_ROW, 1), :] += jnp.broadcast_to(sq * (0.5 / d), (1, HEAD))

        dy = err * (1.0 / d)
        _store_chunks(pack_ref, SUB, _colsum(dy * n2))
        dn2 = dy * gf_ref[...]
        dx2 = r2 * (dn2 - n2 * jnp.mean(dn2 * n2, axis=-1, keepdims=True))
        dpe16 = (dx2 * gate).astype(_MM)
        dpg16 = ((dx2 * pe) * (gate * (1.0 - gate))).astype(_MM)
        acc_pe[...] += _dot_tn(p16, dpe16)
        acc_pg[...] += _dot_tn(h116, dpg16)
        dh1 = _dot_nt(dpg16, wpg_ref[...])
        _store_chunks(pack_ref, 0, _colsum(dh1 * n1))
        dn1 = dh1 * g1_ref[...]
        dx1 = dx2 + r1 * (dn1 - n1 * jnp.mean(dn1 * n1, axis=-1, keepdims=True))
        dx1_ref[...] = dx1
        dx116 = dx1.astype(_MM)
        acc_out[...] += _dot_tn(mg16, dx116)
        dmg = _dot_nt(dx116, wout_ref[...])
        da16 = (dmg * sa).astype(_MM)
        db16 = (dmg * sb).astype(_MM)
        dm_ref[:, 0:d] = ((dmg * br_a) * (sa * (1.0 - sa))).astype(dm_ref.dtype)
        dm_ref[:, d:2 * d] = ((dmg * br_b) * (sb * (1.0 - sb))).astype(dm_ref.dtype)
        acc_pl[...] += _dot_tn(ya16, da16)
        acc_pp[...] += _dot_tn(yb16, db16)
        dya_ref[...] = _dot_nt(da16, wpl_ref[...])
        dyb_ref[...] = _dot_nt(db16, wpp_ref[...])

        @pl.when(i == n - 1)
        def _():
            px, py, c = _my_pos()
            accs = (acc_pl, acc_pp, acc_out, acc_pg, acc_pe)
            outs = (gpl_ref, gpp_ref, gout_ref, gpg_ref, gpe_ref)
            sibs = (sib_pl, sib_pp, sib_out, sib_pg, sib_pe)

            def section(w, k):
                if w == 1:
                    return acc_pp.at[:, pl.ds(k * sec_pp, sec_pp)]
                if w == 4:
                    return acc_pe.at[:, pl.ds(k * sec_pe, sec_pe)]
                return accs[w].at[pl.ds(k * sec_rows, sec_rows), :]

            for cc in range(2):
                @pl.when(c == cc)
                def _(cc=cc):
                    copies = []
                    for w in range(5):
                        for q in range(4):
                            copies.append(pltpu.make_async_remote_copy(
                                src_ref=section(w, 2 * q + 1 - cc), dst_ref=sibs[w].at[q], send_sem=sib_send.at[w, q],
                                recv_sem=sib_recv.at[w, q], device_id=(px, py, 1 - cc), device_id_type=MESH))
                            copies.append(pltpu.make_async_copy(
                                section(w, 2 * q + cc), outs[w].at[2 * q + cc], out_sems.at[4 * w + q]))
                    for cp in copies:
                        cp.start()
                    for cp in copies:
                        cp.wait()

    tile = lambda w: pl.BlockSpec((tm, w), lambda i: (i, 0))
    any_spec = pl.BlockSpec(memory_space=pl.ANY)
    full = lambda shape: pl.BlockSpec(shape, lambda i: (0,) * len(shape))
    return pl.pallas_call(
        body, name="mid", grid=(n,),
        in_specs=[tile(lw), tile(pwid), tile(d), win_cols(n_win_blocks - 2), win_cols(n_win_blocks - 1),
                  tile(d), tile(pdim), tile(d),
                  _const_spec(wpl.shape), _const_spec(wpp.shape), _const_spec(wout.shape), _const_spec(wpg.shape),
                  _const_spec(wpe.shape), _const_spec(g1.shape), _const_spec(gf.shape)],
        out_specs=(tile(lw), tile(pwid), tile(2 * d), tile(d), any_spec, any_spec, any_spec, any_spec, any_spec,
                   full((MID_PACK_ROWS, HEAD))) + (any_spec,) * 5,
        out_shape=(jax.ShapeDtypeStruct((t, lw), F32), jax.ShapeDtypeStruct((t, pwid), F32),
                   jax.ShapeDtypeStruct((t, 2 * d), _MM), jax.ShapeDtypeStruct((t, d), F32),
                   *(jax.ShapeDtypeStruct((N_DEV,) + sec, F32) for sec in grad_secs),
                   jax.ShapeDtypeStruct((MID_PACK_ROWS, HEAD), F32),
                   *(jax.ShapeDtypeStruct((N_DEV // 2,) + sec, F32) for sec in grad_secs)),
        scratch_shapes=[pltpu.VMEM(wpl.shape, F32), pltpu.VMEM(wpp.shape, F32), pltpu.VMEM(wout.shape, F32),
                        pltpu.VMEM(wpg.shape, F32), pltpu.VMEM(wpe.shape, F32),
                        pltpu.SemaphoreType.DMA((5 * N_DEV // 2,)), pltpu.SemaphoreType.DMA((5, N_DEV // 2)),
                        pltpu.SemaphoreType.DMA((5, N_DEV // 2))],
        compiler_params=pltpu.CompilerParams(dimension_semantics=("arbitrary",), vmem_limit_bytes=VMEM_LIMIT),
    )(ya, yb, h, win, win, x, p, tgt, wpl, wpp, wout, wpg, wpe, g1, gf)


def _bwd_seq(z, hl, saved, dya, dyb, dm, cw, wa, wx, lam, pw, ps, pack_mid, sib_grads, tm):
    nb, s, zw = z.shape
    lw = cw.shape[1]
    pwid = ps.shape[1]
    nt = s // tm
    nc = zw + dm.shape[2]
    R = _SEQ_ROWS
    ns = len(sib_grads)

    def body(z_hbm, hl_ref, hlh_ref, xc_buf, r_buf, i_buf, a_ref, diff_ref, yp_ref, dya_ref, dyb_ref, dm_ref,
             cw_ref, wa_ref, wx_ref, lam_ref, pw_ref, ps_ref, pack_mid_ref, *rest):
        sib_src, rest = rest[:ns], rest[ns:]
        dz_ref, pack_ref = rest[:2]
        sib_out, rest = rest[2:2 + ns], rest[2 + ns:]
        a_ext, an_buf, hl_ext, dh_buf, dxc_ext, q_ext, carry, send_sems, recv_sems, z_ring, z_sems = rest
        b = pl.program_id(0)
        j = pl.program_id(1)
        jr = nt - 1 - j
        has_prev = jr > 0

        step = b * nt + j
        n_steps = nb * nt

        def z_copy(st):
            row0 = pl.multiple_of((nt - 1 - st % nt) * tm, tm)
            slot = st % 3
            return pltpu.make_async_copy(z_hbm.at[st // nt, pl.ds(row0, tm), :], z_ring.at[slot], z_sems.at[slot])

        @pl.when(step == 0)
        def _():
            z_copy(step).start()
            if n_steps > 1:
                z_copy(step + 1).start()

        @pl.when(step + 2 < n_steps)
        def _():
            z_copy(step + 2).start()

        @pl.when((b == 0) & (j == 0))
        def _():
            for cp in _chip_copies(sib_src, sib_out, send_sems, recv_sems):
                cp.start()

        dz_ref[:, zw:nc] = dm_ref[...]

        @pl.when((b == 0) & (j == 0))
        def _():
            pack_ref[...] = jnp.zeros_like(pack_ref)

        @pl.when(j == 0)
        def _():
            a_ext[pl.ds(tm, SUB), :] = jnp.zeros((SUB, lw), F32)
            dxc_ext[pl.ds(tm, HALO), :] = jnp.zeros((HALO, lw), F32)
            q_ext[pl.ds(tm, HALO), :] = jnp.zeros((HALO, pwid), F32)
            carry[...] = jnp.zeros_like(carry)

        hl_ext[pl.ds(0, SUB), :] = jnp.where(has_prev, hlh_ref[...], 0.0)
        hl_ext[pl.ds(SUB, tm), :] = hl_ref[...]
        a_ext[pl.ds(0, tm), :] = a_ref[...]
        z_copy(step).wait()
        zq_ref = z_ring.at[step % 3]
        for h in range(N_HEADS):
            cols = slice(h * HEAD, (h + 1) * HEAD)
            ga = zq_ref[:, lw + h * HEAD:lw + (h + 1) * HEAD]
            sg = _sigmoid(ga)
            dyav = dya_ref[:, cols]
            dh_buf[:, cols] = dyav * (ga * sg)
            dga = (dyav * hl_ext[pl.ds(SUB, tm), cols]) * (sg * (1.0 + ga * (1.0 - sg)))
            dz_ref[:, lw + h * HEAD:lw + (h + 1) * HEAD] = dga.astype(dz_ref.dtype)
        an_buf[...] = _rows_ahead(a_ext[...], 1)[0:tm]

        rows = lax.broadcasted_iota(jnp.int32, (SUB, lw), 0)
        nch = tm // SUB

        def step(c, car):
            i0 = pl.multiple_of((nch - 1 - c) * SUB, SUB)
            cv = an_buf[pl.ds(i0, SUB), :]
            bv = dh_buf[pl.ds(i0, SUB), :]
            for sh in (1, 2, 4):
                m = rows < SUB - sh
                c_sh = jnp.where(m, pltpu.roll(cv, SUB - sh, 0), 1.0)
                b_sh = jnp.where(m, pltpu.roll(bv, SUB - sh, 0), 0.0)
                bv = cv * b_sh + bv
                cv = cv * c_sh
            hv = cv * car + bv
            dh_buf[pl.ds(i0, SUB), :] = hv
            return jnp.broadcast_to(hv[0:1, :], (SUB, lw))

        carry[...] = lax.fori_loop(0, nch, step, carry[...])

        for h in range(N_HEADS):
            cols = slice(h * HEAD, (h + 1) * HEAD)
            lam_h = lam_ref[:, cols]
            neg_c_sp = -LRU_C * _softplus_neg(lam_h)
            xc = xc_buf[:, cols]
            r = r_buf[:, cols]
            i = i_buf[:, cols]
            a = a_ref[:, cols]
            mult, inv_mult = _lru_mult(neg_c_sp * r, a)
            dh = dh_buf[:, cols]
            dhx = dh * xc
            di = dhx * mult
            h_prev = _rows_back(hl_ext[:, cols], 1)[SUB:SUB + tm]
            dlog_a = (dh * h_prev) * a - (dhx * i) * ((a * a) * inv_mult)
            pack_ref[pl.ds(R["dlam"][0] + h, 1), :] += _colsum(dlog_a * r) * (LRU_C * _sigmoid(-lam_h))
            dpa = (dlog_a * neg_c_sp) * (r * (1.0 - r))
            dpx = di * (i * (1.0 - i))
            dpa16 = dpa.astype(_MM)
            dpx16 = dpx.astype(_MM)
            xc16 = xc.astype(_MM)
            dxc = dh * (mult * i) + _dot_nt(dpa16, wa_ref[h]) + _dot_nt(dpx16, wx_ref[h])
            pack_ref[pl.ds(R["dwa"][0] + h * HEAD, HEAD), :] += _dot_tn(xc16, dpa16)
            pack_ref[pl.ds(R["dwx"][0] + h * HEAD, HEAD), :] += _dot_tn(xc16, dpx16)
            pack_ref[pl.ds(R["dba"][0] + h, 1), :] += _colsum(dpa)
            pack_ref[pl.ds(R["dbx"][0] + h, 1), :] += _colsum(dpx)
            pack_ref[pl.ds(R["dcb"][0] + h, 1), :] += _colsum(dxc)
            dxc_ext[pl.ds(0, tm), cols] = dxc

        for h in range(N_HEADS):
            cols = slice(h * HEAD, (h + 1) * HEAD)
            dxc_all = dxc_ext[:, cols]
            xa = zq_ref[:, cols]
            dxa = None
            for k in range(4):
                dxc_k = _rows_ahead(dxc_all, 3 - k)[0:tm]
                pack_ref[pl.ds(R["dcw"][0] + SUB * k + h, 1), :] += _colsum(dxc_k * xa)
                term = cw_ref[pl.ds(k, 1), cols] * dxc_k
                dxa = term if dxa is None else dxa + term
            dz_ref[:, cols] = dxa.astype(dz_ref.dtype)

        pos = jr * tm + lax.broadcasted_iota(jnp.int32, (tm, HEAD), 0)
        for g, k in enumerate(POOL_WINDOWS):
            cols = slice(g * HEAD, (g + 1) * HEAD)
            inv_cnt = 1.0 / jnp.minimum(pos + 1, k).astype(F32)
            diff16 = diff_ref[:, cols]
            yp = yp_ref[:, cols]
            sc = ps_ref[:, cols]
            gb = zq_ref[:, 2 * lw + pwid + g * HEAD:2 * lw + pwid + (g + 1) * HEAD]
            sgb = _sigmoid(gb)
            dybv = dyb_ref[:, cols]
            dy_pool = dybv * (gb * sgb)
            dgb = (dybv * (yp * sc)) * (sgb * (1.0 + gb * (1.0 - sgb)))
            pack_ref[pl.ds(R["dps"][0] + g, 1), :] += _colsum(dy_pool * yp)
            dyp16 = (dy_pool * sc).astype(_MM)
            pack_ref[pl.ds(R["dpw"][0] + g * HEAD, HEAD), :] += _dot_tn(diff16, dyp16)
            ddiff = _dot_nt(dyp16, pw_ref[g])
            q_ext[pl.ds(0, tm), cols] = ddiff * inv_cnt
            dxb = _window_sum(q_ext[:, cols], k, _rows_ahead)[0:tm] - ddiff
            dz_ref[:, 2 * lw + g * HEAD:2 * lw + (g + 1) * HEAD] = dxb.astype(dz_ref.dtype)
            dz_ref[:, 2 * lw + pwid + g * HEAD:2 * lw + pwid + (g + 1) * HEAD] = dgb.astype(dz_ref.dtype)

        a_ext[pl.ds(tm, SUB), :] = a_ext[pl.ds(0, SUB), :]
        dxc_ext[pl.ds(tm, HALO), :] = dxc_ext[pl.ds(0, HALO), :]
        q_ext[pl.ds(tm, HALO), :] = q_ext[pl.ds(0, HALO), :]

        @pl.when((b == nb - 1) & (j == nt - 1))
        def _():
            pack_ref[pl.ds(_OFF_MID, MID_PACK_ROWS), :] = pack_mid_ref[...]
            for cp in _chip_copies(sib_src, sib_out, send_sems, recv_sems):
                cp.wait()

    rev = lambda w: pl.BlockSpec((None, tm, w), lambda b, j: (b, nt - 1 - j, 0))
    prev_rows = lambda rows, w: pl.BlockSpec(
        (None, rows, w), lambda b, j: (b, jnp.maximum((nt - 1 - j) * (tm // rows) - 1, 0), 0))
    return pl.pallas_call(
        body, name="bwd_seq", grid=(nb, nt),
        in_specs=[HBM_SPEC, rev(lw), prev_rows(SUB, lw), rev(lw), rev(lw), rev(lw), rev(lw), rev(pwid), rev(pwid),
                  rev(lw), rev(pwid), rev(nc - zw),
                  _const_spec(cw.shape), _const_spec(wa.shape), _const_spec(wx.shape), _const_spec(lam.shape),
                  _const_spec(pw.shape), _const_spec(ps.shape), _const_spec(pack_mid.shape)] + [HBM_SPEC] * ns,
        out_specs=(rev(nc), pl.BlockSpec((SMALL_ROWS, HEAD), lambda b, j: (0, 0))) + (HBM_SPEC,) * ns,
        out_shape=(jax.ShapeDtypeStruct((nb, s, nc), _MM), jax.ShapeDtypeStruct((SMALL_ROWS, HEAD), F32))
        + tuple(jax.ShapeDtypeStruct((3,) + g.shape[1:], g.dtype) for g in sib_grads),
        scratch_shapes=[pltpu.VMEM((tm + SUB, lw), F32), pltpu.VMEM((tm, lw), F32), pltpu.VMEM((tm + SUB, lw), F32),
                        pltpu.VMEM((tm, lw), F32), pltpu.VMEM((tm + HALO, lw), F32),
                        pltpu.VMEM((tm + HALO, pwid), F32), pltpu.VMEM((SUB, lw), F32),
                        pltpu.SemaphoreType.DMA((ns, 3)), pltpu.SemaphoreType.DMA((ns, 3)),
                        pltpu.VMEM((3, tm, zw), F32), pltpu.SemaphoreType.DMA((3,))],
        compiler_params=pltpu.CompilerParams(dimension_semantics=("arbitrary", "arbitrary"), vmem_limit_bytes=VMEM_LIMIT),
    )(z, hl, hl, *saved, dya, dyb, dm, cw, wa, wx, lam, pw, ps, pack_mid, *sib_grads)


def _bwd_win(ht, dz, small):
    nb, d, s = ht.shape
    nc = dz.shape[2]
    sec = nc // N_DEV
    n_chips = N_DEV // 2

    def body(ht_ref, dz_ref, small_src, gwin_ref, recv_ref, small_out, small_chips_out, gsmall_ref,
             acc, own4, sib4, csum, got3, total, local_sems, sib_send, sib_recv, small_send, small_recv,
             sc_send, sc_recv, ld_sems, g_send, g_recv, g_local):
        q = pl.program_id(0)
        b = pl.program_id(1)
        slot = q % 2
        x, y, c = _my_pos()

        def half(sl, core):
            return acc.at[sl, :, pl.ds(pl.multiple_of(core * sec, HEAD), sec)]

        def local_copy(qq, sl):
            return pltpu.make_async_copy(half(sl, c), gwin_ref.at[2 * qq + c], local_sems.at[sl])

        def sib_copy(qq, sl):
            return pltpu.make_async_remote_copy(
                src_ref=half(sl, 1 - c), dst_ref=recv_ref.at[qq], send_sem=sib_send.at[qq],
                recv_sem=sib_recv.at[qq], device_id=(x, y, 1 - c), device_id_type=MESH)

        step_no = q * nb + b
        n_steps = n_chips * nb
        sum_step = min(3, n_steps - 1)
        fwd_step = max(sum_step, min(6, n_steps - 1))
        small_sib = lambda: _sibling_copies([small_src], [small_out], small_send, small_recv)
        small_chip = lambda: _chip_copies([csum], [small_chips_out], sc_send, sc_recv)
        g_start, g_forward, g_finish = _gather_phases([total], [gsmall_ref], [_row_slicer(small.shape[1])],
                                                      g_send, g_recv, g_local)

        def load(srcs, dst):
            cps = [pltpu.make_async_copy(src, dst.at[k], ld_sems.at[k]) for k, src in enumerate(srcs)]
            for cp in cps:
                cp.start()
            for cp in cps:
                cp.wait()

        @pl.when(step_no == 0)
        def _():
            for cp in small_sib():
                cp.start()

        @pl.when(step_no == 1)
        def _():
            for cp in small_sib():
                cp.wait_recv()
            load([small_src.at[2 * qq + c] for qq in range(n_chips)], own4)
            load([small_out.at[qq] for qq in range(n_chips)], sib4)
            csum[...] = own4[...] + sib4[...]
            for cp in small_chip():
                cp.start()

        @pl.when(step_no == sum_step)
        def _():
            for cp in small_chip():
                cp.wait_recv()
            load([small_chips_out.at[k] for k in range(3)], got3)
            total[...] = ((csum[2 * x + y] + got3[0]) + got3[1]) + got3[2]
            g_start()

        pl.when(step_no == fwd_step)(g_forward)

        @pl.when((q >= 2) & (b == 0))
        def _():
            local_copy(q - 2, slot).wait()
            sib_copy(q - 2, slot).wait_send()

        part = _dot(ht_ref[b], dz_ref[...])

        @pl.when(b == 0)
        def _():
            acc[slot] = part

        @pl.when(b != 0)
        def _():
            acc[slot] += part

        @pl.when(b == nb - 1)
        def _():
            local_copy(q, slot).start()
            sib_copy(q, slot).start()

        @pl.when((q == n_chips - 1) & (b == nb - 1))
        def _():
            for qq in (n_chips - 2, n_chips - 1):
                local_copy(qq, qq % 2).wait()
                sib_copy(qq, qq % 2).wait_send()
            for qq in range(n_chips):
                sib_copy(qq, 0).wait_recv()
            g_finish()
            for cp in small_sib() + small_chip():
                cp.wait_send()

    return pl.pallas_call(
        body, name="bwd_win", grid=(n_chips, nb),
        in_specs=[_const_spec(ht.shape), pl.BlockSpec((None, s, 2 * sec), lambda q, b: (b, 0, q))]
        + [HBM_SPEC],
        out_specs=(HBM_SPEC,) * 5,
        out_shape=(jax.ShapeDtypeStruct((N_DEV, d, sec), F32), jax.ShapeDtypeStruct((n_chips, d, sec), F32))
        + (jax.ShapeDtypeStruct((n_chips,) + small.shape[1:], small.dtype),
           jax.ShapeDtypeStruct((3,) + small.shape[1:], small.dtype),
           jax.ShapeDtypeStruct((N_DEV * small.shape[1], small.shape[2]), small.dtype)),
        scratch_shapes=[pltpu.VMEM((2, d, 2 * sec), F32)]
        + [pltpu.VMEM((n_chips,) + small.shape[1:], F32)] * 3
        + [pltpu.VMEM((3,) + small.shape[1:], F32), pltpu.VMEM(small.shape[1:], F32), pltpu.SemaphoreType.DMA((2,)),
           pltpu.SemaphoreType.DMA((n_chips,)), pltpu.SemaphoreType.DMA((n_chips,)),
           pltpu.SemaphoreType.DMA((1, 4)), pltpu.SemaphoreType.DMA((1, 4)),
           pltpu.SemaphoreType.DMA((1, 3)), pltpu.SemaphoreType.DMA((1, 3)), pltpu.SemaphoreType.DMA((n_chips,))]
        + _gather_sems(1),
        compiler_params=pltpu.CompilerParams(dimension_semantics=("arbitrary", "arbitrary"), vmem_limit_bytes=VMEM_LIMIT),
    )(ht, dz, small)


def _bwd_dx(dz, x, dx1, win, g0, chip_srcs, tm):
    t, d = x.shape
    nc = win.shape[1]
    n = t // tm
    nm = len(chip_srcs)

    def body(dz_ref, x_ref, dx1_ref, win_ref, g0_ref, *rest):
        chip_src = rest[:nm]
        gx_ref = rest[nm]
        chip_out = rest[nm + 1:2 * nm + 1]
        parts_ref = rest[2 * nm + 1]
        pack, chip_send, chip_recv, p_send, p_recv, p_local = rest[2 * nm + 2:]
        i = pl.program_id(0)
        x_, y_, c_ = _my_pos()
        chip_copies = lambda: _chip_copies(chip_src, chip_out, chip_send, chip_recv)

        @pl.when(i == 0)
        def _():
            pack[...] = jnp.zeros_like(pack)
            for cp in chip_copies():
                cp.start()

        xv = x_ref[...]
        r0 = _rsqrt_mean_sq(xv)
        xh = xv * r0
        dh = _dot_nt(dz_ref[...], win_ref[...])
        _store_chunks(pack, 0, _colsum(dh * xh))
        dxh = dh * g0_ref[...]
        gx_ref[...] = dx1_ref[...] + r0 * (dxh - xh * jnp.mean(dxh * xh, axis=-1, keepdims=True))

        @pl.when(i == n - 1)
        def _():
            for cp in chip_copies():
                cp.wait()
            row0 = pl.multiple_of((4 * x_ + 2 * y_ + c_) * INP_PACK_ROWS, SUB)
            mine = parts_ref.at[pl.ds(row0, INP_PACK_ROWS), :]
            local = pltpu.make_async_copy(pack, mine, p_local)
            local.start()
            sends = []
            for k in range(1, N_DEV):
                peer = tuple(1 - v if (k >> bit) & 1 else v for v, bit in ((x_, 2), (y_, 1), (c_, 0)))
                sends.append(pltpu.make_async_remote_copy(
                    src_ref=pack, dst_ref=mine, send_sem=p_send.at[k - 1], recv_sem=p_recv.at[k - 1],
                    device_id=peer, device_id_type=MESH))
            for cp in sends:
                cp.start()
            for cp in sends:
                cp.wait()
            local.wait()

    tile = lambda w: pl.BlockSpec((tm, w), lambda i: (i, 0))
    return pl.pallas_call(
        body, name="bwd_dx", grid=(n,),
        in_specs=[tile(nc), tile(d), tile(d), _const_spec(win.shape), _const_spec(g0.shape)] + [HBM_SPEC] * nm,
        out_specs=(tile(d),) + (HBM_SPEC,) * (nm + 1),
        out_shape=(jax.ShapeDtypeStruct((t, d), F32),)
        + tuple(jax.ShapeDtypeStruct((3,) + g.shape[1:], g.dtype) for g in chip_srcs)
        + (jax.ShapeDtypeStruct((N_DEV * INP_PACK_ROWS, HEAD), F32),),
        scratch_shapes=[pltpu.VMEM((INP_PACK_ROWS, HEAD), F32),
                        pltpu.SemaphoreType.DMA((nm, 3)), pltpu.SemaphoreType.DMA((nm, 3)),
                        pltpu.SemaphoreType.DMA((N_DEV - 1,)), pltpu.SemaphoreType.DMA((N_DEV - 1,)),
                        pltpu.SemaphoreType.DMA],
        compiler_params=pltpu.CompilerParams(dimension_semantics=("arbitrary",), vmem_limit_bytes=VMEM_LIMIT),
    )(dz, x, dx1, win, g0, *chip_srcs)


ROW_BLOCKS = 8


def _chip_sums(core_idx, grads, recvs, dtypes, name):
    nw = len(grads)

    def body(idx_ref, *refs):
        gs, rs, outs = refs[:nw], refs[nw:2 * nw], refs[2 * nw:]
        for g, r, o in zip(gs, rs, outs):
            o[...] = (g[...] + r[...]).astype(o.dtype)

    def blk(g):
        return (None,) + g.shape[1:]

    return pl.pallas_call(
        body, name=name,
        grid_spec=pltpu.PrefetchScalarGridSpec(
            num_scalar_prefetch=1, grid=(4,),
            in_specs=[pl.BlockSpec(blk(g), lambda q, s: (2 * q + s[0], 0, 0)) for g in grads]
            + [pl.BlockSpec(blk(g), lambda q, s: (q, 0, 0)) for g in grads],
            out_specs=tuple(pl.BlockSpec(blk(g), lambda q, s: (q, 0, 0)) for g in grads)),
        out_shape=tuple(jax.ShapeDtypeStruct((4,) + g.shape[1:], dt) for g, dt in zip(grads, dtypes)),
    )(core_idx, *grads, *recvs)


def _adamw(w, g, m, v):
    m = ADAM_B1 * m + (1.0 - ADAM_B1) * g
    v = ADAM_B2 * v + (1.0 - ADAM_B2) * (g * g)
    m_hat = m / (1.0 - ADAM_B1 ** ADAM_STEP)
    v_hat = v / (1.0 - ADAM_B2 ** ADAM_STEP)
    delta = -ADAM_LR * (m_hat / (jnp.sqrt(v_hat) + ADAM_EPS) + ADAM_WD * w)
    return delta, m, v


def _adam_sections(sec_idx, grads, recv_sib, recv_chips, wmv):
    nw = len(wmv)

    def body(idx_ref, *refs):
        gs, rs, cs = refs[:nw], refs[nw:2 * nw], refs[2 * nw:3 * nw]
        params = refs[3 * nw:6 * nw]
        outs = refs[6 * nw:]
        for w in range(nw):
            g = gs[w][...] + rs[w][...]
            for k in range(3):
                g = g + cs[w][k].astype(F32)
            wv, mv, vv = (params[3 * w + t][...] for t in range(3))
            delta, m_new, v_new = _adamw(wv, g, mv, vv)
            outs[4 * w][...] = g
            outs[4 * w + 1][...] = delta
            outs[4 * w + 2][...] = m_new
            outs[4 * w + 3][...] = v_new

    def rb(g):
        return g.shape[1] // ROW_BLOCKS

    in_specs = [pl.BlockSpec((None, rb(g), g.shape[2]), lambda i, s: (s[0], i, 0)) for g in grads]
    in_specs += [pl.BlockSpec((None, rb(g), g.shape[2]), lambda i, s: (s[1], i, 0)) for g in grads]
    in_specs += [pl.BlockSpec((3, rb(g), g.shape[2]), lambda i, s: (0, i, 0)) for g in grads]
    sec = lambda g: pl.BlockSpec((rb(g), g.shape[2]), lambda i, s: (i, 0))
    for w in range(nw):
        in_specs += [sec(grads[w])] * 3
    out_specs, out_shape = [], []
    for g in grads:
        out_specs += [sec(g)] * 4
        out_shape += [jax.ShapeDtypeStruct(g.shape[1:], F32)] * 4
    flat = [a for t in wmv for a in t]
    outs = pl.pallas_call(
        body, name="adam_sections",
        grid_spec=pltpu.PrefetchScalarGridSpec(num_scalar_prefetch=1, grid=(ROW_BLOCKS,), in_specs=in_specs,
                                               out_specs=tuple(out_specs)),
        out_shape=tuple(out_shape),
    )(sec_idx, *grads, *recv_sib, *recv_chips, *flat)
    return [outs[4 * w:4 * w + 4] for w in range(nw)]


def _adam_small(sec_idx, gpack, parts, parts_wmv, vec_params, mat_params, conv_wmv):
    items = [(0, parts_wmv, "parts")]
    items += [(r0, t, "vec") for r0, t in vec_params] + [(r0, t, "mat") for r0, t in mat_params]
    items.append((_SEQ_ROWS["dcw"][0], conv_wmv, "conv"))

    def body(idx_ref, g_ref, parts_ref, *refs):
        ins, outs = refs[:3 * len(items)], refs[3 * len(items):]
        for n, (r0, _, kind) in enumerate(items):
            w_ref, m_ref, v_ref = ins[3 * n:3 * n + 3]
            o = outs[4 * n:4 * n + 4]
            if kind == "mat":
                g = g_ref[pl.ds(r0, w_ref.shape[0]), :]
                res = (g,) + _adamw(w_ref[...], g, m_ref[...], v_ref[...])
                for ref, val in zip(o, res):
                    ref[...] = val
            elif kind in ("vec", "parts"):
                for h in range(w_ref.shape[1] // HEAD):
                    cols = slice(h * HEAD, (h + 1) * HEAD)
                    if kind == "vec":
                        g = g_ref[pl.ds(r0 + h, 1), :]
                    else:
                        g = parts_ref[pl.ds(h, 1), :]
                        for dev in range(1, N_DEV):
                            g = g + parts_ref[pl.ds(SUB * dev + h, 1), :]
                    res = (g,) + _adamw(w_ref[:, cols], g, m_ref[:, cols], v_ref[:, cols])
                    for ref, val in zip(o, res):
                        ref[:, cols] = val
            else:
                rows = lax.broadcasted_iota(jnp.int32, (SUB, HEAD), 0)
                for k in range(4):
                    blk = g_ref[pl.ds(r0 + SUB * k, SUB), :]
                    g = jnp.sum(jnp.where(rows == idx_ref[0], blk, 0.0), axis=0, keepdims=True)
                    row = pl.ds(k, 1)
                    res = (g,) + _adamw(w_ref[row, :], g, m_ref[row, :], v_ref[row, :])
                    for ref, val in zip(o, res):
                        ref[row, :] = val
        outs[4 * len(items)][...] = g_ref[pl.ds(_OFF_MID + MID_LOSS_ROW, 1), :]

    flat = [a for _, t, _ in items for a in t]
    full = lambda a: pl.BlockSpec(a.shape, lambda i, s: (0,) * a.ndim)
    out_specs, out_shape = [], []
    for _, t, _ in items:
        out_specs += [full(t[0])] * 4
        out_shape += [jax.ShapeDtypeStruct(t[0].shape, F32)] * 4
    loss_row = jax.ShapeDtypeStruct((1, HEAD), F32)
    out_specs.append(full(loss_row))
    out_shape.append(loss_row)
    outs = pl.pallas_call(
        body, name="adam_small",
        grid_spec=pltpu.PrefetchScalarGridSpec(num_scalar_prefetch=1, grid=(1,),
                                               in_specs=[full(gpack), full(parts)] + [full(a) for a in flat],
                                               out_specs=tuple(out_specs)),
        out_shape=tuple(out_shape),
    )(sec_idx, gpack, parts, *flat)
    return [outs[4 * n:4 * n + 4] for n in range(len(items))], outs[4 * len(items)]


def _col_slicer(width):
    return lambda ref, idx: ref.at[:, pl.ds(pl.multiple_of(idx * width, HEAD), width)]


def _row_slicer(rows):
    return lambda ref, idx: ref.at[pl.ds(pl.multiple_of(idx * rows, SUB), rows), :]


def kernel(x, p, norm_g, w_in, conv_w, conv_b, lru_w_a, lru_b_a, lru_w_x, lru_b_x, lru_lambda, pool_w, pool_scale, w_proj_lru, w_proj_pool, w_out, ple_norm_g, w_ple_gate, w_ple_proj, final_g, loss_target, m_norm_g, m_w_in, m_conv_w, m_conv_b, m_lru_w_a, m_lru_b_a, m_lru_w_x, m_lru_b_x, m_lru_lambda, m_pool_w, m_pool_scale, m_w_proj_lru, m_w_proj_pool, m_w_out, m_ple_norm_g, m_w_ple_gate, m_w_ple_proj, m_final_g, v_norm_g, v_w_in, v_conv_w, v_conv_b, v_lru_w_a, v_lru_b_a, v_lru_w_x, v_lru_b_x, v_lru_lambda, v_pool_w, v_pool_scale, v_w_proj_lru, v_w_proj_pool, v_w_out, v_ple_norm_g, v_w_ple_gate, v_w_ple_proj, v_final_g):
    nb, s, d = x.shape
    t = nb * s
    tm = min(TILE_M, s)
    sec_idx = (4 * lax.axis_index("x") + 2 * lax.axis_index("y") + lax.axis_index("c")).astype(jnp.int32)
    chip_idx = (2 * lax.axis_index("x") + lax.axis_index("y")).astype(jnp.int32)
    core_idx = lax.axis_index("c").astype(jnp.int32)

    h, ht, win, cw = _pre(
        x, norm_g, [w_in[0], conv_w[0]], [True, False],
        [(d, N_DEV * w_in.shape[2]), (conv_w.shape[1], N_DEV * conv_w.shape[2])],
        [_col_slicer(w_in.shape[2]), _col_slicer(conv_w.shape[2])], tm)
    late_shards = [w_proj_lru[0], w_proj_pool[0], w_out[0], w_ple_gate[0], w_ple_proj[0]]
    late_shapes = [(N_DEV * w_proj_lru.shape[1], d), (w_proj_pool.shape[1], N_DEV * w_proj_pool.shape[2]),
                   (N_DEV * w_out.shape[1], d), (N_DEV * w_ple_gate.shape[1], d),
                   (w_ple_proj.shape[1], N_DEV * w_ple_proj.shape[2])]
    late_slicers = [_row_slicer(w_proj_lru.shape[1]), _col_slicer(w_proj_pool.shape[2]), _row_slicer(w_out.shape[1]),
                    _row_slicer(w_ple_gate.shape[1]), _col_slicer(w_ple_proj.shape[2])]

    lw = cw.shape[1]
    pwid = pool_scale.shape[1]
    wa = lru_w_a[0].astype(_MM)
    wx = lru_w_x[0].astype(_MM)
    pw = pool_w[0].astype(_MM)
    ba, bx = lru_b_a[0], lru_b_x[0]
    gf = final_g.reshape(1, d)
    core = core_idx.reshape(1)

    fwd_out = _fwd_seq(
        h, win, cw, conv_b, wa, ba, wx, bx, lru_lambda, pw, pool_scale, late_shards, late_shapes, late_slicers, tm)
    z, hl, ya, yb = fwd_out[:4]
    saved = fwd_out[4:N_SEQ_SAVED]
    wpl, wpp, wout, wpg, wpe = fwd_out[N_SEQ_SAVED:]
    x2d = x.reshape(t, d)
    mid_out = _mid(
        ya.reshape(t, lw), yb.reshape(t, pwid), h.reshape(t, d), win, x2d, p[0].reshape(t, -1),
        loss_target.reshape(t, d), wpl, wpp, wout, wpg, wpe, ple_norm_g, gf, tm)
    dya, dyb, dm, dx1 = mid_out[:4]
    grads_mid, pack_mid, sib_mid = list(mid_out[4:9]), mid_out[9], list(mid_out[10:])
    sums_mid = _chip_sums(core, grads_mid, sib_mid, [_WIRE] * len(grads_mid), "chip_sums_mid")
    seq_out = _bwd_seq(z, hl, saved, dya.reshape(nb, s, lw), dyb.reshape(nb, s, pwid), dm.reshape(nb, s, -1), cw,
                       wa, wx, lru_lambda, pw, pool_scale, pack_mid, sums_mid, tm)
    dz, chips_mid = seq_out[0], list(seq_out[2:])
    small = seq_out[1].reshape(N_DEV, SMALL_SEC, HEAD)
    g_in, sib_in, _, _, gsmall = _bwd_win(ht, dz, small)
    (sums_in,) = _chip_sums(core, [g_in], [sib_in], [_WIRE], "chip_sums_in")
    grad_x, chips_in, g0_parts = _bwd_dx(dz.reshape(t, -1), x2d, dx1, win, norm_g, [sums_in], tm)

    grads = [g_in] + grads_mid
    recv_sib = [sib_in] + sib_mid
    recv_chips = [chips_in] + chips_mid
    wmv = [(w_in[0], m_w_in[0], v_w_in[0]), (w_proj_lru[0], m_w_proj_lru[0], v_w_proj_lru[0]),
           (w_proj_pool[0], m_w_proj_pool[0], v_w_proj_pool[0]), (w_out[0], m_w_out[0], v_w_out[0]),
           (w_ple_gate[0], m_w_ple_gate[0], v_w_ple_gate[0]), (w_ple_proj[0], m_w_ple_proj[0], v_w_ple_proj[0])]
    big = _adam_sections(jnp.stack([sec_idx, chip_idx]), grads, recv_sib, recv_chips, wmv)

    R = _SEQ_ROWS
    vec = lambda r0, *t: (r0, tuple(a.reshape(1, -1) for a in t))
    mat = lambda r0, *t: (r0, tuple(a.reshape(-1, HEAD) for a in t))
    norm_wmv = vec(0, norm_g, m_norm_g, v_norm_g)[1]
    vec_params = [vec(R["dcb"][0], conv_b, m_conv_b, v_conv_b),
                  vec(R["dlam"][0], lru_lambda, m_lru_lambda, v_lru_lambda),
                  vec(R["dps"][0], pool_scale, m_pool_scale, v_pool_scale),
                  vec(_OFF_MID, ple_norm_g, m_ple_norm_g, v_ple_norm_g),
                  vec(_OFF_MID + SUB, final_g, m_final_g, v_final_g)]
    mat_params = [mat(R["dwa"][0], lru_w_a, m_lru_w_a, v_lru_w_a), mat(R["dba"][0], lru_b_a, m_lru_b_a, v_lru_b_a),
                  mat(R["dwx"][0], lru_w_x, m_lru_w_x, v_lru_w_x), mat(R["dbx"][0], lru_b_x, m_lru_b_x, v_lru_b_x),
                  mat(R["dpw"][0], pool_w, m_pool_w, v_pool_w)]
    conv_wmv = (conv_w[0], m_conv_w[0], v_conv_w[0])
    small_out, loss_row = _adam_small(sec_idx.reshape(1), gsmall, g0_parts, norm_wmv, vec_params, mat_params, conv_wmv)
    loss = loss_row[0, 0]

    res = {}
    names_small = ["norm_g", "conv_b", "lru_lambda", "pool_scale", "ple_norm_g", "final_g",
                   "lru_w_a", "lru_b_a", "lru_w_x", "lru_b_x", "pool_w", "conv_w"]
    shapes = {"norm_g": norm_g, "conv_b": conv_b, "lru_lambda": lru_lambda, "pool_scale": pool_scale,
              "ple_norm_g": ple_norm_g, "final_g": final_g, "lru_w_a": lru_w_a, "lru_b_a": lru_b_a, "lru_w_x": lru_w_x,
              "lru_b_x": lru_b_x, "pool_w": pool_w, "conv_w": conv_w}
    for name, quad in zip(names_small, small_out):
        res[name] = [a.reshape(shapes[name].shape) for a in quad]
    names_big = ["w_in", "w_proj_lru", "w_proj_pool", "w_out", "w_ple_gate", "w_ple_proj"]
    for name, quad, (w, _, _) in zip(names_big, big, wmv):
        res[name] = [a.reshape((1,) + w.shape) for a in quad]
    order = ["norm_g", "w_in", "conv_w", "conv_b", "lru_w_a", "lru_b_a", "lru_w_x", "lru_b_x", "lru_lambda", "pool_w",
             "pool_scale", "w_proj_lru", "w_proj_pool", "w_out", "ple_norm_g", "w_ple_gate", "w_ple_proj", "final_g"]
    out = [loss, grad_x.reshape(nb, s, d)]
    for kind in range(4):
        out += [res[name][kind] for name in order]
    return tuple(out)
```
